```python
import jax, jax.numpy as jnp
from jax import lax
import numpy as np

D_MODEL = 2048
BATCH = 8
SEQ = 4096
DEPTH = 1

D_SSM = D_MODEL
SSM_HEADDIM = 64
SSM_HEADS = D_SSM // SSM_HEADDIM
SSM_GROUPS = 8
SSM_STATE = 128
SSM_CONV = 4
CHUNK = 128
DT_MIN = 1e-3
DT_MAX = 1e-1
D_CONV = D_MODEL
SHORT_CONV = 3
D_MIX = D_SSM + D_CONV
D_FF = -(-(8 * D_MODEL) // (3 * 256)) * 256
EPS = 1e-5

D_XBC = D_SSM + 2 * SSM_GROUPS * SSM_STATE
OFF_Z = 0
OFF_XBC = OFF_Z + D_SSM
OFF_DT = OFF_XBC + D_XBC
OFF_CB = OFF_DT + SSM_HEADS
OFF_CC = OFF_CB + D_CONV
OFF_CX = OFF_CC + D_CONV
D_IN = OFF_CX + D_CONV

kernel_name = "hymba_ssd_shortconv_block"


def _rmsnorm(x, g):
    xf = x.astype(jnp.float32)
    y = xf * lax.rsqrt(jnp.mean(xf * xf, axis=-1, keepdims=True) + EPS)
    return (y * g.astype(jnp.float32)).astype(x.dtype)


def _causal_dwconv(u, w):
    K = w.shape[0]
    S = u.shape[1]
    up = jnp.pad(u, ((0, 0), (K - 1, 0), (0, 0)))
    y = up[:, K - 1:K - 1 + S] * w[K - 1]
    for k in range(K - 1):
        y = y + up[:, k:k + S] * w[k]
    return y


def _ssd_chunked(xh, dt, A, Bm, Cm):
    b, S, H, P = xh.shape
    G, N = Bm.shape[2], Bm.shape[3]
    R = H // G
    nc = S // CHUNK
    f32 = jnp.float32
    X = (xh.astype(f32) * dt[..., None]).reshape(b, nc, CHUNK, G, R, P)
    dA = jnp.moveaxis((dt * A).reshape(b, nc, CHUNK, G, R), 2, -1)
    Bc = Bm.astype(f32).reshape(b, nc, CHUNK, G, N)
    Cc = Cm.astype(f32).reshape(b, nc, CHUNK, G, N)
    dA_cs = jnp.cumsum(dA, axis=-1)
    causal = jnp.tril(jnp.ones((CHUNK, CHUNK), dtype=bool))
    seg = dA_cs[..., :, None] - dA_cs[..., None, :]
    L = jnp.exp(jnp.where(causal, seg, -jnp.inf))
    CB = jnp.einsum('bclgn,bcsgn->bcgls', Cc, Bc)
    M = CB[:, :, :, None] * L
    y_diag = jnp.einsum('bcgrls,bcsgrp->bclgrp', M, X)
    decay_states = jnp.exp(dA_cs[..., -1:] - dA_cs)
    states = jnp.einsum('bclgn,bcgrl,bclgrp->bcgrpn', Bc, decay_states, X)
    chunk_decay = jnp.exp(dA_cs[..., -1])

    def step(h, inp):
        dec, st = inp
        return h * dec[..., None, None] + st, h

    h0 = jnp.zeros((b, G, R, P, N), f32)
    _, prev = lax.scan(step, h0, (jnp.moveaxis(chunk_decay, 1, 0), jnp.moveaxis(states, 1, 0)))
    prev = jnp.moveaxis(prev, 0, 1)
    y_off = jnp.einsum('bclgn,bcgrpn,bcgrl->bclgrp', Cc, prev, jnp.exp(dA_cs))
    return (y_diag + y_off).reshape(b, S, H, P)


def _ssd_group(z, xbc, dt_raw, conv_w, conv_b, dt_bias, A_log, Dskip, norm_g):
    b, S, _ = z.shape
    xbc = jax.nn.silu(_causal_dwconv(xbc, conv_w) + conv_b)
    xs = xbc[..., :D_SSM].reshape(b, S, SSM_HEADS, SSM_HEADDIM)
    Bm = xbc[..., D_SSM:D_SSM + SSM_GROUPS * SSM_STATE].reshape(b, S, SSM_GROUPS, SSM_STATE)
    Cm = xbc[..., D_SSM + SSM_GROUPS * SSM_STATE:].reshape(b, S, SSM_GROUPS, SSM_STATE)
    dt = jax.nn.softplus(dt_raw.astype(jnp.float32) + dt_bias.astype(jnp.float32))
    A = -jnp.exp(A_log.astype(jnp.float32))
    y = _ssd_chunked(xs, dt, A, Bm, Cm)
    y = y + Dskip.astype(jnp.float32)[:, None] * xs.astype(jnp.float32)
    y = y.reshape(b, S, D_SSM).astype(z.dtype)
    return _rmsnorm(y * jax.nn.silu(z), norm_g)


def _shortconv_group(gb, gc, u, conv_w):
    return gb * _causal_dwconv(gc * u, conv_w)


def _fwd_setup_inputs(seed: int = 0) -> dict:
    key = jax.random.key(seed)
    ks = jax.random.split(key, 16)
    f32 = jnp.float32
    nrm = lambda k, shape, s: jax.random.normal(k, shape, f32) * s
    x = jax.random.normal(ks[0], (BATCH, SEQ, D_MODEL), f32)
    norm_mix_g = 1.0 + nrm(ks[1], (DEPTH, D_MODEL), 0.02)
    w_in = nrm(ks[2], (DEPTH, D_MODEL, D_IN), D_MODEL ** -0.5)
    ssm_conv_w = nrm(ks[3], (DEPTH, SSM_CONV, D_XBC), SSM_CONV ** -0.5)
    ssm_conv_b = nrm(ks[4], (DEPTH, D_XBC), 0.02)
    dt0 = jnp.exp(jax.random.uniform(ks[5], (DEPTH, SSM_HEADS), f32)
                  * (np.log(DT_MAX) - np.log(DT_MIN)) + np.log(DT_MIN))
    ssm_dt_bias = dt0 + jnp.log(-jnp.expm1(-dt0))
    ssm_A_log = jnp.log(jax.random.uniform(ks[6], (DEPTH, SSM_HEADS), f32, 1.0, 16.0))
    ssm_D = 1.0 + nrm(ks[7], (DEPTH, SSM_HEADS), 0.1)
    ssm_norm_g = 1.0 + nrm(ks[8], (DEPTH, D_SSM), 0.02)
    sc_conv_w = nrm(ks[9], (DEPTH, SHORT_CONV, D_CONV), SHORT_CONV ** -0.5)
    w_out = nrm(ks[10], (DEPTH, D_MIX, D_MODEL), D_MIX ** -0.5)
    norm_ffn_g = 1.0 + nrm(ks[11], (DEPTH, D_MODEL), 0.02)
    w_gate = nrm(ks[12], (DEPTH, D_MODEL, D_FF), D_MODEL ** -0.5)
    w_up = nrm(ks[13], (DEPTH, D_MODEL, D_FF), D_MODEL ** -0.5)
    w_down = nrm(ks[14], (DEPTH, D_FF, D_MODEL), D_FF ** -0.5)
    norm_final_g = 1.0 + nrm(ks[15], (D_MODEL,), 0.02)
    return {"x": x, "norm_mix_g": norm_mix_g, "w_in": w_in, "ssm_conv_w": ssm_conv_w,
            "ssm_conv_b": ssm_conv_b, "ssm_dt_bias": ssm_dt_bias, "ssm_A_log": ssm_A_log,
            "ssm_D": ssm_D, "ssm_norm_g": ssm_norm_g, "sc_conv_w": sc_conv_w, "w_out": w_out,
            "norm_ffn_g": norm_ffn_g, "w_gate": w_gate, "w_up": w_up, "w_down": w_down,
            "norm_final_g": norm_final_g}


def _fwd_reference(x, norm_mix_g, w_in, ssm_conv_w, ssm_conv_b, ssm_dt_bias, ssm_A_log, ssm_D,
              ssm_norm_g, sc_conv_w, w_out, norm_ffn_g, w_gate, w_up, w_down, norm_final_g):
    h = x
    for l in range(DEPTH):
        n = _rmsnorm(h, norm_mix_g[l])
        proj = jnp.einsum('bsd,de->bse', n, w_in[l])
        y_ssm = _ssd_group(proj[..., OFF_Z:OFF_XBC], proj[..., OFF_XBC:OFF_DT],
                           proj[..., OFF_DT:OFF_CB], ssm_conv_w[l], ssm_conv_b[l],
                           ssm_dt_bias[l], ssm_A_log[l], ssm_D[l], ssm_norm_g[l])
        y_sc = _shortconv_group(proj[..., OFF_CB:OFF_CC], proj[..., OFF_CC:OFF_CX],
                                proj[..., OFF_CX:D_IN], sc_conv_w[l])
        y_mix = jnp.concatenate([y_ssm, y_sc], axis=-1)
        h = h + jnp.einsum('bse,ed->bsd', y_mix, w_out[l])
        n2 = _rmsnorm(h, norm_ffn_g[l])
        g = jnp.einsum('bsd,df->bsf', n2, w_gate[l])
        u = jnp.einsum('bsd,df->bsf', n2, w_up[l])
        h = h + jnp.einsum('bsf,fd->bsd', jax.nn.silu(g) * u, w_down[l])
    return _rmsnorm(h, norm_final_g)


import jax as _jax
import jax.numpy as _jnp

TWIN_FORMAT = 'train_step'
FWD_PARAMS = ['x', 'norm_mix_g', 'w_in', 'ssm_conv_w', 'ssm_conv_b', 'ssm_dt_bias', 'ssm_A_log', 'ssm_D', 'ssm_norm_g', 'sc_conv_w', 'w_out', 'norm_ffn_g', 'w_gate', 'w_up', 'w_down', 'norm_final_g']
TWIN_WEIGHTS = ['norm_mix_g', 'w_in', 'ssm_conv_w', 'ssm_conv_b', 'ssm_dt_bias', 'ssm_A_log', 'ssm_D', 'ssm_norm_g', 'sc_conv_w', 'w_out', 'norm_ffn_g', 'w_gate', 'w_up', 'w_down', 'norm_final_g']
TWIN_DIFF_INPUT = 'x'
TWIN_INPUTS = ['x', 'norm_mix_g', 'w_in', 'ssm_conv_w', 'ssm_conv_b', 'ssm_dt_bias', 'ssm_A_log', 'ssm_D', 'ssm_norm_g', 'sc_conv_w', 'w_out', 'norm_ffn_g', 'w_gate', 'w_up', 'w_down', 'norm_final_g', 'loss_target', 'm_norm_mix_g', 'm_w_in', 'm_ssm_conv_w', 'm_ssm_conv_b', 'm_ssm_dt_bias', 'm_ssm_A_log', 'm_ssm_D', 'm_ssm_norm_g', 'm_sc_conv_w', 'm_w_out', 'm_norm_ffn_g', 'm_w_gate', 'm_w_up', 'm_w_down', 'm_norm_final_g', 'v_norm_mix_g', 'v_w_in', 'v_ssm_conv_w', 'v_ssm_conv_b', 'v_ssm_dt_bias', 'v_ssm_A_log', 'v_ssm_D', 'v_ssm_norm_g', 'v_sc_conv_w', 'v_w_out', 'v_norm_ffn_g', 'v_w_gate', 'v_w_up', 'v_w_down', 'v_norm_final_g']
TWIN_OUTPUTS = ['loss', 'grad_x', 'grad_norm_mix_g', 'grad_w_in', 'grad_ssm_conv_w', 'grad_ssm_conv_b', 'grad_ssm_dt_bias', 'grad_ssm_A_log', 'grad_ssm_D', 'grad_ssm_norm_g', 'grad_sc_conv_w', 'grad_w_out', 'grad_norm_ffn_g', 'grad_w_gate', 'grad_w_up', 'grad_w_down', 'grad_norm_final_g', 'delta_norm_mix_g', 'delta_w_in', 'delta_ssm_conv_w', 'delta_ssm_conv_b', 'delta_ssm_dt_bias', 'delta_ssm_A_log', 'delta_ssm_D', 'delta_ssm_norm_g', 'delta_sc_conv_w', 'delta_w_out', 'delta_norm_ffn_g', 'delta_w_gate', 'delta_w_up', 'delta_w_down', 'delta_norm_final_g', 'new_m_norm_mix_g', 'new_m_w_in', 'new_m_ssm_conv_w', 'new_m_ssm_conv_b', 'new_m_ssm_dt_bias', 'new_m_ssm_A_log', 'new_m_ssm_D', 'new_m_ssm_norm_g', 'new_m_sc_conv_w', 'new_m_w_out', 'new_m_norm_ffn_g', 'new_m_w_gate', 'new_m_w_up', 'new_m_w_down', 'new_m_norm_final_g', 'new_v_norm_mix_g', 'new_v_w_in', 'new_v_ssm_conv_w', 'new_v_ssm_conv_b', 'new_v_ssm_dt_bias', 'new_v_ssm_A_log', 'new_v_ssm_D', 'new_v_ssm_norm_g', 'new_v_sc_conv_w', 'new_v_w_out', 'new_v_norm_ffn_g', 'new_v_w_gate', 'new_v_w_up', 'new_v_w_down', 'new_v_norm_final_g']
TWIN_LEAF_KINDS = {'loss': 'loss', 'grad_x': 'grad_x', 'grad_norm_mix_g': 'grad_w', 'grad_w_in': 'grad_w', 'grad_ssm_conv_w': 'grad_w', 'grad_ssm_conv_b': 'grad_w', 'grad_ssm_dt_bias': 'grad_w', 'grad_ssm_A_log': 'grad_w', 'grad_ssm_D': 'grad_w', 'grad_ssm_norm_g': 'grad_w', 'grad_sc_conv_w': 'grad_w', 'grad_w_out': 'grad_w', 'grad_norm_ffn_g': 'grad_w', 'grad_w_gate': 'grad_w', 'grad_w_up': 'grad_w', 'grad_w_down': 'grad_w', 'grad_norm_final_g': 'grad_w', 'delta_norm_mix_g': 'delta_w', 'delta_w_in': 'delta_w', 'delta_ssm_conv_w': 'delta_w', 'delta_ssm_conv_b': 'delta_w', 'delta_ssm_dt_bias': 'delta_w', 'delta_ssm_A_log': 'delta_w', 'delta_ssm_D': 'delta_w', 'delta_ssm_norm_g': 'delta_w', 'delta_sc_conv_w': 'delta_w', 'delta_w_out': 'delta_w', 'delta_norm_ffn_g': 'delta_w', 'delta_w_gate': 'delta_w', 'delta_w_up': 'delta_w', 'delta_w_down': 'delta_w', 'delta_norm_final_g': 'delta_w', 'new_m_norm_mix_g': 'new_m', 'new_m_w_in': 'new_m', 'new_m_ssm_conv_w': 'new_m', 'new_m_ssm_conv_b': 'new_m', 'new_m_ssm_dt_bias': 'new_m', 'new_m_ssm_A_log': 'new_m', 'new_m_ssm_D': 'new_m', 'new_m_ssm_norm_g': 'new_m', 'new_m_sc_conv_w': 'new_m', 'new_m_w_out': 'new_m', 'new_m_norm_ffn_g': 'new_m', 'new_m_w_gate': 'new_m', 'new_m_w_up': 'new_m', 'new_m_w_down': 'new_m', 'new_m_norm_final_g': 'new_m', 'new_v_norm_mix_g': 'new_v', 'new_v_w_in': 'new_v', 'new_v_ssm_conv_w': 'new_v', 'new_v_ssm_conv_b': 'new_v', 'new_v_ssm_dt_bias': 'new_v', 'new_v_ssm_A_log': 'new_v', 'new_v_ssm_D': 'new_v', 'new_v_ssm_norm_g': 'new_v', 'new_v_sc_conv_w': 'new_v', 'new_v_w_out': 'new_v', 'new_v_norm_ffn_g': 'new_v', 'new_v_w_gate': 'new_v', 'new_v_w_up': 'new_v', 'new_v_w_down': 'new_v', 'new_v_norm_final_g': 'new_v'}


def _forward(args):
    return _fwd_reference(*[args[k] for k in FWD_PARAMS])


def _output_shape():
    def fwd():
        inp = _fwd_setup_inputs(0)
        return _fwd_reference(*[inp[k] for k in FWD_PARAMS])
    out = _jax.eval_shape(fwd)
    return out.shape, out.dtype

N_MICROBATCH = 1
ADAM_LR = 0.001
ADAM_B1 = 0.9
ADAM_B2 = 0.999
ADAM_EPS = 1e-08
ADAM_WD = 0.01
ADAM_STEP = 10
PER_EXAMPLE_BATCH_AXIS = {'x': 0, 'loss_target': 0}
SHARED_INPUTS = []
_WEIGHT_DTYPES = {'norm_mix_g': _jnp.float32, 'w_in': _jnp.float32, 'ssm_conv_w': _jnp.float32, 'ssm_conv_b': _jnp.float32, 'ssm_dt_bias': _jnp.float32, 'ssm_A_log': _jnp.float32, 'ssm_D': _jnp.float32, 'ssm_norm_g': _jnp.float32, 'sc_conv_w': _jnp.float32, 'w_out': _jnp.float32, 'norm_ffn_g': _jnp.float32, 'w_gate': _jnp.float32, 'w_up': _jnp.float32, 'w_down': _jnp.float32, 'norm_final_g': _jnp.float32}
MOMENT_SCALE = {'norm_mix_g': 1.125215e-01, 'w_in': 4.504404e-02, 'ssm_conv_w': 3.671547e-02, 'ssm_conv_b': 5.275092e-02, 'ssm_dt_bias': 1.340387e-01, 'ssm_A_log': 1.542131e-01, 'ssm_D': 2.872762e-01, 'ssm_norm_g': 4.786635e-02, 'sc_conv_w': 4.764092e-02, 'w_out': 6.774954e-02, 'norm_ffn_g': 5.057346e-02, 'w_gate': 2.141736e-02, 'w_up': 2.072333e-02, 'w_down': 3.434486e-02, 'norm_final_g': 1.599640e+01}


def _to_microbatches(a, axis):
    t = _jnp.moveaxis(a, axis, 0)
    t = t.reshape((N_MICROBATCH, t.shape[0] // N_MICROBATCH) + t.shape[1:])
    return _jnp.moveaxis(t, 1, axis + 1)


def setup_inputs(seed: int = 0) -> dict:
    inp = _fwd_setup_inputs(seed)
    key = _jax.random.fold_in(_jax.random.key(seed), 7919)
    shape, _ = _output_shape()
    out = dict(inp)
    out["loss_target"] = _jax.random.normal(_jax.random.fold_in(key, 0), shape, _jnp.float32)
    for i, name in enumerate(TWIN_WEIGHTS):
        w = inp[name].astype(_jnp.float32)
        if MOMENT_SCALE is None:
            s = _jnp.sqrt(_jnp.mean(_jnp.square(w)) + 1e-30)
        else:
            s = MOMENT_SCALE[name]
        km, kv = _jax.random.split(_jax.random.fold_in(key, i + 1))
        out[name] = w
        out["m_" + name] = s * _jax.random.normal(km, w.shape, _jnp.float32)
        out["v_" + name] = (s * s) * _jax.random.uniform(kv, w.shape, _jnp.float32, 0.5, 1.5)
    if N_MICROBATCH > 1:
        for name, axis in PER_EXAMPLE_BATCH_AXIS.items():
            out[name] = _to_microbatches(out[name], axis)
    return {'x': out['x'], 'norm_mix_g': out['norm_mix_g'], 'w_in': out['w_in'], 'ssm_conv_w': out['ssm_conv_w'], 'ssm_conv_b': out['ssm_conv_b'], 'ssm_dt_bias': out['ssm_dt_bias'], 'ssm_A_log': out['ssm_A_log'], 'ssm_D': out['ssm_D'], 'ssm_norm_g': out['ssm_norm_g'], 'sc_conv_w': out['sc_conv_w'], 'w_out': out['w_out'], 'norm_ffn_g': out['norm_ffn_g'], 'w_gate': out['w_gate'], 'w_up': out['w_up'], 'w_down': out['w_down'], 'norm_final_g': out['norm_final_g'], 'loss_target': out['loss_target'], 'm_norm_mix_g': out['m_norm_mix_g'], 'm_w_in': out['m_w_in'], 'm_ssm_conv_w': out['m_ssm_conv_w'], 'm_ssm_conv_b': out['m_ssm_conv_b'], 'm_ssm_dt_bias': out['m_ssm_dt_bias'], 'm_ssm_A_log': out['m_ssm_A_log'], 'm_ssm_D': out['m_ssm_D'], 'm_ssm_norm_g': out['m_ssm_norm_g'], 'm_sc_conv_w': out['m_sc_conv_w'], 'm_w_out': out['m_w_out'], 'm_norm_ffn_g': out['m_norm_ffn_g'], 'm_w_gate': out['m_w_gate'], 'm_w_up': out['m_w_up'], 'm_w_down': out['m_w_down'], 'm_norm_final_g': out['m_norm_final_g'], 'v_norm_mix_g': out['v_norm_mix_g'], 'v_w_in': out['v_w_in'], 'v_ssm_conv_w': out['v_ssm_conv_w'], 'v_ssm_conv_b': out['v_ssm_conv_b'], 'v_ssm_dt_bias': out['v_ssm_dt_bias'], 'v_ssm_A_log': out['v_ssm_A_log'], 'v_ssm_D': out['v_ssm_D'], 'v_ssm_norm_g': out['v_ssm_norm_g'], 'v_sc_conv_w': out['v_sc_conv_w'], 'v_w_out': out['v_w_out'], 'v_norm_ffn_g': out['v_norm_ffn_g'], 'v_w_gate': out['v_w_gate'], 'v_w_up': out['v_w_up'], 'v_w_down': out['v_w_down'], 'v_norm_final_g': out['v_norm_final_g']}


def _loss(weights, diff, rest, loss_target):
    with _jax.named_scope("forward"):
        args = {**rest, TWIN_DIFF_INPUT: diff, **{k: w.astype(_WEIGHT_DTYPES[k]) for k, w in weights.items()}}
        y = _forward(args)
    with _jax.named_scope("loss_head"):
        err = _jnp.square(y.astype(_jnp.float32) - loss_target)
        return 0.5 * _jnp.sum(_jnp.mean(err, axis=-1)) if err.ndim else 0.5 * err


def _adamw(w, g, m, v):
    m = ADAM_B1 * m + (1.0 - ADAM_B1) * g
    v = ADAM_B2 * v + (1.0 - ADAM_B2) * _jnp.square(g)
    m_hat = m / (1.0 - ADAM_B1 ** ADAM_STEP)
    v_hat = v / (1.0 - ADAM_B2 ** ADAM_STEP)
    delta = -ADAM_LR * (m_hat / (_jnp.sqrt(v_hat) + ADAM_EPS) + ADAM_WD * w)
    return delta, m, v


def reference(x, norm_mix_g, w_in, ssm_conv_w, ssm_conv_b, ssm_dt_bias, ssm_A_log, ssm_D, ssm_norm_g, sc_conv_w, w_out, norm_ffn_g, w_gate, w_up, w_down, norm_final_g, loss_target, m_norm_mix_g, m_w_in, m_ssm_conv_w, m_ssm_conv_b, m_ssm_dt_bias, m_ssm_A_log, m_ssm_D, m_ssm_norm_g, m_sc_conv_w, m_w_out, m_norm_ffn_g, m_w_gate, m_w_up, m_w_down, m_norm_final_g, v_norm_mix_g, v_w_in, v_ssm_conv_w, v_ssm_conv_b, v_ssm_dt_bias, v_ssm_A_log, v_ssm_D, v_ssm_norm_g, v_sc_conv_w, v_w_out, v_norm_ffn_g, v_w_gate, v_w_up, v_w_down, v_norm_final_g):
    given = dict(x=x, norm_mix_g=norm_mix_g, w_in=w_in, ssm_conv_w=ssm_conv_w, ssm_conv_b=ssm_conv_b, ssm_dt_bias=ssm_dt_bias, ssm_A_log=ssm_A_log, ssm_D=ssm_D, ssm_norm_g=ssm_norm_g, sc_conv_w=sc_conv_w, w_out=w_out, norm_ffn_g=norm_ffn_g, w_gate=w_gate, w_up=w_up, w_down=w_down, norm_final_g=norm_final_g, loss_target=loss_target, m_norm_mix_g=m_norm_mix_g, m_w_in=m_w_in, m_ssm_conv_w=m_ssm_conv_w, m_ssm_conv_b=m_ssm_conv_b, m_ssm_dt_bias=m_ssm_dt_bias, m_ssm_A_log=m_ssm_A_log, m_ssm_D=m_ssm_D, m_ssm_norm_g=m_ssm_norm_g, m_sc_conv_w=m_sc_conv_w, m_w_out=m_w_out, m_norm_ffn_g=m_norm_ffn_g, m_w_gate=m_w_gate, m_w_up=m_w_up, m_w_down=m_w_down, m_norm_final_g=m_norm_final_g, v_norm_mix_g=v_norm_mix_g, v_w_in=v_w_in, v_ssm_conv_w=v_ssm_conv_w, v_ssm_conv_b=v_ssm_conv_b, v_ssm_dt_bias=v_ssm_dt_bias, v_ssm_A_log=v_ssm_A_log, v_ssm_D=v_ssm_D, v_ssm_norm_g=v_ssm_norm_g, v_sc_conv_w=v_sc_conv_w, v_w_out=v_w_out, v_norm_ffn_g=v_norm_ffn_g, v_w_gate=v_w_gate, v_w_up=v_w_up, v_w_down=v_w_down, v_norm_final_g=v_norm_final_g)
    weights = {n: given[n] for n in TWIN_WEIGHTS}
    shared = {n: given[n] for n in SHARED_INPUTS}
    per_example = {n: given[n] for n in ['x']}
    grad_fn = _jax.value_and_grad(_loss, argnums=(0, 1))

    def one_microbatch(ex, loss_target):
        ex = dict(ex)
        diff = ex.pop(TWIN_DIFF_INPUT)
        return grad_fn(weights, diff, {**shared, **ex}, loss_target)

    if N_MICROBATCH == 1:
        loss, (grad_w, grad_x) = one_microbatch(per_example, given["loss_target"])
    else:
        def body(carry, xs):
            loss_sum, grad_sum = carry
            l_k, (gw_k, gx_k) = one_microbatch(xs[0], xs[1])
            with _jax.named_scope("update"):
                return (loss_sum + l_k, _jax.tree.map(_jnp.add, grad_sum, gw_k)), gx_k

        init = (_jnp.zeros((), _jnp.float32), _jax.tree.map(_jnp.zeros_like, weights))
        (loss, grad_w), grad_x = _jax.lax.scan(body, init, (per_example, given["loss_target"]))
    with _jax.named_scope("update"):
        delta_w, new_m, new_v = {}, {}, {}
        for n in TWIN_WEIGHTS:
            delta_w[n], new_m[n], new_v[n] = _adamw(weights[n], grad_w[n], given["m_" + n], given["v_" + n])
    return (loss, grad_x, *[grad_w[n] for n in TWIN_WEIGHTS], *[delta_w[n] for n in TWIN_WEIGHTS],
            *[new_m[n] for n in TWIN_WEIGHTS], *[new_v[n] for n in TWIN_WEIGHTS])
```

```python
import functools

import jax
import jax.numpy as jnp
from jax import lax
from jax.experimental import pallas as pl
from jax.experimental.pallas import tpu as pltpu

F32 = jnp.float32
BF16 = jnp.bfloat16

N_DEV = 8
HEADDIM = 64
N_GROUPS = 8
N_STATE = 128
CHUNK = 128
K_SSM = 4
K_SC = 3
EPS = 1e-5
LANES = 128
V7X_VMEM_BYTES = 64 * 1024 * 1024
VMEM_LIMIT = (V7X_VMEM_BYTES * 3) // 4

ADAM_LR = 0.001
ADAM_B1 = 0.9
ADAM_B2 = 0.999
ADAM_EPS = 1e-08
ADAM_WD = 0.01
ADAM_STEP = 10


def _tile(n, pref, align):
    t = min(pref, n)
    t -= t % align
    while t >= align:
        if n % t == 0:
            return t
        t -= align
    return n


def _params(sem):
    return pltpu.CompilerParams(dimension_semantics=sem, vmem_limit_bytes=VMEM_LIMIT)


def _matmul(a, b, *, ta=False, tb=False, out_dtype=BF16, add=None, name, tm=1024, tn=1024, tk=512):
    m = a.shape[1] if ta else a.shape[0]
    k = a.shape[0] if ta else a.shape[1]
    n = b.shape[0] if tb else b.shape[1]
    assert k == (b.shape[1] if tb else b.shape[0])
    tm, tn, tk = _tile(m, tm, LANES), _tile(n, tn, LANES), _tile(k, tk, LANES)
    nk = k // tk
    dims = (((0 if ta else 1,), (1 if tb else 0,)), ((), ()))

    def body(*refs):
        if add is None:
            a_ref, b_ref, o_ref, acc = refs
        else:
            a_ref, b_ref, add_ref, o_ref, acc = refs
        kk = pl.program_id(2)

        @pl.when(kk == 0)
        def _():
            acc[...] = jnp.zeros_like(acc)

        acc[...] += lax.dot_general(a_ref[...].astype(BF16), b_ref[...].astype(BF16), dims,
                                    preferred_element_type=F32)

        @pl.when(kk == nk - 1)
        def _():
            r = acc[...]
            if add is not None:
                r = r + add_ref[...].astype(F32)
            o_ref[...] = r.astype(o_ref.dtype)

    a_spec = (pl.BlockSpec((tk, tm), lambda i, j, kk: (kk, i)) if ta
              else pl.BlockSpec((tm, tk), lambda i, j, kk: (i, kk)))
    b_spec = (pl.BlockSpec((tn, tk), lambda i, j, kk: (j, kk)) if tb
              else pl.BlockSpec((tk, tn), lambda i, j, kk: (kk, j)))
    o_spec = pl.BlockSpec((tm, tn), lambda i, j, kk: (i, j))
    in_specs = [a_spec, b_spec] + ([o_spec] if add is not None else [])
    args = (a, b) + ((add,) if add is not None else ())
    return pl.pallas_call(
        body, name=name, grid=(m // tm, n // tn, nk),
        in_specs=in_specs, out_specs=o_spec,
        out_shape=jax.ShapeDtypeStruct((m, n), out_dtype),
        scratch_shapes=[pltpu.VMEM((tm, tn), F32)],
        compiler_params=_params(("parallel", "parallel", "arbitrary")),
    )(*args)


def _rows_call(fn, *, rows, tr, row_ins, full_ins, row_outs, acc_outs, name):
    nr, nf, no, na = len(row_ins), len(full_ins), len(row_outs), len(acc_outs)

    def body(*refs):
        vals = [r[...] for r in refs[:nr + nf]]
        outs, accs = fn(*vals)
        for r, v in zip(refs[nr + nf:nr + nf + no], outs):
            r[...] = v.astype(r.dtype)
        if na:
            @pl.when(pl.program_id(0) == 0)
            def _():
                for r in refs[nr + nf + no:]:
                    r[...] = jnp.zeros_like(r)
            for r, v in zip(refs[nr + nf + no:], accs):
                r[...] += v

    in_specs = [pl.BlockSpec((tr, w), functools.partial(lambda cb, i: (i, cb), cb)) for _, w, cb in row_ins]
    in_specs += [pl.BlockSpec(f.shape, lambda i: (0, 0)) for f in full_ins]
    out_specs = [pl.BlockSpec((tr, w), lambda i: (i, 0)) for w, _ in row_outs]
    out_specs += [pl.BlockSpec(s, lambda i: (0, 0)) for s in acc_outs]
    out_shape = [jax.ShapeDtypeStruct((rows, w), dt) for w, dt in row_outs]
    out_shape += [jax.ShapeDtypeStruct(s, F32) for s in acc_outs]
    res = pl.pallas_call(
        body, name=name, grid=(rows // tr,), in_specs=in_specs, out_specs=out_specs, out_shape=out_shape,
        compiler_params=_params(("arbitrary",)),
    )(*[a for a, _, _ in row_ins], *full_ins)
    return res


def _cols_call(fn, *, rows, cols, cw, col_ins, par_ins, col_outs, par_outs, name):
    nc, npar, no = len(col_ins), len(par_ins), len(col_outs)

    def body(*refs):
        vals = [r[...] for r in refs[:nc + npar]]
        outs, pouts = fn(*vals)
        for r, v in zip(refs[nc + npar:], tuple(outs) + tuple(pouts)):
            r[...] = v.astype(r.dtype)

    in_specs = [pl.BlockSpec((rows, cw), functools.partial(lambda off, j: (0, off + j), off)) for _, off in col_ins]
    in_specs += [pl.BlockSpec((p.shape[0], cw), functools.partial(lambda off, j: (0, off + j), off))
                 for p, off in par_ins]
    out_specs = [pl.BlockSpec((rows, cw), lambda j: (0, j)) for _ in col_outs]
    out_specs += [pl.BlockSpec((k, cw), lambda j: (0, j)) for k in par_outs]
    out_shape = [jax.ShapeDtypeStruct((rows, cols), dt) for dt in col_outs]
    out_shape += [jax.ShapeDtypeStruct((k, cols), F32) for k in par_outs]
    return pl.pallas_call(
        body, name=name, grid=(cols // cw,), in_specs=in_specs, out_specs=out_specs, out_shape=out_shape,
        compiler_params=_params(("parallel",)),
    )(*[a for a, _ in col_ins], *[p for p, _ in par_ins])


def _sigmoid(v):
    return 1.0 / (1.0 + jnp.exp(-v))


def _softplus(v):
    return jnp.maximum(v, 0.0) + jnp.log(1.0 + jnp.exp(-jnp.abs(v)))


def _rms(v, g):
    return v * lax.rsqrt(jnp.mean(v * v, axis=-1, keepdims=True) + EPS) * g


def _shift_down(v, s, row):
    return jnp.where(row >= s, pltpu.roll(v, s, 0), 0.0)


def _shift_up(v, s, row):
    n = v.shape[0]
    return jnp.where(row < n - s, pltpu.roll(v, n - s, 0), 0.0)


def _causal_conv(u, w, row):
    k_taps = w.shape[0]
    acc = u * w[k_taps - 1:k_taps, :]
    for k in range(k_taps - 1):
        acc = acc + _shift_down(u, k_taps - 1 - k, row) * w[k:k + 1, :]
    return acc


def _causal_conv_bwd(u, dy, w, row):
    k_taps = w.shape[0]
    tap = lax.broadcasted_iota(jnp.int32, w.shape, 0)
    du = dy * w[k_taps - 1:k_taps, :]
    dw = jnp.where(tap == k_taps - 1, jnp.sum(dy * u, axis=0, keepdims=True), 0.0)
    for k in range(k_taps - 1):
        s = k_taps - 1 - k
        du = du + _shift_up(dy, s, row) * w[k:k + 1, :]
        dw = dw + jnp.where(tap == k, jnp.sum(dy * _shift_down(u, s, row), axis=0, keepdims=True), 0.0)
    return du, dw


def _conv_silu_fwd(u, w, b):
    u = u.astype(F32)
    row = lax.broadcasted_iota(jnp.int32, u.shape, 0)
    pre = _causal_conv(u, w, row) + b
    return (pre * _sigmoid(pre),), ()


def _conv_silu_bwd(u, dy, w, b):
    u = u.astype(F32)
    dy = dy.astype(F32)
    row = lax.broadcasted_iota(jnp.int32, u.shape, 0)
    pre = _causal_conv(u, w, row) + b
    s = _sigmoid(pre)
    dpre = dy * (s * (1.0 + pre * (1.0 - s)))
    du, dw = _causal_conv_bwd(u, dpre, w, row)
    return (du,), (dw, jnp.sum(dpre, axis=0, keepdims=True))


def _shortconv_fwd(gb, gc, u, w):
    gb, gc, u = gb.astype(F32), gc.astype(F32), u.astype(F32)
    row = lax.broadcasted_iota(jnp.int32, u.shape, 0)
    return (gb * _causal_conv(gc * u, w, row),), ()


def _shortconv_bwd(gb, gc, u, dy, w):
    gb, gc, u, dy = gb.astype(F32), gc.astype(F32), u.astype(F32), dy.astype(F32)
    row = lax.broadcasted_iota(jnp.int32, u.shape, 0)
    v = gc * u
    dgb = dy * _causal_conv(v, w, row)
    dv, dw = _causal_conv_bwd(v, dy * gb, w, row)
    return (dgb, dv * u, dv * gc), (dw,)


def _split3(v):
    hi = v.astype(BF16)
    r1 = v - hi.astype(F32)
    mid = r1.astype(BF16)
    lo = (r1 - mid.astype(F32)).astype(BF16)
    return hi, mid, lo


def _exact_dot(v, m01, dims, v_is_lhs):
    def one(p):
        return (lax.dot_general(p, m01, dims, preferred_element_type=F32) if v_is_lhs
                else lax.dot_general(m01, p, dims, preferred_element_type=F32))
    hi, mid, lo = _split3(v)
    return (one(lo) + one(mid)) + one(hi)


_NN = (((1,), (0,)), ((), ()))
_NT = (((1,), (1,)), ((), ()))
_TN = (((0,), (0,)), ((), ()))


@jax.custom_vjp
def _cumsum_rows(tril, v):
    return _exact_dot(v, tril, _NN, False)


def _cumsum_rows_fwd(tril, v):
    return _cumsum_rows(tril, v), tril


def _cumsum_rows_bwd(tril, ct):
    return None, _exact_dot(ct, tril, _TN, False)


_cumsum_rows.defvjp(_cumsum_rows_fwd, _cumsum_rows_bwd)


@jax.custom_vjp
def _cumsum_lanes(tril, v):
    return _exact_dot(v, tril, _NT, True)


def _cumsum_lanes_fwd(tril, v):
    return _cumsum_lanes(tril, v), tril


def _cumsum_lanes_bwd(tril, ct):
    return None, _exact_dot(ct, tril, _NN, True)


_cumsum_lanes.defvjp(_cumsum_lanes_fwd, _cumsum_lanes_bwd)


def _ssd_chunk(g, r_heads, xs, bg, cg, dtc, dtr, bias_r, bias_c, alog_r, alog_c, dskip, hp):
    l_len, rp = xs.shape
    p = rp // r_heads
    dt_c = _softplus(dtc + bias_r)
    dt_r = _softplus(dtr + bias_c)
    da_c = dt_c * (-jnp.exp(alog_r))
    da_r = dt_r * (-jnp.exp(alog_c))
    li = lax.broadcasted_iota(jnp.int32, (l_len, l_len), 0)
    si = lax.broadcasted_iota(jnp.int32, (l_len, l_len), 1)
    causal = si <= li
    tril = jnp.where(causal, 1.0, 0.0).astype(BF16)
    cs_c = _cumsum_rows(tril, da_c)
    cs_r = _cumsum_lanes(tril, da_r)
    head_lane = lax.broadcasted_iota(jnp.int32, (1, dtc.shape[1]), 1)
    head_sub = lax.broadcasted_iota(jnp.int32, (dtr.shape[0], 1), 0)
    lane_head = lax.broadcasted_iota(jnp.int32, (1, rp), 1) // p

    def col(v, r):
        return jnp.sum(jnp.where(head_lane == g * r_heads + r, v, 0.0), axis=1, keepdims=True)

    def expand(v):
        out = jnp.where(lane_head == 0, col(v, 0), 0.0)
        for r in range(1, r_heads):
            out = out + jnp.where(lane_head == r, col(v, r), 0.0)
        return out

    dt_e = expand(dt_c)
    cs_e = expand(cs_c)
    cl_e = expand(cs_c[l_len - 1:l_len, :])
    x = xs * dt_e
    bgb, cgb = bg.astype(BF16), cg.astype(BF16)
    cb = lax.dot_general(cgb, bgb, _NT, preferred_element_type=F32)
    ms, xm = [], []
    for r in range(r_heads):
        row = jnp.sum(jnp.where(head_sub == g * r_heads + r, cs_r, 0.0), axis=0, keepdims=True)
        seg = col(cs_c, r) - row
        decay = jnp.exp(jnp.where(causal, seg, -1e30))
        ms.append((cb * decay).astype(BF16))
        xm.append(jnp.where(lane_head == r, x, 0.0).astype(BF16))
    y_diag = lax.dot_general(jnp.concatenate(ms, axis=1), jnp.concatenate(xm, axis=0), _NN,
                             preferred_element_type=F32)
    y_off = lax.dot_general(cgb, hp.astype(BF16), _NN, preferred_element_type=F32) * jnp.exp(cs_e)
    xd = (x * jnp.exp(cl_e - cs_e)).astype(BF16)
    states = lax.dot_general(bgb, xd, _TN, preferred_element_type=F32)
    h_next = hp * jnp.exp(cl_e) + states
    y = y_diag + y_off + expand(dskip) * xs
    return y, h_next


def _ssd_specs(t_len, d_ssm, r_heads, reverse):
    rp = r_heads * HEADDIM
    nc = t_len // CHUNK
    cidx = (lambda c: nc - 1 - c) if reverse else (lambda c: c)
    b_off = d_ssm // N_STATE
    specs = dict(
        xs=pl.BlockSpec((CHUNK, rp), lambda c, g: (cidx(c), g)),
        b=pl.BlockSpec((CHUNK, N_STATE), lambda c, g: (cidx(c), b_off + g)),
        c=pl.BlockSpec((CHUNK, N_STATE), lambda c, g: (cidx(c), b_off + N_GROUPS + g)),
        dtc=pl.BlockSpec((CHUNK, LANES), lambda c, g: (cidx(c), 0)),
        dtr=lambda h: pl.BlockSpec((h, CHUNK), lambda c, g: (0, cidx(c))),
        full=lambda shape: pl.BlockSpec(shape, lambda c, g: (0, 0)),
        hprev=pl.BlockSpec((None, None, N_STATE, rp), lambda c, g: (cidx(c), g, 0, 0)),
    )
    return specs, nc, rp


def _ssd_fwd(xbc, dtc, dtr, small, *, d_ssm, r_heads):
    t_len = xbc.shape[0]
    sp, nc, rp = _ssd_specs(t_len, d_ssm, r_heads, False)

    def body(xs_ref, b_ref, c_ref, dtc_ref, dtr_ref, br, bc, ar, ac, dk, y_ref, hprev_ref, h_ref):
        c, g = pl.program_id(0), pl.program_id(1)

        @pl.when(c == 0)
        def _():
            h_ref[g] = jnp.zeros((N_STATE, rp), F32)

        hp = h_ref[g]
        hprev_ref[...] = hp
        y, hn = _ssd_chunk(g, r_heads, xs_ref[...].astype(F32), b_ref[...].astype(F32), c_ref[...].astype(F32),
                           dtc_ref[...], dtr_ref[...], br[...], bc[...], ar[...], ac[...], dk[...], hp)
        y_ref[...] = y
        h_ref[g] = hn

    in_specs = [sp["xs"], sp["b"], sp["c"], sp["dtc"], sp["dtr"](dtr.shape[0])]
    in_specs += [sp["full"](s.shape) for s in small]
    return pl.pallas_call(
        body, name="ssd_fwd", grid=(nc, N_GROUPS), in_specs=in_specs,
        out_specs=[pl.BlockSpec((CHUNK, rp), lambda c, g: (c, g)), sp["hprev"]],
        out_shape=[jax.ShapeDtypeStruct((t_len, d_ssm), F32),
                   jax.ShapeDtypeStruct((nc, N_GROUPS, N_STATE, rp), F32)],
        scratch_shapes=[pltpu.VMEM((N_GROUPS, N_STATE, rp), F32)],
        compiler_params=_params(("arbitrary", "arbitrary")),
    )(xbc, xbc, xbc, dtc, dtr, *small)


def _ssd_bwd(xbc, dtc, dtr, small, hprev, dy, *, d_ssm, r_heads):
    t_len = xbc.shape[0]
    sp, nc, rp = _ssd_specs(t_len, d_ssm, r_heads, True)
    n_small = len(small)

    def body(*refs):
        xs_ref, b_ref, c_ref, dtc_ref, dtr_ref = refs[:5]
        small_refs = refs[5:5 + n_small]
        hprev_ref, dy_ref = refs[5 + n_small:7 + n_small]
        dxs_ref, db_ref, dc_ref, ddtc_ref, ddtr_ref = refs[7 + n_small:12 + n_small]
        dsmall_refs = refs[12 + n_small:12 + 2 * n_small]
        dh_ref = refs[12 + 2 * n_small]
        c, g = pl.program_id(0), pl.program_id(1)

        @pl.when(c == 0)
        def _():
            dh_ref[g] = jnp.zeros((N_STATE, rp), F32)

        @pl.when(g == 0)
        def _():
            ddtc_ref[...] = jnp.zeros_like(ddtc_ref)
            ddtr_ref[...] = jnp.zeros_like(ddtr_ref)

        @pl.when((c == 0) & (g == 0))
        def _():
            for r in dsmall_refs:
                r[...] = jnp.zeros_like(r)

        args = (xs_ref[...].astype(F32), b_ref[...].astype(F32), c_ref[...].astype(F32), dtc_ref[...], dtr_ref[...],
                *[r[...] for r in small_refs], hprev_ref[...])
        _, vjp = jax.vjp(functools.partial(_ssd_chunk, g, r_heads), *args)
        grads = vjp((dy_ref[...], dh_ref[g]))
        dxs_ref[...] = grads[0].astype(dxs_ref.dtype)
        db_ref[...] = grads[1].astype(db_ref.dtype)
        dc_ref[...] = grads[2].astype(dc_ref.dtype)
        ddtc_ref[...] += grads[3]
        ddtr_ref[...] += grads[4]
        for r, gr in zip(dsmall_refs, grads[5:5 + n_small]):
            r[...] += gr
        dh_ref[g] = grads[5 + n_small]

    rev = lambda c: nc - 1 - c
    in_specs = [sp["xs"], sp["b"], sp["c"], sp["dtc"], sp["dtr"](dtr.shape[0])]
    in_specs += [sp["full"](s.shape) for s in small]
    in_specs += [sp["hprev"], pl.BlockSpec((CHUNK, rp), lambda c, g: (rev(c), g))]
    out_specs = [pl.BlockSpec((CHUNK, rp), lambda c, g: (rev(c), g)),
                 pl.BlockSpec((CHUNK, N_STATE), lambda c, g: (rev(c), g)),
                 pl.BlockSpec((CHUNK, N_STATE), lambda c, g: (rev(c), g)),
                 sp["dtc"], sp["dtr"](dtr.shape[0])]
    out_specs += [sp["full"](s.shape) for s in small]
    out_shape = [jax.ShapeDtypeStruct((t_len, d_ssm), BF16),
                 jax.ShapeDtypeStruct((t_len, N_GROUPS * N_STATE), BF16),
                 jax.ShapeDtypeStruct((t_len, N_GROUPS * N_STATE), BF16),
                 jax.ShapeDtypeStruct(dtc.shape, F32), jax.ShapeDtypeStruct(dtr.shape, F32)]
    out_shape += [jax.ShapeDtypeStruct(s.shape, F32) for s in small]
    return pl.pallas_call(
        body, name="ssd_bwd", grid=(nc, N_GROUPS), in_specs=in_specs, out_specs=out_specs, out_shape=out_shape,
        scratch_shapes=[pltpu.VMEM((N_GROUPS, N_STATE, rp), F32)],
        compiler_params=_params(("arbitrary", "arbitrary")),
    )(xbc, xbc, xbc, dtc, dtr, *small, hprev, dy)


def _exchange(srcs, *, scatter, name):
    n = len(srcs)

    def body(*refs):
        src, dst = refs[:n], refs[n:2 * n]
        send_sems, recv_sems, local_sems = refs[2 * n:]
        mx, my, mc = lax.axis_index("x"), lax.axis_index("y"), lax.axis_index("c")
        me = 4 * mx + 2 * my + mc
        pending = []
        for a in range(n):
            own = src[a].at[me] if scatter else src[a]
            local = pltpu.make_async_copy(own, dst[a].at[me], local_sems.at[a])
            local.start()
            pending.append(local)
            for j in range(1, N_DEV):
                jx, jy, jc = (j >> 2) & 1, (j >> 1) & 1, j & 1
                px = 1 - mx if jx else mx
                py = 1 - my if jy else my
                pc = 1 - mc if jc else mc
                peer = 4 * px + 2 * py + pc
                out = pltpu.make_async_remote_copy(
                    src_ref=src[a].at[peer] if scatter else src[a], dst_ref=dst[a].at[me],
                    send_sem=send_sems.at[a, j - 1], recv_sem=recv_sems.at[a, j - 1],
                    device_id=(px, py, pc), device_id_type=pl.DeviceIdType.MESH)
                out.start()
                arrival = pltpu.make_async_remote_copy(
                    src_ref=src[a].at[peer] if scatter else src[a], dst_ref=dst[a].at[peer],
                    send_sem=send_sems.at[a, j - 1], recv_sem=recv_sems.at[a, j - 1],
                    device_id=(px, py, pc), device_id_type=pl.DeviceIdType.MESH)
                pending.append((out, arrival))
        for item in pending:
            if isinstance(item, tuple):
                item[0].wait_send()
                item[1].wait_recv()
            else:
                item.wait()

    any_spec = pl.BlockSpec(memory_space=pl.ANY)
    out_shape = [jax.ShapeDtypeStruct(s.shape if scatter else (N_DEV,) + s.shape, s.dtype) for s in srcs]
    return pl.pallas_call(
        body, name=name, in_specs=[any_spec] * n, out_specs=[any_spec] * n, out_shape=out_shape,
        scratch_shapes=[pltpu.SemaphoreType.DMA((n, N_DEV - 1)), pltpu.SemaphoreType.DMA((n, N_DEV - 1)),
                        pltpu.SemaphoreType.DMA((n,))],
        compiler_params=pltpu.CompilerParams(has_side_effects=True),
    )(*srcs)


def _adamw(w, g, m, v):
    m = ADAM_B1 * m + (1.0 - ADAM_B1) * g
    v = ADAM_B2 * v + (1.0 - ADAM_B2) * (g * g)
    m_hat = m / (1.0 - ADAM_B1 ** ADAM_STEP)
    v_hat = v / (1.0 - ADAM_B2 ** ADAM_STEP)
    delta = -ADAM_LR * (m_hat / (jnp.sqrt(v_hat) + ADAM_EPS) + ADAM_WD * w)
    return delta, m, v


def _reduce_adamw(parts, w, m, v, *, name):
    rows, cols = w.shape
    tr = _tile(rows, 128, 16)

    def body(p_ref, w_ref, m_ref, v_ref, g_ref, d_ref, mo_ref, vo_ref):
        g = p_ref[0].astype(F32)
        for k in range(1, N_DEV):
            g = g + p_ref[k].astype(F32)
        delta, mn, vn = _adamw(w_ref[...], g, m_ref[...], v_ref[...])
        g_ref[...] = g
        d_ref[...] = delta
        mo_ref[...] = mn
        vo_ref[...] = vn

    spec = pl.BlockSpec((tr, cols), lambda i: (i, 0))
    return pl.pallas_call(
        body, name=name, grid=(rows // tr,),
        in_specs=[pl.BlockSpec((N_DEV, tr, cols), lambda i: (0, i, 0)), spec, spec, spec],
        out_specs=[spec] * 4, out_shape=[jax.ShapeDtypeStruct((rows, cols), F32)] * 4,
        compiler_params=_params(("parallel",)),
    )(parts, w, m, v)


def _cols_of(g):
    return jnp.transpose(g, (1, 0, 2)).reshape(g.shape[1], -1)


def _cols_split(w):
    r = w.shape[0]
    return jnp.transpose(w.reshape(r, N_DEV, -1), (1, 0, 2))


def _pad_to(a, rows, cols):
    return jnp.pad(a, ((0, rows - a.shape[0]), (0, cols - a.shape[1])))


def kernel(x, norm_mix_g, w_in, ssm_conv_w, ssm_conv_b, ssm_dt_bias, ssm_A_log, ssm_D, ssm_norm_g, sc_conv_w, w_out, norm_ffn_g, w_gate, w_up, w_down, norm_final_g, loss_target, m_norm_mix_g, m_w_in, m_ssm_conv_w, m_ssm_conv_b, m_ssm_dt_bias, m_ssm_A_log, m_ssm_D, m_ssm_norm_g, m_sc_conv_w, m_w_out, m_norm_ffn_g, m_w_gate, m_w_up, m_w_down, m_norm_final_g, v_norm_mix_g, v_w_in, v_ssm_conv_w, v_ssm_conv_b, v_ssm_dt_bias, v_ssm_A_log, v_ssm_D, v_ssm_norm_g, v_sc_conv_w, v_w_out, v_norm_ffn_g, v_w_gate, v_w_up, v_w_down, v_norm_final_g):
    t_len, d = x.shape[1], x.shape[2]
    heads = d // HEADDIM
    r_heads = heads // N_GROUPS
    d_xbc = d + 2 * N_GROUPS * N_STATE
    ff = w_down.shape[1] * N_DEV
    off_xbc, off_dt = d, d + d_xbc
    off_cb = off_dt + heads
    d_in = off_cb + 3 * d
    w_main = 4 * d + d_xbc
    me = 4 * lax.axis_index("x") + 2 * lax.axis_index("y") + lax.axis_index("c")

    x2 = x[0]
    target = loss_target[0]

    small_w = jnp.concatenate([_pad_to(ssm_conv_w[0], K_SSM, d_xbc // N_DEV),
                               _pad_to(sc_conv_w[0], K_SC + 1, d_xbc // N_DEV)], axis=0)
    g_in, g_out, g_gate, g_up, g_down, g_small = _exchange(
        [w_in[0].astype(BF16), w_out[0].astype(BF16), w_gate[0].astype(BF16), w_up[0].astype(BF16),
         w_down[0].astype(BF16), small_w], scatter=False, name="gather_weights")
    w_in_all = _cols_of(g_in)
    wm = jnp.concatenate([w_in_all[:, :off_dt], w_in_all[:, off_cb:]], axis=1)
    wdt = _pad_to(w_in_all[:, off_dt:off_cb], d, LANES)
    wo = g_out.reshape(2 * d, d)
    wgu = jnp.concatenate([_cols_of(g_gate), _cols_of(g_up)], axis=1)
    wd = g_down.reshape(ff, d)
    cw_ssm = _cols_of(g_small[:, :K_SSM, :])
    cw_sc = _cols_of(g_small[:, K_SSM:K_SSM + K_SC, :d // N_DEV])

    g1, g2, g3 = norm_mix_g, norm_ffn_g, norm_final_g.reshape(1, d)
    gs = ssm_norm_g
    dt_bias_r = _pad_to(ssm_dt_bias, 1, LANES)
    alog_r = _pad_to(ssm_A_log, 1, LANES)
    dskip = _pad_to(ssm_D, 1, LANES)
    dt_bias_c = ssm_dt_bias.reshape(heads, 1)
    alog_c = ssm_A_log.reshape(heads, 1)
    small = [dt_bias_r, dt_bias_c, alog_r, alog_c, dskip]

    tr = _tile(t_len, 256, 8)
    cw = LANES
    slab = lambda col: col // cw

    (n1,) = _rows_call(lambda v, g: ((_rms(v, g),), ()), rows=t_len, tr=tr, row_ins=[(x2, d, 0)], full_ins=[g1],
                       row_outs=[(d, BF16)], acc_outs=[], name="norm_mix")
    proj = _matmul(n1, wm, out_dtype=BF16, name="proj_main")
    dt_raw = _matmul(n1, wdt, out_dtype=F32, name="proj_dt")
    dt_raw_t = jnp.transpose(dt_raw[:, :heads])
    (xbc,) = _cols_call(_conv_silu_fwd, rows=t_len, cols=d_xbc, cw=cw, col_ins=[(proj, slab(off_xbc))],
                        par_ins=[(cw_ssm, 0), (ssm_conv_b, 0)], col_outs=[BF16], par_outs=[], name="ssm_conv")
    y_ssd, hprev = _ssd_fwd(xbc, dt_raw, dt_raw_t, small, d_ssm=d, r_heads=r_heads)

    def gate_norm(y, z, g):
        z = z.astype(F32)
        return _rms(y * (z * _sigmoid(z)), g)

    (y_ssm,) = _rows_call(lambda y, z, g: ((gate_norm(y, z, g),), ()), rows=t_len, tr=tr,
                          row_ins=[(y_ssd, d, 0), (proj, d, 0)], full_ins=[gs], row_outs=[(d, BF16)], acc_outs=[],
                          name="ssm_gate_norm")
    sc0 = slab(d + d_xbc)
    (y_sc,) = _cols_call(_shortconv_fwd, rows=t_len, cols=d, cw=cw,
                         col_ins=[(proj, sc0), (proj, sc0 + slab(d)), (proj, sc0 + 2 * slab(d))],
                         par_ins=[(cw_sc, 0)], col_outs=[BF16], par_outs=[], name="shortconv")
    y_mix = jnp.concatenate([y_ssm, y_sc], axis=1)
    h1 = _matmul(y_mix, wo, out_dtype=F32, add=x2, name="out_proj")
    (n2,) = _rows_call(lambda v, g: ((_rms(v, g),), ()), rows=t_len, tr=tr, row_ins=[(h1, d, 0)], full_ins=[g2],
                       row_outs=[(d, BF16)], acc_outs=[], name="norm_ffn")
    gu = _matmul(n2, wgu, out_dtype=BF16, name="ffn_gate_up")
    tr_ff = _tile(t_len, 128, 8)

    def act(gv, uv):
        gv, uv = gv.astype(F32), uv.astype(F32)
        return ((gv * _sigmoid(gv) * uv,), ())

    (a_ff,) = _rows_call(act, rows=t_len, tr=tr_ff, row_ins=[(gu, ff, 0), (gu, ff, 1)], full_ins=[],
                         row_outs=[(ff, BF16)], acc_outs=[], name="ffn_act")
    h2 = _matmul(a_ff, wd, out_dtype=F32, add=h1, name="ffn_down")

    def head(hv, tv, g):
        def f(hh, gg):
            e = _rms(hh, gg) - tv
            return (0.5 / d) * jnp.sum(e * e)
        val, (dh, dg) = jax.value_and_grad(f, argnums=(0, 1))(hv, g)
        return (dh,), (jnp.full((1, LANES), val, F32), dg)

    dh2, loss_acc, dg3 = _rows_call(head, rows=t_len, tr=tr, row_ins=[(h2, d, 0), (target, d, 0)], full_ins=[g3],
                                    row_outs=[(d, F32)], acc_outs=[(1, LANES), (1, d)], name="loss_head")
    loss = lax.psum(loss_acc[0, 0], ("x", "y", "c"))

    da = _matmul(dh2, wd, tb=True, out_dtype=BF16, name="d_ffn_act")
    dwd = _matmul(a_ff, dh2, ta=True, out_dtype=BF16, name="d_w_down")

    def act_bwd(dav, gv, uv):
        dav, gv, uv = dav.astype(F32), gv.astype(F32), uv.astype(F32)
        s = _sigmoid(gv)
        return ((jnp.concatenate([dav * uv * (s * (1.0 + gv * (1.0 - s))), dav * gv * s], axis=1),), ())

    (dgu,) = _rows_call(act_bwd, rows=t_len, tr=tr_ff, row_ins=[(da, ff, 0), (gu, ff, 0), (gu, ff, 1)], full_ins=[],
                        row_outs=[(2 * ff, BF16)], acc_outs=[], name="d_ffn_gate_up")
    dn2 = _matmul(dgu, wgu, tb=True, out_dtype=F32, name="d_norm_ffn_out")
    dwgu = _matmul(n2, dgu, ta=True, out_dtype=BF16, name="d_w_gate_up")

    def norm_bwd(v, dn, dres, g):
        _, vjp = jax.vjp(_rms, v, g)
        dv, dg = vjp(dn)
        return (dv + dres,), (dg,)

    dh1, dg2 = _rows_call(norm_bwd, rows=t_len, tr=tr, row_ins=[(h1, d, 0), (dn2, d, 0), (dh2, d, 0)], full_ins=[g2],
                          row_outs=[(d, F32)], acc_outs=[(1, d)], name="d_norm_ffn")

    dy_mix = _matmul(dh1, wo, tb=True, out_dtype=BF16, name="d_y_mix")
    dwo = _matmul(y_mix, dh1, ta=True, out_dtype=BF16, name="d_w_out")
    (dgb, dgc, du), (dcw_sc,) = (lambda r: (r[:3], r[3:]))(_cols_call(
        _shortconv_bwd, rows=t_len, cols=d, cw=cw,
        col_ins=[(proj, sc0), (proj, sc0 + slab(d)), (proj, sc0 + 2 * slab(d)), (dy_mix, slab(d))],
        par_ins=[(cw_sc, 0)], col_outs=[BF16] * 3, par_outs=[K_SC], name="d_shortconv"))

    def gate_norm_bwd(y, z, dyo, g):
        _, vjp = jax.vjp(gate_norm, y, z.astype(F32), g)
        dy, dz, dg = vjp(dyo.astype(F32))
        return (dy, dz), (dg,)

    dy_ssd, dz, dgs = _rows_call(gate_norm_bwd, rows=t_len, tr=tr,
                                 row_ins=[(y_ssd, d, 0), (proj, d, 0), (dy_mix, d, 0)], full_ins=[gs],
                                 row_outs=[(d, F32), (d, BF16)], acc_outs=[(1, d)], name="d_ssm_gate_norm")
    ssd_grads = _ssd_bwd(xbc, dt_raw, dt_raw_t, small, hprev, dy_ssd, d_ssm=d, r_heads=r_heads)
    dxs, dbm, dcm, ddt_c, ddt_r = ssd_grads[:5]
    dbias_r, dbias_c, dalog_r, dalog_c, ddskip = ssd_grads[5:]
    dxbc = jnp.concatenate([dxs, dbm, dcm], axis=1)
    (dxbc_pre,), (dcw_ssm, dcb_ssm) = (lambda r: (r[:1], r[1:]))(_cols_call(
        _conv_silu_bwd, rows=t_len, cols=d_xbc, cw=cw, col_ins=[(proj, slab(off_xbc)), (dxbc, 0)],
        par_ins=[(cw_ssm, 0), (ssm_conv_b, 0)], col_outs=[BF16], par_outs=[K_SSM, 1], name="d_ssm_conv"))
    dproj = jnp.concatenate([dz, dxbc_pre, dgb, dgc, du], axis=1)
    ddt = ddt_c + _pad_to(jnp.transpose(ddt_r), t_len, LANES)
    dn1 = _matmul(ddt, wdt, tb=True, out_dtype=F32, name="d_norm_mix_out_dt")
    dn1 = _matmul(dproj, wm, tb=True, out_dtype=F32, add=dn1, name="d_norm_mix_out")
    dwm = _matmul(n1, dproj, ta=True, out_dtype=BF16, name="d_w_in_main")
    dwdt = _matmul(n1, ddt, ta=True, out_dtype=BF16, name="d_w_in_dt")
    dx, dg1 = _rows_call(norm_bwd, rows=t_len, tr=tr, row_ins=[(x2, d, 0), (dn1, d, 0), (dh1, d, 0)], full_ins=[g1],
                         row_outs=[(d, F32)], acc_outs=[(1, d)], name="d_norm_mix")

    dw_in_all = jnp.concatenate([dwm[:, :off_dt], dwdt[:, :heads], dwm[:, off_dt:]], axis=1)
    p_in, p_out, p_gate, p_up, p_down = _exchange(
        [_cols_split(dw_in_all), dwo.reshape(N_DEV, -1, d), _cols_split(dwgu[:, :ff]), _cols_split(dwgu[:, ff:]),
         dwd.reshape(N_DEV, -1, d)], scatter=True, name="scatter_weight_grads")

    wide = d_xbc
    rows_small = [dg1, dcb_ssm, dbias_r + _pad_to(dbias_c.reshape(1, heads), 1, LANES),
                  dalog_r + _pad_to(dalog_c.reshape(1, heads), 1, LANES), ddskip, dgs, dg2, dg3]
    packed = jnp.concatenate([_pad_to(r, 1, wide) for r in rows_small]
                             + [dcw_ssm, _pad_to(dcw_sc, K_SC, wide), jnp.zeros((1, wide), F32)], axis=0)
    (p_small,) = _exchange([packed], scatter=False, name="gather_small_grads")

    conv_lo = me * (d_xbc // N_DEV)
    sc_lo = me * (d // N_DEV)

    def pack_state(vals):
        (nm, cb, dtb, al, dk, sg, nf, nfin, cws, scs) = vals
        rows = [_pad_to(a.reshape(1, -1), 1, wide) for a in (nm, cb, dtb, al, dk, sg, nf, nfin)]
        cws_full = lax.dynamic_update_slice(jnp.zeros((K_SSM, wide), F32), cws[0], (0, conv_lo))
        scs_full = lax.dynamic_update_slice(jnp.zeros((K_SC, wide), F32), scs[0], (0, sc_lo))
        return jnp.concatenate(rows + [cws_full, scs_full, jnp.zeros((1, wide), F32)], axis=0)

    w_small = pack_state((norm_mix_g, ssm_conv_b, ssm_dt_bias, ssm_A_log, ssm_D, ssm_norm_g, norm_ffn_g, norm_final_g,
                          ssm_conv_w, sc_conv_w))
    m_small = pack_state((m_norm_mix_g, m_ssm_conv_b, m_ssm_dt_bias, m_ssm_A_log, m_ssm_D, m_ssm_norm_g, m_norm_ffn_g,
                          m_norm_final_g, m_ssm_conv_w, m_sc_conv_w))
    v_small = pack_state((v_norm_mix_g, v_ssm_conv_b, v_ssm_dt_bias, v_ssm_A_log, v_ssm_D, v_ssm_norm_g, v_norm_ffn_g,
                          v_norm_final_g, v_ssm_conv_w, v_sc_conv_w))

    upd = {
        "w_in": _reduce_adamw(p_in, w_in[0], m_w_in[0], v_w_in[0], name="adamw_w_in"),
        "w_out": _reduce_adamw(p_out, w_out[0], m_w_out[0], v_w_out[0], name="adamw_w_out"),
        "w_gate": _reduce_adamw(p_gate, w_gate[0], m_w_gate[0], v_w_gate[0], name="adamw_w_gate"),
        "w_up": _reduce_adamw(p_up, w_up[0], m_w_up[0], v_w_up[0], name="adamw_w_up"),
        "w_down": _reduce_adamw(p_down, w_down[0], m_w_down[0], v_w_down[0], name="adamw_w_down"),
    }
    small_upd = _reduce_adamw(p_small, w_small, m_small, v_small, name="adamw_small")

    def unpack(packed_out):
        vec = lambda i, n, shape: packed_out[i, :n].reshape(shape)
        return {
            "norm_mix_g": vec(0, d, (1, d)), "ssm_conv_b": vec(1, d_xbc, (1, d_xbc)),
            "ssm_dt_bias": vec(2, heads, (1, heads)), "ssm_A_log": vec(3, heads, (1, heads)),
            "ssm_D": vec(4, heads, (1, heads)), "ssm_norm_g": vec(5, d, (1, d)), "norm_ffn_g": vec(6, d, (1, d)),
            "norm_final_g": vec(7, d, (d,)),
            "ssm_conv_w": lax.dynamic_slice(packed_out[8:8 + K_SSM], (0, conv_lo), (K_SSM, d_xbc // N_DEV))[None],
            "sc_conv_w": lax.dynamic_slice(packed_out[8 + K_SSM:8 + K_SSM + K_SC], (0, sc_lo), (K_SC, d // N_DEV))[None],
        }

    names = ["norm_mix_g", "w_in", "ssm_conv_w", "ssm_conv_b", "ssm_dt_bias", "ssm_A_log", "ssm_D", "ssm_norm_g",
             "sc_conv_w", "w_out", "norm_ffn_g", "w_gate", "w_up", "w_down", "norm_final_g"]
    outs = []
    for kind in range(4):
        small_k = unpack(small_upd[kind])
        for nm in names:
            outs.append(upd[nm][kind][None] if nm in upd else small_k[nm])
    return (loss, dx[None], *outs)
```

```python
import collections
import functools

import jax
import jax.numpy as jnp
from jax import lax
from jax.experimental import pallas as pl
from jax.experimental.pallas import tpu as pltpu

F32 = jnp.float32
BF16 = jnp.bfloat16

N_DEV = 8
N_CHIPS = 4
HEADDIM = 64
N_GROUPS = 8
N_STATE = 128
CHUNK = 128
K_SSM = 4
K_SC = 3
EPS = 1e-5
LANES = 128
BF16_ROWS = 16
V7X_VMEM_BYTES = 64 * 1024 * 1024
VMEM_LIMIT = (V7X_VMEM_BYTES * 3) // 4

ADAM_LR = 0.001
ADAM_B1 = 0.9
ADAM_B2 = 0.999
ADAM_EPS = 1e-08
ADAM_WD = 0.01
ADAM_STEP = 10


def _tile(n, pref, align):
    t = min(pref, n)
    t -= t % align
    while t >= align:
        if n % t == 0:
            return t
        t -= align
    return n


_Ride = collections.namedtuple("_Ride", ["ins", "out_shapes", "aliases", "nsem", "plan"])
_ANY = pl.BlockSpec(memory_space=pl.ANY)


def _coords():
    return lax.axis_index("x"), lax.axis_index("y"), lax.axis_index("c")


def _other_chips(x, y):
    return ((1 - x, y), (x, 1 - y), (1 - x, 1 - y))


def _remote(src, dst, send, recv, k, dev):
    return pltpu.make_async_remote_copy(src_ref=src, dst_ref=dst, send_sem=send.at[k], recv_sem=recv.at[k],
                                        device_id=dev, device_id_type=pl.DeviceIdType.MESH)


def _start_all(plan):
    for kind, d in plan:
        if kind != "arrival":
            d.start()


def _wait_all(plan):
    for kind, d in plan:
        if kind == "local":
            d.wait()
        elif kind == "out":
            d.wait_send()
        else:
            d.wait_recv()


def _gather_chips(srcs):
    def plan(ins, outs, send, recv, base):
        x, y, c = _coords()
        me = 4 * x + 2 * y + c
        d = []
        for a, (src, dst) in enumerate(zip(ins, outs)):
            k = base + 4 * a
            d.append(("local", pltpu.make_async_copy(src, dst.at[me], send.at[k + 3])))
            for j, (px, py) in enumerate(_other_chips(x, y)):
                d.append(("out", _remote(src, dst.at[me], send, recv, k + j, (px, py, c))))
                d.append(("arrival", _remote(src, dst.at[4 * px + 2 * py + c], send, recv, k + j, (px, py, c))))
        return d
    shapes = [jax.ShapeDtypeStruct((N_DEV,) + s.shape, s.dtype) for s in srcs]
    return _Ride(list(srcs), shapes, {}, 4 * len(srcs), plan)


def _gather_sibling(bufs):
    def plan(ins, outs, send, recv, base):
        x, y, c = _coords()
        d = []
        for a, buf in enumerate(outs):
            for q in range(N_CHIPS):
                k = base + 4 * a + q
                d.append(("out", _remote(buf.at[2 * q + c], buf.at[2 * q + c], send, recv, k, (x, y, 1 - c))))
                d.append(("arrival", _remote(buf.at[2 * q + c], buf.at[2 * q + 1 - c], send, recv, k, (x, y, 1 - c))))
        return d
    shapes = [jax.ShapeDtypeStruct(b.shape, b.dtype) for b in bufs]
    return _Ride(list(bufs), shapes, {i: i for i in range(len(bufs))}, 4 * len(bufs), plan)


def _scatter_sibling(srcs):
    def plan(ins, outs, send, recv, base):
        x, y, c = _coords()
        d = []
        for a, (src, sib) in enumerate(zip(ins, outs)):
            for q in range(N_CHIPS):
                k = base + 4 * a + q
                d.append(("out", _remote(src.at[2 * q + 1 - c], sib.at[q], send, recv, k, (x, y, 1 - c))))
                d.append(("arrival", _remote(src.at[2 * q + 1 - c], sib.at[q], send, recv, k, (x, y, 1 - c))))
        return d
    shapes = [jax.ShapeDtypeStruct((N_CHIPS,) + s.shape[1:], s.dtype) for s in srcs]
    return _Ride(list(srcs), shapes, {}, 4 * len(srcs), plan)


def _scatter_chips(chips):
    def plan(ins, outs, send, recv, base):
        x, y, c = _coords()
        mine = 2 * x + y
        d = []
        for a, (chip, parts) in enumerate(zip(ins, outs)):
            k = base + 4 * a
            d.append(("local", pltpu.make_async_copy(chip.at[mine], parts.at[mine], send.at[k + 3])))
            for j, (px, py) in enumerate(_other_chips(x, y)):
                q = 2 * px + py
                d.append(("out", _remote(chip.at[q], parts.at[mine], send, recv, k + j, (px, py, c))))
                d.append(("arrival", _remote(chip.at[q], parts.at[q], send, recv, k + j, (px, py, c))))
        return d
    shapes = [jax.ShapeDtypeStruct(s.shape, s.dtype) for s in chips]
    return _Ride(list(chips), shapes, {}, 4 * len(chips), plan)


def _gather_all(srcs):
    def plan(ins, outs, send, recv, base):
        x, y, c = _coords()
        me = 4 * x + 2 * y + c
        d = []
        for a, (src, dst) in enumerate(zip(ins, outs)):
            k = base + N_DEV * a
            d.append(("local", pltpu.make_async_copy(src, dst.at[me], send.at[k])))
            for j in range(1, N_DEV):
                px = 1 - x if (j >> 2) & 1 else x
                py = 1 - y if (j >> 1) & 1 else y
                pc = 1 - c if j & 1 else c
                d.append(("out", _remote(src, dst.at[me], send, recv, k + j, (px, py, pc))))
                d.append(("arrival", _remote(src, dst.at[4 * px + 2 * py + pc], send, recv, k + j, (px, py, pc))))
        return d
    shapes = [jax.ShapeDtypeStruct((N_DEV,) + s.shape, s.dtype) for s in srcs]
    return _Ride(list(srcs), shapes, {}, N_DEV * len(srcs), plan)


def _merge(*rides):
    ins, outs, aliases, parts, nsem = [], [], {}, [], 0
    for r in rides:
        parts.append((len(ins), len(outs), nsem, r))
        aliases.update({len(ins) + i: len(outs) + j for i, j in r.aliases.items()})
        ins += r.ins
        outs += r.out_shapes
        nsem += r.nsem

    def plan(i, o, send, recv, base):
        d = []
        for i0, o0, s0, r in parts:
            d += r.plan(i[i0:i0 + len(r.ins)], o[o0:o0 + len(r.out_shapes)], send, recv, base + s0)
        return d
    return _Ride(ins, outs, aliases, nsem, plan)


def _comm(ride, name):
    n_in, n_out = len(ride.ins), len(ride.out_shapes)

    def body(*refs):
        plan = ride.plan(refs[:n_in], refs[n_in:n_in + n_out], refs[-2], refs[-1], 0)
        _start_all(plan)
        _wait_all(plan)

    return pl.pallas_call(
        body, name=name, in_specs=[_ANY] * n_in, out_specs=[_ANY] * n_out, out_shape=ride.out_shapes,
        scratch_shapes=[pltpu.SemaphoreType.DMA((ride.nsem,)), pltpu.SemaphoreType.DMA((ride.nsem,))],
        input_output_aliases=dict(ride.aliases),
        compiler_params=pltpu.CompilerParams(has_side_effects=True),
    )(*ride.ins)


def _call(body, *, name, grid, in_specs, out_specs, out_shape, args, sem, scratch=(), ride=None):
    params = pltpu.CompilerParams(dimension_semantics=sem, vmem_limit_bytes=VMEM_LIMIT)
    if ride is None:
        res = pl.pallas_call(body, name=name, grid=grid, in_specs=in_specs, out_specs=out_specs,
                             out_shape=out_shape, scratch_shapes=list(scratch), compiler_params=params)(*args)
        return list(res), []
    n_in, n_out, n_scr = len(args), len(out_shape), len(scratch)
    r_in, r_out = len(ride.ins), len(ride.out_shapes)

    def hosted(*refs):
        h_in, rin = refs[:n_in], refs[n_in:n_in + r_in]
        o0 = n_in + r_in
        h_out, rout = refs[o0:o0 + n_out], refs[o0 + n_out:o0 + n_out + r_out]
        s0 = o0 + n_out + r_out
        h_scr, send, recv = refs[s0:s0 + n_scr], refs[s0 + n_scr], refs[s0 + n_scr + 1]
        ids = [pl.program_id(i) for i in range(len(grid))]
        first = functools.reduce(lambda p, q: p & q, [i == 0 for i in ids])
        last = functools.reduce(lambda p, q: p & q, [i == n - 1 for i, n in zip(ids, grid)])

        @pl.when(first)
        def _():
            _start_all(ride.plan(rin, rout, send, recv, 0))

        body(*h_in, *h_out, *h_scr)

        @pl.when(last)
        def _():
            _wait_all(ride.plan(rin, rout, send, recv, 0))

    res = pl.pallas_call(
        hosted, name=name, grid=grid, in_specs=list(in_specs) + [_ANY] * r_in,
        out_specs=list(out_specs) + [_ANY] * r_out, out_shape=list(out_shape) + list(ride.out_shapes),
        scratch_shapes=list(scratch) + [pltpu.SemaphoreType.DMA((ride.nsem,)), pltpu.SemaphoreType.DMA((ride.nsem,))],
        input_output_aliases={n_in + i: n_out + j for i, j in ride.aliases.items()},
        compiler_params=params,
    )(*args, *ride.ins)
    return list(res[:n_out]), list(res[n_out:])


def _matmul(a, b, *, ta=False, tb=False, out_dtype=BF16, add=None, name, tm=1024, tn=1024, tk=512, ride=None):
    m = a.shape[1] if ta else a.shape[0]
    k = a.shape[0] if ta else a.shape[1]
    n = b.shape[0] if tb else b.shape[1]
    assert k == (b.shape[1] if tb else b.shape[0])
    tm, tn, tk = _tile(m, tm, LANES), _tile(n, tn, LANES), _tile(k, tk, LANES)
    nk = k // tk
    dims = (((0 if ta else 1,), (1 if tb else 0,)), ((), ()))

    def body(*refs):
        if add is None:
            a_ref, b_ref, o_ref, acc = refs
        else:
            a_ref, b_ref, add_ref, o_ref, acc = refs
        kk = pl.program_id(2)

        @pl.when(kk == 0)
        def _():
            acc[...] = jnp.zeros_like(acc)

        acc[...] += lax.dot_general(a_ref[...].astype(BF16), b_ref[...].astype(BF16), dims,
                                    preferred_element_type=F32)

        @pl.when(kk == nk - 1)
        def _():
            r = acc[...]
            if add is not None:
                r = r + add_ref[...].astype(F32)
            o_ref[...] = r.astype(o_ref.dtype)

    a_spec = (pl.BlockSpec((tk, tm), lambda i, j, kk: (kk, i)) if ta
              else pl.BlockSpec((tm, tk), lambda i, j, kk: (i, kk)))
    b_spec = (pl.BlockSpec((tn, tk), lambda i, j, kk: (j, kk)) if tb
              else pl.BlockSpec((tk, tn), lambda i, j, kk: (kk, j)))
    o_spec = pl.BlockSpec((tm, tn), lambda i, j, kk: (i, j))
    outs, rides = _call(
        body, name=name, grid=(m // tm, n // tn, nk),
        in_specs=[a_spec, b_spec] + ([o_spec] if add is not None else []), out_specs=[o_spec],
        out_shape=[jax.ShapeDtypeStruct((m, n), out_dtype)], args=(a, b) + ((add,) if add is not None else ()),
        scratch=[pltpu.VMEM((tm, tn), F32)], sem=("parallel", "parallel", "arbitrary"), ride=ride)
    return outs[0], rides


def _rows_call(fn, *, rows, tr, row_ins, full_ins, row_outs, acc_outs, name, ride=None):
    nr, nf, no, na = len(row_ins), len(full_ins), len(row_outs), len(acc_outs)

    def body(*refs):
        vals = [r[...] for r in refs[:nr + nf]]
        outs, accs = fn(*vals)
        for r, v in zip(refs[nr + nf:nr + nf + no], outs):
            r[...] = v.astype(r.dtype)
        if na:
            @pl.when(pl.program_id(0) == 0)
            def _():
                for r in refs[nr + nf + no:]:
                    r[...] = jnp.zeros_like(r)
            for r, v in zip(refs[nr + nf + no:], accs):
                r[...] += v

    in_specs = [pl.BlockSpec((tr, w), functools.partial(lambda cb, i: (i, cb), cb)) for _, w, cb in row_ins]
    in_specs += [pl.BlockSpec(f.shape, lambda i: (0, 0)) for f in full_ins]
    out_specs = [pl.BlockSpec((tr, w), lambda i: (i, 0)) for w, _ in row_outs]
    out_specs += [pl.BlockSpec(s, lambda i: (0, 0)) for s in acc_outs]
    out_shape = [jax.ShapeDtypeStruct((rows, w), dt) for w, dt in row_outs]
    out_shape += [jax.ShapeDtypeStruct(s, F32) for s in acc_outs]
    return _call(body, name=name, grid=(rows // tr,), in_specs=in_specs, out_specs=out_specs, out_shape=out_shape,
                 args=tuple(a for a, _, _ in row_ins) + tuple(full_ins), sem=("arbitrary",), ride=ride)


def _cols_call(fn, *, rows, cols, cw, col_ins, par_ins, col_outs, par_outs, name, ride=None):
    nc, npar = len(col_ins), len(par_ins)

    def body(*refs):
        vals = [r[...] for r in refs[:nc + npar]]
        outs, pouts = fn(*vals)
        for r, v in zip(refs[nc + npar:], tuple(outs) + tuple(pouts)):
            r[...] = v.astype(r.dtype)

    in_specs = [pl.BlockSpec((rows, cw), functools.partial(lambda off, j: (0, off + j), off)) for _, off in col_ins]
    in_specs += [pl.BlockSpec((p.shape[0], cw), functools.partial(lambda off, j: (0, off + j), off))
                 for p, off in par_ins]
    out_specs = [pl.BlockSpec((rows, cw), lambda j: (0, j)) for _ in col_outs]
    out_specs += [pl.BlockSpec((k, cw), lambda j: (0, j)) for k in par_outs]
    out_shape = [jax.ShapeDtypeStruct((rows, cols), dt) for dt in col_outs]
    out_shape += [jax.ShapeDtypeStruct((k, cols), F32) for k in par_outs]
    return _call(body, name=name, grid=(cols // cw,), in_specs=in_specs, out_specs=out_specs, out_shape=out_shape,
                 args=tuple(a for a, _ in col_ins) + tuple(p for p, _ in par_ins), sem=("arbitrary",), ride=ride)


def _sigmoid(v):
    return 1.0 / (1.0 + jnp.exp(-v))


def _softplus(v):
    return jnp.maximum(v, 0.0) + jnp.log(1.0 + jnp.exp(-jnp.abs(v)))


def _rms(v, g):
    return v * lax.rsqrt(jnp.mean(v * v, axis=-1, keepdims=True) + EPS) * g


def _shift_down(v, s, row):
    return jnp.where(row >= s, pltpu.roll(v, s, 0), 0.0)


def _shift_up(v, s, row):
    n = v.shape[0]
    return jnp.where(row < n - s, pltpu.roll(v, n - s, 0), 0.0)


def _causal_conv(u, w, row):
    k_taps = w.shape[0]
    acc = u * w[k_taps - 1:k_taps, :]
    for k in range(k_taps - 1):
        acc = acc + _shift_down(u, k_taps - 1 - k, row) * w[k:k + 1, :]
    return acc


def _causal_conv_bwd(u, dy, w, row):
    k_taps = w.shape[0]
    tap = lax.broadcasted_iota(jnp.int32, w.shape, 0)
    du = dy * w[k_taps - 1:k_taps, :]
    dw = jnp.where(tap == k_taps - 1, jnp.sum(dy * u, axis=0, keepdims=True), 0.0)
    for k in range(k_taps - 1):
        s = k_taps - 1 - k
        du = du + _shift_up(dy, s, row) * w[k:k + 1, :]
        dw = dw + jnp.where(tap == k, jnp.sum(dy * _shift_down(u, s, row), axis=0, keepdims=True), 0.0)
    return du, dw


def _conv_silu_fwd(u, w, b):
    u = u.astype(F32)
    row = lax.broadcasted_iota(jnp.int32, u.shape, 0)
    pre = _causal_conv(u, w, row) + b
    return (pre * _sigmoid(pre),), ()


def _conv_silu_bwd(u, dy, w, b):
    u = u.astype(F32)
    dy = dy.astype(F32)
    row = lax.broadcasted_iota(jnp.int32, u.shape, 0)
    pre = _causal_conv(u, w, row) + b
    s = _sigmoid(pre)
    dpre = dy * (s * (1.0 + pre * (1.0 - s)))
    du, dw = _causal_conv_bwd(u, dpre, w, row)
    return (du,), (dw, jnp.sum(dpre, axis=0, keepdims=True))


def _shortconv_fwd(gb, gc, u, w):
    gb, gc, u = gb.astype(F32), gc.astype(F32), u.astype(F32)
    row = lax.broadcasted_iota(jnp.int32, u.shape, 0)
    return (gb * _causal_conv(gc * u, w, row),), ()


def _shortconv_bwd(gb, gc, u, dy, w):
    gb, gc, u, dy = gb.astype(F32), gc.astype(F32), u.astype(F32), dy.astype(F32)
    row = lax.broadcasted_iota(jnp.int32, u.shape, 0)
    v = gc * u
    dgb = dy * _causal_conv(v, w, row)
    dv, dw = _causal_conv_bwd(v, dy * gb, w, row)
    return (dgb, dv * u, dv * gc), (dw,)


def _split3(v):
    hi = v.astype(BF16)
    r1 = v - hi.astype(F32)
    mid = r1.astype(BF16)
    lo = (r1 - mid.astype(F32)).astype(BF16)
    return hi, mid, lo


def _exact_dot(v, m01, dims, v_is_lhs):
    def one(p):
        return (lax.dot_general(p, m01, dims, preferred_element_type=F32) if v_is_lhs
                else lax.dot_general(m01, p, dims, preferred_element_type=F32))
    hi, mid, lo = _split3(v)
    return (one(lo) + one(mid)) + one(hi)


_NN = (((1,), (0,)), ((), ()))
_NT = (((1,), (1,)), ((), ()))
_TN = (((0,), (0,)), ((), ()))


@jax.custom_vjp
def _cumsum_rows(tril, v):
    return _exact_dot(v, tril, _NN, False)


def _cumsum_rows_fwd(tril, v):
    return _cumsum_rows(tril, v), tril


def _cumsum_rows_bwd(tril, ct):
    return None, _exact_dot(ct, tril, _TN, False)


_cumsum_rows.defvjp(_cumsum_rows_fwd, _cumsum_rows_bwd)


@jax.custom_vjp
def _cumsum_lanes(tril, v):
    return _exact_dot(v, tril, _NT, True)


def _cumsum_lanes_fwd(tril, v):
    return _cumsum_lanes(tril, v), tril


def _cumsum_lanes_bwd(tril, ct):
    return None, _exact_dot(ct, tril, _NN, True)


_cumsum_lanes.defvjp(_cumsum_lanes_fwd, _cumsum_lanes_bwd)


def _ssd_chunk(g, r_heads, xs, bg, cg, dtc, dtr, bias_r, bias_c, alog_r, alog_c, dskip, hp):
    l_len, rp = xs.shape
    p = rp // r_heads
    dt_c = _softplus(dtc + bias_r)
    dt_r = _softplus(dtr + bias_c)
    da_c = dt_c * (-jnp.exp(alog_r))
    da_r = dt_r * (-jnp.exp(alog_c))
    li = lax.broadcasted_iota(jnp.int32, (l_len, l_len), 0)
    si = lax.broadcasted_iota(jnp.int32, (l_len, l_len), 1)
    causal = si <= li
    tril = jnp.where(causal, 1.0, 0.0).astype(BF16)
    cs_c = _cumsum_rows(tril, da_c)
    cs_r = _cumsum_lanes(tril, da_r)
    head_lane = lax.broadcasted_iota(jnp.int32, (1, dtc.shape[1]), 1)
    head_sub = lax.broadcasted_iota(jnp.int32, (dtr.shape[0], 1), 0)
    lane_head = lax.broadcasted_iota(jnp.int32, (1, rp), 1) // p

    def col(v, r):
        return jnp.sum(jnp.where(head_lane == g * r_heads + r, v, 0.0), axis=1, keepdims=True)

    def expand(v):
        out = jnp.where(lane_head == 0, col(v, 0), 0.0)
        for r in range(1, r_heads):
            out = out + jnp.where(lane_head == r, col(v, r), 0.0)
        return out

    dt_e = expand(dt_c)
    cs_e = expand(cs_c)
    cl_e = expand(cs_c[l_len - 1:l_len, :])
    x = xs * dt_e
    bgb, cgb = bg.astype(BF16), cg.astype(BF16)
    cb = lax.dot_general(cgb, bgb, _NT, preferred_element_type=F32)
    ms, xm = [], []
    for r in range(r_heads):
        row = jnp.sum(jnp.where(head_sub == g * r_heads + r, cs_r, 0.0), axis=0, keepdims=True)
        seg = col(cs_c, r) - row
        decay = jnp.exp(jnp.where(causal, seg, -1e30))
        ms.append((cb * decay).astype(BF16))
        xm.append(jnp.where(lane_head == r, x, 0.0).astype(BF16))
    y_diag = lax.dot_general(jnp.concatenate(ms, axis=1), jnp.concatenate(xm, axis=0), _NN,
                             preferred_element_type=F32)
    y_off = lax.dot_general(cgb, hp.astype(BF16), _NN, preferred_element_type=F32) * jnp.exp(cs_e)
    xd = (x * jnp.exp(cl_e - cs_e)).astype(BF16)
    states = lax.dot_general(bgb, xd, _TN, preferred_element_type=F32)
    h_next = hp * jnp.exp(cl_e) + states
    y = y_diag + y_off + expand(dskip) * xs
    return y, h_next


def _ssd_specs(t_len, d_ssm, r_heads, reverse):
    rp = r_heads * HEADDIM
    nc = t_len // CHUNK
    cidx = (lambda c: nc - 1 - c) if reverse else (lambda c: c)
    b_off = d_ssm // N_STATE
    specs = dict(
        xs=pl.BlockSpec((CHUNK, rp), lambda c, g: (cidx(c), g)),
        b=pl.BlockSpec((CHUNK, N_STATE), lambda c, g: (cidx(c), b_off + g)),
        c=pl.BlockSpec((CHUNK, N_STATE), lambda c, g: (cidx(c), b_off + N_GROUPS + g)),
        dtc=pl.BlockSpec((CHUNK, LANES), lambda c, g: (cidx(c), 0)),
        dtr=lambda h: pl.BlockSpec((h, CHUNK), lambda c, g: (0, cidx(c))),
        full=lambda shape: pl.BlockSpec(shape, lambda c, g: (0, 0)),
        hprev=pl.BlockSpec((None, None, N_STATE, rp), lambda c, g: (cidx(c), g, 0, 0)),
    )
    return specs, nc, rp


def _ssd_fwd(xbc, dtc, dtr, small, *, d_ssm, r_heads, ride=None):
    t_len = xbc.shape[0]
    sp, nc, rp = _ssd_specs(t_len, d_ssm, r_heads, False)

    def body(xs_ref, b_ref, c_ref, dtc_ref, dtr_ref, br, bc, ar, ac, dk, y_ref, hprev_ref, h_ref):
        c, g = pl.program_id(0), pl.program_id(1)

        @pl.when(c == 0)
        def _():
            h_ref[g] = jnp.zeros((N_STATE, rp), F32)

        hp = h_ref[g]
        hprev_ref[...] = hp
        y, hn = _ssd_chunk(g, r_heads, xs_ref[...].astype(F32), b_ref[...].astype(F32), c_ref[...].astype(F32),
                           dtc_ref[...], dtr_ref[...], br[...], bc[...], ar[...], ac[...], dk[...], hp)
        y_ref[...] = y
        h_ref[g] = hn

    in_specs = [sp["xs"], sp["b"], sp["c"], sp["dtc"], sp["dtr"](dtr.shape[0])]
    in_specs += [sp["full"](s.shape) for s in small]
    return _call(
        body, name="ssd_fwd", grid=(nc, N_GROUPS), in_specs=in_specs,
        out_specs=[pl.BlockSpec((CHUNK, rp), lambda c, g: (c, g)), sp["hprev"]],
        out_shape=[jax.ShapeDtypeStruct((t_len, d_ssm), F32),
                   jax.ShapeDtypeStruct((nc, N_GROUPS, N_STATE, rp), F32)],
        args=(xbc, xbc, xbc, dtc, dtr, *small), scratch=[pltpu.VMEM((N_GROUPS, N_STATE, rp), F32)],
        sem=("arbitrary", "arbitrary"), ride=ride)


def _ssd_bwd(xbc, dtc, dtr, small, hprev, dy, *, d_ssm, r_heads, ride=None):
    t_len = xbc.shape[0]
    sp, nc, rp = _ssd_specs(t_len, d_ssm, r_heads, True)
    n_small = len(small)

    def body(*refs):
        xs_ref, b_ref, c_ref, dtc_ref, dtr_ref = refs[:5]
        small_refs = refs[5:5 + n_small]
        hprev_ref, dy_ref = refs[5 + n_small:7 + n_small]
        dxs_ref, db_ref, dc_ref, ddtc_ref, ddtr_ref = refs[7 + n_small:12 + n_small]
        dsmall_refs = refs[12 + n_small:12 + 2 * n_small]
        dh_ref = refs[12 + 2 * n_small]
        c, g = pl.program_id(0), pl.program_id(1)

        @pl.when(c == 0)
        def _():
            dh_ref[g] = jnp.zeros((N_STATE, rp), F32)

        @pl.when(g == 0)
        def _():
            ddtc_ref[...] = jnp.zeros_like(ddtc_ref)
            ddtr_ref[...] = jnp.zeros_like(ddtr_ref)

        @pl.when((c == 0) & (g == 0))
        def _():
            for r in dsmall_refs:
                r[...] = jnp.zeros_like(r)

        args = (xs_ref[...].astype(F32), b_ref[...].astype(F32), c_ref[...].astype(F32), dtc_ref[...], dtr_ref[...],
                *[r[...] for r in small_refs], hprev_ref[...])
        _, vjp = jax.vjp(functools.partial(_ssd_chunk, g, r_heads), *args)
        grads = vjp((dy_ref[...], dh_ref[g]))
        dxs_ref[...] = grads[0].astype(dxs_ref.dtype)
        db_ref[...] = grads[1].astype(db_ref.dtype)
        dc_ref[...] = grads[2].astype(dc_ref.dtype)
        ddtc_ref[...] += grads[3]
        ddtr_ref[...] += grads[4]
        for r, gr in zip(dsmall_refs, grads[5:5 + n_small]):
            r[...] += gr
        dh_ref[g] = grads[5 + n_small]

    rev = lambda c: nc - 1 - c
    in_specs = [sp["xs"], sp["b"], sp["c"], sp["dtc"], sp["dtr"](dtr.shape[0])]
    in_specs += [sp["full"](s.shape) for s in small]
    in_specs += [sp["hprev"], pl.BlockSpec((CHUNK, rp), lambda c, g: (rev(c), g))]
    out_specs = [pl.BlockSpec((CHUNK, rp), lambda c, g: (rev(c), g)),
                 pl.BlockSpec((CHUNK, N_STATE), lambda c, g: (rev(c), g)),
                 pl.BlockSpec((CHUNK, N_STATE), lambda c, g: (rev(c), g)),
                 sp["dtc"], sp["dtr"](dtr.shape[0])]
    out_specs += [sp["full"](s.shape) for s in small]
    out_shape = [jax.ShapeDtypeStruct((t_len, d_ssm), BF16),
                 jax.ShapeDtypeStruct((t_len, N_GROUPS * N_STATE), BF16),
                 jax.ShapeDtypeStruct((t_len, N_GROUPS * N_STATE), BF16),
                 jax.ShapeDtypeStruct(dtc.shape, F32), jax.ShapeDtypeStruct(dtr.shape, F32)]
    out_shape += [jax.ShapeDtypeStruct(s.shape, F32) for s in small]
    return _call(
        body, name="ssd_bwd", grid=(nc, N_GROUPS), in_specs=in_specs, out_specs=out_specs, out_shape=out_shape,
        args=(xbc, xbc, xbc, dtc, dtr, *small, hprev, dy), scratch=[pltpu.VMEM((N_GROUPS, N_STATE, rp), F32)],
        sem=("arbitrary", "arbitrary"), ride=ride)


def _chip_sum(src, sib, *, name):
    rows, cols = src.shape[1:]
    tr = _tile(rows, 256, BF16_ROWS)
    core = lax.axis_index("c").astype(jnp.int32).reshape(1)

    def body(c_ref, a_ref, b_ref, o_ref):
        o_ref[...] = (a_ref[...].astype(F32) + b_ref[...].astype(F32)).astype(o_ref.dtype)

    grid_spec = pltpu.PrefetchScalarGridSpec(
        num_scalar_prefetch=1, grid=(N_CHIPS, rows // tr),
        in_specs=[pl.BlockSpec((None, tr, cols), lambda q, i, c_ref: (2 * q + c_ref[0], i, 0)),
                  pl.BlockSpec((None, tr, cols), lambda q, i, c_ref: (q, i, 0))],
        out_specs=pl.BlockSpec((None, tr, cols), lambda q, i, c_ref: (q, i, 0)))
    return pl.pallas_call(
        body, name=name, grid_spec=grid_spec, out_shape=jax.ShapeDtypeStruct(sib.shape, sib.dtype),
        compiler_params=pltpu.CompilerParams(dimension_semantics=("parallel", "parallel"), vmem_limit_bytes=VMEM_LIMIT),
    )(core, src, sib)


def _adamw(w, g, m, v):
    m = ADAM_B1 * m + (1.0 - ADAM_B1) * g
    v = ADAM_B2 * v + (1.0 - ADAM_B2) * (g * g)
    m_hat = m / (1.0 - ADAM_B1 ** ADAM_STEP)
    v_hat = v / (1.0 - ADAM_B2 ** ADAM_STEP)
    delta = -ADAM_LR * (m_hat / (jnp.sqrt(v_hat) + ADAM_EPS) + ADAM_WD * w)
    return delta, m, v


def _reduce_adamw(parts, w, m, v, *, name):
    n_parts = parts.shape[0]
    rows, cols = w.shape
    tr = _tile(rows, 128, BF16_ROWS)

    def body(p_ref, w_ref, m_ref, v_ref, g_ref, d_ref, mo_ref, vo_ref):
        g = p_ref[0].astype(F32)
        for k in range(1, n_parts):
            g = g + p_ref[k].astype(F32)
        delta, mn, vn = _adamw(w_ref[...], g, m_ref[...], v_ref[...])
        g_ref[...] = g
        d_ref[...] = delta
        mo_ref[...] = mn
        vo_ref[...] = vn

    spec = pl.BlockSpec((tr, cols), lambda i: (i, 0))
    outs, _ = _call(
        body, name=name, grid=(rows // tr,),
        in_specs=[pl.BlockSpec((n_parts, tr, cols), lambda i: (0, i, 0)), spec, spec, spec],
        out_specs=[spec] * 4, out_shape=[jax.ShapeDtypeStruct((rows, cols), F32)] * 4,
        args=(parts, w, m, v), sem=("parallel",))
    return outs


def _cols_of(g):
    return jnp.transpose(g, (1, 0, 2)).reshape(g.shape[1], -1)


def _pad_to(a, rows, cols):
    return jnp.pad(a, ((0, rows - a.shape[0]), (0, cols - a.shape[1])))


def kernel(x, norm_mix_g, w_in, ssm_conv_w, ssm_conv_b, ssm_dt_bias, ssm_A_log, ssm_D, ssm_norm_g, sc_conv_w, w_out, norm_ffn_g, w_gate, w_up, w_down, norm_final_g, loss_target, m_norm_mix_g, m_w_in, m_ssm_conv_w, m_ssm_conv_b, m_ssm_dt_bias, m_ssm_A_log, m_ssm_D, m_ssm_norm_g, m_sc_conv_w, m_w_out, m_norm_ffn_g, m_w_gate, m_w_up, m_w_down, m_norm_final_g, v_norm_mix_g, v_w_in, v_ssm_conv_w, v_ssm_conv_b, v_ssm_dt_bias, v_ssm_A_log, v_ssm_D, v_ssm_norm_g, v_sc_conv_w, v_w_out, v_norm_ffn_g, v_w_gate, v_w_up, v_w_down, v_norm_final_g):
    t_len, d = x.shape[1], x.shape[2]
    heads = d // HEADDIM
    r_heads = heads // N_GROUPS
    d_xbc = d + 2 * N_GROUPS * N_STATE
    ff_s = w_down.shape[1]
    ff = ff_s * N_DEV
    off_xbc, off_dt = d, d + d_xbc
    off_cb = off_dt + heads
    d_in = off_cb + 3 * d
    in_s = d_in // N_DEV
    in_p = -(-in_s // (2 * BF16_ROWS)) * (2 * BF16_ROWS)
    w_main = 4 * d + d_xbc
    me = 4 * lax.axis_index("x") + 2 * lax.axis_index("y") + lax.axis_index("c")

    x2 = x[0]
    target = loss_target[0]

    tpose = lambda a: jnp.transpose(a[0])
    win_s = _pad_to(tpose(w_in).astype(BF16), in_p, d)
    wg_s, wu_s = tpose(w_gate).astype(BF16), tpose(w_up).astype(BF16)
    wo_s, wd_s = w_out[0].astype(BF16), w_down[0].astype(BF16)
    small_w = jnp.concatenate([_pad_to(ssm_conv_w[0], K_SSM, d_xbc // N_DEV),
                               _pad_to(sc_conv_w[0], K_SC + 1, d_xbc // N_DEV)], axis=0)

    g1, g2, g3 = norm_mix_g, norm_ffn_g, norm_final_g.reshape(1, d)
    gs = ssm_norm_g
    small = [_pad_to(ssm_dt_bias, 1, LANES), ssm_dt_bias.reshape(heads, 1), _pad_to(ssm_A_log, 1, LANES),
             ssm_A_log.reshape(heads, 1), _pad_to(ssm_D, 1, LANES)]
    tr = _tile(t_len, 256, 8)
    tr_ff = _tile(t_len, 128, 8)
    cw = LANES
    slab = lambda col: col // cw

    gin_1, gsm_1 = _comm(_gather_chips([win_s, small_w]), "gather_w_in_chips")
    (n1,), (gin, gsm) = _rows_call(lambda v, g: ((_rms(v, g),), ()), rows=t_len, tr=tr, row_ins=[(x2, d, 0)],
                                   full_ins=[g1], row_outs=[(d, BF16)], acc_outs=[], name="norm_mix",
                                   ride=_gather_sibling([gin_1, gsm_1]))
    win_all = gin[:, :in_s].reshape(d_in, d)
    wtm = jnp.concatenate([win_all[:off_dt], win_all[off_cb:]], axis=0)
    wtdt = _pad_to(win_all[off_dt:off_cb], LANES, d)
    cw_ssm = _cols_of(gsm[:, :K_SSM, :])
    cw_sc = _cols_of(gsm[:, K_SSM:K_SSM + K_SC, :d // N_DEV])

    proj, (gg_1, gu_1, go_1) = _matmul(n1, wtm, tb=True, out_dtype=BF16, name="proj_main",
                                       ride=_gather_chips([wg_s, wu_s, wo_s]))
    dt_raw, _ = _matmul(n1, wtdt, tb=True, out_dtype=F32, name="proj_dt")
    dt_raw_t = jnp.transpose(dt_raw[:, :heads])
    (xbc,), (gg, gu, go) = _cols_call(_conv_silu_fwd, rows=t_len, cols=d_xbc, cw=cw, col_ins=[(proj, slab(off_xbc))],
                                      par_ins=[(cw_ssm, 0), (ssm_conv_b, 0)], col_outs=[BF16], par_outs=[],
                                      name="ssm_conv", ride=_gather_sibling([gg_1, gu_1, go_1]))
    (y_ssd, hprev), (gd_1,) = _ssd_fwd(xbc, dt_raw, dt_raw_t, small, d_ssm=d, r_heads=r_heads,
                                       ride=_gather_chips([wd_s]))

    def gate_norm(y, z, g):
        z = z.astype(F32)
        return _rms(y * (z * _sigmoid(z)), g)

    (y_ssm,), (gd,) = _rows_call(lambda y, z, g: ((gate_norm(y, z, g),), ()), rows=t_len, tr=tr,
                                 row_ins=[(y_ssd, d, 0), (proj, d, 0)], full_ins=[gs], row_outs=[(d, BF16)],
                                 acc_outs=[], name="ssm_gate_norm", ride=_gather_sibling([gd_1]))
    wgt, wut = gg.reshape(ff, d), gu.reshape(ff, d)
    wo, wd = go.reshape(2 * d, d), gd.reshape(ff, d)
    sc0 = slab(d + d_xbc)
    (y_sc,), _ = _cols_call(_shortconv_fwd, rows=t_len, cols=d, cw=cw,
                            col_ins=[(proj, sc0), (proj, sc0 + slab(d)), (proj, sc0 + 2 * slab(d))],
                            par_ins=[(cw_sc, 0)], col_outs=[BF16], par_outs=[], name="shortconv")
    y_mix = jnp.concatenate([y_ssm, y_sc], axis=1)
    h1, _ = _matmul(y_mix, wo, out_dtype=F32, add=x2, name="out_proj")
    (n2,), _ = _rows_call(lambda v, g: ((_rms(v, g),), ()), rows=t_len, tr=tr, row_ins=[(h1, d, 0)], full_ins=[g2],
                          row_outs=[(d, BF16)], acc_outs=[], name="norm_ffn")
    g_ff, _ = _matmul(n2, wgt, tb=True, out_dtype=BF16, name="ffn_gate")
    u_ff, _ = _matmul(n2, wut, tb=True, out_dtype=BF16, name="ffn_up")

    def act(gv, uv):
        gv, uv = gv.astype(F32), uv.astype(F32)
        return ((gv * _sigmoid(gv) * uv,), ())

    (a_ff,), _ = _rows_call(act, rows=t_len, tr=tr_ff, row_ins=[(g_ff, ff, 0), (u_ff, ff, 0)], full_ins=[],
                            row_outs=[(ff, BF16)], acc_outs=[], name="ffn_act")
    h2, _ = _matmul(a_ff, wd, out_dtype=F32, add=h1, name="ffn_down")

    def head(hv, tv, g):
        def f(hh, gg_):
            e = _rms(hh, gg_) - tv
            return (0.5 / d) * jnp.sum(e * e)
        val, (dh, dg) = jax.value_and_grad(f, argnums=(0, 1))(hv, g)
        return (dh,), (jnp.full((1, LANES), val, F32), dg)

    (dh2, loss_acc, dg3), _ = _rows_call(head, rows=t_len, tr=tr, row_ins=[(h2, d, 0), (target, d, 0)], full_ins=[g3],
                                         row_outs=[(d, F32)], acc_outs=[(1, LANES), (1, d)], name="loss_head")
    loss = lax.psum(loss_acc[0, 0], ("x", "y", "c"))

    da, _ = _matmul(dh2, wd, tb=True, out_dtype=BF16, name="d_ffn_act")
    dwd, _ = _matmul(a_ff, dh2, ta=True, out_dtype=BF16, name="d_w_down")
    dwd8 = dwd.reshape(N_DEV, ff_s, d)

    def act_bwd(dav, gv, uv):
        dav, gv, uv = dav.astype(F32), gv.astype(F32), uv.astype(F32)
        s = _sigmoid(gv)
        return ((dav * uv * (s * (1.0 + gv * (1.0 - s))), dav * gv * s), ())

    (dg_ff, du_ff), (sib_d,) = _rows_call(act_bwd, rows=t_len, tr=tr_ff,
                                          row_ins=[(da, ff, 0), (g_ff, ff, 0), (u_ff, ff, 0)], full_ins=[],
                                          row_outs=[(ff, BF16), (ff, BF16)], acc_outs=[], name="d_ffn_gate_up",
                                          ride=_scatter_sibling([dwd8]))
    chip_d = _chip_sum(dwd8, sib_d, name="chip_sum_w_down")
    dn2, (parts_d,) = _matmul(dg_ff, wgt, out_dtype=F32, name="d_norm_ffn_out_gate", ride=_scatter_chips([chip_d]))
    dn2, _ = _matmul(du_ff, wut, out_dtype=F32, add=dn2, name="d_norm_ffn_out_up")
    dwg, _ = _matmul(dg_ff, n2, ta=True, out_dtype=BF16, name="d_w_gate")
    dwu, _ = _matmul(du_ff, n2, ta=True, out_dtype=BF16, name="d_w_up")
    dwg8, dwu8 = dwg.reshape(N_DEV, ff_s, d), dwu.reshape(N_DEV, ff_s, d)

    def norm_bwd(v, dn, dres, g):
        _, vjp = jax.vjp(_rms, v, g)
        dv, dg = vjp(dn)
        return (dv + dres,), (dg,)

    (dh1, dg2), (sib_g, sib_u) = _rows_call(norm_bwd, rows=t_len, tr=tr,
                                            row_ins=[(h1, d, 0), (dn2, d, 0), (dh2, d, 0)], full_ins=[g2],
                                            row_outs=[(d, F32)], acc_outs=[(1, d)], name="d_norm_ffn",
                                            ride=_scatter_sibling([dwg8, dwu8]))
    chip_g = _chip_sum(dwg8, sib_g, name="chip_sum_w_gate")
    chip_u = _chip_sum(dwu8, sib_u, name="chip_sum_w_up")

    dy_mix, _ = _matmul(dh1, wo, tb=True, out_dtype=BF16, name="d_y_mix")
    dwo, _ = _matmul(y_mix, dh1, ta=True, out_dtype=BF16, name="d_w_out")
    dwo8 = dwo.reshape(N_DEV, 2 * d // N_DEV, d)
    (dgb, dgc, du, dcw_sc), (sib_o,) = _cols_call(
        _shortconv_bwd, rows=t_len, cols=d, cw=cw,
        col_ins=[(proj, sc0), (proj, sc0 + slab(d)), (proj, sc0 + 2 * slab(d)), (dy_mix, slab(d))],
        par_ins=[(cw_sc, 0)], col_outs=[BF16] * 3, par_outs=[K_SC], name="d_shortconv",
        ride=_scatter_sibling([dwo8]))
    chip_o = _chip_sum(dwo8, sib_o, name="chip_sum_w_out")

    def gate_norm_bwd(y, z, dyo, g):
        _, vjp = jax.vjp(gate_norm, y, z.astype(F32), g)
        dy, dz, dg = vjp(dyo.astype(F32))
        return (dy, dz), (dg,)

    (dy_ssd, dz, dgs), _ = _rows_call(gate_norm_bwd, rows=t_len, tr=tr,
                                      row_ins=[(y_ssd, d, 0), (proj, d, 0), (dy_mix, d, 0)], full_ins=[gs],
                                      row_outs=[(d, F32), (d, BF16)], acc_outs=[(1, d)], name="d_ssm_gate_norm")
    ssd_grads, (parts_g, parts_u, parts_o) = _ssd_bwd(xbc, dt_raw, dt_raw_t, small, hprev, dy_ssd, d_ssm=d,
                                                      r_heads=r_heads, ride=_scatter_chips([chip_g, chip_u, chip_o]))
    dxs, dbm, dcm, ddt_c, ddt_r = ssd_grads[:5]
    dbias_r, dbias_c, dalog_r, dalog_c, ddskip = ssd_grads[5:]
    dxbc = jnp.concatenate([dxs, dbm, dcm], axis=1)
    (dxbc_pre, dcw_ssm, dcb_ssm), _ = _cols_call(
        _conv_silu_bwd, rows=t_len, cols=d_xbc, cw=cw, col_ins=[(proj, slab(off_xbc)), (dxbc, 0)],
        par_ins=[(cw_ssm, 0), (ssm_conv_b, 0)], col_outs=[BF16], par_outs=[K_SSM, 1], name="d_ssm_conv")
    dproj = jnp.concatenate([dz, dxbc_pre, dgb, dgc, du], axis=1)
    ddt = ddt_c + _pad_to(jnp.transpose(ddt_r), t_len, LANES)
    dwm, _ = _matmul(dproj, n1, ta=True, out_dtype=BF16, name="d_w_in_main")
    dwdt, _ = _matmul(ddt, n1, ta=True, out_dtype=BF16, name="d_w_in_dt")
    dwin_all = jnp.concatenate([dwm[:off_dt], dwdt[:heads], dwm[off_dt:]], axis=0)
    dwin8 = jnp.pad(dwin_all.reshape(N_DEV, in_s, d), ((0, 0), (0, in_p - in_s), (0, 0)))
    dn1, (sib_in,) = _matmul(ddt, wtdt, out_dtype=F32, name="d_norm_mix_out_dt", ride=_scatter_sibling([dwin8]))
    chip_in = _chip_sum(dwin8, sib_in, name="chip_sum_w_in")
    dn1, (parts_in,) = _matmul(dproj, wtm, out_dtype=F32, add=dn1, name="d_norm_mix_out",
                               ride=_scatter_chips([chip_in]))
    (dx, dg1), _ = _rows_call(norm_bwd, rows=t_len, tr=tr, row_ins=[(x2, d, 0), (dn1, d, 0), (dh1, d, 0)],
                              full_ins=[g1], row_outs=[(d, F32)], acc_outs=[(1, d)], name="d_norm_mix")

    wide = d_xbc
    rows_small = [dg1, dcb_ssm, dbias_r + _pad_to(dbias_c.reshape(1, heads), 1, LANES),
                  dalog_r + _pad_to(dalog_c.reshape(1, heads), 1, LANES), ddskip, dgs, dg2, dg3]
    packed = jnp.concatenate([_pad_to(r, 1, wide) for r in rows_small]
                             + [dcw_ssm, _pad_to(dcw_sc, K_SC, wide), jnp.zeros((1, wide), F32)], axis=0)
    (p_small,) = _comm(_gather_all([packed]), "gather_small_grads")

    conv_lo = me * (d_xbc // N_DEV)
    sc_lo = me * (d // N_DEV)

    def pack_state(vals):
        (nm, cb, dtb, al, dk, sg, nf, nfin, cws, scs) = vals
        rows = [_pad_to(a.reshape(1, -1), 1, wide) for a in (nm, cb, dtb, al, dk, sg, nf, nfin)]
        cws_full = lax.dynamic_update_slice(jnp.zeros((K_SSM, wide), F32), cws[0], (0, conv_lo))
        scs_full = lax.dynamic_update_slice(jnp.zeros((K_SC, wide), F32), scs[0], (0, sc_lo))
        return jnp.concatenate(rows + [cws_full, scs_full, jnp.zeros((1, wide), F32)], axis=0)

    w_small = pack_state((norm_mix_g, ssm_conv_b, ssm_dt_bias, ssm_A_log, ssm_D, ssm_norm_g, norm_ffn_g, norm_final_g,
                          ssm_conv_w, sc_conv_w))
    m_small = pack_state((m_norm_mix_g, m_ssm_conv_b, m_ssm_dt_bias, m_ssm_A_log, m_ssm_D, m_ssm_norm_g, m_norm_ffn_g,
                          m_norm_final_g, m_ssm_conv_w, m_sc_conv_w))
    v_small = pack_state((v_norm_mix_g, v_ssm_conv_b, v_ssm_dt_bias, v_ssm_A_log, v_ssm_D, v_ssm_norm_g, v_norm_ffn_g,
                          v_norm_final_g, v_ssm_conv_w, v_sc_conv_w))

    tin = lambda a: _pad_to(tpose(a), in_p, d)
    tin_back = lambda a: jnp.transpose(a[:in_s])[None]
    t_back = lambda a: jnp.transpose(a)[None]
    upd = {
        "w_in": [tin_back(o) for o in _reduce_adamw(parts_in, tin(w_in), tin(m_w_in), tin(v_w_in), name="adamw_w_in")],
        "w_out": [o[None] for o in _reduce_adamw(parts_o, w_out[0], m_w_out[0], v_w_out[0], name="adamw_w_out")],
        "w_gate": [t_back(o) for o in _reduce_adamw(parts_g, tpose(w_gate), tpose(m_w_gate), tpose(v_w_gate),
                                                    name="adamw_w_gate")],
        "w_up": [t_back(o) for o in _reduce_adamw(parts_u, tpose(w_up), tpose(m_w_up), tpose(v_w_up),
                                                  name="adamw_w_up")],
        "w_down": [o[None] for o in _reduce_adamw(parts_d, w_down[0], m_w_down[0], v_w_down[0], name="adamw_w_down")],
    }
    small_upd = _reduce_adamw(p_small, w_small, m_small, v_small, name="adamw_small")

    def unpack(packed_out):
        vec = lambda i, n, shape: packed_out[i, :n].reshape(shape)
        return {
            "norm_mix_g": vec(0, d, (1, d)), "ssm_conv_b": vec(1, d_xbc, (1, d_xbc)),
            "ssm_dt_bias": vec(2, heads, (1, heads)), "ssm_A_log": vec(3, heads, (1, heads)),
            "ssm_D": vec(4, heads, (1, heads)), "ssm_norm_g": vec(5, d, (1, d)), "norm_ffn_g": vec(6, d, (1, d)),
            "norm_final_g": vec(7, d, (d,)),
            "ssm_conv_w": lax.dynamic_slice(packed_out[8:8 + K_SSM], (0, conv_lo), (K_SSM, d_xbc // N_DEV))[None],
            "sc_conv_w": lax.dynamic_slice(packed_out[8 + K_SSM:8 + K_SSM + K_SC], (0, sc_lo), (K_SC, d // N_DEV))[None],
        }

    names = ["norm_mix_g", "w_in", "ssm_conv_w", "ssm_conv_b", "ssm_dt_bias", "ssm_A_log", "ssm_D", "ssm_norm_g",
             "sc_conv_w", "w_out", "norm_ffn_g", "w_gate", "w_up", "w_down", "norm_final_g"]
    outs = []
    for kind in range(4):
        small_k = unpack(small_upd[kind])
        for nm in names:
            outs.append(upd[nm][kind] if nm in upd else small_k[nm])
    return (loss, dx[None], *outs)
```

```python
import collections
import functools

import jax
import jax.numpy as jnp
from jax import lax
from jax.experimental import pallas as pl
from jax.experimental.pallas import tpu as pltpu

F32 = jnp.float32
BF16 = jnp.bfloat16

N_DEV = 8
N_CHIPS = 4
HEADDIM = 64
N_GROUPS = 8
N_STATE = 128
CHUNK = 128
K_SSM = 4
K_SC = 3
EPS = 1e-5
LANES = 128
BF16_ROWS = 16
MM_TILE_MN = 1408
MM_TILE_K = 2048
V7X_VMEM_BYTES = 64 * 1024 * 1024
VMEM_LIMIT = (V7X_VMEM_BYTES * 3) // 4

ADAM_LR = 0.001
ADAM_B1 = 0.9
ADAM_B2 = 0.999
ADAM_EPS = 1e-08
ADAM_WD = 0.01
ADAM_STEP = 10


def _tile(n, pref, align):
    t = min(pref, n)
    t -= t % align
    while t >= align:
        if n % t == 0:
            return t
        t -= align
    return n


_Ride = collections.namedtuple("_Ride", ["ins", "out_shapes", "aliases", "nsem", "plan"])
_ANY = pl.BlockSpec(memory_space=pl.ANY)


def _coords():
    return lax.axis_index("x"), lax.axis_index("y"), lax.axis_index("c")


def _other_chips(x, y):
    return ((1 - x, y), (x, 1 - y), (1 - x, 1 - y))


def _remote(src, dst, send, recv, k, dev):
    return functools.partial(pltpu.make_async_remote_copy, src_ref=src, dst_ref=dst, send_sem=send.at[k],
                             recv_sem=recv.at[k], device_id=dev, device_id_type=pl.DeviceIdType.MESH)


def _local(src, dst, sem):
    return functools.partial(pltpu.make_async_copy, src, dst, sem)


def _start_all(plan):
    for kind, make in plan:
        if kind != "arrival":
            make().start()


def _wait_all(plan):
    for kind, make in plan:
        if kind == "local":
            make().wait()
        elif kind == "out":
            make().wait_send()
        else:
            make().wait_recv()


def _gather_chips(srcs):
    def plan(ins, outs, send, recv, base):
        x, y, c = _coords()
        me = 4 * x + 2 * y + c
        d = []
        for a, (src, dst) in enumerate(zip(ins, outs)):
            k = base + 4 * a
            d.append(("local", _local(src, dst.at[me], send.at[k + 3])))
            for j, (px, py) in enumerate(_other_chips(x, y)):
                d.append(("out", _remote(src, dst.at[me], send, recv, k + j, (px, py, c))))
                d.append(("arrival", _remote(src, dst.at[4 * px + 2 * py + c], send, recv, k + j, (px, py, c))))
        return d
    shapes = [jax.ShapeDtypeStruct((N_DEV,) + s.shape, s.dtype) for s in srcs]
    return _Ride(list(srcs), shapes, {}, 4 * len(srcs), plan)


def _gather_sibling(bufs):
    def plan(ins, outs, send, recv, base):
        x, y, c = _coords()
        d = []
        for a, buf in enumerate(outs):
            for q in range(N_CHIPS):
                k = base + 4 * a + q
                d.append(("out", _remote(buf.at[2 * q + c], buf.at[2 * q + c], send, recv, k, (x, y, 1 - c))))
                d.append(("arrival", _remote(buf.at[2 * q + c], buf.at[2 * q + 1 - c], send, recv, k, (x, y, 1 - c))))
        return d
    shapes = [jax.ShapeDtypeStruct(b.shape, b.dtype) for b in bufs]
    return _Ride(list(bufs), shapes, {i: i for i in range(len(bufs))}, 4 * len(bufs), plan)


def _scatter_sibling(srcs):
    def plan(ins, outs, send, recv, base):
        x, y, c = _coords()
        d = []
        for a, (src, sib) in enumerate(zip(ins, outs)):
            for q in range(N_CHIPS):
                k = base + 4 * a + q
                d.append(("out", _remote(src.at[2 * q + 1 - c], sib.at[q], send, recv, k, (x, y, 1 - c))))
                d.append(("arrival", _remote(src.at[2 * q + 1 - c], sib.at[q], send, recv, k, (x, y, 1 - c))))
        return d
    shapes = [jax.ShapeDtypeStruct((N_CHIPS,) + s.shape[1:], s.dtype) for s in srcs]
    return _Ride(list(srcs), shapes, {}, 4 * len(srcs), plan)


def _scatter_chips(chips):
    def plan(ins, outs, send, recv, base):
        x, y, c = _coords()
        mine = 2 * x + y
        d = []
        for a, (chip, parts) in enumerate(zip(ins, outs)):
            k = base + 4 * a
            d.append(("local", _local(chip.at[mine], parts.at[mine], send.at[k + 3])))
            for j, (px, py) in enumerate(_other_chips(x, y)):
                q = 2 * px + py
                d.append(("out", _remote(chip.at[q], parts.at[mine], send, recv, k + j, (px, py, c))))
                d.append(("arrival", _remote(chip.at[q], parts.at[q], send, recv, k + j, (px, py, c))))
        return d
    shapes = [jax.ShapeDtypeStruct(s.shape, s.dtype) for s in chips]
    return _Ride(list(chips), shapes, {}, 4 * len(chips), plan)


def _gather_all(srcs):
    def plan(ins, outs, send, recv, base):
        x, y, c = _coords()
        me = 4 * x + 2 * y + c
        d = []
        for a, (src, dst) in enumerate(zip(ins, outs)):
            k = base + N_DEV * a
            d.append(("local", _local(src, dst.at[me], send.at[k])))
            for j in range(1, N_DEV):
                px = 1 - x if (j >> 2) & 1 else x
                py = 1 - y if (j >> 1) & 1 else y
                pc = 1 - c if j & 1 else c
                d.append(("out", _remote(src, dst.at[me], send, recv, k + j, (px, py, pc))))
                d.append(("arrival", _remote(src, dst.at[4 * px + 2 * py + pc], send, recv, k + j, (px, py, pc))))
        return d
    shapes = [jax.ShapeDtypeStruct((N_DEV,) + s.shape, s.dtype) for s in srcs]
    return _Ride(list(srcs), shapes, {}, N_DEV * len(srcs), plan)


def _merge(*rides):
    ins, outs, aliases, parts, nsem = [], [], {}, [], 0
    for r in rides:
        parts.append((len(ins), len(outs), nsem, r))
        aliases.update({len(ins) + i: len(outs) + j for i, j in r.aliases.items()})
        ins += r.ins
        outs += r.out_shapes
        nsem += r.nsem

    def plan(i, o, send, recv, base):
        d = []
        for i0, o0, s0, r in parts:
            d += r.plan(i[i0:i0 + len(r.ins)], o[o0:o0 + len(r.out_shapes)], send, recv, base + s0)
        return d
    return _Ride(ins, outs, aliases, nsem, plan)


def _comm(ride, name):
    n_in, n_out = len(ride.ins), len(ride.out_shapes)

    def body(*refs):
        plan = ride.plan(refs[:n_in], refs[n_in:n_in + n_out], refs[-2], refs[-1], 0)
        _start_all(plan)
        _wait_all(plan)

    return pl.pallas_call(
        body, name=name, in_specs=[_ANY] * n_in, out_specs=[_ANY] * n_out, out_shape=ride.out_shapes,
        scratch_shapes=[pltpu.SemaphoreType.DMA((ride.nsem,)), pltpu.SemaphoreType.DMA((ride.nsem,))],
        input_output_aliases=dict(ride.aliases),
        compiler_params=pltpu.CompilerParams(has_side_effects=True),
    )(*ride.ins)


def _call(body, *, name, grid, in_specs, out_specs, out_shape, args, sem, scratch=(), ride=None):
    params = pltpu.CompilerParams(dimension_semantics=sem, vmem_limit_bytes=VMEM_LIMIT)
    if ride is None:
        res = pl.pallas_call(body, name=name, grid=grid, in_specs=in_specs, out_specs=out_specs,
                             out_shape=out_shape, scratch_shapes=list(scratch), compiler_params=params)(*args)
        return list(res), []
    n_in, n_out, n_scr = len(args), len(out_shape), len(scratch)
    r_in, r_out = len(ride.ins), len(ride.out_shapes)

    def hosted(*refs):
        h_in, rin = refs[:n_in], refs[n_in:n_in + r_in]
        o0 = n_in + r_in
        h_out, rout = refs[o0:o0 + n_out], refs[o0 + n_out:o0 + n_out + r_out]
        s0 = o0 + n_out + r_out
        h_scr, send, recv = refs[s0:s0 + n_scr], refs[s0 + n_scr], refs[s0 + n_scr + 1]
        ids = [pl.program_id(i) for i in range(len(grid))]
        first = functools.reduce(lambda p, q: p & q, [i == 0 for i in ids])
        last = functools.reduce(lambda p, q: p & q, [i == n - 1 for i, n in zip(ids, grid)])

        @pl.when(first)
        def _():
            _start_all(ride.plan(rin, rout, send, recv, 0))

        body(*h_in, *h_out, *h_scr)

        @pl.when(last)
        def _():
            _wait_all(ride.plan(rin, rout, send, recv, 0))

    res = pl.pallas_call(
        hosted, name=name, grid=grid, in_specs=list(in_specs) + [_ANY] * r_in,
        out_specs=list(out_specs) + [_ANY] * r_out, out_shape=list(out_shape) + list(ride.out_shapes),
        scratch_shapes=list(scratch) + [pltpu.SemaphoreType.DMA((ride.nsem,)), pltpu.SemaphoreType.DMA((ride.nsem,))],
        input_output_aliases={n_in + i: n_out + j for i, j in ride.aliases.items()},
        compiler_params=params,
    )(*args, *ride.ins)
    return list(res[:n_out]), list(res[n_out:])


def _matmul(a, b, *, ta=False, tb=False, out_dtype=BF16, add=None, name, ride=None):
    m = a.shape[1] if ta else a.shape[0]
    k = a.shape[0] if ta else a.shape[1]
    n = b.shape[0] if tb else b.shape[1]
    assert k == (b.shape[1] if tb else b.shape[0])
    tm, tn, tk = _tile(m, MM_TILE_MN, LANES), _tile(n, MM_TILE_MN, LANES), _tile(k, MM_TILE_K, LANES)
    nk = k // tk
    dims = (((0 if ta else 1,), (1 if tb else 0,)), ((), ()))

    def body(*refs):
        a_ref, b_ref = refs[:2]
        add_ref = refs[2] if add is not None else None
        o_ref = refs[3] if add is not None else refs[2]

        def finish(r):
            if add is not None:
                r = r + add_ref[...].astype(F32)
            o_ref[...] = r.astype(o_ref.dtype)

        part = lax.dot_general(a_ref[...].astype(BF16), b_ref[...].astype(BF16), dims, preferred_element_type=F32)
        if nk == 1:
            finish(part)
            return
        acc = refs[-1]
        kk = pl.program_id(2)

        @pl.when(kk == 0)
        def _():
            acc[...] = part

        @pl.when((kk > 0) & (kk < nk - 1))
        def _():
            acc[...] += part

        @pl.when(kk == nk - 1)
        def _():
            finish(acc[...] + part)

    a_spec = (pl.BlockSpec((tk, tm), lambda i, j, kk: (kk, i)) if ta
              else pl.BlockSpec((tm, tk), lambda i, j, kk: (i, kk)))
    b_spec = (pl.BlockSpec((tn, tk), lambda i, j, kk: (j, kk)) if tb
              else pl.BlockSpec((tk, tn), lambda i, j, kk: (kk, j)))
    o_spec = pl.BlockSpec((tm, tn), lambda i, j, kk: (i, j))
    outs, rides = _call(
        body, name=name, grid=(m // tm, n // tn, nk),
        in_specs=[a_spec, b_spec] + ([o_spec] if add is not None else []), out_specs=[o_spec],
        out_shape=[jax.ShapeDtypeStruct((m, n), out_dtype)], args=(a, b) + ((add,) if add is not None else ()),
        scratch=[pltpu.VMEM((tm, tn), F32)] if nk > 1 else [], sem=("parallel", "parallel", "arbitrary"), ride=ride)
    return outs[0], rides


def _rows_call(fn, *, rows, tr, row_ins, full_ins, row_outs, acc_outs, name, ride=None):
    nr, nf, no, na = len(row_ins), len(full_ins), len(row_outs), len(acc_outs)

    def body(*refs):
        vals = [r[...] for r in refs[:nr + nf]]
        outs, accs = fn(*vals)
        for r, v in zip(refs[nr + nf:nr + nf + no], outs):
            r[...] = v.astype(r.dtype)
        if na:
            @pl.when(pl.program_id(0) == 0)
            def _():
                for r in refs[nr + nf + no:]:
                    r[...] = jnp.zeros_like(r)
            for r, v in zip(refs[nr + nf + no:], accs):
                r[...] += v

    in_specs = [pl.BlockSpec((tr, w), functools.partial(lambda cb, i: (i, cb), cb)) for _, w, cb in row_ins]
    in_specs += [pl.BlockSpec(f.shape, lambda i: (0, 0)) for f in full_ins]
    out_specs = [pl.BlockSpec((tr, w), lambda i: (i, 0)) for w, _ in row_outs]
    out_specs += [pl.BlockSpec(s, lambda i: (0, 0)) for s in acc_outs]
    out_shape = [jax.ShapeDtypeStruct((rows, w), dt) for w, dt in row_outs]
    out_shape += [jax.ShapeDtypeStruct(s, F32) for s in acc_outs]
    return _call(body, name=name, grid=(rows // tr,), in_specs=in_specs, out_specs=out_specs, out_shape=out_shape,
                 args=tuple(a for a, _, _ in row_ins) + tuple(full_ins), sem=("arbitrary",), ride=ride)


def _cols_call(fn, *, rows, cols, cw, col_ins, par_ins, col_outs, par_outs, name, ride=None):
    nc, npar = len(col_ins), len(par_ins)

    def body(*refs):
        vals = [r[...] for r in refs[:nc + npar]]
        outs, pouts = fn(*vals)
        for r, v in zip(refs[nc + npar:], tuple(outs) + tuple(pouts)):
            r[...] = v.astype(r.dtype)

    in_specs = [pl.BlockSpec((rows, cw), functools.partial(lambda off, j: (0, off + j), off)) for _, off in col_ins]
    in_specs += [pl.BlockSpec((p.shape[0], cw), functools.partial(lambda off, j: (0, off + j), off))
                 for p, off in par_ins]
    out_specs = [pl.BlockSpec((rows, cw), lambda j: (0, j)) for _ in col_outs]
    out_specs += [pl.BlockSpec((k, cw), lambda j: (0, j)) for k in par_outs]
    out_shape = [jax.ShapeDtypeStruct((rows, cols), dt) for dt in col_outs]
    out_shape += [jax.ShapeDtypeStruct((k, cols), F32) for k in par_outs]
    return _call(body, name=name, grid=(cols // cw,), in_specs=in_specs, out_specs=out_specs, out_shape=out_shape,
                 args=tuple(a for a, _ in col_ins) + tuple(p for p, _ in par_ins), sem=("arbitrary",), ride=ride)


def _sigmoid(v):
    return 1.0 / (1.0 + jnp.exp(-v))


def _softplus(v):
    return jnp.maximum(v, 0.0) + jnp.log(1.0 + jnp.exp(-jnp.abs(v)))


def _rms(v, g):
    return v * lax.rsqrt(jnp.mean(v * v, axis=-1, keepdims=True) + EPS) * g


def _shift_down(v, s, row):
    return jnp.where(row >= s, pltpu.roll(v, s, 0), 0.0)


def _shift_up(v, s, row):
    n = v.shape[0]
    return jnp.where(row < n - s, pltpu.roll(v, n - s, 0), 0.0)


def _causal_conv(u, w, row):
    k_taps = w.shape[0]
    acc = u * w[k_taps - 1:k_taps, :]
    for k in range(k_taps - 1):
        acc = acc + _shift_down(u, k_taps - 1 - k, row) * w[k:k + 1, :]
    return acc


def _causal_conv_bwd(u, dy, w, row):
    k_taps = w.shape[0]
    tap = lax.broadcasted_iota(jnp.int32, w.shape, 0)
    du = dy * w[k_taps - 1:k_taps, :]
    dw = jnp.where(tap == k_taps - 1, jnp.sum(dy * u, axis=0, keepdims=True), 0.0)
    for k in range(k_taps - 1):
        s = k_taps - 1 - k
        du = du + _shift_up(dy, s, row) * w[k:k + 1, :]
        dw = dw + jnp.where(tap == k, jnp.sum(dy * _shift_down(u, s, row), axis=0, keepdims=True), 0.0)
    return du, dw


def _conv_silu_fwd(u, w, b):
    u = u.astype(F32)
    row = lax.broadcasted_iota(jnp.int32, u.shape, 0)
    pre = _causal_conv(u, w, row) + b
    return (pre * _sigmoid(pre),), ()


def _conv_silu_bwd(u, dy, w, b):
    u = u.astype(F32)
    dy = dy.astype(F32)
    row = lax.broadcasted_iota(jnp.int32, u.shape, 0)
    pre = _causal_conv(u, w, row) + b
    s = _sigmoid(pre)
    dpre = dy * (s * (1.0 + pre * (1.0 - s)))
    du, dw = _causal_conv_bwd(u, dpre, w, row)
    return (du,), (dw, jnp.sum(dpre, axis=0, keepdims=True))


def _shortconv_fwd(gb, gc, u, w):
    gb, gc, u = gb.astype(F32), gc.astype(F32), u.astype(F32)
    row = lax.broadcasted_iota(jnp.int32, u.shape, 0)
    return (gb * _causal_conv(gc * u, w, row),), ()


def _shortconv_bwd(gb, gc, u, dy, w):
    gb, gc, u, dy = gb.astype(F32), gc.astype(F32), u.astype(F32), dy.astype(F32)
    row = lax.broadcasted_iota(jnp.int32, u.shape, 0)
    v = gc * u
    dgb = dy * _causal_conv(v, w, row)
    dv, dw = _causal_conv_bwd(v, dy * gb, w, row)
    return (dgb, dv * u, dv * gc), (dw,)


def _split3(v):
    hi = v.astype(BF16)
    r1 = v - hi.astype(F32)
    mid = r1.astype(BF16)
    lo = (r1 - mid.astype(F32)).astype(BF16)
    return hi, mid, lo


def _exact_dot(v, m01, dims, v_is_lhs):
    def one(p):
        return (lax.dot_general(p, m01, dims, preferred_element_type=F32) if v_is_lhs
                else lax.dot_general(m01, p, dims, preferred_element_type=F32))
    hi, mid, lo = _split3(v)
    return (one(lo) + one(mid)) + one(hi)


_NN = (((1,), (0,)), ((), ()))
_NT = (((1,), (1,)), ((), ()))
_TN = (((0,), (0,)), ((), ()))


@jax.custom_vjp
def _cumsum_rows(tril, v):
    return _exact_dot(v, tril, _NN, False)


def _cumsum_rows_fwd(tril, v):
    return _cumsum_rows(tril, v), tril


def _cumsum_rows_bwd(tril, ct):
    return None, _exact_dot(ct, tril, _TN, False)


_cumsum_rows.defvjp(_cumsum_rows_fwd, _cumsum_rows_bwd)


@jax.custom_vjp
def _cumsum_lanes(tril, v):
    return _exact_dot(v, tril, _NT, True)


def _cumsum_lanes_fwd(tril, v):
    return _cumsum_lanes(tril, v), tril


def _cumsum_lanes_bwd(tril, ct):
    return None, _exact_dot(ct, tril, _NN, True)


_cumsum_lanes.defvjp(_cumsum_lanes_fwd, _cumsum_lanes_bwd)


def _ssd_chunk(g, r_heads, xs, bg, cg, dtc, dtr, bias_r, bias_c, alog_r, alog_c, dskip, hp):
    l_len, rp = xs.shape
    p = rp // r_heads
    dt_c = _softplus(dtc + bias_r)
    dt_r = _softplus(dtr + bias_c)
    da_c = dt_c * (-jnp.exp(alog_r))
    da_r = dt_r * (-jnp.exp(alog_c))
    li = lax.broadcasted_iota(jnp.int32, (l_len, l_len), 0)
    si = lax.broadcasted_iota(jnp.int32, (l_len, l_len), 1)
    causal = si <= li
    tril = jnp.where(causal, 1.0, 0.0).astype(BF16)
    cs_c = _cumsum_rows(tril, da_c)
    cs_r = _cumsum_lanes(tril, da_r)
    head_lane = lax.broadcasted_iota(jnp.int32, (1, dtc.shape[1]), 1)
    head_sub = lax.broadcasted_iota(jnp.int32, (dtr.shape[0], 1), 0)
    lane_head = lax.broadcasted_iota(jnp.int32, (1, rp), 1) // p

    def col(v, r):
        return jnp.sum(jnp.where(head_lane == g * r_heads + r, v, 0.0), axis=1, keepdims=True)

    def expand(v):
        out = jnp.where(lane_head == 0, col(v, 0), 0.0)
        for r in range(1, r_heads):
            out = out + jnp.where(lane_head == r, col(v, r), 0.0)
        return out

    dt_e = expand(dt_c)
    cs_e = expand(cs_c)
    cl_e = expand(cs_c[l_len - 1:l_len, :])
    x = xs * dt_e
    bgb, cgb = bg.astype(BF16), cg.astype(BF16)
    cb = lax.dot_general(cgb, bgb, _NT, preferred_element_type=F32)
    ms, xm = [], []
    for r in range(r_heads):
        row = jnp.sum(jnp.where(head_sub == g * r_heads + r, cs_r, 0.0), axis=0, keepdims=True)
        seg = col(cs_c, r) - row
        decay = jnp.exp(jnp.where(causal, seg, -1e30))
        ms.append((cb * decay).astype(BF16))
        xm.append(jnp.where(lane_head == r, x, 0.0).astype(BF16))
    y_diag = lax.dot_general(jnp.concatenate(ms, axis=1), jnp.concatenate(xm, axis=0), _NN,
                             preferred_element_type=F32)
    y_off = lax.dot_general(cgb, hp.astype(BF16), _NN, preferred_element_type=F32) * jnp.exp(cs_e)
    xd = (x * jnp.exp(cl_e - cs_e)).astype(BF16)
    states = lax.dot_general(bgb, xd, _TN, preferred_element_type=F32)
    h_next = hp * jnp.exp(cl_e) + states
    y = y_diag + y_off + expand(dskip) * xs
    return y, h_next


def _ssd_specs(t_len, d_ssm, r_heads, reverse):
    rp = r_heads * HEADDIM
    nc = t_len // CHUNK
    cidx = (lambda c: nc - 1 - c) if reverse else (lambda c: c)
    b_off = d_ssm // N_STATE
    specs = dict(
        xs=pl.BlockSpec((CHUNK, rp), lambda c, g: (cidx(c), g)),
        b=pl.BlockSpec((CHUNK, N_STATE), lambda c, g: (cidx(c), b_off + g)),
        c=pl.BlockSpec((CHUNK, N_STATE), lambda c, g: (cidx(c), b_off + N_GROUPS + g)),
        dtc=pl.BlockSpec((CHUNK, LANES), lambda c, g: (cidx(c), 0)),
        dtr=lambda h: pl.BlockSpec((h, CHUNK), lambda c, g: (0, cidx(c))),
        full=lambda shape: pl.BlockSpec(shape, lambda c, g: (0, 0)),
        hprev=pl.BlockSpec((None, None, N_STATE, rp), lambda c, g: (cidx(c), g, 0, 0)),
    )
    return specs, nc, rp


def _ssd_fwd(xbc, dtc, dtr, small, *, d_ssm, r_heads, ride=None):
    t_len = xbc.shape[0]
    sp, nc, rp = _ssd_specs(t_len, d_ssm, r_heads, False)

    def body(xs_ref, b_ref, c_ref, dtc_ref, dtr_ref, br, bc, ar, ac, dk, y_ref, hprev_ref, h_ref):
        c, g = pl.program_id(0), pl.program_id(1)

        @pl.when(c == 0)
        def _():
            h_ref[g] = jnp.zeros((N_STATE, rp), F32)

        hp = h_ref[g]
        hprev_ref[...] = hp
        y, hn = _ssd_chunk(g, r_heads, xs_ref[...].astype(F32), b_ref[...].astype(F32), c_ref[...].astype(F32),
                           dtc_ref[...], dtr_ref[...], br[...], bc[...], ar[...], ac[...], dk[...], hp)
        y_ref[...] = y
        h_ref[g] = hn

    in_specs = [sp["xs"], sp["b"], sp["c"], sp["dtc"], sp["dtr"](dtr.shape[0])]
    in_specs += [sp["full"](s.shape) for s in small]
    return _call(
        body, name="ssd_fwd", grid=(nc, N_GROUPS), in_specs=in_specs,
        out_specs=[pl.BlockSpec((CHUNK, rp), lambda c, g: (c, g)), sp["hprev"]],
        out_shape=[jax.ShapeDtypeStruct((t_len, d_ssm), F32),
                   jax.ShapeDtypeStruct((nc, N_GROUPS, N_STATE, rp), F32)],
        args=(xbc, xbc, xbc, dtc, dtr, *small), scratch=[pltpu.VMEM((N_GROUPS, N_STATE, rp), F32)],
        sem=("arbitrary", "arbitrary"), ride=ride)


def _ssd_bwd(xbc, dtc, dtr, small, hprev, dy, *, d_ssm, r_heads, ride=None):
    t_len = xbc.shape[0]
    sp, nc, rp = _ssd_specs(t_len, d_ssm, r_heads, True)
    n_small = len(small)

    def body(*refs):
        xs_ref, b_ref, c_ref, dtc_ref, dtr_ref = refs[:5]
        small_refs = refs[5:5 + n_small]
        hprev_ref, dy_ref = refs[5 + n_small:7 + n_small]
        dxs_ref, db_ref, dc_ref, ddtc_ref, ddtr_ref = refs[7 + n_small:12 + n_small]
        dsmall_refs = refs[12 + n_small:12 + 2 * n_small]
        dh_ref = refs[12 + 2 * n_small]
        c, g = pl.program_id(0), pl.program_id(1)

        @pl.when(c == 0)
        def _():
            dh_ref[g] = jnp.zeros((N_STATE, rp), F32)

        @pl.when(g == 0)
        def _():
            ddtc_ref[...] = jnp.zeros_like(ddtc_ref)
            ddtr_ref[...] = jnp.zeros_like(ddtr_ref)

        @pl.when((c == 0) & (g == 0))
        def _():
            for r in dsmall_refs:
                r[...] = jnp.zeros_like(r)

        args = (xs_ref[...].astype(F32), b_ref[...].astype(F32), c_ref[...].astype(F32), dtc_ref[...], dtr_ref[...],
                *[r[...] for r in small_refs], hprev_ref[...])
        _, vjp = jax.vjp(functools.partial(_ssd_chunk, g, r_heads), *args)
        grads = vjp((dy_ref[...], dh_ref[g]))
        dxs_ref[...] = grads[0].astype(dxs_ref.dtype)
        db_ref[...] = grads[1].astype(db_ref.dtype)
        dc_ref[...] = grads[2].astype(dc_ref.dtype)
        ddtc_ref[...] += grads[3]
        ddtr_ref[...] += grads[4]
        for r, gr in zip(dsmall_refs, grads[5:5 + n_small]):
            r[...] += gr
        dh_ref[g] = grads[5 + n_small]

    rev = lambda c: nc - 1 - c
    in_specs = [sp["xs"], sp["b"], sp["c"], sp["dtc"], sp["dtr"](dtr.shape[0])]
    in_specs += [sp["full"](s.shape) for s in small]
    in_specs += [sp["hprev"], pl.BlockSpec((CHUNK, rp), lambda c, g: (rev(c), g))]
    out_specs = [pl.BlockSpec((CHUNK, rp), lambda c, g: (rev(c), g)),
                 pl.BlockSpec((CHUNK, N_STATE), lambda c, g: (rev(c), g)),
                 pl.BlockSpec((CHUNK, N_STATE), lambda c, g: (rev(c), g)),
                 sp["dtc"], sp["dtr"](dtr.shape[0])]
    out_specs += [sp["full"](s.shape) for s in small]
    out_shape = [jax.ShapeDtypeStruct((t_len, d_ssm), BF16),
                 jax.ShapeDtypeStruct((t_len, N_GROUPS * N_STATE), BF16),
                 jax.ShapeDtypeStruct((t_len, N_GROUPS * N_STATE), BF16),
                 jax.ShapeDtypeStruct(dtc.shape, F32), jax.ShapeDtypeStruct(dtr.shape, F32)]
    out_shape += [jax.ShapeDtypeStruct(s.shape, F32) for s in small]
    return _call(
        body, name="ssd_bwd", grid=(nc, N_GROUPS), in_specs=in_specs, out_specs=out_specs, out_shape=out_shape,
        args=(xbc, xbc, xbc, dtc, dtr, *small, hprev, dy), scratch=[pltpu.VMEM((N_GROUPS, N_STATE, rp), F32)],
        sem=("arbitrary", "arbitrary"), ride=ride)


def _chip_sum(src, sib, *, name):
    rows, cols = src.shape[1:]
    tr = _tile(rows, 256, BF16_ROWS)
    core = lax.axis_index("c").astype(jnp.int32).reshape(1)

    def body(c_ref, a_ref, b_ref, o_ref):
        o_ref[...] = (a_ref[...].astype(F32) + b_ref[...].astype(F32)).astype(o_ref.dtype)

    grid_spec = pltpu.PrefetchScalarGridSpec(
        num_scalar_prefetch=1, grid=(N_CHIPS, rows // tr),
        in_specs=[pl.BlockSpec((None, tr, cols), lambda q, i, c_ref: (2 * q + c_ref[0], i, 0)),
                  pl.BlockSpec((None, tr, cols), lambda q, i, c_ref: (q, i, 0))],
        out_specs=pl.BlockSpec((None, tr, cols), lambda q, i, c_ref: (q, i, 0)))
    return pl.pallas_call(
        body, name=name, grid_spec=grid_spec, out_shape=jax.ShapeDtypeStruct(sib.shape, sib.dtype),
        compiler_params=pltpu.CompilerParams(dimension_semantics=("parallel", "parallel"), vmem_limit_bytes=VMEM_LIMIT),
    )(core, src, sib)


def _adamw(w, g, m, v):
    m = ADAM_B1 * m + (1.0 - ADAM_B1) * g
    v = ADAM_B2 * v + (1.0 - ADAM_B2) * (g * g)
    m_hat = m / (1.0 - ADAM_B1 ** ADAM_STEP)
    v_hat = v / (1.0 - ADAM_B2 ** ADAM_STEP)
    delta = -ADAM_LR * (m_hat / (jnp.sqrt(v_hat) + ADAM_EPS) + ADAM_WD * w)
    return delta, m, v


def _reduce_adamw(parts, w, m, v, *, name):
    n_parts = parts.shape[0]
    rows, cols = w.shape
    tr = _tile(rows, 128, BF16_ROWS)

    def body(p_ref, w_ref, m_ref, v_ref, g_ref, d_ref, mo_ref, vo_ref):
        g = p_ref[0].astype(F32)
        for k in range(1, n_parts):
            g = g + p_ref[k].astype(F32)
        delta, mn, vn = _adamw(w_ref[...], g, m_ref[...], v_ref[...])
        g_ref[...] = g
        d_ref[...] = delta
        mo_ref[...] = mn
        vo_ref[...] = vn

    spec = pl.BlockSpec((tr, cols), lambda i: (i, 0))
    outs, _ = _call(
        body, name=name, grid=(rows // tr,),
        in_specs=[pl.BlockSpec((n_parts, tr, cols), lambda i: (0, i, 0)), spec, spec, spec],
        out_specs=[spec] * 4, out_shape=[jax.ShapeDtypeStruct((rows, cols), F32)] * 4,
        args=(parts, w, m, v), sem=("parallel",))
    return outs


def _cols_of(g):
    return jnp.transpose(g, (1, 0, 2)).reshape(g.shape[1], -1)


def _pad_to(a, rows, cols):
    return jnp.pad(a, ((0, rows - a.shape[0]), (0, cols - a.shape[1])))


def kernel(x, norm_mix_g, w_in, ssm_conv_w, ssm_conv_b, ssm_dt_bias, ssm_A_log, ssm_D, ssm_norm_g, sc_conv_w, w_out, norm_ffn_g, w_gate, w_up, w_down, norm_final_g, loss_target, m_norm_mix_g, m_w_in, m_ssm_conv_w, m_ssm_conv_b, m_ssm_dt_bias, m_ssm_A_log, m_ssm_D, m_ssm_norm_g, m_sc_conv_w, m_w_out, m_norm_ffn_g, m_w_gate, m_w_up, m_w_down, m_norm_final_g, v_norm_mix_g, v_w_in, v_ssm_conv_w, v_ssm_conv_b, v_ssm_dt_bias, v_ssm_A_log, v_ssm_D, v_ssm_norm_g, v_sc_conv_w, v_w_out, v_norm_ffn_g, v_w_gate, v_w_up, v_w_down, v_norm_final_g):
    t_len, d = x.shape[1], x.shape[2]
    heads = d // HEADDIM
    r_heads = heads // N_GROUPS
    d_xbc = d + 2 * N_GROUPS * N_STATE
    ff_s = w_down.shape[1]
    ff = ff_s * N_DEV
    off_xbc, off_dt = d, d + d_xbc
    off_cb = off_dt + heads
    d_in = off_cb + 3 * d
    in_s = d_in // N_DEV
    in_p = -(-in_s // (2 * BF16_ROWS)) * (2 * BF16_ROWS)
    w_main = 4 * d + d_xbc
    me = 4 * lax.axis_index("x") + 2 * lax.axis_index("y") + lax.axis_index("c")

    x2 = x[0]
    target = loss_target[0]

    tpose = lambda a: jnp.transpose(a[0])
    win_s = _pad_to(tpose(w_in).astype(BF16), in_p, d)
    wg_s, wu_s = tpose(w_gate).astype(BF16), tpose(w_up).astype(BF16)
    wo_s, wd_s = w_out[0].astype(BF16), w_down[0].astype(BF16)
    small_w = jnp.concatenate([_pad_to(ssm_conv_w[0], K_SSM, d_xbc // N_DEV),
                               _pad_to(sc_conv_w[0], K_SC + 1, d_xbc // N_DEV)], axis=0)

    g1, g2, g3 = norm_mix_g, norm_ffn_g, norm_final_g.reshape(1, d)
    gs = ssm_norm_g
    small = [_pad_to(ssm_dt_bias, 1, LANES), ssm_dt_bias.reshape(heads, 1), _pad_to(ssm_A_log, 1, LANES),
             ssm_A_log.reshape(heads, 1), _pad_to(ssm_D, 1, LANES)]
    tr = _tile(t_len, 256, 8)
    tr_ff = _tile(t_len, 128, 8)
    cw = LANES
    slab = lambda col: col // cw

    gin_1, gsm_1 = _comm(_gather_chips([win_s, small_w]), "gather_w_in_chips")
    (n1,), (gin, gsm) = _rows_call(lambda v, g: ((_rms(v, g),), ()), rows=t_len, tr=tr, row_ins=[(x2, d, 0)],
                                   full_ins=[g1], row_outs=[(d, BF16)], acc_outs=[], name="norm_mix",
                                   ride=_gather_sibling([gin_1, gsm_1]))
    win_all = gin[:, :in_s].reshape(d_in, d)
    wtm = jnp.concatenate([win_all[:off_dt], win_all[off_cb:]], axis=0)
    wtdt = _pad_to(win_all[off_dt:off_cb], LANES, d)
    cw_ssm = _cols_of(gsm[:, :K_SSM, :])
    cw_sc = _cols_of(gsm[:, K_SSM:K_SSM + K_SC, :d // N_DEV])

    proj, (gg_1, gu_1, go_1) = _matmul(n1, wtm, tb=True, out_dtype=BF16, name="proj_main",
                                       ride=_gather_chips([wg_s, wu_s, wo_s]))
    dt_raw, _ = _matmul(n1, wtdt, tb=True, out_dtype=F32, name="proj_dt")
    dt_raw_t = jnp.transpose(dt_raw[:, :heads])
    (xbc,), (gg, gu, go) = _cols_call(_conv_silu_fwd, rows=t_len, cols=d_xbc, cw=cw, col_ins=[(proj, slab(off_xbc))],
                                      par_ins=[(cw_ssm, 0), (ssm_conv_b, 0)], col_outs=[BF16], par_outs=[],
                                      name="ssm_conv", ride=_gather_sibling([gg_1, gu_1, go_1]))
    (y_ssd, hprev), (gd_1,) = _ssd_fwd(xbc, dt_raw, dt_raw_t, small, d_ssm=d, r_heads=r_heads,
                                       ride=_gather_chips([wd_s]))

    def gate_norm(y, z, g):
        z = z.astype(F32)
        return _rms(y * (z * _sigmoid(z)), g)

    (y_ssm,), (gd,) = _rows_call(lambda y, z, g: ((gate_norm(y, z, g),), ()), rows=t_len, tr=tr,
                                 row_ins=[(y_ssd, d, 0), (proj, d, 0)], full_ins=[gs], row_outs=[(d, BF16)],
                                 acc_outs=[], name="ssm_gate_norm", ride=_gather_sibling([gd_1]))
    wgt, wut = gg.reshape(ff, d), gu.reshape(ff, d)
    wo, wd = go.reshape(2 * d, d), gd.reshape(ff, d)
    sc0 = slab(d + d_xbc)
    (y_sc,), _ = _cols_call(_shortconv_fwd, rows=t_len, cols=d, cw=cw,
                            col_ins=[(proj, sc0), (proj, sc0 + slab(d)), (proj, sc0 + 2 * slab(d))],
                            par_ins=[(cw_sc, 0)], col_outs=[BF16], par_outs=[], name="shortconv")
    y_mix = jnp.concatenate([y_ssm, y_sc], axis=1)
    h1, _ = _matmul(y_mix, wo, out_dtype=F32, add=x2, name="out_proj")
    (n2,), _ = _rows_call(lambda v, g: ((_rms(v, g),), ()), rows=t_len, tr=tr, row_ins=[(h1, d, 0)], full_ins=[g2],
                          row_outs=[(d, BF16)], acc_outs=[], name="norm_ffn")
    g_ff, _ = _matmul(n2, wgt, tb=True, out_dtype=BF16, name="ffn_gate")
    u_ff, _ = _matmul(n2, wut, tb=True, out_dtype=BF16, name="ffn_up")

    def act(gv, uv):
        gv, uv = gv.astype(F32), uv.astype(F32)
        return ((gv * _sigmoid(gv) * uv,), ())

    (a_ff,), _ = _rows_call(act, rows=t_len, tr=tr_ff, row_ins=[(g_ff, ff, 0), (u_ff, ff, 0)], full_ins=[],
                            row_outs=[(ff, BF16)], acc_outs=[], name="ffn_act")
    h2, _ = _matmul(a_ff, wd, out_dtype=F32, add=h1, name="ffn_down")

    def head(hv, tv, g):
        def f(hh, gg_):
            e = _rms(hh, gg_) - tv
            return (0.5 / d) * jnp.sum(e * e)
        val, (dh, dg) = jax.value_and_grad(f, argnums=(0, 1))(hv, g)
        return (dh, dh), (jnp.full((1, LANES), val, F32), dg)

    (dh2, dh2_b, loss_acc, dg3), _ = _rows_call(head, rows=t_len, tr=tr, row_ins=[(h2, d, 0), (target, d, 0)],
                                                full_ins=[g3], row_outs=[(d, F32), (d, BF16)],
                                                acc_outs=[(1, LANES), (1, d)], name="loss_head")
    loss = lax.psum(loss_acc[0, 0], ("x", "y", "c"))

    da, _ = _matmul(dh2_b, wd, tb=True, out_dtype=BF16, name="d_ffn_act")
    dwd, _ = _matmul(a_ff, dh2_b, ta=True, out_dtype=BF16, name="d_w_down")
    dwd8 = dwd.reshape(N_DEV, ff_s, d)

    def act_bwd(dav, gv, uv):
        dav, gv, uv = dav.astype(F32), gv.astype(F32), uv.astype(F32)
        s = _sigmoid(gv)
        return ((dav * uv * (s * (1.0 + gv * (1.0 - s))), dav * gv * s), ())

    (dg_ff, du_ff), (sib_d,) = _rows_call(act_bwd, rows=t_len, tr=tr_ff,
                                          row_ins=[(da, ff, 0), (g_ff, ff, 0), (u_ff, ff, 0)], full_ins=[],
                                          row_outs=[(ff, BF16), (ff, BF16)], acc_outs=[], name="d_ffn_gate_up",
                                          ride=_scatter_sibling([dwd8]))
    chip_d = _chip_sum(dwd8, sib_d, name="chip_sum_w_down")
    dn2, (parts_d,) = _matmul(dg_ff, wgt, out_dtype=F32, name="d_norm_ffn_out_gate", ride=_scatter_chips([chip_d]))
    dn2, _ = _matmul(du_ff, wut, out_dtype=F32, add=dn2, name="d_norm_ffn_out_up")
    dwg, _ = _matmul(dg_ff, n2, ta=True, out_dtype=BF16, name="d_w_gate")
    dwu, _ = _matmul(du_ff, n2, ta=True, out_dtype=BF16, name="d_w_up")
    dwg8, dwu8 = dwg.reshape(N_DEV, ff_s, d), dwu.reshape(N_DEV, ff_s, d)

    def norm_bwd(v, dn, dres, g):
        _, vjp = jax.vjp(_rms, v, g)
        dv, dg = vjp(dn)
        return (dv + dres,), (dg,)

    def norm_bwd_2(v, dn, dres, g):
        (dv,), acc = norm_bwd(v, dn, dres, g)
        return (dv, dv), acc

    (dh1, dh1_b, dg2), (sib_g, sib_u) = _rows_call(norm_bwd_2, rows=t_len, tr=tr,
                                                   row_ins=[(h1, d, 0), (dn2, d, 0), (dh2, d, 0)], full_ins=[g2],
                                                   row_outs=[(d, F32), (d, BF16)], acc_outs=[(1, d)], name="d_norm_ffn",
                                                   ride=_scatter_sibling([dwg8, dwu8]))
    chip_g = _chip_sum(dwg8, sib_g, name="chip_sum_w_gate")
    chip_u = _chip_sum(dwu8, sib_u, name="chip_sum_w_up")

    dy_mix, _ = _matmul(dh1_b, wo, tb=True, out_dtype=BF16, name="d_y_mix")
    dwo, _ = _matmul(y_mix, dh1_b, ta=True, out_dtype=BF16, name="d_w_out")
    dwo8 = dwo.reshape(N_DEV, 2 * d // N_DEV, d)
    (dgb, dgc, du, dcw_sc), (sib_o,) = _cols_call(
        _shortconv_bwd, rows=t_len, cols=d, cw=cw,
        col_ins=[(proj, sc0), (proj, sc0 + slab(d)), (proj, sc0 + 2 * slab(d)), (dy_mix, slab(d))],
        par_ins=[(cw_sc, 0)], col_outs=[BF16] * 3, par_outs=[K_SC], name="d_shortconv",
        ride=_scatter_sibling([dwo8]))
    chip_o = _chip_sum(dwo8, sib_o, name="chip_sum_w_out")

    def gate_norm_bwd(y, z, dyo, g):
        _, vjp = jax.vjp(gate_norm, y, z.astype(F32), g)
        dy, dz, dg = vjp(dyo.astype(F32))
        return (dy, dz), (dg,)

    (dy_ssd, dz, dgs), _ = _rows_call(gate_norm_bwd, rows=t_len, tr=tr,
                                      row_ins=[(y_ssd, d, 0), (proj, d, 0), (dy_mix, d, 0)], full_ins=[gs],
                                      row_outs=[(d, F32), (d, BF16)], acc_outs=[(1, d)], name="d_ssm_gate_norm")
    ssd_grads, (parts_g, parts_u, parts_o) = _ssd_bwd(xbc, dt_raw, dt_raw_t, small, hprev, dy_ssd, d_ssm=d,
                                                      r_heads=r_heads, ride=_scatter_chips([chip_g, chip_u, chip_o]))
    dxs, dbm, dcm, ddt_c, ddt_r = ssd_grads[:5]
    dbias_r, dbias_c, dalog_r, dalog_c, ddskip = ssd_grads[5:]
    dxbc = jnp.concatenate([dxs, dbm, dcm], axis=1)
    (dxbc_pre, dcw_ssm, dcb_ssm), _ = _cols_call(
        _conv_silu_bwd, rows=t_len, cols=d_xbc, cw=cw, col_ins=[(proj, slab(off_xbc)), (dxbc, 0)],
        par_ins=[(cw_ssm, 0), (ssm_conv_b, 0)], col_outs=[BF16], par_outs=[K_SSM, 1], name="d_ssm_conv")
    dproj = jnp.concatenate([dz, dxbc_pre, dgb, dgc, du], axis=1)
    ddt = ddt_c + _pad_to(jnp.transpose(ddt_r), t_len, LANES)
    dwm, _ = _matmul(dproj, n1, ta=True, out_dtype=BF16, name="d_w_in_main")
    dwdt, _ = _matmul(ddt, n1, ta=True, out_dtype=BF16, name="d_w_in_dt")
    dwin_all = jnp.concatenate([dwm[:off_dt], dwdt[:heads], dwm[off_dt:]], axis=0)
    dwin8 = jnp.pad(dwin_all.reshape(N_DEV, in_s, d), ((0, 0), (0, in_p - in_s), (0, 0)))
    dn1, (sib_in,) = _matmul(ddt, wtdt, out_dtype=F32, name="d_norm_mix_out_dt", ride=_scatter_sibling([dwin8]))
    chip_in = _chip_sum(dwin8, sib_in, name="chip_sum_w_in")
    dn1, (parts_in,) = _matmul(dproj, wtm, out_dtype=F32, add=dn1, name="d_norm_mix_out",
                               ride=_scatter_chips([chip_in]))
    (dx, dg1), _ = _rows_call(norm_bwd, rows=t_len, tr=tr, row_ins=[(x2, d, 0), (dn1, d, 0), (dh1, d, 0)],
                              full_ins=[g1], row_outs=[(d, F32)], acc_outs=[(1, d)], name="d_norm_mix")

    wide = d_xbc
    rows_small = [dg1, dcb_ssm, dbias_r + _pad_to(dbias_c.reshape(1, heads), 1, LANES),
                  dalog_r + _pad_to(dalog_c.reshape(1, heads), 1, LANES), ddskip, dgs, dg2, dg3]
    packed = jnp.concatenate([_pad_to(r, 1, wide) for r in rows_small]
                             + [dcw_ssm, _pad_to(dcw_sc, K_SC, wide), jnp.zeros((1, wide), F32)], axis=0)
    (p_small,) = _comm(_gather_all([packed]), "gather_small_grads")

    conv_lo = me * (d_xbc // N_DEV)
    sc_lo = me * (d // N_DEV)

    def pack_state(vals):
        (nm, cb, dtb, al, dk, sg, nf, nfin, cws, scs) = vals
        rows = [_pad_to(a.reshape(1, -1), 1, wide) for a in (nm, cb, dtb, al, dk, sg, nf, nfin)]
        cws_full = lax.dynamic_update_slice(jnp.zeros((K_SSM, wide), F32), cws[0], (0, conv_lo))
        scs_full = lax.dynamic_update_slice(jnp.zeros((K_SC, wide), F32), scs[0], (0, sc_lo))
        return jnp.concatenate(rows + [cws_full, scs_full, jnp.zeros((1, wide), F32)], axis=0)

    w_small = pack_state((norm_mix_g, ssm_conv_b, ssm_dt_bias, ssm_A_log, ssm_D, ssm_norm_g, norm_ffn_g, norm_final_g,
                          ssm_conv_w, sc_conv_w))
    m_small = pack_state((m_norm_mix_g, m_ssm_conv_b, m_ssm_dt_bias, m_ssm_A_log, m_ssm_D, m_ssm_norm_g, m_norm_ffn_g,
                          m_norm_final_g, m_ssm_conv_w, m_sc_conv_w))
    v_small = pack_state((v_norm_mix_g, v_ssm_conv_b, v_ssm_dt_bias, v_ssm_A_log, v_ssm_D, v_ssm_norm_g, v_norm_ffn_g,
                          v_norm_final_g, v_ssm_conv_w, v_sc_conv_w))

    tin = lambda a: _pad_to(tpose(a), in_p, d)
    tin_back = lambda a: jnp.transpose(a[:in_s])[None]
    t_back = lambda a: jnp.transpose(a)[None]
    upd = {
        "w_in": [tin_back(o) for o in _reduce_adamw(parts_in, tin(w_in), tin(m_w_in), tin(v_w_in), name="adamw_w_in")],
        "w_out": [o[None] for o in _reduce_adamw(parts_o, w_out[0], m_w_out[0], v_w_out[0], name="adamw_w_out")],
        "w_gate": [t_back(o) for o in _reduce_adamw(parts_g, tpose(w_gate), tpose(m_w_gate), tpose(v_w_gate),
                                                    name="adamw_w_gate")],
        "w_up": [t_back(o) for o in _reduce_adamw(parts_u, tpose(w_up), tpose(m_w_up), tpose(v_w_up),
                                                  name="adamw_w_up")],
        "w_down": [o[None] for o in _reduce_adamw(parts_d, w_down[0], m_w_down[0], v_w_down[0], name="adamw_w_down")],
    }
    small_upd = _reduce_adamw(p_small, w_small, m_small, v_small, name="adamw_small")

    def unpack(packed_out):
        vec = lambda i, n, shape: packed_out[i, :n].reshape(shape)
        return {
            "norm_mix_g": vec(0, d, (1, d)), "ssm_conv_b": vec(1, d_xbc, (1, d_xbc)),
            "ssm_dt_bias": vec(2, heads, (1, heads)), "ssm_A_log": vec(3, heads, (1, heads)),
            "ssm_D": vec(4, heads, (1, heads)), "ssm_norm_g": vec(5, d, (1, d)), "norm_ffn_g": vec(6, d, (1, d)),
            "norm_final_g": vec(7, d, (d,)),
            "ssm_conv_w": lax.dynamic_slice(packed_out[8:8 + K_SSM], (0, conv_lo), (K_SSM, d_xbc // N_DEV))[None],
            "sc_conv_w": lax.dynamic_slice(packed_out[8 + K_SSM:8 + K_SSM + K_SC], (0, sc_lo), (K_SC, d // N_DEV))[None],
        }

    names = ["norm_mix_g", "w_in", "ssm_conv_w", "ssm_conv_b", "ssm_dt_bias", "ssm_A_log", "ssm_D", "ssm_norm_g",
             "sc_conv_w", "w_out", "norm_ffn_g", "w_gate", "w_up", "w_down", "norm_final_g"]
    outs = []
    for kind in range(4):
        small_k = unpack(small_upd[kind])
        for nm in names:
            outs.append(upd[nm][kind] if nm in upd else small_k[nm])
    return (loss, dx[None], *outs)
```

```python
import collections
import functools

import jax
import jax.numpy as jnp
from jax import lax
from jax.experimental import pallas as pl
from jax.experimental.pallas import tpu as pltpu

F32 = jnp.float32
BF16 = jnp.bfloat16

N_DEV = 8
N_CHIPS = 4
HEADDIM = 64
N_GROUPS = 8
N_STATE = 128
CHUNK = 128
K_SSM = 4
K_SC = 3
EPS = 1e-5
LANES = 128
BF16_ROWS = 16
MM_TILE_MN = 1408
MM_TILE_K = 2048
V7X_VMEM_BYTES = 64 * 1024 * 1024
VMEM_LIMIT = (V7X_VMEM_BYTES * 3) // 4

ADAM_LR = 0.001
ADAM_B1 = 0.9
ADAM_B2 = 0.999
ADAM_EPS = 1e-08
ADAM_WD = 0.01
ADAM_STEP = 10


def _tile(n, pref, align):
    t = min(pref, n)
    t -= t % align
    while t >= align:
        if n % t == 0:
            return t
        t -= align
    return n


_Ride = collections.namedtuple("_Ride", ["ins", "out_shapes", "aliases", "nsem", "plan"])
_ANY = pl.BlockSpec(memory_space=pl.ANY)


def _coords():
    return lax.axis_index("x"), lax.axis_index("y"), lax.axis_index("c")


def _other_chips(x, y):
    return ((1 - x, y), (x, 1 - y), (1 - x, 1 - y))


def _remote(src, dst, send, recv, k, dev):
    return functools.partial(pltpu.make_async_remote_copy, src_ref=src, dst_ref=dst, send_sem=send.at[k],
                             recv_sem=recv.at[k], device_id=dev, device_id_type=pl.DeviceIdType.MESH)


def _local(src, dst, sem):
    return functools.partial(pltpu.make_async_copy, src, dst, sem)


def _start_all(plan):
    for kind, make in plan:
        if kind != "arrival":
            make().start()


def _wait_all(plan):
    for kind, make in plan:
        if kind == "local":
            make().wait()
        elif kind == "out":
            make().wait_send()
        else:
            make().wait_recv()


def _gather_chips(srcs):
    def plan(ins, outs, send, recv, base):
        x, y, c = _coords()
        me = 4 * x + 2 * y + c
        d = []
        for a, (src, dst) in enumerate(zip(ins, outs)):
            k = base + 4 * a
            d.append(("local", _local(src, dst.at[me], send.at[k + 3])))
            for j, (px, py) in enumerate(_other_chips(x, y)):
                d.append(("out", _remote(src, dst.at[me], send, recv, k + j, (px, py, c))))
                d.append(("arrival", _remote(src, dst.at[4 * px + 2 * py + c], send, recv, k + j, (px, py, c))))
        return d
    shapes = [jax.ShapeDtypeStruct((N_DEV,) + s.shape, s.dtype) for s in srcs]
    return _Ride(list(srcs), shapes, {}, 4 * len(srcs), plan)


def _gather_sibling(bufs):
    def plan(ins, outs, send, recv, base):
        x, y, c = _coords()
        d = []
        for a, buf in enumerate(outs):
            for q in range(N_CHIPS):
                k = base + 4 * a + q
                d.append(("out", _remote(buf.at[2 * q + c], buf.at[2 * q + c], send, recv, k, (x, y, 1 - c))))
                d.append(("arrival", _remote(buf.at[2 * q + c], buf.at[2 * q + 1 - c], send, recv, k, (x, y, 1 - c))))
        return d
    shapes = [jax.ShapeDtypeStruct(b.shape, b.dtype) for b in bufs]
    return _Ride(list(bufs), shapes, {i: i for i in range(len(bufs))}, 4 * len(bufs), plan)


def _scatter_sibling(srcs):
    def plan(ins, outs, send, recv, base):
        x, y, c = _coords()
        d = []
        for a, (src, sib) in enumerate(zip(ins, outs)):
            for q in range(N_CHIPS):
                k = base + 4 * a + q
                d.append(("out", _remote(src.at[2 * q + 1 - c], sib.at[q], send, recv, k, (x, y, 1 - c))))
                d.append(("arrival", _remote(src.at[2 * q + 1 - c], sib.at[q], send, recv, k, (x, y, 1 - c))))
        return d
    shapes = [jax.ShapeDtypeStruct((N_CHIPS,) + s.shape[1:], s.dtype) for s in srcs]
    return _Ride(list(srcs), shapes, {}, 4 * len(srcs), plan)


def _scatter_chips(chips):
    def plan(ins, outs, send, recv, base):
        x, y, c = _coords()
        mine = 2 * x + y
        d = []
        for a, (chip, parts) in enumerate(zip(ins, outs)):
            k = base + 4 * a
            d.append(("local", _local(chip.at[mine], parts.at[mine], send.at[k + 3])))
            for j, (px, py) in enumerate(_other_chips(x, y)):
                q = 2 * px + py
                d.append(("out", _remote(chip.at[q], parts.at[mine], send, recv, k + j, (px, py, c))))
                d.append(("arrival", _remote(chip.at[q], parts.at[q], send, recv, k + j, (px, py, c))))
        return d
    shapes = [jax.ShapeDtypeStruct(s.shape, s.dtype) for s in chips]
    return _Ride(list(chips), shapes, {}, 4 * len(chips), plan)


def _gather_all(srcs):
    def plan(ins, outs, send, recv, base):
        x, y, c = _coords()
        me = 4 * x + 2 * y + c
        d = []
        for a, (src, dst) in enumerate(zip(ins, outs)):
            k = base + N_DEV * a
            d.append(("local", _local(src, dst.at[me], send.at[k])))
            for j in range(1, N_DEV):
                px = 1 - x if (j >> 2) & 1 else x
                py = 1 - y if (j >> 1) & 1 else y
                pc = 1 - c if j & 1 else c
                d.append(("out", _remote(src, dst.at[me], send, recv, k + j, (px, py, pc))))
                d.append(("arrival", _remote(src, dst.at[4 * px + 2 * py + pc], send, recv, k + j, (px, py, pc))))
        return d
    shapes = [jax.ShapeDtypeStruct((N_DEV,) + s.shape, s.dtype) for s in srcs]
    return _Ride(list(srcs), shapes, {}, N_DEV * len(srcs), plan)


def _merge(*rides):
    ins, outs, aliases, parts, nsem = [], [], {}, [], 0
    for r in rides:
        parts.append((len(ins), len(outs), nsem, r))
        aliases.update({len(ins) + i: len(outs) + j for i, j in r.aliases.items()})
        ins += r.ins
        outs += r.out_shapes
        nsem += r.nsem

    def plan(i, o, send, recv, base):
        d = []
        for i0, o0, s0, r in parts:
            d += r.plan(i[i0:i0 + len(r.ins)], o[o0:o0 + len(r.out_shapes)], send, recv, base + s0)
        return d
    return _Ride(ins, outs, aliases, nsem, plan)


def _comm(ride, name):
    n_in, n_out = len(ride.ins), len(ride.out_shapes)

    def body(*refs):
        plan = ride.plan(refs[:n_in], refs[n_in:n_in + n_out], refs[-2], refs[-1], 0)
        _start_all(plan)
        _wait_all(plan)

    return pl.pallas_call(
        body, name=name, in_specs=[_ANY] * n_in, out_specs=[_ANY] * n_out, out_shape=ride.out_shapes,
        scratch_shapes=[pltpu.SemaphoreType.DMA((ride.nsem,)), pltpu.SemaphoreType.DMA((ride.nsem,))],
        input_output_aliases=dict(ride.aliases),
        compiler_params=pltpu.CompilerParams(has_side_effects=True),
    )(*ride.ins)


def _call(body, *, name, grid, in_specs, out_specs, out_shape, args, sem, scratch=(), ride=None):
    params = pltpu.CompilerParams(dimension_semantics=sem, vmem_limit_bytes=VMEM_LIMIT)
    if ride is None:
        res = pl.pallas_call(body, name=name, grid=grid, in_specs=in_specs, out_specs=out_specs,
                             out_shape=out_shape, scratch_shapes=list(scratch), compiler_params=params)(*args)
        return list(res), []
    n_in, n_out, n_scr = len(args), len(out_shape), len(scratch)
    r_in, r_out = len(ride.ins), len(ride.out_shapes)

    def hosted(*refs):
        h_in, rin = refs[:n_in], refs[n_in:n_in + r_in]
        o0 = n_in + r_in
        h_out, rout = refs[o0:o0 + n_out], refs[o0 + n_out:o0 + n_out + r_out]
        s0 = o0 + n_out + r_out
        h_scr, send, recv = refs[s0:s0 + n_scr], refs[s0 + n_scr], refs[s0 + n_scr + 1]
        ids = [pl.program_id(i) for i in range(len(grid))]
        first = functools.reduce(lambda p, q: p & q, [i == 0 for i in ids])
        last = functools.reduce(lambda p, q: p & q, [i == n - 1 for i, n in zip(ids, grid)])

        @pl.when(first)
        def _():
            _start_all(ride.plan(rin, rout, send, recv, 0))

        body(*h_in, *h_out, *h_scr)

        @pl.when(last)
        def _():
            _wait_all(ride.plan(rin, rout, send, recv, 0))

    res = pl.pallas_call(
        hosted, name=name, grid=grid, in_specs=list(in_specs) + [_ANY] * r_in,
        out_specs=list(out_specs) + [_ANY] * r_out, out_shape=list(out_shape) + list(ride.out_shapes),
        scratch_shapes=list(scratch) + [pltpu.SemaphoreType.DMA((ride.nsem,)), pltpu.SemaphoreType.DMA((ride.nsem,))],
        input_output_aliases={n_in + i: n_out + j for i, j in ride.aliases.items()},
        compiler_params=params,
    )(*args, *ride.ins)
    return list(res[:n_out]), list(res[n_out:])


def _matmul(a, b, *, ta=False, tb=False, out_dtype=BF16, add=None, name, ride=None):
    m = a.shape[1] if ta else a.shape[0]
    k = a.shape[0] if ta else a.shape[1]
    n = b.shape[0] if tb else b.shape[1]
    assert k == (b.shape[1] if tb else b.shape[0])
    tm, tn, tk = _tile(m, MM_TILE_MN, LANES), _tile(n, MM_TILE_MN, LANES), _tile(k, MM_TILE_K, LANES)
    nk = k // tk
    dims = (((0 if ta else 1,), (1 if tb else 0,)), ((), ()))

    def body(*refs):
        a_ref, b_ref = refs[:2]
        add_ref = refs[2] if add is not None else None
        o_ref = refs[3] if add is not None else refs[2]

        def finish(r):
            if add is not None:
                r = r + add_ref[...].astype(F32)
            o_ref[...] = r.astype(o_ref.dtype)

        part = lax.dot_general(a_ref[...].astype(BF16), b_ref[...].astype(BF16), dims, preferred_element_type=F32)
        if nk == 1:
            finish(part)
            return
        acc = refs[-1]
        kk = pl.program_id(2)

        @pl.when(kk == 0)
        def _():
            acc[...] = part

        @pl.when((kk > 0) & (kk < nk - 1))
        def _():
            acc[...] += part

        @pl.when(kk == nk - 1)
        def _():
            finish(acc[...] + part)

    a_spec = (pl.BlockSpec((tk, tm), lambda i, j, kk: (kk, i)) if ta
              else pl.BlockSpec((tm, tk), lambda i, j, kk: (i, kk)))
    b_spec = (pl.BlockSpec((tn, tk), lambda i, j, kk: (j, kk)) if tb
              else pl.BlockSpec((tk, tn), lambda i, j, kk: (kk, j)))
    o_spec = pl.BlockSpec((tm, tn), lambda i, j, kk: (i, j))
    outs, rides = _call(
        body, name=name, grid=(m // tm, n // tn, nk),
        in_specs=[a_spec, b_spec] + ([o_spec] if add is not None else []), out_specs=[o_spec],
        out_shape=[jax.ShapeDtypeStruct((m, n), out_dtype)], args=(a, b) + ((add,) if add is not None else ()),
        scratch=[pltpu.VMEM((tm, tn), F32)] if nk > 1 else [], sem=("parallel", "parallel", "arbitrary"), ride=ride)
    return outs[0], rides


def _rows_call(fn, *, rows, tr, row_ins, full_ins, row_outs, acc_outs, name, ride=None):
    nr, nf, no, na = len(row_ins), len(full_ins), len(row_outs), len(acc_outs)

    def body(*refs):
        vals = [r[...] for r in refs[:nr + nf]]
        outs, accs = fn(*vals)
        for r, v in zip(refs[nr + nf:nr + nf + no], outs):
            r[...] = v.astype(r.dtype)
        if na:
            @pl.when(pl.program_id(0) == 0)
            def _():
                for r in refs[nr + nf + no:]:
                    r[...] = jnp.zeros_like(r)
            for r, v in zip(refs[nr + nf + no:], accs):
                r[...] += v

    in_specs = [pl.BlockSpec((tr, w), functools.partial(lambda cb, i: (i, cb), cb)) for _, w, cb in row_ins]
    in_specs += [pl.BlockSpec(f.shape, lambda i: (0, 0)) for f in full_ins]
    out_specs = [pl.BlockSpec((tr, w), lambda i: (i, 0)) for w, _ in row_outs]
    out_specs += [pl.BlockSpec(s, lambda i: (0, 0)) for s in acc_outs]
    out_shape = [jax.ShapeDtypeStruct((rows, w), dt) for w, dt in row_outs]
    out_shape += [jax.ShapeDtypeStruct(s, F32) for s in acc_outs]
    return _call(body, name=name, grid=(rows // tr,), in_specs=in_specs, out_specs=out_specs, out_shape=out_shape,
                 args=tuple(a for a, _, _ in row_ins) + tuple(full_ins), sem=("arbitrary",), ride=ride)


def _cols_call(fn, *, rows, cols, cw, col_ins, par_ins, col_outs, par_outs, name, ride=None):
    nc, npar = len(col_ins), len(par_ins)

    def body(*refs):
        vals = [r[...] for r in refs[:nc + npar]]
        outs, pouts = fn(*vals)
        for r, v in zip(refs[nc + npar:], tuple(outs) + tuple(pouts)):
            r[...] = v.astype(r.dtype)

    in_specs = [pl.BlockSpec((rows, cw), functools.partial(lambda off, j: (0, off + j), off)) for _, off in col_ins]
    in_specs += [pl.BlockSpec((p.shape[0], cw), functools.partial(lambda off, j: (0, off + j), off))
                 for p, off in par_ins]
    out_specs = [pl.BlockSpec((rows, cw), lambda j: (0, j)) for _ in col_outs]
    out_specs += [pl.BlockSpec((k, cw), lambda j: (0, j)) for k in par_outs]
    out_shape = [jax.ShapeDtypeStruct((rows, cols), dt) for dt in col_outs]
    out_shape += [jax.ShapeDtypeStruct((k, cols), F32) for k in par_outs]
    return _call(body, name=name, grid=(cols // cw,), in_specs=in_specs, out_specs=out_specs, out_shape=out_shape,
                 args=tuple(a for a, _ in col_ins) + tuple(p for p, _ in par_ins), sem=("arbitrary",), ride=ride)


def _sigmoid(v):
    return 1.0 / (1.0 + jnp.exp(-v))


def _softplus(v):
    return jnp.maximum(v, 0.0) + jnp.log(1.0 + jnp.exp(-jnp.abs(v)))


def _rms(v, g):
    return v * lax.rsqrt(jnp.mean(v * v, axis=-1, keepdims=True) + EPS) * g


def _shift_down(v, s, row):
    return jnp.where(row >= s, pltpu.roll(v, s, 0), 0.0)


def _shift_up(v, s, row):
    n = v.shape[0]
    return jnp.where(row < n - s, pltpu.roll(v, n - s, 0), 0.0)


def _causal_conv(u, w, row):
    k_taps = w.shape[0]
    acc = u * w[k_taps - 1:k_taps, :]
    for k in range(k_taps - 1):
        acc = acc + _shift_down(u, k_taps - 1 - k, row) * w[k:k + 1, :]
    return acc


def _causal_conv_bwd(u, dy, w, row):
    k_taps = w.shape[0]
    tap = lax.broadcasted_iota(jnp.int32, w.shape, 0)
    du = dy * w[k_taps - 1:k_taps, :]
    dw = jnp.where(tap == k_taps - 1, jnp.sum(dy * u, axis=0, keepdims=True), 0.0)
    for k in range(k_taps - 1):
        s = k_taps - 1 - k
        du = du + _shift_up(dy, s, row) * w[k:k + 1, :]
        dw = dw + jnp.where(tap == k, jnp.sum(dy * _shift_down(u, s, row), axis=0, keepdims=True), 0.0)
    return du, dw


def _conv_silu_fwd(u, w, b):
    u = u.astype(F32)
    row = lax.broadcasted_iota(jnp.int32, u.shape, 0)
    pre = _causal_conv(u, w, row) + b
    return (pre * _sigmoid(pre),), ()


def _conv_silu_bwd(u, dy, w, b):
    u = u.astype(F32)
    dy = dy.astype(F32)
    row = lax.broadcasted_iota(jnp.int32, u.shape, 0)
    pre = _causal_conv(u, w, row) + b
    s = _sigmoid(pre)
    dpre = dy * (s * (1.0 + pre * (1.0 - s)))
    du, dw = _causal_conv_bwd(u, dpre, w, row)
    return (du,), (dw, jnp.sum(dpre, axis=0, keepdims=True))


def _shortconv_fwd(gb, gc, u, w):
    gb, gc, u = gb.astype(F32), gc.astype(F32), u.astype(F32)
    row = lax.broadcasted_iota(jnp.int32, u.shape, 0)
    return (gb * _causal_conv(gc * u, w, row),), ()


def _shortconv_bwd(gb, gc, u, dy, w):
    gb, gc, u, dy = gb.astype(F32), gc.astype(F32), u.astype(F32), dy.astype(F32)
    row = lax.broadcasted_iota(jnp.int32, u.shape, 0)
    v = gc * u
    dgb = dy * _causal_conv(v, w, row)
    dv, dw = _causal_conv_bwd(v, dy * gb, w, row)
    return (dgb, dv * u, dv * gc), (dw,)


def _split3(v):
    hi = v.astype(BF16)
    r1 = v - hi.astype(F32)
    mid = r1.astype(BF16)
    lo = (r1 - mid.astype(F32)).astype(BF16)
    return hi, mid, lo


def _exact_dot(v, m01, dims, v_is_lhs):
    def one(p):
        return (lax.dot_general(p, m01, dims, preferred_element_type=F32) if v_is_lhs
                else lax.dot_general(m01, p, dims, preferred_element_type=F32))
    hi, mid, lo = _split3(v)
    return (one(lo) + one(mid)) + one(hi)


_NN = (((1,), (0,)), ((), ()))
_NT = (((1,), (1,)), ((), ()))
_TN = (((0,), (0,)), ((), ()))


@jax.custom_vjp
def _cumsum_rows(tril, v):
    return _exact_dot(v, tril, _NN, False)


def _cumsum_rows_fwd(tril, v):
    return _cumsum_rows(tril, v), tril


def _cumsum_rows_bwd(tril, ct):
    return None, _exact_dot(ct, tril, _TN, False)


_cumsum_rows.defvjp(_cumsum_rows_fwd, _cumsum_rows_bwd)


@jax.custom_vjp
def _cumsum_lanes(tril, v):
    return _exact_dot(v, tril, _NT, True)


def _cumsum_lanes_fwd(tril, v):
    return _cumsum_lanes(tril, v), tril


def _cumsum_lanes_bwd(tril, ct):
    return None, _exact_dot(ct, tril, _NN, True)


_cumsum_lanes.defvjp(_cumsum_lanes_fwd, _cumsum_lanes_bwd)


@jax.custom_vjp
def _expand(e01, v):
    return _exact_dot(v, e01, _NN, True)


def _expand_fwd(e01, v):
    return _expand(e01, v), e01


def _expand_bwd(e01, ct):
    return None, _exact_dot(ct, e01, _NT, True)


_expand.defvjp(_expand_fwd, _expand_bwd)


def _causal_mask(n):
    li = lax.broadcasted_iota(jnp.int32, (n, n), 0)
    si = lax.broadcasted_iota(jnp.int32, (n, n), 1)
    return si <= li


def _dt_prep(dtc, dtr, bias_r, bias_c, alog_r, alog_c):
    dt_c = _softplus(dtc + bias_r)
    dt_r = _softplus(dtr + bias_c)
    tril = jnp.where(_causal_mask(dtc.shape[0]), 1.0, 0.0).astype(BF16)
    cs_c = _cumsum_rows(tril, dt_c * (-jnp.exp(alog_r)))
    cs_r = _cumsum_lanes(tril, dt_r * (-jnp.exp(alog_c)))
    return dt_c, cs_c, cs_r


def _ssd_chunk(r_heads, xs, bg, cg, dt_c, cs_c, cs_rg, e01, dskip_e, hp):
    l_len, rp = xs.shape
    p = rp // r_heads
    causal = _causal_mask(l_len)
    lane_head = lax.broadcasted_iota(jnp.int32, (1, rp), 1) // p
    dt_e = _expand(e01, dt_c)
    cs_e = _expand(e01, cs_c)
    cl_e = cs_e[l_len - 1:l_len, :]
    x = xs * dt_e
    bgb, cgb = bg.astype(BF16), cg.astype(BF16)
    cb = lax.dot_general(cgb, bgb, _NT, preferred_element_type=F32)
    ms, xm = [], []
    for r in range(r_heads):
        seg = cs_e[:, r * p:r * p + 1] - cs_rg[r:r + 1, :]
        decay = jnp.exp(jnp.where(causal, seg, -1e30))
        ms.append((cb * decay).astype(BF16))
        xm.append(jnp.where(lane_head == r, x, 0.0).astype(BF16))
    y_diag = lax.dot_general(jnp.concatenate(ms, axis=1), jnp.concatenate(xm, axis=0), _NN,
                             preferred_element_type=F32)
    y_off = lax.dot_general(cgb, hp.astype(BF16), _NN, preferred_element_type=F32) * jnp.exp(cs_e)
    xd = (x * jnp.exp(cl_e - cs_e)).astype(BF16)
    states = lax.dot_general(bgb, xd, _TN, preferred_element_type=F32)
    h_next = hp * jnp.exp(cl_e) + states
    y = y_diag + y_off + dskip_e * xs
    return y, h_next


def _ssd_dt(dtc, dtr, small, cots=None):
    t_len, heads = dtc.shape[0], dtr.shape[0]
    nc = t_len // CHUNK
    col = pl.BlockSpec((CHUNK, LANES), lambda c: (c, 0))
    row = pl.BlockSpec((heads, CHUNK), lambda c: (0, c))
    full = [pl.BlockSpec(s.shape, lambda c: (0, 0)) for s in small]
    shapes = [jax.ShapeDtypeStruct((t_len, LANES), F32), jax.ShapeDtypeStruct((t_len, LANES), F32),
              jax.ShapeDtypeStruct((heads, t_len), F32)]
    if cots is None:
        def body(dtc_ref, dtr_ref, br, bc, ar, ac, dt_ref, csc_ref, csr_ref):
            dt_ref[...], csc_ref[...], csr_ref[...] = _dt_prep(dtc_ref[...], dtr_ref[...], br[...], bc[...],
                                                                ar[...], ac[...])
        return _call(body, name="ssd_dt", grid=(nc,), in_specs=[col, row] + full, out_specs=[col, col, row],
                     out_shape=shapes, args=(dtc, dtr, *small), sem=("parallel",))[0]

    g_dt, g_csc, g_csr, ddk, e01 = cots

    def body(dtc_ref, dtr_ref, br, bc, ar, ac, g_dt_ref, g_csc_ref, g_csr_ref, ddk_ref, e_ref,
             ddtc_ref, ddtr_ref, *dsmall):
        _, vjp = jax.vjp(_dt_prep, dtc_ref[...], dtr_ref[...], br[...], bc[...], ar[...], ac[...])
        grads = vjp((g_dt_ref[...], g_csc_ref[...], g_csr_ref[...]))
        ddtc_ref[...], ddtr_ref[...] = grads[0], grads[1]
        ddk8 = jnp.broadcast_to(ddk_ref[...], (8, ddk_ref.shape[1]))
        dskip = _exact_dot(ddk8, e_ref[...], _NT, True)[0:1, :]

        @pl.when(pl.program_id(0) == 0)
        def _():
            for r in dsmall:
                r[...] = jnp.zeros_like(r)

        for r, gr in zip(dsmall, tuple(grads[2:]) + (dskip,)):
            r[...] += gr

    acc = list(small) + [small[0]]
    return _call(body, name="d_ssd_dt", grid=(nc,),
                 in_specs=[col, row] + full + [col, col, row, pl.BlockSpec((None, 1, e01.shape[1]), lambda c: (c, 0, 0)),
                                               pl.BlockSpec(e01.shape, lambda c: (0, 0))],
                 out_specs=[col, row] + [pl.BlockSpec(s.shape, lambda c: (0, 0)) for s in acc],
                 out_shape=[shapes[0], shapes[2]] + [jax.ShapeDtypeStruct(s.shape, F32) for s in acc],
                 args=(dtc, dtr, *small, g_dt, g_csc, g_csr, ddk, e01), sem=("arbitrary",))[0]


def _ssd_specs(t_len, d_ssm, r_heads, reverse):
    rp = r_heads * HEADDIM
    nc = t_len // CHUNK
    cidx = (lambda c: nc - 1 - c) if reverse else (lambda c: c)
    b_off = d_ssm // N_STATE
    specs = dict(
        xs=pl.BlockSpec((CHUNK, rp), lambda c, g: (cidx(c), g)),
        b=pl.BlockSpec((CHUNK, N_STATE), lambda c, g: (cidx(c), b_off + g)),
        c=pl.BlockSpec((CHUNK, N_STATE), lambda c, g: (cidx(c), b_off + N_GROUPS + g)),
        col=pl.BlockSpec((CHUNK, LANES), lambda c, g: (cidx(c), 0)),
        csr=pl.BlockSpec((None, r_heads, CHUNK), lambda c, g: (g, 0, cidx(c))),
        e01=pl.BlockSpec((LANES, rp), lambda c, g: (0, g)),
        dskip=pl.BlockSpec((1, rp), lambda c, g: (0, g)),
        hprev=pl.BlockSpec((None, None, N_STATE, rp), lambda c, g: (cidx(c), g, 0, 0)),
    )
    return specs, nc, rp


def _ssd_fwd(xbc, dt_c, cs_c, cs_r3, e01, dskip_e, *, d_ssm, r_heads, ride=None):
    t_len = xbc.shape[0]
    sp, nc, rp = _ssd_specs(t_len, d_ssm, r_heads, False)

    def body(xs_ref, b_ref, c_ref, dt_ref, csc_ref, csr_ref, e_ref, dk_ref, y_ref, hprev_ref, h_ref):
        c, g = pl.program_id(0), pl.program_id(1)

        @pl.when(c == 0)
        def _():
            h_ref[g] = jnp.zeros((N_STATE, rp), F32)

        hp = h_ref[g]
        hprev_ref[...] = hp
        y, hn = _ssd_chunk(r_heads, xs_ref[...].astype(F32), b_ref[...].astype(F32), c_ref[...].astype(F32),
                           dt_ref[...], csc_ref[...], csr_ref[...], e_ref[...], dk_ref[...], hp)
        y_ref[...] = y
        h_ref[g] = hn

    return _call(
        body, name="ssd_fwd", grid=(nc, N_GROUPS),
        in_specs=[sp["xs"], sp["b"], sp["c"], sp["col"], sp["col"], sp["csr"], sp["e01"], sp["dskip"]],
        out_specs=[pl.BlockSpec((CHUNK, rp), lambda c, g: (c, g)), sp["hprev"]],
        out_shape=[jax.ShapeDtypeStruct((t_len, d_ssm), F32),
                   jax.ShapeDtypeStruct((nc, N_GROUPS, N_STATE, rp), F32)],
        args=(xbc, xbc, xbc, dt_c, cs_c, cs_r3, e01, dskip_e), scratch=[pltpu.VMEM((N_GROUPS, N_STATE, rp), F32)],
        sem=("arbitrary", "arbitrary"), ride=ride)


def _ssd_bwd(xbc, dt_c, cs_c, cs_r3, e01, dskip_e, hprev, dy, *, d_ssm, r_heads, ride=None):
    t_len = xbc.shape[0]
    sp, nc, rp = _ssd_specs(t_len, d_ssm, r_heads, True)
    rev = lambda c: nc - 1 - c

    def body(xs_ref, b_ref, c_ref, dt_ref, csc_ref, csr_ref, e_ref, dk_ref, hprev_ref, dy_ref,
             dxs_ref, db_ref, dc_ref, ddt_ref, dcsc_ref, dcsr_ref, ddk_ref, dh_ref):
        c, g = pl.program_id(0), pl.program_id(1)

        @pl.when(c == 0)
        def _():
            dh_ref[g] = jnp.zeros((N_STATE, rp), F32)

        @pl.when(g == 0)
        def _():
            ddt_ref[...] = jnp.zeros_like(ddt_ref)
            dcsc_ref[...] = jnp.zeros_like(dcsc_ref)

        e01 = e_ref[...]
        fn = lambda xs, bg, cg, dt, csc, csr, dk, hp: _ssd_chunk(r_heads, xs, bg, cg, dt, csc, csr, e01, dk, hp)
        _, vjp = jax.vjp(fn, xs_ref[...].astype(F32), b_ref[...].astype(F32), c_ref[...].astype(F32), dt_ref[...],
                         csc_ref[...], csr_ref[...], dk_ref[...], hprev_ref[...])
        dxs, dbg, dcg, ddt, dcsc, dcsr, ddk, dhp = vjp((dy_ref[...], dh_ref[g]))
        dxs_ref[...] = dxs.astype(dxs_ref.dtype)
        db_ref[...] = dbg.astype(db_ref.dtype)
        dc_ref[...] = dcg.astype(dc_ref.dtype)
        ddt_ref[...] += ddt
        dcsc_ref[...] += dcsc
        dcsr_ref[...] = dcsr
        ddk_ref[...] = ddk
        dh_ref[g] = dhp

    n_bc = N_GROUPS * N_STATE
    return _call(
        body, name="ssd_bwd", grid=(nc, N_GROUPS),
        in_specs=[sp["xs"], sp["b"], sp["c"], sp["col"], sp["col"], sp["csr"], sp["e01"], sp["dskip"], sp["hprev"],
                  pl.BlockSpec((CHUNK, rp), lambda c, g: (rev(c), g))],
        out_specs=[pl.BlockSpec((CHUNK, rp), lambda c, g: (rev(c), g)),
                   pl.BlockSpec((CHUNK, N_STATE), lambda c, g: (rev(c), g)),
                   pl.BlockSpec((CHUNK, N_STATE), lambda c, g: (rev(c), g)),
                   sp["col"], sp["col"], sp["csr"],
                   pl.BlockSpec((None, 1, rp), lambda c, g: (rev(c), 0, g))],
        out_shape=[jax.ShapeDtypeStruct((t_len, d_ssm), BF16), jax.ShapeDtypeStruct((t_len, n_bc), BF16),
                   jax.ShapeDtypeStruct((t_len, n_bc), BF16), jax.ShapeDtypeStruct(dt_c.shape, F32),
                   jax.ShapeDtypeStruct(cs_c.shape, F32), jax.ShapeDtypeStruct(cs_r3.shape, F32),
                   jax.ShapeDtypeStruct((nc, 1, d_ssm), F32)],
        args=(xbc, xbc, xbc, dt_c, cs_c, cs_r3, e01, dskip_e, hprev, dy),
        scratch=[pltpu.VMEM((N_GROUPS, N_STATE, rp), F32)], sem=("arbitrary", "arbitrary"), ride=ride)


def _chip_sum(src, sib, *, name):
    rows, cols = src.shape[1:]
    tr = _tile(rows, 256, BF16_ROWS)
    core = lax.axis_index("c").astype(jnp.int32).reshape(1)

    def body(c_ref, a_ref, b_ref, o_ref):
        o_ref[...] = (a_ref[...].astype(F32) + b_ref[...].astype(F32)).astype(o_ref.dtype)

    grid_spec = pltpu.PrefetchScalarGridSpec(
        num_scalar_prefetch=1, grid=(N_CHIPS, rows // tr),
        in_specs=[pl.BlockSpec((None, tr, cols), lambda q, i, c_ref: (2 * q + c_ref[0], i, 0)),
                  pl.BlockSpec((None, tr, cols), lambda q, i, c_ref: (q, i, 0))],
        out_specs=pl.BlockSpec((None, tr, cols), lambda q, i, c_ref: (q, i, 0)))
    return pl.pallas_call(
        body, name=name, grid_spec=grid_spec, out_shape=jax.ShapeDtypeStruct(sib.shape, sib.dtype),
        compiler_params=pltpu.CompilerParams(dimension_semantics=("parallel", "parallel"), vmem_limit_bytes=VMEM_LIMIT),
    )(core, src, sib)


def _adamw(w, g, m, v):
    m = ADAM_B1 * m + (1.0 - ADAM_B1) * g
    v = ADAM_B2 * v + (1.0 - ADAM_B2) * (g * g)
    m_hat = m / (1.0 - ADAM_B1 ** ADAM_STEP)
    v_hat = v / (1.0 - ADAM_B2 ** ADAM_STEP)
    delta = -ADAM_LR * (m_hat / (jnp.sqrt(v_hat) + ADAM_EPS) + ADAM_WD * w)
    return delta, m, v


def _reduce_adamw(parts, w, m, v, *, name):
    n_parts = parts.shape[0]
    rows, cols = w.shape
    tr = _tile(rows, 128, BF16_ROWS)

    def body(p_ref, w_ref, m_ref, v_ref, g_ref, d_ref, mo_ref, vo_ref):
        g = p_ref[0].astype(F32)
        for k in range(1, n_parts):
            g = g + p_ref[k].astype(F32)
        delta, mn, vn = _adamw(w_ref[...], g, m_ref[...], v_ref[...])
        g_ref[...] = g
        d_ref[...] = delta
        mo_ref[...] = mn
        vo_ref[...] = vn

    spec = pl.BlockSpec((tr, cols), lambda i: (i, 0))
    outs, _ = _call(
        body, name=name, grid=(rows // tr,),
        in_specs=[pl.BlockSpec((n_parts, tr, cols), lambda i: (0, i, 0)), spec, spec, spec],
        out_specs=[spec] * 4, out_shape=[jax.ShapeDtypeStruct((rows, cols), F32)] * 4,
        args=(parts, w, m, v), sem=("parallel",))
    return outs


def _cols_of(g):
    return jnp.transpose(g, (1, 0, 2)).reshape(g.shape[1], -1)


def _pad_to(a, rows, cols):
    return jnp.pad(a, ((0, rows - a.shape[0]), (0, cols - a.shape[1])))


def kernel(x, norm_mix_g, w_in, ssm_conv_w, ssm_conv_b, ssm_dt_bias, ssm_A_log, ssm_D, ssm_norm_g, sc_conv_w, w_out, norm_ffn_g, w_gate, w_up, w_down, norm_final_g, loss_target, m_norm_mix_g, m_w_in, m_ssm_conv_w, m_ssm_conv_b, m_ssm_dt_bias, m_ssm_A_log, m_ssm_D, m_ssm_norm_g, m_sc_conv_w, m_w_out, m_norm_ffn_g, m_w_gate, m_w_up, m_w_down, m_norm_final_g, v_norm_mix_g, v_w_in, v_ssm_conv_w, v_ssm_conv_b, v_ssm_dt_bias, v_ssm_A_log, v_ssm_D, v_ssm_norm_g, v_sc_conv_w, v_w_out, v_norm_ffn_g, v_w_gate, v_w_up, v_w_down, v_norm_final_g):
    t_len, d = x.shape[1], x.shape[2]
    heads = d // HEADDIM
    r_heads = heads // N_GROUPS
    d_xbc = d + 2 * N_GROUPS * N_STATE
    ff_s = w_down.shape[1]
    ff = ff_s * N_DEV
    off_xbc, off_dt = d, d + d_xbc
    off_cb = off_dt + heads
    d_in = off_cb + 3 * d
    in_s = d_in // N_DEV
    in_p = -(-in_s // (2 * BF16_ROWS)) * (2 * BF16_ROWS)
    w_main = 4 * d + d_xbc
    me = 4 * lax.axis_index("x") + 2 * lax.axis_index("y") + lax.axis_index("c")

    x2 = x[0]
    target = loss_target[0]

    tpose = lambda a: jnp.transpose(a[0])
    win_s = _pad_to(tpose(w_in).astype(BF16), in_p, d)
    wg_s, wu_s = tpose(w_gate).astype(BF16), tpose(w_up).astype(BF16)
    wo_s, wd_s = w_out[0].astype(BF16), w_down[0].astype(BF16)
    small_w = jnp.concatenate([_pad_to(ssm_conv_w[0], K_SSM, d_xbc // N_DEV),
                               _pad_to(sc_conv_w[0], K_SC + 1, d_xbc // N_DEV)], axis=0)

    g1, g2, g3 = norm_mix_g, norm_ffn_g, norm_final_g.reshape(1, d)
    gs = ssm_norm_g
    small = [_pad_to(ssm_dt_bias, 1, LANES), ssm_dt_bias.reshape(heads, 1), _pad_to(ssm_A_log, 1, LANES),
             ssm_A_log.reshape(heads, 1)]
    e01 = (lax.broadcasted_iota(jnp.int32, (LANES, d), 1) // HEADDIM
           == lax.broadcasted_iota(jnp.int32, (LANES, d), 0)).astype(BF16)
    dskip_e = jnp.repeat(ssm_D, HEADDIM, axis=1)
    tr = _tile(t_len, 256, 8)
    tr_ff = _tile(t_len, 128, 8)
    cw = LANES
    slab = lambda col: col // cw

    gin_1, gsm_1 = _comm(_gather_chips([win_s, small_w]), "gather_w_in_chips")
    (n1,), (gin, gsm) = _rows_call(lambda v, g: ((_rms(v, g),), ()), rows=t_len, tr=tr, row_ins=[(x2, d, 0)],
                                   full_ins=[g1], row_outs=[(d, BF16)], acc_outs=[], name="norm_mix",
                                   ride=_gather_sibling([gin_1, gsm_1]))
    win_all = gin[:, :in_s].reshape(d_in, d)
    wtm = jnp.concatenate([win_all[:off_dt], win_all[off_cb:]], axis=0)
    wtdt = _pad_to(win_all[off_dt:off_cb], LANES, d)
    cw_ssm = _cols_of(gsm[:, :K_SSM, :])
    cw_sc = _cols_of(gsm[:, K_SSM:K_SSM + K_SC, :d // N_DEV])

    proj, (go_1, gg_1) = _matmul(n1, wtm, tb=True, out_dtype=BF16, name="proj_main",
                                 ride=_gather_chips([wo_s, wg_s]))
    dt_raw, _ = _matmul(n1, wtdt, tb=True, out_dtype=F32, name="proj_dt")
    dt_raw_t = jnp.transpose(dt_raw[:, :heads])
    (xbc,), (go, gg) = _cols_call(_conv_silu_fwd, rows=t_len, cols=d_xbc, cw=cw, col_ins=[(proj, slab(off_xbc))],
                                  par_ins=[(cw_ssm, 0), (ssm_conv_b, 0)], col_outs=[BF16], par_outs=[],
                                  name="ssm_conv", ride=_gather_sibling([go_1, gg_1]))
    dt_c, cs_c, cs_r = _ssd_dt(dt_raw, dt_raw_t, small)
    cs_r3 = cs_r.reshape(N_GROUPS, r_heads, t_len)
    (y_ssd, hprev), (gu_1,) = _ssd_fwd(xbc, dt_c, cs_c, cs_r3, e01, dskip_e, d_ssm=d, r_heads=r_heads,
                                       ride=_gather_chips([wu_s]))

    def gate_norm(y, z, g):
        z = z.astype(F32)
        return _rms(y * (z * _sigmoid(z)), g)

    (y_ssm,), (gu,) = _rows_call(lambda y, z, g: ((gate_norm(y, z, g),), ()), rows=t_len, tr=tr,
                                 row_ins=[(y_ssd, d, 0), (proj, d, 0)], full_ins=[gs], row_outs=[(d, BF16)],
                                 acc_outs=[], name="ssm_gate_norm", ride=_gather_sibling([gu_1]))
    wgt, wut, wo = gg.reshape(ff, d), gu.reshape(ff, d), go.reshape(2 * d, d)
    sc0 = slab(d + d_xbc)
    (y_sc,), _ = _cols_call(_shortconv_fwd, rows=t_len, cols=d, cw=cw,
                            col_ins=[(proj, sc0), (proj, sc0 + slab(d)), (proj, sc0 + 2 * slab(d))],
                            par_ins=[(cw_sc, 0)], col_outs=[BF16], par_outs=[], name="shortconv")
    y_mix = jnp.concatenate([y_ssm, y_sc], axis=1)
    h1, _ = _matmul(y_mix, wo, out_dtype=F32, add=x2, name="out_proj")
    (n2,), _ = _rows_call(lambda v, g: ((_rms(v, g),), ()), rows=t_len, tr=tr, row_ins=[(h1, d, 0)], full_ins=[g2],
                          row_outs=[(d, BF16)], acc_outs=[], name="norm_ffn")
    g_ff, (gd_1,) = _matmul(n2, wgt, tb=True, out_dtype=BF16, name="ffn_gate",
                            ride=_gather_chips([wd_s]))
    u_ff, (gd,) = _matmul(n2, wut, tb=True, out_dtype=BF16, name="ffn_up", ride=_gather_sibling([gd_1]))
    wd = gd.reshape(ff, d)

    def act(gv, uv):
        gv, uv = gv.astype(F32), uv.astype(F32)
        return ((gv * _sigmoid(gv) * uv,), ())

    (a_ff,), _ = _rows_call(act, rows=t_len, tr=tr_ff, row_ins=[(g_ff, ff, 0), (u_ff, ff, 0)], full_ins=[],
                            row_outs=[(ff, BF16)], acc_outs=[], name="ffn_act")
    h2, _ = _matmul(a_ff, wd, out_dtype=F32, add=h1, name="ffn_down")

    def head(hv, tv, g):
        def f(hh, gg_):
            e = _rms(hh, gg_) - tv
            return (0.5 / d) * jnp.sum(e * e)
        val, (dh, dg) = jax.value_and_grad(f, argnums=(0, 1))(hv, g)
        return (dh, dh), (jnp.full((1, LANES), val, F32), dg)

    (dh2, dh2_b, loss_acc, dg3), _ = _rows_call(head, rows=t_len, tr=tr, row_ins=[(h2, d, 0), (target, d, 0)],
                                                full_ins=[g3], row_outs=[(d, F32), (d, BF16)],
                                                acc_outs=[(1, LANES), (1, d)], name="loss_head")
    loss = lax.psum(loss_acc[0, 0], ("x", "y", "c"))

    da, _ = _matmul(dh2_b, wd, tb=True, out_dtype=BF16, name="d_ffn_act")
    dwd, _ = _matmul(a_ff, dh2_b, ta=True, out_dtype=BF16, name="d_w_down")
    dwd8 = dwd.reshape(N_DEV, ff_s, d)

    def act_bwd(dav, gv, uv):
        dav, gv, uv = dav.astype(F32), gv.astype(F32), uv.astype(F32)
        s = _sigmoid(gv)
        return ((dav * uv * (s * (1.0 + gv * (1.0 - s))), dav * gv * s), ())

    (dg_ff, du_ff), (sib_d,) = _rows_call(act_bwd, rows=t_len, tr=tr_ff,
                                          row_ins=[(da, ff, 0), (g_ff, ff, 0), (u_ff, ff, 0)], full_ins=[],
                                          row_outs=[(ff, BF16), (ff, BF16)], acc_outs=[], name="d_ffn_gate_up",
                                          ride=_scatter_sibling([dwd8]))
    chip_d = _chip_sum(dwd8, sib_d, name="chip_sum_w_down")
    dn2, (parts_d,) = _matmul(dg_ff, wgt, out_dtype=F32, name="d_norm_ffn_out_gate", ride=_scatter_chips([chip_d]))
    dn2, _ = _matmul(du_ff, wut, out_dtype=F32, add=dn2, name="d_norm_ffn_out_up")
    dwg, _ = _matmul(dg_ff, n2, ta=True, out_dtype=BF16, name="d_w_gate")
    dwu, _ = _matmul(du_ff, n2, ta=True, out_dtype=BF16, name="d_w_up")
    dwg8, dwu8 = dwg.reshape(N_DEV, ff_s, d), dwu.reshape(N_DEV, ff_s, d)

    def norm_bwd(v, dn, dres, g):
        _, vjp = jax.vjp(_rms, v, g)
        dv, dg = vjp(dn)
        return (dv + dres,), (dg,)

    def norm_bwd_2(v, dn, dres, g):
        (dv,), acc = norm_bwd(v, dn, dres, g)
        return (dv, dv), acc

    (dh1, dh1_b, dg2), (sib_g, sib_u) = _rows_call(norm_bwd_2, rows=t_len, tr=tr,
                                                   row_ins=[(h1, d, 0), (dn2, d, 0), (dh2, d, 0)], full_ins=[g2],
                                                   row_outs=[(d, F32), (d, BF16)], acc_outs=[(1, d)], name="d_norm_ffn",
                                                   ride=_scatter_sibling([dwg8, dwu8]))
    chip_g = _chip_sum(dwg8, sib_g, name="chip_sum_w_gate")
    chip_u = _chip_sum(dwu8, sib_u, name="chip_sum_w_up")

    dy_mix, _ = _matmul(dh1_b, wo, tb=True, out_dtype=BF16, name="d_y_mix")
    dwo, _ = _matmul(y_mix, dh1_b, ta=True, out_dtype=BF16, name="d_w_out")
    dwo8 = dwo.reshape(N_DEV, 2 * d // N_DEV, d)
    (dgb, dgc, du, dcw_sc), (sib_o,) = _cols_call(
        _shortconv_bwd, rows=t_len, cols=d, cw=cw,
        col_ins=[(proj, sc0), (proj, sc0 + slab(d)), (proj, sc0 + 2 * slab(d)), (dy_mix, slab(d))],
        par_ins=[(cw_sc, 0)], col_outs=[BF16] * 3, par_outs=[K_SC], name="d_shortconv",
        ride=_scatter_sibling([dwo8]))
    chip_o = _chip_sum(dwo8, sib_o, name="chip_sum_w_out")

    def gate_norm_bwd(y, z, dyo, g):
        _, vjp = jax.vjp(gate_norm, y, z.astype(F32), g)
        dy, dz, dg = vjp(dyo.astype(F32))
        return (dy, dz), (dg,)

    (dy_ssd, dz, dgs), _ = _rows_call(gate_norm_bwd, rows=t_len, tr=tr,
                                      row_ins=[(y_ssd, d, 0), (proj, d, 0), (dy_mix, d, 0)], full_ins=[gs],
                                      row_outs=[(d, F32), (d, BF16)], acc_outs=[(1, d)], name="d_ssm_gate_norm")
    (dxs, dbm, dcm, g_dt, g_csc, g_csr3, ddk), (parts_g, parts_u, parts_o) = _ssd_bwd(
        xbc, dt_c, cs_c, cs_r3, e01, dskip_e, hprev, dy_ssd, d_ssm=d, r_heads=r_heads,
        ride=_scatter_chips([chip_g, chip_u, chip_o]))
    ddt_c, ddt_r, dbias_r, dbias_c, dalog_r, dalog_c, ddskip = _ssd_dt(
        dt_raw, dt_raw_t, small, cots=(g_dt, g_csc, g_csr3.reshape(heads, t_len), ddk, e01))
    dxbc = jnp.concatenate([dxs, dbm, dcm], axis=1)
    (dxbc_pre, dcw_ssm, dcb_ssm), _ = _cols_call(
        _conv_silu_bwd, rows=t_len, cols=d_xbc, cw=cw, col_ins=[(proj, slab(off_xbc)), (dxbc, 0)],
        par_ins=[(cw_ssm, 0), (ssm_conv_b, 0)], col_outs=[BF16], par_outs=[K_SSM, 1], name="d_ssm_conv")
    dproj = jnp.concatenate([dz, dxbc_pre, dgb, dgc, du], axis=1)
    ddt = ddt_c + _pad_to(jnp.transpose(ddt_r), t_len, LANES)
    dwm, _ = _matmul(dproj, n1, ta=True, out_dtype=BF16, name="d_w_in_main")
    dwdt, _ = _matmul(ddt, n1, ta=True, out_dtype=BF16, name="d_w_in_dt")
    dwin_all = jnp.concatenate([dwm[:off_dt], dwdt[:heads], dwm[off_dt:]], axis=0)
    dwin8 = jnp.pad(dwin_all.reshape(N_DEV, in_s, d), ((0, 0), (0, in_p - in_s), (0, 0)))
    dn1, (sib_in,) = _matmul(ddt, wtdt, out_dtype=F32, name="d_norm_mix_out_dt", ride=_scatter_sibling([dwin8]))
    chip_in = _chip_sum(dwin8, sib_in, name="chip_sum_w_in")
    dn1, (parts_in,) = _matmul(dproj, wtm, out_dtype=F32, add=dn1, name="d_norm_mix_out",
                               ride=_scatter_chips([chip_in]))
    (dx, dg1), _ = _rows_call(norm_bwd, rows=t_len, tr=tr, row_ins=[(x2, d, 0), (dn1, d, 0), (dh1, d, 0)],
                              full_ins=[g1], row_outs=[(d, F32)], acc_outs=[(1, d)], name="d_norm_mix")

    wide = d_xbc
    rows_small = [dg1, dcb_ssm, dbias_r + _pad_to(dbias_c.reshape(1, heads), 1, LANES),
                  dalog_r + _pad_to(dalog_c.reshape(1, heads), 1, LANES), ddskip, dgs, dg2, dg3]
    packed = jnp.concatenate([_pad_to(r, 1, wide) for r in rows_small]
                             + [dcw_ssm, _pad_to(dcw_sc, K_SC, wide), jnp.zeros((1, wide), F32)], axis=0)
    (p_small,) = _comm(_gather_all([packed]), "gather_small_grads")

    conv_lo = me * (d_xbc // N_DEV)
    sc_lo = me * (d // N_DEV)

    def pack_state(vals):
        (nm, cb, dtb, al, dk, sg, nf, nfin, cws, scs) = vals
        rows = [_pad_to(a.reshape(1, -1), 1, wide) for a in (nm, cb, dtb, al, dk, sg, nf, nfin)]
        cws_full = lax.dynamic_update_slice(jnp.zeros((K_SSM, wide), F32), cws[0], (0, conv_lo))
        scs_full = lax.dynamic_update_slice(jnp.zeros((K_SC, wide), F32), scs[0], (0, sc_lo))
        return jnp.concatenate(rows + [cws_full, scs_full, jnp.zeros((1, wide), F32)], axis=0)

    w_small = pack_state((norm_mix_g, ssm_conv_b, ssm_dt_bias, ssm_A_log, ssm_D, ssm_norm_g, norm_ffn_g, norm_final_g,
                          ssm_conv_w, sc_conv_w))
    m_small = pack_state((m_norm_mix_g, m_ssm_conv_b, m_ssm_dt_bias, m_ssm_A_log, m_ssm_D, m_ssm_norm_g, m_norm_ffn_g,
                          m_norm_final_g, m_ssm_conv_w, m_sc_conv_w))
    v_small = pack_state((v_norm_mix_g, v_ssm_conv_b, v_ssm_dt_bias, v_ssm_A_log, v_ssm_D, v_ssm_norm_g, v_norm_ffn_g,
                          v_norm_final_g, v_ssm_conv_w, v_sc_conv_w))

    tin = lambda a: _pad_to(tpose(a), in_p, d)
    tin_back = lambda a: jnp.transpose(a[:in_s])[None]
    t_back = lambda a: jnp.transpose(a)[None]
    upd = {
        "w_in": [tin_back(o) for o in _reduce_adamw(parts_in, tin(w_in), tin(m_w_in), tin(v_w_in), name="adamw_w_in")],
        "w_out": [o[None] for o in _reduce_adamw(parts_o, w_out[0], m_w_out[0], v_w_out[0], name="adamw_w_out")],
        "w_gate": [t_back(o) for o in _reduce_adamw(parts_g, tpose(w_gate), tpose(m_w_gate), tpose(v_w_gate),
                                                    name="adamw_w_gate")],
        "w_up": [t_back(o) for o in _reduce_adamw(parts_u, tpose(w_up), tpose(m_w_up), tpose(v_w_up),
                                                  name="adamw_w_up")],
        "w_down": [o[None] for o in _reduce_adamw(parts_d, w_down[0], m_w_down[0], v_w_down[0], name="adamw_w_down")],
    }
    small_upd = _reduce_adamw(p_small, w_small, m_small, v_small, name="adamw_small")

    def unpack(packed_out):
        vec = lambda i, n, shape: packed_out[i, :n].reshape(shape)
        return {
            "norm_mix_g": vec(0, d, (1, d)), "ssm_conv_b": vec(1, d_xbc, (1, d_xbc)),
            "ssm_dt_bias": vec(2, heads, (1, heads)), "ssm_A_log": vec(3, heads, (1, heads)),
            "ssm_D": vec(4, heads, (1, heads)), "ssm_norm_g": vec(5, d, (1, d)), "norm_ffn_g": vec(6, d, (1, d)),
            "norm_final_g": vec(7, d, (d,)),
            "ssm_conv_w": lax.dynamic_slice(packed_out[8:8 + K_SSM], (0, conv_lo), (K_SSM, d_xbc // N_DEV))[None],
            "sc_conv_w": lax.dynamic_slice(packed_out[8 + K_SSM:8 + K_SSM + K_SC], (0, sc_lo), (K_SC, d // N_DEV))[None],
        }

    names = ["norm_mix_g", "w_in", "ssm_conv_w", "ssm_conv_b", "ssm_dt_bias", "ssm_A_log", "ssm_D", "ssm_norm_g",
             "sc_conv_w", "w_out", "norm_ffn_g", "w_gate", "w_up", "w_down", "norm_final_g"]
    outs = []
    for kind in range(4):
        small_k = unpack(small_upd[kind])
        for nm in names:
            outs.append(upd[nm][kind] if nm in upd else small_k[nm])
    return (loss, dx[None], *outs)
```

```python
import collections
import functools

import jax
import jax.numpy as jnp
from jax import lax
from jax.experimental import pallas as pl
from jax.experimental.pallas import tpu as pltpu

F32 = jnp.float32
BF16 = jnp.bfloat16

N_DEV = 8
N_CHIPS = 4
HEADDIM = 64
N_GROUPS = 8
N_STATE = 128
CHUNK = 128
K_SSM = 4
K_SC = 3
EPS = 1e-5
LANES = 128
BF16_ROWS = 16
MM_TILE_MN = 1408
MM_TILE_K = 2048
SSD_CHUNKS_PER_STEP = 4
V7X_VMEM_BYTES = 64 * 1024 * 1024
VMEM_LIMIT = (V7X_VMEM_BYTES * 3) // 4

ADAM_LR = 0.001
ADAM_B1 = 0.9
ADAM_B2 = 0.999
ADAM_EPS = 1e-08
ADAM_WD = 0.01
ADAM_STEP = 10


def _tile(n, pref, align):
    t = min(pref, n)
    t -= t % align
    while t >= align:
        if n % t == 0:
            return t
        t -= align
    return n


_Ride = collections.namedtuple("_Ride", ["ins", "out_shapes", "aliases", "nsem", "plan"])
_ANY = pl.BlockSpec(memory_space=pl.ANY)


def _coords():
    return lax.axis_index("x"), lax.axis_index("y"), lax.axis_index("c")


def _other_chips(x, y):
    return ((1 - x, y), (x, 1 - y), (1 - x, 1 - y))


def _remote(src, dst, send, recv, k, dev):
    return functools.partial(pltpu.make_async_remote_copy, src_ref=src, dst_ref=dst, send_sem=send.at[k],
                             recv_sem=recv.at[k], device_id=dev, device_id_type=pl.DeviceIdType.MESH)


def _local(src, dst, sem):
    return functools.partial(pltpu.make_async_copy, src, dst, sem)


def _start_all(plan):
    for kind, make in plan:
        if kind != "arrival":
            make().start()


def _wait_all(plan):
    for kind, make in plan:
        if kind == "local":
            make().wait()
        elif kind == "out":
            make().wait_send()
        else:
            make().wait_recv()


def _gather_chips(srcs):
    def plan(ins, outs, send, recv, base):
        x, y, c = _coords()
        me = 4 * x + 2 * y + c
        d = []
        for a, (src, dst) in enumerate(zip(ins, outs)):
            k = base + 4 * a
            d.append(("local", _local(src, dst.at[me], send.at[k + 3])))
            for j, (px, py) in enumerate(_other_chips(x, y)):
                d.append(("out", _remote(src, dst.at[me], send, recv, k + j, (px, py, c))))
                d.append(("arrival", _remote(src, dst.at[4 * px + 2 * py + c], send, recv, k + j, (px, py, c))))
        return d
    shapes = [jax.ShapeDtypeStruct((N_DEV,) + s.shape, s.dtype) for s in srcs]
    return _Ride(list(srcs), shapes, {}, 4 * len(srcs), plan)


def _gather_sibling(bufs):
    def plan(ins, outs, send, recv, base):
        x, y, c = _coords()
        d = []
        for a, buf in enumerate(outs):
            for q in range(N_CHIPS):
                k = base + 4 * a + q
                d.append(("out", _remote(buf.at[2 * q + c], buf.at[2 * q + c], send, recv, k, (x, y, 1 - c))))
                d.append(("arrival", _remote(buf.at[2 * q + c], buf.at[2 * q + 1 - c], send, recv, k, (x, y, 1 - c))))
        return d
    shapes = [jax.ShapeDtypeStruct(b.shape, b.dtype) for b in bufs]
    return _Ride(list(bufs), shapes, {i: i for i in range(len(bufs))}, 4 * len(bufs), plan)


def _scatter_sibling(srcs):
    def plan(ins, outs, send, recv, base):
        x, y, c = _coords()
        d = []
        for a, (src, sib) in enumerate(zip(ins, outs)):
            for q in range(N_CHIPS):
                k = base + 4 * a + q
                d.append(("out", _remote(src.at[2 * q + 1 - c], sib.at[q], send, recv, k, (x, y, 1 - c))))
                d.append(("arrival", _remote(src.at[2 * q + 1 - c], sib.at[q], send, recv, k, (x, y, 1 - c))))
        return d
    shapes = [jax.ShapeDtypeStruct((N_CHIPS,) + s.shape[1:], s.dtype) for s in srcs]
    return _Ride(list(srcs), shapes, {}, 4 * len(srcs), plan)


def _scatter_chips(chips):
    def plan(ins, outs, send, recv, base):
        x, y, c = _coords()
        mine = 2 * x + y
        d = []
        for a, (chip, parts) in enumerate(zip(ins, outs)):
            k = base + 4 * a
            d.append(("local", _local(chip.at[mine], parts.at[mine], send.at[k + 3])))
            for j, (px, py) in enumerate(_other_chips(x, y)):
                q = 2 * px + py
                d.append(("out", _remote(chip.at[q], parts.at[mine], send, recv, k + j, (px, py, c))))
                d.append(("arrival", _remote(chip.at[q], parts.at[q], send, recv, k + j, (px, py, c))))
        return d
    shapes = [jax.ShapeDtypeStruct(s.shape, s.dtype) for s in chips]
    return _Ride(list(chips), shapes, {}, 4 * len(chips), plan)


def _gather_all(srcs):
    def plan(ins, outs, send, recv, base):
        x, y, c = _coords()
        me = 4 * x + 2 * y + c
        d = []
        for a, (src, dst) in enumerate(zip(ins, outs)):
            k = base + N_DEV * a
            d.append(("local", _local(src, dst.at[me], send.at[k])))
            for j in range(1, N_DEV):
                px = 1 - x if (j >> 2) & 1 else x
                py = 1 - y if (j >> 1) & 1 else y
                pc = 1 - c if j & 1 else c
                d.append(("out", _remote(src, dst.at[me], send, recv, k + j, (px, py, pc))))
                d.append(("arrival", _remote(src, dst.at[4 * px + 2 * py + pc], send, recv, k + j, (px, py, pc))))
        return d
    shapes = [jax.ShapeDtypeStruct((N_DEV,) + s.shape, s.dtype) for s in srcs]
    return _Ride(list(srcs), shapes, {}, N_DEV * len(srcs), plan)


def _merge(*rides):
    ins, outs, aliases, parts, nsem = [], [], {}, [], 0
    for r in rides:
        parts.append((len(ins), len(outs), nsem, r))
        aliases.update({len(ins) + i: len(outs) + j for i, j in r.aliases.items()})
        ins += r.ins
        outs += r.out_shapes
        nsem += r.nsem

    def plan(i, o, send, recv, base):
        d = []
        for i0, o0, s0, r in parts:
            d += r.plan(i[i0:i0 + len(r.ins)], o[o0:o0 + len(r.out_shapes)], send, recv, base + s0)
        return d
    return _Ride(ins, outs, aliases, nsem, plan)


def _comm(ride, name):
    n_in, n_out = len(ride.ins), len(ride.out_shapes)

    def body(*refs):
        plan = ride.plan(refs[:n_in], refs[n_in:n_in + n_out], refs[-2], refs[-1], 0)
        _start_all(plan)
        _wait_all(plan)

    return pl.pallas_call(
        body, name=name, in_specs=[_ANY] * n_in, out_specs=[_ANY] * n_out, out_shape=ride.out_shapes,
        scratch_shapes=[pltpu.SemaphoreType.DMA((ride.nsem,)), pltpu.SemaphoreType.DMA((ride.nsem,))],
        input_output_aliases=dict(ride.aliases),
        compiler_params=pltpu.CompilerParams(has_side_effects=True),
    )(*ride.ins)


def _call(body, *, name, grid, in_specs, out_specs, out_shape, args, sem, scratch=(), ride=None, base=None):
    params = pltpu.CompilerParams(dimension_semantics=sem, vmem_limit_bytes=VMEM_LIMIT)
    own_aliases = {}
    if base is not None:
        inner, n_host = body, len(args)
        body = lambda *refs: inner(*refs[:n_host], *refs[n_host + 1:])
        own_aliases[n_host] = base[1]
        args, in_specs = tuple(args) + (base[0],), list(in_specs) + [_ANY]
    if ride is None:
        res = pl.pallas_call(body, name=name, grid=grid, in_specs=in_specs, out_specs=out_specs,
                             out_shape=out_shape, scratch_shapes=list(scratch), input_output_aliases=own_aliases,
                             compiler_params=params)(*args)
        return list(res), []
    n_in, n_out, n_scr = len(args), len(out_shape), len(scratch)
    r_in, r_out = len(ride.ins), len(ride.out_shapes)

    def hosted(*refs):
        h_in, rin = refs[:n_in], refs[n_in:n_in + r_in]
        o0 = n_in + r_in
        h_out, rout = refs[o0:o0 + n_out], refs[o0 + n_out:o0 + n_out + r_out]
        s0 = o0 + n_out + r_out
        h_scr, send, recv = refs[s0:s0 + n_scr], refs[s0 + n_scr], refs[s0 + n_scr + 1]
        ids = [pl.program_id(i) for i in range(len(grid))]
        first = functools.reduce(lambda p, q: p & q, [i == 0 for i in ids])
        last = functools.reduce(lambda p, q: p & q, [i == n - 1 for i, n in zip(ids, grid)])

        @pl.when(first)
        def _():
            _start_all(ride.plan(rin, rout, send, recv, 0))

        body(*h_in, *h_out, *h_scr)

        @pl.when(last)
        def _():
            _wait_all(ride.plan(rin, rout, send, recv, 0))

    res = pl.pallas_call(
        hosted, name=name, grid=grid, in_specs=list(in_specs) + [_ANY] * r_in,
        out_specs=list(out_specs) + [_ANY] * r_out, out_shape=list(out_shape) + list(ride.out_shapes),
        scratch_shapes=list(scratch) + [pltpu.SemaphoreType.DMA((ride.nsem,)), pltpu.SemaphoreType.DMA((ride.nsem,))],
        input_output_aliases={**own_aliases, **{n_in + i: n_out + j for i, j in ride.aliases.items()}},
        compiler_params=params,
    )(*args, *ride.ins)
    return list(res[:n_out]), list(res[n_out:])


def _matmul(a, b, *, ta=False, tb=False, out_dtype=BF16, add=None, name, ride=None):
    m = a.shape[1] if ta else a.shape[0]
    k = a.shape[0] if ta else a.shape[1]
    n = b.shape[0] if tb else b.shape[1]
    assert k == (b.shape[1] if tb else b.shape[0])
    tm, tn, tk = _tile(m, MM_TILE_MN, LANES), _tile(n, MM_TILE_MN, LANES), _tile(k, MM_TILE_K, LANES)
    nk = k // tk
    dims = (((0 if ta else 1,), (1 if tb else 0,)), ((), ()))

    def body(*refs):
        a_ref, b_ref = refs[:2]
        add_ref = refs[2] if add is not None else None
        o_ref = refs[3] if add is not None else refs[2]

        def finish(r):
            if add is not None:
                r = r + add_ref[...].astype(F32)
            o_ref[...] = r.astype(o_ref.dtype)

        part = lax.dot_general(a_ref[...].astype(BF16), b_ref[...].astype(BF16), dims, preferred_element_type=F32)
        if nk == 1:
            finish(part)
            return
        acc = refs[-1]
        kk = pl.program_id(2)

        @pl.when(kk == 0)
        def _():
            acc[...] = part

        @pl.when((kk > 0) & (kk < nk - 1))
        def _():
            acc[...] += part

        @pl.when(kk == nk - 1)
        def _():
            finish(acc[...] + part)

    a_spec = (pl.BlockSpec((tk, tm), lambda i, j, kk: (kk, i)) if ta
              else pl.BlockSpec((tm, tk), lambda i, j, kk: (i, kk)))
    b_spec = (pl.BlockSpec((tn, tk), lambda i, j, kk: (j, kk)) if tb
              else pl.BlockSpec((tk, tn), lambda i, j, kk: (kk, j)))
    o_spec = pl.BlockSpec((tm, tn), lambda i, j, kk: (i, j))
    outs, rides = _call(
        body, name=name, grid=(m // tm, n // tn, nk),
        in_specs=[a_spec, b_spec] + ([o_spec] if add is not None else []), out_specs=[o_spec],
        out_shape=[jax.ShapeDtypeStruct((m, n), out_dtype)], args=(a, b) + ((add,) if add is not None else ()),
        scratch=[pltpu.VMEM((tm, tn), F32)] if nk > 1 else [], sem=("parallel", "parallel", "arbitrary"), ride=ride)
    return outs[0], rides


def _rows_call(fn, *, rows, tr, row_ins, full_ins, row_outs, acc_outs, name, ride=None):
    nr, nf, no, na = len(row_ins), len(full_ins), len(row_outs), len(acc_outs)

    def body(*refs):
        vals = [r[...] for r in refs[:nr + nf]]
        outs, accs = fn(*vals)
        for r, v in zip(refs[nr + nf:nr + nf + no], outs):
            r[...] = v.astype(r.dtype)
        if na:
            @pl.when(pl.program_id(0) == 0)
            def _():
                for r in refs[nr + nf + no:]:
                    r[...] = jnp.zeros_like(r)
            for r, v in zip(refs[nr + nf + no:], accs):
                r[...] += v

    in_specs = [pl.BlockSpec((tr, w), functools.partial(lambda cb, i: (i, cb), cb)) for _, w, cb in row_ins]
    in_specs += [pl.BlockSpec(f.shape, lambda i: (0, 0)) for f in full_ins]
    out_specs = [pl.BlockSpec((tr, o[0]), lambda i: (i, 0)) for o in row_outs]
    out_specs += [pl.BlockSpec(s, lambda i: (0, 0)) for s in acc_outs]
    out_shape = [jax.ShapeDtypeStruct((rows, o[-1] if len(o) == 3 else o[0]), o[1]) for o in row_outs]
    out_shape += [jax.ShapeDtypeStruct(s, F32) for s in acc_outs]
    return _call(body, name=name, grid=(rows // tr,), in_specs=in_specs, out_specs=out_specs, out_shape=out_shape,
                 args=tuple(a for a, _, _ in row_ins) + tuple(full_ins), sem=("arbitrary",), ride=ride)


def _cols_call(fn, *, rows, cols, cw, col_ins, par_ins, col_outs, par_outs, name, ride=None, into=None):
    nc, npar = len(col_ins), len(par_ins)

    def body(*refs):
        vals = [r[...] for r in refs[:nc + npar]]
        outs, pouts = fn(*vals)
        for r, v in zip(refs[nc + npar:], tuple(outs) + tuple(pouts)):
            r[...] = v.astype(r.dtype)

    in_specs = [pl.BlockSpec((rows, cw), functools.partial(lambda off, j: (0, off + j), off)) for _, off in col_ins]
    in_specs += [pl.BlockSpec((p.shape[0], cw), functools.partial(lambda off, j: (0, off + j), off))
                 for p, off in par_ins]
    out_specs = [pl.BlockSpec((rows, cw), lambda j: (0, j)) for _ in col_outs]
    out_specs += [pl.BlockSpec((k, cw), lambda j: (0, j)) for k in par_outs]
    out_shape = [jax.ShapeDtypeStruct((rows, cols), dt) for dt in col_outs]
    out_shape += [jax.ShapeDtypeStruct((k, cols), F32) for k in par_outs]
    if into is not None:
        out_specs[0] = pl.BlockSpec((rows, cw), lambda j: (0, into[1] + j))
        out_shape[0] = jax.ShapeDtypeStruct(into[0].shape, into[0].dtype)
    return _call(body, name=name, grid=(cols // cw,), in_specs=in_specs, out_specs=out_specs, out_shape=out_shape,
                 args=tuple(a for a, _ in col_ins) + tuple(p for p, _ in par_ins), sem=("arbitrary",), ride=ride,
                 base=None if into is None else (into[0], 0))


def _sigmoid(v):
    return 1.0 / (1.0 + jnp.exp(-v))


def _softplus(v):
    return jnp.maximum(v, 0.0) + jnp.log(1.0 + jnp.exp(-jnp.abs(v)))


def _rms(v, g):
    return v * lax.rsqrt(jnp.mean(v * v, axis=-1, keepdims=True) + EPS) * g


def _shift_down(v, s, row):
    return jnp.where(row >= s, pltpu.roll(v, s, 0), 0.0)


def _shift_up(v, s, row):
    n = v.shape[0]
    return jnp.where(row < n - s, pltpu.roll(v, n - s, 0), 0.0)


def _causal_conv(u, w, row):
    k_taps = w.shape[0]
    acc = u * w[k_taps - 1:k_taps, :]
    for k in range(k_taps - 1):
        acc = acc + _shift_down(u, k_taps - 1 - k, row) * w[k:k + 1, :]
    return acc


def _causal_conv_bwd(u, dy, w, row):
    k_taps = w.shape[0]
    tap = lax.broadcasted_iota(jnp.int32, w.shape, 0)
    du = dy * w[k_taps - 1:k_taps, :]
    dw = jnp.where(tap == k_taps - 1, jnp.sum(dy * u, axis=0, keepdims=True), 0.0)
    for k in range(k_taps - 1):
        s = k_taps - 1 - k
        du = du + _shift_up(dy, s, row) * w[k:k + 1, :]
        dw = dw + jnp.where(tap == k, jnp.sum(dy * _shift_down(u, s, row), axis=0, keepdims=True), 0.0)
    return du, dw


def _conv_silu_fwd(u, w, b):
    u = u.astype(F32)
    row = lax.broadcasted_iota(jnp.int32, u.shape, 0)
    pre = _causal_conv(u, w, row) + b
    return (pre * _sigmoid(pre),), ()


def _conv_silu_bwd(u, dy, w, b):
    u = u.astype(F32)
    dy = dy.astype(F32)
    row = lax.broadcasted_iota(jnp.int32, u.shape, 0)
    pre = _causal_conv(u, w, row) + b
    s = _sigmoid(pre)
    dpre = dy * (s * (1.0 + pre * (1.0 - s)))
    du, dw = _causal_conv_bwd(u, dpre, w, row)
    return (du,), (dw, jnp.sum(dpre, axis=0, keepdims=True))


def _shortconv_fwd(gb, gc, u, w):
    gb, gc, u = gb.astype(F32), gc.astype(F32), u.astype(F32)
    row = lax.broadcasted_iota(jnp.int32, u.shape, 0)
    return (gb * _causal_conv(gc * u, w, row),), ()


def _shortconv_bwd(gb, gc, u, dy, w):
    gb, gc, u, dy = gb.astype(F32), gc.astype(F32), u.astype(F32), dy.astype(F32)
    row = lax.broadcasted_iota(jnp.int32, u.shape, 0)
    v = gc * u
    dgb = dy * _causal_conv(v, w, row)
    dv, dw = _causal_conv_bwd(v, dy * gb, w, row)
    return (dgb, dv * u, dv * gc), (dw,)


def _split3(v):
    hi = v.astype(BF16)
    r1 = v - hi.astype(F32)
    mid = r1.astype(BF16)
    lo = (r1 - mid.astype(F32)).astype(BF16)
    return hi, mid, lo


def _exact_dot(v, m01, dims, v_is_lhs):
    def one(p):
        return (lax.dot_general(p, m01, dims, preferred_element_type=F32) if v_is_lhs
                else lax.dot_general(m01, p, dims, preferred_element_type=F32))
    hi, mid, lo = _split3(v)
    return (one(lo) + one(mid)) + one(hi)


_NN = (((1,), (0,)), ((), ()))
_NT = (((1,), (1,)), ((), ()))
_TN = (((0,), (0,)), ((), ()))


@jax.custom_vjp
def _cumsum_rows(tril, v):
    return _exact_dot(v, tril, _NN, False)


def _cumsum_rows_fwd(tril, v):
    return _cumsum_rows(tril, v), tril


def _cumsum_rows_bwd(tril, ct):
    return None, _exact_dot(ct, tril, _TN, False)


_cumsum_rows.defvjp(_cumsum_rows_fwd, _cumsum_rows_bwd)


@jax.custom_vjp
def _cumsum_lanes(tril, v):
    return _exact_dot(v, tril, _NT, True)


def _cumsum_lanes_fwd(tril, v):
    return _cumsum_lanes(tril, v), tril


def _cumsum_lanes_bwd(tril, ct):
    return None, _exact_dot(ct, tril, _NN, True)


_cumsum_lanes.defvjp(_cumsum_lanes_fwd, _cumsum_lanes_bwd)


@jax.custom_vjp
def _expand(e01, v):
    return _exact_dot(v, e01, _NN, True)


def _expand_fwd(e01, v):
    return _expand(e01, v), e01


def _expand_bwd(e01, ct):
    return None, _exact_dot(ct, e01, _NT, True)


_expand.defvjp(_expand_fwd, _expand_bwd)


def _causal_mask(n):
    li = lax.broadcasted_iota(jnp.int32, (n, n), 0)
    si = lax.broadcasted_iota(jnp.int32, (n, n), 1)
    return si <= li


def _dt_prep(dtc, dtr, bias_r, bias_c, alog_r, alog_c):
    dt_c = _softplus(dtc + bias_r)
    dt_r = _softplus(dtr + bias_c)
    tril = jnp.where(_causal_mask(dtc.shape[0]), 1.0, 0.0).astype(BF16)
    cs_c = _cumsum_rows(tril, dt_c * (-jnp.exp(alog_r)))
    cs_r = _cumsum_lanes(tril, dt_r * (-jnp.exp(alog_c)))
    return dt_c, cs_c, cs_r


def _ssd_chunk(r_heads, xs, bg, cg, dt_c, cs_c, cs_rg, e01, dskip_e, hp):
    l_len, rp = xs.shape
    p = rp // r_heads
    causal = _causal_mask(l_len)
    lane_head = lax.broadcasted_iota(jnp.int32, (1, rp), 1) // p
    dt_e = _expand(e01, dt_c)
    cs_e = _expand(e01, cs_c)
    cl_e = cs_e[l_len - 1:l_len, :]
    x = xs * dt_e
    bgb, cgb = bg.astype(BF16), cg.astype(BF16)
    cb = lax.dot_general(cgb, bgb, _NT, preferred_element_type=F32)
    ms, xm = [], []
    for r in range(r_heads):
        seg = cs_e[:, r * p:r * p + 1] - cs_rg[r:r + 1, :]
        decay = jnp.exp(jnp.where(causal, seg, -1e30))
        ms.append((cb * decay).astype(BF16))
        xm.append(jnp.where(lane_head == r, x, 0.0).astype(BF16))
    y_diag = lax.dot_general(jnp.concatenate(ms, axis=1), jnp.concatenate(xm, axis=0), _NN,
                             preferred_element_type=F32)
    y_off = lax.dot_general(cgb, hp.astype(BF16), _NN, preferred_element_type=F32) * jnp.exp(cs_e)
    xd = (x * jnp.exp(cl_e - cs_e)).astype(BF16)
    states = lax.dot_general(bgb, xd, _TN, preferred_element_type=F32)
    h_next = hp * jnp.exp(cl_e) + states
    y = y_diag + y_off + dskip_e * xs
    return y, h_next


def _ssd_dt(dtc, dtr, small, cots=None):
    t_len, heads = dtc.shape[0], dtr.shape[0]
    nc = t_len // CHUNK
    col = pl.BlockSpec((CHUNK, LANES), lambda c: (c, 0))
    row = pl.BlockSpec((heads, CHUNK), lambda c: (0, c))
    full = [pl.BlockSpec(s.shape, lambda c: (0, 0)) for s in small]
    shapes = [jax.ShapeDtypeStruct((t_len, LANES), F32), jax.ShapeDtypeStruct((t_len, LANES), F32),
              jax.ShapeDtypeStruct((heads, t_len), F32)]
    if cots is None:
        def body(dtc_ref, dtr_ref, br, bc, ar, ac, dt_ref, csc_ref, csr_ref):
            dt_ref[...], csc_ref[...], csr_ref[...] = _dt_prep(dtc_ref[...], dtr_ref[...], br[...], bc[...],
                                                                ar[...], ac[...])
        return _call(body, name="ssd_dt", grid=(nc,), in_specs=[col, row] + full, out_specs=[col, col, row],
                     out_shape=shapes, args=(dtc, dtr, *small), sem=("parallel",))[0]

    g_dt, g_csc, g_csr, ddk, e01 = cots

    def body(dtc_ref, dtr_ref, br, bc, ar, ac, g_dt_ref, g_csc_ref, g_csr_ref, ddk_ref, e_ref,
             ddtc_ref, ddtr_ref, *dsmall):
        _, vjp = jax.vjp(_dt_prep, dtc_ref[...], dtr_ref[...], br[...], bc[...], ar[...], ac[...])
        grads = vjp((g_dt_ref[...], g_csc_ref[...], g_csr_ref[...]))
        ddtc_ref[...], ddtr_ref[...] = grads[0], grads[1]
        ddk8 = jnp.broadcast_to(ddk_ref[...], (8, ddk_ref.shape[1]))
        dskip = _exact_dot(ddk8, e_ref[...], _NT, True)[0:1, :]

        @pl.when(pl.program_id(0) == 0)
        def _():
            for r in dsmall:
                r[...] = jnp.zeros_like(r)

        for r, gr in zip(dsmall, tuple(grads[2:]) + (dskip,)):
            r[...] += gr

    acc = list(small) + [small[0]]
    return _call(body, name="d_ssd_dt", grid=(nc,),
                 in_specs=[col, row] + full + [col, col, row, pl.BlockSpec((None, 1, e01.shape[1]), lambda c: (c, 0, 0)),
                                               pl.BlockSpec(e01.shape, lambda c: (0, 0))],
                 out_specs=[col, row] + [pl.BlockSpec(s.shape, lambda c: (0, 0)) for s in acc],
                 out_shape=[shapes[0], shapes[2]] + [jax.ShapeDtypeStruct(s.shape, F32) for s in acc],
                 args=(dtc, dtr, *small, g_dt, g_csc, g_csr, ddk, e01), sem=("arbitrary",))[0]


def _ssd_specs(t_len, d_ssm, r_heads, reverse):
    rp = r_heads * HEADDIM
    nc = t_len // CHUNK
    per = next(p for p in (SSD_CHUNKS_PER_STEP, 2, 1) if nc % p == 0)
    ns, rows = nc // per, per * CHUNK
    cidx = (lambda c: ns - 1 - c) if reverse else (lambda c: c)
    b_off = d_ssm // N_STATE
    specs = dict(
        xs=pl.BlockSpec((rows, rp), lambda c, g: (cidx(c), g)),
        b=pl.BlockSpec((rows, N_STATE), lambda c, g: (cidx(c), b_off + g)),
        c=pl.BlockSpec((rows, N_STATE), lambda c, g: (cidx(c), b_off + N_GROUPS + g)),
        grad_bc=pl.BlockSpec((rows, N_STATE), lambda c, g: (cidx(c), g)),
        col=pl.BlockSpec((rows, LANES), lambda c, g: (cidx(c), 0)),
        csr=pl.BlockSpec((None, r_heads, rows), lambda c, g: (g, 0, cidx(c))),
        e01=pl.BlockSpec((LANES, rp), lambda c, g: (0, g)),
        dskip=pl.BlockSpec((1, rp), lambda c, g: (0, g)),
        hprev=pl.BlockSpec((per, None, N_STATE, rp), lambda c, g: (cidx(c), g, 0, 0)),
        ddk=pl.BlockSpec((per, 1, rp), lambda c, g: (cidx(c), 0, g)),
    )
    return specs, nc, ns, per, rp


def _ssd_fwd(xbc, dt_c, cs_c, cs_r3, e01, dskip_e, *, d_ssm, r_heads, ride=None):
    t_len = xbc.shape[0]
    sp, nc, ns, per, rp = _ssd_specs(t_len, d_ssm, r_heads, False)

    def body(xs_ref, b_ref, c_ref, dt_ref, csc_ref, csr_ref, e_ref, dk_ref, y_ref, hprev_ref, h_ref):
        c, g = pl.program_id(0), pl.program_id(1)

        @pl.when(c == 0)
        def _():
            h_ref[g] = jnp.zeros((N_STATE, rp), F32)

        hp = h_ref[g]
        for s in range(per):
            r = pl.ds(s * CHUNK, CHUNK)
            hprev_ref[s] = hp
            y, hp = _ssd_chunk(r_heads, xs_ref[r, :].astype(F32), b_ref[r, :].astype(F32), c_ref[r, :].astype(F32),
                               dt_ref[r, :], csc_ref[r, :], csr_ref[:, r], e_ref[...], dk_ref[...], hp)
            y_ref[r, :] = y
        h_ref[g] = hp

    return _call(
        body, name="ssd_fwd", grid=(ns, N_GROUPS),
        in_specs=[sp["xs"], sp["b"], sp["c"], sp["col"], sp["col"], sp["csr"], sp["e01"], sp["dskip"]],
        out_specs=[sp["xs"], sp["hprev"]],
        out_shape=[jax.ShapeDtypeStruct((t_len, d_ssm), F32),
                   jax.ShapeDtypeStruct((nc, N_GROUPS, N_STATE, rp), F32)],
        args=(xbc, xbc, xbc, dt_c, cs_c, cs_r3, e01, dskip_e), scratch=[pltpu.VMEM((N_GROUPS, N_STATE, rp), F32)],
        sem=("arbitrary", "arbitrary"), ride=ride)


def _ssd_bwd(xbc, dt_c, cs_c, cs_r3, e01, dskip_e, hprev, dy, *, d_ssm, r_heads, ride=None):
    t_len = xbc.shape[0]
    sp, nc, ns, per, rp = _ssd_specs(t_len, d_ssm, r_heads, True)

    def body(xs_ref, b_ref, c_ref, dt_ref, csc_ref, csr_ref, e_ref, dk_ref, hprev_ref, dy_ref,
             dxs_ref, db_ref, dc_ref, ddt_ref, dcsc_ref, dcsr_ref, ddk_ref, dh_ref):
        c, g = pl.program_id(0), pl.program_id(1)

        @pl.when(c == 0)
        def _():
            dh_ref[g] = jnp.zeros((N_STATE, rp), F32)

        @pl.when(g == 0)
        def _():
            ddt_ref[...] = jnp.zeros_like(ddt_ref)
            dcsc_ref[...] = jnp.zeros_like(dcsc_ref)

        e01 = e_ref[...]
        fn = lambda xs, bg, cg, dt, csc, csr, dk, hp: _ssd_chunk(r_heads, xs, bg, cg, dt, csc, csr, e01, dk, hp)
        dh = dh_ref[g]
        for s in reversed(range(per)):
            r = pl.ds(s * CHUNK, CHUNK)
            _, vjp = jax.vjp(fn, xs_ref[r, :].astype(F32), b_ref[r, :].astype(F32), c_ref[r, :].astype(F32),
                             dt_ref[r, :], csc_ref[r, :], csr_ref[:, r], dk_ref[...], hprev_ref[s])
            dxs, dbg, dcg, ddt, dcsc, dcsr, ddk, dh = vjp((dy_ref[r, :], dh))
            dxs_ref[r, :] = dxs.astype(dxs_ref.dtype)
            db_ref[r, :] = dbg.astype(db_ref.dtype)
            dc_ref[r, :] = dcg.astype(dc_ref.dtype)
            ddt_ref[r, :] += ddt
            dcsc_ref[r, :] += dcsc
            dcsr_ref[:, r] = dcsr
            ddk_ref[s] = ddk
        dh_ref[g] = dh

    n_bc = N_GROUPS * N_STATE
    return _call(
        body, name="ssd_bwd", grid=(ns, N_GROUPS),
        in_specs=[sp["xs"], sp["b"], sp["c"], sp["col"], sp["col"], sp["csr"], sp["e01"], sp["dskip"], sp["hprev"],
                  sp["xs"]],
        out_specs=[sp["xs"], sp["grad_bc"], sp["grad_bc"], sp["col"], sp["col"], sp["csr"], sp["ddk"]],
        out_shape=[jax.ShapeDtypeStruct((t_len, d_ssm), BF16), jax.ShapeDtypeStruct((t_len, n_bc), BF16),
                   jax.ShapeDtypeStruct((t_len, n_bc), BF16), jax.ShapeDtypeStruct(dt_c.shape, F32),
                   jax.ShapeDtypeStruct(cs_c.shape, F32), jax.ShapeDtypeStruct(cs_r3.shape, F32),
                   jax.ShapeDtypeStruct((nc, 1, d_ssm), F32)],
        args=(xbc, xbc, xbc, dt_c, cs_c, cs_r3, e01, dskip_e, hprev, dy),
        scratch=[pltpu.VMEM((N_GROUPS, N_STATE, rp), F32)], sem=("arbitrary", "arbitrary"), ride=ride)


def _chip_sum(src, sib, *, name):
    rows, cols = src.shape[1:]
    tr = _tile(rows, 256, BF16_ROWS)
    core = lax.axis_index("c").astype(jnp.int32).reshape(1)

    def body(c_ref, a_ref, b_ref, o_ref):
        o_ref[...] = (a_ref[...].astype(F32) + b_ref[...].astype(F32)).astype(o_ref.dtype)

    grid_spec = pltpu.PrefetchScalarGridSpec(
        num_scalar_prefetch=1, grid=(N_CHIPS, rows // tr),
        in_specs=[pl.BlockSpec((None, tr, cols), lambda q, i, c_ref: (2 * q + c_ref[0], i, 0)),
                  pl.BlockSpec((None, tr, cols), lambda q, i, c_ref: (q, i, 0))],
        out_specs=pl.BlockSpec((None, tr, cols), lambda q, i, c_ref: (q, i, 0)))
    return pl.pallas_call(
        body, name=name, grid_spec=grid_spec, out_shape=jax.ShapeDtypeStruct(sib.shape, sib.dtype),
        compiler_params=pltpu.CompilerParams(dimension_semantics=("parallel", "parallel"), vmem_limit_bytes=VMEM_LIMIT),
    )(core, src, sib)


def _adamw(w, g, m, v):
    m = ADAM_B1 * m + (1.0 - ADAM_B1) * g
    v = ADAM_B2 * v + (1.0 - ADAM_B2) * (g * g)
    m_hat = m / (1.0 - ADAM_B1 ** ADAM_STEP)
    v_hat = v / (1.0 - ADAM_B2 ** ADAM_STEP)
    delta = -ADAM_LR * (m_hat / (jnp.sqrt(v_hat) + ADAM_EPS) + ADAM_WD * w)
    return delta, m, v


def _reduce_adamw(parts, w, m, v, *, name):
    n_parts = parts.shape[0]
    rows, cols = w.shape
    tr = _tile(rows, 128, BF16_ROWS)

    def body(p_ref, w_ref, m_ref, v_ref, g_ref, d_ref, mo_ref, vo_ref):
        g = p_ref[0].astype(F32)
        for k in range(1, n_parts):
            g = g + p_ref[k].astype(F32)
        delta, mn, vn = _adamw(w_ref[...], g, m_ref[...], v_ref[...])
        g_ref[...] = g
        d_ref[...] = delta
        mo_ref[...] = mn
        vo_ref[...] = vn

    spec = pl.BlockSpec((tr, cols), lambda i: (i, 0))
    outs, _ = _call(
        body, name=name, grid=(rows // tr,),
        in_specs=[pl.BlockSpec((n_parts, tr, cols), lambda i: (0, i, 0)), spec, spec, spec],
        out_specs=[spec] * 4, out_shape=[jax.ShapeDtypeStruct((rows, cols), F32)] * 4,
        args=(parts, w, m, v), sem=("parallel",))
    return outs


def _cols_of(g):
    return jnp.transpose(g, (1, 0, 2)).reshape(g.shape[1], -1)


def _pad_to(a, rows, cols):
    return jnp.pad(a, ((0, rows - a.shape[0]), (0, cols - a.shape[1])))


def kernel(x, norm_mix_g, w_in, ssm_conv_w, ssm_conv_b, ssm_dt_bias, ssm_A_log, ssm_D, ssm_norm_g, sc_conv_w, w_out, norm_ffn_g, w_gate, w_up, w_down, norm_final_g, loss_target, m_norm_mix_g, m_w_in, m_ssm_conv_w, m_ssm_conv_b, m_ssm_dt_bias, m_ssm_A_log, m_ssm_D, m_ssm_norm_g, m_sc_conv_w, m_w_out, m_norm_ffn_g, m_w_gate, m_w_up, m_w_down, m_norm_final_g, v_norm_mix_g, v_w_in, v_ssm_conv_w, v_ssm_conv_b, v_ssm_dt_bias, v_ssm_A_log, v_ssm_D, v_ssm_norm_g, v_sc_conv_w, v_w_out, v_norm_ffn_g, v_w_gate, v_w_up, v_w_down, v_norm_final_g):
    t_len, d = x.shape[1], x.shape[2]
    heads = d // HEADDIM
    r_heads = heads // N_GROUPS
    d_xbc = d + 2 * N_GROUPS * N_STATE
    ff_s = w_down.shape[1]
    ff = ff_s * N_DEV
    off_xbc, off_dt = d, d + d_xbc
    off_cb = off_dt + heads
    d_in = off_cb + 3 * d
    in_s = d_in // N_DEV
    in_p = -(-in_s // (2 * BF16_ROWS)) * (2 * BF16_ROWS)
    w_main = 4 * d + d_xbc
    me = 4 * lax.axis_index("x") + 2 * lax.axis_index("y") + lax.axis_index("c")

    x2 = x[0]
    target = loss_target[0]

    tpose = lambda a: jnp.transpose(a[0])
    win_s = _pad_to(tpose(w_in).astype(BF16), in_p, d)
    wg_s, wu_s = tpose(w_gate).astype(BF16), tpose(w_up).astype(BF16)
    wo_s, wd_s = w_out[0].astype(BF16), w_down[0].astype(BF16)
    small_w = jnp.concatenate([_pad_to(ssm_conv_w[0], K_SSM, d_xbc // N_DEV),
                               _pad_to(sc_conv_w[0], K_SC + 1, d_xbc // N_DEV)], axis=0)

    g1, g2, g3 = norm_mix_g, norm_ffn_g, norm_final_g.reshape(1, d)
    gs = ssm_norm_g
    small = [_pad_to(ssm_dt_bias, 1, LANES), ssm_dt_bias.reshape(heads, 1), _pad_to(ssm_A_log, 1, LANES),
             ssm_A_log.reshape(heads, 1)]
    e01 = (lax.broadcasted_iota(jnp.int32, (LANES, d), 1) // HEADDIM
           == lax.broadcasted_iota(jnp.int32, (LANES, d), 0)).astype(BF16)
    dskip_e = jnp.repeat(ssm_D, HEADDIM, axis=1)
    tr = _tile(t_len, 256, 8)
    tr_ff = _tile(t_len, 128, 8)
    cw = LANES
    slab = lambda col: col // cw

    gin_1, gsm_1 = _comm(_gather_chips([win_s, small_w]), "gather_w_in_chips")
    (n1,), (gin, gsm) = _rows_call(lambda v, g: ((_rms(v, g),), ()), rows=t_len, tr=tr, row_ins=[(x2, d, 0)],
                                   full_ins=[g1], row_outs=[(d, BF16)], acc_outs=[], name="norm_mix",
                                   ride=_gather_sibling([gin_1, gsm_1]))
    in_pieces = []
    for k in range(N_DEV):
        for a, b, dst, shift in ((0, off_dt, 0, 0), (off_dt, off_cb, 1, -off_dt), (off_cb, d_in, 0, -heads)):
            s, e = max(k * in_s, a), min((k + 1) * in_s, b)
            if s < e:
                in_pieces.append((k, s - k * in_s, e - s, dst, s + shift))
    wtm = jnp.concatenate([gin[k, r0:r0 + n] for k, r0, n, dst, _ in in_pieces if dst == 0], axis=0)
    wtdt = jnp.concatenate([gin[k, r0:r0 + n] for k, r0, n, dst, _ in in_pieces if dst == 1]
                           + [jnp.zeros((LANES - heads, d), BF16)], axis=0)
    cw_ssm = _cols_of(gsm[:, :K_SSM, :])
    cw_sc = _cols_of(gsm[:, K_SSM:K_SSM + K_SC, :d // N_DEV])

    proj, (go_1, gg_1) = _matmul(n1, wtm, tb=True, out_dtype=BF16, name="proj_main",
                                 ride=_gather_chips([wo_s, wg_s]))
    dt_raw, _ = _matmul(n1, wtdt, tb=True, out_dtype=F32, name="proj_dt")
    dt_raw_t = jnp.transpose(dt_raw[:, :heads])
    (xbc,), (go, gg) = _cols_call(_conv_silu_fwd, rows=t_len, cols=d_xbc, cw=cw, col_ins=[(proj, slab(off_xbc))],
                                  par_ins=[(cw_ssm, 0), (ssm_conv_b, 0)], col_outs=[BF16], par_outs=[],
                                  name="ssm_conv", ride=_gather_sibling([go_1, gg_1]))
    dt_c, cs_c, cs_r = _ssd_dt(dt_raw, dt_raw_t, small)
    cs_r3 = cs_r.reshape(N_GROUPS, r_heads, t_len)
    (y_ssd, hprev), (gu_1,) = _ssd_fwd(xbc, dt_c, cs_c, cs_r3, e01, dskip_e, d_ssm=d, r_heads=r_heads,
                                       ride=_gather_chips([wu_s]))

    def gate_norm(y, z, g):
        z = z.astype(F32)
        return _rms(y * (z * _sigmoid(z)), g)

    (y_mix,), (gu,) = _rows_call(lambda y, z, g: ((gate_norm(y, z, g),), ()), rows=t_len, tr=tr,
                                 row_ins=[(y_ssd, d, 0), (proj, d, 0)], full_ins=[gs], row_outs=[(d, BF16, 2 * d)],
                                 acc_outs=[], name="ssm_gate_norm", ride=_gather_sibling([gu_1]))
    wgt, wut, wo = gg.reshape(ff, d), gu.reshape(ff, d), go.reshape(2 * d, d)
    sc0 = slab(d + d_xbc)
    (y_mix,), _ = _cols_call(_shortconv_fwd, rows=t_len, cols=d, cw=cw,
                             col_ins=[(proj, sc0), (proj, sc0 + slab(d)), (proj, sc0 + 2 * slab(d))],
                             par_ins=[(cw_sc, 0)], col_outs=[BF16], par_outs=[], name="shortconv",
                             into=(y_mix, slab(d)))
    h1, _ = _matmul(y_mix, wo, out_dtype=F32, add=x2, name="out_proj")
    (n2,), _ = _rows_call(lambda v, g: ((_rms(v, g),), ()), rows=t_len, tr=tr, row_ins=[(h1, d, 0)], full_ins=[g2],
                          row_outs=[(d, BF16)], acc_outs=[], name="norm_ffn")
    g_ff, (gd_1,) = _matmul(n2, wgt, tb=True, out_dtype=BF16, name="ffn_gate",
                            ride=_gather_chips([wd_s]))
    u_ff, (gd,) = _matmul(n2, wut, tb=True, out_dtype=BF16, name="ffn_up", ride=_gather_sibling([gd_1]))
    wd = gd.reshape(ff, d)

    def act(gv, uv):
        gv, uv = gv.astype(F32), uv.astype(F32)
        return ((gv * _sigmoid(gv) * uv,), ())

    (a_ff,), _ = _rows_call(act, rows=t_len, tr=tr_ff, row_ins=[(g_ff, ff, 0), (u_ff, ff, 0)], full_ins=[],
                            row_outs=[(ff, BF16)], acc_outs=[], name="ffn_act")
    h2, _ = _matmul(a_ff, wd, out_dtype=F32, add=h1, name="ffn_down")

    def head(hv, tv, g):
        def f(hh, gg_):
            e = _rms(hh, gg_) - tv
            return (0.5 / d) * jnp.sum(e * e)
        val, (dh, dg) = jax.value_and_grad(f, argnums=(0, 1))(hv, g)
        return (dh, dh), (jnp.full((1, LANES), val, F32), dg)

    (dh2, dh2_b, loss_acc, dg3), _ = _rows_call(head, rows=t_len, tr=tr, row_ins=[(h2, d, 0), (target, d, 0)],
                                                full_ins=[g3], row_outs=[(d, F32), (d, BF16)],
                                                acc_outs=[(1, LANES), (1, d)], name="loss_head")
    loss = lax.psum(loss_acc[0, 0], ("x", "y", "c"))

    da, _ = _matmul(dh2_b, wd, tb=True, out_dtype=BF16, name="d_ffn_act")
    dwd, _ = _matmul(a_ff, dh2_b, ta=True, out_dtype=BF16, name="d_w_down")
    dwd8 = dwd.reshape(N_DEV, ff_s, d)

    def act_bwd(dav, gv, uv):
        dav, gv, uv = dav.astype(F32), gv.astype(F32), uv.astype(F32)
        s = _sigmoid(gv)
        return ((dav * uv * (s * (1.0 + gv * (1.0 - s))), dav * gv * s), ())

    (dg_ff, du_ff), (sib_d,) = _rows_call(act_bwd, rows=t_len, tr=tr_ff,
                                          row_ins=[(da, ff, 0), (g_ff, ff, 0), (u_ff, ff, 0)], full_ins=[],
                                          row_outs=[(ff, BF16), (ff, BF16)], acc_outs=[], name="d_ffn_gate_up",
                                          ride=_scatter_sibling([dwd8]))
    chip_d = _chip_sum(dwd8, sib_d, name="chip_sum_w_down")
    dn2, (parts_d,) = _matmul(dg_ff, wgt, out_dtype=F32, name="d_norm_ffn_out_gate", ride=_scatter_chips([chip_d]))
    dn2, _ = _matmul(du_ff, wut, out_dtype=F32, add=dn2, name="d_norm_ffn_out_up")
    dwg, _ = _matmul(dg_ff, n2, ta=True, out_dtype=BF16, name="d_w_gate")
    dwu, _ = _matmul(du_ff, n2, ta=True, out_dtype=BF16, name="d_w_up")
    dwg8, dwu8 = dwg.reshape(N_DEV, ff_s, d), dwu.reshape(N_DEV, ff_s, d)

    def norm_bwd(v, dn, dres, g):
        _, vjp = jax.vjp(_rms, v, g)
        dv, dg = vjp(dn)
        return (dv + dres,), (dg,)

    def norm_bwd_2(v, dn, dres, g):
        (dv,), acc = norm_bwd(v, dn, dres, g)
        return (dv, dv), acc

    (dh1, dh1_b, dg2), (sib_g, sib_u) = _rows_call(norm_bwd_2, rows=t_len, tr=tr,
                                                   row_ins=[(h1, d, 0), (dn2, d, 0), (dh2, d, 0)], full_ins=[g2],
                                                   row_outs=[(d, F32), (d, BF16)], acc_outs=[(1, d)], name="d_norm_ffn",
                                                   ride=_scatter_sibling([dwg8, dwu8]))
    chip_g = _chip_sum(dwg8, sib_g, name="chip_sum_w_gate")
    chip_u = _chip_sum(dwu8, sib_u, name="chip_sum_w_up")

    dy_mix, _ = _matmul(dh1_b, wo, tb=True, out_dtype=BF16, name="d_y_mix")
    dwo, _ = _matmul(y_mix, dh1_b, ta=True, out_dtype=BF16, name="d_w_out")
    dwo8 = dwo.reshape(N_DEV, 2 * d // N_DEV, d)
    (dgb, dgc, du, dcw_sc), (sib_o,) = _cols_call(
        _shortconv_bwd, rows=t_len, cols=d, cw=cw,
        col_ins=[(proj, sc0), (proj, sc0 + slab(d)), (proj, sc0 + 2 * slab(d)), (dy_mix, slab(d))],
        par_ins=[(cw_sc, 0)], col_outs=[BF16] * 3, par_outs=[K_SC], name="d_shortconv",
        ride=_scatter_sibling([dwo8]))
    chip_o = _chip_sum(dwo8, sib_o, name="chip_sum_w_out")

    def gate_norm_bwd(y, z, dyo, g):
        _, vjp = jax.vjp(gate_norm, y, z.astype(F32), g)
        dy, dz, dg = vjp(dyo.astype(F32))
        return (dy, dz), (dg,)

    (dy_ssd, dproj, dgs), _ = _rows_call(gate_norm_bwd, rows=t_len, tr=tr,
                                         row_ins=[(y_ssd, d, 0), (proj, d, 0), (dy_mix, d, 0)], full_ins=[gs],
                                         row_outs=[(d, F32), (d, BF16, w_main)], acc_outs=[(1, d)],
                                         name="d_ssm_gate_norm")
    (dxs, dbm, dcm, g_dt, g_csc, g_csr3, ddk), (parts_g, parts_u, parts_o) = _ssd_bwd(
        xbc, dt_c, cs_c, cs_r3, e01, dskip_e, hprev, dy_ssd, d_ssm=d, r_heads=r_heads,
        ride=_scatter_chips([chip_g, chip_u, chip_o]))
    ddt_c, ddt_r, dbias_r, dbias_c, dalog_r, dalog_c, ddskip = _ssd_dt(
        dt_raw, dt_raw_t, small, cots=(g_dt, g_csc, g_csr3.reshape(heads, t_len), ddk, e01))
    dcw_parts, dcb_parts, col0 = [], [], 0
    for tag, dpart in (("x", dxs), ("b", dbm), ("c", dcm)):
        (dproj, dcw_p, dcb_p), _ = _cols_call(
            _conv_silu_bwd, rows=t_len, cols=dpart.shape[1], cw=cw,
            col_ins=[(proj, slab(off_xbc + col0)), (dpart, 0)], par_ins=[(cw_ssm, slab(col0)), (ssm_conv_b, slab(col0))],
            col_outs=[BF16], par_outs=[K_SSM, 1], name="d_ssm_conv_" + tag, into=(dproj, slab(off_xbc + col0)))
        dcw_parts.append(dcw_p)
        dcb_parts.append(dcb_p)
        col0 += dpart.shape[1]
    dcw_ssm, dcb_ssm = jnp.concatenate(dcw_parts, axis=1), jnp.concatenate(dcb_parts, axis=1)
    for i, part in enumerate((dgb, dgc, du)):
        dproj = lax.dynamic_update_slice(dproj, part, (0, d + d_xbc + i * d))
    ddt = ddt_c + _pad_to(jnp.transpose(ddt_r), t_len, LANES)
    dwm, _ = _matmul(dproj, n1, ta=True, out_dtype=BF16, name="d_w_in_main")
    dwdt, _ = _matmul(ddt, n1, ta=True, out_dtype=BF16, name="d_w_in_dt")
    blocks = []
    for k in range(N_DEV):
        pcs = [((dwm if dst == 0 else dwdt)[d0:d0 + n], r0) for kk, r0, n, dst, d0 in in_pieces if kk == k]
        blk = jnp.pad(pcs[0][0], ((0, in_p - pcs[0][0].shape[0]), (0, 0)))
        for piece, r0 in pcs[1:]:
            blk = lax.dynamic_update_slice(blk, piece, (r0, 0))
        blocks.append(blk)
    dwin8 = jnp.stack(blocks)
    dn1, (sib_in,) = _matmul(ddt, wtdt, out_dtype=F32, name="d_norm_mix_out_dt", ride=_scatter_sibling([dwin8]))
    chip_in = _chip_sum(dwin8, sib_in, name="chip_sum_w_in")
    dn1, (parts_in,) = _matmul(dproj, wtm, out_dtype=F32, add=dn1, name="d_norm_mix_out",
                               ride=_scatter_chips([chip_in]))
    (dx, dg1), _ = _rows_call(norm_bwd, rows=t_len, tr=tr, row_ins=[(x2, d, 0), (dn1, d, 0), (dh1, d, 0)],
                              full_ins=[g1], row_outs=[(d, F32)], acc_outs=[(1, d)], name="d_norm_mix")

    wide = d_xbc
    rows_small = [dg1, dcb_ssm, dbias_r + _pad_to(dbias_c.reshape(1, heads), 1, LANES),
                  dalog_r + _pad_to(dalog_c.reshape(1, heads), 1, LANES), ddskip, dgs, dg2, dg3]
    packed = jnp.concatenate([_pad_to(r, 1, wide) for r in rows_small]
                             + [dcw_ssm, _pad_to(dcw_sc, K_SC, wide), jnp.zeros((1, wide), F32)], axis=0)
    (p_small,) = _comm(_gather_all([packed]), "gather_small_grads")

    conv_lo = me * (d_xbc // N_DEV)
    sc_lo = me * (d // N_DEV)

    def pack_state(vals):
        (nm, cb, dtb, al, dk, sg, nf, nfin, cws, scs) = vals
        rows = [_pad_to(a.reshape(1, -1), 1, wide) for a in (nm, cb, dtb, al, dk, sg, nf, nfin)]
        cws_full = lax.dynamic_update_slice(jnp.zeros((K_SSM, wide), F32), cws[0], (0, conv_lo))
        scs_full = lax.dynamic_update_slice(jnp.zeros((K_SC, wide), F32), scs[0], (0, sc_lo))
        return jnp.concatenate(rows + [cws_full, scs_full, jnp.zeros((1, wide), F32)], axis=0)

    w_small = pack_state((norm_mix_g, ssm_conv_b, ssm_dt_bias, ssm_A_log, ssm_D, ssm_norm_g, norm_ffn_g, norm_final_g,
                          ssm_conv_w, sc_conv_w))
    m_small = pack_state((m_norm_mix_g, m_ssm_conv_b, m_ssm_dt_bias, m_ssm_A_log, m_ssm_D, m_ssm_norm_g, m_norm_ffn_g,
                          m_norm_final_g, m_ssm_conv_w, m_sc_conv_w))
    v_small = pack_state((v_norm_mix_g, v_ssm_conv_b, v_ssm_dt_bias, v_ssm_A_log, v_ssm_D, v_ssm_norm_g, v_norm_ffn_g,
                          v_norm_final_g, v_ssm_conv_w, v_sc_conv_w))

    tin = lambda a: _pad_to(tpose(a), in_p, d)
    tin_back = lambda a: jnp.transpose(a[:in_s])[None]
    t_back = lambda a: jnp.transpose(a)[None]
    upd = {
        "w_in": [tin_back(o) for o in _reduce_adamw(parts_in, tin(w_in), tin(m_w_in), tin(v_w_in), name="adamw_w_in")],
        "w_out": [o[None] for o in _reduce_adamw(parts_o, w_out[0], m_w_out[0], v_w_out[0], name="adamw_w_out")],
        "w_gate": [t_back(o) for o in _reduce_adamw(parts_g, tpose(w_gate), tpose(m_w_gate), tpose(v_w_gate),
                                                    name="adamw_w_gate")],
        "w_up": [t_back(o) for o in _reduce_adamw(parts_u, tpose(w_up), tpose(m_w_up), tpose(v_w_up),
                                                  name="adamw_w_up")],
        "w_down": [o[None] for o in _reduce_adamw(parts_d, w_down[0], m_w_down[0], v_w_down[0], name="adamw_w_down")],
    }
    small_upd = _reduce_adamw(p_small, w_small, m_small, v_small, name="adamw_small")

    def unpack(packed_out):
        vec = lambda i, n, shape: packed_out[i, :n].reshape(shape)
        return {
            "norm_mix_g": vec(0, d, (1, d)), "ssm_conv_b": vec(1, d_xbc, (1, d_xbc)),
            "ssm_dt_bias": vec(2, heads, (1, heads)), "ssm_A_log": vec(3, heads, (1, heads)),
            "ssm_D": vec(4, heads, (1, heads)), "ssm_norm_g": vec(5, d, (1, d)), "norm_ffn_g": vec(6, d, (1, d)),
            "norm_final_g": vec(7, d, (d,)),
            "ssm_conv_w": lax.dynamic_slice(packed_out[8:8 + K_SSM], (0, conv_lo), (K_SSM, d_xbc // N_DEV))[None],
            "sc_conv_w": lax.dynamic_slice(packed_out[8 + K_SSM:8 + K_SSM + K_SC], (0, sc_lo), (K_SC, d // N_DEV))[None],
        }

    names = ["norm_mix_g", "w_in", "ssm_conv_w", "ssm_conv_b", "ssm_dt_bias", "ssm_A_log", "ssm_D", "ssm_norm_g",
             "sc_conv_w", "w_out", "norm_ffn_g", "w_gate", "w_up", "w_down", "norm_final_g"]
    outs = []
    for kind in range(4):
        small_k = unpack(small_upd[kind])
        for nm in names:
            outs.append(upd[nm][kind] if nm in upd else small_k[nm])
    return (loss, dx[None], *outs)
```

```python
import collections
import functools

import jax
import jax.numpy as jnp
from jax import lax
from jax.experimental import pallas as pl
from jax.experimental.pallas import tpu as pltpu

F32 = jnp.float32
BF16 = jnp.bfloat16

N_DEV = 8
N_CHIPS = 4
HEADDIM = 64
N_GROUPS = 8
N_STATE = 128
CHUNK = 128
K_SSM = 4
K_SC = 3
EPS = 1e-5
LANES = 128
BF16_ROWS = 16
MM_TILE_MN = 1408
MM_TILE_K = 2048
MM_TILE_N_POST = 704
SSD_CHUNKS_PER_STEP = 4
V7X_VMEM_BYTES = 64 * 1024 * 1024
VMEM_LIMIT = (V7X_VMEM_BYTES * 3) // 4

ADAM_LR = 0.001
ADAM_B1 = 0.9
ADAM_B2 = 0.999
ADAM_EPS = 1e-08
ADAM_WD = 0.01
ADAM_STEP = 10


def _tile(n, pref, align):
    t = min(pref, n)
    t -= t % align
    while t >= align:
        if n % t == 0:
            return t
        t -= align
    return n


_Ride = collections.namedtuple("_Ride", ["ins", "out_shapes", "aliases", "nsem", "plan"])
_ANY = pl.BlockSpec(memory_space=pl.ANY)


def _coords():
    return lax.axis_index("x"), lax.axis_index("y"), lax.axis_index("c")


def _other_chips(x, y):
    return ((1 - x, y), (x, 1 - y), (1 - x, 1 - y))


def _remote(src, dst, send, recv, k, dev):
    return functools.partial(pltpu.make_async_remote_copy, src_ref=src, dst_ref=dst, send_sem=send.at[k],
                             recv_sem=recv.at[k], device_id=dev, device_id_type=pl.DeviceIdType.MESH)


def _local(src, dst, sem):
    return functools.partial(pltpu.make_async_copy, src, dst, sem)


def _start_all(plan):
    for kind, make in plan:
        if kind != "arrival":
            make().start()


def _wait_all(plan):
    for kind, make in plan:
        if kind == "local":
            make().wait()
        elif kind == "out":
            make().wait_send()
        else:
            make().wait_recv()


def _gather_chips(srcs):
    def plan(ins, outs, send, recv, base):
        x, y, c = _coords()
        me = 4 * x + 2 * y + c
        d = []
        for a, (src, dst) in enumerate(zip(ins, outs)):
            k = base + 4 * a
            d.append(("local", _local(src, dst.at[me], send.at[k + 3])))
            for j, (px, py) in enumerate(_other_chips(x, y)):
                d.append(("out", _remote(src, dst.at[me], send, recv, k + j, (px, py, c))))
                d.append(("arrival", _remote(src, dst.at[4 * px + 2 * py + c], send, recv, k + j, (px, py, c))))
        return d
    shapes = [jax.ShapeDtypeStruct((N_DEV,) + s.shape, s.dtype) for s in srcs]
    return _Ride(list(srcs), shapes, {}, 4 * len(srcs), plan)


def _gather_sibling(bufs):
    def plan(ins, outs, send, recv, base):
        x, y, c = _coords()
        d = []
        for a, buf in enumerate(outs):
            for q in range(N_CHIPS):
                k = base + 4 * a + q
                d.append(("out", _remote(buf.at[2 * q + c], buf.at[2 * q + c], send, recv, k, (x, y, 1 - c))))
                d.append(("arrival", _remote(buf.at[2 * q + c], buf.at[2 * q + 1 - c], send, recv, k, (x, y, 1 - c))))
        return d
    shapes = [jax.ShapeDtypeStruct(b.shape, b.dtype) for b in bufs]
    return _Ride(list(bufs), shapes, {i: i for i in range(len(bufs))}, 4 * len(bufs), plan)


def _scatter_sibling(srcs):
    def plan(ins, outs, send, recv, base):
        x, y, c = _coords()
        d = []
        for a, (src, sib) in enumerate(zip(ins, outs)):
            for q in range(N_CHIPS):
                k = base + 4 * a + q
                d.append(("out", _remote(src.at[2 * q + 1 - c], sib.at[q], send, recv, k, (x, y, 1 - c))))
                d.append(("arrival", _remote(src.at[2 * q + 1 - c], sib.at[q], send, recv, k, (x, y, 1 - c))))
        return d
    shapes = [jax.ShapeDtypeStruct((N_CHIPS,) + s.shape[1:], s.dtype) for s in srcs]
    return _Ride(list(srcs), shapes, {}, 4 * len(srcs), plan)


def _scatter_chips(chips):
    def plan(ins, outs, send, recv, base):
        x, y, c = _coords()
        mine = 2 * x + y
        d = []
        for a, (chip, parts) in enumerate(zip(ins, outs)):
            k = base + 4 * a
            d.append(("local", _local(chip.at[mine], parts.at[mine], send.at[k + 3])))
            for j, (px, py) in enumerate(_other_chips(x, y)):
                q = 2 * px + py
                d.append(("out", _remote(chip.at[q], parts.at[mine], send, recv, k + j, (px, py, c))))
                d.append(("arrival", _remote(chip.at[q], parts.at[q], send, recv, k + j, (px, py, c))))
        return d
    shapes = [jax.ShapeDtypeStruct(s.shape, s.dtype) for s in chips]
    return _Ride(list(chips), shapes, {}, 4 * len(chips), plan)


def _gather_all(srcs):
    def plan(ins, outs, send, recv, base):
        x, y, c = _coords()
        me = 4 * x + 2 * y + c
        d = []
        for a, (src, dst) in enumerate(zip(ins, outs)):
            k = base + N_DEV * a
            d.append(("local", _local(src, dst.at[me], send.at[k])))
            for j in range(1, N_DEV):
                px = 1 - x if (j >> 2) & 1 else x
                py = 1 - y if (j >> 1) & 1 else y
                pc = 1 - c if j & 1 else c
                d.append(("out", _remote(src, dst.at[me], send, recv, k + j, (px, py, pc))))
                d.append(("arrival", _remote(src, dst.at[4 * px + 2 * py + pc], send, recv, k + j, (px, py, pc))))
        return d
    shapes = [jax.ShapeDtypeStruct((N_DEV,) + s.shape, s.dtype) for s in srcs]
    return _Ride(list(srcs), shapes, {}, N_DEV * len(srcs), plan)


def _merge(*rides):
    ins, outs, aliases, parts, nsem = [], [], {}, [], 0
    for r in rides:
        parts.append((len(ins), len(outs), nsem, r))
        aliases.update({len(ins) + i: len(outs) + j for i, j in r.aliases.items()})
        ins += r.ins
        outs += r.out_shapes
        nsem += r.nsem

    def plan(i, o, send, recv, base):
        d = []
        for i0, o0, s0, r in parts:
            d += r.plan(i[i0:i0 + len(r.ins)], o[o0:o0 + len(r.out_shapes)], send, recv, base + s0)
        return d
    return _Ride(ins, outs, aliases, nsem, plan)


def _comm(ride, name):
    n_in, n_out = len(ride.ins), len(ride.out_shapes)

    def body(*refs):
        plan = ride.plan(refs[:n_in], refs[n_in:n_in + n_out], refs[-2], refs[-1], 0)
        _start_all(plan)
        _wait_all(plan)

    return pl.pallas_call(
        body, name=name, in_specs=[_ANY] * n_in, out_specs=[_ANY] * n_out, out_shape=ride.out_shapes,
        scratch_shapes=[pltpu.SemaphoreType.DMA((ride.nsem,)), pltpu.SemaphoreType.DMA((ride.nsem,))],
        input_output_aliases=dict(ride.aliases),
        compiler_params=pltpu.CompilerParams(has_side_effects=True),
    )(*ride.ins)


def _gather_chips_relayed(big, small, name):
    srcs = list(big) + list(small)
    n, nsem = len(srcs), 5 * len(srcs)

    def body(*refs):
        ins, outs, send, recv = refs[:n], refs[n:2 * n], refs[-2], refs[-1]
        x, y, c = _coords()
        slot = lambda dev: 4 * dev[0] + 2 * dev[1] + dev[2]
        me, nbr_x, nbr_y, diag = (x, y, c), (1 - x, y, c), (x, 1 - y, c), (1 - x, 1 - y, c)
        own, sends = [], []
        for a, (src, dst) in enumerate(zip(ins, outs)):
            k = 5 * a
            own.append(_local(src, dst.at[slot(me)], send.at[k + 4])())
            sends.append(_remote(src, dst.at[slot(me)], send, recv, k, nbr_x)())
            sends.append(_remote(src, dst.at[slot(me)], send, recv, k + 1, nbr_y)())
            if a >= len(big):
                sends.append(_remote(src, dst.at[slot(me)], send, recv, k + 2, diag)())
        for s in own + sends:
            s.start()
        for a, (src, dst) in enumerate(zip(ins, outs)):
            k = 5 * a
            _remote(src, dst.at[slot(nbr_x)], send, recv, k, nbr_x)().wait_recv()
            if a < len(big):
                half = src.shape[0] // 2
                part = dst.at[slot(nbr_x)].at[pl.ds(0, half)]
                fwd = _remote(part, part, send, recv, k + 2, nbr_y)()
                fwd.start()
                sends.append(fwd)
            _remote(src, dst.at[slot(nbr_y)], send, recv, k + 1, nbr_y)().wait_recv()
            if a < len(big):
                part = dst.at[slot(nbr_y)].at[pl.ds(half, src.shape[0] - half)]
                fwd = _remote(part, part, send, recv, k + 3, nbr_x)()
                fwd.start()
                sends.append(fwd)
        for a, (src, dst) in enumerate(zip(ins, outs)):
            k = 5 * a
            if a < len(big):
                half = src.shape[0] // 2
                lo = dst.at[slot(diag)].at[pl.ds(0, half)]
                hi = dst.at[slot(diag)].at[pl.ds(half, src.shape[0] - half)]
                _remote(lo, lo, send, recv, k + 2, nbr_y)().wait_recv()
                _remote(hi, hi, send, recv, k + 3, nbr_x)().wait_recv()
            else:
                _remote(src, dst.at[slot(diag)], send, recv, k + 2, diag)().wait_recv()
        for lc in own:
            lc.wait()
        for s in sends:
            s.wait_send()

    return pl.pallas_call(
        body, name=name, in_specs=[_ANY] * n, out_specs=[_ANY] * n,
        out_shape=[jax.ShapeDtypeStruct((N_DEV,) + s.shape, s.dtype) for s in srcs],
        scratch_shapes=[pltpu.SemaphoreType.DMA((nsem,)), pltpu.SemaphoreType.DMA((nsem,))],
        compiler_params=pltpu.CompilerParams(has_side_effects=True),
    )(*srcs)


def _call(body, *, name, grid, in_specs, out_specs, out_shape, args, sem, scratch=(), ride=None, base=None):
    params = pltpu.CompilerParams(dimension_semantics=sem, vmem_limit_bytes=VMEM_LIMIT)
    own_aliases = {}
    if base is not None:
        inner, n_host = body, len(args)
        body = lambda *refs: inner(*refs[:n_host], *refs[n_host + 1:])
        own_aliases[n_host] = base[1]
        args, in_specs = tuple(args) + (base[0],), list(in_specs) + [_ANY]
    if ride is None:
        res = pl.pallas_call(body, name=name, grid=grid, in_specs=in_specs, out_specs=out_specs,
                             out_shape=out_shape, scratch_shapes=list(scratch), input_output_aliases=own_aliases,
                             compiler_params=params)(*args)
        return list(res), []
    n_in, n_out, n_scr = len(args), len(out_shape), len(scratch)
    r_in, r_out = len(ride.ins), len(ride.out_shapes)

    def hosted(*refs):
        h_in, rin = refs[:n_in], refs[n_in:n_in + r_in]
        o0 = n_in + r_in
        h_out, rout = refs[o0:o0 + n_out], refs[o0 + n_out:o0 + n_out + r_out]
        s0 = o0 + n_out + r_out
        h_scr, send, recv = refs[s0:s0 + n_scr], refs[s0 + n_scr], refs[s0 + n_scr + 1]
        ids = [pl.program_id(i) for i in range(len(grid))]
        first = functools.reduce(lambda p, q: p & q, [i == 0 for i in ids])
        last = functools.reduce(lambda p, q: p & q, [i == n - 1 for i, n in zip(ids, grid)])

        @pl.when(first)
        def _():
            _start_all(ride.plan(rin, rout, send, recv, 0))

        body(*h_in, *h_out, *h_scr)

        @pl.when(last)
        def _():
            _wait_all(ride.plan(rin, rout, send, recv, 0))

    res = pl.pallas_call(
        hosted, name=name, grid=grid, in_specs=list(in_specs) + [_ANY] * r_in,
        out_specs=list(out_specs) + [_ANY] * r_out, out_shape=list(out_shape) + list(ride.out_shapes),
        scratch_shapes=list(scratch) + [pltpu.SemaphoreType.DMA((ride.nsem,)), pltpu.SemaphoreType.DMA((ride.nsem,))],
        input_output_aliases={**own_aliases, **{n_in + i: n_out + j for i, j in ride.aliases.items()}},
        compiler_params=params,
    )(*args, *ride.ins)
    return list(res[:n_out]), list(res[n_out:])


def _matmul(a, b, *, ta=False, tb=False, out_dtype=BF16, add=None, post=None, name, ride=None, tn_max=MM_TILE_MN):
    m = a.shape[1] if ta else a.shape[0]
    k = a.shape[0] if ta else a.shape[1]
    n = b.shape[0] if tb else b.shape[1]
    assert k == (b.shape[1] if tb else b.shape[0])
    tm, tn, tk = _tile(m, MM_TILE_MN, LANES), _tile(n, tn_max, LANES), _tile(k, MM_TILE_K, LANES)
    nk = k // tk
    dims = (((0 if ta else 1,), (1 if tb else 0,)), ((), ()))
    single = post is None
    if add is not None:
        post = (lambda r, t: (r + t,), [add], [out_dtype])
    elif post is None:
        post = (lambda r: (r,), [], [out_dtype])
    post_fn, extras, out_dtypes = post
    n_ex, n_o = len(extras), len(out_dtypes)

    def body(*refs):
        a_ref, b_ref = refs[:2]
        ex_refs, o_refs = refs[2:2 + n_ex], refs[2 + n_ex:2 + n_ex + n_o]

        def finish(r):
            for o_ref, v in zip(o_refs, post_fn(r, *[e[...].astype(F32) for e in ex_refs])):
                o_ref[...] = v.astype(o_ref.dtype)

        part = lax.dot_general(a_ref[...].astype(BF16), b_ref[...].astype(BF16), dims, preferred_element_type=F32)
        if nk == 1:
            finish(part)
            return
        acc = refs[-1]
        kk = pl.program_id(2)

        @pl.when(kk == 0)
        def _():
            acc[...] = part

        @pl.when((kk > 0) & (kk < nk - 1))
        def _():
            acc[...] += part

        @pl.when(kk == nk - 1)
        def _():
            finish(acc[...] + part)

    a_spec = (pl.BlockSpec((tk, tm), lambda i, j, kk: (kk, i)) if ta
              else pl.BlockSpec((tm, tk), lambda i, j, kk: (i, kk)))
    b_spec = (pl.BlockSpec((tn, tk), lambda i, j, kk: (j, kk)) if tb
              else pl.BlockSpec((tk, tn), lambda i, j, kk: (kk, j)))
    o_spec = pl.BlockSpec((tm, tn), lambda i, j, kk: (i, j))
    outs, rides = _call(
        body, name=name, grid=(m // tm, n // tn, nk),
        in_specs=[a_spec, b_spec] + [o_spec] * n_ex, out_specs=[o_spec] * n_o,
        out_shape=[jax.ShapeDtypeStruct((m, n), dt) for dt in out_dtypes], args=(a, b, *extras),
        scratch=[pltpu.VMEM((tm, tn), F32)] if nk > 1 else [], sem=("parallel", "parallel", "arbitrary"), ride=ride)
    return (outs[0] if single else outs), rides


def _rows_call(fn, *, rows, tr, row_ins, full_ins, row_outs, acc_outs, name, ride=None):
    nr, nf, no, na = len(row_ins), len(full_ins), len(row_outs), len(acc_outs)

    def body(*refs):
        vals = [r[...] for r in refs[:nr + nf]]
        outs, accs = fn(*vals)
        for r, v in zip(refs[nr + nf:nr + nf + no], outs):
            r[...] = v.astype(r.dtype)
        if na:
            @pl.when(pl.program_id(0) == 0)
            def _():
                for r in refs[nr + nf + no:]:
                    r[...] = jnp.zeros_like(r)
            for r, v in zip(refs[nr + nf + no:], accs):
                r[...] += v

    in_specs = [pl.BlockSpec((tr, w), functools.partial(lambda cb, i: (i, cb), cb)) for _, w, cb in row_ins]
    in_specs += [pl.BlockSpec(f.shape, lambda i: (0, 0)) for f in full_ins]
    out_specs = [pl.BlockSpec((tr, o[0]), lambda i: (i, 0)) for o in row_outs]
    out_specs += [pl.BlockSpec(s, lambda i: (0, 0)) for s in acc_outs]
    out_shape = [jax.ShapeDtypeStruct((rows, o[-1] if len(o) == 3 else o[0]), o[1]) for o in row_outs]
    out_shape += [jax.ShapeDtypeStruct(s, F32) for s in acc_outs]
    return _call(body, name=name, grid=(rows // tr,), in_specs=in_specs, out_specs=out_specs, out_shape=out_shape,
                 args=tuple(a for a, _, _ in row_ins) + tuple(full_ins), sem=("arbitrary",), ride=ride)


def _cols_call(fn, *, rows, cols, cw, col_ins, par_ins, col_outs, par_outs, name, ride=None, into=None):
    nc, npar = len(col_ins), len(par_ins)

    def body(*refs):
        vals = [r[...] for r in refs[:nc + npar]]
        outs, pouts = fn(*vals)
        for r, v in zip(refs[nc + npar:], tuple(outs) + tuple(pouts)):
            r[...] = v.astype(r.dtype)

    in_specs = [pl.BlockSpec((rows, cw), functools.partial(lambda off, j: (0, off + j), off)) for _, off in col_ins]
    in_specs += [pl.BlockSpec((p.shape[0], cw), functools.partial(lambda off, j: (0, off + j), off))
                 for p, off in par_ins]
    out_specs = [pl.BlockSpec((rows, cw), lambda j: (0, j)) for _ in col_outs]
    out_specs += [pl.BlockSpec((k, cw), lambda j: (0, j)) for k in par_outs]
    out_shape = [jax.ShapeDtypeStruct((rows, cols), dt) for dt in col_outs]
    out_shape += [jax.ShapeDtypeStruct((k, cols), F32) for k in par_outs]
    if into is not None:
        out_specs[0] = pl.BlockSpec((rows, cw), lambda j: (0, into[1] + j))
        out_shape[0] = jax.ShapeDtypeStruct(into[0].shape, into[0].dtype)
    return _call(body, name=name, grid=(cols // cw,), in_specs=in_specs, out_specs=out_specs, out_shape=out_shape,
                 args=tuple(a for a, _ in col_ins) + tuple(p for p, _ in par_ins), sem=("arbitrary",), ride=ride,
                 base=None if into is None else (into[0], 0))


def _sigmoid(v):
    return 1.0 / (1.0 + jnp.exp(-v))


def _softplus(v):
    return jnp.maximum(v, 0.0) + jnp.log(1.0 + jnp.exp(-jnp.abs(v)))


def _rms(v, g):
    return v * lax.rsqrt(jnp.mean(v * v, axis=-1, keepdims=True) + EPS) * g


def _shift_down(v, s, row):
    return jnp.where(row >= s, pltpu.roll(v, s, 0), 0.0)


def _shift_up(v, s, row):
    n = v.shape[0]
    return jnp.where(row < n - s, pltpu.roll(v, n - s, 0), 0.0)


def _causal_conv(u, w, row):
    k_taps = w.shape[0]
    acc = u * w[k_taps - 1:k_taps, :]
    for k in range(k_taps - 1):
        acc = acc + _shift_down(u, k_taps - 1 - k, row) * w[k:k + 1, :]
    return acc


def _causal_conv_bwd(u, dy, w, row):
    k_taps = w.shape[0]
    tap = lax.broadcasted_iota(jnp.int32, w.shape, 0)
    du = dy * w[k_taps - 1:k_taps, :]
    dw = jnp.where(tap == k_taps - 1, jnp.sum(dy * u, axis=0, keepdims=True), 0.0)
    for k in range(k_taps - 1):
        s = k_taps - 1 - k
        du = du + _shift_up(dy, s, row) * w[k:k + 1, :]
        dw = dw + jnp.where(tap == k, jnp.sum(dy * _shift_down(u, s, row), axis=0, keepdims=True), 0.0)
    return du, dw


def _conv_silu_fwd(u, w, b):
    u = u.astype(F32)
    row = lax.broadcasted_iota(jnp.int32, u.shape, 0)
    pre = _causal_conv(u, w, row) + b
    return (pre * _sigmoid(pre),), ()


def _conv_silu_bwd(u, dy, w, b):
    u = u.astype(F32)
    dy = dy.astype(F32)
    row = lax.broadcasted_iota(jnp.int32, u.shape, 0)
    pre = _causal_conv(u, w, row) + b
    s = _sigmoid(pre)
    dpre = dy * (s * (1.0 + pre * (1.0 - s)))
    du, dw = _causal_conv_bwd(u, dpre, w, row)
    return (du,), (dw, jnp.sum(dpre, axis=0, keepdims=True))


def _shortconv_fwd(gb, gc, u, w):
    gb, gc, u = gb.astype(F32), gc.astype(F32), u.astype(F32)
    row = lax.broadcasted_iota(jnp.int32, u.shape, 0)
    return (gb * _causal_conv(gc * u, w, row),), ()


def _shortconv_bwd(gb, gc, u, dy, w):
    gb, gc, u, dy = gb.astype(F32), gc.astype(F32), u.astype(F32), dy.astype(F32)
    row = lax.broadcasted_iota(jnp.int32, u.shape, 0)
    v = gc * u
    dgb = dy * _causal_conv(v, w, row)
    dv, dw = _causal_conv_bwd(v, dy * gb, w, row)
    return (dgb, dv * u, dv * gc), (dw,)


def _split3(v):
    hi = v.astype(BF16)
    r1 = v - hi.astype(F32)
    mid = r1.astype(BF16)
    lo = (r1 - mid.astype(F32)).astype(BF16)
    return hi, mid, lo


def _exact_dot(v, m01, dims, v_is_lhs):
    def one(p):
        return (lax.dot_general(p, m01, dims, preferred_element_type=F32) if v_is_lhs
                else lax.dot_general(m01, p, dims, preferred_element_type=F32))
    hi, mid, lo = _split3(v)
    return (one(lo) + one(mid)) + one(hi)


_NN = (((1,), (0,)), ((), ()))
_NT = (((1,), (1,)), ((), ()))
_TN = (((0,), (0,)), ((), ()))


@jax.custom_vjp
def _cumsum_rows(tril, v):
    return _exact_dot(v, tril, _NN, False)


def _cumsum_rows_fwd(tril, v):
    return _cumsum_rows(tril, v), tril


def _cumsum_rows_bwd(tril, ct):
    return None, _exact_dot(ct, tril, _TN, False)


_cumsum_rows.defvjp(_cumsum_rows_fwd, _cumsum_rows_bwd)


@jax.custom_vjp
def _cumsum_lanes(tril, v):
    return _exact_dot(v, tril, _NT, True)


def _cumsum_lanes_fwd(tril, v):
    return _cumsum_lanes(tril, v), tril


def _cumsum_lanes_bwd(tril, ct):
    return None, _exact_dot(ct, tril, _NN, True)


_cumsum_lanes.defvjp(_cumsum_lanes_fwd, _cumsum_lanes_bwd)


@jax.custom_vjp
def _expand(e01, v):
    return _exact_dot(v, e01, _NN, True)


def _expand_fwd(e01, v):
    return _expand(e01, v), e01


def _expand_bwd(e01, ct):
    return None, _exact_dot(ct, e01, _NT, True)


_expand.defvjp(_expand_fwd, _expand_bwd)


def _causal_mask(n):
    li = lax.broadcasted_iota(jnp.int32, (n, n), 0)
    si = lax.broadcasted_iota(jnp.int32, (n, n), 1)
    return si <= li


def _dt_prep(dtc, dtr, bias_r, bias_c, alog_r, alog_c):
    dt_c = _softplus(dtc + bias_r)
    dt_r = _softplus(dtr + bias_c)
    tril = jnp.where(_causal_mask(dtc.shape[0]), 1.0, 0.0).astype(BF16)
    cs_c = _cumsum_rows(tril, dt_c * (-jnp.exp(alog_r)))
    cs_r = _cumsum_lanes(tril, dt_r * (-jnp.exp(alog_c)))
    return dt_c, cs_c, cs_r


def _ssd_chunk(r_heads, xs, bg, cg, dt_c, cs_c, cs_rg, e01, dskip_e, hp):
    l_len, rp = xs.shape
    p = rp // r_heads
    causal = _causal_mask(l_len)
    lane_head = lax.broadcasted_iota(jnp.int32, (1, rp), 1) // p
    dt_e = _expand(e01, dt_c)
    cs_e = _expand(e01, cs_c)
    cl_e = cs_e[l_len - 1:l_len, :]
    x = xs * dt_e
    bgb, cgb = bg.astype(BF16), cg.astype(BF16)
    cb = lax.dot_general(cgb, bgb, _NT, preferred_element_type=F32)
    ms, xm = [], []
    for r in range(r_heads):
        seg = cs_e[:, r * p:r * p + 1] - cs_rg[r:r + 1, :]
        decay = jnp.exp(jnp.where(causal, seg, -1e30))
        ms.append((cb * decay).astype(BF16))
        xm.append(jnp.where(lane_head == r, x, 0.0).astype(BF16))
    y_diag = lax.dot_general(jnp.concatenate(ms, axis=1), jnp.concatenate(xm, axis=0), _NN,
                             preferred_element_type=F32)
    y_off = lax.dot_general(cgb, hp.astype(BF16), _NN, preferred_element_type=F32) * jnp.exp(cs_e)
    xd = (x * jnp.exp(cl_e - cs_e)).astype(BF16)
    states = lax.dot_general(bgb, xd, _TN, preferred_element_type=F32)
    h_next = hp * jnp.exp(cl_e) + states
    y = y_diag + y_off + dskip_e * xs
    return y, h_next


def _ssd_dt(dtc, dtr, small, cots=None):
    t_len, heads = dtc.shape[0], dtr.shape[0]
    nc = t_len // CHUNK
    col = pl.BlockSpec((CHUNK, LANES), lambda c: (c, 0))
    row = pl.BlockSpec((heads, CHUNK), lambda c: (0, c))
    full = [pl.BlockSpec(s.shape, lambda c: (0, 0)) for s in small]
    shapes = [jax.ShapeDtypeStruct((t_len, LANES), F32), jax.ShapeDtypeStruct((t_len, LANES), F32),
              jax.ShapeDtypeStruct((heads, t_len), F32)]
    if cots is None:
        def body(dtc_ref, dtr_ref, br, bc, ar, ac, dt_ref, csc_ref, csr_ref):
            dt_ref[...], csc_ref[...], csr_ref[...] = _dt_prep(dtc_ref[...], dtr_ref[...], br[...], bc[...],
                                                                ar[...], ac[...])
        return _call(body, name="ssd_dt", grid=(nc,), in_specs=[col, row] + full, out_specs=[col, col, row],
                     out_shape=shapes, args=(dtc, dtr, *small), sem=("parallel",))[0]

    g_dt, g_csc, g_csr, ddk, e01 = cots

    def body(dtc_ref, dtr_ref, br, bc, ar, ac, g_dt_ref, g_csc_ref, g_csr_ref, ddk_ref, e_ref,
             ddtc_ref, ddtr_ref, *dsmall):
        _, vjp = jax.vjp(_dt_prep, dtc_ref[...], dtr_ref[...], br[...], bc[...], ar[...], ac[...])
        grads = vjp((g_dt_ref[...], g_csc_ref[...], g_csr_ref[...]))
        ddtc_ref[...], ddtr_ref[...] = grads[0], grads[1]
        ddk8 = jnp.broadcast_to(ddk_ref[...], (8, ddk_ref.shape[1]))
        dskip = _exact_dot(ddk8, e_ref[...], _NT, True)[0:1, :]

        @pl.when(pl.program_id(0) == 0)
        def _():
            for r in dsmall:
                r[...] = jnp.zeros_like(r)

        for r, gr in zip(dsmall, tuple(grads[2:]) + (dskip,)):
            r[...] += gr

    acc = list(small) + [small[0]]
    return _call(body, name="d_ssd_dt", grid=(nc,),
                 in_specs=[col, row] + full + [col, col, row, pl.BlockSpec((None, 1, e01.shape[1]), lambda c: (c, 0, 0)),
                                               pl.BlockSpec(e01.shape, lambda c: (0, 0))],
                 out_specs=[col, row] + [pl.BlockSpec(s.shape, lambda c: (0, 0)) for s in acc],
                 out_shape=[shapes[0], shapes[2]] + [jax.ShapeDtypeStruct(s.shape, F32) for s in acc],
                 args=(dtc, dtr, *small, g_dt, g_csc, g_csr, ddk, e01), sem=("arbitrary",))[0]


def _ssd_specs(t_len, d_ssm, r_heads, reverse):
    rp = r_heads * HEADDIM
    nc = t_len // CHUNK
    per = next(p for p in (SSD_CHUNKS_PER_STEP, 2, 1) if nc % p == 0)
    ns, rows = nc // per, per * CHUNK
    cidx = (lambda c: ns - 1 - c) if reverse else (lambda c: c)
    b_off = d_ssm // N_STATE
    specs = dict(
        xs=pl.BlockSpec((rows, rp), lambda c, g: (cidx(c), g)),
        b=pl.BlockSpec((rows, N_STATE), lambda c, g: (cidx(c), b_off + g)),
        c=pl.BlockSpec((rows, N_STATE), lambda c, g: (cidx(c), b_off + N_GROUPS + g)),
        grad_bc=pl.BlockSpec((rows, N_STATE), lambda c, g: (cidx(c), g)),
        col=pl.BlockSpec((rows, LANES), lambda c, g: (cidx(c), 0)),
        csr=pl.BlockSpec((None, r_heads, rows), lambda c, g: (g, 0, cidx(c))),
        e01=pl.BlockSpec((LANES, rp), lambda c, g: (0, g)),
        dskip=pl.BlockSpec((1, rp), lambda c, g: (0, g)),
        hprev=pl.BlockSpec((per, None, N_STATE, rp), lambda c, g: (cidx(c), g, 0, 0)),
        ddk=pl.BlockSpec((per, 1, rp), lambda c, g: (cidx(c), 0, g)),
    )
    return specs, nc, ns, per, rp


def _ssd_fwd(xbc, dt_c, cs_c, cs_r3, e01, dskip_e, *, d_ssm, r_heads, ride=None):
    t_len = xbc.shape[0]
    sp, nc, ns, per, rp = _ssd_specs(t_len, d_ssm, r_heads, False)

    def body(xs_ref, b_ref, c_ref, dt_ref, csc_ref, csr_ref, e_ref, dk_ref, y_ref, hprev_ref, h_ref):
        c, g = pl.program_id(0), pl.program_id(1)

        @pl.when(c == 0)
        def _():
            h_ref[g] = jnp.zeros((N_STATE, rp), F32)

        hp = h_ref[g]
        for s in range(per):
            r = pl.ds(s * CHUNK, CHUNK)
            hprev_ref[s] = hp
            y, hp = _ssd_chunk(r_heads, xs_ref[r, :].astype(F32), b_ref[r, :].astype(F32), c_ref[r, :].astype(F32),
                               dt_ref[r, :], csc_ref[r, :], csr_ref[:, r], e_ref[...], dk_ref[...], hp)
            y_ref[r, :] = y
        h_ref[g] = hp

    return _call(
        body, name="ssd_fwd", grid=(ns, N_GROUPS),
        in_specs=[sp["xs"], sp["b"], sp["c"], sp["col"], sp["col"], sp["csr"], sp["e01"], sp["dskip"]],
        out_specs=[sp["xs"], sp["hprev"]],
        out_shape=[jax.ShapeDtypeStruct((t_len, d_ssm), F32),
                   jax.ShapeDtypeStruct((nc, N_GROUPS, N_STATE, rp), F32)],
        args=(xbc, xbc, xbc, dt_c, cs_c, cs_r3, e01, dskip_e), scratch=[pltpu.VMEM((N_GROUPS, N_STATE, rp), F32)],
        sem=("arbitrary", "arbitrary"), ride=ride)


def _ssd_bwd(xbc, dt_c, cs_c, cs_r3, e01, dskip_e, hprev, dy, *, d_ssm, r_heads, ride=None):
    t_len = xbc.shape[0]
    sp, nc, ns, per, rp = _ssd_specs(t_len, d_ssm, r_heads, True)

    def body(xs_ref, b_ref, c_ref, dt_ref, csc_ref, csr_ref, e_ref, dk_ref, hprev_ref, dy_ref,
             dxs_ref, db_ref, dc_ref, ddt_ref, dcsc_ref, dcsr_ref, ddk_ref, dh_ref):
        c, g = pl.program_id(0), pl.program_id(1)

        @pl.when(c == 0)
        def _():
            dh_ref[g] = jnp.zeros((N_STATE, rp), F32)

        @pl.when(g == 0)
        def _():
            ddt_ref[...] = jnp.zeros_like(ddt_ref)
            dcsc_ref[...] = jnp.zeros_like(dcsc_ref)

        e01 = e_ref[...]
        fn = lambda xs, bg, cg, dt, csc, csr, dk, hp: _ssd_chunk(r_heads, xs, bg, cg, dt, csc, csr, e01, dk, hp)
        dh = dh_ref[g]
        for s in reversed(range(per)):
            r = pl.ds(s * CHUNK, CHUNK)
            _, vjp = jax.vjp(fn, xs_ref[r, :].astype(F32), b_ref[r, :].astype(F32), c_ref[r, :].astype(F32),
                             dt_ref[r, :], csc_ref[r, :], csr_ref[:, r], dk_ref[...], hprev_ref[s])
            dxs, dbg, dcg, ddt, dcsc, dcsr, ddk, dh = vjp((dy_ref[r, :], dh))
            dxs_ref[r, :] = dxs.astype(dxs_ref.dtype)
            db_ref[r, :] = dbg.astype(db_ref.dtype)
            dc_ref[r, :] = dcg.astype(dc_ref.dtype)
            ddt_ref[r, :] += ddt
            dcsc_ref[r, :] += dcsc
            dcsr_ref[:, r] = dcsr
            ddk_ref[s] = ddk
        dh_ref[g] = dh

    n_bc = N_GROUPS * N_STATE
    return _call(
        body, name="ssd_bwd", grid=(ns, N_GROUPS),
        in_specs=[sp["xs"], sp["b"], sp["c"], sp["col"], sp["col"], sp["csr"], sp["e01"], sp["dskip"], sp["hprev"],
                  sp["xs"]],
        out_specs=[sp["xs"], sp["grad_bc"], sp["grad_bc"], sp["col"], sp["col"], sp["csr"], sp["ddk"]],
        out_shape=[jax.ShapeDtypeStruct((t_len, d_ssm), BF16), jax.ShapeDtypeStruct((t_len, n_bc), BF16),
                   jax.ShapeDtypeStruct((t_len, n_bc), BF16), jax.ShapeDtypeStruct(dt_c.shape, F32),
                   jax.ShapeDtypeStruct(cs_c.shape, F32), jax.ShapeDtypeStruct(cs_r3.shape, F32),
                   jax.ShapeDtypeStruct((nc, 1, d_ssm), F32)],
        args=(xbc, xbc, xbc, dt_c, cs_c, cs_r3, e01, dskip_e, hprev, dy),
        scratch=[pltpu.VMEM((N_GROUPS, N_STATE, rp), F32)], sem=("arbitrary", "arbitrary"), ride=ride)


def _chip_sum(src, sib, *, name):
    rows, cols = src.shape[1:]
    tr = _tile(rows, 256, BF16_ROWS)
    core = lax.axis_index("c").astype(jnp.int32).reshape(1)

    def body(c_ref, a_ref, b_ref, o_ref):
        o_ref[...] = (a_ref[...].astype(F32) + b_ref[...].astype(F32)).astype(o_ref.dtype)

    grid_spec = pltpu.PrefetchScalarGridSpec(
        num_scalar_prefetch=1, grid=(N_CHIPS, rows // tr),
        in_specs=[pl.BlockSpec((None, tr, cols), lambda q, i, c_ref: (2 * q + c_ref[0], i, 0)),
                  pl.BlockSpec((None, tr, cols), lambda q, i, c_ref: (q, i, 0))],
        out_specs=pl.BlockSpec((None, tr, cols), lambda q, i, c_ref: (q, i, 0)))
    return pl.pallas_call(
        body, name=name, grid_spec=grid_spec, out_shape=jax.ShapeDtypeStruct(sib.shape, sib.dtype),
        compiler_params=pltpu.CompilerParams(dimension_semantics=("parallel", "parallel"), vmem_limit_bytes=VMEM_LIMIT),
    )(core, src, sib)


def _adamw(w, g, m, v):
    m = ADAM_B1 * m + (1.0 - ADAM_B1) * g
    v = ADAM_B2 * v + (1.0 - ADAM_B2) * (g * g)
    m_hat = m / (1.0 - ADAM_B1 ** ADAM_STEP)
    v_hat = v / (1.0 - ADAM_B2 ** ADAM_STEP)
    delta = -ADAM_LR * (m_hat / (jnp.sqrt(v_hat) + ADAM_EPS) + ADAM_WD * w)
    return delta, m, v


def _reduce_adamw(parts, w, m, v, *, name):
    n_parts = parts.shape[0]
    rows, cols = w.shape
    tr = _tile(rows, 128, BF16_ROWS)

    def body(p_ref, w_ref, m_ref, v_ref, g_ref, d_ref, mo_ref, vo_ref):
        g = p_ref[0].astype(F32)
        for k in range(1, n_parts):
            g = g + p_ref[k].astype(F32)
        delta, mn, vn = _adamw(w_ref[...], g, m_ref[...], v_ref[...])
        g_ref[...] = g
        d_ref[...] = delta
        mo_ref[...] = mn
        vo_ref[...] = vn

    spec = pl.BlockSpec((tr, cols), lambda i: (i, 0))
    outs, _ = _call(
        body, name=name, grid=(rows // tr,),
        in_specs=[pl.BlockSpec((n_parts, tr, cols), lambda i: (0, i, 0)), spec, spec, spec],
        out_specs=[spec] * 4, out_shape=[jax.ShapeDtypeStruct((rows, cols), F32)] * 4,
        args=(parts, w, m, v), sem=("parallel",))
    return outs


def _cols_of(g):
    return jnp.transpose(g, (1, 0, 2)).reshape(g.shape[1], -1)


def _pad_to(a, rows, cols):
    return jnp.pad(a, ((0, rows - a.shape[0]), (0, cols - a.shape[1])))


def kernel(x, norm_mix_g, w_in, ssm_conv_w, ssm_conv_b, ssm_dt_bias, ssm_A_log, ssm_D, ssm_norm_g, sc_conv_w, w_out, norm_ffn_g, w_gate, w_up, w_down, norm_final_g, loss_target, m_norm_mix_g, m_w_in, m_ssm_conv_w, m_ssm_conv_b, m_ssm_dt_bias, m_ssm_A_log, m_ssm_D, m_ssm_norm_g, m_sc_conv_w, m_w_out, m_norm_ffn_g, m_w_gate, m_w_up, m_w_down, m_norm_final_g, v_norm_mix_g, v_w_in, v_ssm_conv_w, v_ssm_conv_b, v_ssm_dt_bias, v_ssm_A_log, v_ssm_D, v_ssm_norm_g, v_sc_conv_w, v_w_out, v_norm_ffn_g, v_w_gate, v_w_up, v_w_down, v_norm_final_g):
    t_len, d = x.shape[1], x.shape[2]
    heads = d // HEADDIM
    r_heads = heads // N_GROUPS
    d_xbc = d + 2 * N_GROUPS * N_STATE
    ff_s = w_down.shape[1]
    ff = ff_s * N_DEV
    off_xbc, off_dt = d, d + d_xbc
    off_cb = off_dt + heads
    d_in = off_cb + 3 * d
    in_s = d_in // N_DEV
    in_p = -(-in_s // (2 * BF16_ROWS)) * (2 * BF16_ROWS)
    w_main = 4 * d + d_xbc
    me = 4 * lax.axis_index("x") + 2 * lax.axis_index("y") + lax.axis_index("c")

    x2 = x[0]
    target = loss_target[0]

    tpose = lambda a: jnp.transpose(a[0])
    win_s = _pad_to(tpose(w_in).astype(BF16), in_p, d)
    wg_s, wu_s = tpose(w_gate).astype(BF16), tpose(w_up).astype(BF16)
    wo_s, wd_s = w_out[0].astype(BF16), w_down[0].astype(BF16)
    small_w = jnp.concatenate([_pad_to(ssm_conv_w[0], K_SSM, d_xbc // N_DEV),
                               _pad_to(sc_conv_w[0], K_SC + 1, d_xbc // N_DEV)], axis=0)

    g1, g2, g3 = norm_mix_g, norm_ffn_g, norm_final_g.reshape(1, d)
    gs = ssm_norm_g
    small = [_pad_to(ssm_dt_bias, 1, LANES), ssm_dt_bias.reshape(heads, 1), _pad_to(ssm_A_log, 1, LANES),
             ssm_A_log.reshape(heads, 1)]
    e01 = (lax.broadcasted_iota(jnp.int32, (LANES, d), 1) // HEADDIM
           == lax.broadcasted_iota(jnp.int32, (LANES, d), 0)).astype(BF16)
    dskip_e = jnp.repeat(ssm_D, HEADDIM, axis=1)
    tr = _tile(t_len, 256, 8)
    tr_ff = _tile(t_len, 128, 8)
    cw = LANES
    slab = lambda col: col // cw

    gin_1, gsm_1 = _gather_chips_relayed([win_s], [small_w], "gather_w_in_chips")
    (n1,), (gin, gsm) = _rows_call(lambda v, g: ((_rms(v, g),), ()), rows=t_len, tr=tr, row_ins=[(x2, d, 0)],
                                   full_ins=[g1], row_outs=[(d, BF16)], acc_outs=[], name="norm_mix",
                                   ride=_gather_sibling([gin_1, gsm_1]))
    in_pieces = []
    for k in range(N_DEV):
        for a, b, dst, shift in ((0, off_dt, 0, 0), (off_dt, off_cb, 1, -off_dt), (off_cb, d_in, 0, -heads)):
            s, e = max(k * in_s, a), min((k + 1) * in_s, b)
            if s < e:
                in_pieces.append((k, s - k * in_s, e - s, dst, s + shift))
    wtm, wtdt = jnp.zeros((w_main, d), BF16), jnp.zeros((LANES, d), BF16)
    for k, r0, n, dst, d0 in in_pieces:
        if dst == 0:
            wtm = lax.dynamic_update_slice(wtm, gin[k, r0:r0 + n], (d0, 0))
        else:
            wtdt = lax.dynamic_update_slice(wtdt, gin[k, r0:r0 + n], (d0, 0))
    cw_ssm = _cols_of(gsm[:, :K_SSM, :])
    cw_sc = _cols_of(gsm[:, K_SSM:K_SSM + K_SC, :d // N_DEV])

    proj, (go_1, gg_1) = _matmul(n1, wtm, tb=True, out_dtype=BF16, name="proj_main",
                                 ride=_gather_chips([wo_s, wg_s]))
    dt_raw, _ = _matmul(n1, wtdt, tb=True, out_dtype=F32, name="proj_dt")
    dt_raw_t = jnp.transpose(dt_raw[:, :heads])
    (xbc,), (go, gg) = _cols_call(_conv_silu_fwd, rows=t_len, cols=d_xbc, cw=cw, col_ins=[(proj, slab(off_xbc))],
                                  par_ins=[(cw_ssm, 0), (ssm_conv_b, 0)], col_outs=[BF16], par_outs=[],
                                  name="ssm_conv", ride=_gather_sibling([go_1, gg_1]))
    dt_c, cs_c, cs_r = _ssd_dt(dt_raw, dt_raw_t, small)
    cs_r3 = cs_r.reshape(N_GROUPS, r_heads, t_len)
    (y_ssd, hprev), (gu_1,) = _ssd_fwd(xbc, dt_c, cs_c, cs_r3, e01, dskip_e, d_ssm=d, r_heads=r_heads,
                                       ride=_gather_chips([wu_s]))

    def gate_norm(y, z, g):
        z = z.astype(F32)
        return _rms(y * (z * _sigmoid(z)), g)

    (y_mix,), (gu,) = _rows_call(lambda y, z, g: ((gate_norm(y, z, g),), ()), rows=t_len, tr=tr,
                                 row_ins=[(y_ssd, d, 0), (proj, d, 0)], full_ins=[gs], row_outs=[(d, BF16, 2 * d)],
                                 acc_outs=[], name="ssm_gate_norm", ride=_gather_sibling([gu_1]))
    wgt, wut, wo = gg.reshape(ff, d), gu.reshape(ff, d), go.reshape(2 * d, d)
    sc0 = slab(d + d_xbc)
    (y_mix,), _ = _cols_call(_shortconv_fwd, rows=t_len, cols=d, cw=cw,
                             col_ins=[(proj, sc0), (proj, sc0 + slab(d)), (proj, sc0 + 2 * slab(d))],
                             par_ins=[(cw_sc, 0)], col_outs=[BF16], par_outs=[], name="shortconv",
                             into=(y_mix, slab(d)))
    h1, _ = _matmul(y_mix, wo, out_dtype=F32, add=x2, name="out_proj")
    (n2,), _ = _rows_call(lambda v, g: ((_rms(v, g),), ()), rows=t_len, tr=tr, row_ins=[(h1, d, 0)], full_ins=[g2],
                          row_outs=[(d, BF16)], acc_outs=[], name="norm_ffn")
    g_ff, (gd_1,) = _matmul(n2, wgt, tb=True, out_dtype=BF16, name="ffn_gate",
                            ride=_gather_chips([wd_s]))
    (u_ff, a_ff), (gd,) = _matmul(n2, wut, tb=True, name="ffn_up", ride=_gather_sibling([gd_1]),
                                  post=(lambda uv, gv: (uv, gv * _sigmoid(gv) * uv), [g_ff], [BF16, BF16]),
                                  tn_max=MM_TILE_N_POST)
    wd = gd.reshape(ff, d)
    h2, _ = _matmul(a_ff, wd, out_dtype=F32, add=h1, name="ffn_down")

    def head(hv, tv, g):
        def f(hh, gg_):
            e = _rms(hh, gg_) - tv
            return (0.5 / d) * jnp.sum(e * e)
        val, (dh, dg) = jax.value_and_grad(f, argnums=(0, 1))(hv, g)
        return (dh, dh), (jnp.full((1, LANES), val, F32), dg)

    (dh2, dh2_b, loss_acc, dg3), _ = _rows_call(head, rows=t_len, tr=tr, row_ins=[(h2, d, 0), (target, d, 0)],
                                                full_ins=[g3], row_outs=[(d, F32), (d, BF16)],
                                                acc_outs=[(1, LANES), (1, d)], name="loss_head")
    loss = lax.psum(loss_acc[0, 0], ("x", "y", "c"))

    def act_bwd(dav, gv, uv):
        s = _sigmoid(gv)
        return dav * uv * (s * (1.0 + gv * (1.0 - s))), dav * gv * s

    (dg_ff, du_ff), _ = _matmul(dh2_b, wd, tb=True, name="d_ffn_gate_up",
                                post=(act_bwd, [g_ff, u_ff], [BF16, BF16]), tn_max=MM_TILE_N_POST)
    dwd, _ = _matmul(a_ff, dh2_b, ta=True, out_dtype=BF16, name="d_w_down")
    dwd8 = dwd.reshape(N_DEV, ff_s, d)
    dn2, (sib_d,) = _matmul(dg_ff, wgt, out_dtype=F32, name="d_norm_ffn_out_gate", ride=_scatter_sibling([dwd8]))
    chip_d = _chip_sum(dwd8, sib_d, name="chip_sum_w_down")
    dn2, (parts_d,) = _matmul(du_ff, wut, out_dtype=F32, add=dn2, name="d_norm_ffn_out_up",
                              ride=_scatter_chips([chip_d]))
    dwg, _ = _matmul(dg_ff, n2, ta=True, out_dtype=BF16, name="d_w_gate")
    dwu, _ = _matmul(du_ff, n2, ta=True, out_dtype=BF16, name="d_w_up")
    dwg8, dwu8 = dwg.reshape(N_DEV, ff_s, d), dwu.reshape(N_DEV, ff_s, d)

    def norm_bwd(v, dn, dres, g):
        _, vjp = jax.vjp(_rms, v, g)
        dv, dg = vjp(dn)
        return (dv + dres,), (dg,)

    def norm_bwd_2(v, dn, dres, g):
        (dv,), acc = norm_bwd(v, dn, dres, g)
        return (dv, dv), acc

    (dh1, dh1_b, dg2), (sib_g, sib_u) = _rows_call(norm_bwd_2, rows=t_len, tr=tr,
                                                   row_ins=[(h1, d, 0), (dn2, d, 0), (dh2, d, 0)], full_ins=[g2],
                                                   row_outs=[(d, F32), (d, BF16)], acc_outs=[(1, d)], name="d_norm_ffn",
                                                   ride=_scatter_sibling([dwg8, dwu8]))
    chip_g = _chip_sum(dwg8, sib_g, name="chip_sum_w_gate")
    chip_u = _chip_sum(dwu8, sib_u, name="chip_sum_w_up")

    dy_mix, _ = _matmul(dh1_b, wo, tb=True, out_dtype=BF16, name="d_y_mix")
    dwo, _ = _matmul(y_mix, dh1_b, ta=True, out_dtype=BF16, name="d_w_out")
    dwo8 = dwo.reshape(N_DEV, 2 * d // N_DEV, d)
    (dgb, dgc, du, dcw_sc), (sib_o,) = _cols_call(
        _shortconv_bwd, rows=t_len, cols=d, cw=cw,
        col_ins=[(proj, sc0), (proj, sc0 + slab(d)), (proj, sc0 + 2 * slab(d)), (dy_mix, slab(d))],
        par_ins=[(cw_sc, 0)], col_outs=[BF16] * 3, par_outs=[K_SC], name="d_shortconv",
        ride=_scatter_sibling([dwo8]))
    chip_o = _chip_sum(dwo8, sib_o, name="chip_sum_w_out")

    def gate_norm_bwd(y, z, dyo, g):
        _, vjp = jax.vjp(gate_norm, y, z.astype(F32), g)
        dy, dz, dg = vjp(dyo.astype(F32))
        return (dy, dz), (dg,)

    (dy_ssd, dproj, dgs), _ = _rows_call(gate_norm_bwd, rows=t_len, tr=tr,
                                         row_ins=[(y_ssd, d, 0), (proj, d, 0), (dy_mix, d, 0)], full_ins=[gs],
                                         row_outs=[(d, F32), (d, BF16, w_main)], acc_outs=[(1, d)],
                                         name="d_ssm_gate_norm")
    (dxs, dbm, dcm, g_dt, g_csc, g_csr3, ddk), (parts_g, parts_u, parts_o) = _ssd_bwd(
        xbc, dt_c, cs_c, cs_r3, e01, dskip_e, hprev, dy_ssd, d_ssm=d, r_heads=r_heads,
        ride=_scatter_chips([chip_g, chip_u, chip_o]))
    ddt_c, ddt_r, dbias_r, dbias_c, dalog_r, dalog_c, ddskip = _ssd_dt(
        dt_raw, dt_raw_t, small, cots=(g_dt, g_csc, g_csr3.reshape(heads, t_len), ddk, e01))
    dcw_parts, dcb_parts, col0 = [], [], 0
    for tag, dpart in (("x", dxs), ("b", dbm), ("c", dcm)):
        (dproj, dcw_p, dcb_p), _ = _cols_call(
            _conv_silu_bwd, rows=t_len, cols=dpart.shape[1], cw=cw,
            col_ins=[(proj, slab(off_xbc + col0)), (dpart, 0)], par_ins=[(cw_ssm, slab(col0)), (ssm_conv_b, slab(col0))],
            col_outs=[BF16], par_outs=[K_SSM, 1], name="d_ssm_conv_" + tag, into=(dproj, slab(off_xbc + col0)))
        dcw_parts.append(dcw_p)
        dcb_parts.append(dcb_p)
        col0 += dpart.shape[1]
    dcw_ssm, dcb_ssm = jnp.concatenate(dcw_parts, axis=1), jnp.concatenate(dcb_parts, axis=1)
    for i, part in enumerate((dgb, dgc, du)):
        dproj = lax.dynamic_update_slice(dproj, part, (0, d + d_xbc + i * d))
    ddt = ddt_c + _pad_to(jnp.transpose(ddt_r), t_len, LANES)
    dwm, _ = _matmul(dproj, n1, ta=True, out_dtype=BF16, name="d_w_in_main")
    dwdt, _ = _matmul(ddt, n1, ta=True, out_dtype=BF16, name="d_w_in_dt")
    blocks = []
    for k in range(N_DEV):
        pcs = [((dwm if dst == 0 else dwdt)[d0:d0 + n], r0) for kk, r0, n, dst, d0 in in_pieces if kk == k]
        blk = jnp.pad(pcs[0][0], ((0, in_p - pcs[0][0].shape[0]), (0, 0)))
        for piece, r0 in pcs[1:]:
            blk = lax.dynamic_update_slice(blk, piece, (r0, 0))
        blocks.append(blk)
    dwin8 = jnp.stack(blocks)
    dn1, (sib_in,) = _matmul(ddt, wtdt, out_dtype=F32, name="d_norm_mix_out_dt", ride=_scatter_sibling([dwin8]))
    chip_in = _chip_sum(dwin8, sib_in, name="chip_sum_w_in")
    dn1, (parts_in,) = _matmul(dproj, wtm, out_dtype=F32, add=dn1, name="d_norm_mix_out",
                               ride=_scatter_chips([chip_in]))
    (dx, dg1), _ = _rows_call(norm_bwd, rows=t_len, tr=tr, row_ins=[(x2, d, 0), (dn1, d, 0), (dh1, d, 0)],
                              full_ins=[g1], row_outs=[(d, F32)], acc_outs=[(1, d)], name="d_norm_mix")

    wide = d_xbc
    rows_small = [dg1, dcb_ssm, dbias_r + _pad_to(dbias_c.reshape(1, heads), 1, LANES),
                  dalog_r + _pad_to(dalog_c.reshape(1, heads), 1, LANES), ddskip, dgs, dg2, dg3]
    packed = jnp.concatenate([_pad_to(r, 1, wide) for r in rows_small]
                             + [dcw_ssm, _pad_to(dcw_sc, K_SC, wide), jnp.zeros((1, wide), F32)], axis=0)
    (p_small,) = _comm(_gather_all([packed]), "gather_small_grads")

    conv_lo = me * (d_xbc // N_DEV)
    sc_lo = me * (d // N_DEV)

    def pack_state(vals):
        (nm, cb, dtb, al, dk, sg, nf, nfin, cws, scs) = vals
        rows = [_pad_to(a.reshape(1, -1), 1, wide) for a in (nm, cb, dtb, al, dk, sg, nf, nfin)]
        cws_full = lax.dynamic_update_slice(jnp.zeros((K_SSM, wide), F32), cws[0], (0, conv_lo))
        scs_full = lax.dynamic_update_slice(jnp.zeros((K_SC, wide), F32), scs[0], (0, sc_lo))
        return jnp.concatenate(rows + [cws_full, scs_full, jnp.zeros((1, wide), F32)], axis=0)

    w_small = pack_state((norm_mix_g, ssm_conv_b, ssm_dt_bias, ssm_A_log, ssm_D, ssm_norm_g, norm_ffn_g, norm_final_g,
                          ssm_conv_w, sc_conv_w))
    m_small = pack_state((m_norm_mix_g, m_ssm_conv_b, m_ssm_dt_bias, m_ssm_A_log, m_ssm_D, m_ssm_norm_g, m_norm_ffn_g,
                          m_norm_final_g, m_ssm_conv_w, m_sc_conv_w))
    v_small = pack_state((v_norm_mix_g, v_ssm_conv_b, v_ssm_dt_bias, v_ssm_A_log, v_ssm_D, v_ssm_norm_g, v_norm_ffn_g,
                          v_norm_final_g, v_ssm_conv_w, v_sc_conv_w))

    tin = lambda a: _pad_to(tpose(a), in_p, d)
    tin_back = lambda a: jnp.transpose(a[:in_s])[None]
    t_back = lambda a: jnp.transpose(a)[None]
    upd = {
        "w_in": [tin_back(o) for o in _reduce_adamw(parts_in, tin(w_in), tin(m_w_in), tin(v_w_in), name="adamw_w_in")],
        "w_out": [o[None] for o in _reduce_adamw(parts_o, w_out[0], m_w_out[0], v_w_out[0], name="adamw_w_out")],
        "w_gate": [t_back(o) for o in _reduce_adamw(parts_g, tpose(w_gate), tpose(m_w_gate), tpose(v_w_gate),
                                                    name="adamw_w_gate")],
        "w_up": [t_back(o) for o in _reduce_adamw(parts_u, tpose(w_up), tpose(m_w_up), tpose(v_w_up),
                                                  name="adamw_w_up")],
        "w_down": [o[None] for o in _reduce_adamw(parts_d, w_down[0], m_w_down[0], v_w_down[0], name="adamw_w_down")],
    }
    small_upd = _reduce_adamw(p_small, w_small, m_small, v_small, name="adamw_small")

    def unpack(packed_out):
        vec = lambda i, n, shape: packed_out[i, :n].reshape(shape)
        return {
            "norm_mix_g": vec(0, d, (1, d)), "ssm_conv_b": vec(1, d_xbc, (1, d_xbc)),
            "ssm_dt_bias": vec(2, heads, (1, heads)), "ssm_A_log": vec(3, heads, (1, heads)),
            "ssm_D": vec(4, heads, (1, heads)), "ssm_norm_g": vec(5, d, (1, d)), "norm_ffn_g": vec(6, d, (1, d)),
            "norm_final_g": vec(7, d, (d,)),
            "ssm_conv_w": lax.dynamic_slice(packed_out[8:8 + K_SSM], (0, conv_lo), (K_SSM, d_xbc // N_DEV))[None],
            "sc_conv_w": lax.dynamic_slice(packed_out[8 + K_SSM:8 + K_SSM + K_SC], (0, sc_lo), (K_SC, d // N_DEV))[None],
        }

    names = ["norm_mix_g", "w_in", "ssm_conv_w", "ssm_conv_b", "ssm_dt_bias", "ssm_A_log", "ssm_D", "ssm_norm_g",
             "sc_conv_w", "w_out", "norm_ffn_g", "w_gate", "w_up", "w_down", "norm_final_g"]
    outs = []
    for kind in range(4):
        small_k = unpack(small_upd[kind])
        for nm in names:
            outs.append(upd[nm][kind] if nm in upd else small_k[nm])
    return (loss, dx[None], *outs)
```

```python
import collections
import functools

import jax
import jax.numpy as jnp
from jax import lax
from jax.experimental import pallas as pl
from jax.experimental.pallas import tpu as pltpu

F32 = jnp.float32
BF16 = jnp.bfloat16

N_DEV = 8
N_CHIPS = 4
HEADDIM = 64
N_GROUPS = 8
N_STATE = 128
CHUNK = 128
K_SSM = 4
K_SC = 3
EPS = 1e-5
LANES = 128
BF16_ROWS = 16
MM_TILE_MN = 1408
MM_TILE_K = 2048
MM_TILE_N_POST = 704
SSD_CHUNKS_PER_STEP = 4
ROW_BLOCK = 256
V7X_VMEM_BYTES = 64 * 1024 * 1024
VMEM_LIMIT = (V7X_VMEM_BYTES * 3) // 4

ADAM_LR = 0.001
ADAM_B1 = 0.9
ADAM_B2 = 0.999
ADAM_EPS = 1e-08
ADAM_WD = 0.01
ADAM_STEP = 10


def _tile(n, pref, align):
    t = min(pref, n)
    t -= t % align
    while t >= align:
        if n % t == 0:
            return t
        t -= align
    return n


_Ride = collections.namedtuple("_Ride", ["ins", "out_shapes", "aliases", "nsem", "plan"])
_ANY = pl.BlockSpec(memory_space=pl.ANY)


def _coords():
    return lax.axis_index("x"), lax.axis_index("y"), lax.axis_index("c")


def _other_chips(x, y):
    return ((1 - x, y), (x, 1 - y), (1 - x, 1 - y))


def _remote(src, dst, send, recv, k, dev):
    return functools.partial(pltpu.make_async_remote_copy, src_ref=src, dst_ref=dst, send_sem=send.at[k],
                             recv_sem=recv.at[k], device_id=dev, device_id_type=pl.DeviceIdType.MESH)


def _local(src, dst, sem):
    return functools.partial(pltpu.make_async_copy, src, dst, sem)


def _start_all(plan):
    for kind, make in plan:
        if kind != "arrival":
            make().start()


def _wait_all(plan):
    for kind, make in plan:
        if kind == "local":
            make().wait()
        elif kind == "out":
            make().wait_send()
        else:
            make().wait_recv()


def _gather_chips(srcs):
    def plan(ins, outs, send, recv, base):
        x, y, c = _coords()
        me = 4 * x + 2 * y + c
        d = []
        for a, (src, dst) in enumerate(zip(ins, outs)):
            k = base + 4 * a
            d.append(("local", _local(src, dst.at[me], send.at[k + 3])))
            for j, (px, py) in enumerate(_other_chips(x, y)):
                d.append(("out", _remote(src, dst.at[me], send, recv, k + j, (px, py, c))))
                d.append(("arrival", _remote(src, dst.at[4 * px + 2 * py + c], send, recv, k + j, (px, py, c))))
        return d
    shapes = [jax.ShapeDtypeStruct((N_DEV,) + s.shape, s.dtype) for s in srcs]
    return _Ride(list(srcs), shapes, {}, 4 * len(srcs), plan)


def _gather_sibling(bufs):
    def plan(ins, outs, send, recv, base):
        x, y, c = _coords()
        d = []
        for a, buf in enumerate(outs):
            for q in range(N_CHIPS):
                k = base + 4 * a + q
                d.append(("out", _remote(buf.at[2 * q + c], buf.at[2 * q + c], send, recv, k, (x, y, 1 - c))))
                d.append(("arrival", _remote(buf.at[2 * q + c], buf.at[2 * q + 1 - c], send, recv, k, (x, y, 1 - c))))
        return d
    shapes = [jax.ShapeDtypeStruct(b.shape, b.dtype) for b in bufs]
    return _Ride(list(bufs), shapes, {i: i for i in range(len(bufs))}, 4 * len(bufs), plan)


def _scatter_sibling(srcs):
    def plan(ins, outs, send, recv, base):
        x, y, c = _coords()
        d = []
        for a, (src, sib) in enumerate(zip(ins, outs)):
            for q in range(N_CHIPS):
                k = base + 4 * a + q
                d.append(("out", _remote(src.at[2 * q + 1 - c], sib.at[q], send, recv, k, (x, y, 1 - c))))
                d.append(("arrival", _remote(src.at[2 * q + 1 - c], sib.at[q], send, recv, k, (x, y, 1 - c))))
        return d
    shapes = [jax.ShapeDtypeStruct((N_CHIPS,) + s.shape[1:], s.dtype) for s in srcs]
    return _Ride(list(srcs), shapes, {}, 4 * len(srcs), plan)


def _scatter_chips(chips):
    def plan(ins, outs, send, recv, base):
        x, y, c = _coords()
        mine = 2 * x + y
        d = []
        for a, (chip, parts) in enumerate(zip(ins, outs)):
            k = base + 4 * a
            d.append(("local", _local(chip.at[mine], parts.at[mine], send.at[k + 3])))
            for j, (px, py) in enumerate(_other_chips(x, y)):
                q = 2 * px + py
                d.append(("out", _remote(chip.at[q], parts.at[mine], send, recv, k + j, (px, py, c))))
                d.append(("arrival", _remote(chip.at[q], parts.at[q], send, recv, k + j, (px, py, c))))
        return d
    shapes = [jax.ShapeDtypeStruct(s.shape, s.dtype) for s in chips]
    return _Ride(list(chips), shapes, {}, 4 * len(chips), plan)


def _gather_all(srcs):
    def plan(ins, outs, send, recv, base):
        x, y, c = _coords()
        me = 4 * x + 2 * y + c
        d = []
        for a, (src, dst) in enumerate(zip(ins, outs)):
            k = base + N_DEV * a
            d.append(("local", _local(src, dst.at[me], send.at[k])))
            for j in range(1, N_DEV):
                px = 1 - x if (j >> 2) & 1 else x
                py = 1 - y if (j >> 1) & 1 else y
                pc = 1 - c if j & 1 else c
                d.append(("out", _remote(src, dst.at[me], send, recv, k + j, (px, py, pc))))
                d.append(("arrival", _remote(src, dst.at[4 * px + 2 * py + pc], send, recv, k + j, (px, py, pc))))
        return d
    shapes = [jax.ShapeDtypeStruct((N_DEV,) + s.shape, s.dtype) for s in srcs]
    return _Ride(list(srcs), shapes, {}, N_DEV * len(srcs), plan)


def _merge(*rides):
    ins, outs, aliases, parts, nsem = [], [], {}, [], 0
    for r in rides:
        parts.append((len(ins), len(outs), nsem, r))
        aliases.update({len(ins) + i: len(outs) + j for i, j in r.aliases.items()})
        ins += r.ins
        outs += r.out_shapes
        nsem += r.nsem

    def plan(i, o, send, recv, base):
        d = []
        for i0, o0, s0, r in parts:
            d += r.plan(i[i0:i0 + len(r.ins)], o[o0:o0 + len(r.out_shapes)], send, recv, base + s0)
        return d
    return _Ride(ins, outs, aliases, nsem, plan)


def _comm(ride, name):
    n_in, n_out = len(ride.ins), len(ride.out_shapes)

    def body(*refs):
        plan = ride.plan(refs[:n_in], refs[n_in:n_in + n_out], refs[-2], refs[-1], 0)
        _start_all(plan)
        _wait_all(plan)

    return pl.pallas_call(
        body, name=name, in_specs=[_ANY] * n_in, out_specs=[_ANY] * n_out, out_shape=ride.out_shapes,
        scratch_shapes=[pltpu.SemaphoreType.DMA((ride.nsem,)), pltpu.SemaphoreType.DMA((ride.nsem,))],
        input_output_aliases=dict(ride.aliases),
        compiler_params=pltpu.CompilerParams(has_side_effects=True),
    )(*ride.ins)


def _gather_chips_relayed(big, small, name):
    srcs = list(big) + list(small)
    n, nsem = len(srcs), 5 * len(srcs)

    def body(*refs):
        ins, outs, send, recv = refs[:n], refs[n:2 * n], refs[-2], refs[-1]
        x, y, c = _coords()
        slot = lambda dev: 4 * dev[0] + 2 * dev[1] + dev[2]
        me, nbr_x, nbr_y, diag = (x, y, c), (1 - x, y, c), (x, 1 - y, c), (1 - x, 1 - y, c)
        own, sends = [], []
        for a, (src, dst) in enumerate(zip(ins, outs)):
            k = 5 * a
            own.append(_local(src, dst.at[slot(me)], send.at[k + 4])())
            sends.append(_remote(src, dst.at[slot(me)], send, recv, k, nbr_x)())
            sends.append(_remote(src, dst.at[slot(me)], send, recv, k + 1, nbr_y)())
            if a >= len(big):
                sends.append(_remote(src, dst.at[slot(me)], send, recv, k + 2, diag)())
        for s in own + sends:
            s.start()
        for a, (src, dst) in enumerate(zip(ins, outs)):
            k = 5 * a
            _remote(src, dst.at[slot(nbr_x)], send, recv, k, nbr_x)().wait_recv()
            if a < len(big):
                half = src.shape[0] // 2
                part = dst.at[slot(nbr_x)].at[pl.ds(0, half)]
                fwd = _remote(part, part, send, recv, k + 2, nbr_y)()
                fwd.start()
                sends.append(fwd)
            _remote(src, dst.at[slot(nbr_y)], send, recv, k + 1, nbr_y)().wait_recv()
            if a < len(big):
                part = dst.at[slot(nbr_y)].at[pl.ds(half, src.shape[0] - half)]
                fwd = _remote(part, part, send, recv, k + 3, nbr_x)()
                fwd.start()
                sends.append(fwd)
        for a, (src, dst) in enumerate(zip(ins, outs)):
            k = 5 * a
            if a < len(big):
                half = src.shape[0] // 2
                lo = dst.at[slot(diag)].at[pl.ds(0, half)]
                hi = dst.at[slot(diag)].at[pl.ds(half, src.shape[0] - half)]
                _remote(lo, lo, send, recv, k + 2, nbr_y)().wait_recv()
                _remote(hi, hi, send, recv, k + 3, nbr_x)().wait_recv()
            else:
                _remote(src, dst.at[slot(diag)], send, recv, k + 2, diag)().wait_recv()
        for lc in own:
            lc.wait()
        for s in sends:
            s.wait_send()

    return pl.pallas_call(
        body, name=name, in_specs=[_ANY] * n, out_specs=[_ANY] * n,
        out_shape=[jax.ShapeDtypeStruct((N_DEV,) + s.shape, s.dtype) for s in srcs],
        scratch_shapes=[pltpu.SemaphoreType.DMA((nsem,)), pltpu.SemaphoreType.DMA((nsem,))],
        compiler_params=pltpu.CompilerParams(has_side_effects=True),
    )(*srcs)


def _call(body, *, name, grid, in_specs, out_specs, out_shape, args, sem, scratch=(), ride=None, base=None):
    params = pltpu.CompilerParams(dimension_semantics=sem, vmem_limit_bytes=VMEM_LIMIT)
    own_aliases = {}
    if base is not None:
        inner, n_host = body, len(args)
        body = lambda *refs: inner(*refs[:n_host], *refs[n_host + 1:])
        own_aliases[n_host] = base[1]
        args, in_specs = tuple(args) + (base[0],), list(in_specs) + [_ANY]
    if ride is None:
        res = pl.pallas_call(body, name=name, grid=grid, in_specs=in_specs, out_specs=out_specs,
                             out_shape=out_shape, scratch_shapes=list(scratch), input_output_aliases=own_aliases,
                             compiler_params=params)(*args)
        return list(res), []
    n_in, n_out, n_scr = len(args), len(out_shape), len(scratch)
    r_in, r_out = len(ride.ins), len(ride.out_shapes)

    def hosted(*refs):
        h_in, rin = refs[:n_in], refs[n_in:n_in + r_in]
        o0 = n_in + r_in
        h_out, rout = refs[o0:o0 + n_out], refs[o0 + n_out:o0 + n_out + r_out]
        s0 = o0 + n_out + r_out
        h_scr, send, recv = refs[s0:s0 + n_scr], refs[s0 + n_scr], refs[s0 + n_scr + 1]
        ids = [pl.program_id(i) for i in range(len(grid))]
        first = functools.reduce(lambda p, q: p & q, [i == 0 for i in ids])
        last = functools.reduce(lambda p, q: p & q, [i == n - 1 for i, n in zip(ids, grid)])

        @pl.when(first)
        def _():
            _start_all(ride.plan(rin, rout, send, recv, 0))

        body(*h_in, *h_out, *h_scr)

        @pl.when(last)
        def _():
            _wait_all(ride.plan(rin, rout, send, recv, 0))

    res = pl.pallas_call(
        hosted, name=name, grid=grid, in_specs=list(in_specs) + [_ANY] * r_in,
        out_specs=list(out_specs) + [_ANY] * r_out, out_shape=list(out_shape) + list(ride.out_shapes),
        scratch_shapes=list(scratch) + [pltpu.SemaphoreType.DMA((ride.nsem,)), pltpu.SemaphoreType.DMA((ride.nsem,))],
        input_output_aliases={**own_aliases, **{n_in + i: n_out + j for i, j in ride.aliases.items()}},
        compiler_params=params,
    )(*args, *ride.ins)
    return list(res[:n_out]), list(res[n_out:])


def _matmul(a, b, *, ta=False, tb=False, out_dtype=BF16, add=None, post=None, name, ride=None, tn_max=MM_TILE_MN):
    m = a.shape[1] if ta else a.shape[0]
    k = a.shape[0] if ta else a.shape[1]
    n = b.shape[0] if tb else b.shape[1]
    assert k == (b.shape[1] if tb else b.shape[0])
    tm, tn, tk = _tile(m, MM_TILE_MN, LANES), _tile(n, tn_max, LANES), _tile(k, MM_TILE_K, LANES)
    nk = k // tk
    dims = (((0 if ta else 1,), (1 if tb else 0,)), ((), ()))
    single = post is None
    if add is not None:
        post = (lambda r, t: (r + t,), [add], [out_dtype])
    elif post is None:
        post = (lambda r: (r,), [], [out_dtype])
    post_fn, extras, out_dtypes = post
    n_ex, n_o = len(extras), len(out_dtypes)

    def body(*refs):
        a_ref, b_ref = refs[:2]
        ex_refs, o_refs = refs[2:2 + n_ex], refs[2 + n_ex:2 + n_ex + n_o]

        def finish(r):
            for o_ref, v in zip(o_refs, post_fn(r, *[e[...].astype(F32) for e in ex_refs])):
                o_ref[...] = v.astype(o_ref.dtype)

        part = lax.dot_general(a_ref[...].astype(BF16), b_ref[...].astype(BF16), dims, preferred_element_type=F32)
        if nk == 1:
            finish(part)
            return
        acc = refs[-1]
        kk = pl.program_id(2)

        @pl.when(kk == 0)
        def _():
            acc[...] = part

        @pl.when((kk > 0) & (kk < nk - 1))
        def _():
            acc[...] += part

        @pl.when(kk == nk - 1)
        def _():
            finish(acc[...] + part)

    a_spec = (pl.BlockSpec((tk, tm), lambda i, j, kk: (kk, i)) if ta
              else pl.BlockSpec((tm, tk), lambda i, j, kk: (i, kk)))
    b_spec = (pl.BlockSpec((tn, tk), lambda i, j, kk: (j, kk)) if tb
              else pl.BlockSpec((tk, tn), lambda i, j, kk: (kk, j)))
    o_spec = pl.BlockSpec((tm, tn), lambda i, j, kk: (i, j))
    outs, rides = _call(
        body, name=name, grid=(m // tm, n // tn, nk),
        in_specs=[a_spec, b_spec] + [o_spec] * n_ex, out_specs=[o_spec] * n_o,
        out_shape=[jax.ShapeDtypeStruct((m, n), dt) for dt in out_dtypes], args=(a, b, *extras),
        scratch=[pltpu.VMEM((tm, tn), F32)] if nk > 1 else [], sem=("parallel", "parallel", "arbitrary"), ride=ride)
    return (outs[0] if single else outs), rides


def _rows_call(fn, *, rows, tr, row_ins, full_ins, row_outs, acc_outs, name, ride=None):
    nr, nf, no, na = len(row_ins), len(full_ins), len(row_outs), len(acc_outs)

    def body(*refs):
        vals = [r[...] for r in refs[:nr + nf]]
        outs, accs = fn(*vals)
        for r, v in zip(refs[nr + nf:nr + nf + no], outs):
            r[...] = v.astype(r.dtype)
        if na:
            @pl.when(pl.program_id(0) == 0)
            def _():
                for r in refs[nr + nf + no:]:
                    r[...] = jnp.zeros_like(r)
            for r, v in zip(refs[nr + nf + no:], accs):
                r[...] += v

    in_specs = [pl.BlockSpec((tr, w), functools.partial(lambda cb, i: (i, cb), cb)) for _, w, cb in row_ins]
    in_specs += [pl.BlockSpec(f.shape, lambda i: (0, 0)) for f in full_ins]
    out_specs = [pl.BlockSpec((tr, o[0]), lambda i: (i, 0)) for o in row_outs]
    out_specs += [pl.BlockSpec(s, lambda i: (0, 0)) for s in acc_outs]
    out_shape = [jax.ShapeDtypeStruct((rows, o[-1] if len(o) == 3 else o[0]), o[1]) for o in row_outs]
    out_shape += [jax.ShapeDtypeStruct(s, F32) for s in acc_outs]
    return _call(body, name=name, grid=(rows // tr,), in_specs=in_specs, out_specs=out_specs, out_shape=out_shape,
                 args=tuple(a for a, _, _ in row_ins) + tuple(full_ins), sem=("arbitrary",), ride=ride)


def _cols_call(fn, *, rows, cols, cw, col_ins, par_ins, col_outs, par_outs, name, ride=None, into=None):
    nc, npar = len(col_ins), len(par_ins)

    def body(*refs):
        vals = [r[...] for r in refs[:nc + npar]]
        outs, pouts = fn(*vals)
        for r, v in zip(refs[nc + npar:], tuple(outs) + tuple(pouts)):
            r[...] = v.astype(r.dtype)

    in_specs = [pl.BlockSpec((rows, cw), functools.partial(lambda off, j: (0, off + j), off)) for _, off in col_ins]
    in_specs += [pl.BlockSpec((p.shape[0], cw), functools.partial(lambda off, j: (0, off + j), off))
                 for p, off in par_ins]
    out_specs = [pl.BlockSpec((rows, cw), lambda j: (0, j)) for _ in col_outs]
    out_specs += [pl.BlockSpec((k, cw), lambda j: (0, j)) for k in par_outs]
    out_shape = [jax.ShapeDtypeStruct((rows, cols), dt) for dt in col_outs]
    out_shape += [jax.ShapeDtypeStruct((k, cols), F32) for k in par_outs]
    if into is not None:
        out_specs[0] = pl.BlockSpec((rows, cw), lambda j: (0, into[1] + j))
        out_shape[0] = jax.ShapeDtypeStruct(into[0].shape, into[0].dtype)
    return _call(body, name=name, grid=(cols // cw,), in_specs=in_specs, out_specs=out_specs, out_shape=out_shape,
                 args=tuple(a for a, _ in col_ins) + tuple(p for p, _ in par_ins), sem=("arbitrary",), ride=ride,
                 base=None if into is None else (into[0], 0))


def _sigmoid(v):
    return 1.0 / (1.0 + jnp.exp(-v))


def _softplus(v):
    return jnp.maximum(v, 0.0) + jnp.log(1.0 + jnp.exp(-jnp.abs(v)))


def _rms(v, g):
    return v * lax.rsqrt(jnp.mean(v * v, axis=-1, keepdims=True) + EPS) * g


def _shift_down(v, s, row):
    return jnp.where(row >= s, pltpu.roll(v, s, 0), 0.0)


def _shift_up(v, s, row):
    n = v.shape[0]
    return jnp.where(row < n - s, pltpu.roll(v, n - s, 0), 0.0)


def _causal_conv(u, w, row):
    k_taps = w.shape[0]
    acc = u * w[k_taps - 1:k_taps, :]
    for k in range(k_taps - 1):
        acc = acc + _shift_down(u, k_taps - 1 - k, row) * w[k:k + 1, :]
    return acc


def _causal_conv_bwd(u, dy, w, row):
    k_taps = w.shape[0]
    tap = lax.broadcasted_iota(jnp.int32, w.shape, 0)
    du = dy * w[k_taps - 1:k_taps, :]
    dw = jnp.where(tap == k_taps - 1, jnp.sum(dy * u, axis=0, keepdims=True), 0.0)
    for k in range(k_taps - 1):
        s = k_taps - 1 - k
        du = du + _shift_up(dy, s, row) * w[k:k + 1, :]
        dw = dw + jnp.where(tap == k, jnp.sum(dy * _shift_down(u, s, row), axis=0, keepdims=True), 0.0)
    return du, dw


def _conv_silu_fwd(u, w, b):
    u = u.astype(F32)
    row = lax.broadcasted_iota(jnp.int32, u.shape, 0)
    pre = _causal_conv(u, w, row) + b
    return (pre * _sigmoid(pre),), ()


def _conv_silu_bwd(u, dy, w, b):
    u = u.astype(F32)
    dy = dy.astype(F32)
    row = lax.broadcasted_iota(jnp.int32, u.shape, 0)
    pre = _causal_conv(u, w, row) + b
    s = _sigmoid(pre)
    dpre = dy * (s * (1.0 + pre * (1.0 - s)))
    du, dw = _causal_conv_bwd(u, dpre, w, row)
    return (du,), (dw, jnp.sum(dpre, axis=0, keepdims=True))


def _shortconv_fwd(gb, gc, u, w):
    gb, gc, u = gb.astype(F32), gc.astype(F32), u.astype(F32)
    row = lax.broadcasted_iota(jnp.int32, u.shape, 0)
    return (gb * _causal_conv(gc * u, w, row),), ()


def _shortconv_bwd(gb, gc, u, dy, w):
    gb, gc, u, dy = gb.astype(F32), gc.astype(F32), u.astype(F32), dy.astype(F32)
    row = lax.broadcasted_iota(jnp.int32, u.shape, 0)
    v = gc * u
    dgb = dy * _causal_conv(v, w, row)
    dv, dw = _causal_conv_bwd(v, dy * gb, w, row)
    return (dgb, dv * u, dv * gc), (dw,)


def _split3(v):
    hi = v.astype(BF16)
    r1 = v - hi.astype(F32)
    mid = r1.astype(BF16)
    lo = (r1 - mid.astype(F32)).astype(BF16)
    return hi, mid, lo


def _exact_dot(v, m01, dims, v_is_lhs):
    def one(p):
        return (lax.dot_general(p, m01, dims, preferred_element_type=F32) if v_is_lhs
                else lax.dot_general(m01, p, dims, preferred_element_type=F32))
    hi, mid, lo = _split3(v)
    return (one(lo) + one(mid)) + one(hi)


_NN = (((1,), (0,)), ((), ()))
_NT = (((1,), (1,)), ((), ()))
_TN = (((0,), (0,)), ((), ()))


@jax.custom_vjp
def _cumsum_rows(tril, v):
    return _exact_dot(v, tril, _NN, False)


def _cumsum_rows_fwd(tril, v):
    return _cumsum_rows(tril, v), tril


def _cumsum_rows_bwd(tril, ct):
    return None, _exact_dot(ct, tril, _TN, False)


_cumsum_rows.defvjp(_cumsum_rows_fwd, _cumsum_rows_bwd)


@jax.custom_vjp
def _cumsum_lanes(tril, v):
    return _exact_dot(v, tril, _NT, True)


def _cumsum_lanes_fwd(tril, v):
    return _cumsum_lanes(tril, v), tril


def _cumsum_lanes_bwd(tril, ct):
    return None, _exact_dot(ct, tril, _NN, True)


_cumsum_lanes.defvjp(_cumsum_lanes_fwd, _cumsum_lanes_bwd)


@jax.custom_vjp
def _expand(e01, v):
    return _exact_dot(v, e01, _NN, True)


def _expand_fwd(e01, v):
    return _expand(e01, v), e01


def _expand_bwd(e01, ct):
    return None, _exact_dot(ct, e01, _NT, True)


_expand.defvjp(_expand_fwd, _expand_bwd)


def _causal_mask(n):
    li = lax.broadcasted_iota(jnp.int32, (n, n), 0)
    si = lax.broadcasted_iota(jnp.int32, (n, n), 1)
    return si <= li


def _dt_prep(dtc, dtr, bias_r, bias_c, alog_r, alog_c):
    dt_c = _softplus(dtc + bias_r)
    dt_r = _softplus(dtr + bias_c)
    tril = jnp.where(_causal_mask(dtc.shape[0]), 1.0, 0.0).astype(BF16)
    cs_c = _cumsum_rows(tril, dt_c * (-jnp.exp(alog_r)))
    cs_r = _cumsum_lanes(tril, dt_r * (-jnp.exp(alog_c)))
    return dt_c, cs_c, cs_r


def _ssd_chunk(r_heads, xs, bg, cg, dt_c, cs_c, cs_rg, e01, dskip_e, hp):
    l_len, rp = xs.shape
    p = rp // r_heads
    causal = _causal_mask(l_len)
    lane_head = lax.broadcasted_iota(jnp.int32, (1, rp), 1) // p
    dt_e = _expand(e01, dt_c)
    cs_e = _expand(e01, cs_c)
    cl_e = cs_e[l_len - 1:l_len, :]
    x = xs * dt_e
    bgb, cgb = bg.astype(BF16), cg.astype(BF16)
    cb = lax.dot_general(cgb, bgb, _NT, preferred_element_type=F32)
    ms, xm = [], []
    for r in range(r_heads):
        seg = cs_e[:, r * p:r * p + 1] - cs_rg[r:r + 1, :]
        decay = jnp.exp(jnp.where(causal, seg, -1e30))
        ms.append((cb * decay).astype(BF16))
        xm.append(jnp.where(lane_head == r, x, 0.0).astype(BF16))
    y_diag = lax.dot_general(jnp.concatenate(ms, axis=1), jnp.concatenate(xm, axis=0), _NN,
                             preferred_element_type=F32)
    y_off = lax.dot_general(cgb, hp.astype(BF16), _NN, preferred_element_type=F32) * jnp.exp(cs_e)
    xd = (x * jnp.exp(cl_e - cs_e)).astype(BF16)
    states = lax.dot_general(bgb, xd, _TN, preferred_element_type=F32)
    h_next = hp * jnp.exp(cl_e) + states
    y = y_diag + y_off + dskip_e * xs
    return y, h_next


def _ssd_dt(dtc, dtr, small, cots=None):
    t_len, heads = dtc.shape[0], dtr.shape[0]
    nc = t_len // CHUNK
    col = pl.BlockSpec((CHUNK, LANES), lambda c: (c, 0))
    row = pl.BlockSpec((heads, CHUNK), lambda c: (0, c))
    full = [pl.BlockSpec(s.shape, lambda c: (0, 0)) for s in small]
    shapes = [jax.ShapeDtypeStruct((t_len, LANES), F32), jax.ShapeDtypeStruct((t_len, LANES), F32),
              jax.ShapeDtypeStruct((heads, t_len), F32)]
    if cots is None:
        def body(dtc_ref, dtr_ref, br, bc, ar, ac, dt_ref, csc_ref, csr_ref):
            dt_ref[...], csc_ref[...], csr_ref[...] = _dt_prep(dtc_ref[...], dtr_ref[...], br[...], bc[...],
                                                                ar[...], ac[...])
        return _call(body, name="ssd_dt", grid=(nc,), in_specs=[col, row] + full, out_specs=[col, col, row],
                     out_shape=shapes, args=(dtc, dtr, *small), sem=("parallel",))[0]

    g_dt, g_csc, g_csr, ddk, e01 = cots

    def body(dtc_ref, dtr_ref, br, bc, ar, ac, g_dt_ref, g_csc_ref, g_csr_ref, ddk_ref, e_ref,
             ddtc_ref, ddtr_ref, *dsmall):
        _, vjp = jax.vjp(_dt_prep, dtc_ref[...], dtr_ref[...], br[...], bc[...], ar[...], ac[...])
        grads = vjp((g_dt_ref[...], g_csc_ref[...], g_csr_ref[...]))
        ddtc_ref[...], ddtr_ref[...] = grads[0], grads[1]
        ddk8 = jnp.broadcast_to(ddk_ref[...], (8, ddk_ref.shape[1]))
        dskip = _exact_dot(ddk8, e_ref[...], _NT, True)[0:1, :]

        @pl.when(pl.program_id(0) == 0)
        def _():
            for r in dsmall:
                r[...] = jnp.zeros_like(r)

        for r, gr in zip(dsmall, tuple(grads[2:]) + (dskip,)):
            r[...] += gr

    acc = list(small) + [small[0]]
    return _call(body, name="d_ssd_dt", grid=(nc,),
                 in_specs=[col, row] + full + [col, col, row, pl.BlockSpec((None, 1, e01.shape[1]), lambda c: (c, 0, 0)),
                                               pl.BlockSpec(e01.shape, lambda c: (0, 0))],
                 out_specs=[col, row] + [pl.BlockSpec(s.shape, lambda c: (0, 0)) for s in acc],
                 out_shape=[shapes[0], shapes[2]] + [jax.ShapeDtypeStruct(s.shape, F32) for s in acc],
                 args=(dtc, dtr, *small, g_dt, g_csc, g_csr, ddk, e01), sem=("arbitrary",))[0]


def _ssd_specs(t_len, d_ssm, r_heads, reverse):
    rp = r_heads * HEADDIM
    nc = t_len // CHUNK
    per = next(p for p in (SSD_CHUNKS_PER_STEP, 2, 1) if nc % p == 0)
    ns, rows = nc // per, per * CHUNK
    cidx = (lambda c: ns - 1 - c) if reverse else (lambda c: c)
    b_off = d_ssm // N_STATE
    specs = dict(
        xs=pl.BlockSpec((rows, rp), lambda c, g: (cidx(c), g)),
        b=pl.BlockSpec((rows, N_STATE), lambda c, g: (cidx(c), b_off + g)),
        c=pl.BlockSpec((rows, N_STATE), lambda c, g: (cidx(c), b_off + N_GROUPS + g)),
        grad_bc=pl.BlockSpec((rows, N_STATE), lambda c, g: (cidx(c), g)),
        col=pl.BlockSpec((rows, LANES), lambda c, g: (cidx(c), 0)),
        csr=pl.BlockSpec((None, r_heads, rows), lambda c, g: (g, 0, cidx(c))),
        e01=pl.BlockSpec((LANES, rp), lambda c, g: (0, g)),
        dskip=pl.BlockSpec((1, rp), lambda c, g: (0, g)),
        hprev=pl.BlockSpec((per, None, N_STATE, rp), lambda c, g: (cidx(c), g, 0, 0)),
        ddk=pl.BlockSpec((per, 1, rp), lambda c, g: (cidx(c), 0, g)),
    )
    return specs, nc, ns, per, rp


def _ssd_fwd(xbc, dt_c, cs_c, cs_r3, e01, dskip_e, *, d_ssm, r_heads, ride=None):
    t_len = xbc.shape[0]
    sp, nc, ns, per, rp = _ssd_specs(t_len, d_ssm, r_heads, False)

    def body(xs_ref, b_ref, c_ref, dt_ref, csc_ref, csr_ref, e_ref, dk_ref, y_ref, hprev_ref, h_ref):
        c, g = pl.program_id(0), pl.program_id(1)

        @pl.when(c == 0)
        def _():
            h_ref[g] = jnp.zeros((N_STATE, rp), F32)

        hp = h_ref[g]
        for s in range(per):
            r = pl.ds(s * CHUNK, CHUNK)
            hprev_ref[s] = hp
            y, hp = _ssd_chunk(r_heads, xs_ref[r, :].astype(F32), b_ref[r, :].astype(F32), c_ref[r, :].astype(F32),
                               dt_ref[r, :], csc_ref[r, :], csr_ref[:, r], e_ref[...], dk_ref[...], hp)
            y_ref[r, :] = y
        h_ref[g] = hp

    return _call(
        body, name="ssd_fwd", grid=(ns, N_GROUPS),
        in_specs=[sp["xs"], sp["b"], sp["c"], sp["col"], sp["col"], sp["csr"], sp["e01"], sp["dskip"]],
        out_specs=[sp["xs"], sp["hprev"]],
        out_shape=[jax.ShapeDtypeStruct((t_len, d_ssm), F32),
                   jax.ShapeDtypeStruct((nc, N_GROUPS, N_STATE, rp), F32)],
        args=(xbc, xbc, xbc, dt_c, cs_c, cs_r3, e01, dskip_e), scratch=[pltpu.VMEM((N_GROUPS, N_STATE, rp), F32)],
        sem=("arbitrary", "arbitrary"), ride=ride)


def _ssd_bwd(xbc, dt_c, cs_c, cs_r3, e01, dskip_e, hprev, dy, *, d_ssm, r_heads, ride=None):
    t_len = xbc.shape[0]
    sp, nc, ns, per, rp = _ssd_specs(t_len, d_ssm, r_heads, True)

    def body(xs_ref, b_ref, c_ref, dt_ref, csc_ref, csr_ref, e_ref, dk_ref, hprev_ref, dy_ref,
             dxs_ref, db_ref, dc_ref, ddt_ref, dcsc_ref, dcsr_ref, ddk_ref, dh_ref):
        c, g = pl.program_id(0), pl.program_id(1)

        @pl.when(c == 0)
        def _():
            dh_ref[g] = jnp.zeros((N_STATE, rp), F32)

        @pl.when(g == 0)
        def _():
            ddt_ref[...] = jnp.zeros_like(ddt_ref)
            dcsc_ref[...] = jnp.zeros_like(dcsc_ref)

        e01 = e_ref[...]
        fn = lambda xs, bg, cg, dt, csc, csr, dk, hp: _ssd_chunk(r_heads, xs, bg, cg, dt, csc, csr, e01, dk, hp)
        dh = dh_ref[g]
        for s in reversed(range(per)):
            r = pl.ds(s * CHUNK, CHUNK)
            _, vjp = jax.vjp(fn, xs_ref[r, :].astype(F32), b_ref[r, :].astype(F32), c_ref[r, :].astype(F32),
                             dt_ref[r, :], csc_ref[r, :], csr_ref[:, r], dk_ref[...], hprev_ref[s])
            dxs, dbg, dcg, ddt, dcsc, dcsr, ddk, dh = vjp((dy_ref[r, :], dh))
            dxs_ref[r, :] = dxs.astype(dxs_ref.dtype)
            db_ref[r, :] = dbg.astype(db_ref.dtype)
            dc_ref[r, :] = dcg.astype(dc_ref.dtype)
            ddt_ref[r, :] += ddt
            dcsc_ref[r, :] += dcsc
            dcsr_ref[:, r] = dcsr
            ddk_ref[s] = ddk
        dh_ref[g] = dh

    n_bc = N_GROUPS * N_STATE
    return _call(
        body, name="ssd_bwd", grid=(ns, N_GROUPS),
        in_specs=[sp["xs"], sp["b"], sp["c"], sp["col"], sp["col"], sp["csr"], sp["e01"], sp["dskip"], sp["hprev"],
                  sp["xs"]],
        out_specs=[sp["xs"], sp["grad_bc"], sp["grad_bc"], sp["col"], sp["col"], sp["csr"], sp["ddk"]],
        out_shape=[jax.ShapeDtypeStruct((t_len, d_ssm), BF16), jax.ShapeDtypeStruct((t_len, n_bc), BF16),
                   jax.ShapeDtypeStruct((t_len, n_bc), BF16), jax.ShapeDtypeStruct(dt_c.shape, F32),
                   jax.ShapeDtypeStruct(cs_c.shape, F32), jax.ShapeDtypeStruct(cs_r3.shape, F32),
                   jax.ShapeDtypeStruct((nc, 1, d_ssm), F32)],
        args=(xbc, xbc, xbc, dt_c, cs_c, cs_r3, e01, dskip_e, hprev, dy),
        scratch=[pltpu.VMEM((N_GROUPS, N_STATE, rp), F32)], sem=("arbitrary", "arbitrary"), ride=ride)


def _chip_sum(src, sib, *, name):
    rows, cols = src.shape[1:]
    tr = _tile(rows, 256, BF16_ROWS)
    core = lax.axis_index("c").astype(jnp.int32).reshape(1)

    def body(c_ref, a_ref, b_ref, o_ref):
        o_ref[...] = (a_ref[...].astype(F32) + b_ref[...].astype(F32)).astype(o_ref.dtype)

    grid_spec = pltpu.PrefetchScalarGridSpec(
        num_scalar_prefetch=1, grid=(N_CHIPS, rows // tr),
        in_specs=[pl.BlockSpec((None, tr, cols), lambda q, i, c_ref: (2 * q + c_ref[0], i, 0)),
                  pl.BlockSpec((None, tr, cols), lambda q, i, c_ref: (q, i, 0))],
        out_specs=pl.BlockSpec((None, tr, cols), lambda q, i, c_ref: (q, i, 0)))
    return pl.pallas_call(
        body, name=name, grid_spec=grid_spec, out_shape=jax.ShapeDtypeStruct(sib.shape, sib.dtype),
        compiler_params=pltpu.CompilerParams(dimension_semantics=("parallel", "parallel"), vmem_limit_bytes=VMEM_LIMIT),
    )(core, src, sib)


def _adamw(w, g, m, v):
    m = ADAM_B1 * m + (1.0 - ADAM_B1) * g
    v = ADAM_B2 * v + (1.0 - ADAM_B2) * (g * g)
    m_hat = m / (1.0 - ADAM_B1 ** ADAM_STEP)
    v_hat = v / (1.0 - ADAM_B2 ** ADAM_STEP)
    delta = -ADAM_LR * (m_hat / (jnp.sqrt(v_hat) + ADAM_EPS) + ADAM_WD * w)
    return delta, m, v


def _reduce_adamw(parts, w, m, v, *, name):
    n_parts = parts.shape[0]
    rows, cols = w.shape
    tr = _tile(rows, 128, BF16_ROWS)

    def body(p_ref, w_ref, m_ref, v_ref, g_ref, d_ref, mo_ref, vo_ref):
        g = p_ref[0].astype(F32)
        for k in range(1, n_parts):
            g = g + p_ref[k].astype(F32)
        delta, mn, vn = _adamw(w_ref[...], g, m_ref[...], v_ref[...])
        g_ref[...] = g
        d_ref[...] = delta
        mo_ref[...] = mn
        vo_ref[...] = vn

    spec = pl.BlockSpec((tr, cols), lambda i: (i, 0))
    outs, _ = _call(
        body, name=name, grid=(rows // tr,),
        in_specs=[pl.BlockSpec((n_parts, tr, cols), lambda i: (0, i, 0)), spec, spec, spec],
        out_specs=[spec] * 4, out_shape=[jax.ShapeDtypeStruct((rows, cols), F32)] * 4,
        args=(parts, w, m, v), sem=("parallel",))
    return outs


def _move_rows(src, src_row, name, extra=None, extra_row=None):
    rb, n_out, cols = ROW_BLOCK, len(src_row), src.shape[1]
    assert n_out % rb == 0 and src.shape[0] % rb == 0 and src.shape[0] // rb >= 3
    n_blocks, max_b0, seg_cap = n_out // rb, src.shape[0] // rb - 3, 4

    def segments(rows_of, lo):
        segs, r = [], 0
        while r < rb:
            if rows_of[r] < 0:
                r += 1
                continue
            e = r
            while e + 1 < rb and rows_of[e + 1] == rows_of[e] + 1:
                e += 1
            segs.append((r, e + 1, rows_of[r] - r - lo))
            r = e + 1
        assert len(segs) <= seg_cap
        return segs + [(0, 0, 0)] * (seg_cap - len(segs))

    table = []
    for j in range(n_blocks):
        rows_j = list(src_row[j * rb:(j + 1) * rb])
        valid = [v for v in rows_j if v >= 0]
        b0 = min(max((min(valid) // rb) if valid else 0, 0), max_b0)
        assert not valid or max(valid) < (b0 + 3) * rb
        row = [b0] + [v for seg in segments(rows_j, b0 * rb) for v in seg]
        if extra is not None:
            row += [v for seg in segments(list(extra_row[j * rb:(j + 1) * rb]), 0) for v in seg]
        table.append(row)
    table = jnp.asarray(table, jnp.int32)

    def select(tbl_ref, j, first, width):
        r = lax.broadcasted_iota(jnp.int32, (rb, width), 0)
        c = lax.broadcasted_iota(jnp.int32, (rb, width), 1)
        hit = jnp.zeros((rb, width), jnp.bool_)
        for s in range(seg_cap):
            lo, hi, off = (tbl_ref[j, first + 3 * s + i] for i in range(3))
            hit = hit | ((r >= lo) & (r < hi) & (c == r + off))
        return jnp.where(hit, 1.0, 0.0).astype(BF16)

    def body(tbl_ref, *refs):
        o_ref = refs[-1]
        j = pl.program_id(0)
        sel = select(tbl_ref, j, 1, 3 * rb)
        acc = jnp.zeros((rb, cols), F32)
        for b in range(3):
            acc = acc + lax.dot_general(sel[:, b * rb:(b + 1) * rb], refs[b][...], _NN, preferred_element_type=F32)
        if extra is not None:
            acc = acc + lax.dot_general(select(tbl_ref, j, 1 + 3 * seg_cap, extra.shape[0]), refs[3][...], _NN,
                                        preferred_element_type=F32)
        o_ref[...] = acc.astype(o_ref.dtype)

    in_specs = [pl.BlockSpec((rb, cols), functools.partial(lambda b, j, tbl: (tbl[j, 0] + b, 0), b)) for b in range(3)]
    args = [src, src, src]
    if extra is not None:
        in_specs.append(pl.BlockSpec(extra.shape, lambda j, tbl: (0, 0)))
        args.append(extra)
    grid_spec = pltpu.PrefetchScalarGridSpec(num_scalar_prefetch=1, grid=(n_blocks,), in_specs=in_specs,
                                             out_specs=pl.BlockSpec((rb, cols), lambda j, tbl: (j, 0)))
    return pl.pallas_call(
        body, name=name, grid_spec=grid_spec, out_shape=jax.ShapeDtypeStruct((n_out, cols), src.dtype),
        compiler_params=pltpu.CompilerParams(dimension_semantics=("parallel",), vmem_limit_bytes=VMEM_LIMIT),
    )(table, *args)


def _cols_of(g):
    return jnp.transpose(g, (1, 0, 2)).reshape(g.shape[1], -1)


def _pad_to(a, rows, cols):
    return jnp.pad(a, ((0, rows - a.shape[0]), (0, cols - a.shape[1])))


def kernel(x, norm_mix_g, w_in, ssm_conv_w, ssm_conv_b, ssm_dt_bias, ssm_A_log, ssm_D, ssm_norm_g, sc_conv_w, w_out, norm_ffn_g, w_gate, w_up, w_down, norm_final_g, loss_target, m_norm_mix_g, m_w_in, m_ssm_conv_w, m_ssm_conv_b, m_ssm_dt_bias, m_ssm_A_log, m_ssm_D, m_ssm_norm_g, m_sc_conv_w, m_w_out, m_norm_ffn_g, m_w_gate, m_w_up, m_w_down, m_norm_final_g, v_norm_mix_g, v_w_in, v_ssm_conv_w, v_ssm_conv_b, v_ssm_dt_bias, v_ssm_A_log, v_ssm_D, v_ssm_norm_g, v_sc_conv_w, v_w_out, v_norm_ffn_g, v_w_gate, v_w_up, v_w_down, v_norm_final_g):
    t_len, d = x.shape[1], x.shape[2]
    heads = d // HEADDIM
    r_heads = heads // N_GROUPS
    d_xbc = d + 2 * N_GROUPS * N_STATE
    ff_s = w_down.shape[1]
    ff = ff_s * N_DEV
    off_xbc, off_dt = d, d + d_xbc
    off_cb = off_dt + heads
    d_in = off_cb + 3 * d
    in_s = d_in // N_DEV
    in_p = -(-in_s // (2 * BF16_ROWS)) * (2 * BF16_ROWS)
    w_main = 4 * d + d_xbc
    me = 4 * lax.axis_index("x") + 2 * lax.axis_index("y") + lax.axis_index("c")

    x2 = x[0]
    target = loss_target[0]

    tpose = lambda a: jnp.transpose(a[0])
    win_s = _pad_to(tpose(w_in).astype(BF16), in_p, d)
    wg_s, wu_s = tpose(w_gate).astype(BF16), tpose(w_up).astype(BF16)
    wo_s, wd_s = w_out[0].astype(BF16), w_down[0].astype(BF16)
    small_w = jnp.concatenate([_pad_to(ssm_conv_w[0], K_SSM, d_xbc // N_DEV),
                               _pad_to(sc_conv_w[0], K_SC + 1, d_xbc // N_DEV)], axis=0)

    g1, g2, g3 = norm_mix_g, norm_ffn_g, norm_final_g.reshape(1, d)
    gs = ssm_norm_g
    small = [_pad_to(ssm_dt_bias, 1, LANES), ssm_dt_bias.reshape(heads, 1), _pad_to(ssm_A_log, 1, LANES),
             ssm_A_log.reshape(heads, 1)]
    e01 = (lax.broadcasted_iota(jnp.int32, (LANES, d), 1) // HEADDIM
           == lax.broadcasted_iota(jnp.int32, (LANES, d), 0)).astype(BF16)
    dskip_e = jnp.repeat(ssm_D, HEADDIM, axis=1)
    tr = _tile(t_len, 256, 8)
    tr_ff = _tile(t_len, 128, 8)
    cw = LANES
    slab = lambda col: col // cw

    gin_1, gsm_1 = _gather_chips_relayed([win_s], [small_w], "gather_w_in_chips")
    (n1,), (gin, gsm) = _rows_call(lambda v, g: ((_rms(v, g),), ()), rows=t_len, tr=tr, row_ins=[(x2, d, 0)],
                                   full_ins=[g1], row_outs=[(d, BF16)], acc_outs=[], name="norm_mix",
                                   ride=_gather_sibling([gin_1, gsm_1]))
    in_pieces = []
    for k in range(N_DEV):
        for a, b, dst, shift in ((0, off_dt, 0, 0), (off_dt, off_cb, 1, -off_dt), (off_cb, d_in, 0, -heads)):
            s, e = max(k * in_s, a), min((k + 1) * in_s, b)
            if s < e:
                in_pieces.append((k, s - k * in_s, e - s, dst, s + shift))
    ref_row = lambda t: t if t < off_dt else t + heads
    wtm = _move_rows(gin.reshape(N_DEV * in_p, d),
                     [(ref_row(t) // in_s) * in_p + ref_row(t) % in_s for t in range(w_main)], "place_w_in")
    wtdt = jnp.zeros((LANES, d), BF16)
    for k, r0, n, dst, d0 in in_pieces:
        if dst == 1:
            wtdt = lax.dynamic_update_slice(wtdt, gin[k, r0:r0 + n], (d0, 0))
    cw_ssm = _cols_of(gsm[:, :K_SSM, :])
    cw_sc = _cols_of(gsm[:, K_SSM:K_SSM + K_SC, :d // N_DEV])

    proj, (go_1, gg_1) = _matmul(n1, wtm, tb=True, out_dtype=BF16, name="proj_main",
                                 ride=_gather_chips([wo_s, wg_s]))
    dt_raw, _ = _matmul(n1, wtdt, tb=True, out_dtype=F32, name="proj_dt")
    dt_raw_t = jnp.transpose(dt_raw[:, :heads])
    (xbc,), (go, gg) = _cols_call(_conv_silu_fwd, rows=t_len, cols=d_xbc, cw=cw, col_ins=[(proj, slab(off_xbc))],
                                  par_ins=[(cw_ssm, 0), (ssm_conv_b, 0)], col_outs=[BF16], par_outs=[],
                                  name="ssm_conv", ride=_gather_sibling([go_1, gg_1]))
    dt_c, cs_c, cs_r = _ssd_dt(dt_raw, dt_raw_t, small)
    cs_r3 = cs_r.reshape(N_GROUPS, r_heads, t_len)
    (y_ssd, hprev), (gu_1,) = _ssd_fwd(xbc, dt_c, cs_c, cs_r3, e01, dskip_e, d_ssm=d, r_heads=r_heads,
                                       ride=_gather_chips([wu_s]))

    def gate_norm(y, z, g):
        z = z.astype(F32)
        return _rms(y * (z * _sigmoid(z)), g)

    (y_mix,), (gu,) = _rows_call(lambda y, z, g: ((gate_norm(y, z, g),), ()), rows=t_len, tr=tr,
                                 row_ins=[(y_ssd, d, 0), (proj, d, 0)], full_ins=[gs], row_outs=[(d, BF16, 2 * d)],
                                 acc_outs=[], name="ssm_gate_norm", ride=_gather_sibling([gu_1]))
    wgt, wut, wo = gg.reshape(ff, d), gu.reshape(ff, d), go.reshape(2 * d, d)
    sc0 = slab(d + d_xbc)
    (y_mix,), _ = _cols_call(_shortconv_fwd, rows=t_len, cols=d, cw=cw,
                             col_ins=[(proj, sc0), (proj, sc0 + slab(d)), (proj, sc0 + 2 * slab(d))],
                             par_ins=[(cw_sc, 0)], col_outs=[BF16], par_outs=[], name="shortconv",
                             into=(y_mix, slab(d)))
    h1, _ = _matmul(y_mix, wo, out_dtype=F32, add=x2, name="out_proj")
    (n2,), _ = _rows_call(lambda v, g: ((_rms(v, g),), ()), rows=t_len, tr=tr, row_ins=[(h1, d, 0)], full_ins=[g2],
                          row_outs=[(d, BF16)], acc_outs=[], name="norm_ffn")
    g_ff, (gd_1,) = _matmul(n2, wgt, tb=True, out_dtype=BF16, name="ffn_gate",
                            ride=_gather_chips([wd_s]))
    (u_ff, a_ff), (gd,) = _matmul(n2, wut, tb=True, name="ffn_up", ride=_gather_sibling([gd_1]),
                                  post=(lambda uv, gv: (uv, gv * _sigmoid(gv) * uv), [g_ff], [BF16, BF16]),
                                  tn_max=MM_TILE_N_POST)
    wd = gd.reshape(ff, d)
    h2, _ = _matmul(a_ff, wd, out_dtype=F32, add=h1, name="ffn_down")

    def head(hv, tv, g):
        def f(hh, gg_):
            e = _rms(hh, gg_) - tv
            return (0.5 / d) * jnp.sum(e * e)
        val, (dh, dg) = jax.value_and_grad(f, argnums=(0, 1))(hv, g)
        return (dh, dh), (jnp.full((1, LANES), val, F32), dg)

    (dh2, dh2_b, loss_acc, dg3), _ = _rows_call(head, rows=t_len, tr=tr, row_ins=[(h2, d, 0), (target, d, 0)],
                                                full_ins=[g3], row_outs=[(d, F32), (d, BF16)],
                                                acc_outs=[(1, LANES), (1, d)], name="loss_head")
    loss = lax.psum(loss_acc[0, 0], ("x", "y", "c"))

    def act_bwd(dav, gv, uv):
        s = _sigmoid(gv)
        return dav * uv * (s * (1.0 + gv * (1.0 - s))), dav * gv * s

    (dg_ff, du_ff), _ = _matmul(dh2_b, wd, tb=True, name="d_ffn_gate_up",
                                post=(act_bwd, [g_ff, u_ff], [BF16, BF16]), tn_max=MM_TILE_N_POST)
    dwd, _ = _matmul(a_ff, dh2_b, ta=True, out_dtype=BF16, name="d_w_down")
    dwd8 = dwd.reshape(N_DEV, ff_s, d)
    dn2, (sib_d,) = _matmul(dg_ff, wgt, out_dtype=F32, name="d_norm_ffn_out_gate", ride=_scatter_sibling([dwd8]))
    chip_d = _chip_sum(dwd8, sib_d, name="chip_sum_w_down")
    dn2, (parts_d,) = _matmul(du_ff, wut, out_dtype=F32, add=dn2, name="d_norm_ffn_out_up",
                              ride=_scatter_chips([chip_d]))
    dwg, _ = _matmul(dg_ff, n2, ta=True, out_dtype=BF16, name="d_w_gate")
    dwu, _ = _matmul(du_ff, n2, ta=True, out_dtype=BF16, name="d_w_up")
    dwg8, dwu8 = dwg.reshape(N_DEV, ff_s, d), dwu.reshape(N_DEV, ff_s, d)

    def norm_bwd(v, dn, dres, g):
        _, vjp = jax.vjp(_rms, v, g)
        dv, dg = vjp(dn)
        return (dv + dres,), (dg,)

    def norm_bwd_2(v, dn, dres, g):
        (dv,), acc = norm_bwd(v, dn, dres, g)
        return (dv, dv), acc

    (dh1, dh1_b, dg2), (sib_g, sib_u) = _rows_call(norm_bwd_2, rows=t_len, tr=tr,
                                                   row_ins=[(h1, d, 0), (dn2, d, 0), (dh2, d, 0)], full_ins=[g2],
                                                   row_outs=[(d, F32), (d, BF16)], acc_outs=[(1, d)], name="d_norm_ffn",
                                                   ride=_scatter_sibling([dwg8, dwu8]))
    chip_g = _chip_sum(dwg8, sib_g, name="chip_sum_w_gate")
    chip_u = _chip_sum(dwu8, sib_u, name="chip_sum_w_up")

    dy_mix, _ = _matmul(dh1_b, wo, tb=True, out_dtype=BF16, name="d_y_mix")
    dwo, _ = _matmul(y_mix, dh1_b, ta=True, out_dtype=BF16, name="d_w_out")
    dwo8 = dwo.reshape(N_DEV, 2 * d // N_DEV, d)
    (dgb, dgc, du, dcw_sc), (sib_o,) = _cols_call(
        _shortconv_bwd, rows=t_len, cols=d, cw=cw,
        col_ins=[(proj, sc0), (proj, sc0 + slab(d)), (proj, sc0 + 2 * slab(d)), (dy_mix, slab(d))],
        par_ins=[(cw_sc, 0)], col_outs=[BF16] * 3, par_outs=[K_SC], name="d_shortconv",
        ride=_scatter_sibling([dwo8]))
    chip_o = _chip_sum(dwo8, sib_o, name="chip_sum_w_out")

    def gate_norm_bwd(y, z, dyo, g):
        _, vjp = jax.vjp(gate_norm, y, z.astype(F32), g)
        dy, dz, dg = vjp(dyo.astype(F32))
        return (dy, dz), (dg,)

    (dy_ssd, dproj, dgs), _ = _rows_call(gate_norm_bwd, rows=t_len, tr=tr,
                                         row_ins=[(y_ssd, d, 0), (proj, d, 0), (dy_mix, d, 0)], full_ins=[gs],
                                         row_outs=[(d, F32), (d, BF16, w_main)], acc_outs=[(1, d)],
                                         name="d_ssm_gate_norm")
    (dxs, dbm, dcm, g_dt, g_csc, g_csr3, ddk), (parts_g, parts_u, parts_o) = _ssd_bwd(
        xbc, dt_c, cs_c, cs_r3, e01, dskip_e, hprev, dy_ssd, d_ssm=d, r_heads=r_heads,
        ride=_scatter_chips([chip_g, chip_u, chip_o]))
    ddt_c, ddt_r, dbias_r, dbias_c, dalog_r, dalog_c, ddskip = _ssd_dt(
        dt_raw, dt_raw_t, small, cots=(g_dt, g_csc, g_csr3.reshape(heads, t_len), ddk, e01))
    dcw_parts, dcb_parts, col0 = [], [], 0
    for tag, dpart in (("x", dxs), ("b", dbm), ("c", dcm)):
        (dproj, dcw_p, dcb_p), _ = _cols_call(
            _conv_silu_bwd, rows=t_len, cols=dpart.shape[1], cw=cw,
            col_ins=[(proj, slab(off_xbc + col0)), (dpart, 0)], par_ins=[(cw_ssm, slab(col0)), (ssm_conv_b, slab(col0))],
            col_outs=[BF16], par_outs=[K_SSM, 1], name="d_ssm_conv_" + tag, into=(dproj, slab(off_xbc + col0)))
        dcw_parts.append(dcw_p)
        dcb_parts.append(dcb_p)
        col0 += dpart.shape[1]
    dcw_ssm, dcb_ssm = jnp.concatenate(dcw_parts, axis=1), jnp.concatenate(dcb_parts, axis=1)
    for i, part in enumerate((dgb, dgc, du)):
        dproj = lax.dynamic_update_slice(dproj, part, (0, d + d_xbc + i * d))
    ddt = ddt_c + _pad_to(jnp.transpose(ddt_r), t_len, LANES)
    dwm, _ = _matmul(dproj, n1, ta=True, out_dtype=BF16, name="d_w_in_main")
    dwdt, _ = _matmul(ddt, n1, ta=True, out_dtype=BF16, name="d_w_in_dt")
    own_ref = [k * in_s + i if i < in_s else -1 for k in range(N_DEV) for i in range(in_p)]
    dwin8 = _move_rows(
        dwm, [-1 if g < 0 or off_dt <= g < off_cb else (g if g < off_dt else g - heads) for g in own_ref],
        "place_d_w_in", extra=dwdt, extra_row=[g - off_dt if off_dt <= g < off_cb else -1 for g in own_ref],
    ).reshape(N_DEV, in_p, d)
    dn1, (sib_in,) = _matmul(ddt, wtdt, out_dtype=F32, name="d_norm_mix_out_dt", ride=_scatter_sibling([dwin8]))
    chip_in = _chip_sum(dwin8, sib_in, name="chip_sum_w_in")
    dn1, (parts_in,) = _matmul(dproj, wtm, out_dtype=F32, add=dn1, name="d_norm_mix_out",
                               ride=_scatter_chips([chip_in]))
    (dx, dg1), _ = _rows_call(norm_bwd, rows=t_len, tr=tr, row_ins=[(x2, d, 0), (dn1, d, 0), (dh1, d, 0)],
                              full_ins=[g1], row_outs=[(d, F32)], acc_outs=[(1, d)], name="d_norm_mix")

    wide = d_xbc
    rows_small = [dg1, dcb_ssm, dbias_r + _pad_to(dbias_c.reshape(1, heads), 1, LANES),
                  dalog_r + _pad_to(dalog_c.reshape(1, heads), 1, LANES), ddskip, dgs, dg2, dg3]
    packed = jnp.concatenate([_pad_to(r, 1, wide) for r in rows_small]
                             + [dcw_ssm, _pad_to(dcw_sc, K_SC, wide), jnp.zeros((1, wide), F32)], axis=0)
    (p_small,) = _comm(_gather_all([packed]), "gather_small_grads")

    conv_lo = me * (d_xbc // N_DEV)
    sc_lo = me * (d // N_DEV)

    def pack_state(vals):
        (nm, cb, dtb, al, dk, sg, nf, nfin, cws, scs) = vals
        rows = [_pad_to(a.reshape(1, -1), 1, wide) for a in (nm, cb, dtb, al, dk, sg, nf, nfin)]
        cws_full = lax.dynamic_update_slice(jnp.zeros((K_SSM, wide), F32), cws[0], (0, conv_lo))
        scs_full = lax.dynamic_update_slice(jnp.zeros((K_SC, wide), F32), scs[0], (0, sc_lo))
        return jnp.concatenate(rows + [cws_full, scs_full, jnp.zeros((1, wide), F32)], axis=0)

    w_small = pack_state((norm_mix_g, ssm_conv_b, ssm_dt_bias, ssm_A_log, ssm_D, ssm_norm_g, norm_ffn_g, norm_final_g,
                          ssm_conv_w, sc_conv_w))
    m_small = pack_state((m_norm_mix_g, m_ssm_conv_b, m_ssm_dt_bias, m_ssm_A_log, m_ssm_D, m_ssm_norm_g, m_norm_ffn_g,
                          m_norm_final_g, m_ssm_conv_w, m_sc_conv_w))
    v_small = pack_state((v_norm_mix_g, v_ssm_conv_b, v_ssm_dt_bias, v_ssm_A_log, v_ssm_D, v_ssm_norm_g, v_norm_ffn_g,
                          v_norm_final_g, v_ssm_conv_w, v_sc_conv_w))

    tin = lambda a: _pad_to(tpose(a), in_p, d)
    tin_back = lambda a: jnp.transpose(a[:in_s])[None]
    t_back = lambda a: jnp.transpose(a)[None]
    upd = {
        "w_in": [tin_back(o) for o in _reduce_adamw(parts_in, tin(w_in), tin(m_w_in), tin(v_w_in), name="adamw_w_in")],
        "w_out": [o[None] for o in _reduce_adamw(parts_o, w_out[0], m_w_out[0], v_w_out[0], name="adamw_w_out")],
        "w_gate": [t_back(o) for o in _reduce_adamw(parts_g, tpose(w_gate), tpose(m_w_gate), tpose(v_w_gate),
                                                    name="adamw_w_gate")],
        "w_up": [t_back(o) for o in _reduce_adamw(parts_u, tpose(w_up), tpose(m_w_up), tpose(v_w_up),
                                                  name="adamw_w_up")],
        "w_down": [o[None] for o in _reduce_adamw(parts_d, w_down[0], m_w_down[0], v_w_down[0], name="adamw_w_down")],
    }
    small_upd = _reduce_adamw(p_small, w_small, m_small, v_small, name="adamw_small")

    def unpack(packed_out):
        vec = lambda i, n, shape: packed_out[i, :n].reshape(shape)
        return {
            "norm_mix_g": vec(0, d, (1, d)), "ssm_conv_b": vec(1, d_xbc, (1, d_xbc)),
            "ssm_dt_bias": vec(2, heads, (1, heads)), "ssm_A_log": vec(3, heads, (1, heads)),
            "ssm_D": vec(4, heads, (1, heads)), "ssm_norm_g": vec(5, d, (1, d)), "norm_ffn_g": vec(6, d, (1, d)),
            "norm_final_g": vec(7, d, (d,)),
            "ssm_conv_w": lax.dynamic_slice(packed_out[8:8 + K_SSM], (0, conv_lo), (K_SSM, d_xbc // N_DEV))[None],
            "sc_conv_w": lax.dynamic_slice(packed_out[8 + K_SSM:8 + K_SSM + K_SC], (0, sc_lo), (K_SC, d // N_DEV))[None],
        }

    names = ["norm_mix_g", "w_in", "ssm_conv_w", "ssm_conv_b", "ssm_dt_bias", "ssm_A_log", "ssm_D", "ssm_norm_g",
             "sc_conv_w", "w_out", "norm_ffn_g", "w_gate", "w_up", "w_down", "norm_final_g"]
    outs = []
    for kind in range(4):
        small_k = unpack(small_upd[kind])
        for nm in names:
            outs.append(upd[nm][kind] if nm in upd else small_k[nm])
    return (loss, dx[None], *outs)
```

```python
import collections
import functools

import jax
import jax.numpy as jnp
from jax import lax
from jax.experimental import pallas as pl
from jax.experimental.pallas import tpu as pltpu

F32 = jnp.float32
BF16 = jnp.bfloat16

N_DEV = 8
N_CHIPS = 4
HEADDIM = 64
N_GROUPS = 8
N_STATE = 128
CHUNK = 128
K_SSM = 4
K_SC = 3
EPS = 1e-5
LANES = 128
BF16_ROWS = 16
MM_TILE_MN = 1408
MM_TILE_K = 2816
W_IN_SCATTER_SPLIT = 6 / 7
MM_TILE_N_POST = 704
SSD_CHUNKS_PER_STEP = 4
ROW_BLOCK = 256
V7X_VMEM_BYTES = 64 * 1024 * 1024
VMEM_LIMIT = (V7X_VMEM_BYTES * 3) // 4

ADAM_LR = 0.001
ADAM_B1 = 0.9
ADAM_B2 = 0.999
ADAM_EPS = 1e-08
ADAM_WD = 0.01
ADAM_STEP = 10


def _tile(n, pref, align):
    t = min(pref, n)
    t -= t % align
    while t >= align:
        if n % t == 0:
            return t
        t -= align
    return n


_Ride = collections.namedtuple("_Ride", ["ins", "out_shapes", "aliases", "nsem", "plan"])
_ANY = pl.BlockSpec(memory_space=pl.ANY)


def _coords():
    return lax.axis_index("x"), lax.axis_index("y"), lax.axis_index("c")


def _other_chips(x, y):
    return ((1 - x, y), (x, 1 - y), (1 - x, 1 - y))


def _remote(src, dst, send, recv, k, dev):
    return functools.partial(pltpu.make_async_remote_copy, src_ref=src, dst_ref=dst, send_sem=send.at[k],
                             recv_sem=recv.at[k], device_id=dev, device_id_type=pl.DeviceIdType.MESH)


def _local(src, dst, sem):
    return functools.partial(pltpu.make_async_copy, src, dst, sem)


def _start_all(plan):
    for kind, make in plan:
        if kind != "arrival":
            make().start()


def _wait_all(plan):
    for kind, make in plan:
        if kind == "local":
            make().wait()
        elif kind == "out":
            make().wait_send()
        else:
            make().wait_recv()


def _gather_chips(srcs):
    def plan(ins, outs, send, recv, base):
        x, y, c = _coords()
        me = 4 * x + 2 * y + c
        d = []
        for a, (src, dst) in enumerate(zip(ins, outs)):
            k = base + 4 * a
            d.append(("local", _local(src, dst.at[me], send.at[k + 3])))
            for j, (px, py) in enumerate(_other_chips(x, y)):
                d.append(("out", _remote(src, dst.at[me], send, recv, k + j, (px, py, c))))
                d.append(("arrival", _remote(src, dst.at[4 * px + 2 * py + c], send, recv, k + j, (px, py, c))))
        return d
    shapes = [jax.ShapeDtypeStruct((N_DEV,) + s.shape, s.dtype) for s in srcs]
    return _Ride(list(srcs), shapes, {}, 4 * len(srcs), plan)


def _gather_sibling(bufs):
    def plan(ins, outs, send, recv, base):
        x, y, c = _coords()
        d = []
        for a, buf in enumerate(outs):
            for q in range(N_CHIPS):
                k = base + 4 * a + q
                d.append(("out", _remote(buf.at[2 * q + c], buf.at[2 * q + c], send, recv, k, (x, y, 1 - c))))
                d.append(("arrival", _remote(buf.at[2 * q + c], buf.at[2 * q + 1 - c], send, recv, k, (x, y, 1 - c))))
        return d
    shapes = [jax.ShapeDtypeStruct(b.shape, b.dtype) for b in bufs]
    return _Ride(list(bufs), shapes, {i: i for i in range(len(bufs))}, 4 * len(bufs), plan)


def _scatter_sibling(srcs):
    def plan(ins, outs, send, recv, base):
        x, y, c = _coords()
        d = []
        for a, (src, sib) in enumerate(zip(ins, outs)):
            for q in range(N_CHIPS):
                k = base + 4 * a + q
                d.append(("out", _remote(src.at[2 * q + 1 - c], sib.at[q], send, recv, k, (x, y, 1 - c))))
                d.append(("arrival", _remote(src.at[2 * q + 1 - c], sib.at[q], send, recv, k, (x, y, 1 - c))))
        return d
    shapes = [jax.ShapeDtypeStruct((N_CHIPS,) + s.shape[1:], s.dtype) for s in srcs]
    return _Ride(list(srcs), shapes, {}, 4 * len(srcs), plan)


def _scatter_chips(chips, rows=None, into=None):
    n = len(chips)

    def plan(ins, outs, send, recv, base):
        x, y, c = _coords()
        mine = 2 * x + y
        cut = (lambda ref: ref) if rows is None else (lambda ref: ref.at[pl.ds(rows[0], rows[1])])
        d = []
        for a, (chip, parts) in enumerate(zip(ins[:n], outs)):
            k = base + 4 * a
            d.append(("local", _local(cut(chip.at[mine]), cut(parts.at[mine]), send.at[k + 3])))
            for j, (px, py) in enumerate(_other_chips(x, y)):
                q = 2 * px + py
                d.append(("out", _remote(cut(chip.at[q]), cut(parts.at[mine]), send, recv, k + j, (px, py, c))))
                d.append(("arrival", _remote(cut(chip.at[q]), cut(parts.at[q]), send, recv, k + j, (px, py, c))))
        return d
    shapes = [jax.ShapeDtypeStruct(s.shape, s.dtype) for s in chips]
    if into is None:
        return _Ride(list(chips), shapes, {}, 4 * n, plan)
    return _Ride(list(chips) + list(into), shapes, {n + a: a for a in range(n)}, 4 * n, plan)


def _gather_all(srcs):
    def plan(ins, outs, send, recv, base):
        x, y, c = _coords()
        me = 4 * x + 2 * y + c
        d = []
        for a, (src, dst) in enumerate(zip(ins, outs)):
            k = base + N_DEV * a
            d.append(("local", _local(src, dst.at[me], send.at[k])))
            for j in range(1, N_DEV):
                px = 1 - x if (j >> 2) & 1 else x
                py = 1 - y if (j >> 1) & 1 else y
                pc = 1 - c if j & 1 else c
                d.append(("out", _remote(src, dst.at[me], send, recv, k + j, (px, py, pc))))
                d.append(("arrival", _remote(src, dst.at[4 * px + 2 * py + pc], send, recv, k + j, (px, py, pc))))
        return d
    shapes = [jax.ShapeDtypeStruct((N_DEV,) + s.shape, s.dtype) for s in srcs]
    return _Ride(list(srcs), shapes, {}, N_DEV * len(srcs), plan)


def _merge(*rides):
    ins, outs, aliases, parts, nsem = [], [], {}, [], 0
    for r in rides:
        parts.append((len(ins), len(outs), nsem, r))
        aliases.update({len(ins) + i: len(outs) + j for i, j in r.aliases.items()})
        ins += r.ins
        outs += r.out_shapes
        nsem += r.nsem

    def plan(i, o, send, recv, base):
        d = []
        for i0, o0, s0, r in parts:
            d += r.plan(i[i0:i0 + len(r.ins)], o[o0:o0 + len(r.out_shapes)], send, recv, base + s0)
        return d
    return _Ride(ins, outs, aliases, nsem, plan)


def _comm(ride, name):
    n_in, n_out = len(ride.ins), len(ride.out_shapes)

    def body(*refs):
        plan = ride.plan(refs[:n_in], refs[n_in:n_in + n_out], refs[-2], refs[-1], 0)
        _start_all(plan)
        _wait_all(plan)

    return pl.pallas_call(
        body, name=name, in_specs=[_ANY] * n_in, out_specs=[_ANY] * n_out, out_shape=ride.out_shapes,
        scratch_shapes=[pltpu.SemaphoreType.DMA((ride.nsem,)), pltpu.SemaphoreType.DMA((ride.nsem,))],
        input_output_aliases=dict(ride.aliases),
        compiler_params=pltpu.CompilerParams(has_side_effects=True),
    )(*ride.ins)


def _gather_chips_relayed(big, small, name):
    srcs = list(big) + list(small)
    n, nsem = len(srcs), 5 * len(srcs)

    def body(*refs):
        ins, outs, send, recv = refs[:n], refs[n:2 * n], refs[-2], refs[-1]
        x, y, c = _coords()
        slot = lambda dev: 4 * dev[0] + 2 * dev[1] + dev[2]
        me, nbr_x, nbr_y, diag = (x, y, c), (1 - x, y, c), (x, 1 - y, c), (1 - x, 1 - y, c)
        own, sends = [], []
        for a, (src, dst) in enumerate(zip(ins, outs)):
            k = 5 * a
            own.append(_local(src, dst.at[slot(me)], send.at[k + 4])())
            sends.append(_remote(src, dst.at[slot(me)], send, recv, k, nbr_x)())
            sends.append(_remote(src, dst.at[slot(me)], send, recv, k + 1, nbr_y)())
            if a >= len(big):
                sends.append(_remote(src, dst.at[slot(me)], send, recv, k + 2, diag)())
        for s in own + sends:
            s.start()
        for a, (src, dst) in enumerate(zip(ins, outs)):
            k = 5 * a
            _remote(src, dst.at[slot(nbr_x)], send, recv, k, nbr_x)().wait_recv()
            if a < len(big):
                half = src.shape[0] // 2
                part = dst.at[slot(nbr_x)].at[pl.ds(0, half)]
                fwd = _remote(part, part, send, recv, k + 2, nbr_y)()
                fwd.start()
                sends.append(fwd)
            _remote(src, dst.at[slot(nbr_y)], send, recv, k + 1, nbr_y)().wait_recv()
            if a < len(big):
                part = dst.at[slot(nbr_y)].at[pl.ds(half, src.shape[0] - half)]
                fwd = _remote(part, part, send, recv, k + 3, nbr_x)()
                fwd.start()
                sends.append(fwd)
        for a, (src, dst) in enumerate(zip(ins, outs)):
            k = 5 * a
            if a < len(big):
                half = src.shape[0] // 2
                lo = dst.at[slot(diag)].at[pl.ds(0, half)]
                hi = dst.at[slot(diag)].at[pl.ds(half, src.shape[0] - half)]
                _remote(lo, lo, send, recv, k + 2, nbr_y)().wait_recv()
                _remote(hi, hi, send, recv, k + 3, nbr_x)().wait_recv()
            else:
                _remote(src, dst.at[slot(diag)], send, recv, k + 2, diag)().wait_recv()
        for lc in own:
            lc.wait()
        for s in sends:
            s.wait_send()

    return pl.pallas_call(
        body, name=name, in_specs=[_ANY] * n, out_specs=[_ANY] * n,
        out_shape=[jax.ShapeDtypeStruct((N_DEV,) + s.shape, s.dtype) for s in srcs],
        scratch_shapes=[pltpu.SemaphoreType.DMA((nsem,)), pltpu.SemaphoreType.DMA((nsem,))],
        compiler_params=pltpu.CompilerParams(has_side_effects=True),
    )(*srcs)


def _call(body, *, name, grid, in_specs, out_specs, out_shape, args, sem, scratch=(), ride=None, base=None):
    params = pltpu.CompilerParams(dimension_semantics=sem, vmem_limit_bytes=VMEM_LIMIT)
    own_aliases = {}
    if base is not None:
        inner, n_host = body, len(args)
        body = lambda *refs: inner(*refs[:n_host], *refs[n_host + 1:])
        own_aliases[n_host] = base[1]
        args, in_specs = tuple(args) + (base[0],), list(in_specs) + [_ANY]
    if ride is None:
        res = pl.pallas_call(body, name=name, grid=grid, in_specs=in_specs, out_specs=out_specs,
                             out_shape=out_shape, scratch_shapes=list(scratch), input_output_aliases=own_aliases,
                             compiler_params=params)(*args)
        return list(res), []
    n_in, n_out, n_scr = len(args), len(out_shape), len(scratch)
    r_in, r_out = len(ride.ins), len(ride.out_shapes)

    def hosted(*refs):
        h_in, rin = refs[:n_in], refs[n_in:n_in + r_in]
        o0 = n_in + r_in
        h_out, rout = refs[o0:o0 + n_out], refs[o0 + n_out:o0 + n_out + r_out]
        s0 = o0 + n_out + r_out
        h_scr, send, recv = refs[s0:s0 + n_scr], refs[s0 + n_scr], refs[s0 + n_scr + 1]
        ids = [pl.program_id(i) for i in range(len(grid))]
        first = functools.reduce(lambda p, q: p & q, [i == 0 for i in ids])
        last = functools.reduce(lambda p, q: p & q, [i == n - 1 for i, n in zip(ids, grid)])

        @pl.when(first)
        def _():
            _start_all(ride.plan(rin, rout, send, recv, 0))

        body(*h_in, *h_out, *h_scr)

        @pl.when(last)
        def _():
            _wait_all(ride.plan(rin, rout, send, recv, 0))

    res = pl.pallas_call(
        hosted, name=name, grid=grid, in_specs=list(in_specs) + [_ANY] * r_in,
        out_specs=list(out_specs) + [_ANY] * r_out, out_shape=list(out_shape) + list(ride.out_shapes),
        scratch_shapes=list(scratch) + [pltpu.SemaphoreType.DMA((ride.nsem,)), pltpu.SemaphoreType.DMA((ride.nsem,))],
        input_output_aliases={**own_aliases, **{n_in + i: n_out + j for i, j in ride.aliases.items()}},
        compiler_params=params,
    )(*args, *ride.ins)
    return list(res[:n_out]), list(res[n_out:])


def _matmul(a, b, *, ta=False, tb=False, out_dtype=BF16, add=None, post=None, name, ride=None, tn_max=MM_TILE_MN):
    m = a.shape[1] if ta else a.shape[0]
    k = a.shape[0] if ta else a.shape[1]
    n = b.shape[0] if tb else b.shape[1]
    assert k == (b.shape[1] if tb else b.shape[0])
    tm, tn, tk = _tile(m, MM_TILE_MN, LANES), _tile(n, tn_max, LANES), _tile(k, MM_TILE_K, LANES)
    nk = k // tk
    dims = (((0 if ta else 1,), (1 if tb else 0,)), ((), ()))
    single = post is None
    if add is not None:
        post = (lambda r, t: (r + t,), [add], [out_dtype])
    elif post is None:
        post = (lambda r: (r,), [], [out_dtype])
    post_fn, extras, out_dtypes = post
    n_ex, n_o = len(extras), len(out_dtypes)

    def body(*refs):
        a_ref, b_ref = refs[:2]
        ex_refs, o_refs = refs[2:2 + n_ex], refs[2 + n_ex:2 + n_ex + n_o]

        def finish(r):
            for o_ref, v in zip(o_refs, post_fn(r, *[e[...].astype(F32) for e in ex_refs])):
                o_ref[...] = v.astype(o_ref.dtype)

        part = lax.dot_general(a_ref[...].astype(BF16), b_ref[...].astype(BF16), dims, preferred_element_type=F32)
        if nk == 1:
            finish(part)
            return
        acc = refs[-1]
        kk = pl.program_id(2)

        @pl.when(kk == 0)
        def _():
            acc[...] = part

        @pl.when((kk > 0) & (kk < nk - 1))
        def _():
            acc[...] += part

        @pl.when(kk == nk - 1)
        def _():
            finish(acc[...] + part)

    a_spec = (pl.BlockSpec((tk, tm), lambda i, j, kk: (kk, i)) if ta
              else pl.BlockSpec((tm, tk), lambda i, j, kk: (i, kk)))
    b_spec = (pl.BlockSpec((tn, tk), lambda i, j, kk: (j, kk)) if tb
              else pl.BlockSpec((tk, tn), lambda i, j, kk: (kk, j)))
    o_spec = pl.BlockSpec((tm, tn), lambda i, j, kk: (i, j))
    outs, rides = _call(
        body, name=name, grid=(m // tm, n // tn, nk),
        in_specs=[a_spec, b_spec] + [o_spec] * n_ex, out_specs=[o_spec] * n_o,
        out_shape=[jax.ShapeDtypeStruct((m, n), dt) for dt in out_dtypes], args=(a, b, *extras),
        scratch=[pltpu.VMEM((tm, tn), F32)] if nk > 1 else [], sem=("parallel", "parallel", "arbitrary"), ride=ride)
    return (outs[0] if single else outs), rides


def _rows_call(fn, *, rows, tr, row_ins, full_ins, row_outs, acc_outs, name, ride=None):
    nr, nf, no, na = len(row_ins), len(full_ins), len(row_outs), len(acc_outs)

    def body(*refs):
        vals = [r[...] for r in refs[:nr + nf]]
        outs, accs = fn(*vals)
        for r, v in zip(refs[nr + nf:nr + nf + no], outs):
            r[...] = v.astype(r.dtype)
        if na:
            @pl.when(pl.program_id(0) == 0)
            def _():
                for r in refs[nr + nf + no:]:
                    r[...] = jnp.zeros_like(r)
            for r, v in zip(refs[nr + nf + no:], accs):
                r[...] += v

    in_specs = [pl.BlockSpec((tr, w), functools.partial(lambda cb, i: (i, cb), cb)) for _, w, cb in row_ins]
    in_specs += [pl.BlockSpec(f.shape, lambda i: (0, 0)) for f in full_ins]
    out_specs = [pl.BlockSpec((tr, o[0]), lambda i: (i, 0)) for o in row_outs]
    out_specs += [pl.BlockSpec(s, lambda i: (0, 0)) for s in acc_outs]
    out_shape = [jax.ShapeDtypeStruct((rows, o[-1] if len(o) == 3 else o[0]), o[1]) for o in row_outs]
    out_shape += [jax.ShapeDtypeStruct(s, F32) for s in acc_outs]
    return _call(body, name=name, grid=(rows // tr,), in_specs=in_specs, out_specs=out_specs, out_shape=out_shape,
                 args=tuple(a for a, _, _ in row_ins) + tuple(full_ins), sem=("arbitrary",), ride=ride)


def _cols_call(fn, *, rows, cols, cw, col_ins, par_ins, col_outs, par_outs, name, ride=None, into=None):
    nc, npar = len(col_ins), len(par_ins)

    def body(*refs):
        vals = [r[...] for r in refs[:nc + npar]]
        outs, pouts = fn(*vals)
        for r, v in zip(refs[nc + npar:], tuple(outs) + tuple(pouts)):
            r[...] = v.astype(r.dtype)

    in_specs = [pl.BlockSpec((rows, cw), functools.partial(lambda off, j: (0, off + j), off)) for _, off in col_ins]
    in_specs += [pl.BlockSpec((p.shape[0], cw), functools.partial(lambda off, j: (0, off + j), off))
                 for p, off in par_ins]
    out_specs = [pl.BlockSpec((rows, cw), lambda j: (0, j)) for _ in col_outs]
    out_specs += [pl.BlockSpec((k, cw), lambda j: (0, j)) for k in par_outs]
    out_shape = [jax.ShapeDtypeStruct((rows, cols), dt) for dt in col_outs]
    out_shape += [jax.ShapeDtypeStruct((k, cols), F32) for k in par_outs]
    if into is not None:
        out_specs[0] = pl.BlockSpec((rows, cw), lambda j: (0, into[1] + j))
        out_shape[0] = jax.ShapeDtypeStruct(into[0].shape, into[0].dtype)
    return _call(body, name=name, grid=(cols // cw,), in_specs=in_specs, out_specs=out_specs, out_shape=out_shape,
                 args=tuple(a for a, _ in col_ins) + tuple(p for p, _ in par_ins), sem=("arbitrary",), ride=ride,
                 base=None if into is None else (into[0], 0))


def _sigmoid(v):
    return 1.0 / (1.0 + jnp.exp(-v))


def _softplus(v):
    return jnp.maximum(v, 0.0) + jnp.log(1.0 + jnp.exp(-jnp.abs(v)))


def _rms(v, g):
    return v * lax.rsqrt(jnp.mean(v * v, axis=-1, keepdims=True) + EPS) * g


def _shift_down(v, s, row):
    return jnp.where(row >= s, pltpu.roll(v, s, 0), 0.0)


def _shift_up(v, s, row):
    n = v.shape[0]
    return jnp.where(row < n - s, pltpu.roll(v, n - s, 0), 0.0)


def _causal_conv(u, w, row):
    k_taps = w.shape[0]
    acc = u * w[k_taps - 1:k_taps, :]
    for k in range(k_taps - 1):
        acc = acc + _shift_down(u, k_taps - 1 - k, row) * w[k:k + 1, :]
    return acc


def _causal_conv_bwd(u, dy, w, row):
    k_taps = w.shape[0]
    tap = lax.broadcasted_iota(jnp.int32, w.shape, 0)
    du = dy * w[k_taps - 1:k_taps, :]
    dw = jnp.where(tap == k_taps - 1, jnp.sum(dy * u, axis=0, keepdims=True), 0.0)
    for k in range(k_taps - 1):
        s = k_taps - 1 - k
        du = du + _shift_up(dy, s, row) * w[k:k + 1, :]
        dw = dw + jnp.where(tap == k, jnp.sum(dy * _shift_down(u, s, row), axis=0, keepdims=True), 0.0)
    return du, dw


def _conv_silu_fwd(u, w, b):
    u = u.astype(F32)
    row = lax.broadcasted_iota(jnp.int32, u.shape, 0)
    pre = _causal_conv(u, w, row) + b
    return (pre * _sigmoid(pre),), ()


def _conv_silu_bwd(u, dy, w, b):
    u = u.astype(F32)
    dy = dy.astype(F32)
    row = lax.broadcasted_iota(jnp.int32, u.shape, 0)
    pre = _causal_conv(u, w, row) + b
    s = _sigmoid(pre)
    dpre = dy * (s * (1.0 + pre * (1.0 - s)))
    du, dw = _causal_conv_bwd(u, dpre, w, row)
    return (du,), (dw, jnp.sum(dpre, axis=0, keepdims=True))


def _shortconv_fwd(gb, gc, u, w):
    gb, gc, u = gb.astype(F32), gc.astype(F32), u.astype(F32)
    row = lax.broadcasted_iota(jnp.int32, u.shape, 0)
    return (gb * _causal_conv(gc * u, w, row),), ()


def _shortconv_bwd(gb, gc, u, dy, w):
    gb, gc, u, dy = gb.astype(F32), gc.astype(F32), u.astype(F32), dy.astype(F32)
    row = lax.broadcasted_iota(jnp.int32, u.shape, 0)
    v = gc * u
    dgb = dy * _causal_conv(v, w, row)
    dv, dw = _causal_conv_bwd(v, dy * gb, w, row)
    return (dgb, dv * u, dv * gc), (dw,)


def _split3(v):
    hi = v.astype(BF16)
    r1 = v - hi.astype(F32)
    mid = r1.astype(BF16)
    lo = (r1 - mid.astype(F32)).astype(BF16)
    return hi, mid, lo


def _exact_dot(v, m01, dims, v_is_lhs):
    def one(p):
        return (lax.dot_general(p, m01, dims, preferred_element_type=F32) if v_is_lhs
                else lax.dot_general(m01, p, dims, preferred_element_type=F32))
    hi, mid, lo = _split3(v)
    return (one(lo) + one(mid)) + one(hi)


_NN = (((1,), (0,)), ((), ()))
_NT = (((1,), (1,)), ((), ()))
_TN = (((0,), (0,)), ((), ()))


@jax.custom_vjp
def _cumsum_rows(tril, v):
    return _exact_dot(v, tril, _NN, False)


def _cumsum_rows_fwd(tril, v):
    return _cumsum_rows(tril, v), tril


def _cumsum_rows_bwd(tril, ct):
    return None, _exact_dot(ct, tril, _TN, False)


_cumsum_rows.defvjp(_cumsum_rows_fwd, _cumsum_rows_bwd)


@jax.custom_vjp
def _cumsum_lanes(tril, v):
    return _exact_dot(v, tril, _NT, True)


def _cumsum_lanes_fwd(tril, v):
    return _cumsum_lanes(tril, v), tril


def _cumsum_lanes_bwd(tril, ct):
    return None, _exact_dot(ct, tril, _NN, True)


_cumsum_lanes.defvjp(_cumsum_lanes_fwd, _cumsum_lanes_bwd)


@jax.custom_vjp
def _expand(e01, v):
    return _exact_dot(v, e01, _NN, True)


def _expand_fwd(e01, v):
    return _expand(e01, v), e01


def _expand_bwd(e01, ct):
    return None, _exact_dot(ct, e01, _NT, True)


_expand.defvjp(_expand_fwd, _expand_bwd)


def _causal_mask(n):
    li = lax.broadcasted_iota(jnp.int32, (n, n), 0)
    si = lax.broadcasted_iota(jnp.int32, (n, n), 1)
    return si <= li


def _dt_prep(dtc, dtr, bias_r, bias_c, alog_r, alog_c):
    dt_c = _softplus(dtc + bias_r)
    dt_r = _softplus(dtr + bias_c)
    tril = jnp.where(_causal_mask(dtc.shape[0]), 1.0, 0.0).astype(BF16)
    cs_c = _cumsum_rows(tril, dt_c * (-jnp.exp(alog_r)))
    cs_r = _cumsum_lanes(tril, dt_r * (-jnp.exp(alog_c)))
    return dt_c, cs_c, cs_r


def _ssd_chunk(r_heads, xs, bg, cg, dt_c, cs_c, cs_rg, e01, dskip_e, hp):
    l_len, rp = xs.shape
    p = rp // r_heads
    causal = _causal_mask(l_len)
    lane_head = lax.broadcasted_iota(jnp.int32, (1, rp), 1) // p
    dt_e = _expand(e01, dt_c)
    cs_e = _expand(e01, cs_c)
    cl_e = cs_e[l_len - 1:l_len, :]
    x = xs * dt_e
    bgb, cgb = bg.astype(BF16), cg.astype(BF16)
    cb = lax.dot_general(cgb, bgb, _NT, preferred_element_type=F32)
    ms, xm = [], []
    for r in range(r_heads):
        seg = cs_e[:, r * p:r * p + 1] - cs_rg[r:r + 1, :]
        decay = jnp.exp(jnp.where(causal, seg, -1e30))
        ms.append((cb * decay).astype(BF16))
        xm.append(jnp.where(lane_head == r, x, 0.0).astype(BF16))
    y_diag = lax.dot_general(jnp.concatenate(ms, axis=1), jnp.concatenate(xm, axis=0), _NN,
                             preferred_element_type=F32)
    y_off = lax.dot_general(cgb, hp.astype(BF16), _NN, preferred_element_type=F32) * jnp.exp(cs_e)
    xd = (x * jnp.exp(cl_e - cs_e)).astype(BF16)
    states = lax.dot_general(bgb, xd, _TN, preferred_element_type=F32)
    h_next = hp * jnp.exp(cl_e) + states
    y = y_diag + y_off + dskip_e * xs
    return y, h_next


def _ssd_dt(dtc, dtr, small, cots=None):
    t_len, heads = dtc.shape[0], dtr.shape[0]
    nc = t_len // CHUNK
    col = pl.BlockSpec((CHUNK, LANES), lambda c: (c, 0))
    row = pl.BlockSpec((heads, CHUNK), lambda c: (0, c))
    full = [pl.BlockSpec(s.shape, lambda c: (0, 0)) for s in small]
    shapes = [jax.ShapeDtypeStruct((t_len, LANES), F32), jax.ShapeDtypeStruct((t_len, LANES), F32),
              jax.ShapeDtypeStruct((heads, t_len), F32)]
    if cots is None:
        def body(dtc_ref, dtr_ref, br, bc, ar, ac, dt_ref, csc_ref, csr_ref):
            dt_ref[...], csc_ref[...], csr_ref[...] = _dt_prep(dtc_ref[...], dtr_ref[...], br[...], bc[...],
                                                                ar[...], ac[...])
        return _call(body, name="ssd_dt", grid=(nc,), in_specs=[col, row] + full, out_specs=[col, col, row],
                     out_shape=shapes, args=(dtc, dtr, *small), sem=("parallel",))[0]

    g_dt, g_csc, g_csr, ddk, e01 = cots

    def body(dtc_ref, dtr_ref, br, bc, ar, ac, g_dt_ref, g_csc_ref, g_csr_ref, ddk_ref, e_ref,
             ddtc_ref, ddtr_ref, *dsmall):
        _, vjp = jax.vjp(_dt_prep, dtc_ref[...], dtr_ref[...], br[...], bc[...], ar[...], ac[...])
        grads = vjp((g_dt_ref[...], g_csc_ref[...], g_csr_ref[...]))
        ddtc_ref[...], ddtr_ref[...] = grads[0], grads[1]
        ddk8 = jnp.broadcast_to(ddk_ref[...], (8, ddk_ref.shape[1]))
        dskip = _exact_dot(ddk8, e_ref[...], _NT, True)[0:1, :]

        @pl.when(pl.program_id(0) == 0)
        def _():
            for r in dsmall:
                r[...] = jnp.zeros_like(r)

        for r, gr in zip(dsmall, tuple(grads[2:]) + (dskip,)):
            r[...] += gr

    acc = list(small) + [small[0]]
    return _call(body, name="d_ssd_dt", grid=(nc,),
                 in_specs=[col, row] + full + [col, col, row, pl.BlockSpec((None, 1, e01.shape[1]), lambda c: (c, 0, 0)),
                                               pl.BlockSpec(e01.shape, lambda c: (0, 0))],
                 out_specs=[col, row] + [pl.BlockSpec(s.shape, lambda c: (0, 0)) for s in acc],
                 out_shape=[shapes[0], shapes[2]] + [jax.ShapeDtypeStruct(s.shape, F32) for s in acc],
                 args=(dtc, dtr, *small, g_dt, g_csc, g_csr, ddk, e01), sem=("arbitrary",))[0]


def _ssd_specs(t_len, d_ssm, r_heads, reverse):
    rp = r_heads * HEADDIM
    nc = t_len // CHUNK
    per = next(p for p in (SSD_CHUNKS_PER_STEP, 2, 1) if nc % p == 0)
    ns, rows = nc // per, per * CHUNK
    cidx = (lambda c: ns - 1 - c) if reverse else (lambda c: c)
    b_off = d_ssm // N_STATE
    specs = dict(
        xs=pl.BlockSpec((rows, rp), lambda c, g: (cidx(c), g)),
        b=pl.BlockSpec((rows, N_STATE), lambda c, g: (cidx(c), b_off + g)),
        c=pl.BlockSpec((rows, N_STATE), lambda c, g: (cidx(c), b_off + N_GROUPS + g)),
        grad_bc=pl.BlockSpec((rows, N_STATE), lambda c, g: (cidx(c), g)),
        col=pl.BlockSpec((rows, LANES), lambda c, g: (cidx(c), 0)),
        csr=pl.BlockSpec((None, r_heads, rows), lambda c, g: (g, 0, cidx(c))),
        e01=pl.BlockSpec((LANES, rp), lambda c, g: (0, g)),
        dskip=pl.BlockSpec((1, rp), lambda c, g: (0, g)),
        hprev=pl.BlockSpec((per, None, N_STATE, rp), lambda c, g: (cidx(c), g, 0, 0)),
        ddk=pl.BlockSpec((per, 1, rp), lambda c, g: (cidx(c), 0, g)),
    )
    return specs, nc, ns, per, rp


def _ssd_fwd(xbc, dt_c, cs_c, cs_r3, e01, dskip_e, *, d_ssm, r_heads, ride=None):
    t_len = xbc.shape[0]
    sp, nc, ns, per, rp = _ssd_specs(t_len, d_ssm, r_heads, False)

    def body(xs_ref, b_ref, c_ref, dt_ref, csc_ref, csr_ref, e_ref, dk_ref, y_ref, hprev_ref, h_ref):
        c, g = pl.program_id(0), pl.program_id(1)

        @pl.when(c == 0)
        def _():
            h_ref[g] = jnp.zeros((N_STATE, rp), F32)

        hp = h_ref[g]
        for s in range(per):
            r = pl.ds(s * CHUNK, CHUNK)
            hprev_ref[s] = hp
            y, hp = _ssd_chunk(r_heads, xs_ref[r, :].astype(F32), b_ref[r, :].astype(F32), c_ref[r, :].astype(F32),
                               dt_ref[r, :], csc_ref[r, :], csr_ref[:, r], e_ref[...], dk_ref[...], hp)
            y_ref[r, :] = y
        h_ref[g] = hp

    return _call(
        body, name="ssd_fwd", grid=(ns, N_GROUPS),
        in_specs=[sp["xs"], sp["b"], sp["c"], sp["col"], sp["col"], sp["csr"], sp["e01"], sp["dskip"]],
        out_specs=[sp["xs"], sp["hprev"]],
        out_shape=[jax.ShapeDtypeStruct((t_len, d_ssm), F32),
                   jax.ShapeDtypeStruct((nc, N_GROUPS, N_STATE, rp), F32)],
        args=(xbc, xbc, xbc, dt_c, cs_c, cs_r3, e01, dskip_e), scratch=[pltpu.VMEM((N_GROUPS, N_STATE, rp), F32)],
        sem=("arbitrary", "arbitrary"), ride=ride)


def _ssd_bwd(xbc, dt_c, cs_c, cs_r3, e01, dskip_e, hprev, dy, *, d_ssm, r_heads, ride=None):
    t_len = xbc.shape[0]
    sp, nc, ns, per, rp = _ssd_specs(t_len, d_ssm, r_heads, True)

    def body(xs_ref, b_ref, c_ref, dt_ref, csc_ref, csr_ref, e_ref, dk_ref, hprev_ref, dy_ref,
             dxs_ref, db_ref, dc_ref, ddt_ref, dcsc_ref, dcsr_ref, ddk_ref, dh_ref):
        c, g = pl.program_id(0), pl.program_id(1)

        @pl.when(c == 0)
        def _():
            dh_ref[g] = jnp.zeros((N_STATE, rp), F32)

        @pl.when(g == 0)
        def _():
            ddt_ref[...] = jnp.zeros_like(ddt_ref)
            dcsc_ref[...] = jnp.zeros_like(dcsc_ref)

        e01 = e_ref[...]
        fn = lambda xs, bg, cg, dt, csc, csr, dk, hp: _ssd_chunk(r_heads, xs, bg, cg, dt, csc, csr, e01, dk, hp)
        dh = dh_ref[g]
        for s in reversed(range(per)):
            r = pl.ds(s * CHUNK, CHUNK)
            _, vjp = jax.vjp(fn, xs_ref[r, :].astype(F32), b_ref[r, :].astype(F32), c_ref[r, :].astype(F32),
                             dt_ref[r, :], csc_ref[r, :], csr_ref[:, r], dk_ref[...], hprev_ref[s])
            dxs, dbg, dcg, ddt, dcsc, dcsr, ddk, dh = vjp((dy_ref[r, :], dh))
            dxs_ref[r, :] = dxs.astype(dxs_ref.dtype)
            db_ref[r, :] = dbg.astype(db_ref.dtype)
            dc_ref[r, :] = dcg.astype(dc_ref.dtype)
            ddt_ref[r, :] += ddt
            dcsc_ref[r, :] += dcsc
            dcsr_ref[:, r] = dcsr
            ddk_ref[s] = ddk
        dh_ref[g] = dh

    n_bc = N_GROUPS * N_STATE
    return _call(
        body, name="ssd_bwd", grid=(ns, N_GROUPS),
        in_specs=[sp["xs"], sp["b"], sp["c"], sp["col"], sp["col"], sp["csr"], sp["e01"], sp["dskip"], sp["hprev"],
                  sp["xs"]],
        out_specs=[sp["xs"], sp["grad_bc"], sp["grad_bc"], sp["col"], sp["col"], sp["csr"], sp["ddk"]],
        out_shape=[jax.ShapeDtypeStruct((t_len, d_ssm), BF16), jax.ShapeDtypeStruct((t_len, n_bc), BF16),
                   jax.ShapeDtypeStruct((t_len, n_bc), BF16), jax.ShapeDtypeStruct(dt_c.shape, F32),
                   jax.ShapeDtypeStruct(cs_c.shape, F32), jax.ShapeDtypeStruct(cs_r3.shape, F32),
                   jax.ShapeDtypeStruct((nc, 1, d_ssm), F32)],
        args=(xbc, xbc, xbc, dt_c, cs_c, cs_r3, e01, dskip_e, hprev, dy),
        scratch=[pltpu.VMEM((N_GROUPS, N_STATE, rp), F32)], sem=("arbitrary", "arbitrary"), ride=ride)


def _chip_sum(src, sib, *, name):
    rows, cols = src.shape[1:]
    tr = _tile(rows, 256, BF16_ROWS)
    core = lax.axis_index("c").astype(jnp.int32).reshape(1)

    def body(c_ref, a_ref, b_ref, o_ref):
        o_ref[...] = (a_ref[...].astype(F32) + b_ref[...].astype(F32)).astype(o_ref.dtype)

    grid_spec = pltpu.PrefetchScalarGridSpec(
        num_scalar_prefetch=1, grid=(N_CHIPS, rows // tr),
        in_specs=[pl.BlockSpec((None, tr, cols), lambda q, i, c_ref: (2 * q + c_ref[0], i, 0)),
                  pl.BlockSpec((None, tr, cols), lambda q, i, c_ref: (q, i, 0))],
        out_specs=pl.BlockSpec((None, tr, cols), lambda q, i, c_ref: (q, i, 0)))
    return pl.pallas_call(
        body, name=name, grid_spec=grid_spec, out_shape=jax.ShapeDtypeStruct(sib.shape, sib.dtype),
        compiler_params=pltpu.CompilerParams(dimension_semantics=("parallel", "parallel"), vmem_limit_bytes=VMEM_LIMIT),
    )(core, src, sib)


def _adamw(w, g, m, v):
    m = ADAM_B1 * m + (1.0 - ADAM_B1) * g
    v = ADAM_B2 * v + (1.0 - ADAM_B2) * (g * g)
    m_hat = m / (1.0 - ADAM_B1 ** ADAM_STEP)
    v_hat = v / (1.0 - ADAM_B2 ** ADAM_STEP)
    delta = -ADAM_LR * (m_hat / (jnp.sqrt(v_hat) + ADAM_EPS) + ADAM_WD * w)
    return delta, m, v


def _reduce_adamw(parts, w, m, v, *, name):
    n_parts = parts.shape[0]
    rows, cols = w.shape
    tr = _tile(rows, 128, BF16_ROWS)

    def body(p_ref, w_ref, m_ref, v_ref, g_ref, d_ref, mo_ref, vo_ref):
        g = p_ref[0].astype(F32)
        for k in range(1, n_parts):
            g = g + p_ref[k].astype(F32)
        delta, mn, vn = _adamw(w_ref[...], g, m_ref[...], v_ref[...])
        g_ref[...] = g
        d_ref[...] = delta
        mo_ref[...] = mn
        vo_ref[...] = vn

    spec = pl.BlockSpec((tr, cols), lambda i: (i, 0))
    outs, _ = _call(
        body, name=name, grid=(rows // tr,),
        in_specs=[pl.BlockSpec((n_parts, tr, cols), lambda i: (0, i, 0)), spec, spec, spec],
        out_specs=[spec] * 4, out_shape=[jax.ShapeDtypeStruct((rows, cols), F32)] * 4,
        args=(parts, w, m, v), sem=("parallel",))
    return outs


def _move_rows(src, src_row, name, extra=None, extra_row=None):
    rb, n_out, cols = ROW_BLOCK, len(src_row), src.shape[1]
    assert n_out % rb == 0 and src.shape[0] % rb == 0 and src.shape[0] // rb >= 3
    n_blocks, max_b0, seg_cap = n_out // rb, src.shape[0] // rb - 3, 4

    def segments(rows_of, lo):
        segs, r = [], 0
        while r < rb:
            if rows_of[r] < 0:
                r += 1
                continue
            e = r
            while e + 1 < rb and rows_of[e + 1] == rows_of[e] + 1:
                e += 1
            segs.append((r, e + 1, rows_of[r] - r - lo))
            r = e + 1
        assert len(segs) <= seg_cap
        return segs + [(0, 0, 0)] * (seg_cap - len(segs))

    table = []
    for j in range(n_blocks):
        rows_j = list(src_row[j * rb:(j + 1) * rb])
        valid = [v for v in rows_j if v >= 0]
        b0 = min(max((min(valid) // rb) if valid else 0, 0), max_b0)
        assert not valid or max(valid) < (b0 + 3) * rb
        row = [b0] + [v for seg in segments(rows_j, b0 * rb) for v in seg]
        extra_j = [] if extra is None else list(extra_row[j * rb:(j + 1) * rb])
        if extra is not None:
            row += [v for seg in segments(extra_j, 0) for v in seg]
        row += [int(bool(valid) and max(valid) >= (b0 + 2) * rb), int(any(v >= 0 for v in extra_j))]
        table.append(row)
    table = jnp.asarray(table, jnp.int32)
    flag_third, flag_extra = len(table[0]) - 2, len(table[0]) - 1

    def select(tbl_ref, j, first, width):
        r = lax.broadcasted_iota(jnp.int32, (rb, width), 0)
        c = lax.broadcasted_iota(jnp.int32, (rb, width), 1)
        hit = jnp.zeros((rb, width), jnp.bool_)
        for s in range(seg_cap):
            lo, hi, off = (tbl_ref[j, first + 3 * s + i] for i in range(3))
            hit = hit | ((r >= lo) & (r < hi) & (c == r + off))
        return jnp.where(hit, 1.0, 0.0).astype(BF16)

    def body(tbl_ref, *refs):
        o_ref = refs[-1]
        j = pl.program_id(0)
        sel = select(tbl_ref, j, 1, 3 * rb)
        pick = lambda b: lax.dot_general(sel[:, b * rb:(b + 1) * rb], refs[b][...], _NN, preferred_element_type=F32)
        o_ref[...] = (pick(0) + pick(1)).astype(o_ref.dtype)

        @pl.when(tbl_ref[j, flag_third] == 1)
        def _():
            o_ref[...] = (o_ref[...].astype(F32) + pick(2)).astype(o_ref.dtype)

        if extra is not None:
            @pl.when(tbl_ref[j, flag_extra] == 1)
            def _():
                more = lax.dot_general(select(tbl_ref, j, 1 + 3 * seg_cap, extra.shape[0]), refs[3][...], _NN,
                                       preferred_element_type=F32)
                o_ref[...] = (o_ref[...].astype(F32) + more).astype(o_ref.dtype)

    in_specs = [pl.BlockSpec((rb, cols), functools.partial(lambda b, j, tbl: (tbl[j, 0] + b, 0), b)) for b in range(3)]
    args = [src, src, src]
    if extra is not None:
        in_specs.append(pl.BlockSpec(extra.shape, lambda j, tbl: (0, 0)))
        args.append(extra)
    grid_spec = pltpu.PrefetchScalarGridSpec(num_scalar_prefetch=1, grid=(n_blocks,), in_specs=in_specs,
                                             out_specs=pl.BlockSpec((rb, cols), lambda j, tbl: (j, 0)))
    return pl.pallas_call(
        body, name=name, grid_spec=grid_spec, out_shape=jax.ShapeDtypeStruct((n_out, cols), src.dtype),
        compiler_params=pltpu.CompilerParams(dimension_semantics=("parallel",), vmem_limit_bytes=VMEM_LIMIT),
    )(table, *args)


def _cols_of(g):
    return jnp.transpose(g, (1, 0, 2)).reshape(g.shape[1], -1)


def _pad_to(a, rows, cols):
    return jnp.pad(a, ((0, rows - a.shape[0]), (0, cols - a.shape[1])))


def kernel(x, norm_mix_g, w_in, ssm_conv_w, ssm_conv_b, ssm_dt_bias, ssm_A_log, ssm_D, ssm_norm_g, sc_conv_w, w_out, norm_ffn_g, w_gate, w_up, w_down, norm_final_g, loss_target, m_norm_mix_g, m_w_in, m_ssm_conv_w, m_ssm_conv_b, m_ssm_dt_bias, m_ssm_A_log, m_ssm_D, m_ssm_norm_g, m_sc_conv_w, m_w_out, m_norm_ffn_g, m_w_gate, m_w_up, m_w_down, m_norm_final_g, v_norm_mix_g, v_w_in, v_ssm_conv_w, v_ssm_conv_b, v_ssm_dt_bias, v_ssm_A_log, v_ssm_D, v_ssm_norm_g, v_sc_conv_w, v_w_out, v_norm_ffn_g, v_w_gate, v_w_up, v_w_down, v_norm_final_g):
    t_len, d = x.shape[1], x.shape[2]
    heads = d // HEADDIM
    r_heads = heads // N_GROUPS
    d_xbc = d + 2 * N_GROUPS * N_STATE
    ff_s = w_down.shape[1]
    ff = ff_s * N_DEV
    off_xbc, off_dt = d, d + d_xbc
    off_cb = off_dt + heads
    d_in = off_cb + 3 * d
    in_s = d_in // N_DEV
    in_p = -(-in_s // (2 * BF16_ROWS)) * (2 * BF16_ROWS)
    w_main = 4 * d + d_xbc
    me = 4 * lax.axis_index("x") + 2 * lax.axis_index("y") + lax.axis_index("c")

    x2 = x[0]
    target = loss_target[0]

    tpose = lambda a: jnp.transpose(a[0])
    win_s = _pad_to(tpose(w_in).astype(BF16), in_p, d)
    wg_s, wu_s = tpose(w_gate).astype(BF16), tpose(w_up).astype(BF16)
    wo_s, wd_s = w_out[0].astype(BF16), w_down[0].astype(BF16)
    small_w = jnp.concatenate([_pad_to(ssm_conv_w[0], K_SSM, d_xbc // N_DEV),
                               _pad_to(sc_conv_w[0], K_SC + 1, d_xbc // N_DEV)], axis=0)

    g1, g2, g3 = norm_mix_g, norm_ffn_g, norm_final_g.reshape(1, d)
    gs = ssm_norm_g
    small = [_pad_to(ssm_dt_bias, 1, LANES), ssm_dt_bias.reshape(heads, 1), _pad_to(ssm_A_log, 1, LANES),
             ssm_A_log.reshape(heads, 1)]
    e01 = (lax.broadcasted_iota(jnp.int32, (LANES, d), 1) // HEADDIM
           == lax.broadcasted_iota(jnp.int32, (LANES, d), 0)).astype(BF16)
    dskip_e = jnp.repeat(ssm_D, HEADDIM, axis=1)
    tr = _tile(t_len, 256, 8)
    tr_ff = _tile(t_len, 128, 8)
    cw = LANES
    slab = lambda col: col // cw

    gin_1, gsm_1 = _gather_chips_relayed([win_s], [small_w], "gather_w_in_chips")
    (n1,), (gin, gsm) = _rows_call(lambda v, g: ((_rms(v, g),), ()), rows=t_len, tr=tr, row_ins=[(x2, d, 0)],
                                   full_ins=[g1], row_outs=[(d, BF16)], acc_outs=[], name="norm_mix",
                                   ride=_gather_sibling([gin_1, gsm_1]))
    in_pieces = []
    for k in range(N_DEV):
        for a, b, dst, shift in ((0, off_dt, 0, 0), (off_dt, off_cb, 1, -off_dt), (off_cb, d_in, 0, -heads)):
            s, e = max(k * in_s, a), min((k + 1) * in_s, b)
            if s < e:
                in_pieces.append((k, s - k * in_s, e - s, dst, s + shift))
    ref_row = lambda t: t if t < off_dt else t + heads
    wtm = _move_rows(gin.reshape(N_DEV * in_p, d),
                     [(ref_row(t) // in_s) * in_p + ref_row(t) % in_s for t in range(w_main)], "place_w_in")
    wtdt = jnp.zeros((LANES, d), BF16)
    for k, r0, n, dst, d0 in in_pieces:
        if dst == 1:
            wtdt = lax.dynamic_update_slice(wtdt, gin[k, r0:r0 + n], (d0, 0))
    cw_ssm = _cols_of(gsm[:, :K_SSM, :])
    cw_sc = _cols_of(gsm[:, K_SSM:K_SSM + K_SC, :d // N_DEV])

    proj, (go_1, gg_1) = _matmul(n1, wtm, tb=True, out_dtype=BF16, name="proj_main",
                                 ride=_gather_chips([wo_s, wg_s]))
    dt_raw, _ = _matmul(n1, wtdt, tb=True, out_dtype=F32, name="proj_dt")
    dt_raw_t = jnp.transpose(dt_raw[:, :heads])
    (xbc,), (go, gg) = _cols_call(_conv_silu_fwd, rows=t_len, cols=d_xbc, cw=cw, col_ins=[(proj, slab(off_xbc))],
                                  par_ins=[(cw_ssm, 0), (ssm_conv_b, 0)], col_outs=[BF16], par_outs=[],
                                  name="ssm_conv", ride=_gather_sibling([go_1, gg_1]))
    dt_c, cs_c, cs_r = _ssd_dt(dt_raw, dt_raw_t, small)
    cs_r3 = cs_r.reshape(N_GROUPS, r_heads, t_len)
    (y_ssd, hprev), (gu_1,) = _ssd_fwd(xbc, dt_c, cs_c, cs_r3, e01, dskip_e, d_ssm=d, r_heads=r_heads,
                                       ride=_gather_chips([wu_s]))

    def gate_norm(y, z, g):
        z = z.astype(F32)
        return _rms(y * (z * _sigmoid(z)), g)

    (y_mix,), (gu,) = _rows_call(lambda y, z, g: ((gate_norm(y, z, g),), ()), rows=t_len, tr=tr,
                                 row_ins=[(y_ssd, d, 0), (proj, d, 0)], full_ins=[gs], row_outs=[(d, BF16, 2 * d)],
                                 acc_outs=[], name="ssm_gate_norm", ride=_gather_sibling([gu_1]))
    wgt, wut, wo = gg.reshape(ff, d), gu.reshape(ff, d), go.reshape(2 * d, d)
    sc0 = slab(d + d_xbc)
    (y_mix,), _ = _cols_call(_shortconv_fwd, rows=t_len, cols=d, cw=cw,
                             col_ins=[(proj, sc0), (proj, sc0 + slab(d)), (proj, sc0 + 2 * slab(d))],
                             par_ins=[(cw_sc, 0)], col_outs=[BF16], par_outs=[], name="shortconv",
                             into=(y_mix, slab(d)))
    h1, _ = _matmul(y_mix, wo, out_dtype=F32, add=x2, name="out_proj")
    (n2,), _ = _rows_call(lambda v, g: ((_rms(v, g),), ()), rows=t_len, tr=tr, row_ins=[(h1, d, 0)], full_ins=[g2],
                          row_outs=[(d, BF16)], acc_outs=[], name="norm_ffn")
    g_ff, (gd_1,) = _matmul(n2, wgt, tb=True, out_dtype=BF16, name="ffn_gate",
                            ride=_gather_chips([wd_s]))
    (u_ff, a_ff), (gd,) = _matmul(n2, wut, tb=True, name="ffn_up", ride=_gather_sibling([gd_1]),
                                  post=(lambda uv, gv: (uv, gv * _sigmoid(gv) * uv), [g_ff], [BF16, BF16]),
                                  tn_max=MM_TILE_N_POST)
    wd = gd.reshape(ff, d)
    h2, _ = _matmul(a_ff, wd, out_dtype=F32, add=h1, name="ffn_down")

    def head(hv, tv, g):
        def f(hh, gg_):
            e = _rms(hh, gg_) - tv
            return (0.5 / d) * jnp.sum(e * e)
        val, (dh, dg) = jax.value_and_grad(f, argnums=(0, 1))(hv, g)
        return (dh, dh), (jnp.full((1, LANES), val, F32), dg)

    (dh2, dh2_b, loss_acc, dg3), _ = _rows_call(head, rows=t_len, tr=tr, row_ins=[(h2, d, 0), (target, d, 0)],
                                                full_ins=[g3], row_outs=[(d, F32), (d, BF16)],
                                                acc_outs=[(1, LANES), (1, d)], name="loss_head")
    loss = lax.psum(loss_acc[0, 0], ("x", "y", "c"))

    def act_bwd(dav, gv, uv):
        s = _sigmoid(gv)
        return dav * uv * (s * (1.0 + gv * (1.0 - s))), dav * gv * s

    (dg_ff, du_ff), _ = _matmul(dh2_b, wd, tb=True, name="d_ffn_gate_up",
                                post=(act_bwd, [g_ff, u_ff], [BF16, BF16]), tn_max=MM_TILE_N_POST)
    dwd, _ = _matmul(a_ff, dh2_b, ta=True, out_dtype=BF16, name="d_w_down")
    dwd8 = dwd.reshape(N_DEV, ff_s, d)
    dn2, (sib_d,) = _matmul(dg_ff, wgt, out_dtype=F32, name="d_norm_ffn_out_gate", ride=_scatter_sibling([dwd8]))
    chip_d = _chip_sum(dwd8, sib_d, name="chip_sum_w_down")
    dn2, (parts_d,) = _matmul(du_ff, wut, out_dtype=F32, add=dn2, name="d_norm_ffn_out_up",
                              ride=_scatter_chips([chip_d]))
    dwg, _ = _matmul(dg_ff, n2, ta=True, out_dtype=BF16, name="d_w_gate")
    dwu, _ = _matmul(du_ff, n2, ta=True, out_dtype=BF16, name="d_w_up")
    dwg8, dwu8 = dwg.reshape(N_DEV, ff_s, d), dwu.reshape(N_DEV, ff_s, d)

    def norm_bwd(v, dn, dres, g):
        _, vjp = jax.vjp(_rms, v, g)
        dv, dg = vjp(dn)
        return (dv + dres,), (dg,)

    def norm_bwd_2(v, dn, dres, g):
        (dv,), acc = norm_bwd(v, dn, dres, g)
        return (dv, dv), acc

    (dh1, dh1_b, dg2), (sib_g, sib_u) = _rows_call(norm_bwd_2, rows=t_len, tr=tr,
                                                   row_ins=[(h1, d, 0), (dn2, d, 0), (dh2, d, 0)], full_ins=[g2],
                                                   row_outs=[(d, F32), (d, BF16)], acc_outs=[(1, d)], name="d_norm_ffn",
                                                   ride=_scatter_sibling([dwg8, dwu8]))
    chip_g = _chip_sum(dwg8, sib_g, name="chip_sum_w_gate")
    chip_u = _chip_sum(dwu8, sib_u, name="chip_sum_w_up")

    dy_mix, _ = _matmul(dh1_b, wo, tb=True, out_dtype=BF16, name="d_y_mix")
    dwo, _ = _matmul(y_mix, dh1_b, ta=True, out_dtype=BF16, name="d_w_out")
    dwo8 = dwo.reshape(N_DEV, 2 * d // N_DEV, d)
    (dgb, dgc, du, dcw_sc), (sib_o,) = _cols_call(
        _shortconv_bwd, rows=t_len, cols=d, cw=cw,
        col_ins=[(proj, sc0), (proj, sc0 + slab(d)), (proj, sc0 + 2 * slab(d)), (dy_mix, slab(d))],
        par_ins=[(cw_sc, 0)], col_outs=[BF16] * 3, par_outs=[K_SC], name="d_shortconv",
        ride=_scatter_sibling([dwo8]))
    chip_o = _chip_sum(dwo8, sib_o, name="chip_sum_w_out")

    def gate_norm_bwd(y, z, dyo, g):
        _, vjp = jax.vjp(gate_norm, y, z.astype(F32), g)
        dy, dz, dg = vjp(dyo.astype(F32))
        return (dy, dz), (dg,)

    (dy_ssd, dproj, dgs), _ = _rows_call(gate_norm_bwd, rows=t_len, tr=tr,
                                         row_ins=[(y_ssd, d, 0), (proj, d, 0), (dy_mix, d, 0)], full_ins=[gs],
                                         row_outs=[(d, F32), (d, BF16, w_main)], acc_outs=[(1, d)],
                                         name="d_ssm_gate_norm")
    (dxs, dbm, dcm, g_dt, g_csc, g_csr3, ddk), (parts_g, parts_u, parts_o) = _ssd_bwd(
        xbc, dt_c, cs_c, cs_r3, e01, dskip_e, hprev, dy_ssd, d_ssm=d, r_heads=r_heads,
        ride=_scatter_chips([chip_g, chip_u, chip_o]))
    ddt_c, ddt_r, dbias_r, dbias_c, dalog_r, dalog_c, ddskip = _ssd_dt(
        dt_raw, dt_raw_t, small, cots=(g_dt, g_csc, g_csr3.reshape(heads, t_len), ddk, e01))
    dcw_parts, dcb_parts, col0 = [], [], 0
    for tag, dpart in (("x", dxs), ("b", dbm), ("c", dcm)):
        (dproj, dcw_p, dcb_p), _ = _cols_call(
            _conv_silu_bwd, rows=t_len, cols=dpart.shape[1], cw=cw,
            col_ins=[(proj, slab(off_xbc + col0)), (dpart, 0)], par_ins=[(cw_ssm, slab(col0)), (ssm_conv_b, slab(col0))],
            col_outs=[BF16], par_outs=[K_SSM, 1], name="d_ssm_conv_" + tag, into=(dproj, slab(off_xbc + col0)))
        dcw_parts.append(dcw_p)
        dcb_parts.append(dcb_p)
        col0 += dpart.shape[1]
    dcw_ssm, dcb_ssm = jnp.concatenate(dcw_parts, axis=1), jnp.concatenate(dcb_parts, axis=1)
    for i, part in enumerate((dgb, dgc, du)):
        dproj = lax.dynamic_update_slice(dproj, part, (0, d + d_xbc + i * d))
    ddt = ddt_c + _pad_to(jnp.transpose(ddt_r), t_len, LANES)
    dwm, _ = _matmul(dproj, n1, ta=True, out_dtype=BF16, name="d_w_in_main")
    dwdt, _ = _matmul(ddt, n1, ta=True, out_dtype=BF16, name="d_w_in_dt")
    own_ref = [k * in_s + i if i < in_s else -1 for k in range(N_DEV) for i in range(in_p)]
    dwin8 = _move_rows(
        dwm, [-1 if g < 0 or off_dt <= g < off_cb else (g if g < off_dt else g - heads) for g in own_ref],
        "place_d_w_in", extra=dwdt, extra_row=[g - off_dt if off_dt <= g < off_cb else -1 for g in own_ref],
    ).reshape(N_DEV, in_p, d)
    dn1, (sib_in,) = _matmul(ddt, wtdt, out_dtype=F32, name="d_norm_mix_out_dt", ride=_scatter_sibling([dwin8]))
    chip_in = _chip_sum(dwin8, sib_in, name="chip_sum_w_in")
    cut = int(in_p * W_IN_SCATTER_SPLIT) // BF16_ROWS * BF16_ROWS
    dn1, (parts_in,) = _matmul(dproj, wtm, out_dtype=F32, add=dn1, name="d_norm_mix_out",
                               ride=_scatter_chips([chip_in], rows=(0, cut)))
    (dx, dg1), (parts_in,) = _rows_call(norm_bwd, rows=t_len, tr=tr, row_ins=[(x2, d, 0), (dn1, d, 0), (dh1, d, 0)],
                                        full_ins=[g1], row_outs=[(d, F32)], acc_outs=[(1, d)], name="d_norm_mix",
                                        ride=_scatter_chips([chip_in], rows=(cut, in_p - cut), into=[parts_in]))

    wide = d_xbc
    rows_small = [dg1, dcb_ssm, dbias_r + _pad_to(dbias_c.reshape(1, heads), 1, LANES),
                  dalog_r + _pad_to(dalog_c.reshape(1, heads), 1, LANES), ddskip, dgs, dg2, dg3]
    packed = jnp.concatenate([_pad_to(r, 1, wide) for r in rows_small]
                             + [dcw_ssm, _pad_to(dcw_sc, K_SC, wide), jnp.zeros((1, wide), F32)], axis=0)
    (p_small,) = _comm(_gather_all([packed]), "gather_small_grads")

    conv_lo = me * (d_xbc // N_DEV)
    sc_lo = me * (d // N_DEV)

    def pack_state(vals):
        (nm, cb, dtb, al, dk, sg, nf, nfin, cws, scs) = vals
        rows = [_pad_to(a.reshape(1, -1), 1, wide) for a in (nm, cb, dtb, al, dk, sg, nf, nfin)]
        cws_full = lax.dynamic_update_slice(jnp.zeros((K_SSM, wide), F32), cws[0], (0, conv_lo))
        scs_full = lax.dynamic_update_slice(jnp.zeros((K_SC, wide), F32), scs[0], (0, sc_lo))
        return jnp.concatenate(rows + [cws_full, scs_full, jnp.zeros((1, wide), F32)], axis=0)

    w_small = pack_state((norm_mix_g, ssm_conv_b, ssm_dt_bias, ssm_A_log, ssm_D, ssm_norm_g, norm_ffn_g, norm_final_g,
                          ssm_conv_w, sc_conv_w))
    m_small = pack_state((m_norm_mix_g, m_ssm_conv_b, m_ssm_dt_bias, m_ssm_A_log, m_ssm_D, m_ssm_norm_g, m_norm_ffn_g,
                          m_norm_final_g, m_ssm_conv_w, m_sc_conv_w))
    v_small = pack_state((v_norm_mix_g, v_ssm_conv_b, v_ssm_dt_bias, v_ssm_A_log, v_ssm_D, v_ssm_norm_g, v_norm_ffn_g,
                          v_norm_final_g, v_ssm_conv_w, v_sc_conv_w))

    tin = lambda a: _pad_to(tpose(a), in_p, d)
    tin_back = lambda a: jnp.transpose(a[:in_s])[None]
    t_back = lambda a: jnp.transpose(a)[None]
    upd = {
        "w_in": [tin_back(o) for o in _reduce_adamw(parts_in, tin(w_in), tin(m_w_in), tin(v_w_in), name="adamw_w_in")],
        "w_out": [o[None] for o in _reduce_adamw(parts_o, w_out[0], m_w_out[0], v_w_out[0], name="adamw_w_out")],
        "w_gate": [t_back(o) for o in _reduce_adamw(parts_g, tpose(w_gate), tpose(m_w_gate), tpose(v_w_gate),
                                                    name="adamw_w_gate")],
        "w_up": [t_back(o) for o in _reduce_adamw(parts_u, tpose(w_up), tpose(m_w_up), tpose(v_w_up),
                                                  name="adamw_w_up")],
        "w_down": [o[None] for o in _reduce_adamw(parts_d, w_down[0], m_w_down[0], v_w_down[0], name="adamw_w_down")],
    }
    small_upd = _reduce_adamw(p_small, w_small, m_small, v_small, name="adamw_small")

    def unpack(packed_out):
        vec = lambda i, n, shape: packed_out[i, :n].reshape(shape)
        return {
            "norm_mix_g": vec(0, d, (1, d)), "ssm_conv_b": vec(1, d_xbc, (1, d_xbc)),
            "ssm_dt_bias": vec(2, heads, (1, heads)), "ssm_A_log": vec(3, heads, (1, heads)),
            "ssm_D": vec(4, heads, (1, heads)), "ssm_norm_g": vec(5, d, (1, d)), "norm_ffn_g": vec(6, d, (1, d)),
            "norm_final_g": vec(7, d, (d,)),
            "ssm_conv_w": lax.dynamic_slice(packed_out[8:8 + K_SSM], (0, conv_lo), (K_SSM, d_xbc // N_DEV))[None],
            "sc_conv_w": lax.dynamic_slice(packed_out[8 + K_SSM:8 + K_SSM + K_SC], (0, sc_lo), (K_SC, d // N_DEV))[None],
        }

    names = ["norm_mix_g", "w_in", "ssm_conv_w", "ssm_conv_b", "ssm_dt_bias", "ssm_A_log", "ssm_D", "ssm_norm_g",
             "sc_conv_w", "w_out", "norm_ffn_g", "w_gate", "w_up", "w_down", "norm_final_g"]
    outs = []
    for kind in range(4):
        small_k = unpack(small_upd[kind])
        for nm in names:
            outs.append(upd[nm][kind] if nm in upd else small_k[nm])
    return (loss, dx[None], *outs)
```

```python
import collections
import functools

import jax
import jax.numpy as jnp
from jax import lax
from jax.experimental import pallas as pl
from jax.experimental.pallas import tpu as pltpu

F32 = jnp.float32
BF16 = jnp.bfloat16

N_DEV = 8
N_CHIPS = 4
HEADDIM = 64
N_GROUPS = 8
N_STATE = 128
CHUNK = 128
K_SSM = 4
K_SC = 3
EPS = 1e-5
LANES = 128
BF16_ROWS = 16
MM_TILE_MN = 1408
MM_TILE_K = 2816
W_IN_SCATTER_SPLIT = 6 / 7
MM_TILE_N_POST = 704
SSD_CHUNKS_PER_STEP = 4
SSD_GROUPS_PER_STEP = 2
ROW_BLOCK = 256
V7X_VMEM_BYTES = 64 * 1024 * 1024
VMEM_LIMIT = (V7X_VMEM_BYTES * 3) // 4

ADAM_LR = 0.001
ADAM_B1 = 0.9
ADAM_B2 = 0.999
ADAM_EPS = 1e-08
ADAM_WD = 0.01
ADAM_STEP = 10


def _tile(n, pref, align):
    t = min(pref, n)
    t -= t % align
    while t >= align:
        if n % t == 0:
            return t
        t -= align
    return n


_Ride = collections.namedtuple("_Ride", ["ins", "out_shapes", "aliases", "nsem", "plan"])
_ANY = pl.BlockSpec(memory_space=pl.ANY)


def _coords():
    return lax.axis_index("x"), lax.axis_index("y"), lax.axis_index("c")


def _other_chips(x, y):
    return ((1 - x, y), (x, 1 - y), (1 - x, 1 - y))


def _remote(src, dst, send, recv, k, dev):
    return functools.partial(pltpu.make_async_remote_copy, src_ref=src, dst_ref=dst, send_sem=send.at[k],
                             recv_sem=recv.at[k], device_id=dev, device_id_type=pl.DeviceIdType.MESH)


def _local(src, dst, sem):
    return functools.partial(pltpu.make_async_copy, src, dst, sem)


def _start_all(plan):
    for kind, make in plan:
        if kind != "arrival":
            make().start()


def _wait_all(plan):
    for kind, make in plan:
        if kind == "local":
            make().wait()
        elif kind == "out":
            make().wait_send()
        else:
            make().wait_recv()


def _gather_chips(srcs):
    def plan(ins, outs, send, recv, base):
        x, y, c = _coords()
        me = 4 * x + 2 * y + c
        d = []
        for a, (src, dst) in enumerate(zip(ins, outs)):
            k = base + 4 * a
            d.append(("local", _local(src, dst.at[me], send.at[k + 3])))
            for j, (px, py) in enumerate(_other_chips(x, y)):
                d.append(("out", _remote(src, dst.at[me], send, recv, k + j, (px, py, c))))
                d.append(("arrival", _remote(src, dst.at[4 * px + 2 * py + c], send, recv, k + j, (px, py, c))))
        return d
    shapes = [jax.ShapeDtypeStruct((N_DEV,) + s.shape, s.dtype) for s in srcs]
    return _Ride(list(srcs), shapes, {}, 4 * len(srcs), plan)


def _gather_sibling(bufs):
    def plan(ins, outs, send, recv, base):
        x, y, c = _coords()
        d = []
        for a, buf in enumerate(outs):
            for q in range(N_CHIPS):
                k = base + 4 * a + q
                d.append(("out", _remote(buf.at[2 * q + c], buf.at[2 * q + c], send, recv, k, (x, y, 1 - c))))
                d.append(("arrival", _remote(buf.at[2 * q + c], buf.at[2 * q + 1 - c], send, recv, k, (x, y, 1 - c))))
        return d
    shapes = [jax.ShapeDtypeStruct(b.shape, b.dtype) for b in bufs]
    return _Ride(list(bufs), shapes, {i: i for i in range(len(bufs))}, 4 * len(bufs), plan)


def _scatter_sibling(srcs):
    def plan(ins, outs, send, recv, base):
        x, y, c = _coords()
        d = []
        for a, (src, sib) in enumerate(zip(ins, outs)):
            for q in range(N_CHIPS):
                k = base + 4 * a + q
                d.append(("out", _remote(src.at[2 * q + 1 - c], sib.at[q], send, recv, k, (x, y, 1 - c))))
                d.append(("arrival", _remote(src.at[2 * q + 1 - c], sib.at[q], send, recv, k, (x, y, 1 - c))))
        return d
    shapes = [jax.ShapeDtypeStruct((N_CHIPS,) + s.shape[1:], s.dtype) for s in srcs]
    return _Ride(list(srcs), shapes, {}, 4 * len(srcs), plan)


def _scatter_chips(chips, rows=None, into=None):
    n = len(chips)

    def plan(ins, outs, send, recv, base):
        x, y, c = _coords()
        mine = 2 * x + y
        cut = (lambda ref: ref) if rows is None else (lambda ref: ref.at[pl.ds(rows[0], rows[1])])
        d = []
        for a, (chip, parts) in enumerate(zip(ins[:n], outs)):
            k = base + 4 * a
            d.append(("local", _local(cut(chip.at[mine]), cut(parts.at[mine]), send.at[k + 3])))
            for j, (px, py) in enumerate(_other_chips(x, y)):
                q = 2 * px + py
                d.append(("out", _remote(cut(chip.at[q]), cut(parts.at[mine]), send, recv, k + j, (px, py, c))))
                d.append(("arrival", _remote(cut(chip.at[q]), cut(parts.at[q]), send, recv, k + j, (px, py, c))))
        return d
    shapes = [jax.ShapeDtypeStruct(s.shape, s.dtype) for s in chips]
    if into is None:
        return _Ride(list(chips), shapes, {}, 4 * n, plan)
    return _Ride(list(chips) + list(into), shapes, {n + a: a for a in range(n)}, 4 * n, plan)


def _gather_all(srcs):
    def plan(ins, outs, send, recv, base):
        x, y, c = _coords()
        me = 4 * x + 2 * y + c
        d = []
        for a, (src, dst) in enumerate(zip(ins, outs)):
            k = base + N_DEV * a
            d.append(("local", _local(src, dst.at[me], send.at[k])))
            for j in range(1, N_DEV):
                px = 1 - x if (j >> 2) & 1 else x
                py = 1 - y if (j >> 1) & 1 else y
                pc = 1 - c if j & 1 else c
                d.append(("out", _remote(src, dst.at[me], send, recv, k + j, (px, py, pc))))
                d.append(("arrival", _remote(src, dst.at[4 * px + 2 * py + pc], send, recv, k + j, (px, py, pc))))
        return d
    shapes = [jax.ShapeDtypeStruct((N_DEV,) + s.shape, s.dtype) for s in srcs]
    return _Ride(list(srcs), shapes, {}, N_DEV * len(srcs), plan)


def _merge(*rides):
    ins, outs, aliases, parts, nsem = [], [], {}, [], 0
    for r in rides:
        parts.append((len(ins), len(outs), nsem, r))
        aliases.update({len(ins) + i: len(outs) + j for i, j in r.aliases.items()})
        ins += r.ins
        outs += r.out_shapes
        nsem += r.nsem

    def plan(i, o, send, recv, base):
        d = []
        for i0, o0, s0, r in parts:
            d += r.plan(i[i0:i0 + len(r.ins)], o[o0:o0 + len(r.out_shapes)], send, recv, base + s0)
        return d
    return _Ride(ins, outs, aliases, nsem, plan)


def _comm(ride, name):
    n_in, n_out = len(ride.ins), len(ride.out_shapes)

    def body(*refs):
        plan = ride.plan(refs[:n_in], refs[n_in:n_in + n_out], refs[-2], refs[-1], 0)
        _start_all(plan)
        _wait_all(plan)

    return pl.pallas_call(
        body, name=name, in_specs=[_ANY] * n_in, out_specs=[_ANY] * n_out, out_shape=ride.out_shapes,
        scratch_shapes=[pltpu.SemaphoreType.DMA((ride.nsem,)), pltpu.SemaphoreType.DMA((ride.nsem,))],
        input_output_aliases=dict(ride.aliases),
        compiler_params=pltpu.CompilerParams(has_side_effects=True),
    )(*ride.ins)


def _gather_chips_relayed(big, small, name):
    srcs = list(big) + list(small)
    n, nsem = len(srcs), 5 * len(srcs)

    def body(*refs):
        ins, outs, send, recv = refs[:n], refs[n:2 * n], refs[-2], refs[-1]
        x, y, c = _coords()
        slot = lambda dev: 4 * dev[0] + 2 * dev[1] + dev[2]
        me, nbr_x, nbr_y, diag = (x, y, c), (1 - x, y, c), (x, 1 - y, c), (1 - x, 1 - y, c)
        own, sends = [], []
        for a, (src, dst) in enumerate(zip(ins, outs)):
            k = 5 * a
            own.append(_local(src, dst.at[slot(me)], send.at[k + 4])())
            sends.append(_remote(src, dst.at[slot(me)], send, recv, k, nbr_x)())
            sends.append(_remote(src, dst.at[slot(me)], send, recv, k + 1, nbr_y)())
            if a >= len(big):
                sends.append(_remote(src, dst.at[slot(me)], send, recv, k + 2, diag)())
        for s in own + sends:
            s.start()
        for a, (src, dst) in enumerate(zip(ins, outs)):
            k = 5 * a
            _remote(src, dst.at[slot(nbr_x)], send, recv, k, nbr_x)().wait_recv()
            if a < len(big):
                half = src.shape[0] // 2
                part = dst.at[slot(nbr_x)].at[pl.ds(0, half)]
                fwd = _remote(part, part, send, recv, k + 2, nbr_y)()
                fwd.start()
                sends.append(fwd)
            _remote(src, dst.at[slot(nbr_y)], send, recv, k + 1, nbr_y)().wait_recv()
            if a < len(big):
                part = dst.at[slot(nbr_y)].at[pl.ds(half, src.shape[0] - half)]
                fwd = _remote(part, part, send, recv, k + 3, nbr_x)()
                fwd.start()
                sends.append(fwd)
        for a, (src, dst) in enumerate(zip(ins, outs)):
            k = 5 * a
            if a < len(big):
                half = src.shape[0] // 2
                lo = dst.at[slot(diag)].at[pl.ds(0, half)]
                hi = dst.at[slot(diag)].at[pl.ds(half, src.shape[0] - half)]
                _remote(lo, lo, send, recv, k + 2, nbr_y)().wait_recv()
                _remote(hi, hi, send, recv, k + 3, nbr_x)().wait_recv()
            else:
                _remote(src, dst.at[slot(diag)], send, recv, k + 2, diag)().wait_recv()
        for lc in own:
            lc.wait()
        for s in sends:
            s.wait_send()

    return pl.pallas_call(
        body, name=name, in_specs=[_ANY] * n, out_specs=[_ANY] * n,
        out_shape=[jax.ShapeDtypeStruct((N_DEV,) + s.shape, s.dtype) for s in srcs],
        scratch_shapes=[pltpu.SemaphoreType.DMA((nsem,)), pltpu.SemaphoreType.DMA((nsem,))],
        compiler_params=pltpu.CompilerParams(has_side_effects=True),
    )(*srcs)


def _call(body, *, name, grid, in_specs, out_specs, out_shape, args, sem, scratch=(), ride=None, base=None):
    params = pltpu.CompilerParams(dimension_semantics=sem, vmem_limit_bytes=VMEM_LIMIT)
    own_aliases = {}
    if base is not None:
        inner, n_host = body, len(args)
        body = lambda *refs: inner(*refs[:n_host], *refs[n_host + 1:])
        own_aliases[n_host] = base[1]
        args, in_specs = tuple(args) + (base[0],), list(in_specs) + [_ANY]
    if ride is None:
        res = pl.pallas_call(body, name=name, grid=grid, in_specs=in_specs, out_specs=out_specs,
                             out_shape=out_shape, scratch_shapes=list(scratch), input_output_aliases=own_aliases,
                             compiler_params=params)(*args)
        return list(res), []
    n_in, n_out, n_scr = len(args), len(out_shape), len(scratch)
    r_in, r_out = len(ride.ins), len(ride.out_shapes)

    def hosted(*refs):
        h_in, rin = refs[:n_in], refs[n_in:n_in + r_in]
        o0 = n_in + r_in
        h_out, rout = refs[o0:o0 + n_out], refs[o0 + n_out:o0 + n_out + r_out]
        s0 = o0 + n_out + r_out
        h_scr, send, recv = refs[s0:s0 + n_scr], refs[s0 + n_scr], refs[s0 + n_scr + 1]
        ids = [pl.program_id(i) for i in range(len(grid))]
        first = functools.reduce(lambda p, q: p & q, [i == 0 for i in ids])
        last = functools.reduce(lambda p, q: p & q, [i == n - 1 for i, n in zip(ids, grid)])

        @pl.when(first)
        def _():
            _start_all(ride.plan(rin, rout, send, recv, 0))

        body(*h_in, *h_out, *h_scr)

        @pl.when(last)
        def _():
            _wait_all(ride.plan(rin, rout, send, recv, 0))

    res = pl.pallas_call(
        hosted, name=name, grid=grid, in_specs=list(in_specs) + [_ANY] * r_in,
        out_specs=list(out_specs) + [_ANY] * r_out, out_shape=list(out_shape) + list(ride.out_shapes),
        scratch_shapes=list(scratch) + [pltpu.SemaphoreType.DMA((ride.nsem,)), pltpu.SemaphoreType.DMA((ride.nsem,))],
        input_output_aliases={**own_aliases, **{n_in + i: n_out + j for i, j in ride.aliases.items()}},
        compiler_params=params,
    )(*args, *ride.ins)
    return list(res[:n_out]), list(res[n_out:])


def _matmul(a, b, *, ta=False, tb=False, out_dtype=BF16, add=None, post=None, name, ride=None, tn_max=MM_TILE_MN):
    m = a.shape[1] if ta else a.shape[0]
    k = a.shape[0] if ta else a.shape[1]
    n = b.shape[0] if tb else b.shape[1]
    assert k == (b.shape[1] if tb else b.shape[0])
    tm, tn, tk = _tile(m, MM_TILE_MN, LANES), _tile(n, tn_max, LANES), _tile(k, MM_TILE_K, LANES)
    nk = k // tk
    dims = (((0 if ta else 1,), (1 if tb else 0,)), ((), ()))
    single = post is None
    if add is not None:
        post = (lambda r, t: (r + t,), [add], [out_dtype])
    elif post is None:
        post = (lambda r: (r,), [], [out_dtype])
    post_fn, extras, out_dtypes = post
    n_ex, n_o = len(extras), len(out_dtypes)

    def body(*refs):
        a_ref, b_ref = refs[:2]
        ex_refs, o_refs = refs[2:2 + n_ex], refs[2 + n_ex:2 + n_ex + n_o]

        def finish(r):
            for o_ref, v in zip(o_refs, post_fn(r, *[e[...].astype(F32) for e in ex_refs])):
                o_ref[...] = v.astype(o_ref.dtype)

        part = lax.dot_general(a_ref[...].astype(BF16), b_ref[...].astype(BF16), dims, preferred_element_type=F32)
        if nk == 1:
            finish(part)
            return
        acc = refs[-1]
        kk = pl.program_id(2)

        @pl.when(kk == 0)
        def _():
            acc[...] = part

        @pl.when((kk > 0) & (kk < nk - 1))
        def _():
            acc[...] += part

        @pl.when(kk == nk - 1)
        def _():
            finish(acc[...] + part)

    a_spec = (pl.BlockSpec((tk, tm), lambda i, j, kk: (kk, i)) if ta
              else pl.BlockSpec((tm, tk), lambda i, j, kk: (i, kk)))
    b_spec = (pl.BlockSpec((tn, tk), lambda i, j, kk: (j, kk)) if tb
              else pl.BlockSpec((tk, tn), lambda i, j, kk: (kk, j)))
    o_spec = pl.BlockSpec((tm, tn), lambda i, j, kk: (i, j))
    outs, rides = _call(
        body, name=name, grid=(m // tm, n // tn, nk),
        in_specs=[a_spec, b_spec] + [o_spec] * n_ex, out_specs=[o_spec] * n_o,
        out_shape=[jax.ShapeDtypeStruct((m, n), dt) for dt in out_dtypes], args=(a, b, *extras),
        scratch=[pltpu.VMEM((tm, tn), F32)] if nk > 1 else [], sem=("parallel", "parallel", "arbitrary"), ride=ride)
    return (outs[0] if single else outs), rides


def _rows_call(fn, *, rows, tr, row_ins, full_ins, row_outs, acc_outs, name, ride=None):
    nr, nf, no, na = len(row_ins), len(full_ins), len(row_outs), len(acc_outs)

    def body(*refs):
        vals = [r[...] for r in refs[:nr + nf]]
        outs, accs = fn(*vals)
        for r, v in zip(refs[nr + nf:nr + nf + no], outs):
            r[...] = v.astype(r.dtype)
        if na:
            @pl.when(pl.program_id(0) == 0)
            def _():
                for r in refs[nr + nf + no:]:
                    r[...] = jnp.zeros_like(r)
            for r, v in zip(refs[nr + nf + no:], accs):
                r[...] += v

    in_specs = [pl.BlockSpec((tr, w), functools.partial(lambda cb, i: (i, cb), cb)) for _, w, cb in row_ins]
    in_specs += [pl.BlockSpec(f.shape, lambda i: (0, 0)) for f in full_ins]
    out_specs = [pl.BlockSpec((tr, o[0]), lambda i: (i, 0)) for o in row_outs]
    out_specs += [pl.BlockSpec(s, lambda i: (0, 0)) for s in acc_outs]
    out_shape = [jax.ShapeDtypeStruct((rows, o[-1] if len(o) == 3 else o[0]), o[1]) for o in row_outs]
    out_shape += [jax.ShapeDtypeStruct(s, F32) for s in acc_outs]
    return _call(body, name=name, grid=(rows // tr,), in_specs=in_specs, out_specs=out_specs, out_shape=out_shape,
                 args=tuple(a for a, _, _ in row_ins) + tuple(full_ins), sem=("arbitrary",), ride=ride)


def _cols_call(fn, *, rows, cols, cw, col_ins, par_ins, col_outs, par_outs, name, ride=None, into=None):
    nc, npar = len(col_ins), len(par_ins)

    def body(*refs):
        vals = [r[...] for r in refs[:nc + npar]]
        outs, pouts = fn(*vals)
        for r, v in zip(refs[nc + npar:], tuple(outs) + tuple(pouts)):
            r[...] = v.astype(r.dtype)

    in_specs = [pl.BlockSpec((rows, cw), functools.partial(lambda off, j: (0, off + j), off)) for _, off in col_ins]
    in_specs += [pl.BlockSpec((p.shape[0], cw), functools.partial(lambda off, j: (0, off + j), off))
                 for p, off in par_ins]
    out_specs = [pl.BlockSpec((rows, cw), lambda j: (0, j)) for _ in col_outs]
    out_specs += [pl.BlockSpec((k, cw), lambda j: (0, j)) for k in par_outs]
    out_shape = [jax.ShapeDtypeStruct((rows, cols), dt) for dt in col_outs]
    out_shape += [jax.ShapeDtypeStruct((k, cols), F32) for k in par_outs]
    if into is not None:
        out_specs[0] = pl.BlockSpec((rows, cw), lambda j: (0, into[1] + j))
        out_shape[0] = jax.ShapeDtypeStruct(into[0].shape, into[0].dtype)
    return _call(body, name=name, grid=(cols // cw,), in_specs=in_specs, out_specs=out_specs, out_shape=out_shape,
                 args=tuple(a for a, _ in col_ins) + tuple(p for p, _ in par_ins), sem=("arbitrary",), ride=ride,
                 base=None if into is None else (into[0], 0))


def _sigmoid(v):
    return 1.0 / (1.0 + jnp.exp(-v))


def _softplus(v):
    return jnp.maximum(v, 0.0) + jnp.log(1.0 + jnp.exp(-jnp.abs(v)))


def _rms(v, g):
    return v * lax.rsqrt(jnp.mean(v * v, axis=-1, keepdims=True) + EPS) * g


def _shift_down(v, s, row):
    return jnp.where(row >= s, pltpu.roll(v, s, 0), 0.0)


def _shift_up(v, s, row):
    n = v.shape[0]
    return jnp.where(row < n - s, pltpu.roll(v, n - s, 0), 0.0)


def _causal_conv(u, w, row):
    k_taps = w.shape[0]
    acc = u * w[k_taps - 1:k_taps, :]
    for k in range(k_taps - 1):
        acc = acc + _shift_down(u, k_taps - 1 - k, row) * w[k:k + 1, :]
    return acc


def _causal_conv_bwd(u, dy, w, row):
    k_taps = w.shape[0]
    tap = lax.broadcasted_iota(jnp.int32, w.shape, 0)
    du = dy * w[k_taps - 1:k_taps, :]
    dw = jnp.where(tap == k_taps - 1, jnp.sum(dy * u, axis=0, keepdims=True), 0.0)
    for k in range(k_taps - 1):
        s = k_taps - 1 - k
        du = du + _shift_up(dy, s, row) * w[k:k + 1, :]
        dw = dw + jnp.where(tap == k, jnp.sum(dy * _shift_down(u, s, row), axis=0, keepdims=True), 0.0)
    return du, dw


def _conv_silu_fwd(u, w, b):
    u = u.astype(F32)
    row = lax.broadcasted_iota(jnp.int32, u.shape, 0)
    pre = _causal_conv(u, w, row) + b
    return (pre * _sigmoid(pre),), ()


def _conv_silu_bwd(u, dy, w, b):
    u = u.astype(F32)
    dy = dy.astype(F32)
    row = lax.broadcasted_iota(jnp.int32, u.shape, 0)
    pre = _causal_conv(u, w, row) + b
    s = _sigmoid(pre)
    dpre = dy * (s * (1.0 + pre * (1.0 - s)))
    du, dw = _causal_conv_bwd(u, dpre, w, row)
    return (du,), (dw, jnp.sum(dpre, axis=0, keepdims=True))


def _shortconv_fwd(gb, gc, u, w):
    gb, gc, u = gb.astype(F32), gc.astype(F32), u.astype(F32)
    row = lax.broadcasted_iota(jnp.int32, u.shape, 0)
    return (gb * _causal_conv(gc * u, w, row),), ()


def _shortconv_bwd(gb, gc, u, dy, w):
    gb, gc, u, dy = gb.astype(F32), gc.astype(F32), u.astype(F32), dy.astype(F32)
    row = lax.broadcasted_iota(jnp.int32, u.shape, 0)
    v = gc * u
    dgb = dy * _causal_conv(v, w, row)
    dv, dw = _causal_conv_bwd(v, dy * gb, w, row)
    return (dgb, dv * u, dv * gc), (dw,)


def _split3(v):
    hi = v.astype(BF16)
    r1 = v - hi.astype(F32)
    mid = r1.astype(BF16)
    lo = (r1 - mid.astype(F32)).astype(BF16)
    return hi, mid, lo


def _exact_dot(v, m01, dims, v_is_lhs):
    def one(p):
        return (lax.dot_general(p, m01, dims, preferred_element_type=F32) if v_is_lhs
                else lax.dot_general(m01, p, dims, preferred_element_type=F32))
    hi, mid, lo = _split3(v)
    return (one(lo) + one(mid)) + one(hi)


_NN = (((1,), (0,)), ((), ()))
_NT = (((1,), (1,)), ((), ()))
_TN = (((0,), (0,)), ((), ()))


@jax.custom_vjp
def _cumsum_rows(tril, v):
    return _exact_dot(v, tril, _NN, False)


def _cumsum_rows_fwd(tril, v):
    return _cumsum_rows(tril, v), tril


def _cumsum_rows_bwd(tril, ct):
    return None, _exact_dot(ct, tril, _TN, False)


_cumsum_rows.defvjp(_cumsum_rows_fwd, _cumsum_rows_bwd)


@jax.custom_vjp
def _cumsum_lanes(tril, v):
    return _exact_dot(v, tril, _NT, True)


def _cumsum_lanes_fwd(tril, v):
    return _cumsum_lanes(tril, v), tril


def _cumsum_lanes_bwd(tril, ct):
    return None, _exact_dot(ct, tril, _NN, True)


_cumsum_lanes.defvjp(_cumsum_lanes_fwd, _cumsum_lanes_bwd)


@jax.custom_vjp
def _expand(e01, v):
    return _exact_dot(v, e01, _NN, True)


def _expand_fwd(e01, v):
    return _expand(e01, v), e01


def _expand_bwd(e01, ct):
    return None, _exact_dot(ct, e01, _NT, True)


_expand.defvjp(_expand_fwd, _expand_bwd)


def _causal_mask(n):
    li = lax.broadcasted_iota(jnp.int32, (n, n), 0)
    si = lax.broadcasted_iota(jnp.int32, (n, n), 1)
    return si <= li


def _dt_prep(dtc, dtr, bias_r, bias_c, alog_r, alog_c):
    dt_c = _softplus(dtc + bias_r)
    dt_r = _softplus(dtr + bias_c)
    tril = jnp.where(_causal_mask(dtc.shape[0]), 1.0, 0.0).astype(BF16)
    cs_c = _cumsum_rows(tril, dt_c * (-jnp.exp(alog_r)))
    cs_r = _cumsum_lanes(tril, dt_r * (-jnp.exp(alog_c)))
    return dt_c, cs_c, cs_r


def _ssd_chunk(r_heads, xs, bg, cg, dt_c, cs_c, cs_rg, e01, dskip_e, hp):
    l_len, rp = xs.shape
    p = rp // r_heads
    causal = _causal_mask(l_len)
    lane_head = lax.broadcasted_iota(jnp.int32, (1, rp), 1) // p
    dt_e = _expand(e01, dt_c)
    cs_e = _expand(e01, cs_c)
    cl_e = cs_e[l_len - 1:l_len, :]
    x = xs * dt_e
    bgb, cgb = bg.astype(BF16), cg.astype(BF16)
    cb = lax.dot_general(cgb, bgb, _NT, preferred_element_type=F32)
    ms, xm = [], []
    for r in range(r_heads):
        seg = cs_e[:, r * p:r * p + 1] - cs_rg[r:r + 1, :]
        decay = jnp.exp(jnp.where(causal, seg, -1e30))
        ms.append((cb * decay).astype(BF16))
        xm.append(jnp.where(lane_head == r, x, 0.0).astype(BF16))
    y_diag = lax.dot_general(jnp.concatenate(ms, axis=1), jnp.concatenate(xm, axis=0), _NN,
                             preferred_element_type=F32)
    y_off = lax.dot_general(cgb, hp.astype(BF16), _NN, preferred_element_type=F32) * jnp.exp(cs_e)
    xd = (x * jnp.exp(cl_e - cs_e)).astype(BF16)
    states = lax.dot_general(bgb, xd, _TN, preferred_element_type=F32)
    h_next = hp * jnp.exp(cl_e) + states
    y = y_diag + y_off + dskip_e * xs
    return y, h_next


def _ssd_dt(dtc, dtr, small, cots=None):
    t_len, heads = dtc.shape[0], dtr.shape[0]
    nc = t_len // CHUNK
    col = pl.BlockSpec((CHUNK, LANES), lambda c: (c, 0))
    row = pl.BlockSpec((heads, CHUNK), lambda c: (0, c))
    full = [pl.BlockSpec(s.shape, lambda c: (0, 0)) for s in small]
    shapes = [jax.ShapeDtypeStruct((t_len, LANES), F32), jax.ShapeDtypeStruct((t_len, LANES), F32),
              jax.ShapeDtypeStruct((heads, t_len), F32)]
    if cots is None:
        def body(dtc_ref, dtr_ref, br, bc, ar, ac, dt_ref, csc_ref, csr_ref):
            dt_ref[...], csc_ref[...], csr_ref[...] = _dt_prep(dtc_ref[...], dtr_ref[...], br[...], bc[...],
                                                                ar[...], ac[...])
        return _call(body, name="ssd_dt", grid=(nc,), in_specs=[col, row] + full, out_specs=[col, col, row],
                     out_shape=shapes, args=(dtc, dtr, *small), sem=("parallel",))[0]

    g_dt, g_csc, g_csr, ddk, e01 = cots

    def body(dtc_ref, dtr_ref, br, bc, ar, ac, g_dt_ref, g_csc_ref, g_csr_ref, ddk_ref, e_ref,
             ddtc_ref, ddtr_ref, *dsmall):
        _, vjp = jax.vjp(_dt_prep, dtc_ref[...], dtr_ref[...], br[...], bc[...], ar[...], ac[...])
        grads = vjp((g_dt_ref[...], g_csc_ref[...], g_csr_ref[...]))
        ddtc_ref[...], ddtr_ref[...] = grads[0], grads[1]
        ddk8 = jnp.broadcast_to(ddk_ref[...], (8, ddk_ref.shape[1]))
        dskip = _exact_dot(ddk8, e_ref[...], _NT, True)[0:1, :]

        @pl.when(pl.program_id(0) == 0)
        def _():
            for r in dsmall:
                r[...] = jnp.zeros_like(r)

        for r, gr in zip(dsmall, tuple(grads[2:]) + (dskip,)):
            r[...] += gr

    acc = list(small) + [small[0]]
    return _call(body, name="d_ssd_dt", grid=(nc,),
                 in_specs=[col, row] + full + [col, col, row, pl.BlockSpec((None, 1, e01.shape[1]), lambda c: (c, 0, 0)),
                                               pl.BlockSpec(e01.shape, lambda c: (0, 0))],
                 out_specs=[col, row] + [pl.BlockSpec(s.shape, lambda c: (0, 0)) for s in acc],
                 out_shape=[shapes[0], shapes[2]] + [jax.ShapeDtypeStruct(s.shape, F32) for s in acc],
                 args=(dtc, dtr, *small, g_dt, g_csc, g_csr, ddk, e01), sem=("arbitrary",))[0]


def _ssd_specs(t_len, d_ssm, r_heads, reverse):
    rp = r_heads * HEADDIM
    nc = t_len // CHUNK
    per = next(p for p in (SSD_CHUNKS_PER_STEP, 2, 1) if nc % p == 0)
    ns, rows, gs = nc // per, per * CHUNK, SSD_GROUPS_PER_STEP
    cidx = (lambda c: ns - 1 - c) if reverse else (lambda c: c)
    b_off = d_ssm // (N_STATE * gs)
    specs = dict(
        xs=pl.BlockSpec((rows, gs * rp), lambda c, g: (cidx(c), g)),
        b=pl.BlockSpec((rows, gs * N_STATE), lambda c, g: (cidx(c), b_off + g)),
        c=pl.BlockSpec((rows, gs * N_STATE), lambda c, g: (cidx(c), b_off + N_GROUPS // gs + g)),
        grad_bc=pl.BlockSpec((rows, gs * N_STATE), lambda c, g: (cidx(c), g)),
        col=pl.BlockSpec((rows, LANES), lambda c, g: (cidx(c), 0)),
        csr=pl.BlockSpec((gs, r_heads, rows), lambda c, g: (g, 0, cidx(c))),
        e01=pl.BlockSpec((LANES, gs * rp), lambda c, g: (0, g)),
        dskip=pl.BlockSpec((1, gs * rp), lambda c, g: (0, g)),
        hprev=pl.BlockSpec((per, gs, N_STATE, rp), lambda c, g: (cidx(c), g, 0, 0)),
        ddk=pl.BlockSpec((per, 1, gs * rp), lambda c, g: (cidx(c), 0, g)),
    )
    return specs, nc, ns, per, rp


def _ssd_fwd(xbc, dt_c, cs_c, cs_r3, e01, dskip_e, *, d_ssm, r_heads, ride=None):
    t_len = xbc.shape[0]
    sp, nc, ns, per, rp = _ssd_specs(t_len, d_ssm, r_heads, False)

    def body(xs_ref, b_ref, c_ref, dt_ref, csc_ref, csr_ref, e_ref, dk_ref, y_ref, hprev_ref, h_ref):
        c, gp = pl.program_id(0), pl.program_id(1)
        groups = [gp * SSD_GROUPS_PER_STEP + gi for gi in range(SSD_GROUPS_PER_STEP)]

        @pl.when(c == 0)
        def _():
            for g in groups:
                h_ref[g] = jnp.zeros((N_STATE, rp), F32)

        hp = [h_ref[g] for g in groups]
        for s in range(per):
            r = pl.ds(s * CHUNK, CHUNK)
            for gi in range(SSD_GROUPS_PER_STEP):
                cols, bc = pl.ds(gi * rp, rp), pl.ds(gi * N_STATE, N_STATE)
                hprev_ref[s, gi] = hp[gi]
                y, hp[gi] = _ssd_chunk(r_heads, xs_ref[r, cols].astype(F32), b_ref[r, bc].astype(F32),
                                       c_ref[r, bc].astype(F32), dt_ref[r, :], csc_ref[r, :], csr_ref[gi, :, r],
                                       e_ref[:, cols], dk_ref[:, cols], hp[gi])
                y_ref[r, cols] = y
        for gi, g in enumerate(groups):
            h_ref[g] = hp[gi]

    return _call(
        body, name="ssd_fwd", grid=(ns, N_GROUPS // SSD_GROUPS_PER_STEP),
        in_specs=[sp["xs"], sp["b"], sp["c"], sp["col"], sp["col"], sp["csr"], sp["e01"], sp["dskip"]],
        out_specs=[sp["xs"], sp["hprev"]],
        out_shape=[jax.ShapeDtypeStruct((t_len, d_ssm), F32),
                   jax.ShapeDtypeStruct((nc, N_GROUPS, N_STATE, rp), F32)],
        args=(xbc, xbc, xbc, dt_c, cs_c, cs_r3, e01, dskip_e), scratch=[pltpu.VMEM((N_GROUPS, N_STATE, rp), F32)],
        sem=("arbitrary", "arbitrary"), ride=ride)


def _ssd_bwd(xbc, dt_c, cs_c, cs_r3, e01, dskip_e, hprev, dy, *, d_ssm, r_heads, ride=None):
    t_len = xbc.shape[0]
    sp, nc, ns, per, rp = _ssd_specs(t_len, d_ssm, r_heads, True)

    def body(xs_ref, b_ref, c_ref, dt_ref, csc_ref, csr_ref, e_ref, dk_ref, hprev_ref, dy_ref,
             dxs_ref, db_ref, dc_ref, ddt_ref, dcsc_ref, dcsr_ref, ddk_ref, dh_ref):
        c, gp = pl.program_id(0), pl.program_id(1)
        groups = [gp * SSD_GROUPS_PER_STEP + gi for gi in range(SSD_GROUPS_PER_STEP)]

        @pl.when(gp == 0)
        def _():
            ddt_ref[...] = jnp.zeros_like(ddt_ref)
            dcsc_ref[...] = jnp.zeros_like(dcsc_ref)

        @pl.when(c == 0)
        def _():
            for g in groups:
                dh_ref[g] = jnp.zeros((N_STATE, rp), F32)

        dh = [dh_ref[g] for g in groups]
        for s in reversed(range(per)):
            r = pl.ds(s * CHUNK, CHUNK)
            ddt_sum, dcsc_sum = ddt_ref[r, :], dcsc_ref[r, :]
            for gi in range(SSD_GROUPS_PER_STEP):
                cols, bc = pl.ds(gi * rp, rp), pl.ds(gi * N_STATE, N_STATE)
                e01 = e_ref[:, cols]
                fn = lambda xs, bg, cg, dt, csc, csr, dk, hp: _ssd_chunk(r_heads, xs, bg, cg, dt, csc, csr, e01, dk, hp)
                _, vjp = jax.vjp(fn, xs_ref[r, cols].astype(F32), b_ref[r, bc].astype(F32), c_ref[r, bc].astype(F32),
                                 dt_ref[r, :], csc_ref[r, :], csr_ref[gi, :, r], dk_ref[:, cols], hprev_ref[s, gi])
                dxs, dbg, dcg, ddt, dcsc, dcsr, ddk, dh[gi] = vjp((dy_ref[r, cols], dh[gi]))
                dxs_ref[r, cols] = dxs.astype(dxs_ref.dtype)
                db_ref[r, bc] = dbg.astype(db_ref.dtype)
                dc_ref[r, bc] = dcg.astype(dc_ref.dtype)
                ddt_sum, dcsc_sum = ddt_sum + ddt, dcsc_sum + dcsc
                dcsr_ref[gi, :, r] = dcsr
                ddk_ref[s, :, cols] = ddk
            ddt_ref[r, :], dcsc_ref[r, :] = ddt_sum, dcsc_sum
        for gi, g in enumerate(groups):
            dh_ref[g] = dh[gi]

    n_bc = N_GROUPS * N_STATE
    return _call(
        body, name="ssd_bwd", grid=(ns, N_GROUPS // SSD_GROUPS_PER_STEP),
        in_specs=[sp["xs"], sp["b"], sp["c"], sp["col"], sp["col"], sp["csr"], sp["e01"], sp["dskip"], sp["hprev"],
                  sp["xs"]],
        out_specs=[sp["xs"], sp["grad_bc"], sp["grad_bc"], sp["col"], sp["col"], sp["csr"], sp["ddk"]],
        out_shape=[jax.ShapeDtypeStruct((t_len, d_ssm), BF16), jax.ShapeDtypeStruct((t_len, n_bc), BF16),
                   jax.ShapeDtypeStruct((t_len, n_bc), BF16), jax.ShapeDtypeStruct(dt_c.shape, F32),
                   jax.ShapeDtypeStruct(cs_c.shape, F32), jax.ShapeDtypeStruct(cs_r3.shape, F32),
                   jax.ShapeDtypeStruct((nc, 1, d_ssm), F32)],
        args=(xbc, xbc, xbc, dt_c, cs_c, cs_r3, e01, dskip_e, hprev, dy),
        scratch=[pltpu.VMEM((N_GROUPS, N_STATE, rp), F32)], sem=("arbitrary", "arbitrary"), ride=ride)


def _chip_sum(src, sib, *, name):
    rows, cols = src.shape[1:]
    tr = _tile(rows, 256, BF16_ROWS)
    core = lax.axis_index("c").astype(jnp.int32).reshape(1)

    def body(c_ref, a_ref, b_ref, o_ref):
        o_ref[...] = (a_ref[...].astype(F32) + b_ref[...].astype(F32)).astype(o_ref.dtype)

    grid_spec = pltpu.PrefetchScalarGridSpec(
        num_scalar_prefetch=1, grid=(N_CHIPS, rows // tr),
        in_specs=[pl.BlockSpec((None, tr, cols), lambda q, i, c_ref: (2 * q + c_ref[0], i, 0)),
                  pl.BlockSpec((None, tr, cols), lambda q, i, c_ref: (q, i, 0))],
        out_specs=pl.BlockSpec((None, tr, cols), lambda q, i, c_ref: (q, i, 0)))
    return pl.pallas_call(
        body, name=name, grid_spec=grid_spec, out_shape=jax.ShapeDtypeStruct(sib.shape, sib.dtype),
        compiler_params=pltpu.CompilerParams(dimension_semantics=("parallel", "parallel"), vmem_limit_bytes=VMEM_LIMIT),
    )(core, src, sib)


def _adamw(w, g, m, v):
    m = ADAM_B1 * m + (1.0 - ADAM_B1) * g
    v = ADAM_B2 * v + (1.0 - ADAM_B2) * (g * g)
    m_hat = m / (1.0 - ADAM_B1 ** ADAM_STEP)
    v_hat = v / (1.0 - ADAM_B2 ** ADAM_STEP)
    delta = -ADAM_LR * (m_hat / (jnp.sqrt(v_hat) + ADAM_EPS) + ADAM_WD * w)
    return delta, m, v


def _reduce_adamw(parts, w, m, v, *, name):
    n_parts = parts.shape[0]
    rows, cols = w.shape
    tr = _tile(rows, 128, BF16_ROWS)

    def body(p_ref, w_ref, m_ref, v_ref, g_ref, d_ref, mo_ref, vo_ref):
        g = p_ref[0].astype(F32)
        for k in range(1, n_parts):
            g = g + p_ref[k].astype(F32)
        delta, mn, vn = _adamw(w_ref[...], g, m_ref[...], v_ref[...])
        g_ref[...] = g
        d_ref[...] = delta
        mo_ref[...] = mn
        vo_ref[...] = vn

    spec = pl.BlockSpec((tr, cols), lambda i: (i, 0))
    outs, _ = _call(
        body, name=name, grid=(rows // tr,),
        in_specs=[pl.BlockSpec((n_parts, tr, cols), lambda i: (0, i, 0)), spec, spec, spec],
        out_specs=[spec] * 4, out_shape=[jax.ShapeDtypeStruct((rows, cols), F32)] * 4,
        args=(parts, w, m, v), sem=("parallel",))
    return outs


def _move_rows(src, src_row, name, extra=None, extra_row=None):
    rb, n_out, cols = ROW_BLOCK, len(src_row), src.shape[1]
    assert n_out % rb == 0 and src.shape[0] % rb == 0 and src.shape[0] // rb >= 3
    n_blocks, max_b0, seg_cap = n_out // rb, src.shape[0] // rb - 3, 4

    def segments(rows_of, lo):
        segs, r = [], 0
        while r < rb:
            if rows_of[r] < 0:
                r += 1
                continue
            e = r
            while e + 1 < rb and rows_of[e + 1] == rows_of[e] + 1:
                e += 1
            segs.append((r, e + 1, rows_of[r] - r - lo))
            r = e + 1
        assert len(segs) <= seg_cap
        return segs + [(0, 0, 0)] * (seg_cap - len(segs))

    table = []
    for j in range(n_blocks):
        rows_j = list(src_row[j * rb:(j + 1) * rb])
        valid = [v for v in rows_j if v >= 0]
        b0 = min(max((min(valid) // rb) if valid else 0, 0), max_b0)
        assert not valid or max(valid) < (b0 + 3) * rb
        row = [b0] + [v for seg in segments(rows_j, b0 * rb) for v in seg]
        extra_j = [] if extra is None else list(extra_row[j * rb:(j + 1) * rb])
        if extra is not None:
            row += [v for seg in segments(extra_j, 0) for v in seg]
        row += [int(bool(valid) and max(valid) >= (b0 + 2) * rb), int(any(v >= 0 for v in extra_j))]
        table.append(row)
    table = jnp.asarray(table, jnp.int32)
    flag_third, flag_extra = len(table[0]) - 2, len(table[0]) - 1

    def select(tbl_ref, j, first, width):
        r = lax.broadcasted_iota(jnp.int32, (rb, width), 0)
        c = lax.broadcasted_iota(jnp.int32, (rb, width), 1)
        hit = jnp.zeros((rb, width), jnp.bool_)
        for s in range(seg_cap):
            lo, hi, off = (tbl_ref[j, first + 3 * s + i] for i in range(3))
            hit = hit | ((r >= lo) & (r < hi) & (c == r + off))
        return jnp.where(hit, 1.0, 0.0).astype(BF16)

    def body(tbl_ref, *refs):
        o_ref = refs[-1]
        j = pl.program_id(0)
        sel = select(tbl_ref, j, 1, 3 * rb)
        pick = lambda b: lax.dot_general(sel[:, b * rb:(b + 1) * rb], refs[b][...], _NN, preferred_element_type=F32)
        o_ref[...] = (pick(0) + pick(1)).astype(o_ref.dtype)

        @pl.when(tbl_ref[j, flag_third] == 1)
        def _():
            o_ref[...] = (o_ref[...].astype(F32) + pick(2)).astype(o_ref.dtype)

        if extra is not None:
            @pl.when(tbl_ref[j, flag_extra] == 1)
            def _():
                more = lax.dot_general(select(tbl_ref, j, 1 + 3 * seg_cap, extra.shape[0]), refs[3][...], _NN,
                                       preferred_element_type=F32)
                o_ref[...] = (o_ref[...].astype(F32) + more).astype(o_ref.dtype)

    in_specs = [pl.BlockSpec((rb, cols), functools.partial(lambda b, j, tbl: (tbl[j, 0] + b, 0), b)) for b in range(3)]
    args = [src, src, src]
    if extra is not None:
        in_specs.append(pl.BlockSpec(extra.shape, lambda j, tbl: (0, 0)))
        args.append(extra)
    grid_spec = pltpu.PrefetchScalarGridSpec(num_scalar_prefetch=1, grid=(n_blocks,), in_specs=in_specs,
                                             out_specs=pl.BlockSpec((rb, cols), lambda j, tbl: (j, 0)))
    return pl.pallas_call(
        body, name=name, grid_spec=grid_spec, out_shape=jax.ShapeDtypeStruct((n_out, cols), src.dtype),
        compiler_params=pltpu.CompilerParams(dimension_semantics=("parallel",), vmem_limit_bytes=VMEM_LIMIT),
    )(table, *args)


def _cols_of(g):
    return jnp.transpose(g, (1, 0, 2)).reshape(g.shape[1], -1)


def _pad_to(a, rows, cols):
    return jnp.pad(a, ((0, rows - a.shape[0]), (0, cols - a.shape[1])))


def kernel(x, norm_mix_g, w_in, ssm_conv_w, ssm_conv_b, ssm_dt_bias, ssm_A_log, ssm_D, ssm_norm_g, sc_conv_w, w_out, norm_ffn_g, w_gate, w_up, w_down, norm_final_g, loss_target, m_norm_mix_g, m_w_in, m_ssm_conv_w, m_ssm_conv_b, m_ssm_dt_bias, m_ssm_A_log, m_ssm_D, m_ssm_norm_g, m_sc_conv_w, m_w_out, m_norm_ffn_g, m_w_gate, m_w_up, m_w_down, m_norm_final_g, v_norm_mix_g, v_w_in, v_ssm_conv_w, v_ssm_conv_b, v_ssm_dt_bias, v_ssm_A_log, v_ssm_D, v_ssm_norm_g, v_sc_conv_w, v_w_out, v_norm_ffn_g, v_w_gate, v_w_up, v_w_down, v_norm_final_g):
    t_len, d = x.shape[1], x.shape[2]
    heads = d // HEADDIM
    r_heads = heads // N_GROUPS
    d_xbc = d + 2 * N_GROUPS * N_STATE
    ff_s = w_down.shape[1]
    ff = ff_s * N_DEV
    off_xbc, off_dt = d, d + d_xbc
    off_cb = off_dt + heads
    d_in = off_cb + 3 * d
    in_s = d_in // N_DEV
    in_p = -(-in_s // (2 * BF16_ROWS)) * (2 * BF16_ROWS)
    w_main = 4 * d + d_xbc
    me = 4 * lax.axis_index("x") + 2 * lax.axis_index("y") + lax.axis_index("c")

    x2 = x[0]
    target = loss_target[0]

    tpose = lambda a: jnp.transpose(a[0])
    win_s = _pad_to(tpose(w_in).astype(BF16), in_p, d)
    wg_s, wu_s = tpose(w_gate).astype(BF16), tpose(w_up).astype(BF16)
    wo_s, wd_s = w_out[0].astype(BF16), w_down[0].astype(BF16)
    small_w = jnp.concatenate([_pad_to(ssm_conv_w[0], K_SSM, d_xbc // N_DEV),
                               _pad_to(sc_conv_w[0], K_SC + 1, d_xbc // N_DEV)], axis=0)

    g1, g2, g3 = norm_mix_g, norm_ffn_g, norm_final_g.reshape(1, d)
    gs = ssm_norm_g
    small = [_pad_to(ssm_dt_bias, 1, LANES), ssm_dt_bias.reshape(heads, 1), _pad_to(ssm_A_log, 1, LANES),
             ssm_A_log.reshape(heads, 1)]
    e01 = (lax.broadcasted_iota(jnp.int32, (LANES, d), 1) // HEADDIM
           == lax.broadcasted_iota(jnp.int32, (LANES, d), 0)).astype(BF16)
    dskip_e = jnp.repeat(ssm_D, HEADDIM, axis=1)
    tr = _tile(t_len, 256, 8)
    tr_ff = _tile(t_len, 128, 8)
    cw = LANES
    slab = lambda col: col // cw

    gin_1, gsm_1 = _gather_chips_relayed([win_s], [small_w], "gather_w_in_chips")
    (n1,), (gin, gsm) = _rows_call(lambda v, g: ((_rms(v, g),), ()), rows=t_len, tr=tr, row_ins=[(x2, d, 0)],
                                   full_ins=[g1], row_outs=[(d, BF16)], acc_outs=[], name="norm_mix",
                                   ride=_gather_sibling([gin_1, gsm_1]))
    in_pieces = []
    for k in range(N_DEV):
        for a, b, dst, shift in ((0, off_dt, 0, 0), (off_dt, off_cb, 1, -off_dt), (off_cb, d_in, 0, -heads)):
            s, e = max(k * in_s, a), min((k + 1) * in_s, b)
            if s < e:
                in_pieces.append((k, s - k * in_s, e - s, dst, s + shift))
    ref_row = lambda t: t if t < off_dt else t + heads
    wtm = _move_rows(gin.reshape(N_DEV * in_p, d),
                     [(ref_row(t) // in_s) * in_p + ref_row(t) % in_s for t in range(w_main)], "place_w_in")
    wtdt = jnp.zeros((LANES, d), BF16)
    for k, r0, n, dst, d0 in in_pieces:
        if dst == 1:
            wtdt = lax.dynamic_update_slice(wtdt, gin[k, r0:r0 + n], (d0, 0))
    cw_ssm = _cols_of(gsm[:, :K_SSM, :])
    cw_sc = _cols_of(gsm[:, K_SSM:K_SSM + K_SC, :d // N_DEV])

    proj, (go_1, gg_1) = _matmul(n1, wtm, tb=True, out_dtype=BF16, name="proj_main",
                                 ride=_gather_chips([wo_s, wg_s]))
    dt_raw, _ = _matmul(n1, wtdt, tb=True, out_dtype=F32, name="proj_dt")
    dt_raw_t = jnp.transpose(dt_raw[:, :heads])
    (xbc,), (go, gg) = _cols_call(_conv_silu_fwd, rows=t_len, cols=d_xbc, cw=cw, col_ins=[(proj, slab(off_xbc))],
                                  par_ins=[(cw_ssm, 0), (ssm_conv_b, 0)], col_outs=[BF16], par_outs=[],
                                  name="ssm_conv", ride=_gather_sibling([go_1, gg_1]))
    dt_c, cs_c, cs_r = _ssd_dt(dt_raw, dt_raw_t, small)
    cs_r3 = cs_r.reshape(N_GROUPS, r_heads, t_len)
    (y_ssd, hprev), (gu_1,) = _ssd_fwd(xbc, dt_c, cs_c, cs_r3, e01, dskip_e, d_ssm=d, r_heads=r_heads,
                                       ride=_gather_chips([wu_s]))

    def gate_norm(y, z, g):
        z = z.astype(F32)
        return _rms(y * (z * _sigmoid(z)), g)

    (y_mix,), (gu,) = _rows_call(lambda y, z, g: ((gate_norm(y, z, g),), ()), rows=t_len, tr=tr,
                                 row_ins=[(y_ssd, d, 0), (proj, d, 0)], full_ins=[gs], row_outs=[(d, BF16, 2 * d)],
                                 acc_outs=[], name="ssm_gate_norm", ride=_gather_sibling([gu_1]))
    wgt, wut, wo = gg.reshape(ff, d), gu.reshape(ff, d), go.reshape(2 * d, d)
    sc0 = slab(d + d_xbc)
    (y_mix,), _ = _cols_call(_shortconv_fwd, rows=t_len, cols=d, cw=cw,
                             col_ins=[(proj, sc0), (proj, sc0 + slab(d)), (proj, sc0 + 2 * slab(d))],
                             par_ins=[(cw_sc, 0)], col_outs=[BF16], par_outs=[], name="shortconv",
                             into=(y_mix, slab(d)))
    h1, _ = _matmul(y_mix, wo, out_dtype=F32, add=x2, name="out_proj")
    (n2,), _ = _rows_call(lambda v, g: ((_rms(v, g),), ()), rows=t_len, tr=tr, row_ins=[(h1, d, 0)], full_ins=[g2],
                          row_outs=[(d, BF16)], acc_outs=[], name="norm_ffn")
    g_ff, (gd_1,) = _matmul(n2, wgt, tb=True, out_dtype=BF16, name="ffn_gate",
                            ride=_gather_chips([wd_s]))
    (u_ff, a_ff), (gd,) = _matmul(n2, wut, tb=True, name="ffn_up", ride=_gather_sibling([gd_1]),
                                  post=(lambda uv, gv: (uv, gv * _sigmoid(gv) * uv), [g_ff], [BF16, BF16]),
                                  tn_max=MM_TILE_N_POST)
    wd = gd.reshape(ff, d)
    h2, _ = _matmul(a_ff, wd, out_dtype=F32, add=h1, name="ffn_down")

    def head(hv, tv, g):
        def f(hh, gg_):
            e = _rms(hh, gg_) - tv
            return (0.5 / d) * jnp.sum(e * e)
        val, (dh, dg) = jax.value_and_grad(f, argnums=(0, 1))(hv, g)
        return (dh, dh), (jnp.full((1, LANES), val, F32), dg)

    (dh2, dh2_b, loss_acc, dg3), _ = _rows_call(head, rows=t_len, tr=tr, row_ins=[(h2, d, 0), (target, d, 0)],
                                                full_ins=[g3], row_outs=[(d, F32), (d, BF16)],
                                                acc_outs=[(1, LANES), (1, d)], name="loss_head")
    loss = lax.psum(loss_acc[0, 0], ("x", "y", "c"))

    def act_bwd(dav, gv, uv):
        s = _sigmoid(gv)
        return dav * uv * (s * (1.0 + gv * (1.0 - s))), dav * gv * s

    (dg_ff, du_ff), _ = _matmul(dh2_b, wd, tb=True, name="d_ffn_gate_up",
                                post=(act_bwd, [g_ff, u_ff], [BF16, BF16]), tn_max=MM_TILE_N_POST)
    dwd, _ = _matmul(a_ff, dh2_b, ta=True, out_dtype=BF16, name="d_w_down")
    dwd8 = dwd.reshape(N_DEV, ff_s, d)
    dn2, (sib_d,) = _matmul(dg_ff, wgt, out_dtype=F32, name="d_norm_ffn_out_gate", ride=_scatter_sibling([dwd8]))
    chip_d = _chip_sum(dwd8, sib_d, name="chip_sum_w_down")
    dn2, (parts_d,) = _matmul(du_ff, wut, out_dtype=F32, add=dn2, name="d_norm_ffn_out_up",
                              ride=_scatter_chips([chip_d]))
    dwg, _ = _matmul(dg_ff, n2, ta=True, out_dtype=BF16, name="d_w_gate")
    dwu, _ = _matmul(du_ff, n2, ta=True, out_dtype=BF16, name="d_w_up")
    dwg8, dwu8 = dwg.reshape(N_DEV, ff_s, d), dwu.reshape(N_DEV, ff_s, d)

    def norm_bwd(v, dn, dres, g):
        _, vjp = jax.vjp(_rms, v, g)
        dv, dg = vjp(dn)
        return (dv + dres,), (dg,)

    def norm_bwd_2(v, dn, dres, g):
        (dv,), acc = norm_bwd(v, dn, dres, g)
        return (dv, dv), acc

    (dh1, dh1_b, dg2), (sib_g, sib_u) = _rows_call(norm_bwd_2, rows=t_len, tr=tr,
                                                   row_ins=[(h1, d, 0), (dn2, d, 0), (dh2, d, 0)], full_ins=[g2],
                                                   row_outs=[(d, F32), (d, BF16)], acc_outs=[(1, d)], name="d_norm_ffn",
                                                   ride=_scatter_sibling([dwg8, dwu8]))
    chip_g = _chip_sum(dwg8, sib_g, name="chip_sum_w_gate")
    chip_u = _chip_sum(dwu8, sib_u, name="chip_sum_w_up")

    dy_mix, _ = _matmul(dh1_b, wo, tb=True, out_dtype=BF16, name="d_y_mix")
    dwo, _ = _matmul(y_mix, dh1_b, ta=True, out_dtype=BF16, name="d_w_out")
    dwo8 = dwo.reshape(N_DEV, 2 * d // N_DEV, d)
    (dgb, dgc, du, dcw_sc), (sib_o,) = _cols_call(
        _shortconv_bwd, rows=t_len, cols=d, cw=cw,
        col_ins=[(proj, sc0), (proj, sc0 + slab(d)), (proj, sc0 + 2 * slab(d)), (dy_mix, slab(d))],
        par_ins=[(cw_sc, 0)], col_outs=[BF16] * 3, par_outs=[K_SC], name="d_shortconv",
        ride=_scatter_sibling([dwo8]))
    chip_o = _chip_sum(dwo8, sib_o, name="chip_sum_w_out")

    def gate_norm_bwd(y, z, dyo, g):
        _, vjp = jax.vjp(gate_norm, y, z.astype(F32), g)
        dy, dz, dg = vjp(dyo.astype(F32))
        return (dy, dz), (dg,)

    (dy_ssd, dproj, dgs), _ = _rows_call(gate_norm_bwd, rows=t_len, tr=tr,
                                         row_ins=[(y_ssd, d, 0), (proj, d, 0), (dy_mix, d, 0)], full_ins=[gs],
                                         row_outs=[(d, F32), (d, BF16, w_main)], acc_outs=[(1, d)],
                                         name="d_ssm_gate_norm")
    (dxs, dbm, dcm, g_dt, g_csc, g_csr3, ddk), (parts_g, parts_u, parts_o) = _ssd_bwd(
        xbc, dt_c, cs_c, cs_r3, e01, dskip_e, hprev, dy_ssd, d_ssm=d, r_heads=r_heads,
        ride=_scatter_chips([chip_g, chip_u, chip_o]))
    ddt_c, ddt_r, dbias_r, dbias_c, dalog_r, dalog_c, ddskip = _ssd_dt(
        dt_raw, dt_raw_t, small, cots=(g_dt, g_csc, g_csr3.reshape(heads, t_len), ddk, e01))
    dcw_parts, dcb_parts, col0 = [], [], 0
    for tag, dpart in (("x", dxs), ("b", dbm), ("c", dcm)):
        (dproj, dcw_p, dcb_p), _ = _cols_call(
            _conv_silu_bwd, rows=t_len, cols=dpart.shape[1], cw=cw,
            col_ins=[(proj, slab(off_xbc + col0)), (dpart, 0)], par_ins=[(cw_ssm, slab(col0)), (ssm_conv_b, slab(col0))],
            col_outs=[BF16], par_outs=[K_SSM, 1], name="d_ssm_conv_" + tag, into=(dproj, slab(off_xbc + col0)))
        dcw_parts.append(dcw_p)
        dcb_parts.append(dcb_p)
        col0 += dpart.shape[1]
    dcw_ssm, dcb_ssm = jnp.concatenate(dcw_parts, axis=1), jnp.concatenate(dcb_parts, axis=1)
    for i, part in enumerate((dgb, dgc, du)):
        dproj = lax.dynamic_update_slice(dproj, part, (0, d + d_xbc + i * d))
    ddt = ddt_c + _pad_to(jnp.transpose(ddt_r), t_len, LANES)
    dwm, _ = _matmul(dproj, n1, ta=True, out_dtype=BF16, name="d_w_in_main")
    dwdt, _ = _matmul(ddt, n1, ta=True, out_dtype=BF16, name="d_w_in_dt")
    own_ref = [k * in_s + i if i < in_s else -1 for k in range(N_DEV) for i in range(in_p)]
    dwin8 = _move_rows(
        dwm, [-1 if g < 0 or off_dt <= g < off_cb else (g if g < off_dt else g - heads) for g in own_ref],
        "place_d_w_in", extra=dwdt, extra_row=[g - off_dt if off_dt <= g < off_cb else -1 for g in own_ref],
    ).reshape(N_DEV, in_p, d)
    dn1, (sib_in,) = _matmul(ddt, wtdt, out_dtype=F32, name="d_norm_mix_out_dt", ride=_scatter_sibling([dwin8]))
    chip_in = _chip_sum(dwin8, sib_in, name="chip_sum_w_in")
    cut = int(in_p * W_IN_SCATTER_SPLIT) // BF16_ROWS * BF16_ROWS
    dn1, (parts_in,) = _matmul(dproj, wtm, out_dtype=F32, add=dn1, name="d_norm_mix_out",
                               ride=_scatter_chips([chip_in], rows=(0, cut)))
    (dx, dg1), (parts_in,) = _rows_call(norm_bwd, rows=t_len, tr=tr, row_ins=[(x2, d, 0), (dn1, d, 0), (dh1, d, 0)],
                                        full_ins=[g1], row_outs=[(d, F32)], acc_outs=[(1, d)], name="d_norm_mix",
                                        ride=_scatter_chips([chip_in], rows=(cut, in_p - cut), into=[parts_in]))

    wide = d_xbc
    rows_small = [dg1, dcb_ssm, dbias_r + _pad_to(dbias_c.reshape(1, heads), 1, LANES),
                  dalog_r + _pad_to(dalog_c.reshape(1, heads), 1, LANES), ddskip, dgs, dg2, dg3]
    packed = jnp.concatenate([_pad_to(r, 1, wide) for r in rows_small]
                             + [dcw_ssm, _pad_to(dcw_sc, K_SC, wide), jnp.zeros((1, wide), F32)], axis=0)
    (p_small,) = _comm(_gather_all([packed]), "gather_small_grads")

    conv_lo = me * (d_xbc // N_DEV)
    sc_lo = me * (d // N_DEV)

    def pack_state(vals):
        (nm, cb, dtb, al, dk, sg, nf, nfin, cws, scs) = vals
        rows = [_pad_to(a.reshape(1, -1), 1, wide) for a in (nm, cb, dtb, al, dk, sg, nf, nfin)]
        cws_full = lax.dynamic_update_slice(jnp.zeros((K_SSM, wide), F32), cws[0], (0, conv_lo))
        scs_full = lax.dynamic_update_slice(jnp.zeros((K_SC, wide), F32), scs[0], (0, sc_lo))
        return jnp.concatenate(rows + [cws_full, scs_full, jnp.zeros((1, wide), F32)], axis=0)

    w_small = pack_state((norm_mix_g, ssm_conv_b, ssm_dt_bias, ssm_A_log, ssm_D, ssm_norm_g, norm_ffn_g, norm_final_g,
                          ssm_conv_w, sc_conv_w))
    m_small = pack_state((m_norm_mix_g, m_ssm_conv_b, m_ssm_dt_bias, m_ssm_A_log, m_ssm_D, m_ssm_norm_g, m_norm_ffn_g,
                          m_norm_final_g, m_ssm_conv_w, m_sc_conv_w))
    v_small = pack_state((v_norm_mix_g, v_ssm_conv_b, v_ssm_dt_bias, v_ssm_A_log, v_ssm_D, v_ssm_norm_g, v_norm_ffn_g,
                          v_norm_final_g, v_ssm_conv_w, v_sc_conv_w))

    tin = lambda a: _pad_to(tpose(a), in_p, d)
    tin_back = lambda a: jnp.transpose(a[:in_s])[None]
    t_back = lambda a: jnp.transpose(a)[None]
    upd = {
        "w_in": [tin_back(o) for o in _reduce_adamw(parts_in, tin(w_in), tin(m_w_in), tin(v_w_in), name="adamw_w_in")],
        "w_out": [o[None] for o in _reduce_adamw(parts_o, w_out[0], m_w_out[0], v_w_out[0], name="adamw_w_out")],
        "w_gate": [t_back(o) for o in _reduce_adamw(parts_g, tpose(w_gate), tpose(m_w_gate), tpose(v_w_gate),
                                                    name="adamw_w_gate")],
        "w_up": [t_back(o) for o in _reduce_adamw(parts_u, tpose(w_up), tpose(m_w_up), tpose(v_w_up),
                                                  name="adamw_w_up")],
        "w_down": [o[None] for o in _reduce_adamw(parts_d, w_down[0], m_w_down[0], v_w_down[0], name="adamw_w_down")],
    }
    small_upd = _reduce_adamw(p_small, w_small, m_small, v_small, name="adamw_small")

    def unpack(packed_out):
        vec = lambda i, n, shape: packed_out[i, :n].reshape(shape)
        return {
            "norm_mix_g": vec(0, d, (1, d)), "ssm_conv_b": vec(1, d_xbc, (1, d_xbc)),
            "ssm_dt_bias": vec(2, heads, (1, heads)), "ssm_A_log": vec(3, heads, (1, heads)),
            "ssm_D": vec(4, heads, (1, heads)), "ssm_norm_g": vec(5, d, (1, d)), "norm_ffn_g": vec(6, d, (1, d)),
            "norm_final_g": vec(7, d, (d,)),
            "ssm_conv_w": lax.dynamic_slice(packed_out[8:8 + K_SSM], (0, conv_lo), (K_SSM, d_xbc // N_DEV))[None],
            "sc_conv_w": lax.dynamic_slice(packed_out[8 + K_SSM:8 + K_SSM + K_SC], (0, sc_lo), (K_SC, d // N_DEV))[None],
        }

    names = ["norm_mix_g", "w_in", "ssm_conv_w", "ssm_conv_b", "ssm_dt_bias", "ssm_A_log", "ssm_D", "ssm_norm_g",
             "sc_conv_w", "w_out", "norm_ffn_g", "w_gate", "w_up", "w_down", "norm_final_g"]
    outs = []
    for kind in range(4):
        small_k = unpack(small_upd[kind])
        for nm in names:
            outs.append(upd[nm][kind] if nm in upd else small_k[nm])
    return (loss, dx[None], *outs)
```

```python
import collections
import functools

import jax
import jax.numpy as jnp
from jax import lax
from jax.experimental import pallas as pl
from jax.experimental.pallas import tpu as pltpu

F32 = jnp.float32
BF16 = jnp.bfloat16

N_DEV = 8
N_CHIPS = 4
HEADDIM = 64
N_GROUPS = 8
N_STATE = 128
CHUNK = 128
K_SSM = 4
K_SC = 3
EPS = 1e-5
LANES = 128
BF16_ROWS = 16
MM_TILE_MN = 1408
MM_TILE_K = 2816
W_IN_SCATTER_SPLIT = 6 / 7
W_UP_GATHER_SPLIT = 0.8
MM_TILE_N_POST = 704
SSD_CHUNKS_PER_STEP = 4
SSD_GROUPS_PER_STEP = 2
ROW_BLOCK = 256
V7X_VMEM_BYTES = 64 * 1024 * 1024
VMEM_LIMIT = (V7X_VMEM_BYTES * 3) // 4

ADAM_LR = 0.001
ADAM_B1 = 0.9
ADAM_B2 = 0.999
ADAM_EPS = 1e-08
ADAM_WD = 0.01
ADAM_STEP = 10


def _tile(n, pref, align):
    t = min(pref, n)
    t -= t % align
    while t >= align:
        if n % t == 0:
            return t
        t -= align
    return n


_Ride = collections.namedtuple("_Ride", ["ins", "out_shapes", "aliases", "nsem", "plan"])
_ANY = pl.BlockSpec(memory_space=pl.ANY)


def _coords():
    return lax.axis_index("x"), lax.axis_index("y"), lax.axis_index("c")


def _other_chips(x, y):
    return ((1 - x, y), (x, 1 - y), (1 - x, 1 - y))


def _remote(src, dst, send, recv, k, dev):
    return functools.partial(pltpu.make_async_remote_copy, src_ref=src, dst_ref=dst, send_sem=send.at[k],
                             recv_sem=recv.at[k], device_id=dev, device_id_type=pl.DeviceIdType.MESH)


def _local(src, dst, sem):
    return functools.partial(pltpu.make_async_copy, src, dst, sem)


def _start_all(plan):
    for kind, make in plan:
        if kind != "arrival":
            make().start()


def _wait_all(plan):
    for kind, make in plan:
        if kind == "local":
            make().wait()
        elif kind == "out":
            make().wait_send()
        else:
            make().wait_recv()


def _gather_chips(srcs, rows=None, into=None):
    n = len(srcs)

    def plan(ins, outs, send, recv, base):
        x, y, c = _coords()
        me = 4 * x + 2 * y + c
        cut = (lambda ref: ref) if rows is None else (lambda ref: ref.at[pl.ds(rows[0], rows[1])])
        d = []
        for a, (src, dst) in enumerate(zip(ins[:n], outs)):
            k = base + 4 * a
            d.append(("local", _local(cut(src), cut(dst.at[me]), send.at[k + 3])))
            for j, (px, py) in enumerate(_other_chips(x, y)):
                d.append(("out", _remote(cut(src), cut(dst.at[me]), send, recv, k + j, (px, py, c))))
                d.append(("arrival", _remote(cut(src), cut(dst.at[4 * px + 2 * py + c]), send, recv, k + j,
                                             (px, py, c))))
        return d
    shapes = [jax.ShapeDtypeStruct((N_DEV,) + s.shape, s.dtype) for s in srcs]
    if into is None:
        return _Ride(list(srcs), shapes, {}, 4 * n, plan)
    return _Ride(list(srcs) + list(into), shapes, {n + a: a for a in range(n)}, 4 * n, plan)


def _gather_sibling(bufs):
    def plan(ins, outs, send, recv, base):
        x, y, c = _coords()
        d = []
        for a, buf in enumerate(outs):
            for q in range(N_CHIPS):
                k = base + 4 * a + q
                d.append(("out", _remote(buf.at[2 * q + c], buf.at[2 * q + c], send, recv, k, (x, y, 1 - c))))
                d.append(("arrival", _remote(buf.at[2 * q + c], buf.at[2 * q + 1 - c], send, recv, k, (x, y, 1 - c))))
        return d
    shapes = [jax.ShapeDtypeStruct(b.shape, b.dtype) for b in bufs]
    return _Ride(list(bufs), shapes, {i: i for i in range(len(bufs))}, 4 * len(bufs), plan)


def _scatter_sibling(srcs):
    def plan(ins, outs, send, recv, base):
        x, y, c = _coords()
        d = []
        for a, (src, sib) in enumerate(zip(ins, outs)):
            for q in range(N_CHIPS):
                k = base + 4 * a + q
                d.append(("out", _remote(src.at[2 * q + 1 - c], sib.at[q], send, recv, k, (x, y, 1 - c))))
                d.append(("arrival", _remote(src.at[2 * q + 1 - c], sib.at[q], send, recv, k, (x, y, 1 - c))))
        return d
    shapes = [jax.ShapeDtypeStruct((N_CHIPS,) + s.shape[1:], s.dtype) for s in srcs]
    return _Ride(list(srcs), shapes, {}, 4 * len(srcs), plan)


def _scatter_chips(chips, rows=None, into=None):
    n = len(chips)

    def plan(ins, outs, send, recv, base):
        x, y, c = _coords()
        mine = 2 * x + y
        cut = (lambda ref: ref) if rows is None else (lambda ref: ref.at[pl.ds(rows[0], rows[1])])
        d = []
        for a, (chip, parts) in enumerate(zip(ins[:n], outs)):
            k = base + 4 * a
            d.append(("local", _local(cut(chip.at[mine]), cut(parts.at[mine]), send.at[k + 3])))
            for j, (px, py) in enumerate(_other_chips(x, y)):
                q = 2 * px + py
                d.append(("out", _remote(cut(chip.at[q]), cut(parts.at[mine]), send, recv, k + j, (px, py, c))))
                d.append(("arrival", _remote(cut(chip.at[q]), cut(parts.at[q]), send, recv, k + j, (px, py, c))))
        return d
    shapes = [jax.ShapeDtypeStruct(s.shape, s.dtype) for s in chips]
    if into is None:
        return _Ride(list(chips), shapes, {}, 4 * n, plan)
    return _Ride(list(chips) + list(into), shapes, {n + a: a for a in range(n)}, 4 * n, plan)


def _gather_all(srcs):
    def plan(ins, outs, send, recv, base):
        x, y, c = _coords()
        me = 4 * x + 2 * y + c
        d = []
        for a, (src, dst) in enumerate(zip(ins, outs)):
            k = base + N_DEV * a
            d.append(("local", _local(src, dst.at[me], send.at[k])))
            for j in range(1, N_DEV):
                px = 1 - x if (j >> 2) & 1 else x
                py = 1 - y if (j >> 1) & 1 else y
                pc = 1 - c if j & 1 else c
                d.append(("out", _remote(src, dst.at[me], send, recv, k + j, (px, py, pc))))
                d.append(("arrival", _remote(src, dst.at[4 * px + 2 * py + pc], send, recv, k + j, (px, py, pc))))
        return d
    shapes = [jax.ShapeDtypeStruct((N_DEV,) + s.shape, s.dtype) for s in srcs]
    return _Ride(list(srcs), shapes, {}, N_DEV * len(srcs), plan)


def _merge(*rides):
    ins, outs, aliases, parts, nsem = [], [], {}, [], 0
    for r in rides:
        parts.append((len(ins), len(outs), nsem, r))
        aliases.update({len(ins) + i: len(outs) + j for i, j in r.aliases.items()})
        ins += r.ins
        outs += r.out_shapes
        nsem += r.nsem

    def plan(i, o, send, recv, base):
        d = []
        for i0, o0, s0, r in parts:
            d += r.plan(i[i0:i0 + len(r.ins)], o[o0:o0 + len(r.out_shapes)], send, recv, base + s0)
        return d
    return _Ride(ins, outs, aliases, nsem, plan)


def _comm(ride, name):
    n_in, n_out = len(ride.ins), len(ride.out_shapes)

    def body(*refs):
        plan = ride.plan(refs[:n_in], refs[n_in:n_in + n_out], refs[-2], refs[-1], 0)
        _start_all(plan)
        _wait_all(plan)

    return pl.pallas_call(
        body, name=name, in_specs=[_ANY] * n_in, out_specs=[_ANY] * n_out, out_shape=ride.out_shapes,
        scratch_shapes=[pltpu.SemaphoreType.DMA((ride.nsem,)), pltpu.SemaphoreType.DMA((ride.nsem,))],
        input_output_aliases=dict(ride.aliases),
        compiler_params=pltpu.CompilerParams(has_side_effects=True),
    )(*ride.ins)


def _gather_chips_relayed(big, small, name):
    srcs = list(big) + list(small)
    n, nsem = len(srcs), 5 * len(srcs)

    def body(*refs):
        ins, outs, send, recv = refs[:n], refs[n:2 * n], refs[-2], refs[-1]
        x, y, c = _coords()
        slot = lambda dev: 4 * dev[0] + 2 * dev[1] + dev[2]
        me, nbr_x, nbr_y, diag = (x, y, c), (1 - x, y, c), (x, 1 - y, c), (1 - x, 1 - y, c)
        own, sends = [], []
        for a, (src, dst) in enumerate(zip(ins, outs)):
            k = 5 * a
            own.append(_local(src, dst.at[slot(me)], send.at[k + 4])())
            sends.append(_remote(src, dst.at[slot(me)], send, recv, k, nbr_x)())
            sends.append(_remote(src, dst.at[slot(me)], send, recv, k + 1, nbr_y)())
            if a >= len(big):
                sends.append(_remote(src, dst.at[slot(me)], send, recv, k + 2, diag)())
        for s in own + sends:
            s.start()
        for a, (src, dst) in enumerate(zip(ins, outs)):
            k = 5 * a
            _remote(src, dst.at[slot(nbr_x)], send, recv, k, nbr_x)().wait_recv()
            if a < len(big):
                half = src.shape[0] // 2
                part = dst.at[slot(nbr_x)].at[pl.ds(0, half)]
                fwd = _remote(part, part, send, recv, k + 2, nbr_y)()
                fwd.start()
                sends.append(fwd)
            _remote(src, dst.at[slot(nbr_y)], send, recv, k + 1, nbr_y)().wait_recv()
            if a < len(big):
                part = dst.at[slot(nbr_y)].at[pl.ds(half, src.shape[0] - half)]
                fwd = _remote(part, part, send, recv, k + 3, nbr_x)()
                fwd.start()
                sends.append(fwd)
        for a, (src, dst) in enumerate(zip(ins, outs)):
            k = 5 * a
            if a < len(big):
                half = src.shape[0] // 2
                lo = dst.at[slot(diag)].at[pl.ds(0, half)]
                hi = dst.at[slot(diag)].at[pl.ds(half, src.shape[0] - half)]
                _remote(lo, lo, send, recv, k + 2, nbr_y)().wait_recv()
                _remote(hi, hi, send, recv, k + 3, nbr_x)().wait_recv()
            else:
                _remote(src, dst.at[slot(diag)], send, recv, k + 2, diag)().wait_recv()
        for lc in own:
            lc.wait()
        for s in sends:
            s.wait_send()

    return pl.pallas_call(
        body, name=name, in_specs=[_ANY] * n, out_specs=[_ANY] * n,
        out_shape=[jax.ShapeDtypeStruct((N_DEV,) + s.shape, s.dtype) for s in srcs],
        scratch_shapes=[pltpu.SemaphoreType.DMA((nsem,)), pltpu.SemaphoreType.DMA((nsem,))],
        compiler_params=pltpu.CompilerParams(has_side_effects=True),
    )(*srcs)


def _call(body, *, name, grid, in_specs, out_specs, out_shape, args, sem, scratch=(), ride=None, base=None):
    params = pltpu.CompilerParams(dimension_semantics=sem, vmem_limit_bytes=VMEM_LIMIT)
    own_aliases = {}
    if base is not None:
        inner, n_host = body, len(args)
        body = lambda *refs: inner(*refs[:n_host], *refs[n_host + 1:])
        own_aliases[n_host] = base[1]
        args, in_specs = tuple(args) + (base[0],), list(in_specs) + [_ANY]
    if ride is None:
        res = pl.pallas_call(body, name=name, grid=grid, in_specs=in_specs, out_specs=out_specs,
                             out_shape=out_shape, scratch_shapes=list(scratch), input_output_aliases=own_aliases,
                             compiler_params=params)(*args)
        return list(res), []
    n_in, n_out, n_scr = len(args), len(out_shape), len(scratch)
    r_in, r_out = len(ride.ins), len(ride.out_shapes)

    def hosted(*refs):
        h_in, rin = refs[:n_in], refs[n_in:n_in + r_in]
        o0 = n_in + r_in
        h_out, rout = refs[o0:o0 + n_out], refs[o0 + n_out:o0 + n_out + r_out]
        s0 = o0 + n_out + r_out
        h_scr, send, recv = refs[s0:s0 + n_scr], refs[s0 + n_scr], refs[s0 + n_scr + 1]
        ids = [pl.program_id(i) for i in range(len(grid))]
        first = functools.reduce(lambda p, q: p & q, [i == 0 for i in ids])
        last = functools.reduce(lambda p, q: p & q, [i == n - 1 for i, n in zip(ids, grid)])

        @pl.when(first)
        def _():
            _start_all(ride.plan(rin, rout, send, recv, 0))

        body(*h_in, *h_out, *h_scr)

        @pl.when(last)
        def _():
            _wait_all(ride.plan(rin, rout, send, recv, 0))

    res = pl.pallas_call(
        hosted, name=name, grid=grid, in_specs=list(in_specs) + [_ANY] * r_in,
        out_specs=list(out_specs) + [_ANY] * r_out, out_shape=list(out_shape) + list(ride.out_shapes),
        scratch_shapes=list(scratch) + [pltpu.SemaphoreType.DMA((ride.nsem,)), pltpu.SemaphoreType.DMA((ride.nsem,))],
        input_output_aliases={**own_aliases, **{n_in + i: n_out + j for i, j in ride.aliases.items()}},
        compiler_params=params,
    )(*args, *ride.ins)
    return list(res[:n_out]), list(res[n_out:])


def _matmul(a, b, *, ta=False, tb=False, out_dtype=BF16, add=None, post=None, name, ride=None, tn_max=MM_TILE_MN):
    m = a.shape[1] if ta else a.shape[0]
    k = a.shape[0] if ta else a.shape[1]
    n = b.shape[0] if tb else b.shape[1]
    assert k == (b.shape[1] if tb else b.shape[0])
    tm, tn, tk = _tile(m, MM_TILE_MN, LANES), _tile(n, tn_max, LANES), _tile(k, MM_TILE_K, LANES)
    nk = k // tk
    dims = (((0 if ta else 1,), (1 if tb else 0,)), ((), ()))
    single = post is None
    if add is not None:
        post = (lambda r, t: (r + t,), [add], [out_dtype])
    elif post is None:
        post = (lambda r: (r,), [], [out_dtype])
    post_fn, extras, out_dtypes = post
    n_ex, n_o = len(extras), len(out_dtypes)

    def body(*refs):
        a_ref, b_ref = refs[:2]
        ex_refs, o_refs = refs[2:2 + n_ex], refs[2 + n_ex:2 + n_ex + n_o]

        def finish(r):
            for o_ref, v in zip(o_refs, post_fn(r, *[e[...].astype(F32) for e in ex_refs])):
                o_ref[...] = v.astype(o_ref.dtype)

        part = lax.dot_general(a_ref[...].astype(BF16), b_ref[...].astype(BF16), dims, preferred_element_type=F32)
        if nk == 1:
            finish(part)
            return
        acc = refs[-1]
        kk = pl.program_id(2)

        @pl.when(kk == 0)
        def _():
            acc[...] = part

        @pl.when((kk > 0) & (kk < nk - 1))
        def _():
            acc[...] += part

        @pl.when(kk == nk - 1)
        def _():
            finish(acc[...] + part)

    a_spec = (pl.BlockSpec((tk, tm), lambda i, j, kk: (kk, i)) if ta
              else pl.BlockSpec((tm, tk), lambda i, j, kk: (i, kk)))
    b_spec = (pl.BlockSpec((tn, tk), lambda i, j, kk: (j, kk)) if tb
              else pl.BlockSpec((tk, tn), lambda i, j, kk: (kk, j)))
    o_spec = pl.BlockSpec((tm, tn), lambda i, j, kk: (i, j))
    outs, rides = _call(
        body, name=name, grid=(m // tm, n // tn, nk),
        in_specs=[a_spec, b_spec] + [o_spec] * n_ex, out_specs=[o_spec] * n_o,
        out_shape=[jax.ShapeDtypeStruct((m, n), dt) for dt in out_dtypes], args=(a, b, *extras),
        scratch=[pltpu.VMEM((tm, tn), F32)] if nk > 1 else [], sem=("parallel", "parallel", "arbitrary"), ride=ride)
    return (outs[0] if single else outs), rides


def _rows_call(fn, *, rows, tr, row_ins, full_ins, row_outs, acc_outs, name, ride=None):
    nr, nf, no, na = len(row_ins), len(full_ins), len(row_outs), len(acc_outs)

    def body(*refs):
        vals = [r[...] for r in refs[:nr + nf]]
        outs, accs = fn(*vals)
        for r, v in zip(refs[nr + nf:nr + nf + no], outs):
            r[...] = v.astype(r.dtype)
        if na:
            @pl.when(pl.program_id(0) == 0)
            def _():
                for r in refs[nr + nf + no:]:
                    r[...] = jnp.zeros_like(r)
            for r, v in zip(refs[nr + nf + no:], accs):
                r[...] += v

    in_specs = [pl.BlockSpec((tr, w), functools.partial(lambda cb, i: (i, cb), cb)) for _, w, cb in row_ins]
    in_specs += [pl.BlockSpec(f.shape, lambda i: (0, 0)) for f in full_ins]
    out_specs = [pl.BlockSpec((tr, o[0]), lambda i: (i, 0)) for o in row_outs]
    out_specs += [pl.BlockSpec(s, lambda i: (0, 0)) for s in acc_outs]
    out_shape = [jax.ShapeDtypeStruct((rows, o[-1] if len(o) == 3 else o[0]), o[1]) for o in row_outs]
    out_shape += [jax.ShapeDtypeStruct(s, F32) for s in acc_outs]
    return _call(body, name=name, grid=(rows // tr,), in_specs=in_specs, out_specs=out_specs, out_shape=out_shape,
                 args=tuple(a for a, _, _ in row_ins) + tuple(full_ins), sem=("arbitrary",), ride=ride)


def _cols_call(fn, *, rows, cols, cw, col_ins, par_ins, col_outs, par_outs, name, ride=None, into=None):
    nc, npar = len(col_ins), len(par_ins)

    def body(*refs):
        vals = [r[...] for r in refs[:nc + npar]]
        outs, pouts = fn(*vals)
        for r, v in zip(refs[nc + npar:], tuple(outs) + tuple(pouts)):
            r[...] = v.astype(r.dtype)

    in_specs = [pl.BlockSpec((rows, cw), functools.partial(lambda off, j: (0, off + j), off)) for _, off in col_ins]
    in_specs += [pl.BlockSpec((p.shape[0], cw), functools.partial(lambda off, j: (0, off + j), off))
                 for p, off in par_ins]
    out_specs = [pl.BlockSpec((rows, cw), lambda j: (0, j)) for _ in col_outs]
    out_specs += [pl.BlockSpec((k, cw), lambda j: (0, j)) for k in par_outs]
    out_shape = [jax.ShapeDtypeStruct((rows, cols), dt) for dt in col_outs]
    out_shape += [jax.ShapeDtypeStruct((k, cols), F32) for k in par_outs]
    if into is not None:
        out_specs[0] = pl.BlockSpec((rows, cw), lambda j: (0, into[1] + j))
        out_shape[0] = jax.ShapeDtypeStruct(into[0].shape, into[0].dtype)
    return _call(body, name=name, grid=(cols // cw,), in_specs=in_specs, out_specs=out_specs, out_shape=out_shape,
                 args=tuple(a for a, _ in col_ins) + tuple(p for p, _ in par_ins), sem=("arbitrary",), ride=ride,
                 base=None if into is None else (into[0], 0))


def _sigmoid(v):
    return 1.0 / (1.0 + jnp.exp(-v))


def _softplus(v):
    return jnp.maximum(v, 0.0) + jnp.log(1.0 + jnp.exp(-jnp.abs(v)))


def _rms(v, g):
    return v * lax.rsqrt(jnp.mean(v * v, axis=-1, keepdims=True) + EPS) * g


def _shift_down(v, s, row):
    return jnp.where(row >= s, pltpu.roll(v, s, 0), 0.0)


def _shift_up(v, s, row):
    n = v.shape[0]
    return jnp.where(row < n - s, pltpu.roll(v, n - s, 0), 0.0)


def _causal_conv(u, w, row):
    k_taps = w.shape[0]
    acc = u * w[k_taps - 1:k_taps, :]
    for k in range(k_taps - 1):
        acc = acc + _shift_down(u, k_taps - 1 - k, row) * w[k:k + 1, :]
    return acc


def _causal_conv_bwd(u, dy, w, row):
    k_taps = w.shape[0]
    tap = lax.broadcasted_iota(jnp.int32, w.shape, 0)
    du = dy * w[k_taps - 1:k_taps, :]
    dw = jnp.where(tap == k_taps - 1, jnp.sum(dy * u, axis=0, keepdims=True), 0.0)
    for k in range(k_taps - 1):
        s = k_taps - 1 - k
        du = du + _shift_up(dy, s, row) * w[k:k + 1, :]
        dw = dw + jnp.where(tap == k, jnp.sum(dy * _shift_down(u, s, row), axis=0, keepdims=True), 0.0)
    return du, dw


def _conv_silu_fwd(u, w, b):
    u = u.astype(F32)
    row = lax.broadcasted_iota(jnp.int32, u.shape, 0)
    pre = _causal_conv(u, w, row) + b
    return (pre * _sigmoid(pre),), ()


def _conv_silu_bwd(u, dy, w, b):
    u = u.astype(F32)
    dy = dy.astype(F32)
    row = lax.broadcasted_iota(jnp.int32, u.shape, 0)
    pre = _causal_conv(u, w, row) + b
    s = _sigmoid(pre)
    dpre = dy * (s * (1.0 + pre * (1.0 - s)))
    du, dw = _causal_conv_bwd(u, dpre, w, row)
    return (du,), (dw, jnp.sum(dpre, axis=0, keepdims=True))


def _shortconv_fwd(gb, gc, u, w):
    gb, gc, u = gb.astype(F32), gc.astype(F32), u.astype(F32)
    row = lax.broadcasted_iota(jnp.int32, u.shape, 0)
    return (gb * _causal_conv(gc * u, w, row),), ()


def _shortconv_bwd(gb, gc, u, dy, w):
    gb, gc, u, dy = gb.astype(F32), gc.astype(F32), u.astype(F32), dy.astype(F32)
    row = lax.broadcasted_iota(jnp.int32, u.shape, 0)
    v = gc * u
    dgb = dy * _causal_conv(v, w, row)
    dv, dw = _causal_conv_bwd(v, dy * gb, w, row)
    return (dgb, dv * u, dv * gc), (dw,)


def _split3(v):
    hi = v.astype(BF16)
    r1 = v - hi.astype(F32)
    mid = r1.astype(BF16)
    lo = (r1 - mid.astype(F32)).astype(BF16)
    return hi, mid, lo


def _exact_dot(v, m01, dims, v_is_lhs):
    def one(p):
        return (lax.dot_general(p, m01, dims, preferred_element_type=F32) if v_is_lhs
                else lax.dot_general(m01, p, dims, preferred_element_type=F32))
    hi, mid, lo = _split3(v)
    return (one(lo) + one(mid)) + one(hi)


_NN = (((1,), (0,)), ((), ()))
_NT = (((1,), (1,)), ((), ()))
_TN = (((0,), (0,)), ((), ()))


@jax.custom_vjp
def _cumsum_rows(tril, v):
    return _exact_dot(v, tril, _NN, False)


def _cumsum_rows_fwd(tril, v):
    return _cumsum_rows(tril, v), tril


def _cumsum_rows_bwd(tril, ct):
    return None, _exact_dot(ct, tril, _TN, False)


_cumsum_rows.defvjp(_cumsum_rows_fwd, _cumsum_rows_bwd)


@jax.custom_vjp
def _cumsum_lanes(tril, v):
    return _exact_dot(v, tril, _NT, True)


def _cumsum_lanes_fwd(tril, v):
    return _cumsum_lanes(tril, v), tril


def _cumsum_lanes_bwd(tril, ct):
    return None, _exact_dot(ct, tril, _NN, True)


_cumsum_lanes.defvjp(_cumsum_lanes_fwd, _cumsum_lanes_bwd)


@jax.custom_vjp
def _expand(e01, v):
    return _exact_dot(v, e01, _NN, True)


def _expand_fwd(e01, v):
    return _expand(e01, v), e01


def _expand_bwd(e01, ct):
    return None, _exact_dot(ct, e01, _NT, True)


_expand.defvjp(_expand_fwd, _expand_bwd)


def _causal_mask(n):
    li = lax.broadcasted_iota(jnp.int32, (n, n), 0)
    si = lax.broadcasted_iota(jnp.int32, (n, n), 1)
    return si <= li


def _dt_prep(dtc, dtr, bias_r, bias_c, alog_r, alog_c):
    dt_c = _softplus(dtc + bias_r)
    dt_r = _softplus(dtr + bias_c)
    tril = jnp.where(_causal_mask(dtc.shape[0]), 1.0, 0.0).astype(BF16)
    cs_c = _cumsum_rows(tril, dt_c * (-jnp.exp(alog_r)))
    cs_r = _cumsum_lanes(tril, dt_r * (-jnp.exp(alog_c)))
    return dt_c, cs_c, cs_r


def _ssd_chunk(r_heads, xs, bg, cg, dt_c, cs_c, cs_rg, e01, dskip_e, hp):
    l_len, rp = xs.shape
    p = rp // r_heads
    causal = _causal_mask(l_len)
    lane_head = lax.broadcasted_iota(jnp.int32, (1, rp), 1) // p
    dt_e = _expand(e01, dt_c)
    cs_e = _expand(e01, cs_c)
    cl_e = cs_e[l_len - 1:l_len, :]
    x = xs * dt_e
    bgb, cgb = bg.astype(BF16), cg.astype(BF16)
    cb = lax.dot_general(cgb, bgb, _NT, preferred_element_type=F32)
    ms, xm = [], []
    for r in range(r_heads):
        seg = cs_e[:, r * p:r * p + 1] - cs_rg[r:r + 1, :]
        decay = jnp.exp(jnp.where(causal, seg, -1e30))
        ms.append((cb * decay).astype(BF16))
        xm.append(jnp.where(lane_head == r, x, 0.0).astype(BF16))
    y_diag = lax.dot_general(jnp.concatenate(ms, axis=1), jnp.concatenate(xm, axis=0), _NN,
                             preferred_element_type=F32)
    y_off = lax.dot_general(cgb, hp.astype(BF16), _NN, preferred_element_type=F32) * jnp.exp(cs_e)
    xd = (x * jnp.exp(cl_e - cs_e)).astype(BF16)
    states = lax.dot_general(bgb, xd, _TN, preferred_element_type=F32)
    h_next = hp * jnp.exp(cl_e) + states
    y = y_diag + y_off + dskip_e * xs
    return y, h_next


def _ssd_dt(dtc, dtr, small, cots=None):
    t_len, heads = dtc.shape[0], dtr.shape[0]
    nc = t_len // CHUNK
    col = pl.BlockSpec((CHUNK, LANES), lambda c: (c, 0))
    row = pl.BlockSpec((heads, CHUNK), lambda c: (0, c))
    full = [pl.BlockSpec(s.shape, lambda c: (0, 0)) for s in small]
    shapes = [jax.ShapeDtypeStruct((t_len, LANES), F32), jax.ShapeDtypeStruct((t_len, LANES), F32),
              jax.ShapeDtypeStruct((heads, t_len), F32)]
    if cots is None:
        def body(dtc_ref, dtr_ref, br, bc, ar, ac, dt_ref, csc_ref, csr_ref):
            dt_ref[...], csc_ref[...], csr_ref[...] = _dt_prep(dtc_ref[...], dtr_ref[...], br[...], bc[...],
                                                                ar[...], ac[...])
        return _call(body, name="ssd_dt", grid=(nc,), in_specs=[col, row] + full, out_specs=[col, col, row],
                     out_shape=shapes, args=(dtc, dtr, *small), sem=("parallel",))[0]

    g_dt, g_csc, g_csr, ddk, e01 = cots

    def body(dtc_ref, dtr_ref, br, bc, ar, ac, g_dt_ref, g_csc_ref, g_csr_ref, ddk_ref, e_ref,
             ddtc_ref, ddtr_ref, *dsmall):
        _, vjp = jax.vjp(_dt_prep, dtc_ref[...], dtr_ref[...], br[...], bc[...], ar[...], ac[...])
        grads = vjp((g_dt_ref[...], g_csc_ref[...], g_csr_ref[...]))
        ddtc_ref[...], ddtr_ref[...] = grads[0], grads[1]
        ddk8 = jnp.broadcast_to(ddk_ref[...], (8, ddk_ref.shape[1]))
        dskip = _exact_dot(ddk8, e_ref[...], _NT, True)[0:1, :]

        @pl.when(pl.program_id(0) == 0)
        def _():
            for r in dsmall:
                r[...] = jnp.zeros_like(r)

        for r, gr in zip(dsmall, tuple(grads[2:]) + (dskip,)):
            r[...] += gr

    acc = list(small) + [small[0]]
    return _call(body, name="d_ssd_dt", grid=(nc,),
                 in_specs=[col, row] + full + [col, col, row, pl.BlockSpec((None, 1, e01.shape[1]), lambda c: (c, 0, 0)),
                                               pl.BlockSpec(e01.shape, lambda c: (0, 0))],
                 out_specs=[col, row] + [pl.BlockSpec(s.shape, lambda c: (0, 0)) for s in acc],
                 out_shape=[shapes[0], shapes[2]] + [jax.ShapeDtypeStruct(s.shape, F32) for s in acc],
                 args=(dtc, dtr, *small, g_dt, g_csc, g_csr, ddk, e01), sem=("arbitrary",))[0]


def _ssd_specs(t_len, d_ssm, r_heads, reverse):
    rp = r_heads * HEADDIM
    nc = t_len // CHUNK
    per = next(p for p in (SSD_CHUNKS_PER_STEP, 2, 1) if nc % p == 0)
    ns, rows, gs = nc // per, per * CHUNK, SSD_GROUPS_PER_STEP
    cidx = (lambda c: ns - 1 - c) if reverse else (lambda c: c)
    b_off = d_ssm // (N_STATE * gs)
    specs = dict(
        xs=pl.BlockSpec((rows, gs * rp), lambda c, g: (cidx(c), g)),
        b=pl.BlockSpec((rows, gs * N_STATE), lambda c, g: (cidx(c), b_off + g)),
        c=pl.BlockSpec((rows, gs * N_STATE), lambda c, g: (cidx(c), b_off + N_GROUPS // gs + g)),
        grad_bc=pl.BlockSpec((rows, gs * N_STATE), lambda c, g: (cidx(c), g)),
        col=pl.BlockSpec((rows, LANES), lambda c, g: (cidx(c), 0)),
        csr=pl.BlockSpec((gs, r_heads, rows), lambda c, g: (g, 0, cidx(c))),
        e01=pl.BlockSpec((LANES, gs * rp), lambda c, g: (0, g)),
        dskip=pl.BlockSpec((1, gs * rp), lambda c, g: (0, g)),
        hprev=pl.BlockSpec((per, gs, N_STATE, rp), lambda c, g: (cidx(c), g, 0, 0)),
        ddk=pl.BlockSpec((per, 1, gs * rp), lambda c, g: (cidx(c), 0, g)),
    )
    return specs, nc, ns, per, rp


def _ssd_fwd(xbc, dt_c, cs_c, cs_r3, e01, dskip_e, *, d_ssm, r_heads, ride=None):
    t_len = xbc.shape[0]
    sp, nc, ns, per, rp = _ssd_specs(t_len, d_ssm, r_heads, False)

    def body(xs_ref, b_ref, c_ref, dt_ref, csc_ref, csr_ref, e_ref, dk_ref, y_ref, hprev_ref, h_ref):
        c, gp = pl.program_id(0), pl.program_id(1)
        groups = [gp * SSD_GROUPS_PER_STEP + gi for gi in range(SSD_GROUPS_PER_STEP)]

        @pl.when(c == 0)
        def _():
            for g in groups:
                h_ref[g] = jnp.zeros((N_STATE, rp), F32)

        hp = [h_ref[g] for g in groups]
        for s in range(per):
            r = pl.ds(s * CHUNK, CHUNK)
            for gi in range(SSD_GROUPS_PER_STEP):
                cols, bc = pl.ds(gi * rp, rp), pl.ds(gi * N_STATE, N_STATE)
                hprev_ref[s, gi] = hp[gi]
                y, hp[gi] = _ssd_chunk(r_heads, xs_ref[r, cols].astype(F32), b_ref[r, bc].astype(F32),
                                       c_ref[r, bc].astype(F32), dt_ref[r, :], csc_ref[r, :], csr_ref[gi, :, r],
                                       e_ref[:, cols], dk_ref[:, cols], hp[gi])
                y_ref[r, cols] = y
        for gi, g in enumerate(groups):
            h_ref[g] = hp[gi]

    return _call(
        body, name="ssd_fwd", grid=(ns, N_GROUPS // SSD_GROUPS_PER_STEP),
        in_specs=[sp["xs"], sp["b"], sp["c"], sp["col"], sp["col"], sp["csr"], sp["e01"], sp["dskip"]],
        out_specs=[sp["xs"], sp["hprev"]],
        out_shape=[jax.ShapeDtypeStruct((t_len, d_ssm), F32),
                   jax.ShapeDtypeStruct((nc, N_GROUPS, N_STATE, rp), F32)],
        args=(xbc, xbc, xbc, dt_c, cs_c, cs_r3, e01, dskip_e), scratch=[pltpu.VMEM((N_GROUPS, N_STATE, rp), F32)],
        sem=("arbitrary", "arbitrary"), ride=ride)


def _ssd_bwd(xbc, dt_c, cs_c, cs_r3, e01, dskip_e, hprev, dy, *, d_ssm, r_heads, ride=None):
    t_len = xbc.shape[0]
    sp, nc, ns, per, rp = _ssd_specs(t_len, d_ssm, r_heads, True)

    def body(xs_ref, b_ref, c_ref, dt_ref, csc_ref, csr_ref, e_ref, dk_ref, hprev_ref, dy_ref,
             dxs_ref, db_ref, dc_ref, ddt_ref, dcsc_ref, dcsr_ref, ddk_ref, dh_ref):
        c, gp = pl.program_id(0), pl.program_id(1)
        groups = [gp * SSD_GROUPS_PER_STEP + gi for gi in range(SSD_GROUPS_PER_STEP)]

        @pl.when(gp == 0)
        def _():
            ddt_ref[...] = jnp.zeros_like(ddt_ref)
            dcsc_ref[...] = jnp.zeros_like(dcsc_ref)

        @pl.when(c == 0)
        def _():
            for g in groups:
                dh_ref[g] = jnp.zeros((N_STATE, rp), F32)

        dh = [dh_ref[g] for g in groups]
        for s in reversed(range(per)):
            r = pl.ds(s * CHUNK, CHUNK)
            ddt_sum, dcsc_sum = ddt_ref[r, :], dcsc_ref[r, :]
            for gi in range(SSD_GROUPS_PER_STEP):
                cols, bc = pl.ds(gi * rp, rp), pl.ds(gi * N_STATE, N_STATE)
                e01 = e_ref[:, cols]
                fn = lambda xs, bg, cg, dt, csc, csr, dk, hp: _ssd_chunk(r_heads, xs, bg, cg, dt, csc, csr, e01, dk, hp)
                _, vjp = jax.vjp(fn, xs_ref[r, cols].astype(F32), b_ref[r, bc].astype(F32), c_ref[r, bc].astype(F32),
                                 dt_ref[r, :], csc_ref[r, :], csr_ref[gi, :, r], dk_ref[:, cols], hprev_ref[s, gi])
                dxs, dbg, dcg, ddt, dcsc, dcsr, ddk, dh[gi] = vjp((dy_ref[r, cols], dh[gi]))
                dxs_ref[r, cols] = dxs.astype(dxs_ref.dtype)
                db_ref[r, bc] = dbg.astype(db_ref.dtype)
                dc_ref[r, bc] = dcg.astype(dc_ref.dtype)
                ddt_sum, dcsc_sum = ddt_sum + ddt, dcsc_sum + dcsc
                dcsr_ref[gi, :, r] = dcsr
                ddk_ref[s, :, cols] = ddk
            ddt_ref[r, :], dcsc_ref[r, :] = ddt_sum, dcsc_sum
        for gi, g in enumerate(groups):
            dh_ref[g] = dh[gi]

    n_bc = N_GROUPS * N_STATE
    return _call(
        body, name="ssd_bwd", grid=(ns, N_GROUPS // SSD_GROUPS_PER_STEP),
        in_specs=[sp["xs"], sp["b"], sp["c"], sp["col"], sp["col"], sp["csr"], sp["e01"], sp["dskip"], sp["hprev"],
                  sp["xs"]],
        out_specs=[sp["xs"], sp["grad_bc"], sp["grad_bc"], sp["col"], sp["col"], sp["csr"], sp["ddk"]],
        out_shape=[jax.ShapeDtypeStruct((t_len, d_ssm), BF16), jax.ShapeDtypeStruct((t_len, n_bc), BF16),
                   jax.ShapeDtypeStruct((t_len, n_bc), BF16), jax.ShapeDtypeStruct(dt_c.shape, F32),
                   jax.ShapeDtypeStruct(cs_c.shape, F32), jax.ShapeDtypeStruct(cs_r3.shape, F32),
                   jax.ShapeDtypeStruct((nc, 1, d_ssm), F32)],
        args=(xbc, xbc, xbc, dt_c, cs_c, cs_r3, e01, dskip_e, hprev, dy),
        scratch=[pltpu.VMEM((N_GROUPS, N_STATE, rp), F32)], sem=("arbitrary", "arbitrary"), ride=ride)


def _chip_sum(src, sib, *, name):
    rows, cols = src.shape[1:]
    tr = _tile(rows, 256, BF16_ROWS)
    core = lax.axis_index("c").astype(jnp.int32).reshape(1)

    def body(c_ref, a_ref, b_ref, o_ref):
        o_ref[...] = (a_ref[...].astype(F32) + b_ref[...].astype(F32)).astype(o_ref.dtype)

    grid_spec = pltpu.PrefetchScalarGridSpec(
        num_scalar_prefetch=1, grid=(N_CHIPS, rows // tr),
        in_specs=[pl.BlockSpec((None, tr, cols), lambda q, i, c_ref: (2 * q + c_ref[0], i, 0)),
                  pl.BlockSpec((None, tr, cols), lambda q, i, c_ref: (q, i, 0))],
        out_specs=pl.BlockSpec((None, tr, cols), lambda q, i, c_ref: (q, i, 0)))
    return pl.pallas_call(
        body, name=name, grid_spec=grid_spec, out_shape=jax.ShapeDtypeStruct(sib.shape, sib.dtype),
        compiler_params=pltpu.CompilerParams(dimension_semantics=("parallel", "parallel"), vmem_limit_bytes=VMEM_LIMIT),
    )(core, src, sib)


def _adamw(w, g, m, v):
    m = ADAM_B1 * m + (1.0 - ADAM_B1) * g
    v = ADAM_B2 * v + (1.0 - ADAM_B2) * (g * g)
    m_hat = m / (1.0 - ADAM_B1 ** ADAM_STEP)
    v_hat = v / (1.0 - ADAM_B2 ** ADAM_STEP)
    delta = -ADAM_LR * (m_hat / (jnp.sqrt(v_hat) + ADAM_EPS) + ADAM_WD * w)
    return delta, m, v


def _reduce_adamw(parts, w, m, v, *, name):
    n_parts = parts.shape[0]
    rows, cols = w.shape
    tr = _tile(rows, 128, BF16_ROWS)

    def body(p_ref, w_ref, m_ref, v_ref, g_ref, d_ref, mo_ref, vo_ref):
        g = p_ref[0].astype(F32)
        for k in range(1, n_parts):
            g = g + p_ref[k].astype(F32)
        delta, mn, vn = _adamw(w_ref[...], g, m_ref[...], v_ref[...])
        g_ref[...] = g
        d_ref[...] = delta
        mo_ref[...] = mn
        vo_ref[...] = vn

    spec = pl.BlockSpec((tr, cols), lambda i: (i, 0))
    outs, _ = _call(
        body, name=name, grid=(rows // tr,),
        in_specs=[pl.BlockSpec((n_parts, tr, cols), lambda i: (0, i, 0)), spec, spec, spec],
        out_specs=[spec] * 4, out_shape=[jax.ShapeDtypeStruct((rows, cols), F32)] * 4,
        args=(parts, w, m, v), sem=("parallel",))
    return outs


def _move_rows(src, src_row, name, extra=None, extra_row=None):
    rb, n_out, cols = ROW_BLOCK, len(src_row), src.shape[1]
    assert n_out % rb == 0 and src.shape[0] % rb == 0 and src.shape[0] // rb >= 3
    n_blocks, max_b0, seg_cap = n_out // rb, src.shape[0] // rb - 3, 4

    def segments(rows_of, lo):
        segs, r = [], 0
        while r < rb:
            if rows_of[r] < 0:
                r += 1
                continue
            e = r
            while e + 1 < rb and rows_of[e + 1] == rows_of[e] + 1:
                e += 1
            segs.append((r, e + 1, rows_of[r] - r - lo))
            r = e + 1
        assert len(segs) <= seg_cap
        return segs + [(0, 0, 0)] * (seg_cap - len(segs))

    table = []
    for j in range(n_blocks):
        rows_j = list(src_row[j * rb:(j + 1) * rb])
        valid = [v for v in rows_j if v >= 0]
        b0 = min(max((min(valid) // rb) if valid else 0, 0), max_b0)
        assert not valid or max(valid) < (b0 + 3) * rb
        row = [b0] + [v for seg in segments(rows_j, b0 * rb) for v in seg]
        extra_j = [] if extra is None else list(extra_row[j * rb:(j + 1) * rb])
        if extra is not None:
            row += [v for seg in segments(extra_j, 0) for v in seg]
        row += [int(bool(valid) and max(valid) >= (b0 + 2) * rb), int(any(v >= 0 for v in extra_j))]
        table.append(row)
    table = jnp.asarray(table, jnp.int32)
    flag_third, flag_extra = len(table[0]) - 2, len(table[0]) - 1

    def select(tbl_ref, j, first, width):
        r = lax.broadcasted_iota(jnp.int32, (rb, width), 0)
        c = lax.broadcasted_iota(jnp.int32, (rb, width), 1)
        hit = jnp.zeros((rb, width), jnp.bool_)
        for s in range(seg_cap):
            lo, hi, off = (tbl_ref[j, first + 3 * s + i] for i in range(3))
            hit = hit | ((r >= lo) & (r < hi) & (c == r + off))
        return jnp.where(hit, 1.0, 0.0).astype(BF16)

    def body(tbl_ref, *refs):
        o_ref = refs[-1]
        j = pl.program_id(0)
        sel = select(tbl_ref, j, 1, 3 * rb)
        pick = lambda b: lax.dot_general(sel[:, b * rb:(b + 1) * rb], refs[b][...], _NN, preferred_element_type=F32)
        o_ref[...] = (pick(0) + pick(1)).astype(o_ref.dtype)

        @pl.when(tbl_ref[j, flag_third] == 1)
        def _():
            o_ref[...] = (o_ref[...].astype(F32) + pick(2)).astype(o_ref.dtype)

        if extra is not None:
            @pl.when(tbl_ref[j, flag_extra] == 1)
            def _():
                more = lax.dot_general(select(tbl_ref, j, 1 + 3 * seg_cap, extra.shape[0]), refs[3][...], _NN,
                                       preferred_element_type=F32)
                o_ref[...] = (o_ref[...].astype(F32) + more).astype(o_ref.dtype)

    in_specs = [pl.BlockSpec((rb, cols), functools.partial(lambda b, j, tbl: (tbl[j, 0] + b, 0), b)) for b in range(3)]
    args = [src, src, src]
    if extra is not None:
        in_specs.append(pl.BlockSpec(extra.shape, lambda j, tbl: (0, 0)))
        args.append(extra)
    grid_spec = pltpu.PrefetchScalarGridSpec(num_scalar_prefetch=1, grid=(n_blocks,), in_specs=in_specs,
                                             out_specs=pl.BlockSpec((rb, cols), lambda j, tbl: (j, 0)))
    return pl.pallas_call(
        body, name=name, grid_spec=grid_spec, out_shape=jax.ShapeDtypeStruct((n_out, cols), src.dtype),
        compiler_params=pltpu.CompilerParams(dimension_semantics=("parallel",), vmem_limit_bytes=VMEM_LIMIT),
    )(table, *args)


def _cols_of(g):
    return jnp.transpose(g, (1, 0, 2)).reshape(g.shape[1], -1)


def _pad_to(a, rows, cols):
    return jnp.pad(a, ((0, rows - a.shape[0]), (0, cols - a.shape[1])))


def kernel(x, norm_mix_g, w_in, ssm_conv_w, ssm_conv_b, ssm_dt_bias, ssm_A_log, ssm_D, ssm_norm_g, sc_conv_w, w_out, norm_ffn_g, w_gate, w_up, w_down, norm_final_g, loss_target, m_norm_mix_g, m_w_in, m_ssm_conv_w, m_ssm_conv_b, m_ssm_dt_bias, m_ssm_A_log, m_ssm_D, m_ssm_norm_g, m_sc_conv_w, m_w_out, m_norm_ffn_g, m_w_gate, m_w_up, m_w_down, m_norm_final_g, v_norm_mix_g, v_w_in, v_ssm_conv_w, v_ssm_conv_b, v_ssm_dt_bias, v_ssm_A_log, v_ssm_D, v_ssm_norm_g, v_sc_conv_w, v_w_out, v_norm_ffn_g, v_w_gate, v_w_up, v_w_down, v_norm_final_g):
    t_len, d = x.shape[1], x.shape[2]
    heads = d // HEADDIM
    r_heads = heads // N_GROUPS
    d_xbc = d + 2 * N_GROUPS * N_STATE
    ff_s = w_down.shape[1]
    ff = ff_s * N_DEV
    off_xbc, off_dt = d, d + d_xbc
    off_cb = off_dt + heads
    d_in = off_cb + 3 * d
    in_s = d_in // N_DEV
    in_p = -(-in_s // (2 * BF16_ROWS)) * (2 * BF16_ROWS)
    w_main = 4 * d + d_xbc
    me = 4 * lax.axis_index("x") + 2 * lax.axis_index("y") + lax.axis_index("c")

    x2 = x[0]
    target = loss_target[0]

    tpose = lambda a: jnp.transpose(a[0])
    win_s = _pad_to(tpose(w_in).astype(BF16), in_p, d)
    wg_s, wu_s = tpose(w_gate).astype(BF16), tpose(w_up).astype(BF16)
    wo_s, wd_s = w_out[0].astype(BF16), w_down[0].astype(BF16)
    small_w = jnp.concatenate([_pad_to(ssm_conv_w[0], K_SSM, d_xbc // N_DEV),
                               _pad_to(sc_conv_w[0], K_SC + 1, d_xbc // N_DEV)], axis=0)

    g1, g2, g3 = norm_mix_g, norm_ffn_g, norm_final_g.reshape(1, d)
    gs = ssm_norm_g
    small = [_pad_to(ssm_dt_bias, 1, LANES), ssm_dt_bias.reshape(heads, 1), _pad_to(ssm_A_log, 1, LANES),
             ssm_A_log.reshape(heads, 1)]
    e01 = (lax.broadcasted_iota(jnp.int32, (LANES, d), 1) // HEADDIM
           == lax.broadcasted_iota(jnp.int32, (LANES, d), 0)).astype(BF16)
    dskip_e = jnp.repeat(ssm_D, HEADDIM, axis=1)
    tr = _tile(t_len, 256, 8)
    tr_ff = _tile(t_len, 128, 8)
    cw = LANES
    slab = lambda col: col // cw

    gin_1, gsm_1 = _gather_chips_relayed([win_s], [small_w], "gather_w_in_chips")
    (n1,), (gin, gsm) = _rows_call(lambda v, g: ((_rms(v, g),), ()), rows=t_len, tr=tr, row_ins=[(x2, d, 0)],
                                   full_ins=[g1], row_outs=[(d, BF16)], acc_outs=[], name="norm_mix",
                                   ride=_gather_sibling([gin_1, gsm_1]))
    in_pieces = []
    for k in range(N_DEV):
        for a, b, dst, shift in ((0, off_dt, 0, 0), (off_dt, off_cb, 1, -off_dt), (off_cb, d_in, 0, -heads)):
            s, e = max(k * in_s, a), min((k + 1) * in_s, b)
            if s < e:
                in_pieces.append((k, s - k * in_s, e - s, dst, s + shift))
    ref_row = lambda t: t if t < off_dt else t + heads
    wtm = _move_rows(gin.reshape(N_DEV * in_p, d),
                     [(ref_row(t) // in_s) * in_p + ref_row(t) % in_s for t in range(w_main)], "place_w_in")
    wtdt = jnp.zeros((LANES, d), BF16)
    for k, r0, n, dst, d0 in in_pieces:
        if dst == 1:
            wtdt = lax.dynamic_update_slice(wtdt, gin[k, r0:r0 + n], (d0, 0))
    cw_ssm = _cols_of(gsm[:, :K_SSM, :])
    cw_sc = _cols_of(gsm[:, K_SSM:K_SSM + K_SC, :d // N_DEV])

    proj, (go_1, gg_1) = _matmul(n1, wtm, tb=True, out_dtype=BF16, name="proj_main",
                                 ride=_gather_chips([wo_s, wg_s]))
    dt_raw, _ = _matmul(n1, wtdt, tb=True, out_dtype=F32, name="proj_dt")
    dt_raw_t = jnp.transpose(dt_raw[:, :heads])
    (xbc,), (go, gg) = _cols_call(_conv_silu_fwd, rows=t_len, cols=d_xbc, cw=cw, col_ins=[(proj, slab(off_xbc))],
                                  par_ins=[(cw_ssm, 0), (ssm_conv_b, 0)], col_outs=[BF16], par_outs=[],
                                  name="ssm_conv", ride=_gather_sibling([go_1, gg_1]))
    dt_c, cs_c, cs_r = _ssd_dt(dt_raw, dt_raw_t, small)
    cs_r3 = cs_r.reshape(N_GROUPS, r_heads, t_len)
    up_cut = int(ff_s * W_UP_GATHER_SPLIT) // BF16_ROWS * BF16_ROWS
    down_cut = ff_s // 2 // BF16_ROWS * BF16_ROWS
    (y_ssd, hprev), (gu_1,) = _ssd_fwd(xbc, dt_c, cs_c, cs_r3, e01, dskip_e, d_ssm=d, r_heads=r_heads,
                                       ride=_gather_chips([wu_s], rows=(0, up_cut)))

    def gate_norm(y, z, g):
        z = z.astype(F32)
        return _rms(y * (z * _sigmoid(z)), g)

    (y_mix,), _ = _rows_call(lambda y, z, g: ((gate_norm(y, z, g),), ()), rows=t_len, tr=tr,
                             row_ins=[(y_ssd, d, 0), (proj, d, 0)], full_ins=[gs], row_outs=[(d, BF16, 2 * d)],
                             acc_outs=[], name="ssm_gate_norm")
    wgt, wo = gg.reshape(ff, d), go.reshape(2 * d, d)
    sc0 = slab(d + d_xbc)
    (y_mix,), _ = _cols_call(_shortconv_fwd, rows=t_len, cols=d, cw=cw,
                             col_ins=[(proj, sc0), (proj, sc0 + slab(d)), (proj, sc0 + 2 * slab(d))],
                             par_ins=[(cw_sc, 0)], col_outs=[BF16], par_outs=[], name="shortconv",
                             into=(y_mix, slab(d)))
    h1, (gu_1, gd_1) = _matmul(y_mix, wo, out_dtype=F32, add=x2, name="out_proj", ride=_merge(
        _gather_chips([wu_s], rows=(up_cut, ff_s - up_cut), into=[gu_1]), _gather_chips([wd_s], rows=(0, down_cut))))
    (n2,), (gu,) = _rows_call(lambda v, g: ((_rms(v, g),), ()), rows=t_len, tr=tr, row_ins=[(h1, d, 0)],
                              full_ins=[g2], row_outs=[(d, BF16)], acc_outs=[], name="norm_ffn",
                              ride=_gather_sibling([gu_1]))
    wut = gu.reshape(ff, d)
    g_ff, (gd_1,) = _matmul(n2, wgt, tb=True, out_dtype=BF16, name="ffn_gate",
                            ride=_gather_chips([wd_s], rows=(down_cut, ff_s - down_cut), into=[gd_1]))
    (u_ff, a_ff), (gd,) = _matmul(n2, wut, tb=True, name="ffn_up", ride=_gather_sibling([gd_1]),
                                  post=(lambda uv, gv: (uv, gv * _sigmoid(gv) * uv), [g_ff], [BF16, BF16]),
                                  tn_max=MM_TILE_N_POST)
    wd = gd.reshape(ff, d)
    h2, _ = _matmul(a_ff, wd, out_dtype=F32, add=h1, name="ffn_down")

    def head(hv, tv, g):
        def f(hh, gg_):
            e = _rms(hh, gg_) - tv
            return (0.5 / d) * jnp.sum(e * e)
        val, (dh, dg) = jax.value_and_grad(f, argnums=(0, 1))(hv, g)
        return (dh, dh), (jnp.full((1, LANES), val, F32), dg)

    (dh2, dh2_b, loss_acc, dg3), _ = _rows_call(head, rows=t_len, tr=tr, row_ins=[(h2, d, 0), (target, d, 0)],
                                                full_ins=[g3], row_outs=[(d, F32), (d, BF16)],
                                                acc_outs=[(1, LANES), (1, d)], name="loss_head")
    loss = lax.psum(loss_acc[0, 0], ("x", "y", "c"))

    def act_bwd(dav, gv, uv):
        s = _sigmoid(gv)
        return dav * uv * (s * (1.0 + gv * (1.0 - s))), dav * gv * s

    (dg_ff, du_ff), _ = _matmul(dh2_b, wd, tb=True, name="d_ffn_gate_up",
                                post=(act_bwd, [g_ff, u_ff], [BF16, BF16]), tn_max=MM_TILE_N_POST)
    dwd, _ = _matmul(a_ff, dh2_b, ta=True, out_dtype=BF16, name="d_w_down")
    dwd8 = dwd.reshape(N_DEV, ff_s, d)
    dn2, (sib_d,) = _matmul(dg_ff, wgt, out_dtype=F32, name="d_norm_ffn_out_gate", ride=_scatter_sibling([dwd8]))
    chip_d = _chip_sum(dwd8, sib_d, name="chip_sum_w_down")
    dn2, (parts_d,) = _matmul(du_ff, wut, out_dtype=F32, add=dn2, name="d_norm_ffn_out_up",
                              ride=_scatter_chips([chip_d], rows=(0, down_cut)))
    dwg, (parts_d,) = _matmul(dg_ff, n2, ta=True, out_dtype=BF16, name="d_w_gate",
                              ride=_scatter_chips([chip_d], rows=(down_cut, ff_s - down_cut), into=[parts_d]))
    dwu, _ = _matmul(du_ff, n2, ta=True, out_dtype=BF16, name="d_w_up")
    dwg8, dwu8 = dwg.reshape(N_DEV, ff_s, d), dwu.reshape(N_DEV, ff_s, d)

    def norm_bwd(v, dn, dres, g):
        _, vjp = jax.vjp(_rms, v, g)
        dv, dg = vjp(dn)
        return (dv + dres,), (dg,)

    def norm_bwd_2(v, dn, dres, g):
        (dv,), acc = norm_bwd(v, dn, dres, g)
        return (dv, dv), acc

    (dh1, dh1_b, dg2), (sib_g, sib_u) = _rows_call(norm_bwd_2, rows=t_len, tr=tr,
                                                   row_ins=[(h1, d, 0), (dn2, d, 0), (dh2, d, 0)], full_ins=[g2],
                                                   row_outs=[(d, F32), (d, BF16)], acc_outs=[(1, d)], name="d_norm_ffn",
                                                   ride=_scatter_sibling([dwg8, dwu8]))
    chip_g = _chip_sum(dwg8, sib_g, name="chip_sum_w_gate")
    chip_u = _chip_sum(dwu8, sib_u, name="chip_sum_w_up")

    dy_mix, _ = _matmul(dh1_b, wo, tb=True, out_dtype=BF16, name="d_y_mix")
    dwo, _ = _matmul(y_mix, dh1_b, ta=True, out_dtype=BF16, name="d_w_out")
    dwo8 = dwo.reshape(N_DEV, 2 * d // N_DEV, d)
    (dgb, dgc, du, dcw_sc), (sib_o,) = _cols_call(
        _shortconv_bwd, rows=t_len, cols=d, cw=cw,
        col_ins=[(proj, sc0), (proj, sc0 + slab(d)), (proj, sc0 + 2 * slab(d)), (dy_mix, slab(d))],
        par_ins=[(cw_sc, 0)], col_outs=[BF16] * 3, par_outs=[K_SC], name="d_shortconv",
        ride=_scatter_sibling([dwo8]))
    chip_o = _chip_sum(dwo8, sib_o, name="chip_sum_w_out")

    def gate_norm_bwd(y, z, dyo, g):
        _, vjp = jax.vjp(gate_norm, y, z.astype(F32), g)
        dy, dz, dg = vjp(dyo.astype(F32))
        return (dy, dz), (dg,)

    (dy_ssd, dproj, dgs), _ = _rows_call(gate_norm_bwd, rows=t_len, tr=tr,
                                         row_ins=[(y_ssd, d, 0), (proj, d, 0), (dy_mix, d, 0)], full_ins=[gs],
                                         row_outs=[(d, F32), (d, BF16, w_main)], acc_outs=[(1, d)],
                                         name="d_ssm_gate_norm")
    (dxs, dbm, dcm, g_dt, g_csc, g_csr3, ddk), (parts_g, parts_o) = _ssd_bwd(
        xbc, dt_c, cs_c, cs_r3, e01, dskip_e, hprev, dy_ssd, d_ssm=d, r_heads=r_heads,
        ride=_scatter_chips([chip_g, chip_o]))
    ddt_c, ddt_r, dbias_r, dbias_c, dalog_r, dalog_c, ddskip = _ssd_dt(
        dt_raw, dt_raw_t, small, cots=(g_dt, g_csc, g_csr3.reshape(heads, t_len), ddk, e01))
    dcw_parts, dcb_parts, col0 = [], [], 0
    for tag, dpart in (("x", dxs), ("b", dbm), ("c", dcm)):
        (dproj, dcw_p, dcb_p), _ = _cols_call(
            _conv_silu_bwd, rows=t_len, cols=dpart.shape[1], cw=cw,
            col_ins=[(proj, slab(off_xbc + col0)), (dpart, 0)], par_ins=[(cw_ssm, slab(col0)), (ssm_conv_b, slab(col0))],
            col_outs=[BF16], par_outs=[K_SSM, 1], name="d_ssm_conv_" + tag, into=(dproj, slab(off_xbc + col0)))
        dcw_parts.append(dcw_p)
        dcb_parts.append(dcb_p)
        col0 += dpart.shape[1]
    dcw_ssm, dcb_ssm = jnp.concatenate(dcw_parts, axis=1), jnp.concatenate(dcb_parts, axis=1)
    for i, part in enumerate((dgb, dgc, du)):
        dproj = lax.dynamic_update_slice(dproj, part, (0, d + d_xbc + i * d))
    ddt = ddt_c + _pad_to(jnp.transpose(ddt_r), t_len, LANES)
    dwm, (parts_u,) = _matmul(dproj, n1, ta=True, out_dtype=BF16, name="d_w_in_main",
                              ride=_scatter_chips([chip_u]))
    dwdt, _ = _matmul(ddt, n1, ta=True, out_dtype=BF16, name="d_w_in_dt")
    own_ref = [k * in_s + i if i < in_s else -1 for k in range(N_DEV) for i in range(in_p)]
    dwin8 = _move_rows(
        dwm, [-1 if g < 0 or off_dt <= g < off_cb else (g if g < off_dt else g - heads) for g in own_ref],
        "place_d_w_in", extra=dwdt, extra_row=[g - off_dt if off_dt <= g < off_cb else -1 for g in own_ref],
    ).reshape(N_DEV, in_p, d)
    dn1, (sib_in,) = _matmul(ddt, wtdt, out_dtype=F32, name="d_norm_mix_out_dt", ride=_scatter_sibling([dwin8]))
    chip_in = _chip_sum(dwin8, sib_in, name="chip_sum_w_in")
    cut = int(in_p * W_IN_SCATTER_SPLIT) // BF16_ROWS * BF16_ROWS
    dn1, (parts_in,) = _matmul(dproj, wtm, out_dtype=F32, add=dn1, name="d_norm_mix_out",
                               ride=_scatter_chips([chip_in], rows=(0, cut)))
    (dx, dg1), (parts_in,) = _rows_call(norm_bwd, rows=t_len, tr=tr, row_ins=[(x2, d, 0), (dn1, d, 0), (dh1, d, 0)],
                                        full_ins=[g1], row_outs=[(d, F32)], acc_outs=[(1, d)], name="d_norm_mix",
                                        ride=_scatter_chips([chip_in], rows=(cut, in_p - cut), into=[parts_in]))

    wide = d_xbc
    rows_small = [dg1, dcb_ssm, dbias_r + _pad_to(dbias_c.reshape(1, heads), 1, LANES),
                  dalog_r + _pad_to(dalog_c.reshape(1, heads), 1, LANES), ddskip, dgs, dg2, dg3]
    packed = jnp.concatenate([_pad_to(r, 1, wide) for r in rows_small]
                             + [dcw_ssm, _pad_to(dcw_sc, K_SC, wide), jnp.zeros((1, wide), F32)], axis=0)
    (p_small,) = _comm(_gather_all([packed]), "gather_small_grads")

    conv_lo = me * (d_xbc // N_DEV)
    sc_lo = me * (d // N_DEV)

    def pack_state(vals):
        (nm, cb, dtb, al, dk, sg, nf, nfin, cws, scs) = vals
        rows = [_pad_to(a.reshape(1, -1), 1, wide) for a in (nm, cb, dtb, al, dk, sg, nf, nfin)]
        cws_full = lax.dynamic_update_slice(jnp.zeros((K_SSM, wide), F32), cws[0], (0, conv_lo))
        scs_full = lax.dynamic_update_slice(jnp.zeros((K_SC, wide), F32), scs[0], (0, sc_lo))
        return jnp.concatenate(rows + [cws_full, scs_full, jnp.zeros((1, wide), F32)], axis=0)

    w_small = pack_state((norm_mix_g, ssm_conv_b, ssm_dt_bias, ssm_A_log, ssm_D, ssm_norm_g, norm_ffn_g, norm_final_g,
                          ssm_conv_w, sc_conv_w))
    m_small = pack_state((m_norm_mix_g, m_ssm_conv_b, m_ssm_dt_bias, m_ssm_A_log, m_ssm_D, m_ssm_norm_g, m_norm_ffn_g,
                          m_norm_final_g, m_ssm_conv_w, m_sc_conv_w))
    v_small = pack_state((v_norm_mix_g, v_ssm_conv_b, v_ssm_dt_bias, v_ssm_A_log, v_ssm_D, v_ssm_norm_g, v_norm_ffn_g,
                          v_norm_final_g, v_ssm_conv_w, v_sc_conv_w))

    tin = lambda a: _pad_to(tpose(a), in_p, d)
    tin_back = lambda a: jnp.transpose(a[:in_s])[None]
    t_back = lambda a: jnp.transpose(a)[None]
    upd = {
        "w_in": [tin_back(o) for o in _reduce_adamw(parts_in, tin(w_in), tin(m_w_in), tin(v_w_in), name="adamw_w_in")],
        "w_out": [o[None] for o in _reduce_adamw(parts_o, w_out[0], m_w_out[0], v_w_out[0], name="adamw_w_out")],
        "w_gate": [t_back(o) for o in _reduce_adamw(parts_g, tpose(w_gate), tpose(m_w_gate), tpose(v_w_gate),
                                                    name="adamw_w_gate")],
        "w_up": [t_back(o) for o in _reduce_adamw(parts_u, tpose(w_up), tpose(m_w_up), tpose(v_w_up),
                                                  name="adamw_w_up")],
        "w_down": [o[None] for o in _reduce_adamw(parts_d, w_down[0], m_w_down[0], v_w_down[0], name="adamw_w_down")],
    }
    small_upd = _reduce_adamw(p_small, w_small, m_small, v_small, name="adamw_small")

    def unpack(packed_out):
        vec = lambda i, n, shape: packed_out[i, :n].reshape(shape)
        return {
            "norm_mix_g": vec(0, d, (1, d)), "ssm_conv_b": vec(1, d_xbc, (1, d_xbc)),
            "ssm_dt_bias": vec(2, heads, (1, heads)), "ssm_A_log": vec(3, heads, (1, heads)),
            "ssm_D": vec(4, heads, (1, heads)), "ssm_norm_g": vec(5, d, (1, d)), "norm_ffn_g": vec(6, d, (1, d)),
            "norm_final_g": vec(7, d, (d,)),
            "ssm_conv_w": lax.dynamic_slice(packed_out[8:8 + K_SSM], (0, conv_lo), (K_SSM, d_xbc // N_DEV))[None],
            "sc_conv_w": lax.dynamic_slice(packed_out[8 + K_SSM:8 + K_SSM + K_SC], (0, sc_lo), (K_SC, d // N_DEV))[None],
        }

    names = ["norm_mix_g", "w_in", "ssm_conv_w", "ssm_conv_b", "ssm_dt_bias", "ssm_A_log", "ssm_D", "ssm_norm_g",
             "sc_conv_w", "w_out", "norm_ffn_g", "w_gate", "w_up", "w_down", "norm_final_g"]
    outs = []
    for kind in range(4):
        small_k = unpack(small_upd[kind])
        for nm in names:
            outs.append(upd[nm][kind] if nm in upd else small_k[nm])
    return (loss, dx[None], *outs)
```

```python
import collections
import functools

import jax
import jax.numpy as jnp
from jax import lax
from jax.experimental import pallas as pl
from jax.experimental.pallas import tpu as pltpu

F32 = jnp.float32
BF16 = jnp.bfloat16

N_DEV = 8
N_CHIPS = 4
HEADDIM = 64
N_GROUPS = 8
N_STATE = 128
CHUNK = 128
K_SSM = 4
K_SC = 3
EPS = 1e-5
LANES = 128
BF16_ROWS = 16
MM_TILE_MN = 1408
MM_TILE_K = 2816
W_IN_SCATTER_SPLIT = 6 / 7
W_UP_GATHER_SPLIT = 0.8
W_DOWN_GATHER_SPLIT = 0.4
MM_TILE_N_POST = 704
SSD_CHUNKS_PER_STEP = 4
SSD_GROUPS_PER_STEP = 2
ROW_BLOCK = 256
V7X_VMEM_BYTES = 64 * 1024 * 1024
VMEM_LIMIT = (V7X_VMEM_BYTES * 3) // 4

ADAM_LR = 0.001
ADAM_B1 = 0.9
ADAM_B2 = 0.999
ADAM_EPS = 1e-08
ADAM_WD = 0.01
ADAM_STEP = 10


def _tile(n, pref, align):
    t = min(pref, n)
    t -= t % align
    while t >= align:
        if n % t == 0:
            return t
        t -= align
    return n


_Ride = collections.namedtuple("_Ride", ["ins", "out_shapes", "aliases", "nsem", "plan"])
_ANY = pl.BlockSpec(memory_space=pl.ANY)


def _coords():
    return lax.axis_index("x"), lax.axis_index("y"), lax.axis_index("c")


def _other_chips(x, y):
    return ((1 - x, y), (x, 1 - y), (1 - x, 1 - y))


def _remote(src, dst, send, recv, k, dev):
    return functools.partial(pltpu.make_async_remote_copy, src_ref=src, dst_ref=dst, send_sem=send.at[k],
                             recv_sem=recv.at[k], device_id=dev, device_id_type=pl.DeviceIdType.MESH)


def _local(src, dst, sem):
    return functools.partial(pltpu.make_async_copy, src, dst, sem)


def _start_all(plan):
    for kind, make in plan:
        if kind != "arrival":
            make().start()


def _wait_all(plan):
    for kind, make in plan:
        if kind == "local":
            make().wait()
        elif kind == "out":
            make().wait_send()
        else:
            make().wait_recv()


def _gather_chips(srcs, rows=None, into=None):
    n = len(srcs)

    def plan(ins, outs, send, recv, base):
        x, y, c = _coords()
        me = 4 * x + 2 * y + c
        cut = (lambda ref: ref) if rows is None else (lambda ref: ref.at[pl.ds(rows[0], rows[1])])
        d = []
        for a, (src, dst) in enumerate(zip(ins[:n], outs)):
            k = base + 4 * a
            d.append(("local", _local(cut(src), cut(dst.at[me]), send.at[k + 3])))
            for j, (px, py) in enumerate(_other_chips(x, y)):
                d.append(("out", _remote(cut(src), cut(dst.at[me]), send, recv, k + j, (px, py, c))))
                d.append(("arrival", _remote(cut(src), cut(dst.at[4 * px + 2 * py + c]), send, recv, k + j,
                                             (px, py, c))))
        return d
    shapes = [jax.ShapeDtypeStruct((N_DEV,) + s.shape, s.dtype) for s in srcs]
    if into is None:
        return _Ride(list(srcs), shapes, {}, 4 * n, plan)
    return _Ride(list(srcs) + list(into), shapes, {n + a: a for a in range(n)}, 4 * n, plan)


def _gather_sibling(bufs):
    def plan(ins, outs, send, recv, base):
        x, y, c = _coords()
        d = []
        for a, buf in enumerate(outs):
            for q in range(N_CHIPS):
                k = base + 4 * a + q
                d.append(("out", _remote(buf.at[2 * q + c], buf.at[2 * q + c], send, recv, k, (x, y, 1 - c))))
                d.append(("arrival", _remote(buf.at[2 * q + c], buf.at[2 * q + 1 - c], send, recv, k, (x, y, 1 - c))))
        return d
    shapes = [jax.ShapeDtypeStruct(b.shape, b.dtype) for b in bufs]
    return _Ride(list(bufs), shapes, {i: i for i in range(len(bufs))}, 4 * len(bufs), plan)


def _scatter_sibling(srcs):
    def plan(ins, outs, send, recv, base):
        x, y, c = _coords()
        d = []
        for a, (src, sib) in enumerate(zip(ins, outs)):
            for q in range(N_CHIPS):
                k = base + 4 * a + q
                d.append(("out", _remote(src.at[2 * q + 1 - c], sib.at[q], send, recv, k, (x, y, 1 - c))))
                d.append(("arrival", _remote(src.at[2 * q + 1 - c], sib.at[q], send, recv, k, (x, y, 1 - c))))
        return d
    shapes = [jax.ShapeDtypeStruct((N_CHIPS,) + s.shape[1:], s.dtype) for s in srcs]
    return _Ride(list(srcs), shapes, {}, 4 * len(srcs), plan)


def _scatter_chips(chips, rows=None, into=None):
    n = len(chips)

    def plan(ins, outs, send, recv, base):
        x, y, c = _coords()
        mine = 2 * x + y
        cut = (lambda ref: ref) if rows is None else (lambda ref: ref.at[pl.ds(rows[0], rows[1])])
        d = []
        for a, (chip, parts) in enumerate(zip(ins[:n], outs)):
            k = base + 4 * a
            d.append(("local", _local(cut(chip.at[mine]), cut(parts.at[mine]), send.at[k + 3])))
            for j, (px, py) in enumerate(_other_chips(x, y)):
                q = 2 * px + py
                d.append(("out", _remote(cut(chip.at[q]), cut(parts.at[mine]), send, recv, k + j, (px, py, c))))
                d.append(("arrival", _remote(cut(chip.at[q]), cut(parts.at[q]), send, recv, k + j, (px, py, c))))
        return d
    shapes = [jax.ShapeDtypeStruct(s.shape, s.dtype) for s in chips]
    if into is None:
        return _Ride(list(chips), shapes, {}, 4 * n, plan)
    return _Ride(list(chips) + list(into), shapes, {n + a: a for a in range(n)}, 4 * n, plan)


def _gather_all(srcs):
    def plan(ins, outs, send, recv, base):
        x, y, c = _coords()
        me = 4 * x + 2 * y + c
        d = []
        for a, (src, dst) in enumerate(zip(ins, outs)):
            k = base + N_DEV * a
            d.append(("local", _local(src, dst.at[me], send.at[k])))
            for j in range(1, N_DEV):
                px = 1 - x if (j >> 2) & 1 else x
                py = 1 - y if (j >> 1) & 1 else y
                pc = 1 - c if j & 1 else c
                d.append(("out", _remote(src, dst.at[me], send, recv, k + j, (px, py, pc))))
                d.append(("arrival", _remote(src, dst.at[4 * px + 2 * py + pc], send, recv, k + j, (px, py, pc))))
        return d
    shapes = [jax.ShapeDtypeStruct((N_DEV,) + s.shape, s.dtype) for s in srcs]
    return _Ride(list(srcs), shapes, {}, N_DEV * len(srcs), plan)


def _merge(*rides):
    ins, outs, aliases, parts, nsem = [], [], {}, [], 0
    for r in rides:
        parts.append((len(ins), len(outs), nsem, r))
        aliases.update({len(ins) + i: len(outs) + j for i, j in r.aliases.items()})
        ins += r.ins
        outs += r.out_shapes
        nsem += r.nsem

    def plan(i, o, send, recv, base):
        d = []
        for i0, o0, s0, r in parts:
            d += r.plan(i[i0:i0 + len(r.ins)], o[o0:o0 + len(r.out_shapes)], send, recv, base + s0)
        return d
    return _Ride(ins, outs, aliases, nsem, plan)


def _comm(ride, name):
    n_in, n_out = len(ride.ins), len(ride.out_shapes)

    def body(*refs):
        plan = ride.plan(refs[:n_in], refs[n_in:n_in + n_out], refs[-2], refs[-1], 0)
        _start_all(plan)
        _wait_all(plan)

    return pl.pallas_call(
        body, name=name, in_specs=[_ANY] * n_in, out_specs=[_ANY] * n_out, out_shape=ride.out_shapes,
        scratch_shapes=[pltpu.SemaphoreType.DMA((ride.nsem,)), pltpu.SemaphoreType.DMA((ride.nsem,))],
        input_output_aliases=dict(ride.aliases),
        compiler_params=pltpu.CompilerParams(has_side_effects=True),
    )(*ride.ins)


def _gather_chips_relayed(big, small, name):
    srcs = list(big) + list(small)
    n, nsem = len(srcs), 5 * len(srcs)

    def body(*refs):
        ins, outs, send, recv = refs[:n], refs[n:2 * n], refs[-2], refs[-1]
        x, y, c = _coords()
        slot = lambda dev: 4 * dev[0] + 2 * dev[1] + dev[2]
        me, nbr_x, nbr_y, diag = (x, y, c), (1 - x, y, c), (x, 1 - y, c), (1 - x, 1 - y, c)
        own, sends = [], []
        for a, (src, dst) in enumerate(zip(ins, outs)):
            k = 5 * a
            own.append(_local(src, dst.at[slot(me)], send.at[k + 4])())
            sends.append(_remote(src, dst.at[slot(me)], send, recv, k, nbr_x)())
            sends.append(_remote(src, dst.at[slot(me)], send, recv, k + 1, nbr_y)())
            if a >= len(big):
                sends.append(_remote(src, dst.at[slot(me)], send, recv, k + 2, diag)())
        for s in own + sends:
            s.start()
        for a, (src, dst) in enumerate(zip(ins, outs)):
            k = 5 * a
            _remote(src, dst.at[slot(nbr_x)], send, recv, k, nbr_x)().wait_recv()
            if a < len(big):
                half = src.shape[0] // 2
                part = dst.at[slot(nbr_x)].at[pl.ds(0, half)]
                fwd = _remote(part, part, send, recv, k + 2, nbr_y)()
                fwd.start()
                sends.append(fwd)
            _remote(src, dst.at[slot(nbr_y)], send, recv, k + 1, nbr_y)().wait_recv()
            if a < len(big):
                part = dst.at[slot(nbr_y)].at[pl.ds(half, src.shape[0] - half)]
                fwd = _remote(part, part, send, recv, k + 3, nbr_x)()
                fwd.start()
                sends.append(fwd)
        for a, (src, dst) in enumerate(zip(ins, outs)):
            k = 5 * a
            if a < len(big):
                half = src.shape[0] // 2
                lo = dst.at[slot(diag)].at[pl.ds(0, half)]
                hi = dst.at[slot(diag)].at[pl.ds(half, src.shape[0] - half)]
                _remote(lo, lo, send, recv, k + 2, nbr_y)().wait_recv()
                _remote(hi, hi, send, recv, k + 3, nbr_x)().wait_recv()
            else:
                _remote(src, dst.at[slot(diag)], send, recv, k + 2, diag)().wait_recv()
        for lc in own:
            lc.wait()
        for s in sends:
            s.wait_send()

    return pl.pallas_call(
        body, name=name, in_specs=[_ANY] * n, out_specs=[_ANY] * n,
        out_shape=[jax.ShapeDtypeStruct((N_DEV,) + s.shape, s.dtype) for s in srcs],
        scratch_shapes=[pltpu.SemaphoreType.DMA((nsem,)), pltpu.SemaphoreType.DMA((nsem,))],
        compiler_params=pltpu.CompilerParams(has_side_effects=True),
    )(*srcs)


def _call(body, *, name, grid, in_specs, out_specs, out_shape, args, sem, scratch=(), ride=None, base=None):
    params = pltpu.CompilerParams(dimension_semantics=sem, vmem_limit_bytes=VMEM_LIMIT)
    own_aliases = {}
    if base is not None:
        inner, n_host = body, len(args)
        body = lambda *refs: inner(*refs[:n_host], *refs[n_host + 1:])
        own_aliases[n_host] = base[1]
        args, in_specs = tuple(args) + (base[0],), list(in_specs) + [_ANY]
    if ride is None:
        res = pl.pallas_call(body, name=name, grid=grid, in_specs=in_specs, out_specs=out_specs,
                             out_shape=out_shape, scratch_shapes=list(scratch), input_output_aliases=own_aliases,
                             compiler_params=params)(*args)
        return list(res), []
    n_in, n_out, n_scr = len(args), len(out_shape), len(scratch)
    r_in, r_out = len(ride.ins), len(ride.out_shapes)

    def hosted(*refs):
        h_in, rin = refs[:n_in], refs[n_in:n_in + r_in]
        o0 = n_in + r_in
        h_out, rout = refs[o0:o0 + n_out], refs[o0 + n_out:o0 + n_out + r_out]
        s0 = o0 + n_out + r_out
        h_scr, send, recv = refs[s0:s0 + n_scr], refs[s0 + n_scr], refs[s0 + n_scr + 1]
        ids = [pl.program_id(i) for i in range(len(grid))]
        first = functools.reduce(lambda p, q: p & q, [i == 0 for i in ids])
        last = functools.reduce(lambda p, q: p & q, [i == n - 1 for i, n in zip(ids, grid)])

        @pl.when(first)
        def _():
            _start_all(ride.plan(rin, rout, send, recv, 0))

        body(*h_in, *h_out, *h_scr)

        @pl.when(last)
        def _():
            _wait_all(ride.plan(rin, rout, send, recv, 0))

    res = pl.pallas_call(
        hosted, name=name, grid=grid, in_specs=list(in_specs) + [_ANY] * r_in,
        out_specs=list(out_specs) + [_ANY] * r_out, out_shape=list(out_shape) + list(ride.out_shapes),
        scratch_shapes=list(scratch) + [pltpu.SemaphoreType.DMA((ride.nsem,)), pltpu.SemaphoreType.DMA((ride.nsem,))],
        input_output_aliases={**own_aliases, **{n_in + i: n_out + j for i, j in ride.aliases.items()}},
        compiler_params=params,
    )(*args, *ride.ins)
    return list(res[:n_out]), list(res[n_out:])


def _matmul(a, b, *, ta=False, tb=False, out_dtype=BF16, add=None, post=None, name, ride=None, tn_max=MM_TILE_MN):
    m = a.shape[1] if ta else a.shape[0]
    k = a.shape[0] if ta else a.shape[1]
    n = b.shape[0] if tb else b.shape[1]
    assert k == (b.shape[1] if tb else b.shape[0])
    tm, tn, tk = _tile(m, MM_TILE_MN, LANES), _tile(n, tn_max, LANES), _tile(k, MM_TILE_K, LANES)
    nk = k // tk
    dims = (((0 if ta else 1,), (1 if tb else 0,)), ((), ()))
    single = post is None
    if add is not None:
        post = (lambda r, t: (r + t,), [add], [out_dtype])
    elif post is None:
        post = (lambda r: (r,), [], [out_dtype])
    post_fn, extras, out_dtypes = post
    n_ex, n_o = len(extras), len(out_dtypes)

    def body(*refs):
        a_ref, b_ref = refs[:2]
        ex_refs, o_refs = refs[2:2 + n_ex], refs[2 + n_ex:2 + n_ex + n_o]

        def finish(r):
            for o_ref, v in zip(o_refs, post_fn(r, *[e[...].astype(F32) for e in ex_refs])):
                o_ref[...] = v.astype(o_ref.dtype)

        part = lax.dot_general(a_ref[...].astype(BF16), b_ref[...].astype(BF16), dims, preferred_element_type=F32)
        if nk == 1:
            finish(part)
            return
        acc = refs[-1]
        kk = pl.program_id(2)

        @pl.when(kk == 0)
        def _():
            acc[...] = part

        @pl.when((kk > 0) & (kk < nk - 1))
        def _():
            acc[...] += part

        @pl.when(kk == nk - 1)
        def _():
            finish(acc[...] + part)

    a_spec = (pl.BlockSpec((tk, tm), lambda i, j, kk: (kk, i)) if ta
              else pl.BlockSpec((tm, tk), lambda i, j, kk: (i, kk)))
    b_spec = (pl.BlockSpec((tn, tk), lambda i, j, kk: (j, kk)) if tb
              else pl.BlockSpec((tk, tn), lambda i, j, kk: (kk, j)))
    o_spec = pl.BlockSpec((tm, tn), lambda i, j, kk: (i, j))
    outs, rides = _call(
        body, name=name, grid=(m // tm, n // tn, nk),
        in_specs=[a_spec, b_spec] + [o_spec] * n_ex, out_specs=[o_spec] * n_o,
        out_shape=[jax.ShapeDtypeStruct((m, n), dt) for dt in out_dtypes], args=(a, b, *extras),
        scratch=[pltpu.VMEM((tm, tn), F32)] if nk > 1 else [], sem=("parallel", "parallel", "arbitrary"), ride=ride)
    return (outs[0] if single else outs), rides


def _rows_call(fn, *, rows, tr, row_ins, full_ins, row_outs, acc_outs, name, ride=None):
    nr, nf, no, na = len(row_ins), len(full_ins), len(row_outs), len(acc_outs)

    def body(*refs):
        vals = [r[...] for r in refs[:nr + nf]]
        outs, accs = fn(*vals)
        for r, v in zip(refs[nr + nf:nr + nf + no], outs):
            r[...] = v.astype(r.dtype)
        if na:
            @pl.when(pl.program_id(0) == 0)
            def _():
                for r in refs[nr + nf + no:]:
                    r[...] = jnp.zeros_like(r)
            for r, v in zip(refs[nr + nf + no:], accs):
                r[...] += v

    in_specs = [pl.BlockSpec((tr, w), functools.partial(lambda cb, i: (i, cb), cb)) for _, w, cb in row_ins]
    in_specs += [pl.BlockSpec(f.shape, lambda i: (0, 0)) for f in full_ins]
    out_specs = [pl.BlockSpec((tr, o[0]), lambda i: (i, 0)) for o in row_outs]
    out_specs += [pl.BlockSpec(s, lambda i: (0, 0)) for s in acc_outs]
    out_shape = [jax.ShapeDtypeStruct((rows, o[-1] if len(o) == 3 else o[0]), o[1]) for o in row_outs]
    out_shape += [jax.ShapeDtypeStruct(s, F32) for s in acc_outs]
    return _call(body, name=name, grid=(rows // tr,), in_specs=in_specs, out_specs=out_specs, out_shape=out_shape,
                 args=tuple(a for a, _, _ in row_ins) + tuple(full_ins), sem=("arbitrary",), ride=ride)


def _cols_call(fn, *, rows, cols, cw, col_ins, par_ins, col_outs, par_outs, name, ride=None, into=None):
    nc, npar = len(col_ins), len(par_ins)

    def body(*refs):
        vals = [r[...] for r in refs[:nc + npar]]
        outs, pouts = fn(*vals)
        for r, v in zip(refs[nc + npar:], tuple(outs) + tuple(pouts)):
            r[...] = v.astype(r.dtype)

    in_specs = [pl.BlockSpec((rows, cw), functools.partial(lambda off, j: (0, off + j), off)) for _, off in col_ins]
    in_specs += [pl.BlockSpec((p.shape[0], cw), functools.partial(lambda off, j: (0, off + j), off))
                 for p, off in par_ins]
    out_specs = [pl.BlockSpec((rows, cw), lambda j: (0, j)) for _ in col_outs]
    out_specs += [pl.BlockSpec((k, cw), lambda j: (0, j)) for k in par_outs]
    out_shape = [jax.ShapeDtypeStruct((rows, cols), dt) for dt in col_outs]
    out_shape += [jax.ShapeDtypeStruct((k, cols), F32) for k in par_outs]
    if into is not None:
        out_specs[0] = pl.BlockSpec((rows, cw), lambda j: (0, into[1] + j))
        out_shape[0] = jax.ShapeDtypeStruct(into[0].shape, into[0].dtype)
    return _call(body, name=name, grid=(cols // cw,), in_specs=in_specs, out_specs=out_specs, out_shape=out_shape,
                 args=tuple(a for a, _ in col_ins) + tuple(p for p, _ in par_ins), sem=("arbitrary",), ride=ride,
                 base=None if into is None else (into[0], 0))


def _sigmoid(v):
    return 1.0 / (1.0 + jnp.exp(-v))


def _softplus(v):
    return jnp.maximum(v, 0.0) + jnp.log(1.0 + jnp.exp(-jnp.abs(v)))


def _rms(v, g):
    return v * lax.rsqrt(jnp.mean(v * v, axis=-1, keepdims=True) + EPS) * g


def _shift_down(v, s, row):
    return jnp.where(row >= s, pltpu.roll(v, s, 0), 0.0)


def _shift_up(v, s, row):
    n = v.shape[0]
    return jnp.where(row < n - s, pltpu.roll(v, n - s, 0), 0.0)


def _causal_conv(u, w, row):
    k_taps = w.shape[0]
    acc = u * w[k_taps - 1:k_taps, :]
    for k in range(k_taps - 1):
        acc = acc + _shift_down(u, k_taps - 1 - k, row) * w[k:k + 1, :]
    return acc


def _causal_conv_bwd(u, dy, w, row):
    k_taps = w.shape[0]
    tap = lax.broadcasted_iota(jnp.int32, w.shape, 0)
    du = dy * w[k_taps - 1:k_taps, :]
    dw = jnp.where(tap == k_taps - 1, jnp.sum(dy * u, axis=0, keepdims=True), 0.0)
    for k in range(k_taps - 1):
        s = k_taps - 1 - k
        du = du + _shift_up(dy, s, row) * w[k:k + 1, :]
        dw = dw + jnp.where(tap == k, jnp.sum(dy * _shift_down(u, s, row), axis=0, keepdims=True), 0.0)
    return du, dw


def _conv_silu_fwd(u, w, b):
    u = u.astype(F32)
    row = lax.broadcasted_iota(jnp.int32, u.shape, 0)
    pre = _causal_conv(u, w, row) + b
    return (pre * _sigmoid(pre),), ()


def _conv_silu_bwd(u, dy, w, b):
    u = u.astype(F32)
    dy = dy.astype(F32)
    row = lax.broadcasted_iota(jnp.int32, u.shape, 0)
    pre = _causal_conv(u, w, row) + b
    s = _sigmoid(pre)
    dpre = dy * (s * (1.0 + pre * (1.0 - s)))
    du, dw = _causal_conv_bwd(u, dpre, w, row)
    return (du,), (dw, jnp.sum(dpre, axis=0, keepdims=True))


def _shortconv_fwd(gb, gc, u, w):
    gb, gc, u = gb.astype(F32), gc.astype(F32), u.astype(F32)
    row = lax.broadcasted_iota(jnp.int32, u.shape, 0)
    return (gb * _causal_conv(gc * u, w, row),), ()


def _shortconv_bwd(gb, gc, u, dy, w):
    gb, gc, u, dy = gb.astype(F32), gc.astype(F32), u.astype(F32), dy.astype(F32)
    row = lax.broadcasted_iota(jnp.int32, u.shape, 0)
    v = gc * u
    dgb = dy * _causal_conv(v, w, row)
    dv, dw = _causal_conv_bwd(v, dy * gb, w, row)
    return (dgb, dv * u, dv * gc), (dw,)


def _split3(v):
    hi = v.astype(BF16)
    r1 = v - hi.astype(F32)
    mid = r1.astype(BF16)
    lo = (r1 - mid.astype(F32)).astype(BF16)
    return hi, mid, lo


def _exact_dot(v, m01, dims, v_is_lhs):
    def one(p):
        return (lax.dot_general(p, m01, dims, preferred_element_type=F32) if v_is_lhs
                else lax.dot_general(m01, p, dims, preferred_element_type=F32))
    hi, mid, lo = _split3(v)
    return (one(lo) + one(mid)) + one(hi)


_NN = (((1,), (0,)), ((), ()))
_NT = (((1,), (1,)), ((), ()))
_TN = (((0,), (0,)), ((), ()))


@jax.custom_vjp
def _cumsum_rows(tril, v):
    return _exact_dot(v, tril, _NN, False)


def _cumsum_rows_fwd(tril, v):
    return _cumsum_rows(tril, v), tril


def _cumsum_rows_bwd(tril, ct):
    return None, _exact_dot(ct, tril, _TN, False)


_cumsum_rows.defvjp(_cumsum_rows_fwd, _cumsum_rows_bwd)


@jax.custom_vjp
def _cumsum_lanes(tril, v):
    return _exact_dot(v, tril, _NT, True)


def _cumsum_lanes_fwd(tril, v):
    return _cumsum_lanes(tril, v), tril


def _cumsum_lanes_bwd(tril, ct):
    return None, _exact_dot(ct, tril, _NN, True)


_cumsum_lanes.defvjp(_cumsum_lanes_fwd, _cumsum_lanes_bwd)


@jax.custom_vjp
def _expand(e01, v):
    return _exact_dot(v, e01, _NN, True)


def _expand_fwd(e01, v):
    return _expand(e01, v), e01


def _expand_bwd(e01, ct):
    return None, _exact_dot(ct, e01, _NT, True)


_expand.defvjp(_expand_fwd, _expand_bwd)


def _causal_mask(n):
    li = lax.broadcasted_iota(jnp.int32, (n, n), 0)
    si = lax.broadcasted_iota(jnp.int32, (n, n), 1)
    return si <= li


def _dt_prep(dtc, dtr, bias_r, bias_c, alog_r, alog_c):
    dt_c = _softplus(dtc + bias_r)
    dt_r = _softplus(dtr + bias_c)
    tril = jnp.where(_causal_mask(dtc.shape[0]), 1.0, 0.0).astype(BF16)
    cs_c = _cumsum_rows(tril, dt_c * (-jnp.exp(alog_r)))
    cs_r = _cumsum_lanes(tril, dt_r * (-jnp.exp(alog_c)))
    return dt_c, cs_c, cs_r


def _ssd_chunk(r_heads, xs, bg, cg, dt_c, cs_c, cs_rg, e01, dskip_e, hp):
    l_len, rp = xs.shape
    p = rp // r_heads
    causal = _causal_mask(l_len)
    lane_head = lax.broadcasted_iota(jnp.int32, (1, rp), 1) // p
    dt_e = _expand(e01, dt_c)
    cs_e = _expand(e01, cs_c)
    cl_e = cs_e[l_len - 1:l_len, :]
    x = xs * dt_e
    bgb, cgb = bg.astype(BF16), cg.astype(BF16)
    cb = lax.dot_general(cgb, bgb, _NT, preferred_element_type=F32)
    ms, xm = [], []
    for r in range(r_heads):
        seg = cs_e[:, r * p:r * p + 1] - cs_rg[r:r + 1, :]
        decay = jnp.exp(jnp.where(causal, seg, -1e30))
        ms.append((cb * decay).astype(BF16))
        xm.append(jnp.where(lane_head == r, x, 0.0).astype(BF16))
    y_diag = lax.dot_general(jnp.concatenate(ms, axis=1), jnp.concatenate(xm, axis=0), _NN,
                             preferred_element_type=F32)
    y_off = lax.dot_general(cgb, hp.astype(BF16), _NN, preferred_element_type=F32) * jnp.exp(cs_e)
    xd = (x * jnp.exp(cl_e - cs_e)).astype(BF16)
    states = lax.dot_general(bgb, xd, _TN, preferred_element_type=F32)
    h_next = hp * jnp.exp(cl_e) + states
    y = y_diag + y_off + dskip_e * xs
    return y, h_next


def _ssd_dt(dtc, dtr, small, cots=None):
    t_len, heads = dtc.shape[0], dtr.shape[0]
    nc = t_len // CHUNK
    col = pl.BlockSpec((CHUNK, LANES), lambda c: (c, 0))
    row = pl.BlockSpec((heads, CHUNK), lambda c: (0, c))
    full = [pl.BlockSpec(s.shape, lambda c: (0, 0)) for s in small]
    shapes = [jax.ShapeDtypeStruct((t_len, LANES), F32), jax.ShapeDtypeStruct((t_len, LANES), F32),
              jax.ShapeDtypeStruct((heads, t_len), F32)]
    if cots is None:
        def body(dtc_ref, dtr_ref, br, bc, ar, ac, dt_ref, csc_ref, csr_ref):
            dt_ref[...], csc_ref[...], csr_ref[...] = _dt_prep(dtc_ref[...], dtr_ref[...], br[...], bc[...],
                                                                ar[...], ac[...])
        return _call(body, name="ssd_dt", grid=(nc,), in_specs=[col, row] + full, out_specs=[col, col, row],
                     out_shape=shapes, args=(dtc, dtr, *small), sem=("parallel",))[0]

    g_dt, g_csc, g_csr, ddk, e01 = cots

    def body(dtc_ref, dtr_ref, br, bc, ar, ac, g_dt_ref, g_csc_ref, g_csr_ref, ddk_ref, e_ref,
             ddtc_ref, ddtr_ref, *dsmall):
        _, vjp = jax.vjp(_dt_prep, dtc_ref[...], dtr_ref[...], br[...], bc[...], ar[...], ac[...])
        grads = vjp((g_dt_ref[...], g_csc_ref[...], g_csr_ref[...]))
        ddtc_ref[...], ddtr_ref[...] = grads[0], grads[1]
        ddk8 = jnp.broadcast_to(ddk_ref[...], (8, ddk_ref.shape[1]))
        dskip = _exact_dot(ddk8, e_ref[...], _NT, True)[0:1, :]

        @pl.when(pl.program_id(0) == 0)
        def _():
            for r in dsmall:
                r[...] = jnp.zeros_like(r)

        for r, gr in zip(dsmall, tuple(grads[2:]) + (dskip,)):
            r[...] += gr

    acc = list(small) + [small[0]]
    return _call(body, name="d_ssd_dt", grid=(nc,),
                 in_specs=[col, row] + full + [col, col, row, pl.BlockSpec((None, 1, e01.shape[1]), lambda c: (c, 0, 0)),
                                               pl.BlockSpec(e01.shape, lambda c: (0, 0))],
                 out_specs=[col, row] + [pl.BlockSpec(s.shape, lambda c: (0, 0)) for s in acc],
                 out_shape=[shapes[0], shapes[2]] + [jax.ShapeDtypeStruct(s.shape, F32) for s in acc],
                 args=(dtc, dtr, *small, g_dt, g_csc, g_csr, ddk, e01), sem=("arbitrary",))[0]


def _ssd_specs(t_len, d_ssm, r_heads, reverse):
    rp = r_heads * HEADDIM
    nc = t_len // CHUNK
    per = next(p for p in (SSD_CHUNKS_PER_STEP, 2, 1) if nc % p == 0)
    ns, rows, gs = nc // per, per * CHUNK, SSD_GROUPS_PER_STEP
    cidx = (lambda c: ns - 1 - c) if reverse else (lambda c: c)
    b_off = d_ssm // (N_STATE * gs)
    specs = dict(
        xs=pl.BlockSpec((rows, gs * rp), lambda c, g: (cidx(c), g)),
        b=pl.BlockSpec((rows, gs * N_STATE), lambda c, g: (cidx(c), b_off + g)),
        c=pl.BlockSpec((rows, gs * N_STATE), lambda c, g: (cidx(c), b_off + N_GROUPS // gs + g)),
        grad_bc=pl.BlockSpec((rows, gs * N_STATE), lambda c, g: (cidx(c), g)),
        col=pl.BlockSpec((rows, LANES), lambda c, g: (cidx(c), 0)),
        csr=pl.BlockSpec((gs, r_heads, rows), lambda c, g: (g, 0, cidx(c))),
        e01=pl.BlockSpec((LANES, gs * rp), lambda c, g: (0, g)),
        dskip=pl.BlockSpec((1, gs * rp), lambda c, g: (0, g)),
        hprev=pl.BlockSpec((per, gs, N_STATE, rp), lambda c, g: (cidx(c), g, 0, 0)),
        ddk=pl.BlockSpec((per, 1, gs * rp), lambda c, g: (cidx(c), 0, g)),
    )
    return specs, nc, ns, per, rp


def _ssd_fwd(xbc, dt_c, cs_c, cs_r3, e01, dskip_e, *, d_ssm, r_heads, ride=None):
    t_len = xbc.shape[0]
    sp, nc, ns, per, rp = _ssd_specs(t_len, d_ssm, r_heads, False)

    def body(xs_ref, b_ref, c_ref, dt_ref, csc_ref, csr_ref, e_ref, dk_ref, y_ref, hprev_ref, h_ref):
        c, gp = pl.program_id(0), pl.program_id(1)
        groups = [gp * SSD_GROUPS_PER_STEP + gi for gi in range(SSD_GROUPS_PER_STEP)]

        @pl.when(c == 0)
        def _():
            for g in groups:
                h_ref[g] = jnp.zeros((N_STATE, rp), F32)

        hp = [h_ref[g] for g in groups]
        for s in range(per):
            r = pl.ds(s * CHUNK, CHUNK)
            for gi in range(SSD_GROUPS_PER_STEP):
                cols, bc = pl.ds(gi * rp, rp), pl.ds(gi * N_STATE, N_STATE)
                hprev_ref[s, gi] = hp[gi]
                y, hp[gi] = _ssd_chunk(r_heads, xs_ref[r, cols].astype(F32), b_ref[r, bc].astype(F32),
                                       c_ref[r, bc].astype(F32), dt_ref[r, :], csc_ref[r, :], csr_ref[gi, :, r],
                                       e_ref[:, cols], dk_ref[:, cols], hp[gi])
                y_ref[r, cols] = y
        for gi, g in enumerate(groups):
            h_ref[g] = hp[gi]

    return _call(
        body, name="ssd_fwd", grid=(ns, N_GROUPS // SSD_GROUPS_PER_STEP),
        in_specs=[sp["xs"], sp["b"], sp["c"], sp["col"], sp["col"], sp["csr"], sp["e01"], sp["dskip"]],
        out_specs=[sp["xs"], sp["hprev"]],
        out_shape=[jax.ShapeDtypeStruct((t_len, d_ssm), F32),
                   jax.ShapeDtypeStruct((nc, N_GROUPS, N_STATE, rp), F32)],
        args=(xbc, xbc, xbc, dt_c, cs_c, cs_r3, e01, dskip_e), scratch=[pltpu.VMEM((N_GROUPS, N_STATE, rp), F32)],
        sem=("arbitrary", "arbitrary"), ride=ride)


def _ssd_bwd(xbc, dt_c, cs_c, cs_r3, e01, dskip_e, hprev, dy, *, d_ssm, r_heads, ride=None):
    t_len = xbc.shape[0]
    sp, nc, ns, per, rp = _ssd_specs(t_len, d_ssm, r_heads, True)

    def body(xs_ref, b_ref, c_ref, dt_ref, csc_ref, csr_ref, e_ref, dk_ref, hprev_ref, dy_ref,
             dxs_ref, db_ref, dc_ref, ddt_ref, dcsc_ref, dcsr_ref, ddk_ref, dh_ref):
        c, gp = pl.program_id(0), pl.program_id(1)
        groups = [gp * SSD_GROUPS_PER_STEP + gi for gi in range(SSD_GROUPS_PER_STEP)]

        @pl.when(gp == 0)
        def _():
            ddt_ref[...] = jnp.zeros_like(ddt_ref)
            dcsc_ref[...] = jnp.zeros_like(dcsc_ref)

        @pl.when(c == 0)
        def _():
            for g in groups:
                dh_ref[g] = jnp.zeros((N_STATE, rp), F32)

        dh = [dh_ref[g] for g in groups]
        for s in reversed(range(per)):
            r = pl.ds(s * CHUNK, CHUNK)
            ddt_sum, dcsc_sum = ddt_ref[r, :], dcsc_ref[r, :]
            for gi in range(SSD_GROUPS_PER_STEP):
                cols, bc = pl.ds(gi * rp, rp), pl.ds(gi * N_STATE, N_STATE)
                e01 = e_ref[:, cols]
                fn = lambda xs, bg, cg, dt, csc, csr, dk, hp: _ssd_chunk(r_heads, xs, bg, cg, dt, csc, csr, e01, dk, hp)
                _, vjp = jax.vjp(fn, xs_ref[r, cols].astype(F32), b_ref[r, bc].astype(F32), c_ref[r, bc].astype(F32),
                                 dt_ref[r, :], csc_ref[r, :], csr_ref[gi, :, r], dk_ref[:, cols], hprev_ref[s, gi])
                dxs, dbg, dcg, ddt, dcsc, dcsr, ddk, dh[gi] = vjp((dy_ref[r, cols], dh[gi]))
                dxs_ref[r, cols] = dxs.astype(dxs_ref.dtype)
                db_ref[r, bc] = dbg.astype(db_ref.dtype)
                dc_ref[r, bc] = dcg.astype(dc_ref.dtype)
                ddt_sum, dcsc_sum = ddt_sum + ddt, dcsc_sum + dcsc
                dcsr_ref[gi, :, r] = dcsr
                ddk_ref[s, :, cols] = ddk
            ddt_ref[r, :], dcsc_ref[r, :] = ddt_sum, dcsc_sum
        for gi, g in enumerate(groups):
            dh_ref[g] = dh[gi]

    n_bc = N_GROUPS * N_STATE
    return _call(
        body, name="ssd_bwd", grid=(ns, N_GROUPS // SSD_GROUPS_PER_STEP),
        in_specs=[sp["xs"], sp["b"], sp["c"], sp["col"], sp["col"], sp["csr"], sp["e01"], sp["dskip"], sp["hprev"],
                  sp["xs"]],
        out_specs=[sp["xs"], sp["grad_bc"], sp["grad_bc"], sp["col"], sp["col"], sp["csr"], sp["ddk"]],
        out_shape=[jax.ShapeDtypeStruct((t_len, d_ssm), BF16), jax.ShapeDtypeStruct((t_len, n_bc), BF16),
                   jax.ShapeDtypeStruct((t_len, n_bc), BF16), jax.ShapeDtypeStruct(dt_c.shape, F32),
                   jax.ShapeDtypeStruct(cs_c.shape, F32), jax.ShapeDtypeStruct(cs_r3.shape, F32),
                   jax.ShapeDtypeStruct((nc, 1, d_ssm), F32)],
        args=(xbc, xbc, xbc, dt_c, cs_c, cs_r3, e01, dskip_e, hprev, dy),
        scratch=[pltpu.VMEM((N_GROUPS, N_STATE, rp), F32)], sem=("arbitrary", "arbitrary"), ride=ride)


def _chip_sum(src, sib, *, name):
    rows, cols = src.shape[1:]
    tr = _tile(rows, 256, BF16_ROWS)
    core = lax.axis_index("c").astype(jnp.int32).reshape(1)

    def body(c_ref, a_ref, b_ref, o_ref):
        o_ref[...] = (a_ref[...].astype(F32) + b_ref[...].astype(F32)).astype(o_ref.dtype)

    grid_spec = pltpu.PrefetchScalarGridSpec(
        num_scalar_prefetch=1, grid=(N_CHIPS, rows // tr),
        in_specs=[pl.BlockSpec((None, tr, cols), lambda q, i, c_ref: (2 * q + c_ref[0], i, 0)),
                  pl.BlockSpec((None, tr, cols), lambda q, i, c_ref: (q, i, 0))],
        out_specs=pl.BlockSpec((None, tr, cols), lambda q, i, c_ref: (q, i, 0)))
    return pl.pallas_call(
        body, name=name, grid_spec=grid_spec, out_shape=jax.ShapeDtypeStruct(sib.shape, sib.dtype),
        compiler_params=pltpu.CompilerParams(dimension_semantics=("parallel", "parallel"), vmem_limit_bytes=VMEM_LIMIT),
    )(core, src, sib)


def _adamw(w, g, m, v):
    m = ADAM_B1 * m + (1.0 - ADAM_B1) * g
    v = ADAM_B2 * v + (1.0 - ADAM_B2) * (g * g)
    m_hat = m / (1.0 - ADAM_B1 ** ADAM_STEP)
    v_hat = v / (1.0 - ADAM_B2 ** ADAM_STEP)
    delta = -ADAM_LR * (m_hat / (jnp.sqrt(v_hat) + ADAM_EPS) + ADAM_WD * w)
    return delta, m, v


def _reduce_adamw(parts, w, m, v, *, name):
    n_parts = parts.shape[0]
    rows, cols = w.shape
    tr = _tile(rows, 128, BF16_ROWS)

    def body(p_ref, w_ref, m_ref, v_ref, g_ref, d_ref, mo_ref, vo_ref):
        g = p_ref[0].astype(F32)
        for k in range(1, n_parts):
            g = g + p_ref[k].astype(F32)
        delta, mn, vn = _adamw(w_ref[...], g, m_ref[...], v_ref[...])
        g_ref[...] = g
        d_ref[...] = delta
        mo_ref[...] = mn
        vo_ref[...] = vn

    spec = pl.BlockSpec((tr, cols), lambda i: (i, 0))
    outs, _ = _call(
        body, name=name, grid=(rows // tr,),
        in_specs=[pl.BlockSpec((n_parts, tr, cols), lambda i: (0, i, 0)), spec, spec, spec],
        out_specs=[spec] * 4, out_shape=[jax.ShapeDtypeStruct((rows, cols), F32)] * 4,
        args=(parts, w, m, v), sem=("parallel",))
    return outs


def _move_rows(src, src_row, name, extra=None, extra_row=None):
    rb, n_out, cols = ROW_BLOCK, len(src_row), src.shape[1]
    assert n_out % rb == 0 and src.shape[0] % rb == 0 and src.shape[0] // rb >= 3
    n_blocks, max_b0, seg_cap = n_out // rb, src.shape[0] // rb - 3, 4

    def segments(rows_of, lo):
        segs, r = [], 0
        while r < rb:
            if rows_of[r] < 0:
                r += 1
                continue
            e = r
            while e + 1 < rb and rows_of[e + 1] == rows_of[e] + 1:
                e += 1
            segs.append((r, e + 1, rows_of[r] - r - lo))
            r = e + 1
        assert len(segs) <= seg_cap
        return segs + [(0, 0, 0)] * (seg_cap - len(segs))

    table = []
    for j in range(n_blocks):
        rows_j = list(src_row[j * rb:(j + 1) * rb])
        valid = [v for v in rows_j if v >= 0]
        b0 = min(max((min(valid) // rb) if valid else 0, 0), max_b0)
        assert not valid or max(valid) < (b0 + 3) * rb
        row = [b0] + [v for seg in segments(rows_j, b0 * rb) for v in seg]
        extra_j = [] if extra is None else list(extra_row[j * rb:(j + 1) * rb])
        if extra is not None:
            row += [v for seg in segments(extra_j, 0) for v in seg]
        need_third = bool(valid) and max(valid) >= (b0 + 2) * rb
        third = b0 + 2 if need_third or not table else table[-1][-1]
        row += [int(need_third), int(any(v >= 0 for v in extra_j)), third]
        table.append(row)
    flag_third, flag_extra, col_third = len(table[0]) - 3, len(table[0]) - 2, len(table[0]) - 1
    table = jnp.asarray(table, jnp.int32)

    def select(tbl_ref, j, first, width, col0=0):
        r = lax.broadcasted_iota(jnp.int32, (rb, width), 0)
        c = lax.broadcasted_iota(jnp.int32, (rb, width), 1) + col0
        hit = jnp.zeros((rb, width), jnp.bool_)
        for s in range(seg_cap):
            lo, hi, off = (tbl_ref[j, first + 3 * s + i] for i in range(3))
            hit = hit | ((r >= lo) & (r < hi) & (c == r + off))
        return jnp.where(hit, 1.0, 0.0).astype(BF16)

    def body(tbl_ref, *refs):
        o_ref = refs[-1]
        j = pl.program_id(0)
        sel = select(tbl_ref, j, 1, 2 * rb)
        pick = lambda m, b: lax.dot_general(m, refs[b][...], _NN, preferred_element_type=F32)
        o_ref[...] = (pick(sel[:, :rb], 0) + pick(sel[:, rb:], 1)).astype(o_ref.dtype)

        @pl.when(tbl_ref[j, flag_third] == 1)
        def _():
            o_ref[...] = (o_ref[...].astype(F32) + pick(select(tbl_ref, j, 1, rb, 2 * rb), 2)).astype(o_ref.dtype)

        if extra is not None:
            @pl.when(tbl_ref[j, flag_extra] == 1)
            def _():
                more = lax.dot_general(select(tbl_ref, j, 1 + 3 * seg_cap, extra.shape[0]), refs[3][...], _NN,
                                       preferred_element_type=F32)
                o_ref[...] = (o_ref[...].astype(F32) + more).astype(o_ref.dtype)

    in_specs = [pl.BlockSpec((rb, cols), functools.partial(lambda b, j, tbl: (tbl[j, 0] + b, 0), b)) for b in range(2)]
    in_specs.append(pl.BlockSpec((rb, cols), lambda j, tbl: (tbl[j, col_third], 0)))
    args = [src, src, src]
    if extra is not None:
        in_specs.append(pl.BlockSpec(extra.shape, lambda j, tbl: (0, 0)))
        args.append(extra)
    grid_spec = pltpu.PrefetchScalarGridSpec(num_scalar_prefetch=1, grid=(n_blocks,), in_specs=in_specs,
                                             out_specs=pl.BlockSpec((rb, cols), lambda j, tbl: (j, 0)))
    return pl.pallas_call(
        body, name=name, grid_spec=grid_spec, out_shape=jax.ShapeDtypeStruct((n_out, cols), src.dtype),
        compiler_params=pltpu.CompilerParams(dimension_semantics=("parallel",), vmem_limit_bytes=VMEM_LIMIT),
    )(table, *args)


def _cols_of(g):
    return jnp.transpose(g, (1, 0, 2)).reshape(g.shape[1], -1)


def _pad_to(a, rows, cols):
    return jnp.pad(a, ((0, rows - a.shape[0]), (0, cols - a.shape[1])))


def kernel(x, norm_mix_g, w_in, ssm_conv_w, ssm_conv_b, ssm_dt_bias, ssm_A_log, ssm_D, ssm_norm_g, sc_conv_w, w_out, norm_ffn_g, w_gate, w_up, w_down, norm_final_g, loss_target, m_norm_mix_g, m_w_in, m_ssm_conv_w, m_ssm_conv_b, m_ssm_dt_bias, m_ssm_A_log, m_ssm_D, m_ssm_norm_g, m_sc_conv_w, m_w_out, m_norm_ffn_g, m_w_gate, m_w_up, m_w_down, m_norm_final_g, v_norm_mix_g, v_w_in, v_ssm_conv_w, v_ssm_conv_b, v_ssm_dt_bias, v_ssm_A_log, v_ssm_D, v_ssm_norm_g, v_sc_conv_w, v_w_out, v_norm_ffn_g, v_w_gate, v_w_up, v_w_down, v_norm_final_g):
    t_len, d = x.shape[1], x.shape[2]
    heads = d // HEADDIM
    r_heads = heads // N_GROUPS
    d_xbc = d + 2 * N_GROUPS * N_STATE
    ff_s = w_down.shape[1]
    ff = ff_s * N_DEV
    off_xbc, off_dt = d, d + d_xbc
    off_cb = off_dt + heads
    d_in = off_cb + 3 * d
    in_s = d_in // N_DEV
    in_p = -(-in_s // (2 * BF16_ROWS)) * (2 * BF16_ROWS)
    w_main = 4 * d + d_xbc
    me = 4 * lax.axis_index("x") + 2 * lax.axis_index("y") + lax.axis_index("c")

    x2 = x[0]
    target = loss_target[0]

    tpose = lambda a: jnp.transpose(a[0])
    win_s = _pad_to(tpose(w_in).astype(BF16), in_p, d)
    wg_s, wu_s = tpose(w_gate).astype(BF16), tpose(w_up).astype(BF16)
    wo_s, wd_s = w_out[0].astype(BF16), w_down[0].astype(BF16)
    small_w = jnp.concatenate([_pad_to(ssm_conv_w[0], K_SSM, d_xbc // N_DEV),
                               _pad_to(sc_conv_w[0], K_SC + 1, d_xbc // N_DEV)], axis=0)

    g1, g2, g3 = norm_mix_g, norm_ffn_g, norm_final_g.reshape(1, d)
    gs = ssm_norm_g
    small = [_pad_to(ssm_dt_bias, 1, LANES), ssm_dt_bias.reshape(heads, 1), _pad_to(ssm_A_log, 1, LANES),
             ssm_A_log.reshape(heads, 1)]
    e01 = (lax.broadcasted_iota(jnp.int32, (LANES, d), 1) // HEADDIM
           == lax.broadcasted_iota(jnp.int32, (LANES, d), 0)).astype(BF16)
    dskip_e = jnp.repeat(ssm_D, HEADDIM, axis=1)
    tr = _tile(t_len, 256, 8)
    tr_ff = _tile(t_len, 128, 8)
    cw = LANES
    slab = lambda col: col // cw

    gin_1, gsm_1 = _gather_chips_relayed([win_s], [small_w], "gather_w_in_chips")
    (n1,), (gin, gsm) = _rows_call(lambda v, g: ((_rms(v, g),), ()), rows=t_len, tr=tr, row_ins=[(x2, d, 0)],
                                   full_ins=[g1], row_outs=[(d, BF16)], acc_outs=[], name="norm_mix",
                                   ride=_gather_sibling([gin_1, gsm_1]))
    in_pieces = []
    for k in range(N_DEV):
        for a, b, dst, shift in ((0, off_dt, 0, 0), (off_dt, off_cb, 1, -off_dt), (off_cb, d_in, 0, -heads)):
            s, e = max(k * in_s, a), min((k + 1) * in_s, b)
            if s < e:
                in_pieces.append((k, s - k * in_s, e - s, dst, s + shift))
    ref_row = lambda t: t if t < off_dt else t + heads
    wtm = _move_rows(gin.reshape(N_DEV * in_p, d),
                     [(ref_row(t) // in_s) * in_p + ref_row(t) % in_s for t in range(w_main)], "place_w_in")
    wtdt = jnp.zeros((LANES, d), BF16)
    for k, r0, n, dst, d0 in in_pieces:
        if dst == 1:
            wtdt = lax.dynamic_update_slice(wtdt, gin[k, r0:r0 + n], (d0, 0))
    cw_ssm = _cols_of(gsm[:, :K_SSM, :])
    cw_sc = _cols_of(gsm[:, K_SSM:K_SSM + K_SC, :d // N_DEV])

    proj, (go_1, gg_1) = _matmul(n1, wtm, tb=True, out_dtype=BF16, name="proj_main",
                                 ride=_gather_chips([wo_s, wg_s]))
    dt_raw, _ = _matmul(n1, wtdt, tb=True, out_dtype=F32, name="proj_dt")
    dt_raw_t = jnp.transpose(dt_raw[:, :heads])
    (xbc,), (go, gg) = _cols_call(_conv_silu_fwd, rows=t_len, cols=d_xbc, cw=cw, col_ins=[(proj, slab(off_xbc))],
                                  par_ins=[(cw_ssm, 0), (ssm_conv_b, 0)], col_outs=[BF16], par_outs=[],
                                  name="ssm_conv", ride=_gather_sibling([go_1, gg_1]))
    dt_c, cs_c, cs_r = _ssd_dt(dt_raw, dt_raw_t, small)
    cs_r3 = cs_r.reshape(N_GROUPS, r_heads, t_len)
    up_cut = int(ff_s * W_UP_GATHER_SPLIT) // BF16_ROWS * BF16_ROWS
    down_cut = int(ff_s * W_DOWN_GATHER_SPLIT) // BF16_ROWS * BF16_ROWS
    half_cut = ff_s // 2 // BF16_ROWS * BF16_ROWS
    (y_ssd, hprev), (gu_1,) = _ssd_fwd(xbc, dt_c, cs_c, cs_r3, e01, dskip_e, d_ssm=d, r_heads=r_heads,
                                       ride=_gather_chips([wu_s], rows=(0, up_cut)))

    def gate_norm(y, z, g):
        z = z.astype(F32)
        return _rms(y * (z * _sigmoid(z)), g)

    (y_mix,), _ = _rows_call(lambda y, z, g: ((gate_norm(y, z, g),), ()), rows=t_len, tr=tr,
                             row_ins=[(y_ssd, d, 0), (proj, d, 0)], full_ins=[gs], row_outs=[(d, BF16, 2 * d)],
                             acc_outs=[], name="ssm_gate_norm")
    wgt, wo = gg.reshape(ff, d), go.reshape(2 * d, d)
    sc0 = slab(d + d_xbc)
    (y_mix,), _ = _cols_call(_shortconv_fwd, rows=t_len, cols=d, cw=cw,
                             col_ins=[(proj, sc0), (proj, sc0 + slab(d)), (proj, sc0 + 2 * slab(d))],
                             par_ins=[(cw_sc, 0)], col_outs=[BF16], par_outs=[], name="shortconv",
                             into=(y_mix, slab(d)))
    h1, (gu_1, gd_1) = _matmul(y_mix, wo, out_dtype=F32, add=x2, name="out_proj", ride=_merge(
        _gather_chips([wu_s], rows=(up_cut, ff_s - up_cut), into=[gu_1]), _gather_chips([wd_s], rows=(0, down_cut))))
    (n2,), (gu,) = _rows_call(lambda v, g: ((_rms(v, g),), ()), rows=t_len, tr=tr, row_ins=[(h1, d, 0)],
                              full_ins=[g2], row_outs=[(d, BF16)], acc_outs=[], name="norm_ffn",
                              ride=_gather_sibling([gu_1]))
    wut = gu.reshape(ff, d)
    g_ff, (gd_1,) = _matmul(n2, wgt, tb=True, out_dtype=BF16, name="ffn_gate",
                            ride=_gather_chips([wd_s], rows=(down_cut, ff_s - down_cut), into=[gd_1]))
    (u_ff, a_ff), (gd,) = _matmul(n2, wut, tb=True, name="ffn_up", ride=_gather_sibling([gd_1]),
                                  post=(lambda uv, gv: (uv, gv * _sigmoid(gv) * uv), [g_ff], [BF16, BF16]),
                                  tn_max=MM_TILE_N_POST)
    wd = gd.reshape(ff, d)
    h2, _ = _matmul(a_ff, wd, out_dtype=F32, add=h1, name="ffn_down")

    def head(hv, tv, g):
        def f(hh, gg_):
            e = _rms(hh, gg_) - tv
            return (0.5 / d) * jnp.sum(e * e)
        val, (dh, dg) = jax.value_and_grad(f, argnums=(0, 1))(hv, g)
        return (dh, dh), (jnp.full((1, LANES), val, F32), dg)

    (dh2, dh2_b, loss_acc, dg3), _ = _rows_call(head, rows=t_len, tr=tr, row_ins=[(h2, d, 0), (target, d, 0)],
                                                full_ins=[g3], row_outs=[(d, F32), (d, BF16)],
                                                acc_outs=[(1, LANES), (1, d)], name="loss_head")
    loss = lax.psum(loss_acc[0, 0], ("x", "y", "c"))

    def act_bwd(dav, gv, uv):
        s = _sigmoid(gv)
        return dav * uv * (s * (1.0 + gv * (1.0 - s))), dav * gv * s

    (dg_ff, du_ff), _ = _matmul(dh2_b, wd, tb=True, name="d_ffn_gate_up",
                                post=(act_bwd, [g_ff, u_ff], [BF16, BF16]), tn_max=MM_TILE_N_POST)
    dwd, _ = _matmul(a_ff, dh2_b, ta=True, out_dtype=BF16, name="d_w_down")
    dwd8 = dwd.reshape(N_DEV, ff_s, d)
    dn2, (sib_d,) = _matmul(dg_ff, wgt, out_dtype=F32, name="d_norm_ffn_out_gate", ride=_scatter_sibling([dwd8]))
    chip_d = _chip_sum(dwd8, sib_d, name="chip_sum_w_down")
    dn2, (parts_d,) = _matmul(du_ff, wut, out_dtype=F32, add=dn2, name="d_norm_ffn_out_up",
                              ride=_scatter_chips([chip_d], rows=(0, half_cut)))
    dwg, (parts_d,) = _matmul(dg_ff, n2, ta=True, out_dtype=BF16, name="d_w_gate",
                              ride=_scatter_chips([chip_d], rows=(half_cut, ff_s - half_cut), into=[parts_d]))
    dwu, _ = _matmul(du_ff, n2, ta=True, out_dtype=BF16, name="d_w_up")
    dwg8, dwu8 = dwg.reshape(N_DEV, ff_s, d), dwu.reshape(N_DEV, ff_s, d)

    def norm_bwd(v, dn, dres, g):
        _, vjp = jax.vjp(_rms, v, g)
        dv, dg = vjp(dn)
        return (dv + dres,), (dg,)

    def norm_bwd_2(v, dn, dres, g):
        (dv,), acc = norm_bwd(v, dn, dres, g)
        return (dv, dv), acc

    (dh1, dh1_b, dg2), (sib_g, sib_u) = _rows_call(norm_bwd_2, rows=t_len, tr=tr,
                                                   row_ins=[(h1, d, 0), (dn2, d, 0), (dh2, d, 0)], full_ins=[g2],
                                                   row_outs=[(d, F32), (d, BF16)], acc_outs=[(1, d)], name="d_norm_ffn",
                                                   ride=_scatter_sibling([dwg8, dwu8]))
    chip_g = _chip_sum(dwg8, sib_g, name="chip_sum_w_gate")
    chip_u = _chip_sum(dwu8, sib_u, name="chip_sum_w_up")

    dy_mix, _ = _matmul(dh1_b, wo, tb=True, out_dtype=BF16, name="d_y_mix")
    dwo, _ = _matmul(y_mix, dh1_b, ta=True, out_dtype=BF16, name="d_w_out")
    dwo8 = dwo.reshape(N_DEV, 2 * d // N_DEV, d)
    (dgb, dgc, du, dcw_sc), (sib_o,) = _cols_call(
        _shortconv_bwd, rows=t_len, cols=d, cw=cw,
        col_ins=[(proj, sc0), (proj, sc0 + slab(d)), (proj, sc0 + 2 * slab(d)), (dy_mix, slab(d))],
        par_ins=[(cw_sc, 0)], col_outs=[BF16] * 3, par_outs=[K_SC], name="d_shortconv",
        ride=_scatter_sibling([dwo8]))
    chip_o = _chip_sum(dwo8, sib_o, name="chip_sum_w_out")

    def gate_norm_bwd(y, z, dyo, g):
        _, vjp = jax.vjp(gate_norm, y, z.astype(F32), g)
        dy, dz, dg = vjp(dyo.astype(F32))
        return (dy, dz), (dg,)

    (dy_ssd, dproj, dgs), _ = _rows_call(gate_norm_bwd, rows=t_len, tr=tr,
                                         row_ins=[(y_ssd, d, 0), (proj, d, 0), (dy_mix, d, 0)], full_ins=[gs],
                                         row_outs=[(d, F32), (d, BF16, w_main)], acc_outs=[(1, d)],
                                         name="d_ssm_gate_norm")
    (dxs, dbm, dcm, g_dt, g_csc, g_csr3, ddk), (parts_g, parts_o) = _ssd_bwd(
        xbc, dt_c, cs_c, cs_r3, e01, dskip_e, hprev, dy_ssd, d_ssm=d, r_heads=r_heads,
        ride=_scatter_chips([chip_g, chip_o]))
    ddt_c, ddt_r, dbias_r, dbias_c, dalog_r, dalog_c, ddskip = _ssd_dt(
        dt_raw, dt_raw_t, small, cots=(g_dt, g_csc, g_csr3.reshape(heads, t_len), ddk, e01))
    dcw_parts, dcb_parts, col0 = [], [], 0
    for tag, dpart in (("x", dxs), ("b", dbm), ("c", dcm)):
        (dproj, dcw_p, dcb_p), _ = _cols_call(
            _conv_silu_bwd, rows=t_len, cols=dpart.shape[1], cw=cw,
            col_ins=[(proj, slab(off_xbc + col0)), (dpart, 0)], par_ins=[(cw_ssm, slab(col0)), (ssm_conv_b, slab(col0))],
            col_outs=[BF16], par_outs=[K_SSM, 1], name="d_ssm_conv_" + tag, into=(dproj, slab(off_xbc + col0)))
        dcw_parts.append(dcw_p)
        dcb_parts.append(dcb_p)
        col0 += dpart.shape[1]
    dcw_ssm, dcb_ssm = jnp.concatenate(dcw_parts, axis=1), jnp.concatenate(dcb_parts, axis=1)
    for i, part in enumerate((dgb, dgc, du)):
        dproj = lax.dynamic_update_slice(dproj, part, (0, d + d_xbc + i * d))
    ddt = ddt_c + _pad_to(jnp.transpose(ddt_r), t_len, LANES)
    dwm, (parts_u,) = _matmul(dproj, n1, ta=True, out_dtype=BF16, name="d_w_in_main",
                              ride=_scatter_chips([chip_u]))
    dwdt, _ = _matmul(ddt, n1, ta=True, out_dtype=BF16, name="d_w_in_dt")
    own_ref = [k * in_s + i if i < in_s else -1 for k in range(N_DEV) for i in range(in_p)]
    dwin8 = _move_rows(
        dwm, [-1 if g < 0 or off_dt <= g < off_cb else (g if g < off_dt else g - heads) for g in own_ref],
        "place_d_w_in", extra=dwdt, extra_row=[g - off_dt if off_dt <= g < off_cb else -1 for g in own_ref],
    ).reshape(N_DEV, in_p, d)
    dn1, (sib_in,) = _matmul(ddt, wtdt, out_dtype=F32, name="d_norm_mix_out_dt", ride=_scatter_sibling([dwin8]))
    chip_in = _chip_sum(dwin8, sib_in, name="chip_sum_w_in")
    cut = int(in_p * W_IN_SCATTER_SPLIT) // BF16_ROWS * BF16_ROWS
    dn1, (parts_in,) = _matmul(dproj, wtm, out_dtype=F32, add=dn1, name="d_norm_mix_out",
                               ride=_scatter_chips([chip_in], rows=(0, cut)))
    (dx, dg1), _ = _rows_call(norm_bwd, rows=t_len, tr=tr, row_ins=[(x2, d, 0), (dn1, d, 0), (dh1, d, 0)],
                              full_ins=[g1], row_outs=[(d, F32)], acc_outs=[(1, d)], name="d_norm_mix")

    wide = d_xbc
    rows_small = [dg1, dcb_ssm, dbias_r + _pad_to(dbias_c.reshape(1, heads), 1, LANES),
                  dalog_r + _pad_to(dalog_c.reshape(1, heads), 1, LANES), ddskip, dgs, dg2, dg3]
    packed = jnp.concatenate([_pad_to(r, 1, wide) for r in rows_small]
                             + [dcw_ssm, _pad_to(dcw_sc, K_SC, wide), jnp.zeros((1, wide), F32)], axis=0)
    p_small, parts_in = _comm(_merge(_gather_all([packed]), _scatter_chips([chip_in], rows=(cut, in_p - cut),
                                                                           into=[parts_in])), "gather_small_grads")

    conv_lo = me * (d_xbc // N_DEV)
    sc_lo = me * (d // N_DEV)

    def pack_state(vals):
        (nm, cb, dtb, al, dk, sg, nf, nfin, cws, scs) = vals
        rows = [_pad_to(a.reshape(1, -1), 1, wide) for a in (nm, cb, dtb, al, dk, sg, nf, nfin)]
        cws_full = lax.dynamic_update_slice(jnp.zeros((K_SSM, wide), F32), cws[0], (0, conv_lo))
        scs_full = lax.dynamic_update_slice(jnp.zeros((K_SC, wide), F32), scs[0], (0, sc_lo))
        return jnp.concatenate(rows + [cws_full, scs_full, jnp.zeros((1, wide), F32)], axis=0)

    w_small = pack_state((norm_mix_g, ssm_conv_b, ssm_dt_bias, ssm_A_log, ssm_D, ssm_norm_g, norm_ffn_g, norm_final_g,
                          ssm_conv_w, sc_conv_w))
    m_small = pack_state((m_norm_mix_g, m_ssm_conv_b, m_ssm_dt_bias, m_ssm_A_log, m_ssm_D, m_ssm_norm_g, m_norm_ffn_g,
                          m_norm_final_g, m_ssm_conv_w, m_sc_conv_w))
    v_small = pack_state((v_norm_mix_g, v_ssm_conv_b, v_ssm_dt_bias, v_ssm_A_log, v_ssm_D, v_ssm_norm_g, v_norm_ffn_g,
                          v_norm_final_g, v_ssm_conv_w, v_sc_conv_w))

    tin = lambda a: _pad_to(tpose(a), in_p, d)
    tin_back = lambda a: jnp.transpose(a[:in_s])[None]
    t_back = lambda a: jnp.transpose(a)[None]
    upd = {
        "w_in": [tin_back(o) for o in _reduce_adamw(parts_in, tin(w_in), tin(m_w_in), tin(v_w_in), name="adamw_w_in")],
        "w_out": [o[None] for o in _reduce_adamw(parts_o, w_out[0], m_w_out[0], v_w_out[0], name="adamw_w_out")],
        "w_gate": [t_back(o) for o in _reduce_adamw(parts_g, tpose(w_gate), tpose(m_w_gate), tpose(v_w_gate),
                                                    name="adamw_w_gate")],
        "w_up": [t_back(o) for o in _reduce_adamw(parts_u, tpose(w_up), tpose(m_w_up), tpose(v_w_up),
                                                  name="adamw_w_up")],
        "w_down": [o[None] for o in _reduce_adamw(parts_d, w_down[0], m_w_down[0], v_w_down[0], name="adamw_w_down")],
    }
    small_upd = _reduce_adamw(p_small, w_small, m_small, v_small, name="adamw_small")

    def unpack(packed_out):
        vec = lambda i, n, shape: packed_out[i, :n].reshape(shape)
        return {
            "norm_mix_g": vec(0, d, (1, d)), "ssm_conv_b": vec(1, d_xbc, (1, d_xbc)),
            "ssm_dt_bias": vec(2, heads, (1, heads)), "ssm_A_log": vec(3, heads, (1, heads)),
            "ssm_D": vec(4, heads, (1, heads)), "ssm_norm_g": vec(5, d, (1, d)), "norm_ffn_g": vec(6, d, (1, d)),
            "norm_final_g": vec(7, d, (d,)),
            "ssm_conv_w": lax.dynamic_slice(packed_out[8:8 + K_SSM], (0, conv_lo), (K_SSM, d_xbc // N_DEV))[None],
            "sc_conv_w": lax.dynamic_slice(packed_out[8 + K_SSM:8 + K_SSM + K_SC], (0, sc_lo), (K_SC, d // N_DEV))[None],
        }

    names = ["norm_mix_g", "w_in", "ssm_conv_w", "ssm_conv_b", "ssm_dt_bias", "ssm_A_log", "ssm_D", "ssm_norm_g",
             "sc_conv_w", "w_out", "norm_ffn_g", "w_gate", "w_up", "w_down", "norm_final_g"]
    outs = []
    for kind in range(4):
        small_k = unpack(small_upd[kind])
        for nm in names:
            outs.append(upd[nm][kind] if nm in upd else small_k[nm])
    return (loss, dx[None], *outs)
```

```python
import collections
import functools

import jax
import jax.numpy as jnp
from jax import lax
from jax.experimental import pallas as pl
from jax.experimental.pallas import tpu as pltpu

F32 = jnp.float32
BF16 = jnp.bfloat16

N_DEV = 8
N_CHIPS = 4
HEADDIM = 64
N_GROUPS = 8
N_STATE = 128
CHUNK = 128
K_SSM = 4
K_SC = 3
EPS = 1e-5
LANES = 128
BF16_ROWS = 16
MM_TILE_MN = 1408
MM_TILE_K = 2816
W_IN_SCATTER_SPLIT = 6 / 7
W_UP_GATHER_SPLIT = 0.8
W_DOWN_GATHER_SPLIT = 0.4
MM_TILE_N_POST = 704
SSD_CHUNKS_PER_STEP = 4
SSD_GROUPS_PER_STEP = 2
ROW_BLOCK = 256
V7X_VMEM_BYTES = 64 * 1024 * 1024
VMEM_LIMIT = (V7X_VMEM_BYTES * 3) // 4

ADAM_LR = 0.001
ADAM_B1 = 0.9
ADAM_B2 = 0.999
ADAM_EPS = 1e-08
ADAM_WD = 0.01
ADAM_STEP = 10


def _tile(n, pref, align):
    t = min(pref, n)
    t -= t % align
    while t >= align:
        if n % t == 0:
            return t
        t -= align
    return n


_Ride = collections.namedtuple("_Ride", ["ins", "out_shapes", "aliases", "nsem", "plan"])
_ANY = pl.BlockSpec(memory_space=pl.ANY)


def _coords():
    return lax.axis_index("x"), lax.axis_index("y"), lax.axis_index("c")


def _other_chips(x, y):
    return ((1 - x, y), (x, 1 - y), (1 - x, 1 - y))


def _remote(src, dst, send, recv, k, dev):
    return functools.partial(pltpu.make_async_remote_copy, src_ref=src, dst_ref=dst, send_sem=send.at[k],
                             recv_sem=recv.at[k], device_id=dev, device_id_type=pl.DeviceIdType.MESH)


def _local(src, dst, sem):
    return functools.partial(pltpu.make_async_copy, src, dst, sem)


def _start_all(plan):
    for kind, make in plan:
        if kind != "arrival":
            make().start()


def _wait_all(plan):
    for kind, make in plan:
        if kind == "local":
            make().wait()
        elif kind == "out":
            make().wait_send()
        else:
            make().wait_recv()


def _gather_chips(srcs, rows=None, into=None):
    n = len(srcs)

    def plan(ins, outs, send, recv, base):
        x, y, c = _coords()
        me = 4 * x + 2 * y + c
        cut = (lambda ref: ref) if rows is None else (lambda ref: ref.at[pl.ds(rows[0], rows[1])])
        d = []
        for a, (src, dst) in enumerate(zip(ins[:n], outs)):
            k = base + 4 * a
            d.append(("local", _local(cut(src), cut(dst.at[me]), send.at[k + 3])))
            for j, (px, py) in enumerate(_other_chips(x, y)):
                d.append(("out", _remote(cut(src), cut(dst.at[me]), send, recv, k + j, (px, py, c))))
                d.append(("arrival", _remote(cut(src), cut(dst.at[4 * px + 2 * py + c]), send, recv, k + j,
                                             (px, py, c))))
        return d
    shapes = [jax.ShapeDtypeStruct((N_DEV,) + s.shape, s.dtype) for s in srcs]
    if into is None:
        return _Ride(list(srcs), shapes, {}, 4 * n, plan)
    return _Ride(list(srcs) + list(into), shapes, {n + a: a for a in range(n)}, 4 * n, plan)


def _gather_sibling(bufs):
    def plan(ins, outs, send, recv, base):
        x, y, c = _coords()
        d = []
        for a, buf in enumerate(outs):
            for q in range(N_CHIPS):
                k = base + 4 * a + q
                d.append(("out", _remote(buf.at[2 * q + c], buf.at[2 * q + c], send, recv, k, (x, y, 1 - c))))
                d.append(("arrival", _remote(buf.at[2 * q + c], buf.at[2 * q + 1 - c], send, recv, k, (x, y, 1 - c))))
        return d
    shapes = [jax.ShapeDtypeStruct(b.shape, b.dtype) for b in bufs]
    return _Ride(list(bufs), shapes, {i: i for i in range(len(bufs))}, 4 * len(bufs), plan)


def _scatter_sibling(srcs):
    def plan(ins, outs, send, recv, base):
        x, y, c = _coords()
        d = []
        for a, (src, sib) in enumerate(zip(ins, outs)):
            for q in range(N_CHIPS):
                k = base + 4 * a + q
                d.append(("out", _remote(src.at[2 * q + 1 - c], sib.at[q], send, recv, k, (x, y, 1 - c))))
                d.append(("arrival", _remote(src.at[2 * q + 1 - c], sib.at[q], send, recv, k, (x, y, 1 - c))))
        return d
    shapes = [jax.ShapeDtypeStruct((N_CHIPS,) + s.shape[1:], s.dtype) for s in srcs]
    return _Ride(list(srcs), shapes, {}, 4 * len(srcs), plan)


def _scatter_chips(chips, rows=None, into=None):
    n = len(chips)

    def plan(ins, outs, send, recv, base):
        x, y, c = _coords()
        mine = 2 * x + y
        cut = (lambda ref: ref) if rows is None else (lambda ref: ref.at[pl.ds(rows[0], rows[1])])
        d = []
        for a, (chip, parts) in enumerate(zip(ins[:n], outs)):
            k = base + 4 * a
            d.append(("local", _local(cut(chip.at[mine]), cut(parts.at[mine]), send.at[k + 3])))
            for j, (px, py) in enumerate(_other_chips(x, y)):
                q = 2 * px + py
                d.append(("out", _remote(cut(chip.at[q]), cut(parts.at[mine]), send, recv, k + j, (px, py, c))))
                d.append(("arrival", _remote(cut(chip.at[q]), cut(parts.at[q]), send, recv, k + j, (px, py, c))))
        return d
    shapes = [jax.ShapeDtypeStruct(s.shape, s.dtype) for s in chips]
    if into is None:
        return _Ride(list(chips), shapes, {}, 4 * n, plan)
    return _Ride(list(chips) + list(into), shapes, {n + a: a for a in range(n)}, 4 * n, plan)


def _gather_all(srcs):
    def plan(ins, outs, send, recv, base):
        x, y, c = _coords()
        me = 4 * x + 2 * y + c
        d = []
        for a, (src, dst) in enumerate(zip(ins, outs)):
            k = base + N_DEV * a
            d.append(("local", _local(src, dst.at[me], send.at[k])))
            for j in range(1, N_DEV):
                px = 1 - x if (j >> 2) & 1 else x
                py = 1 - y if (j >> 1) & 1 else y
                pc = 1 - c if j & 1 else c
                d.append(("out", _remote(src, dst.at[me], send, recv, k + j, (px, py, pc))))
                d.append(("arrival", _remote(src, dst.at[4 * px + 2 * py + pc], send, recv, k + j, (px, py, pc))))
        return d
    shapes = [jax.ShapeDtypeStruct((N_DEV,) + s.shape, s.dtype) for s in srcs]
    return _Ride(list(srcs), shapes, {}, N_DEV * len(srcs), plan)


def _merge(*rides):
    ins, outs, aliases, parts, nsem = [], [], {}, [], 0
    for r in rides:
        parts.append((len(ins), len(outs), nsem, r))
        aliases.update({len(ins) + i: len(outs) + j for i, j in r.aliases.items()})
        ins += r.ins
        outs += r.out_shapes
        nsem += r.nsem

    def plan(i, o, send, recv, base):
        d = []
        for i0, o0, s0, r in parts:
            d += r.plan(i[i0:i0 + len(r.ins)], o[o0:o0 + len(r.out_shapes)], send, recv, base + s0)
        return d
    return _Ride(ins, outs, aliases, nsem, plan)


def _comm(ride, name):
    n_in, n_out = len(ride.ins), len(ride.out_shapes)

    def body(*refs):
        plan = ride.plan(refs[:n_in], refs[n_in:n_in + n_out], refs[-2], refs[-1], 0)
        _start_all(plan)
        _wait_all(plan)

    return pl.pallas_call(
        body, name=name, in_specs=[_ANY] * n_in, out_specs=[_ANY] * n_out, out_shape=ride.out_shapes,
        scratch_shapes=[pltpu.SemaphoreType.DMA((ride.nsem,)), pltpu.SemaphoreType.DMA((ride.nsem,))],
        input_output_aliases=dict(ride.aliases),
        compiler_params=pltpu.CompilerParams(has_side_effects=True),
    )(*ride.ins)


def _gather_chips_relayed(big, small, name):
    srcs = list(big) + list(small)
    n, nsem = len(srcs), 5 * len(srcs)

    def body(*refs):
        ins, outs, send, recv = refs[:n], refs[n:2 * n], refs[-2], refs[-1]
        x, y, c = _coords()
        slot = lambda dev: 4 * dev[0] + 2 * dev[1] + dev[2]
        me, nbr_x, nbr_y, diag = (x, y, c), (1 - x, y, c), (x, 1 - y, c), (1 - x, 1 - y, c)
        own, sends = [], []
        for a, (src, dst) in enumerate(zip(ins, outs)):
            k = 5 * a
            own.append(_local(src, dst.at[slot(me)], send.at[k + 4])())
            sends.append(_remote(src, dst.at[slot(me)], send, recv, k, nbr_x)())
            sends.append(_remote(src, dst.at[slot(me)], send, recv, k + 1, nbr_y)())
            if a >= len(big):
                sends.append(_remote(src, dst.at[slot(me)], send, recv, k + 2, diag)())
        for s in own + sends:
            s.start()
        for a, (src, dst) in enumerate(zip(ins, outs)):
            k = 5 * a
            _remote(src, dst.at[slot(nbr_x)], send, recv, k, nbr_x)().wait_recv()
            if a < len(big):
                half = src.shape[0] // 2
                part = dst.at[slot(nbr_x)].at[pl.ds(0, half)]
                fwd = _remote(part, part, send, recv, k + 2, nbr_y)()
                fwd.start()
                sends.append(fwd)
            _remote(src, dst.at[slot(nbr_y)], send, recv, k + 1, nbr_y)().wait_recv()
            if a < len(big):
                part = dst.at[slot(nbr_y)].at[pl.ds(half, src.shape[0] - half)]
                fwd = _remote(part, part, send, recv, k + 3, nbr_x)()
                fwd.start()
                sends.append(fwd)
        for a, (src, dst) in enumerate(zip(ins, outs)):
            k = 5 * a
            if a < len(big):
                half = src.shape[0] // 2
                lo = dst.at[slot(diag)].at[pl.ds(0, half)]
                hi = dst.at[slot(diag)].at[pl.ds(half, src.shape[0] - half)]
                _remote(lo, lo, send, recv, k + 2, nbr_y)().wait_recv()
                _remote(hi, hi, send, recv, k + 3, nbr_x)().wait_recv()
            else:
                _remote(src, dst.at[slot(diag)], send, recv, k + 2, diag)().wait_recv()
        for lc in own:
            lc.wait()
        for s in sends:
            s.wait_send()

    return pl.pallas_call(
        body, name=name, in_specs=[_ANY] * n, out_specs=[_ANY] * n,
        out_shape=[jax.ShapeDtypeStruct((N_DEV,) + s.shape, s.dtype) for s in srcs],
        scratch_shapes=[pltpu.SemaphoreType.DMA((nsem,)), pltpu.SemaphoreType.DMA((nsem,))],
        compiler_params=pltpu.CompilerParams(has_side_effects=True),
    )(*srcs)


def _call(body, *, name, grid, in_specs, out_specs, out_shape, args, sem, scratch=(), ride=None, base=None):
    params = pltpu.CompilerParams(dimension_semantics=sem, vmem_limit_bytes=VMEM_LIMIT)
    own_aliases = {}
    if base is not None:
        inner, n_host = body, len(args)
        body = lambda *refs: inner(*refs[:n_host], *refs[n_host + 1:])
        own_aliases[n_host] = base[1]
        args, in_specs = tuple(args) + (base[0],), list(in_specs) + [_ANY]
    if ride is None:
        res = pl.pallas_call(body, name=name, grid=grid, in_specs=in_specs, out_specs=out_specs,
                             out_shape=out_shape, scratch_shapes=list(scratch), input_output_aliases=own_aliases,
                             compiler_params=params)(*args)
        return list(res), []
    n_in, n_out, n_scr = len(args), len(out_shape), len(scratch)
    r_in, r_out = len(ride.ins), len(ride.out_shapes)

    def hosted(*refs):
        h_in, rin = refs[:n_in], refs[n_in:n_in + r_in]
        o0 = n_in + r_in
        h_out, rout = refs[o0:o0 + n_out], refs[o0 + n_out:o0 + n_out + r_out]
        s0 = o0 + n_out + r_out
        h_scr, send, recv = refs[s0:s0 + n_scr], refs[s0 + n_scr], refs[s0 + n_scr + 1]
        ids = [pl.program_id(i) for i in range(len(grid))]
        first = functools.reduce(lambda p, q: p & q, [i == 0 for i in ids])
        last = functools.reduce(lambda p, q: p & q, [i == n - 1 for i, n in zip(ids, grid)])

        @pl.when(first)
        def _():
            _start_all(ride.plan(rin, rout, send, recv, 0))

        body(*h_in, *h_out, *h_scr)

        @pl.when(last)
        def _():
            _wait_all(ride.plan(rin, rout, send, recv, 0))

    res = pl.pallas_call(
        hosted, name=name, grid=grid, in_specs=list(in_specs) + [_ANY] * r_in,
        out_specs=list(out_specs) + [_ANY] * r_out, out_shape=list(out_shape) + list(ride.out_shapes),
        scratch_shapes=list(scratch) + [pltpu.SemaphoreType.DMA((ride.nsem,)), pltpu.SemaphoreType.DMA((ride.nsem,))],
        input_output_aliases={**own_aliases, **{n_in + i: n_out + j for i, j in ride.aliases.items()}},
        compiler_params=params,
    )(*args, *ride.ins)
    return list(res[:n_out]), list(res[n_out:])


def _matmul(a, b, *, ta=False, tb=False, out_dtype=BF16, add=None, post=None, name, ride=None, tn_max=MM_TILE_MN):
    m = a.shape[1] if ta else a.shape[0]
    k = a.shape[0] if ta else a.shape[1]
    n = b.shape[0] if tb else b.shape[1]
    assert k == (b.shape[1] if tb else b.shape[0])
    tm, tn, tk = _tile(m, MM_TILE_MN, LANES), _tile(n, tn_max, LANES), _tile(k, MM_TILE_K, LANES)
    nk = k // tk
    dims = (((0 if ta else 1,), (1 if tb else 0,)), ((), ()))
    single = post is None
    if add is not None:
        post = (lambda r, t: (r + t,), [add], [out_dtype])
    elif post is None:
        post = (lambda r: (r,), [], [out_dtype])
    post_fn, extras, out_dtypes = post
    n_ex, n_o = len(extras), len(out_dtypes)

    def body(*refs):
        a_ref, b_ref = refs[:2]
        ex_refs, o_refs = refs[2:2 + n_ex], refs[2 + n_ex:2 + n_ex + n_o]

        def finish(r):
            for o_ref, v in zip(o_refs, post_fn(r, *[e[...].astype(F32) for e in ex_refs])):
                o_ref[...] = v.astype(o_ref.dtype)

        if nk == 1 and not ta and not single and tm % (2 * BF16_ROWS) == 0:
            for rs in (pl.ds(0, tm // 2), pl.ds(tm // 2, tm // 2)):
                half = lax.dot_general(a_ref[rs, :].astype(BF16), b_ref[...].astype(BF16), dims,
                                       preferred_element_type=F32)
                for o_ref, v in zip(o_refs, post_fn(half, *[e[rs, :].astype(F32) for e in ex_refs])):
                    o_ref[rs, :] = v.astype(o_ref.dtype)
            return
        part = lax.dot_general(a_ref[...].astype(BF16), b_ref[...].astype(BF16), dims, preferred_element_type=F32)
        if nk == 1:
            finish(part)
            return
        acc = refs[-1]
        kk = pl.program_id(2)

        @pl.when(kk == 0)
        def _():
            acc[...] = part

        @pl.when((kk > 0) & (kk < nk - 1))
        def _():
            acc[...] += part

        @pl.when(kk == nk - 1)
        def _():
            finish(acc[...] + part)

    a_spec = (pl.BlockSpec((tk, tm), lambda i, j, kk: (kk, i)) if ta
              else pl.BlockSpec((tm, tk), lambda i, j, kk: (i, kk)))
    b_spec = (pl.BlockSpec((tn, tk), lambda i, j, kk: (j, kk)) if tb
              else pl.BlockSpec((tk, tn), lambda i, j, kk: (kk, j)))
    o_spec = pl.BlockSpec((tm, tn), lambda i, j, kk: (i, j))
    outs, rides = _call(
        body, name=name, grid=(m // tm, n // tn, nk),
        in_specs=[a_spec, b_spec] + [o_spec] * n_ex, out_specs=[o_spec] * n_o,
        out_shape=[jax.ShapeDtypeStruct((m, n), dt) for dt in out_dtypes], args=(a, b, *extras),
        scratch=[pltpu.VMEM((tm, tn), F32)] if nk > 1 else [], sem=("parallel", "parallel", "arbitrary"), ride=ride)
    return (outs[0] if single else outs), rides


def _rows_call(fn, *, rows, tr, row_ins, full_ins, row_outs, acc_outs, name, ride=None):
    nr, nf, no, na = len(row_ins), len(full_ins), len(row_outs), len(acc_outs)

    def body(*refs):
        vals = [r[...] for r in refs[:nr + nf]]
        outs, accs = fn(*vals)
        for r, v in zip(refs[nr + nf:nr + nf + no], outs):
            r[...] = v.astype(r.dtype)
        if na:
            @pl.when(pl.program_id(0) == 0)
            def _():
                for r in refs[nr + nf + no:]:
                    r[...] = jnp.zeros_like(r)
            for r, v in zip(refs[nr + nf + no:], accs):
                r[...] += v

    in_specs = [pl.BlockSpec((tr, w), functools.partial(lambda cb, i: (i, cb), cb)) for _, w, cb in row_ins]
    in_specs += [pl.BlockSpec(f.shape, lambda i: (0, 0)) for f in full_ins]
    out_specs = [pl.BlockSpec((tr, o[0]), lambda i: (i, 0)) for o in row_outs]
    out_specs += [pl.BlockSpec(s, lambda i: (0, 0)) for s in acc_outs]
    out_shape = [jax.ShapeDtypeStruct((rows, o[-1] if len(o) == 3 else o[0]), o[1]) for o in row_outs]
    out_shape += [jax.ShapeDtypeStruct(s, F32) for s in acc_outs]
    return _call(body, name=name, grid=(rows // tr,), in_specs=in_specs, out_specs=out_specs, out_shape=out_shape,
                 args=tuple(a for a, _, _ in row_ins) + tuple(full_ins), sem=("arbitrary",), ride=ride)


def _cols_call(fn, *, rows, cols, cw, col_ins, par_ins, col_outs, par_outs, name, ride=None, into=None):
    nc, npar = len(col_ins), len(par_ins)

    def body(*refs):
        vals = [r[...] for r in refs[:nc + npar]]
        outs, pouts = fn(*vals)
        for r, v in zip(refs[nc + npar:], tuple(outs) + tuple(pouts)):
            r[...] = v.astype(r.dtype)

    in_specs = [pl.BlockSpec((rows, cw), functools.partial(lambda off, j: (0, off + j), off)) for _, off in col_ins]
    in_specs += [pl.BlockSpec((p.shape[0], cw), functools.partial(lambda off, j: (0, off + j), off))
                 for p, off in par_ins]
    out_specs = [pl.BlockSpec((rows, cw), lambda j: (0, j)) for _ in col_outs]
    out_specs += [pl.BlockSpec((k, cw), lambda j: (0, j)) for k in par_outs]
    out_shape = [jax.ShapeDtypeStruct((rows, cols), dt) for dt in col_outs]
    out_shape += [jax.ShapeDtypeStruct((k, cols), F32) for k in par_outs]
    if into is not None:
        out_specs[0] = pl.BlockSpec((rows, cw), lambda j: (0, into[1] + j))
        out_shape[0] = jax.ShapeDtypeStruct(into[0].shape, into[0].dtype)
    return _call(body, name=name, grid=(cols // cw,), in_specs=in_specs, out_specs=out_specs, out_shape=out_shape,
                 args=tuple(a for a, _ in col_ins) + tuple(p for p, _ in par_ins), sem=("arbitrary",), ride=ride,
                 base=None if into is None else (into[0], 0))


def _sigmoid(v):
    return 1.0 / (1.0 + jnp.exp(-v))


def _softplus(v):
    return jnp.maximum(v, 0.0) + jnp.log(1.0 + jnp.exp(-jnp.abs(v)))


def _rms(v, g):
    return v * lax.rsqrt(jnp.mean(v * v, axis=-1, keepdims=True) + EPS) * g


def _shift_down(v, s, row):
    return jnp.where(row >= s, pltpu.roll(v, s, 0), 0.0)


def _shift_up(v, s, row):
    n = v.shape[0]
    return jnp.where(row < n - s, pltpu.roll(v, n - s, 0), 0.0)


def _causal_conv(u, w, row):
    k_taps = w.shape[0]
    acc = u * w[k_taps - 1:k_taps, :]
    for k in range(k_taps - 1):
        acc = acc + _shift_down(u, k_taps - 1 - k, row) * w[k:k + 1, :]
    return acc


def _causal_conv_bwd(u, dy, w, row):
    k_taps = w.shape[0]
    tap = lax.broadcasted_iota(jnp.int32, w.shape, 0)
    du = dy * w[k_taps - 1:k_taps, :]
    dw = jnp.where(tap == k_taps - 1, jnp.sum(dy * u, axis=0, keepdims=True), 0.0)
    for k in range(k_taps - 1):
        s = k_taps - 1 - k
        du = du + _shift_up(dy, s, row) * w[k:k + 1, :]
        dw = dw + jnp.where(tap == k, jnp.sum(dy * _shift_down(u, s, row), axis=0, keepdims=True), 0.0)
    return du, dw


def _conv_silu_fwd(u, w, b):
    u = u.astype(F32)
    row = lax.broadcasted_iota(jnp.int32, u.shape, 0)
    pre = _causal_conv(u, w, row) + b
    return (pre * _sigmoid(pre),), ()


def _conv_silu_bwd(u, dy, w, b):
    u = u.astype(F32)
    dy = dy.astype(F32)
    row = lax.broadcasted_iota(jnp.int32, u.shape, 0)
    pre = _causal_conv(u, w, row) + b
    s = _sigmoid(pre)
    dpre = dy * (s * (1.0 + pre * (1.0 - s)))
    du, dw = _causal_conv_bwd(u, dpre, w, row)
    return (du,), (dw, jnp.sum(dpre, axis=0, keepdims=True))


def _shortconv_fwd(gb, gc, u, w):
    gb, gc, u = gb.astype(F32), gc.astype(F32), u.astype(F32)
    row = lax.broadcasted_iota(jnp.int32, u.shape, 0)
    return (gb * _causal_conv(gc * u, w, row),), ()


def _shortconv_bwd(gb, gc, u, dy, w):
    gb, gc, u, dy = gb.astype(F32), gc.astype(F32), u.astype(F32), dy.astype(F32)
    row = lax.broadcasted_iota(jnp.int32, u.shape, 0)
    v = gc * u
    dgb = dy * _causal_conv(v, w, row)
    dv, dw = _causal_conv_bwd(v, dy * gb, w, row)
    return (dgb, dv * u, dv * gc), (dw,)


def _split3(v):
    hi = v.astype(BF16)
    r1 = v - hi.astype(F32)
    mid = r1.astype(BF16)
    lo = (r1 - mid.astype(F32)).astype(BF16)
    return hi, mid, lo


def _exact_dot(v, m01, dims, v_is_lhs):
    def one(p):
        return (lax.dot_general(p, m01, dims, preferred_element_type=F32) if v_is_lhs
                else lax.dot_general(m01, p, dims, preferred_element_type=F32))
    hi, mid, lo = _split3(v)
    return (one(lo) + one(mid)) + one(hi)


_NN = (((1,), (0,)), ((), ()))
_NT = (((1,), (1,)), ((), ()))
_TN = (((0,), (0,)), ((), ()))


@jax.custom_vjp
def _cumsum_rows(tril, v):
    return _exact_dot(v, tril, _NN, False)


def _cumsum_rows_fwd(tril, v):
    return _cumsum_rows(tril, v), tril


def _cumsum_rows_bwd(tril, ct):
    return None, _exact_dot(ct, tril, _TN, False)


_cumsum_rows.defvjp(_cumsum_rows_fwd, _cumsum_rows_bwd)


@jax.custom_vjp
def _cumsum_lanes(tril, v):
    return _exact_dot(v, tril, _NT, True)


def _cumsum_lanes_fwd(tril, v):
    return _cumsum_lanes(tril, v), tril


def _cumsum_lanes_bwd(tril, ct):
    return None, _exact_dot(ct, tril, _NN, True)


_cumsum_lanes.defvjp(_cumsum_lanes_fwd, _cumsum_lanes_bwd)


@jax.custom_vjp
def _expand(e01, v):
    return _exact_dot(v, e01, _NN, True)


def _expand_fwd(e01, v):
    return _expand(e01, v), e01


def _expand_bwd(e01, ct):
    return None, _exact_dot(ct, e01, _NT, True)


_expand.defvjp(_expand_fwd, _expand_bwd)


def _causal_mask(n):
    li = lax.broadcasted_iota(jnp.int32, (n, n), 0)
    si = lax.broadcasted_iota(jnp.int32, (n, n), 1)
    return si <= li


def _dt_prep(dtc, dtr, bias_r, bias_c, alog_r, alog_c):
    dt_c = _softplus(dtc + bias_r)
    dt_r = _softplus(dtr + bias_c)
    tril = jnp.where(_causal_mask(dtc.shape[0]), 1.0, 0.0).astype(BF16)
    cs_c = _cumsum_rows(tril, dt_c * (-jnp.exp(alog_r)))
    cs_r = _cumsum_lanes(tril, dt_r * (-jnp.exp(alog_c)))
    return dt_c, cs_c, cs_r


def _ssd_chunk(r_heads, xs, bg, cg, dt_c, cs_c, cs_rg, e01, dskip_e, hp):
    l_len, rp = xs.shape
    p = rp // r_heads
    causal = _causal_mask(l_len)
    lane_head = lax.broadcasted_iota(jnp.int32, (1, rp), 1) // p
    dt_e = _expand(e01, dt_c)
    cs_e = _expand(e01, cs_c)
    cl_e = cs_e[l_len - 1:l_len, :]
    x = xs * dt_e
    bgb, cgb = bg.astype(BF16), cg.astype(BF16)
    cb = lax.dot_general(cgb, bgb, _NT, preferred_element_type=F32)
    ms, xm = [], []
    for r in range(r_heads):
        seg = cs_e[:, r * p:r * p + 1] - cs_rg[r:r + 1, :]
        decay = jnp.exp(jnp.where(causal, seg, -1e30))
        ms.append((cb * decay).astype(BF16))
        xm.append(jnp.where(lane_head == r, x, 0.0).astype(BF16))
    y_diag = lax.dot_general(jnp.concatenate(ms, axis=1), jnp.concatenate(xm, axis=0), _NN,
                             preferred_element_type=F32)
    y_off = lax.dot_general(cgb, hp.astype(BF16), _NN, preferred_element_type=F32) * jnp.exp(cs_e)
    xd = (x * jnp.exp(cl_e - cs_e)).astype(BF16)
    states = lax.dot_general(bgb, xd, _TN, preferred_element_type=F32)
    h_next = hp * jnp.exp(cl_e) + states
    y = y_diag + y_off + dskip_e * xs
    return y, h_next


def _ssd_dt(dtc, dtr, small, cots=None):
    t_len, heads = dtc.shape[0], dtr.shape[0]
    nc = t_len // CHUNK
    col = pl.BlockSpec((CHUNK, LANES), lambda c: (c, 0))
    row = pl.BlockSpec((heads, CHUNK), lambda c: (0, c))
    full = [pl.BlockSpec(s.shape, lambda c: (0, 0)) for s in small]
    shapes = [jax.ShapeDtypeStruct((t_len, LANES), F32), jax.ShapeDtypeStruct((t_len, LANES), F32),
              jax.ShapeDtypeStruct((heads, t_len), F32)]
    if cots is None:
        def body(dtc_ref, dtr_ref, br, bc, ar, ac, dt_ref, csc_ref, csr_ref):
            dt_ref[...], csc_ref[...], csr_ref[...] = _dt_prep(dtc_ref[...], dtr_ref[...], br[...], bc[...],
                                                                ar[...], ac[...])
        return _call(body, name="ssd_dt", grid=(nc,), in_specs=[col, row] + full, out_specs=[col, col, row],
                     out_shape=shapes, args=(dtc, dtr, *small), sem=("parallel",))[0]

    g_dt, g_csc, g_csr, ddk, e01 = cots

    def body(dtc_ref, dtr_ref, br, bc, ar, ac, g_dt_ref, g_csc_ref, g_csr_ref, ddk_ref, e_ref,
             ddtc_ref, ddtr_ref, *dsmall):
        _, vjp = jax.vjp(_dt_prep, dtc_ref[...], dtr_ref[...], br[...], bc[...], ar[...], ac[...])
        grads = vjp((g_dt_ref[...], g_csc_ref[...], g_csr_ref[...]))
        ddtc_ref[...], ddtr_ref[...] = grads[0], grads[1]
        ddk8 = jnp.broadcast_to(ddk_ref[...], (8, ddk_ref.shape[1]))
        dskip = _exact_dot(ddk8, e_ref[...], _NT, True)[0:1, :]

        @pl.when(pl.program_id(0) == 0)
        def _():
            for r in dsmall:
                r[...] = jnp.zeros_like(r)

        for r, gr in zip(dsmall, tuple(grads[2:]) + (dskip,)):
            r[...] += gr

    acc = list(small) + [small[0]]
    return _call(body, name="d_ssd_dt", grid=(nc,),
                 in_specs=[col, row] + full + [col, col, row, pl.BlockSpec((None, 1, e01.shape[1]), lambda c: (c, 0, 0)),
                                               pl.BlockSpec(e01.shape, lambda c: (0, 0))],
                 out_specs=[col, row] + [pl.BlockSpec(s.shape, lambda c: (0, 0)) for s in acc],
                 out_shape=[shapes[0], shapes[2]] + [jax.ShapeDtypeStruct(s.shape, F32) for s in acc],
                 args=(dtc, dtr, *small, g_dt, g_csc, g_csr, ddk, e01), sem=("arbitrary",))[0]


def _ssd_specs(t_len, d_ssm, r_heads, reverse):
    rp = r_heads * HEADDIM
    nc = t_len // CHUNK
    per = next(p for p in (SSD_CHUNKS_PER_STEP, 2, 1) if nc % p == 0)
    ns, rows, gs = nc // per, per * CHUNK, SSD_GROUPS_PER_STEP
    cidx = (lambda c: ns - 1 - c) if reverse else (lambda c: c)
    b_off = d_ssm // (N_STATE * gs)
    specs = dict(
        xs=pl.BlockSpec((rows, gs * rp), lambda c, g: (cidx(c), g)),
        b=pl.BlockSpec((rows, gs * N_STATE), lambda c, g: (cidx(c), b_off + g)),
        c=pl.BlockSpec((rows, gs * N_STATE), lambda c, g: (cidx(c), b_off + N_GROUPS // gs + g)),
        grad_bc=pl.BlockSpec((rows, gs * N_STATE), lambda c, g: (cidx(c), g)),
        col=pl.BlockSpec((rows, LANES), lambda c, g: (cidx(c), 0)),
        csr=pl.BlockSpec((gs, r_heads, rows), lambda c, g: (g, 0, cidx(c))),
        e01=pl.BlockSpec((LANES, gs * rp), lambda c, g: (0, g)),
        dskip=pl.BlockSpec((1, gs * rp), lambda c, g: (0, g)),
        hprev=pl.BlockSpec((per, gs, N_STATE, rp), lambda c, g: (cidx(c), g, 0, 0)),
        ddk=pl.BlockSpec((per, 1, gs * rp), lambda c, g: (cidx(c), 0, g)),
    )
    return specs, nc, ns, per, rp


def _ssd_fwd(xbc, dt_c, cs_c, cs_r3, e01, dskip_e, *, d_ssm, r_heads, ride=None):
    t_len = xbc.shape[0]
    sp, nc, ns, per, rp = _ssd_specs(t_len, d_ssm, r_heads, False)

    def body(xs_ref, b_ref, c_ref, dt_ref, csc_ref, csr_ref, e_ref, dk_ref, y_ref, hprev_ref, h_ref):
        c, gp = pl.program_id(0), pl.program_id(1)
        groups = [gp * SSD_GROUPS_PER_STEP + gi for gi in range(SSD_GROUPS_PER_STEP)]

        @pl.when(c == 0)
        def _():
            for g in groups:
                h_ref[g] = jnp.zeros((N_STATE, rp), F32)

        hp = [h_ref[g] for g in groups]
        for s in range(per):
            r = pl.ds(s * CHUNK, CHUNK)
            for gi in range(SSD_GROUPS_PER_STEP):
                cols, bc = pl.ds(gi * rp, rp), pl.ds(gi * N_STATE, N_STATE)
                hprev_ref[s, gi] = hp[gi]
                y, hp[gi] = _ssd_chunk(r_heads, xs_ref[r, cols].astype(F32), b_ref[r, bc].astype(F32),
                                       c_ref[r, bc].astype(F32), dt_ref[r, :], csc_ref[r, :], csr_ref[gi, :, r],
                                       e_ref[:, cols], dk_ref[:, cols], hp[gi])
                y_ref[r, cols] = y
        for gi, g in enumerate(groups):
            h_ref[g] = hp[gi]

    return _call(
        body, name="ssd_fwd", grid=(ns, N_GROUPS // SSD_GROUPS_PER_STEP),
        in_specs=[sp["xs"], sp["b"], sp["c"], sp["col"], sp["col"], sp["csr"], sp["e01"], sp["dskip"]],
        out_specs=[sp["xs"], sp["hprev"]],
        out_shape=[jax.ShapeDtypeStruct((t_len, d_ssm), F32),
                   jax.ShapeDtypeStruct((nc, N_GROUPS, N_STATE, rp), F32)],
        args=(xbc, xbc, xbc, dt_c, cs_c, cs_r3, e01, dskip_e), scratch=[pltpu.VMEM((N_GROUPS, N_STATE, rp), F32)],
        sem=("arbitrary", "arbitrary"), ride=ride)


def _ssd_bwd(xbc, dt_c, cs_c, cs_r3, e01, dskip_e, hprev, dy, *, d_ssm, r_heads, ride=None):
    t_len = xbc.shape[0]
    sp, nc, ns, per, rp = _ssd_specs(t_len, d_ssm, r_heads, True)

    def body(xs_ref, b_ref, c_ref, dt_ref, csc_ref, csr_ref, e_ref, dk_ref, hprev_ref, dy_ref,
             dxs_ref, db_ref, dc_ref, ddt_ref, dcsc_ref, dcsr_ref, ddk_ref, dh_ref):
        c, gp = pl.program_id(0), pl.program_id(1)
        groups = [gp * SSD_GROUPS_PER_STEP + gi for gi in range(SSD_GROUPS_PER_STEP)]

        @pl.when(gp == 0)
        def _():
            ddt_ref[...] = jnp.zeros_like(ddt_ref)
            dcsc_ref[...] = jnp.zeros_like(dcsc_ref)

        @pl.when(c == 0)
        def _():
            for g in groups:
                dh_ref[g] = jnp.zeros((N_STATE, rp), F32)

        dh = [dh_ref[g] for g in groups]
        for s in reversed(range(per)):
            r = pl.ds(s * CHUNK, CHUNK)
            ddt_sum, dcsc_sum = ddt_ref[r, :], dcsc_ref[r, :]
            for gi in range(SSD_GROUPS_PER_STEP):
                cols, bc = pl.ds(gi * rp, rp), pl.ds(gi * N_STATE, N_STATE)
                e01 = e_ref[:, cols]
                fn = lambda xs, bg, cg, dt, csc, csr, dk, hp: _ssd_chunk(r_heads, xs, bg, cg, dt, csc, csr, e01, dk, hp)
                _, vjp = jax.vjp(fn, xs_ref[r, cols].astype(F32), b_ref[r, bc].astype(F32), c_ref[r, bc].astype(F32),
                                 dt_ref[r, :], csc_ref[r, :], csr_ref[gi, :, r], dk_ref[:, cols], hprev_ref[s, gi])
                dxs, dbg, dcg, ddt, dcsc, dcsr, ddk, dh[gi] = vjp((dy_ref[r, cols], dh[gi]))
                dxs_ref[r, cols] = dxs.astype(dxs_ref.dtype)
                db_ref[r, bc] = dbg.astype(db_ref.dtype)
                dc_ref[r, bc] = dcg.astype(dc_ref.dtype)
                ddt_sum, dcsc_sum = ddt_sum + ddt, dcsc_sum + dcsc
                dcsr_ref[gi, :, r] = dcsr
                ddk_ref[s, :, cols] = ddk
            ddt_ref[r, :], dcsc_ref[r, :] = ddt_sum, dcsc_sum
        for gi, g in enumerate(groups):
            dh_ref[g] = dh[gi]

    n_bc = N_GROUPS * N_STATE
    return _call(
        body, name="ssd_bwd", grid=(ns, N_GROUPS // SSD_GROUPS_PER_STEP),
        in_specs=[sp["xs"], sp["b"], sp["c"], sp["col"], sp["col"], sp["csr"], sp["e01"], sp["dskip"], sp["hprev"],
                  sp["xs"]],
        out_specs=[sp["xs"], sp["grad_bc"], sp["grad_bc"], sp["col"], sp["col"], sp["csr"], sp["ddk"]],
        out_shape=[jax.ShapeDtypeStruct((t_len, d_ssm), BF16), jax.ShapeDtypeStruct((t_len, n_bc), BF16),
                   jax.ShapeDtypeStruct((t_len, n_bc), BF16), jax.ShapeDtypeStruct(dt_c.shape, F32),
                   jax.ShapeDtypeStruct(cs_c.shape, F32), jax.ShapeDtypeStruct(cs_r3.shape, F32),
                   jax.ShapeDtypeStruct((nc, 1, d_ssm), F32)],
        args=(xbc, xbc, xbc, dt_c, cs_c, cs_r3, e01, dskip_e, hprev, dy),
        scratch=[pltpu.VMEM((N_GROUPS, N_STATE, rp), F32)], sem=("arbitrary", "arbitrary"), ride=ride)


def _chip_sum(src, sib, *, name):
    rows, cols = src.shape[1:]
    tr = _tile(rows, 256, BF16_ROWS)
    core = lax.axis_index("c").astype(jnp.int32).reshape(1)

    def body(c_ref, a_ref, b_ref, o_ref):
        o_ref[...] = (a_ref[...].astype(F32) + b_ref[...].astype(F32)).astype(o_ref.dtype)

    grid_spec = pltpu.PrefetchScalarGridSpec(
        num_scalar_prefetch=1, grid=(N_CHIPS, rows // tr),
        in_specs=[pl.BlockSpec((None, tr, cols), lambda q, i, c_ref: (2 * q + c_ref[0], i, 0)),
                  pl.BlockSpec((None, tr, cols), lambda q, i, c_ref: (q, i, 0))],
        out_specs=pl.BlockSpec((None, tr, cols), lambda q, i, c_ref: (q, i, 0)))
    return pl.pallas_call(
        body, name=name, grid_spec=grid_spec, out_shape=jax.ShapeDtypeStruct(sib.shape, sib.dtype),
        compiler_params=pltpu.CompilerParams(dimension_semantics=("parallel", "parallel"), vmem_limit_bytes=VMEM_LIMIT),
    )(core, src, sib)


def _adamw(w, g, m, v):
    m = ADAM_B1 * m + (1.0 - ADAM_B1) * g
    v = ADAM_B2 * v + (1.0 - ADAM_B2) * (g * g)
    m_hat = m / (1.0 - ADAM_B1 ** ADAM_STEP)
    v_hat = v / (1.0 - ADAM_B2 ** ADAM_STEP)
    delta = -ADAM_LR * (m_hat / (jnp.sqrt(v_hat) + ADAM_EPS) + ADAM_WD * w)
    return delta, m, v


def _reduce_adamw(parts, w, m, v, *, name):
    n_parts = parts.shape[0]
    rows, cols = w.shape
    tr = _tile(rows, 128, BF16_ROWS)

    def body(p_ref, w_ref, m_ref, v_ref, g_ref, d_ref, mo_ref, vo_ref):
        g = p_ref[0].astype(F32)
        for k in range(1, n_parts):
            g = g + p_ref[k].astype(F32)
        delta, mn, vn = _adamw(w_ref[...], g, m_ref[...], v_ref[...])
        g_ref[...] = g
        d_ref[...] = delta
        mo_ref[...] = mn
        vo_ref[...] = vn

    spec = pl.BlockSpec((tr, cols), lambda i: (i, 0))
    outs, _ = _call(
        body, name=name, grid=(rows // tr,),
        in_specs=[pl.BlockSpec((n_parts, tr, cols), lambda i: (0, i, 0)), spec, spec, spec],
        out_specs=[spec] * 4, out_shape=[jax.ShapeDtypeStruct((rows, cols), F32)] * 4,
        args=(parts, w, m, v), sem=("parallel",))
    return outs


def _move_rows(src, src_row, name, extra=None, extra_row=None):
    rb, n_out, cols = ROW_BLOCK, len(src_row), src.shape[1]
    assert n_out % rb == 0 and src.shape[0] % rb == 0 and src.shape[0] // rb >= 3
    n_blocks, max_b0, seg_cap = n_out // rb, src.shape[0] // rb - 3, 4

    def segments(rows_of, lo):
        segs, r = [], 0
        while r < rb:
            if rows_of[r] < 0:
                r += 1
                continue
            e = r
            while e + 1 < rb and rows_of[e + 1] == rows_of[e] + 1:
                e += 1
            segs.append((r, e + 1, rows_of[r] - r - lo))
            r = e + 1
        assert len(segs) <= seg_cap
        return segs + [(0, 0, 0)] * (seg_cap - len(segs))

    table = []
    for j in range(n_blocks):
        rows_j = list(src_row[j * rb:(j + 1) * rb])
        valid = [v for v in rows_j if v >= 0]
        b0 = min(max((min(valid) // rb) if valid else 0, 0), max_b0)
        assert not valid or max(valid) < (b0 + 3) * rb
        row = [b0] + [v for seg in segments(rows_j, b0 * rb) for v in seg]
        extra_j = [] if extra is None else list(extra_row[j * rb:(j + 1) * rb])
        if extra is not None:
            row += [v for seg in segments(extra_j, 0) for v in seg]
        need_third = bool(valid) and max(valid) >= (b0 + 2) * rb
        third = b0 + 2 if need_third or not table else table[-1][-1]
        row += [int(need_third), int(any(v >= 0 for v in extra_j)), third]
        table.append(row)
    flag_third, flag_extra, col_third = len(table[0]) - 3, len(table[0]) - 2, len(table[0]) - 1
    table = jnp.asarray(table, jnp.int32)

    def select(tbl_ref, j, first, width, col0=0):
        r = lax.broadcasted_iota(jnp.int32, (rb, width), 0)
        c = lax.broadcasted_iota(jnp.int32, (rb, width), 1) + col0
        hit = jnp.zeros((rb, width), jnp.bool_)
        for s in range(seg_cap):
            lo, hi, off = (tbl_ref[j, first + 3 * s + i] for i in range(3))
            hit = hit | ((r >= lo) & (r < hi) & (c == r + off))
        return jnp.where(hit, 1.0, 0.0).astype(BF16)

    def body(tbl_ref, *refs):
        o_ref = refs[-1]
        j = pl.program_id(0)
        sel = select(tbl_ref, j, 1, 2 * rb)
        pick = lambda m, b: lax.dot_general(m, refs[b][...], _NN, preferred_element_type=F32)
        o_ref[...] = (pick(sel[:, :rb], 0) + pick(sel[:, rb:], 1)).astype(o_ref.dtype)

        @pl.when(tbl_ref[j, flag_third] == 1)
        def _():
            o_ref[...] = (o_ref[...].astype(F32) + pick(select(tbl_ref, j, 1, rb, 2 * rb), 2)).astype(o_ref.dtype)

        if extra is not None:
            @pl.when(tbl_ref[j, flag_extra] == 1)
            def _():
                more = lax.dot_general(select(tbl_ref, j, 1 + 3 * seg_cap, extra.shape[0]), refs[3][...], _NN,
                                       preferred_element_type=F32)
                o_ref[...] = (o_ref[...].astype(F32) + more).astype(o_ref.dtype)

    in_specs = [pl.BlockSpec((rb, cols), functools.partial(lambda b, j, tbl: (tbl[j, 0] + b, 0), b)) for b in range(2)]
    in_specs.append(pl.BlockSpec((rb, cols), lambda j, tbl: (tbl[j, col_third], 0)))
    args = [src, src, src]
    if extra is not None:
        in_specs.append(pl.BlockSpec(extra.shape, lambda j, tbl: (0, 0)))
        args.append(extra)
    grid_spec = pltpu.PrefetchScalarGridSpec(num_scalar_prefetch=1, grid=(n_blocks,), in_specs=in_specs,
                                             out_specs=pl.BlockSpec((rb, cols), lambda j, tbl: (j, 0)))
    return pl.pallas_call(
        body, name=name, grid_spec=grid_spec, out_shape=jax.ShapeDtypeStruct((n_out, cols), src.dtype),
        compiler_params=pltpu.CompilerParams(dimension_semantics=("parallel",), vmem_limit_bytes=VMEM_LIMIT),
    )(table, *args)


def _cols_of(g):
    return jnp.transpose(g, (1, 0, 2)).reshape(g.shape[1], -1)


def _pad_to(a, rows, cols):
    return jnp.pad(a, ((0, rows - a.shape[0]), (0, cols - a.shape[1])))


def kernel(x, norm_mix_g, w_in, ssm_conv_w, ssm_conv_b, ssm_dt_bias, ssm_A_log, ssm_D, ssm_norm_g, sc_conv_w, w_out, norm_ffn_g, w_gate, w_up, w_down, norm_final_g, loss_target, m_norm_mix_g, m_w_in, m_ssm_conv_w, m_ssm_conv_b, m_ssm_dt_bias, m_ssm_A_log, m_ssm_D, m_ssm_norm_g, m_sc_conv_w, m_w_out, m_norm_ffn_g, m_w_gate, m_w_up, m_w_down, m_norm_final_g, v_norm_mix_g, v_w_in, v_ssm_conv_w, v_ssm_conv_b, v_ssm_dt_bias, v_ssm_A_log, v_ssm_D, v_ssm_norm_g, v_sc_conv_w, v_w_out, v_norm_ffn_g, v_w_gate, v_w_up, v_w_down, v_norm_final_g):
    t_len, d = x.shape[1], x.shape[2]
    heads = d // HEADDIM
    r_heads = heads // N_GROUPS
    d_xbc = d + 2 * N_GROUPS * N_STATE
    ff_s = w_down.shape[1]
    ff = ff_s * N_DEV
    off_xbc, off_dt = d, d + d_xbc
    off_cb = off_dt + heads
    d_in = off_cb + 3 * d
    in_s = d_in // N_DEV
    in_p = -(-in_s // (2 * BF16_ROWS)) * (2 * BF16_ROWS)
    w_main = 4 * d + d_xbc
    me = 4 * lax.axis_index("x") + 2 * lax.axis_index("y") + lax.axis_index("c")

    x2 = x[0]
    target = loss_target[0]

    tpose = lambda a: jnp.transpose(a[0])
    win_s = _pad_to(tpose(w_in).astype(BF16), in_p, d)
    wg_s, wu_s = tpose(w_gate).astype(BF16), tpose(w_up).astype(BF16)
    wo_s, wd_s = w_out[0].astype(BF16), w_down[0].astype(BF16)
    small_w = jnp.concatenate([_pad_to(ssm_conv_w[0], K_SSM, d_xbc // N_DEV),
                               _pad_to(sc_conv_w[0], K_SC + 1, d_xbc // N_DEV)], axis=0)

    g1, g2, g3 = norm_mix_g, norm_ffn_g, norm_final_g.reshape(1, d)
    gs = ssm_norm_g
    small = [_pad_to(ssm_dt_bias, 1, LANES), ssm_dt_bias.reshape(heads, 1), _pad_to(ssm_A_log, 1, LANES),
             ssm_A_log.reshape(heads, 1)]
    e01 = (lax.broadcasted_iota(jnp.int32, (LANES, d), 1) // HEADDIM
           == lax.broadcasted_iota(jnp.int32, (LANES, d), 0)).astype(BF16)
    dskip_e = jnp.repeat(ssm_D, HEADDIM, axis=1)
    tr = _tile(t_len, 256, 8)
    tr_ff = _tile(t_len, 128, 8)
    cw = LANES
    slab = lambda col: col // cw

    gin_1, gsm_1 = _gather_chips_relayed([win_s], [small_w], "gather_w_in_chips")
    (n1,), (gin, gsm) = _rows_call(lambda v, g: ((_rms(v, g),), ()), rows=t_len, tr=tr, row_ins=[(x2, d, 0)],
                                   full_ins=[g1], row_outs=[(d, BF16)], acc_outs=[], name="norm_mix",
                                   ride=_gather_sibling([gin_1, gsm_1]))
    in_pieces = []
    for k in range(N_DEV):
        for a, b, dst, shift in ((0, off_dt, 0, 0), (off_dt, off_cb, 1, -off_dt), (off_cb, d_in, 0, -heads)):
            s, e = max(k * in_s, a), min((k + 1) * in_s, b)
            if s < e:
                in_pieces.append((k, s - k * in_s, e - s, dst, s + shift))
    ref_row = lambda t: t if t < off_dt else t + heads
    wtm = _move_rows(gin.reshape(N_DEV * in_p, d),
                     [(ref_row(t) // in_s) * in_p + ref_row(t) % in_s for t in range(w_main)], "place_w_in")
    wtdt = jnp.zeros((LANES, d), BF16)
    for k, r0, n, dst, d0 in in_pieces:
        if dst == 1:
            wtdt = lax.dynamic_update_slice(wtdt, gin[k, r0:r0 + n], (d0, 0))
    cw_ssm = _cols_of(gsm[:, :K_SSM, :])
    cw_sc = _cols_of(gsm[:, K_SSM:K_SSM + K_SC, :d // N_DEV])

    proj, (go_1, gg_1) = _matmul(n1, wtm, tb=True, out_dtype=BF16, name="proj_main",
                                 ride=_gather_chips([wo_s, wg_s]))
    dt_raw, _ = _matmul(n1, wtdt, tb=True, out_dtype=F32, name="proj_dt")
    dt_raw_t = jnp.transpose(dt_raw[:, :heads])
    (xbc,), (go, gg) = _cols_call(_conv_silu_fwd, rows=t_len, cols=d_xbc, cw=cw, col_ins=[(proj, slab(off_xbc))],
                                  par_ins=[(cw_ssm, 0), (ssm_conv_b, 0)], col_outs=[BF16], par_outs=[],
                                  name="ssm_conv", ride=_gather_sibling([go_1, gg_1]))
    dt_c, cs_c, cs_r = _ssd_dt(dt_raw, dt_raw_t, small)
    cs_r3 = cs_r.reshape(N_GROUPS, r_heads, t_len)
    up_cut = int(ff_s * W_UP_GATHER_SPLIT) // BF16_ROWS * BF16_ROWS
    down_cut = int(ff_s * W_DOWN_GATHER_SPLIT) // BF16_ROWS * BF16_ROWS
    half_cut = ff_s // 2 // BF16_ROWS * BF16_ROWS
    (y_ssd, hprev), (gu_1,) = _ssd_fwd(xbc, dt_c, cs_c, cs_r3, e01, dskip_e, d_ssm=d, r_heads=r_heads,
                                       ride=_gather_chips([wu_s], rows=(0, up_cut)))

    def gate_norm(y, z, g):
        z = z.astype(F32)
        return _rms(y * (z * _sigmoid(z)), g)

    (y_mix,), _ = _rows_call(lambda y, z, g: ((gate_norm(y, z, g),), ()), rows=t_len, tr=tr,
                             row_ins=[(y_ssd, d, 0), (proj, d, 0)], full_ins=[gs], row_outs=[(d, BF16, 2 * d)],
                             acc_outs=[], name="ssm_gate_norm")
    wgt, wo = gg.reshape(ff, d), go.reshape(2 * d, d)
    sc0 = slab(d + d_xbc)
    (y_mix,), _ = _cols_call(_shortconv_fwd, rows=t_len, cols=d, cw=cw,
                             col_ins=[(proj, sc0), (proj, sc0 + slab(d)), (proj, sc0 + 2 * slab(d))],
                             par_ins=[(cw_sc, 0)], col_outs=[BF16], par_outs=[], name="shortconv",
                             into=(y_mix, slab(d)))
    h1, (gu_1, gd_1) = _matmul(y_mix, wo, out_dtype=F32, add=x2, name="out_proj", ride=_merge(
        _gather_chips([wu_s], rows=(up_cut, ff_s - up_cut), into=[gu_1]), _gather_chips([wd_s], rows=(0, down_cut))))
    (n2,), (gu,) = _rows_call(lambda v, g: ((_rms(v, g),), ()), rows=t_len, tr=tr, row_ins=[(h1, d, 0)],
                              full_ins=[g2], row_outs=[(d, BF16)], acc_outs=[], name="norm_ffn",
                              ride=_gather_sibling([gu_1]))
    wut = gu.reshape(ff, d)
    g_ff, (gd_1,) = _matmul(n2, wgt, tb=True, out_dtype=BF16, name="ffn_gate",
                            ride=_gather_chips([wd_s], rows=(down_cut, ff_s - down_cut), into=[gd_1]))
    (u_ff, a_ff), (gd,) = _matmul(n2, wut, tb=True, name="ffn_up", ride=_gather_sibling([gd_1]),
                                  post=(lambda uv, gv: (uv, gv * _sigmoid(gv) * uv), [g_ff], [BF16, BF16]),
                                  tn_max=MM_TILE_N_POST)
    wd = gd.reshape(ff, d)
    h2, _ = _matmul(a_ff, wd, out_dtype=F32, add=h1, name="ffn_down")

    def head(hv, tv, g):
        def f(hh, gg_):
            e = _rms(hh, gg_) - tv
            return (0.5 / d) * jnp.sum(e * e)
        val, (dh, dg) = jax.value_and_grad(f, argnums=(0, 1))(hv, g)
        return (dh, dh), (jnp.full((1, LANES), val, F32), dg)

    (dh2, dh2_b, loss_acc, dg3), _ = _rows_call(head, rows=t_len, tr=tr, row_ins=[(h2, d, 0), (target, d, 0)],
                                                full_ins=[g3], row_outs=[(d, F32), (d, BF16)],
                                                acc_outs=[(1, LANES), (1, d)], name="loss_head")
    loss = lax.psum(loss_acc[0, 0], ("x", "y", "c"))

    def act_bwd(dav, gv, uv):
        s = _sigmoid(gv)
        return dav * uv * (s * (1.0 + gv * (1.0 - s))), dav * gv * s

    (dg_ff, du_ff), _ = _matmul(dh2_b, wd, tb=True, name="d_ffn_gate_up",
                                post=(act_bwd, [g_ff, u_ff], [BF16, BF16]), tn_max=MM_TILE_N_POST)
    dwd, _ = _matmul(a_ff, dh2_b, ta=True, out_dtype=BF16, name="d_w_down")
    dwd8 = dwd.reshape(N_DEV, ff_s, d)
    dn2, (sib_d,) = _matmul(dg_ff, wgt, out_dtype=F32, name="d_norm_ffn_out_gate", ride=_scatter_sibling([dwd8]))
    chip_d = _chip_sum(dwd8, sib_d, name="chip_sum_w_down")
    dn2, (parts_d,) = _matmul(du_ff, wut, out_dtype=F32, add=dn2, name="d_norm_ffn_out_up",
                              ride=_scatter_chips([chip_d], rows=(0, half_cut)))
    dwg, (parts_d,) = _matmul(dg_ff, n2, ta=True, out_dtype=BF16, name="d_w_gate",
                              ride=_scatter_chips([chip_d], rows=(half_cut, ff_s - half_cut), into=[parts_d]))
    dwu, _ = _matmul(du_ff, n2, ta=True, out_dtype=BF16, name="d_w_up")
    dwg8, dwu8 = dwg.reshape(N_DEV, ff_s, d), dwu.reshape(N_DEV, ff_s, d)

    def norm_bwd(v, dn, dres, g):
        _, vjp = jax.vjp(_rms, v, g)
        dv, dg = vjp(dn)
        return (dv + dres,), (dg,)

    def norm_bwd_2(v, dn, dres, g):
        (dv,), acc = norm_bwd(v, dn, dres, g)
        return (dv, dv), acc

    (dh1, dh1_b, dg2), (sib_g, sib_u) = _rows_call(norm_bwd_2, rows=t_len, tr=tr,
                                                   row_ins=[(h1, d, 0), (dn2, d, 0), (dh2, d, 0)], full_ins=[g2],
                                                   row_outs=[(d, F32), (d, BF16)], acc_outs=[(1, d)], name="d_norm_ffn",
                                                   ride=_scatter_sibling([dwg8, dwu8]))
    chip_g = _chip_sum(dwg8, sib_g, name="chip_sum_w_gate")
    chip_u = _chip_sum(dwu8, sib_u, name="chip_sum_w_up")

    dy_mix, _ = _matmul(dh1_b, wo, tb=True, out_dtype=BF16, name="d_y_mix")
    dwo, _ = _matmul(y_mix, dh1_b, ta=True, out_dtype=BF16, name="d_w_out")
    dwo8 = dwo.reshape(N_DEV, 2 * d // N_DEV, d)
    (dgb, dgc, du, dcw_sc), (sib_o,) = _cols_call(
        _shortconv_bwd, rows=t_len, cols=d, cw=cw,
        col_ins=[(proj, sc0), (proj, sc0 + slab(d)), (proj, sc0 + 2 * slab(d)), (dy_mix, slab(d))],
        par_ins=[(cw_sc, 0)], col_outs=[BF16] * 3, par_outs=[K_SC], name="d_shortconv",
        ride=_scatter_sibling([dwo8]))
    chip_o = _chip_sum(dwo8, sib_o, name="chip_sum_w_out")

    def gate_norm_bwd(y, z, dyo, g):
        _, vjp = jax.vjp(gate_norm, y, z.astype(F32), g)
        dy, dz, dg = vjp(dyo.astype(F32))
        return (dy, dz), (dg,)

    (dy_ssd, dproj, dgs), _ = _rows_call(gate_norm_bwd, rows=t_len, tr=tr,
                                         row_ins=[(y_ssd, d, 0), (proj, d, 0), (dy_mix, d, 0)], full_ins=[gs],
                                         row_outs=[(d, F32), (d, BF16, w_main)], acc_outs=[(1, d)],
                                         name="d_ssm_gate_norm")
    (dxs, dbm, dcm, g_dt, g_csc, g_csr3, ddk), (parts_g, parts_o) = _ssd_bwd(
        xbc, dt_c, cs_c, cs_r3, e01, dskip_e, hprev, dy_ssd, d_ssm=d, r_heads=r_heads,
        ride=_scatter_chips([chip_g, chip_o]))
    ddt_c, ddt_r, dbias_r, dbias_c, dalog_r, dalog_c, ddskip = _ssd_dt(
        dt_raw, dt_raw_t, small, cots=(g_dt, g_csc, g_csr3.reshape(heads, t_len), ddk, e01))
    dcw_parts, dcb_parts, col0 = [], [], 0
    for tag, dpart in (("x", dxs), ("b", dbm), ("c", dcm)):
        (dproj, dcw_p, dcb_p), _ = _cols_call(
            _conv_silu_bwd, rows=t_len, cols=dpart.shape[1], cw=cw,
            col_ins=[(proj, slab(off_xbc + col0)), (dpart, 0)], par_ins=[(cw_ssm, slab(col0)), (ssm_conv_b, slab(col0))],
            col_outs=[BF16], par_outs=[K_SSM, 1], name="d_ssm_conv_" + tag, into=(dproj, slab(off_xbc + col0)))
        dcw_parts.append(dcw_p)
        dcb_parts.append(dcb_p)
        col0 += dpart.shape[1]
    dcw_ssm, dcb_ssm = jnp.concatenate(dcw_parts, axis=1), jnp.concatenate(dcb_parts, axis=1)
    for i, part in enumerate((dgb, dgc, du)):
        dproj = lax.dynamic_update_slice(dproj, part, (0, d + d_xbc + i * d))
    ddt = ddt_c + _pad_to(jnp.transpose(ddt_r), t_len, LANES)
    dwm, (parts_u,) = _matmul(dproj, n1, ta=True, out_dtype=BF16, name="d_w_in_main",
                              ride=_scatter_chips([chip_u]))
    dwdt, _ = _matmul(ddt, n1, ta=True, out_dtype=BF16, name="d_w_in_dt")
    own_ref = [k * in_s + i if i < in_s else -1 for k in range(N_DEV) for i in range(in_p)]
    dwin8 = _move_rows(
        dwm, [-1 if g < 0 or off_dt <= g < off_cb else (g if g < off_dt else g - heads) for g in own_ref],
        "place_d_w_in", extra=dwdt, extra_row=[g - off_dt if off_dt <= g < off_cb else -1 for g in own_ref],
    ).reshape(N_DEV, in_p, d)
    dn1, (sib_in,) = _matmul(ddt, wtdt, out_dtype=F32, name="d_norm_mix_out_dt", ride=_scatter_sibling([dwin8]))
    chip_in = _chip_sum(dwin8, sib_in, name="chip_sum_w_in")
    cut = int(in_p * W_IN_SCATTER_SPLIT) // BF16_ROWS * BF16_ROWS
    dn1, (parts_in,) = _matmul(dproj, wtm, out_dtype=F32, add=dn1, name="d_norm_mix_out",
                               ride=_scatter_chips([chip_in], rows=(0, cut)))
    (dx, dg1), _ = _rows_call(norm_bwd, rows=t_len, tr=tr, row_ins=[(x2, d, 0), (dn1, d, 0), (dh1, d, 0)],
                              full_ins=[g1], row_outs=[(d, F32)], acc_outs=[(1, d)], name="d_norm_mix")

    wide = d_xbc
    rows_small = [dg1, dcb_ssm, dbias_r + _pad_to(dbias_c.reshape(1, heads), 1, LANES),
                  dalog_r + _pad_to(dalog_c.reshape(1, heads), 1, LANES), ddskip, dgs, dg2, dg3]
    packed = jnp.concatenate([_pad_to(r, 1, wide) for r in rows_small]
                             + [dcw_ssm, _pad_to(dcw_sc, K_SC, wide), jnp.zeros((1, wide), F32)], axis=0)
    p_small, parts_in = _comm(_merge(_gather_all([packed]), _scatter_chips([chip_in], rows=(cut, in_p - cut),
                                                                           into=[parts_in])), "gather_small_grads")

    conv_lo = me * (d_xbc // N_DEV)
    sc_lo = me * (d // N_DEV)

    def pack_state(vals):
        (nm, cb, dtb, al, dk, sg, nf, nfin, cws, scs) = vals
        rows = [_pad_to(a.reshape(1, -1), 1, wide) for a in (nm, cb, dtb, al, dk, sg, nf, nfin)]
        cws_full = lax.dynamic_update_slice(jnp.zeros((K_SSM, wide), F32), cws[0], (0, conv_lo))
        scs_full = lax.dynamic_update_slice(jnp.zeros((K_SC, wide), F32), scs[0], (0, sc_lo))
        return jnp.concatenate(rows + [cws_full, scs_full, jnp.zeros((1, wide), F32)], axis=0)

    w_small = pack_state((norm_mix_g, ssm_conv_b, ssm_dt_bias, ssm_A_log, ssm_D, ssm_norm_g, norm_ffn_g, norm_final_g,
                          ssm_conv_w, sc_conv_w))
    m_small = pack_state((m_norm_mix_g, m_ssm_conv_b, m_ssm_dt_bias, m_ssm_A_log, m_ssm_D, m_ssm_norm_g, m_norm_ffn_g,
                          m_norm_final_g, m_ssm_conv_w, m_sc_conv_w))
    v_small = pack_state((v_norm_mix_g, v_ssm_conv_b, v_ssm_dt_bias, v_ssm_A_log, v_ssm_D, v_ssm_norm_g, v_norm_ffn_g,
                          v_norm_final_g, v_ssm_conv_w, v_sc_conv_w))

    tin = lambda a: _pad_to(tpose(a), in_p, d)
    tin_back = lambda a: jnp.transpose(a[:in_s])[None]
    t_back = lambda a: jnp.transpose(a)[None]
    upd = {
        "w_in": [tin_back(o) for o in _reduce_adamw(parts_in, tin(w_in), tin(m_w_in), tin(v_w_in), name="adamw_w_in")],
        "w_out": [o[None] for o in _reduce_adamw(parts_o, w_out[0], m_w_out[0], v_w_out[0], name="adamw_w_out")],
        "w_gate": [t_back(o) for o in _reduce_adamw(parts_g, tpose(w_gate), tpose(m_w_gate), tpose(v_w_gate),
                                                    name="adamw_w_gate")],
        "w_up": [t_back(o) for o in _reduce_adamw(parts_u, tpose(w_up), tpose(m_w_up), tpose(v_w_up),
                                                  name="adamw_w_up")],
        "w_down": [o[None] for o in _reduce_adamw(parts_d, w_down[0], m_w_down[0], v_w_down[0], name="adamw_w_down")],
    }
    small_upd = _reduce_adamw(p_small, w_small, m_small, v_small, name="adamw_small")

    def unpack(packed_out):
        vec = lambda i, n, shape: packed_out[i, :n].reshape(shape)
        return {
            "norm_mix_g": vec(0, d, (1, d)), "ssm_conv_b": vec(1, d_xbc, (1, d_xbc)),
            "ssm_dt_bias": vec(2, heads, (1, heads)), "ssm_A_log": vec(3, heads, (1, heads)),
            "ssm_D": vec(4, heads, (1, heads)), "ssm_norm_g": vec(5, d, (1, d)), "norm_ffn_g": vec(6, d, (1, d)),
            "norm_final_g": vec(7, d, (d,)),
            "ssm_conv_w": lax.dynamic_slice(packed_out[8:8 + K_SSM], (0, conv_lo), (K_SSM, d_xbc // N_DEV))[None],
            "sc_conv_w": lax.dynamic_slice(packed_out[8 + K_SSM:8 + K_SSM + K_SC], (0, sc_lo), (K_SC, d // N_DEV))[None],
        }

    names = ["norm_mix_g", "w_in", "ssm_conv_w", "ssm_conv_b", "ssm_dt_bias", "ssm_A_log", "ssm_D", "ssm_norm_g",
             "sc_conv_w", "w_out", "norm_ffn_g", "w_gate", "w_up", "w_down", "norm_final_g"]
    outs = []
    for kind in range(4):
        small_k = unpack(small_upd[kind])
        for nm in names:
            outs.append(upd[nm][kind] if nm in upd else small_k[nm])
    return (loss, dx[None], *outs)
```

```python
import collections
import functools

import jax
import jax.numpy as jnp
from jax import lax
from jax.experimental import pallas as pl
from jax.experimental.pallas import tpu as pltpu

F32 = jnp.float32
BF16 = jnp.bfloat16

N_DEV = 8
N_CHIPS = 4
HEADDIM = 64
N_GROUPS = 8
N_STATE = 128
CHUNK = 128
K_SSM = 4
K_SC = 3
EPS = 1e-5
LANES = 128
BF16_ROWS = 16
MM_TILE_MN = 1408
MM_TILE_K = 2816
W_IN_SCATTER_SPLIT = 13 / 14
W_UP_GATHER_SPLIT = 0.8
W_DOWN_GATHER_SPLIT = 0.4
MM_TILE_N_POST = 704
SSD_CHUNKS_PER_STEP = 4
SSD_GROUPS_PER_STEP = 4
ROW_BLOCK = 256
V7X_VMEM_BYTES = 64 * 1024 * 1024
VMEM_LIMIT = (V7X_VMEM_BYTES * 3) // 4

ADAM_LR = 0.001
ADAM_B1 = 0.9
ADAM_B2 = 0.999
ADAM_EPS = 1e-08
ADAM_WD = 0.01
ADAM_STEP = 10


def _tile(n, pref, align):
    t = min(pref, n)
    t -= t % align
    while t >= align:
        if n % t == 0:
            return t
        t -= align
    return n


_Ride = collections.namedtuple("_Ride", ["ins", "out_shapes", "aliases", "nsem", "plan"])
_ANY = pl.BlockSpec(memory_space=pl.ANY)


def _coords():
    return lax.axis_index("x"), lax.axis_index("y"), lax.axis_index("c")


def _other_chips(x, y):
    return ((1 - x, y), (x, 1 - y), (1 - x, 1 - y))


def _remote(src, dst, send, recv, k, dev):
    return functools.partial(pltpu.make_async_remote_copy, src_ref=src, dst_ref=dst, send_sem=send.at[k],
                             recv_sem=recv.at[k], device_id=dev, device_id_type=pl.DeviceIdType.MESH)


def _local(src, dst, sem):
    return functools.partial(pltpu.make_async_copy, src, dst, sem)


def _start_all(plan):
    for kind, make in plan:
        if kind != "arrival":
            make().start()


def _wait_all(plan):
    for kind, make in plan:
        if kind == "local":
            make().wait()
        elif kind == "out":
            make().wait_send()
        else:
            make().wait_recv()


def _gather_chips(srcs, rows=None, into=None):
    n = len(srcs)

    def plan(ins, outs, send, recv, base):
        x, y, c = _coords()
        me = 4 * x + 2 * y + c
        cut = (lambda ref: ref) if rows is None else (lambda ref: ref.at[pl.ds(rows[0], rows[1])])
        d = []
        for a, (src, dst) in enumerate(zip(ins[:n], outs)):
            k = base + 4 * a
            d.append(("local", _local(cut(src), cut(dst.at[me]), send.at[k + 3])))
            for j, (px, py) in enumerate(_other_chips(x, y)):
                d.append(("out", _remote(cut(src), cut(dst.at[me]), send, recv, k + j, (px, py, c))))
                d.append(("arrival", _remote(cut(src), cut(dst.at[4 * px + 2 * py + c]), send, recv, k + j,
                                             (px, py, c))))
        return d
    shapes = [jax.ShapeDtypeStruct((N_DEV,) + s.shape, s.dtype) for s in srcs]
    if into is None:
        return _Ride(list(srcs), shapes, {}, 4 * n, plan)
    return _Ride(list(srcs) + list(into), shapes, {n + a: a for a in range(n)}, 4 * n, plan)


def _gather_sibling(bufs):
    def plan(ins, outs, send, recv, base):
        x, y, c = _coords()
        d = []
        for a, buf in enumerate(outs):
            for q in range(N_CHIPS):
                k = base + 4 * a + q
                d.append(("out", _remote(buf.at[2 * q + c], buf.at[2 * q + c], send, recv, k, (x, y, 1 - c))))
                d.append(("arrival", _remote(buf.at[2 * q + c], buf.at[2 * q + 1 - c], send, recv, k, (x, y, 1 - c))))
        return d
    shapes = [jax.ShapeDtypeStruct(b.shape, b.dtype) for b in bufs]
    return _Ride(list(bufs), shapes, {i: i for i in range(len(bufs))}, 4 * len(bufs), plan)


def _scatter_sibling(srcs):
    def plan(ins, outs, send, recv, base):
        x, y, c = _coords()
        d = []
        for a, (src, sib) in enumerate(zip(ins, outs)):
            for q in range(N_CHIPS):
                k = base + 4 * a + q
                d.append(("out", _remote(src.at[2 * q + 1 - c], sib.at[q], send, recv, k, (x, y, 1 - c))))
                d.append(("arrival", _remote(src.at[2 * q + 1 - c], sib.at[q], send, recv, k, (x, y, 1 - c))))
        return d
    shapes = [jax.ShapeDtypeStruct((N_CHIPS,) + s.shape[1:], s.dtype) for s in srcs]
    return _Ride(list(srcs), shapes, {}, 4 * len(srcs), plan)


def _scatter_chips(chips, rows=None, into=None):
    n = len(chips)

    def plan(ins, outs, send, recv, base):
        x, y, c = _coords()
        mine = 2 * x + y
        cut = (lambda ref: ref) if rows is None else (lambda ref: ref.at[pl.ds(rows[0], rows[1])])
        d = []
        for a, (chip, parts) in enumerate(zip(ins[:n], outs)):
            k = base + 4 * a
            d.append(("local", _local(cut(chip.at[mine]), cut(parts.at[mine]), send.at[k + 3])))
            for j, (px, py) in enumerate(_other_chips(x, y)):
                q = 2 * px + py
                d.append(("out", _remote(cut(chip.at[q]), cut(parts.at[mine]), send, recv, k + j, (px, py, c))))
                d.append(("arrival", _remote(cut(chip.at[q]), cut(parts.at[q]), send, recv, k + j, (px, py, c))))
        return d
    shapes = [jax.ShapeDtypeStruct(s.shape, s.dtype) for s in chips]
    if into is None:
        return _Ride(list(chips), shapes, {}, 4 * n, plan)
    return _Ride(list(chips) + list(into), shapes, {n + a: a for a in range(n)}, 4 * n, plan)


def _gather_all(srcs):
    def plan(ins, outs, send, recv, base):
        x, y, c = _coords()
        me = 4 * x + 2 * y + c
        d = []
        for a, (src, dst) in enumerate(zip(ins, outs)):
            k = base + N_DEV * a
            d.append(("local", _local(src, dst.at[me], send.at[k])))
            for j in range(1, N_DEV):
                px = 1 - x if (j >> 2) & 1 else x
                py = 1 - y if (j >> 1) & 1 else y
                pc = 1 - c if j & 1 else c
                d.append(("out", _remote(src, dst.at[me], send, recv, k + j, (px, py, pc))))
                d.append(("arrival", _remote(src, dst.at[4 * px + 2 * py + pc], send, recv, k + j, (px, py, pc))))
        return d
    shapes = [jax.ShapeDtypeStruct((N_DEV,) + s.shape, s.dtype) for s in srcs]
    return _Ride(list(srcs), shapes, {}, N_DEV * len(srcs), plan)


def _merge(*rides):
    ins, outs, aliases, parts, nsem = [], [], {}, [], 0
    for r in rides:
        parts.append((len(ins), len(outs), nsem, r))
        aliases.update({len(ins) + i: len(outs) + j for i, j in r.aliases.items()})
        ins += r.ins
        outs += r.out_shapes
        nsem += r.nsem

    def plan(i, o, send, recv, base):
        d = []
        for i0, o0, s0, r in parts:
            d += r.plan(i[i0:i0 + len(r.ins)], o[o0:o0 + len(r.out_shapes)], send, recv, base + s0)
        return d
    return _Ride(ins, outs, aliases, nsem, plan)


def _comm(ride, name):
    n_in, n_out = len(ride.ins), len(ride.out_shapes)

    def body(*refs):
        plan = ride.plan(refs[:n_in], refs[n_in:n_in + n_out], refs[-2], refs[-1], 0)
        _start_all(plan)
        _wait_all(plan)

    return pl.pallas_call(
        body, name=name, in_specs=[_ANY] * n_in, out_specs=[_ANY] * n_out, out_shape=ride.out_shapes,
        scratch_shapes=[pltpu.SemaphoreType.DMA((ride.nsem,)), pltpu.SemaphoreType.DMA((ride.nsem,))],
        input_output_aliases=dict(ride.aliases),
        compiler_params=pltpu.CompilerParams(has_side_effects=True),
    )(*ride.ins)


def _gather_chips_relayed(big, small, name):
    srcs = list(big) + list(small)
    n, nsem = len(srcs), 5 * len(srcs)

    def body(*refs):
        ins, outs, send, recv = refs[:n], refs[n:2 * n], refs[-2], refs[-1]
        x, y, c = _coords()
        slot = lambda dev: 4 * dev[0] + 2 * dev[1] + dev[2]
        me, nbr_x, nbr_y, diag = (x, y, c), (1 - x, y, c), (x, 1 - y, c), (1 - x, 1 - y, c)
        own, sends = [], []
        for a, (src, dst) in enumerate(zip(ins, outs)):
            k = 5 * a
            own.append(_local(src, dst.at[slot(me)], send.at[k + 4])())
            sends.append(_remote(src, dst.at[slot(me)], send, recv, k, nbr_x)())
            sends.append(_remote(src, dst.at[slot(me)], send, recv, k + 1, nbr_y)())
            if a >= len(big):
                sends.append(_remote(src, dst.at[slot(me)], send, recv, k + 2, diag)())
        for s in own + sends:
            s.start()
        for a, (src, dst) in enumerate(zip(ins, outs)):
            k = 5 * a
            _remote(src, dst.at[slot(nbr_x)], send, recv, k, nbr_x)().wait_recv()
            if a < len(big):
                half = src.shape[0] // 2
                part = dst.at[slot(nbr_x)].at[pl.ds(0, half)]
                fwd = _remote(part, part, send, recv, k + 2, nbr_y)()
                fwd.start()
                sends.append(fwd)
            _remote(src, dst.at[slot(nbr_y)], send, recv, k + 1, nbr_y)().wait_recv()
            if a < len(big):
                part = dst.at[slot(nbr_y)].at[pl.ds(half, src.shape[0] - half)]
                fwd = _remote(part, part, send, recv, k + 3, nbr_x)()
                fwd.start()
                sends.append(fwd)
        for a, (src, dst) in enumerate(zip(ins, outs)):
            k = 5 * a
            if a < len(big):
                half = src.shape[0] // 2
                lo = dst.at[slot(diag)].at[pl.ds(0, half)]
                hi = dst.at[slot(diag)].at[pl.ds(half, src.shape[0] - half)]
                _remote(lo, lo, send, recv, k + 2, nbr_y)().wait_recv()
                _remote(hi, hi, send, recv, k + 3, nbr_x)().wait_recv()
            else:
                _remote(src, dst.at[slot(diag)], send, recv, k + 2, diag)().wait_recv()
        for lc in own:
            lc.wait()
        for s in sends:
            s.wait_send()

    return pl.pallas_call(
        body, name=name, in_specs=[_ANY] * n, out_specs=[_ANY] * n,
        out_shape=[jax.ShapeDtypeStruct((N_DEV,) + s.shape, s.dtype) for s in srcs],
        scratch_shapes=[pltpu.SemaphoreType.DMA((nsem,)), pltpu.SemaphoreType.DMA((nsem,))],
        compiler_params=pltpu.CompilerParams(has_side_effects=True),
    )(*srcs)


def _call(body, *, name, grid, in_specs, out_specs, out_shape, args, sem, scratch=(), ride=None, base=None):
    params = pltpu.CompilerParams(dimension_semantics=sem, vmem_limit_bytes=VMEM_LIMIT)
    own_aliases = {}
    if base is not None:
        inner, n_host = body, len(args)
        body = lambda *refs: inner(*refs[:n_host], *refs[n_host + 1:])
        own_aliases[n_host] = base[1]
        args, in_specs = tuple(args) + (base[0],), list(in_specs) + [_ANY]
    if ride is None:
        res = pl.pallas_call(body, name=name, grid=grid, in_specs=in_specs, out_specs=out_specs,
                             out_shape=out_shape, scratch_shapes=list(scratch), input_output_aliases=own_aliases,
                             compiler_params=params)(*args)
        return list(res), []
    n_in, n_out, n_scr = len(args), len(out_shape), len(scratch)
    r_in, r_out = len(ride.ins), len(ride.out_shapes)

    def hosted(*refs):
        h_in, rin = refs[:n_in], refs[n_in:n_in + r_in]
        o0 = n_in + r_in
        h_out, rout = refs[o0:o0 + n_out], refs[o0 + n_out:o0 + n_out + r_out]
        s0 = o0 + n_out + r_out
        h_scr, send, recv = refs[s0:s0 + n_scr], refs[s0 + n_scr], refs[s0 + n_scr + 1]
        ids = [pl.program_id(i) for i in range(len(grid))]
        first = functools.reduce(lambda p, q: p & q, [i == 0 for i in ids])
        last = functools.reduce(lambda p, q: p & q, [i == n - 1 for i, n in zip(ids, grid)])

        @pl.when(first)
        def _():
            _start_all(ride.plan(rin, rout, send, recv, 0))

        body(*h_in, *h_out, *h_scr)

        @pl.when(last)
        def _():
            _wait_all(ride.plan(rin, rout, send, recv, 0))

    res = pl.pallas_call(
        hosted, name=name, grid=grid, in_specs=list(in_specs) + [_ANY] * r_in,
        out_specs=list(out_specs) + [_ANY] * r_out, out_shape=list(out_shape) + list(ride.out_shapes),
        scratch_shapes=list(scratch) + [pltpu.SemaphoreType.DMA((ride.nsem,)), pltpu.SemaphoreType.DMA((ride.nsem,))],
        input_output_aliases={**own_aliases, **{n_in + i: n_out + j for i, j in ride.aliases.items()}},
        compiler_params=params,
    )(*args, *ride.ins)
    return list(res[:n_out]), list(res[n_out:])


def _matmul(a, b, *, ta=False, tb=False, out_dtype=BF16, add=None, post=None, name, ride=None, tn_max=MM_TILE_MN):
    m = a.shape[1] if ta else a.shape[0]
    k = a.shape[0] if ta else a.shape[1]
    n = b.shape[0] if tb else b.shape[1]
    assert k == (b.shape[1] if tb else b.shape[0])
    tm, tn, tk = _tile(m, MM_TILE_MN, LANES), _tile(n, tn_max, LANES), _tile(k, MM_TILE_K, LANES)
    nk = k // tk
    dims = (((0 if ta else 1,), (1 if tb else 0,)), ((), ()))
    single = post is None
    if add is not None:
        post = (lambda r, t: (r + t,), [add], [out_dtype])
    elif post is None:
        post = (lambda r: (r,), [], [out_dtype])
    post_fn, extras, out_dtypes = post
    n_ex, n_o = len(extras), len(out_dtypes)

    def body(*refs):
        a_ref, b_ref = refs[:2]
        ex_refs, o_refs = refs[2:2 + n_ex], refs[2 + n_ex:2 + n_ex + n_o]

        def finish(r):
            for o_ref, v in zip(o_refs, post_fn(r, *[e[...].astype(F32) for e in ex_refs])):
                o_ref[...] = v.astype(o_ref.dtype)

        part = lax.dot_general(a_ref[...].astype(BF16), b_ref[...].astype(BF16), dims, preferred_element_type=F32)
        if nk == 1:
            finish(part)
            return
        acc = refs[-1]
        kk = pl.program_id(2)

        @pl.when(kk == 0)
        def _():
            acc[...] = part

        @pl.when((kk > 0) & (kk < nk - 1))
        def _():
            acc[...] += part

        @pl.when(kk == nk - 1)
        def _():
            finish(acc[...] + part)

    a_spec = (pl.BlockSpec((tk, tm), lambda i, j, kk: (kk, i)) if ta
              else pl.BlockSpec((tm, tk), lambda i, j, kk: (i, kk)))
    b_spec = (pl.BlockSpec((tn, tk), lambda i, j, kk: (j, kk)) if tb
              else pl.BlockSpec((tk, tn), lambda i, j, kk: (kk, j)))
    o_spec = pl.BlockSpec((tm, tn), lambda i, j, kk: (i, j))
    outs, rides = _call(
        body, name=name, grid=(m // tm, n // tn, nk),
        in_specs=[a_spec, b_spec] + [o_spec] * n_ex, out_specs=[o_spec] * n_o,
        out_shape=[jax.ShapeDtypeStruct((m, n), dt) for dt in out_dtypes], args=(a, b, *extras),
        scratch=[pltpu.VMEM((tm, tn), F32)] if nk > 1 else [], sem=("parallel", "parallel", "arbitrary"), ride=ride)
    return (outs[0] if single else outs), rides


def _rows_call(fn, *, rows, tr, row_ins, full_ins, row_outs, acc_outs, name, ride=None):
    nr, nf, no, na = len(row_ins), len(full_ins), len(row_outs), len(acc_outs)

    def body(*refs):
        vals = [r[...] for r in refs[:nr + nf]]
        outs, accs = fn(*vals)
        for r, v in zip(refs[nr + nf:nr + nf + no], outs):
            r[...] = v.astype(r.dtype)
        if na:
            @pl.when(pl.program_id(0) == 0)
            def _():
                for r in refs[nr + nf + no:]:
                    r[...] = jnp.zeros_like(r)
            for r, v in zip(refs[nr + nf + no:], accs):
                r[...] += v

    in_specs = [pl.BlockSpec((tr, w), functools.partial(lambda cb, i: (i, cb), cb)) for _, w, cb in row_ins]
    in_specs += [pl.BlockSpec(f.shape, lambda i: (0, 0)) for f in full_ins]
    out_specs = [pl.BlockSpec((tr, o[0]), lambda i: (i, 0)) for o in row_outs]
    out_specs += [pl.BlockSpec(s, lambda i: (0, 0)) for s in acc_outs]
    out_shape = [jax.ShapeDtypeStruct((rows, o[-1] if len(o) == 3 else o[0]), o[1]) for o in row_outs]
    out_shape += [jax.ShapeDtypeStruct(s, F32) for s in acc_outs]
    return _call(body, name=name, grid=(rows // tr,), in_specs=in_specs, out_specs=out_specs, out_shape=out_shape,
                 args=tuple(a for a, _, _ in row_ins) + tuple(full_ins), sem=("arbitrary",), ride=ride)


def _cols_call(fn, *, rows, cols, cw, col_ins, par_ins, col_outs, par_outs, name, ride=None, into=None):
    nc, npar = len(col_ins), len(par_ins)

    def body(*refs):
        vals = [r[...] for r in refs[:nc + npar]]
        outs, pouts = fn(*vals)
        for r, v in zip(refs[nc + npar:], tuple(outs) + tuple(pouts)):
            r[...] = v.astype(r.dtype)

    in_specs = [pl.BlockSpec((rows, cw), functools.partial(lambda off, j: (0, off + j), off)) for _, off in col_ins]
    in_specs += [pl.BlockSpec((p.shape[0], cw), functools.partial(lambda off, j: (0, off + j), off))
                 for p, off in par_ins]
    out_specs = [pl.BlockSpec((rows, cw), lambda j: (0, j)) for _ in col_outs]
    out_specs += [pl.BlockSpec((k, cw), lambda j: (0, j)) for k in par_outs]
    out_shape = [jax.ShapeDtypeStruct((rows, cols), dt) for dt in col_outs]
    out_shape += [jax.ShapeDtypeStruct((k, cols), F32) for k in par_outs]
    if into is not None:
        out_specs[0] = pl.BlockSpec((rows, cw), lambda j: (0, into[1] + j))
        out_shape[0] = jax.ShapeDtypeStruct(into[0].shape, into[0].dtype)
    return _call(body, name=name, grid=(cols // cw,), in_specs=in_specs, out_specs=out_specs, out_shape=out_shape,
                 args=tuple(a for a, _ in col_ins) + tuple(p for p, _ in par_ins), sem=("arbitrary",), ride=ride,
                 base=None if into is None else (into[0], 0))


def _sigmoid(v):
    return 1.0 / (1.0 + jnp.exp(-v))


def _softplus(v):
    return jnp.maximum(v, 0.0) + jnp.log(1.0 + jnp.exp(-jnp.abs(v)))


def _rms(v, g):
    return v * lax.rsqrt(jnp.mean(v * v, axis=-1, keepdims=True) + EPS) * g


def _shift_down(v, s, row):
    return jnp.where(row >= s, pltpu.roll(v, s, 0), 0.0)


def _shift_up(v, s, row):
    n = v.shape[0]
    return jnp.where(row < n - s, pltpu.roll(v, n - s, 0), 0.0)


def _causal_conv(u, w, row):
    k_taps = w.shape[0]
    acc = u * w[k_taps - 1:k_taps, :]
    for k in range(k_taps - 1):
        acc = acc + _shift_down(u, k_taps - 1 - k, row) * w[k:k + 1, :]
    return acc


def _causal_conv_bwd(u, dy, w, row):
    k_taps = w.shape[0]
    tap = lax.broadcasted_iota(jnp.int32, w.shape, 0)
    du = dy * w[k_taps - 1:k_taps, :]
    dw = jnp.where(tap == k_taps - 1, jnp.sum(dy * u, axis=0, keepdims=True), 0.0)
    for k in range(k_taps - 1):
        s = k_taps - 1 - k
        du = du + _shift_up(dy, s, row) * w[k:k + 1, :]
        dw = dw + jnp.where(tap == k, jnp.sum(dy * _shift_down(u, s, row), axis=0, keepdims=True), 0.0)
    return du, dw


def _conv_silu_fwd(u, w, b):
    u = u.astype(F32)
    row = lax.broadcasted_iota(jnp.int32, u.shape, 0)
    pre = _causal_conv(u, w, row) + b
    return (pre * _sigmoid(pre),), ()


def _conv_silu_bwd(u, dy, w, b):
    u = u.astype(F32)
    dy = dy.astype(F32)
    row = lax.broadcasted_iota(jnp.int32, u.shape, 0)
    pre = _causal_conv(u, w, row) + b
    s = _sigmoid(pre)
    dpre = dy * (s * (1.0 + pre * (1.0 - s)))
    du, dw = _causal_conv_bwd(u, dpre, w, row)
    return (du,), (dw, jnp.sum(dpre, axis=0, keepdims=True))


def _shortconv_fwd(gb, gc, u, w):
    gb, gc, u = gb.astype(F32), gc.astype(F32), u.astype(F32)
    row = lax.broadcasted_iota(jnp.int32, u.shape, 0)
    return (gb * _causal_conv(gc * u, w, row),), ()


def _shortconv_bwd(gb, gc, u, dy, w):
    gb, gc, u, dy = gb.astype(F32), gc.astype(F32), u.astype(F32), dy.astype(F32)
    row = lax.broadcasted_iota(jnp.int32, u.shape, 0)
    v = gc * u
    dgb = dy * _causal_conv(v, w, row)
    dv, dw = _causal_conv_bwd(v, dy * gb, w, row)
    return (dgb, dv * u, dv * gc), (dw,)


def _split3(v):
    hi = v.astype(BF16)
    r1 = v - hi.astype(F32)
    mid = r1.astype(BF16)
    lo = (r1 - mid.astype(F32)).astype(BF16)
    return hi, mid, lo


def _exact_dot(v, m01, dims, v_is_lhs):
    def one(p):
        return (lax.dot_general(p, m01, dims, preferred_element_type=F32) if v_is_lhs
                else lax.dot_general(m01, p, dims, preferred_element_type=F32))
    hi, mid, lo = _split3(v)
    return (one(lo) + one(mid)) + one(hi)


_NN = (((1,), (0,)), ((), ()))
_NT = (((1,), (1,)), ((), ()))
_TN = (((0,), (0,)), ((), ()))


@jax.custom_vjp
def _cumsum_rows(tril, v):
    return _exact_dot(v, tril, _NN, False)


def _cumsum_rows_fwd(tril, v):
    return _cumsum_rows(tril, v), tril


def _cumsum_rows_bwd(tril, ct):
    return None, _exact_dot(ct, tril, _TN, False)


_cumsum_rows.defvjp(_cumsum_rows_fwd, _cumsum_rows_bwd)


@jax.custom_vjp
def _cumsum_lanes(tril, v):
    return _exact_dot(v, tril, _NT, True)


def _cumsum_lanes_fwd(tril, v):
    return _cumsum_lanes(tril, v), tril


def _cumsum_lanes_bwd(tril, ct):
    return None, _exact_dot(ct, tril, _NN, True)


_cumsum_lanes.defvjp(_cumsum_lanes_fwd, _cumsum_lanes_bwd)


@jax.custom_vjp
def _expand(e01, v):
    return _exact_dot(v, e01, _NN, True)


def _expand_fwd(e01, v):
    return _expand(e01, v), e01


def _expand_bwd(e01, ct):
    return None, _exact_dot(ct, e01, _NT, True)


_expand.defvjp(_expand_fwd, _expand_bwd)


def _causal_mask(n):
    li = lax.broadcasted_iota(jnp.int32, (n, n), 0)
    si = lax.broadcasted_iota(jnp.int32, (n, n), 1)
    return si <= li


def _dt_prep(dtc, dtr, bias_r, bias_c, alog_r, alog_c):
    dt_c = _softplus(dtc + bias_r)
    dt_r = _softplus(dtr + bias_c)
    tril = jnp.where(_causal_mask(dtc.shape[0]), 1.0, 0.0).astype(BF16)
    cs_c = _cumsum_rows(tril, dt_c * (-jnp.exp(alog_r)))
    cs_r = _cumsum_lanes(tril, dt_r * (-jnp.exp(alog_c)))
    return dt_c, cs_c, cs_r


def _ssd_chunk(r_heads, xs, bg, cg, dt_c, cs_c, cs_rg, e01, dskip_e, hp):
    l_len, rp = xs.shape
    p = rp // r_heads
    causal = _causal_mask(l_len)
    lane_head = lax.broadcasted_iota(jnp.int32, (1, rp), 1) // p
    dt_e = _expand(e01, dt_c)
    cs_e = _expand(e01, cs_c)
    cl_e = cs_e[l_len - 1:l_len, :]
    x = xs * dt_e
    bgb, cgb = bg.astype(BF16), cg.astype(BF16)
    cb = lax.dot_general(cgb, bgb, _NT, preferred_element_type=F32)
    ms, xm = [], []
    for r in range(r_heads):
        seg = cs_e[:, r * p:r * p + 1] - cs_rg[r:r + 1, :]
        decay = jnp.exp(jnp.where(causal, seg, -1e30))
        ms.append((cb * decay).astype(BF16))
        xm.append(jnp.where(lane_head == r, x, 0.0).astype(BF16))
    y_diag = lax.dot_general(jnp.concatenate(ms, axis=1), jnp.concatenate(xm, axis=0), _NN,
                             preferred_element_type=F32)
    y_off = lax.dot_general(cgb, hp.astype(BF16), _NN, preferred_element_type=F32) * jnp.exp(cs_e)
    xd = (x * jnp.exp(cl_e - cs_e)).astype(BF16)
    states = lax.dot_general(bgb, xd, _TN, preferred_element_type=F32)
    h_next = hp * jnp.exp(cl_e) + states
    y = y_diag + y_off + dskip_e * xs
    return y, h_next


def _ssd_dt(dtc, dtr, small, cots=None):
    t_len, heads = dtc.shape[0], dtr.shape[0]
    nc = t_len // CHUNK
    col = pl.BlockSpec((CHUNK, LANES), lambda c: (c, 0))
    row = pl.BlockSpec((heads, CHUNK), lambda c: (0, c))
    full = [pl.BlockSpec(s.shape, lambda c: (0, 0)) for s in small]
    shapes = [jax.ShapeDtypeStruct((t_len, LANES), F32), jax.ShapeDtypeStruct((t_len, LANES), F32),
              jax.ShapeDtypeStruct((heads, t_len), F32)]
    if cots is None:
        def body(dtc_ref, dtr_ref, br, bc, ar, ac, dt_ref, csc_ref, csr_ref):
            dt_ref[...], csc_ref[...], csr_ref[...] = _dt_prep(dtc_ref[...], dtr_ref[...], br[...], bc[...],
                                                                ar[...], ac[...])
        return _call(body, name="ssd_dt", grid=(nc,), in_specs=[col, row] + full, out_specs=[col, col, row],
                     out_shape=shapes, args=(dtc, dtr, *small), sem=("parallel",))[0]

    g_dt, g_csc, g_csr, ddk, e01 = cots

    def body(dtc_ref, dtr_ref, br, bc, ar, ac, g_dt_ref, g_csc_ref, g_csr_ref, ddk_ref, e_ref,
             ddtc_ref, ddtr_ref, *dsmall):
        _, vjp = jax.vjp(_dt_prep, dtc_ref[...], dtr_ref[...], br[...], bc[...], ar[...], ac[...])
        grads = vjp((g_dt_ref[...], g_csc_ref[...], g_csr_ref[...]))
        ddtc_ref[...], ddtr_ref[...] = grads[0], grads[1]
        ddk8 = jnp.broadcast_to(ddk_ref[...], (8, ddk_ref.shape[1]))
        dskip = _exact_dot(ddk8, e_ref[...], _NT, True)[0:1, :]

        @pl.when(pl.program_id(0) == 0)
        def _():
            for r in dsmall:
                r[...] = jnp.zeros_like(r)

        for r, gr in zip(dsmall, tuple(grads[2:]) + (dskip,)):
            r[...] += gr

    acc = list(small) + [small[0]]
    return _call(body, name="d_ssd_dt", grid=(nc,),
                 in_specs=[col, row] + full + [col, col, row, pl.BlockSpec((None, 1, e01.shape[1]), lambda c: (c, 0, 0)),
                                               pl.BlockSpec(e01.shape, lambda c: (0, 0))],
                 out_specs=[col, row] + [pl.BlockSpec(s.shape, lambda c: (0, 0)) for s in acc],
                 out_shape=[shapes[0], shapes[2]] + [jax.ShapeDtypeStruct(s.shape, F32) for s in acc],
                 args=(dtc, dtr, *small, g_dt, g_csc, g_csr, ddk, e01), sem=("arbitrary",))[0]


def _ssd_specs(t_len, d_ssm, r_heads, reverse):
    rp = r_heads * HEADDIM
    nc = t_len // CHUNK
    per = next(p for p in (SSD_CHUNKS_PER_STEP, 2, 1) if nc % p == 0)
    ns, rows, gs = nc // per, per * CHUNK, SSD_GROUPS_PER_STEP
    cidx = (lambda c: ns - 1 - c) if reverse else (lambda c: c)
    b_off = d_ssm // (N_STATE * gs)
    specs = dict(
        xs=pl.BlockSpec((rows, gs * rp), lambda c, g: (cidx(c), g)),
        b=pl.BlockSpec((rows, gs * N_STATE), lambda c, g: (cidx(c), b_off + g)),
        c=pl.BlockSpec((rows, gs * N_STATE), lambda c, g: (cidx(c), b_off + N_GROUPS // gs + g)),
        grad_bc=pl.BlockSpec((rows, gs * N_STATE), lambda c, g: (cidx(c), g)),
        col=pl.BlockSpec((rows, LANES), lambda c, g: (cidx(c), 0)),
        csr=pl.BlockSpec((gs, r_heads, rows), lambda c, g: (g, 0, cidx(c))),
        e01=pl.BlockSpec((LANES, gs * rp), lambda c, g: (0, g)),
        dskip=pl.BlockSpec((1, gs * rp), lambda c, g: (0, g)),
        hprev=pl.BlockSpec((per, gs, N_STATE, rp), lambda c, g: (cidx(c), g, 0, 0)),
        ddk=pl.BlockSpec((per, 1, gs * rp), lambda c, g: (cidx(c), 0, g)),
    )
    return specs, nc, ns, per, rp


def _ssd_fwd(xbc, dt_c, cs_c, cs_r3, e01, dskip_e, *, d_ssm, r_heads, ride=None):
    t_len = xbc.shape[0]
    sp, nc, ns, per, rp = _ssd_specs(t_len, d_ssm, r_heads, False)

    def body(xs_ref, b_ref, c_ref, dt_ref, csc_ref, csr_ref, e_ref, dk_ref, y_ref, hprev_ref, h_ref):
        c, gp = pl.program_id(0), pl.program_id(1)
        groups = [gp * SSD_GROUPS_PER_STEP + gi for gi in range(SSD_GROUPS_PER_STEP)]

        @pl.when(c == 0)
        def _():
            for g in groups:
                h_ref[g] = jnp.zeros((N_STATE, rp), F32)

        hp = [h_ref[g] for g in groups]
        for s in range(per):
            r = pl.ds(s * CHUNK, CHUNK)
            for gi in range(SSD_GROUPS_PER_STEP):
                cols, bc = pl.ds(gi * rp, rp), pl.ds(gi * N_STATE, N_STATE)
                hprev_ref[s, gi] = hp[gi]
                y, hp[gi] = _ssd_chunk(r_heads, xs_ref[r, cols].astype(F32), b_ref[r, bc].astype(F32),
                                       c_ref[r, bc].astype(F32), dt_ref[r, :], csc_ref[r, :], csr_ref[gi, :, r],
                                       e_ref[:, cols], dk_ref[:, cols], hp[gi])
                y_ref[r, cols] = y
        for gi, g in enumerate(groups):
            h_ref[g] = hp[gi]

    return _call(
        body, name="ssd_fwd", grid=(ns, N_GROUPS // SSD_GROUPS_PER_STEP),
        in_specs=[sp["xs"], sp["b"], sp["c"], sp["col"], sp["col"], sp["csr"], sp["e01"], sp["dskip"]],
        out_specs=[sp["xs"], sp["hprev"]],
        out_shape=[jax.ShapeDtypeStruct((t_len, d_ssm), F32),
                   jax.ShapeDtypeStruct((nc, N_GROUPS, N_STATE, rp), F32)],
        args=(xbc, xbc, xbc, dt_c, cs_c, cs_r3, e01, dskip_e), scratch=[pltpu.VMEM((N_GROUPS, N_STATE, rp), F32)],
        sem=("arbitrary", "arbitrary"), ride=ride)


def _ssd_bwd(xbc, dt_c, cs_c, cs_r3, e01, dskip_e, hprev, dy, *, d_ssm, r_heads, ride=None):
    t_len = xbc.shape[0]
    sp, nc, ns, per, rp = _ssd_specs(t_len, d_ssm, r_heads, True)

    def body(xs_ref, b_ref, c_ref, dt_ref, csc_ref, csr_ref, e_ref, dk_ref, hprev_ref, dy_ref,
             dxs_ref, db_ref, dc_ref, ddt_ref, dcsc_ref, dcsr_ref, ddk_ref, dh_ref):
        c, gp = pl.program_id(0), pl.program_id(1)
        groups = [gp * SSD_GROUPS_PER_STEP + gi for gi in range(SSD_GROUPS_PER_STEP)]

        @pl.when(gp == 0)
        def _():
            ddt_ref[...] = jnp.zeros_like(ddt_ref)
            dcsc_ref[...] = jnp.zeros_like(dcsc_ref)

        @pl.when(c == 0)
        def _():
            for g in groups:
                dh_ref[g] = jnp.zeros((N_STATE, rp), F32)

        dh = [dh_ref[g] for g in groups]
        for s in reversed(range(per)):
            r = pl.ds(s * CHUNK, CHUNK)
            ddt_sum, dcsc_sum = ddt_ref[r, :], dcsc_ref[r, :]
            for gi in range(SSD_GROUPS_PER_STEP):
                cols, bc = pl.ds(gi * rp, rp), pl.ds(gi * N_STATE, N_STATE)
                e01 = e_ref[:, cols]
                fn = lambda xs, bg, cg, dt, csc, csr, dk, hp: _ssd_chunk(r_heads, xs, bg, cg, dt, csc, csr, e01, dk, hp)
                _, vjp = jax.vjp(fn, xs_ref[r, cols].astype(F32), b_ref[r, bc].astype(F32), c_ref[r, bc].astype(F32),
                                 dt_ref[r, :], csc_ref[r, :], csr_ref[gi, :, r], dk_ref[:, cols], hprev_ref[s, gi])
                dxs, dbg, dcg, ddt, dcsc, dcsr, ddk, dh[gi] = vjp((dy_ref[r, cols], dh[gi]))
                dxs_ref[r, cols] = dxs.astype(dxs_ref.dtype)
                db_ref[r, bc] = dbg.astype(db_ref.dtype)
                dc_ref[r, bc] = dcg.astype(dc_ref.dtype)
                ddt_sum, dcsc_sum = ddt_sum + ddt, dcsc_sum + dcsc
                dcsr_ref[gi, :, r] = dcsr
                ddk_ref[s, :, cols] = ddk
            ddt_ref[r, :], dcsc_ref[r, :] = ddt_sum, dcsc_sum
        for gi, g in enumerate(groups):
            dh_ref[g] = dh[gi]

    n_bc = N_GROUPS * N_STATE
    return _call(
        body, name="ssd_bwd", grid=(ns, N_GROUPS // SSD_GROUPS_PER_STEP),
        in_specs=[sp["xs"], sp["b"], sp["c"], sp["col"], sp["col"], sp["csr"], sp["e01"], sp["dskip"], sp["hprev"],
                  sp["xs"]],
        out_specs=[sp["xs"], sp["grad_bc"], sp["grad_bc"], sp["col"], sp["col"], sp["csr"], sp["ddk"]],
        out_shape=[jax.ShapeDtypeStruct((t_len, d_ssm), BF16), jax.ShapeDtypeStruct((t_len, n_bc), BF16),
                   jax.ShapeDtypeStruct((t_len, n_bc), BF16), jax.ShapeDtypeStruct(dt_c.shape, F32),
                   jax.ShapeDtypeStruct(cs_c.shape, F32), jax.ShapeDtypeStruct(cs_r3.shape, F32),
                   jax.ShapeDtypeStruct((nc, 1, d_ssm), F32)],
        args=(xbc, xbc, xbc, dt_c, cs_c, cs_r3, e01, dskip_e, hprev, dy),
        scratch=[pltpu.VMEM((N_GROUPS, N_STATE, rp), F32)], sem=("arbitrary", "arbitrary"), ride=ride)


def _chip_sum(src, sib, *, name):
    rows, cols = src.shape[1:]
    tr = _tile(rows, 256, BF16_ROWS)
    core = lax.axis_index("c").astype(jnp.int32).reshape(1)

    def body(c_ref, a_ref, b_ref, o_ref):
        o_ref[...] = (a_ref[...].astype(F32) + b_ref[...].astype(F32)).astype(o_ref.dtype)

    grid_spec = pltpu.PrefetchScalarGridSpec(
        num_scalar_prefetch=1, grid=(N_CHIPS, rows // tr),
        in_specs=[pl.BlockSpec((None, tr, cols), lambda q, i, c_ref: (2 * q + c_ref[0], i, 0)),
                  pl.BlockSpec((None, tr, cols), lambda q, i, c_ref: (q, i, 0))],
        out_specs=pl.BlockSpec((None, tr, cols), lambda q, i, c_ref: (q, i, 0)))
    return pl.pallas_call(
        body, name=name, grid_spec=grid_spec, out_shape=jax.ShapeDtypeStruct(sib.shape, sib.dtype),
        compiler_params=pltpu.CompilerParams(dimension_semantics=("parallel", "parallel"), vmem_limit_bytes=VMEM_LIMIT),
    )(core, src, sib)


def _adamw(w, g, m, v):
    m = ADAM_B1 * m + (1.0 - ADAM_B1) * g
    v = ADAM_B2 * v + (1.0 - ADAM_B2) * (g * g)
    m_hat = m / (1.0 - ADAM_B1 ** ADAM_STEP)
    v_hat = v / (1.0 - ADAM_B2 ** ADAM_STEP)
    delta = -ADAM_LR * (m_hat / (jnp.sqrt(v_hat) + ADAM_EPS) + ADAM_WD * w)
    return delta, m, v


def _reduce_adamw(parts, w, m, v, *, name):
    n_parts = parts.shape[0]
    rows, cols = w.shape
    tr = _tile(rows, 128, BF16_ROWS)

    def body(p_ref, w_ref, m_ref, v_ref, g_ref, d_ref, mo_ref, vo_ref):
        g = p_ref[0].astype(F32)
        for k in range(1, n_parts):
            g = g + p_ref[k].astype(F32)
        delta, mn, vn = _adamw(w_ref[...], g, m_ref[...], v_ref[...])
        g_ref[...] = g
        d_ref[...] = delta
        mo_ref[...] = mn
        vo_ref[...] = vn

    spec = pl.BlockSpec((tr, cols), lambda i: (i, 0))
    outs, _ = _call(
        body, name=name, grid=(rows // tr,),
        in_specs=[pl.BlockSpec((n_parts, tr, cols), lambda i: (0, i, 0)), spec, spec, spec],
        out_specs=[spec] * 4, out_shape=[jax.ShapeDtypeStruct((rows, cols), F32)] * 4,
        args=(parts, w, m, v), sem=("parallel",))
    return outs


def _move_rows(src, src_row, name, extra=None, extra_row=None):
    rb, n_out, cols = ROW_BLOCK, len(src_row), src.shape[1]
    assert n_out % rb == 0 and src.shape[0] % rb == 0 and src.shape[0] // rb >= 3
    n_blocks, max_b0, seg_cap = n_out // rb, src.shape[0] // rb - 3, 4

    def segments(rows_of, lo):
        segs, r = [], 0
        while r < rb:
            if rows_of[r] < 0:
                r += 1
                continue
            e = r
            while e + 1 < rb and rows_of[e + 1] == rows_of[e] + 1:
                e += 1
            segs.append((r, e + 1, rows_of[r] - r - lo))
            r = e + 1
        assert len(segs) <= seg_cap
        return segs + [(0, 0, 0)] * (seg_cap - len(segs))

    table = []
    for j in range(n_blocks):
        rows_j = list(src_row[j * rb:(j + 1) * rb])
        valid = [v for v in rows_j if v >= 0]
        b0 = min(max((min(valid) // rb) if valid else 0, 0), max_b0)
        assert not valid or max(valid) < (b0 + 3) * rb
        row = [b0] + [v for seg in segments(rows_j, b0 * rb) for v in seg]
        extra_j = [] if extra is None else list(extra_row[j * rb:(j + 1) * rb])
        if extra is not None:
            row += [v for seg in segments(extra_j, 0) for v in seg]
        need_third = bool(valid) and max(valid) >= (b0 + 2) * rb
        third = b0 + 2 if need_third or not table else table[-1][-1]
        row += [int(need_third), int(any(v >= 0 for v in extra_j)), third]
        table.append(row)
    flag_third, flag_extra, col_third = len(table[0]) - 3, len(table[0]) - 2, len(table[0]) - 1
    table = jnp.asarray(table, jnp.int32)

    def select(tbl_ref, j, first, width, col0=0):
        r = lax.broadcasted_iota(jnp.int32, (rb, width), 0)
        c = lax.broadcasted_iota(jnp.int32, (rb, width), 1) + col0
        hit = jnp.zeros((rb, width), jnp.bool_)
        for s in range(seg_cap):
            lo, hi, off = (tbl_ref[j, first + 3 * s + i] for i in range(3))
            hit = hit | ((r >= lo) & (r < hi) & (c == r + off))
        return jnp.where(hit, 1.0, 0.0).astype(BF16)

    def body(tbl_ref, *refs):
        o_ref = refs[-1]
        j = pl.program_id(0)
        sel = select(tbl_ref, j, 1, 2 * rb)
        pick = lambda m, b: lax.dot_general(m, refs[b][...], _NN, preferred_element_type=F32)
        o_ref[...] = (pick(sel[:, :rb], 0) + pick(sel[:, rb:], 1)).astype(o_ref.dtype)

        @pl.when(tbl_ref[j, flag_third] == 1)
        def _():
            o_ref[...] = (o_ref[...].astype(F32) + pick(select(tbl_ref, j, 1, rb, 2 * rb), 2)).astype(o_ref.dtype)

        if extra is not None:
            @pl.when(tbl_ref[j, flag_extra] == 1)
            def _():
                more = lax.dot_general(select(tbl_ref, j, 1 + 3 * seg_cap, extra.shape[0]), refs[3][...], _NN,
                                       preferred_element_type=F32)
                o_ref[...] = (o_ref[...].astype(F32) + more).astype(o_ref.dtype)

    in_specs = [pl.BlockSpec((rb, cols), functools.partial(lambda b, j, tbl: (tbl[j, 0] + b, 0), b)) for b in range(2)]
    in_specs.append(pl.BlockSpec((rb, cols), lambda j, tbl: (tbl[j, col_third], 0)))
    args = [src, src, src]
    if extra is not None:
        in_specs.append(pl.BlockSpec(extra.shape, lambda j, tbl: (0, 0)))
        args.append(extra)
    grid_spec = pltpu.PrefetchScalarGridSpec(num_scalar_prefetch=1, grid=(n_blocks,), in_specs=in_specs,
                                             out_specs=pl.BlockSpec((rb, cols), lambda j, tbl: (j, 0)))
    return pl.pallas_call(
        body, name=name, grid_spec=grid_spec, out_shape=jax.ShapeDtypeStruct((n_out, cols), src.dtype),
        compiler_params=pltpu.CompilerParams(dimension_semantics=("parallel",), vmem_limit_bytes=VMEM_LIMIT),
    )(table, *args)


def _cols_of(g):
    return jnp.transpose(g, (1, 0, 2)).reshape(g.shape[1], -1)


def _pad_to(a, rows, cols):
    return jnp.pad(a, ((0, rows - a.shape[0]), (0, cols - a.shape[1])))


def kernel(x, norm_mix_g, w_in, ssm_conv_w, ssm_conv_b, ssm_dt_bias, ssm_A_log, ssm_D, ssm_norm_g, sc_conv_w, w_out, norm_ffn_g, w_gate, w_up, w_down, norm_final_g, loss_target, m_norm_mix_g, m_w_in, m_ssm_conv_w, m_ssm_conv_b, m_ssm_dt_bias, m_ssm_A_log, m_ssm_D, m_ssm_norm_g, m_sc_conv_w, m_w_out, m_norm_ffn_g, m_w_gate, m_w_up, m_w_down, m_norm_final_g, v_norm_mix_g, v_w_in, v_ssm_conv_w, v_ssm_conv_b, v_ssm_dt_bias, v_ssm_A_log, v_ssm_D, v_ssm_norm_g, v_sc_conv_w, v_w_out, v_norm_ffn_g, v_w_gate, v_w_up, v_w_down, v_norm_final_g):
    t_len, d = x.shape[1], x.shape[2]
    heads = d // HEADDIM
    r_heads = heads // N_GROUPS
    d_xbc = d + 2 * N_GROUPS * N_STATE
    ff_s = w_down.shape[1]
    ff = ff_s * N_DEV
    off_xbc, off_dt = d, d + d_xbc
    off_cb = off_dt + heads
    d_in = off_cb + 3 * d
    in_s = d_in // N_DEV
    in_p = -(-in_s // (2 * BF16_ROWS)) * (2 * BF16_ROWS)
    w_main = 4 * d + d_xbc
    me = 4 * lax.axis_index("x") + 2 * lax.axis_index("y") + lax.axis_index("c")

    x2 = x[0]
    target = loss_target[0]

    tpose = lambda a: jnp.transpose(a[0])
    win_s = _pad_to(tpose(w_in).astype(BF16), in_p, d)
    wg_s, wu_s = tpose(w_gate).astype(BF16), tpose(w_up).astype(BF16)
    wo_s, wd_s = w_out[0].astype(BF16), w_down[0].astype(BF16)
    small_w = jnp.concatenate([_pad_to(ssm_conv_w[0], K_SSM, d_xbc // N_DEV),
                               _pad_to(sc_conv_w[0], K_SC + 1, d_xbc // N_DEV)], axis=0)

    g1, g2, g3 = norm_mix_g, norm_ffn_g, norm_final_g.reshape(1, d)
    gs = ssm_norm_g
    small = [_pad_to(ssm_dt_bias, 1, LANES), ssm_dt_bias.reshape(heads, 1), _pad_to(ssm_A_log, 1, LANES),
             ssm_A_log.reshape(heads, 1)]
    e01 = (lax.broadcasted_iota(jnp.int32, (LANES, d), 1) // HEADDIM
           == lax.broadcasted_iota(jnp.int32, (LANES, d), 0)).astype(BF16)
    dskip_e = jnp.repeat(ssm_D, HEADDIM, axis=1)
    tr = _tile(t_len, 256, 8)
    tr_wide = _tile(t_len, 512, 8)
    tr_ff = _tile(t_len, 128, 8)
    cw = LANES
    slab = lambda col: col // cw

    gin_1, gsm_1 = _gather_chips_relayed([win_s], [small_w], "gather_w_in_chips")
    (n1,), (gin, gsm) = _rows_call(lambda v, g: ((_rms(v, g),), ()), rows=t_len, tr=tr_wide, row_ins=[(x2, d, 0)],
                                   full_ins=[g1], row_outs=[(d, BF16)], acc_outs=[], name="norm_mix",
                                   ride=_gather_sibling([gin_1, gsm_1]))
    in_pieces = []
    for k in range(N_DEV):
        for a, b, dst, shift in ((0, off_dt, 0, 0), (off_dt, off_cb, 1, -off_dt), (off_cb, d_in, 0, -heads)):
            s, e = max(k * in_s, a), min((k + 1) * in_s, b)
            if s < e:
                in_pieces.append((k, s - k * in_s, e - s, dst, s + shift))
    ref_row = lambda t: t if t < off_dt else t + heads
    wtm = _move_rows(gin.reshape(N_DEV * in_p, d),
                     [(ref_row(t) // in_s) * in_p + ref_row(t) % in_s for t in range(w_main)], "place_w_in")
    wtdt = jnp.zeros((LANES, d), BF16)
    for k, r0, n, dst, d0 in in_pieces:
        if dst == 1:
            wtdt = lax.dynamic_update_slice(wtdt, gin[k, r0:r0 + n], (d0, 0))
    cw_ssm = _cols_of(gsm[:, :K_SSM, :])
    cw_sc = _cols_of(gsm[:, K_SSM:K_SSM + K_SC, :d // N_DEV])

    proj, (go_1, gg_1) = _matmul(n1, wtm, tb=True, out_dtype=BF16, name="proj_main",
                                 ride=_gather_chips([wo_s, wg_s]))
    dt_raw, _ = _matmul(n1, wtdt, tb=True, out_dtype=F32, name="proj_dt")
    dt_raw_t = jnp.transpose(dt_raw[:, :heads])
    (xbc,), (go, gg) = _cols_call(_conv_silu_fwd, rows=t_len, cols=d_xbc, cw=cw, col_ins=[(proj, slab(off_xbc))],
                                  par_ins=[(cw_ssm, 0), (ssm_conv_b, 0)], col_outs=[BF16], par_outs=[],
                                  name="ssm_conv", ride=_gather_sibling([go_1, gg_1]))
    dt_c, cs_c, cs_r = _ssd_dt(dt_raw, dt_raw_t, small)
    cs_r3 = cs_r.reshape(N_GROUPS, r_heads, t_len)
    up_cut = int(ff_s * W_UP_GATHER_SPLIT) // BF16_ROWS * BF16_ROWS
    down_cut = int(ff_s * W_DOWN_GATHER_SPLIT) // BF16_ROWS * BF16_ROWS
    half_cut = ff_s // 2 // BF16_ROWS * BF16_ROWS
    (y_ssd, hprev), (gu_1,) = _ssd_fwd(xbc, dt_c, cs_c, cs_r3, e01, dskip_e, d_ssm=d, r_heads=r_heads,
                                       ride=_gather_chips([wu_s], rows=(0, up_cut)))

    def gate_norm(y, z, g):
        z = z.astype(F32)
        return _rms(y * (z * _sigmoid(z)), g)

    (y_mix,), _ = _rows_call(lambda y, z, g: ((gate_norm(y, z, g),), ()), rows=t_len, tr=tr_wide,
                             row_ins=[(y_ssd, d, 0), (proj, d, 0)], full_ins=[gs], row_outs=[(d, BF16, 2 * d)],
                             acc_outs=[], name="ssm_gate_norm")
    wgt, wo = gg.reshape(ff, d), go.reshape(2 * d, d)
    sc0 = slab(d + d_xbc)
    (y_mix,), _ = _cols_call(_shortconv_fwd, rows=t_len, cols=d, cw=cw,
                             col_ins=[(proj, sc0), (proj, sc0 + slab(d)), (proj, sc0 + 2 * slab(d))],
                             par_ins=[(cw_sc, 0)], col_outs=[BF16], par_outs=[], name="shortconv",
                             into=(y_mix, slab(d)))
    h1, (gu_1, gd_1) = _matmul(y_mix, wo, out_dtype=F32, add=x2, name="out_proj", ride=_merge(
        _gather_chips([wu_s], rows=(up_cut, ff_s - up_cut), into=[gu_1]), _gather_chips([wd_s], rows=(0, down_cut))))
    (n2,), (gu,) = _rows_call(lambda v, g: ((_rms(v, g),), ()), rows=t_len, tr=tr_wide, row_ins=[(h1, d, 0)],
                              full_ins=[g2], row_outs=[(d, BF16)], acc_outs=[], name="norm_ffn",
                              ride=_gather_sibling([gu_1]))
    wut = gu.reshape(ff, d)
    g_ff, (gd_1,) = _matmul(n2, wgt, tb=True, out_dtype=BF16, name="ffn_gate",
                            ride=_gather_chips([wd_s], rows=(down_cut, ff_s - down_cut), into=[gd_1]))
    (u_ff, a_ff), (gd,) = _matmul(n2, wut, tb=True, name="ffn_up", ride=_gather_sibling([gd_1]),
                                  post=(lambda uv, gv: (uv, gv * _sigmoid(gv) * uv), [g_ff], [BF16, BF16]),
                                  tn_max=MM_TILE_N_POST)
    wd = gd.reshape(ff, d)
    h2, _ = _matmul(a_ff, wd, out_dtype=F32, add=h1, name="ffn_down")

    def head(hv, tv, g):
        def f(hh, gg_):
            e = _rms(hh, gg_) - tv
            return (0.5 / d) * jnp.sum(e * e)
        val, (dh, dg) = jax.value_and_grad(f, argnums=(0, 1))(hv, g)
        return (dh, dh), (jnp.full((1, LANES), val, F32), dg)

    (dh2, dh2_b, loss_acc, dg3), _ = _rows_call(head, rows=t_len, tr=tr, row_ins=[(h2, d, 0), (target, d, 0)],
                                                full_ins=[g3], row_outs=[(d, F32), (d, BF16)],
                                                acc_outs=[(1, LANES), (1, d)], name="loss_head")
    loss = lax.psum(loss_acc[0, 0], ("x", "y", "c"))

    def act_bwd(dav, gv, uv):
        s = _sigmoid(gv)
        return dav * uv * (s * (1.0 + gv * (1.0 - s))), dav * gv * s

    (dg_ff, du_ff), _ = _matmul(dh2_b, wd, tb=True, name="d_ffn_gate_up",
                                post=(act_bwd, [g_ff, u_ff], [BF16, BF16]), tn_max=MM_TILE_N_POST)
    dwd, _ = _matmul(a_ff, dh2_b, ta=True, out_dtype=BF16, name="d_w_down")
    dwd8 = dwd.reshape(N_DEV, ff_s, d)
    dn2, (sib_d,) = _matmul(dg_ff, wgt, out_dtype=F32, name="d_norm_ffn_out_gate", ride=_scatter_sibling([dwd8]))
    chip_d = _chip_sum(dwd8, sib_d, name="chip_sum_w_down")
    dn2, (parts_d,) = _matmul(du_ff, wut, out_dtype=F32, add=dn2, name="d_norm_ffn_out_up",
                              ride=_scatter_chips([chip_d], rows=(0, half_cut)))
    dwg, (parts_d,) = _matmul(dg_ff, n2, ta=True, out_dtype=BF16, name="d_w_gate",
                              ride=_scatter_chips([chip_d], rows=(half_cut, ff_s - half_cut), into=[parts_d]))
    dwu, _ = _matmul(du_ff, n2, ta=True, out_dtype=BF16, name="d_w_up")
    dwg8, dwu8 = dwg.reshape(N_DEV, ff_s, d), dwu.reshape(N_DEV, ff_s, d)

    def norm_bwd(v, dn, dres, g):
        _, vjp = jax.vjp(_rms, v, g)
        dv, dg = vjp(dn)
        return (dv + dres,), (dg,)

    def norm_bwd_2(v, dn, dres, g):
        (dv,), acc = norm_bwd(v, dn, dres, g)
        return (dv, dv), acc

    (dh1, dh1_b, dg2), (sib_g, sib_u) = _rows_call(norm_bwd_2, rows=t_len, tr=tr,
                                                   row_ins=[(h1, d, 0), (dn2, d, 0), (dh2, d, 0)], full_ins=[g2],
                                                   row_outs=[(d, F32), (d, BF16)], acc_outs=[(1, d)], name="d_norm_ffn",
                                                   ride=_scatter_sibling([dwg8, dwu8]))
    chip_g = _chip_sum(dwg8, sib_g, name="chip_sum_w_gate")
    chip_u = _chip_sum(dwu8, sib_u, name="chip_sum_w_up")

    dy_mix, _ = _matmul(dh1_b, wo, tb=True, out_dtype=BF16, name="d_y_mix")
    dwo, _ = _matmul(y_mix, dh1_b, ta=True, out_dtype=BF16, name="d_w_out")
    dwo8 = dwo.reshape(N_DEV, 2 * d // N_DEV, d)
    (dgb, dgc, du, dcw_sc), (sib_o,) = _cols_call(
        _shortconv_bwd, rows=t_len, cols=d, cw=cw,
        col_ins=[(proj, sc0), (proj, sc0 + slab(d)), (proj, sc0 + 2 * slab(d)), (dy_mix, slab(d))],
        par_ins=[(cw_sc, 0)], col_outs=[BF16] * 3, par_outs=[K_SC], name="d_shortconv",
        ride=_scatter_sibling([dwo8]))
    chip_o = _chip_sum(dwo8, sib_o, name="chip_sum_w_out")

    def gate_norm_bwd(y, z, dyo, g):
        _, vjp = jax.vjp(gate_norm, y, z.astype(F32), g)
        dy, dz, dg = vjp(dyo.astype(F32))
        return (dy, dz), (dg,)

    (dy_ssd, dproj, dgs), _ = _rows_call(gate_norm_bwd, rows=t_len, tr=tr,
                                         row_ins=[(y_ssd, d, 0), (proj, d, 0), (dy_mix, d, 0)], full_ins=[gs],
                                         row_outs=[(d, F32), (d, BF16, w_main)], acc_outs=[(1, d)],
                                         name="d_ssm_gate_norm")
    (dxs, dbm, dcm, g_dt, g_csc, g_csr3, ddk), (parts_g, parts_o) = _ssd_bwd(
        xbc, dt_c, cs_c, cs_r3, e01, dskip_e, hprev, dy_ssd, d_ssm=d, r_heads=r_heads,
        ride=_scatter_chips([chip_g, chip_o]))
    ddt_c, ddt_r, dbias_r, dbias_c, dalog_r, dalog_c, ddskip = _ssd_dt(
        dt_raw, dt_raw_t, small, cots=(g_dt, g_csc, g_csr3.reshape(heads, t_len), ddk, e01))
    dcw_parts, dcb_parts, col0 = [], [], 0
    for tag, dpart in (("x", dxs), ("b", dbm), ("c", dcm)):
        (dproj, dcw_p, dcb_p), _ = _cols_call(
            _conv_silu_bwd, rows=t_len, cols=dpart.shape[1], cw=cw,
            col_ins=[(proj, slab(off_xbc + col0)), (dpart, 0)], par_ins=[(cw_ssm, slab(col0)), (ssm_conv_b, slab(col0))],
            col_outs=[BF16], par_outs=[K_SSM, 1], name="d_ssm_conv_" + tag, into=(dproj, slab(off_xbc + col0)))
        dcw_parts.append(dcw_p)
        dcb_parts.append(dcb_p)
        col0 += dpart.shape[1]
    dcw_ssm, dcb_ssm = jnp.concatenate(dcw_parts, axis=1), jnp.concatenate(dcb_parts, axis=1)
    for i, part in enumerate((dgb, dgc, du)):
        dproj = lax.dynamic_update_slice(dproj, part, (0, d + d_xbc + i * d))
    ddt = ddt_c + _pad_to(jnp.transpose(ddt_r), t_len, LANES)
    dwm, (parts_u,) = _matmul(dproj, n1, ta=True, out_dtype=BF16, name="d_w_in_main",
                              ride=_scatter_chips([chip_u]))
    dwdt, _ = _matmul(ddt, n1, ta=True, out_dtype=BF16, name="d_w_in_dt")
    own_ref = [k * in_s + i if i < in_s else -1 for k in range(N_DEV) for i in range(in_p)]
    dwin8 = _move_rows(
        dwm, [-1 if g < 0 or off_dt <= g < off_cb else (g if g < off_dt else g - heads) for g in own_ref],
        "place_d_w_in", extra=dwdt, extra_row=[g - off_dt if off_dt <= g < off_cb else -1 for g in own_ref],
    ).reshape(N_DEV, in_p, d)
    dn1, (sib_in,) = _matmul(ddt, wtdt, out_dtype=F32, name="d_norm_mix_out_dt", ride=_scatter_sibling([dwin8]))
    chip_in = _chip_sum(dwin8, sib_in, name="chip_sum_w_in")
    cut = int(in_p * W_IN_SCATTER_SPLIT) // BF16_ROWS * BF16_ROWS
    dn1, (parts_in,) = _matmul(dproj, wtm, out_dtype=F32, add=dn1, name="d_norm_mix_out",
                               ride=_scatter_chips([chip_in], rows=(0, cut)))
    (dx, dg1), _ = _rows_call(norm_bwd, rows=t_len, tr=tr, row_ins=[(x2, d, 0), (dn1, d, 0), (dh1, d, 0)],
                              full_ins=[g1], row_outs=[(d, F32)], acc_outs=[(1, d)], name="d_norm_mix")

    wide = d_xbc
    rows_small = [dg1, dcb_ssm, dbias_r + _pad_to(dbias_c.reshape(1, heads), 1, LANES),
                  dalog_r + _pad_to(dalog_c.reshape(1, heads), 1, LANES), ddskip, dgs, dg2, dg3]
    packed = jnp.concatenate([_pad_to(r, 1, wide) for r in rows_small]
                             + [dcw_ssm, _pad_to(dcw_sc, K_SC, wide), jnp.zeros((1, wide), F32)], axis=0)
    p_small, parts_in = _comm(_merge(_gather_all([packed]), _scatter_chips([chip_in], rows=(cut, in_p - cut),
                                                                           into=[parts_in])), "gather_small_grads")

    conv_lo = me * (d_xbc // N_DEV)
    sc_lo = me * (d // N_DEV)

    def pack_state(vals):
        (nm, cb, dtb, al, dk, sg, nf, nfin, cws, scs) = vals
        rows = [_pad_to(a.reshape(1, -1), 1, wide) for a in (nm, cb, dtb, al, dk, sg, nf, nfin)]
        cws_full = lax.dynamic_update_slice(jnp.zeros((K_SSM, wide), F32), cws[0], (0, conv_lo))
        scs_full = lax.dynamic_update_slice(jnp.zeros((K_SC, wide), F32), scs[0], (0, sc_lo))
        return jnp.concatenate(rows + [cws_full, scs_full, jnp.zeros((1, wide), F32)], axis=0)

    w_small = pack_state((norm_mix_g, ssm_conv_b, ssm_dt_bias, ssm_A_log, ssm_D, ssm_norm_g, norm_ffn_g, norm_final_g,
                          ssm_conv_w, sc_conv_w))
    m_small = pack_state((m_norm_mix_g, m_ssm_conv_b, m_ssm_dt_bias, m_ssm_A_log, m_ssm_D, m_ssm_norm_g, m_norm_ffn_g,
                          m_norm_final_g, m_ssm_conv_w, m_sc_conv_w))
    v_small = pack_state((v_norm_mix_g, v_ssm_conv_b, v_ssm_dt_bias, v_ssm_A_log, v_ssm_D, v_ssm_norm_g, v_norm_ffn_g,
                          v_norm_final_g, v_ssm_conv_w, v_sc_conv_w))

    tin = lambda a: _pad_to(tpose(a), in_p, d)
    tin_back = lambda a: jnp.transpose(a[:in_s])[None]
    t_back = lambda a: jnp.transpose(a)[None]
    upd = {
        "w_in": [tin_back(o) for o in _reduce_adamw(parts_in, tin(w_in), tin(m_w_in), tin(v_w_in), name="adamw_w_in")],
        "w_out": [o[None] for o in _reduce_adamw(parts_o, w_out[0], m_w_out[0], v_w_out[0], name="adamw_w_out")],
        "w_gate": [t_back(o) for o in _reduce_adamw(parts_g, tpose(w_gate), tpose(m_w_gate), tpose(v_w_gate),
                                                    name="adamw_w_gate")],
        "w_up": [t_back(o) for o in _reduce_adamw(parts_u, tpose(w_up), tpose(m_w_up), tpose(v_w_up),
                                                  name="adamw_w_up")],
        "w_down": [o[None] for o in _reduce_adamw(parts_d, w_down[0], m_w_down[0], v_w_down[0], name="adamw_w_down")],
    }
    small_upd = _reduce_adamw(p_small, w_small, m_small, v_small, name="adamw_small")

    def unpack(packed_out):
        vec = lambda i, n, shape: packed_out[i, :n].reshape(shape)
        return {
            "norm_mix_g": vec(0, d, (1, d)), "ssm_conv_b": vec(1, d_xbc, (1, d_xbc)),
            "ssm_dt_bias": vec(2, heads, (1, heads)), "ssm_A_log": vec(3, heads, (1, heads)),
            "ssm_D": vec(4, heads, (1, heads)), "ssm_norm_g": vec(5, d, (1, d)), "norm_ffn_g": vec(6, d, (1, d)),
            "norm_final_g": vec(7, d, (d,)),
            "ssm_conv_w": lax.dynamic_slice(packed_out[8:8 + K_SSM], (0, conv_lo), (K_SSM, d_xbc // N_DEV))[None],
            "sc_conv_w": lax.dynamic_slice(packed_out[8 + K_SSM:8 + K_SSM + K_SC], (0, sc_lo), (K_SC, d // N_DEV))[None],
        }

    names = ["norm_mix_g", "w_in", "ssm_conv_w", "ssm_conv_b", "ssm_dt_bias", "ssm_A_log", "ssm_D", "ssm_norm_g",
             "sc_conv_w", "w_out", "norm_ffn_g", "w_gate", "w_up", "w_down", "norm_final_g"]
    outs = []
    for kind in range(4):
        small_k = unpack(small_upd[kind])
        for nm in names:
            outs.append(upd[nm][kind] if nm in upd else small_k[nm])
    return (loss, dx[None], *outs)
```

```python
import collections
import functools

import jax
import jax.numpy as jnp
from jax import lax
from jax.experimental import pallas as pl
from jax.experimental.pallas import tpu as pltpu

F32 = jnp.float32
BF16 = jnp.bfloat16

N_DEV = 8
N_CHIPS = 4
HEADDIM = 64
N_GROUPS = 8
N_STATE = 128
CHUNK = 128
K_SSM = 4
K_SC = 3
EPS = 1e-5
LANES = 128
BF16_ROWS = 16
MM_TILE_MN = 1408
MM_TILE_K = 2816
W_IN_SCATTER_SPLIT = 13 / 14
W_UP_GATHER_SPLIT = 0.7
W_DOWN_GATHER_SPLIT = 0.3
MM_TILE_N_POST = 704
SSD_CHUNKS_PER_STEP = 4
SSD_GROUPS_PER_STEP = 4
ROW_BLOCK = 256
V7X_VMEM_BYTES = 64 * 1024 * 1024
VMEM_LIMIT = (V7X_VMEM_BYTES * 3) // 4

ADAM_LR = 0.001
ADAM_B1 = 0.9
ADAM_B2 = 0.999
ADAM_EPS = 1e-08
ADAM_WD = 0.01
ADAM_STEP = 10


def _tile(n, pref, align):
    t = min(pref, n)
    t -= t % align
    while t >= align:
        if n % t == 0:
            return t
        t -= align
    return n


_Ride = collections.namedtuple("_Ride", ["ins", "out_shapes", "aliases", "nsem", "plan"])
_ANY = pl.BlockSpec(memory_space=pl.ANY)


def _coords():
    return lax.axis_index("x"), lax.axis_index("y"), lax.axis_index("c")


def _other_chips(x, y):
    return ((1 - x, y), (x, 1 - y), (1 - x, 1 - y))


def _remote(src, dst, send, recv, k, dev):
    return functools.partial(pltpu.make_async_remote_copy, src_ref=src, dst_ref=dst, send_sem=send.at[k],
                             recv_sem=recv.at[k], device_id=dev, device_id_type=pl.DeviceIdType.MESH)


def _local(src, dst, sem):
    return functools.partial(pltpu.make_async_copy, src, dst, sem)


def _start_all(plan):
    for kind, make in plan:
        if kind != "arrival":
            make().start()


def _wait_all(plan):
    for kind, make in plan:
        if kind == "local":
            make().wait()
        elif kind == "out":
            make().wait_send()
        else:
            make().wait_recv()


def _gather_chips(srcs, rows=None, into=None):
    n = len(srcs)

    def plan(ins, outs, send, recv, base):
        x, y, c = _coords()
        me = 4 * x + 2 * y + c
        cut = (lambda ref: ref) if rows is None else (lambda ref: ref.at[pl.ds(rows[0], rows[1])])
        d = []
        for a, (src, dst) in enumerate(zip(ins[:n], outs)):
            k = base + 4 * a
            d.append(("local", _local(cut(src), cut(dst.at[me]), send.at[k + 3])))
            for j, (px, py) in enumerate(_other_chips(x, y)):
                d.append(("out", _remote(cut(src), cut(dst.at[me]), send, recv, k + j, (px, py, c))))
                d.append(("arrival", _remote(cut(src), cut(dst.at[4 * px + 2 * py + c]), send, recv, k + j,
                                             (px, py, c))))
        return d
    shapes = [jax.ShapeDtypeStruct((N_DEV,) + s.shape, s.dtype) for s in srcs]
    if into is None:
        return _Ride(list(srcs), shapes, {}, 4 * n, plan)
    return _Ride(list(srcs) + list(into), shapes, {n + a: a for a in range(n)}, 4 * n, plan)


def _gather_sibling(bufs):
    def plan(ins, outs, send, recv, base):
        x, y, c = _coords()
        d = []
        for a, buf in enumerate(outs):
            for q in range(N_CHIPS):
                k = base + 4 * a + q
                d.append(("out", _remote(buf.at[2 * q + c], buf.at[2 * q + c], send, recv, k, (x, y, 1 - c))))
                d.append(("arrival", _remote(buf.at[2 * q + c], buf.at[2 * q + 1 - c], send, recv, k, (x, y, 1 - c))))
        return d
    shapes = [jax.ShapeDtypeStruct(b.shape, b.dtype) for b in bufs]
    return _Ride(list(bufs), shapes, {i: i for i in range(len(bufs))}, 4 * len(bufs), plan)


def _scatter_sibling(srcs):
    def plan(ins, outs, send, recv, base):
        x, y, c = _coords()
        d = []
        for a, (src, sib) in enumerate(zip(ins, outs)):
            for q in range(N_CHIPS):
                k = base + 4 * a + q
                d.append(("out", _remote(src.at[2 * q + 1 - c], sib.at[q], send, recv, k, (x, y, 1 - c))))
                d.append(("arrival", _remote(src.at[2 * q + 1 - c], sib.at[q], send, recv, k, (x, y, 1 - c))))
        return d
    shapes = [jax.ShapeDtypeStruct((N_CHIPS,) + s.shape[1:], s.dtype) for s in srcs]
    return _Ride(list(srcs), shapes, {}, 4 * len(srcs), plan)


def _scatter_chips(chips, rows=None, into=None):
    n = len(chips)

    def plan(ins, outs, send, recv, base):
        x, y, c = _coords()
        mine = 2 * x + y
        cut = (lambda ref: ref) if rows is None else (lambda ref: ref.at[pl.ds(rows[0], rows[1])])
        d = []
        for a, (chip, parts) in enumerate(zip(ins[:n], outs)):
            k = base + 4 * a
            d.append(("local", _local(cut(chip.at[mine]), cut(parts.at[mine]), send.at[k + 3])))
            for j, (px, py) in enumerate(_other_chips(x, y)):
                q = 2 * px + py
                d.append(("out", _remote(cut(chip.at[q]), cut(parts.at[mine]), send, recv, k + j, (px, py, c))))
                d.append(("arrival", _remote(cut(chip.at[q]), cut(parts.at[q]), send, recv, k + j, (px, py, c))))
        return d
    shapes = [jax.ShapeDtypeStruct(s.shape, s.dtype) for s in chips]
    if into is None:
        return _Ride(list(chips), shapes, {}, 4 * n, plan)
    return _Ride(list(chips) + list(into), shapes, {n + a: a for a in range(n)}, 4 * n, plan)


def _gather_all(srcs):
    def plan(ins, outs, send, recv, base):
        x, y, c = _coords()
        me = 4 * x + 2 * y + c
        d = []
        for a, (src, dst) in enumerate(zip(ins, outs)):
            k = base + N_DEV * a
            d.append(("local", _local(src, dst.at[me], send.at[k])))
            for j in range(1, N_DEV):
                px = 1 - x if (j >> 2) & 1 else x
                py = 1 - y if (j >> 1) & 1 else y
                pc = 1 - c if j & 1 else c
                d.append(("out", _remote(src, dst.at[me], send, recv, k + j, (px, py, pc))))
                d.append(("arrival", _remote(src, dst.at[4 * px + 2 * py + pc], send, recv, k + j, (px, py, pc))))
        return d
    shapes = [jax.ShapeDtypeStruct((N_DEV,) + s.shape, s.dtype) for s in srcs]
    return _Ride(list(srcs), shapes, {}, N_DEV * len(srcs), plan)


def _merge(*rides):
    ins, outs, aliases, parts, nsem = [], [], {}, [], 0
    for r in rides:
        parts.append((len(ins), len(outs), nsem, r))
        aliases.update({len(ins) + i: len(outs) + j for i, j in r.aliases.items()})
        ins += r.ins
        outs += r.out_shapes
        nsem += r.nsem

    def plan(i, o, send, recv, base):
        d = []
        for i0, o0, s0, r in parts:
            d += r.plan(i[i0:i0 + len(r.ins)], o[o0:o0 + len(r.out_shapes)], send, recv, base + s0)
        return d
    return _Ride(ins, outs, aliases, nsem, plan)


def _comm(ride, name):
    n_in, n_out = len(ride.ins), len(ride.out_shapes)

    def body(*refs):
        plan = ride.plan(refs[:n_in], refs[n_in:n_in + n_out], refs[-2], refs[-1], 0)
        _start_all(plan)
        _wait_all(plan)

    return pl.pallas_call(
        body, name=name, in_specs=[_ANY] * n_in, out_specs=[_ANY] * n_out, out_shape=ride.out_shapes,
        scratch_shapes=[pltpu.SemaphoreType.DMA((ride.nsem,)), pltpu.SemaphoreType.DMA((ride.nsem,))],
        input_output_aliases=dict(ride.aliases),
        compiler_params=pltpu.CompilerParams(has_side_effects=True),
    )(*ride.ins)


def _gather_chips_relayed(big, small, name):
    srcs = list(big) + list(small)
    n, nsem = len(srcs), 5 * len(srcs)

    def body(*refs):
        ins, outs, send, recv = refs[:n], refs[n:2 * n], refs[-2], refs[-1]
        x, y, c = _coords()
        slot = lambda dev: 4 * dev[0] + 2 * dev[1] + dev[2]
        me, nbr_x, nbr_y, diag = (x, y, c), (1 - x, y, c), (x, 1 - y, c), (1 - x, 1 - y, c)
        own, sends = [], []
        for a, (src, dst) in enumerate(zip(ins, outs)):
            k = 5 * a
            own.append(_local(src, dst.at[slot(me)], send.at[k + 4])())
            sends.append(_remote(src, dst.at[slot(me)], send, recv, k, nbr_x)())
            sends.append(_remote(src, dst.at[slot(me)], send, recv, k + 1, nbr_y)())
            if a >= len(big):
                sends.append(_remote(src, dst.at[slot(me)], send, recv, k + 2, diag)())
        for s in own + sends:
            s.start()
        for a, (src, dst) in enumerate(zip(ins, outs)):
            k = 5 * a
            _remote(src, dst.at[slot(nbr_x)], send, recv, k, nbr_x)().wait_recv()
            if a < len(big):
                half = src.shape[0] // 2
                part = dst.at[slot(nbr_x)].at[pl.ds(0, half)]
                fwd = _remote(part, part, send, recv, k + 2, nbr_y)()
                fwd.start()
                sends.append(fwd)
            _remote(src, dst.at[slot(nbr_y)], send, recv, k + 1, nbr_y)().wait_recv()
            if a < len(big):
                part = dst.at[slot(nbr_y)].at[pl.ds(half, src.shape[0] - half)]
                fwd = _remote(part, part, send, recv, k + 3, nbr_x)()
                fwd.start()
                sends.append(fwd)
        for a, (src, dst) in enumerate(zip(ins, outs)):
            k = 5 * a
            if a < len(big):
                half = src.shape[0] // 2
                lo = dst.at[slot(diag)].at[pl.ds(0, half)]
                hi = dst.at[slot(diag)].at[pl.ds(half, src.shape[0] - half)]
                _remote(lo, lo, send, recv, k + 2, nbr_y)().wait_recv()
                _remote(hi, hi, send, recv, k + 3, nbr_x)().wait_recv()
            else:
                _remote(src, dst.at[slot(diag)], send, recv, k + 2, diag)().wait_recv()
        for lc in own:
            lc.wait()
        for s in sends:
            s.wait_send()

    return pl.pallas_call(
        body, name=name, in_specs=[_ANY] * n, out_specs=[_ANY] * n,
        out_shape=[jax.ShapeDtypeStruct((N_DEV,) + s.shape, s.dtype) for s in srcs],
        scratch_shapes=[pltpu.SemaphoreType.DMA((nsem,)), pltpu.SemaphoreType.DMA((nsem,))],
        compiler_params=pltpu.CompilerParams(has_side_effects=True),
    )(*srcs)


def _call(body, *, name, grid, in_specs, out_specs, out_shape, args, sem, scratch=(), ride=None, base=None):
    params = pltpu.CompilerParams(dimension_semantics=sem, vmem_limit_bytes=VMEM_LIMIT)
    own_aliases = {}
    if base is not None:
        inner, n_host = body, len(args)
        body = lambda *refs: inner(*refs[:n_host], *refs[n_host + 1:])
        own_aliases[n_host] = base[1]
        args, in_specs = tuple(args) + (base[0],), list(in_specs) + [_ANY]
    if ride is None:
        res = pl.pallas_call(body, name=name, grid=grid, in_specs=in_specs, out_specs=out_specs,
                             out_shape=out_shape, scratch_shapes=list(scratch), input_output_aliases=own_aliases,
                             compiler_params=params)(*args)
        return list(res), []
    n_in, n_out, n_scr = len(args), len(out_shape), len(scratch)
    r_in, r_out = len(ride.ins), len(ride.out_shapes)

    def hosted(*refs):
        h_in, rin = refs[:n_in], refs[n_in:n_in + r_in]
        o0 = n_in + r_in
        h_out, rout = refs[o0:o0 + n_out], refs[o0 + n_out:o0 + n_out + r_out]
        s0 = o0 + n_out + r_out
        h_scr, send, recv = refs[s0:s0 + n_scr], refs[s0 + n_scr], refs[s0 + n_scr + 1]
        ids = [pl.program_id(i) for i in range(len(grid))]
        first = functools.reduce(lambda p, q: p & q, [i == 0 for i in ids])
        last = functools.reduce(lambda p, q: p & q, [i == n - 1 for i, n in zip(ids, grid)])

        @pl.when(first)
        def _():
            _start_all(ride.plan(rin, rout, send, recv, 0))

        body(*h_in, *h_out, *h_scr)

        @pl.when(last)
        def _():
            _wait_all(ride.plan(rin, rout, send, recv, 0))

    res = pl.pallas_call(
        hosted, name=name, grid=grid, in_specs=list(in_specs) + [_ANY] * r_in,
        out_specs=list(out_specs) + [_ANY] * r_out, out_shape=list(out_shape) + list(ride.out_shapes),
        scratch_shapes=list(scratch) + [pltpu.SemaphoreType.DMA((ride.nsem,)), pltpu.SemaphoreType.DMA((ride.nsem,))],
        input_output_aliases={**own_aliases, **{n_in + i: n_out + j for i, j in ride.aliases.items()}},
        compiler_params=params,
    )(*args, *ride.ins)
    return list(res[:n_out]), list(res[n_out:])


def _matmul(a, b, *, ta=False, tb=False, out_dtype=BF16, add=None, post=None, name, ride=None, tn_max=MM_TILE_MN):
    m = a.shape[1] if ta else a.shape[0]
    k = a.shape[0] if ta else a.shape[1]
    n = b.shape[0] if tb else b.shape[1]
    assert k == (b.shape[1] if tb else b.shape[0])
    tm, tn, tk = _tile(m, MM_TILE_MN, LANES), _tile(n, tn_max, LANES), _tile(k, MM_TILE_K, LANES)
    nk = k // tk
    dims = (((0 if ta else 1,), (1 if tb else 0,)), ((), ()))
    single = post is None
    if add is not None:
        post = (lambda r, t: (r + t,), [add], [out_dtype])
    elif post is None:
        post = (lambda r: (r,), [], [out_dtype])
    post_fn, extras, out_dtypes = post
    n_ex, n_o = len(extras), len(out_dtypes)

    def body(*refs):
        a_ref, b_ref = refs[:2]
        ex_refs, o_refs = refs[2:2 + n_ex], refs[2 + n_ex:2 + n_ex + n_o]

        def finish(r):
            for o_ref, v in zip(o_refs, post_fn(r, *[e[...].astype(F32) for e in ex_refs])):
                o_ref[...] = v.astype(o_ref.dtype)

        part = lax.dot_general(a_ref[...].astype(BF16), b_ref[...].astype(BF16), dims, preferred_element_type=F32)
        if nk == 1:
            finish(part)
            return
        acc = refs[-1]
        kk = pl.program_id(2)

        @pl.when(kk == 0)
        def _():
            acc[...] = part

        @pl.when((kk > 0) & (kk < nk - 1))
        def _():
            acc[...] += part

        @pl.when(kk == nk - 1)
        def _():
            finish(acc[...] + part)

    a_spec = (pl.BlockSpec((tk, tm), lambda i, j, kk: (kk, i)) if ta
              else pl.BlockSpec((tm, tk), lambda i, j, kk: (i, kk)))
    b_spec = (pl.BlockSpec((tn, tk), lambda i, j, kk: (j, kk)) if tb
              else pl.BlockSpec((tk, tn), lambda i, j, kk: (kk, j)))
    o_spec = pl.BlockSpec((tm, tn), lambda i, j, kk: (i, j))
    outs, rides = _call(
        body, name=name, grid=(m // tm, n // tn, nk),
        in_specs=[a_spec, b_spec] + [o_spec] * n_ex, out_specs=[o_spec] * n_o,
        out_shape=[jax.ShapeDtypeStruct((m, n), dt) for dt in out_dtypes], args=(a, b, *extras),
        scratch=[pltpu.VMEM((tm, tn), F32)] if nk > 1 else [], sem=("parallel", "parallel", "arbitrary"), ride=ride)
    return (outs[0] if single else outs), rides


def _rows_call(fn, *, rows, tr, row_ins, full_ins, row_outs, acc_outs, name, ride=None):
    nr, nf, no, na = len(row_ins), len(full_ins), len(row_outs), len(acc_outs)

    def body(*refs):
        vals = [r[...] for r in refs[:nr + nf]]
        outs, accs = fn(*vals)
        for r, v in zip(refs[nr + nf:nr + nf + no], outs):
            r[...] = v.astype(r.dtype)
        if na:
            @pl.when(pl.program_id(0) == 0)
            def _():
                for r in refs[nr + nf + no:]:
                    r[...] = jnp.zeros_like(r)
            for r, v in zip(refs[nr + nf + no:], accs):
                r[...] += v

    in_specs = [pl.BlockSpec((tr, w), functools.partial(lambda cb, i: (i, cb), cb)) for _, w, cb in row_ins]
    in_specs += [pl.BlockSpec(f.shape, lambda i: (0, 0)) for f in full_ins]
    out_specs = [pl.BlockSpec((tr, o[0]), lambda i: (i, 0)) for o in row_outs]
    out_specs += [pl.BlockSpec(s, lambda i: (0, 0)) for s in acc_outs]
    out_shape = [jax.ShapeDtypeStruct((rows, o[-1] if len(o) == 3 else o[0]), o[1]) for o in row_outs]
    out_shape += [jax.ShapeDtypeStruct(s, F32) for s in acc_outs]
    return _call(body, name=name, grid=(rows // tr,), in_specs=in_specs, out_specs=out_specs, out_shape=out_shape,
                 args=tuple(a for a, _, _ in row_ins) + tuple(full_ins), sem=("arbitrary",), ride=ride)


def _cols_call(fn, *, rows, cols, cw, col_ins, par_ins, col_outs, par_outs, name, ride=None, into=None):
    nc, npar = len(col_ins), len(par_ins)

    def body(*refs):
        vals = [r[...] for r in refs[:nc + npar]]
        outs, pouts = fn(*vals)
        for r, v in zip(refs[nc + npar:], tuple(outs) + tuple(pouts)):
            r[...] = v.astype(r.dtype)

    in_specs = [pl.BlockSpec((rows, cw), functools.partial(lambda off, j: (0, off + j), off)) for _, off in col_ins]
    in_specs += [pl.BlockSpec((p.shape[0], cw), functools.partial(lambda off, j: (0, off + j), off))
                 for p, off in par_ins]
    out_specs = [pl.BlockSpec((rows, cw), lambda j: (0, j)) for _ in col_outs]
    out_specs += [pl.BlockSpec((k, cw), lambda j: (0, j)) for k in par_outs]
    out_shape = [jax.ShapeDtypeStruct((rows, cols), dt) for dt in col_outs]
    out_shape += [jax.ShapeDtypeStruct((k, cols), F32) for k in par_outs]
    if into is not None:
        out_specs[0] = pl.BlockSpec((rows, cw), lambda j: (0, into[1] + j))
        out_shape[0] = jax.ShapeDtypeStruct(into[0].shape, into[0].dtype)
    return _call(body, name=name, grid=(cols // cw,), in_specs=in_specs, out_specs=out_specs, out_shape=out_shape,
                 args=tuple(a for a, _ in col_ins) + tuple(p for p, _ in par_ins), sem=("arbitrary",), ride=ride,
                 base=None if into is None else (into[0], 0))


def _sigmoid(v):
    return 1.0 / (1.0 + jnp.exp(-v))


def _softplus(v):
    return jnp.maximum(v, 0.0) + jnp.log(1.0 + jnp.exp(-jnp.abs(v)))


def _rms(v, g):
    return v * lax.rsqrt(jnp.mean(v * v, axis=-1, keepdims=True) + EPS) * g


def _shift_down(v, s, row):
    return jnp.where(row >= s, pltpu.roll(v, s, 0), 0.0)


def _shift_up(v, s, row):
    n = v.shape[0]
    return jnp.where(row < n - s, pltpu.roll(v, n - s, 0), 0.0)


def _causal_conv(u, w, row):
    k_taps = w.shape[0]
    acc = u * w[k_taps - 1:k_taps, :]
    for k in range(k_taps - 1):
        acc = acc + _shift_down(u, k_taps - 1 - k, row) * w[k:k + 1, :]
    return acc


def _causal_conv_bwd(u, dy, w, row):
    k_taps = w.shape[0]
    tap = lax.broadcasted_iota(jnp.int32, w.shape, 0)
    du = dy * w[k_taps - 1:k_taps, :]
    dw = jnp.where(tap == k_taps - 1, jnp.sum(dy * u, axis=0, keepdims=True), 0.0)
    for k in range(k_taps - 1):
        s = k_taps - 1 - k
        du = du + _shift_up(dy, s, row) * w[k:k + 1, :]
        dw = dw + jnp.where(tap == k, jnp.sum(dy * _shift_down(u, s, row), axis=0, keepdims=True), 0.0)
    return du, dw


def _conv_silu_fwd(u, w, b):
    u = u.astype(F32)
    row = lax.broadcasted_iota(jnp.int32, u.shape, 0)
    pre = _causal_conv(u, w, row) + b
    return (pre * _sigmoid(pre),), ()


def _conv_silu_bwd(u, dy, w, b):
    u = u.astype(F32)
    dy = dy.astype(F32)
    row = lax.broadcasted_iota(jnp.int32, u.shape, 0)
    pre = _causal_conv(u, w, row) + b
    s = _sigmoid(pre)
    dpre = dy * (s * (1.0 + pre * (1.0 - s)))
    du, dw = _causal_conv_bwd(u, dpre, w, row)
    return (du,), (dw, jnp.sum(dpre, axis=0, keepdims=True))


def _shortconv_fwd(gb, gc, u, w):
    gb, gc, u = gb.astype(F32), gc.astype(F32), u.astype(F32)
    row = lax.broadcasted_iota(jnp.int32, u.shape, 0)
    return (gb * _causal_conv(gc * u, w, row),), ()


def _shortconv_bwd(gb, gc, u, dy, w):
    gb, gc, u, dy = gb.astype(F32), gc.astype(F32), u.astype(F32), dy.astype(F32)
    row = lax.broadcasted_iota(jnp.int32, u.shape, 0)
    v = gc * u
    dgb = dy * _causal_conv(v, w, row)
    dv, dw = _causal_conv_bwd(v, dy * gb, w, row)
    return (dgb, dv * u, dv * gc), (dw,)


def _split3(v):
    hi = v.astype(BF16)
    r1 = v - hi.astype(F32)
    mid = r1.astype(BF16)
    lo = (r1 - mid.astype(F32)).astype(BF16)
    return hi, mid, lo


def _exact_dot(v, m01, dims, v_is_lhs):
    def one(p):
        return (lax.dot_general(p, m01, dims, preferred_element_type=F32) if v_is_lhs
                else lax.dot_general(m01, p, dims, preferred_element_type=F32))
    hi, mid, lo = _split3(v)
    return (one(lo) + one(mid)) + one(hi)


_NN = (((1,), (0,)), ((), ()))
_NT = (((1,), (1,)), ((), ()))
_TN = (((0,), (0,)), ((), ()))


@jax.custom_vjp
def _cumsum_rows(tril, v):
    return _exact_dot(v, tril, _NN, False)


def _cumsum_rows_fwd(tril, v):
    return _cumsum_rows(tril, v), tril


def _cumsum_rows_bwd(tril, ct):
    return None, _exact_dot(ct, tril, _TN, False)


_cumsum_rows.defvjp(_cumsum_rows_fwd, _cumsum_rows_bwd)


@jax.custom_vjp
def _cumsum_lanes(tril, v):
    return _exact_dot(v, tril, _NT, True)


def _cumsum_lanes_fwd(tril, v):
    return _cumsum_lanes(tril, v), tril


def _cumsum_lanes_bwd(tril, ct):
    return None, _exact_dot(ct, tril, _NN, True)


_cumsum_lanes.defvjp(_cumsum_lanes_fwd, _cumsum_lanes_bwd)


@jax.custom_vjp
def _expand(e01, v):
    return _exact_dot(v, e01, _NN, True)


def _expand_fwd(e01, v):
    return _expand(e01, v), e01


def _expand_bwd(e01, ct):
    return None, _exact_dot(ct, e01, _NT, True)


_expand.defvjp(_expand_fwd, _expand_bwd)


def _causal_mask(n):
    li = lax.broadcasted_iota(jnp.int32, (n, n), 0)
    si = lax.broadcasted_iota(jnp.int32, (n, n), 1)
    return si <= li


def _dt_prep(dtc, dtr, bias_r, bias_c, alog_r, alog_c):
    dt_c = _softplus(dtc + bias_r)
    dt_r = _softplus(dtr + bias_c)
    tril = jnp.where(_causal_mask(dtc.shape[0]), 1.0, 0.0).astype(BF16)
    cs_c = _cumsum_rows(tril, dt_c * (-jnp.exp(alog_r)))
    cs_r = _cumsum_lanes(tril, dt_r * (-jnp.exp(alog_c)))
    return dt_c, cs_c, cs_r


def _ssd_chunk(r_heads, xs, bg, cg, dt_c, cs_c, cs_rg, e01, dskip_e, hp):
    l_len, rp = xs.shape
    p = rp // r_heads
    causal = _causal_mask(l_len)
    lane_head = lax.broadcasted_iota(jnp.int32, (1, rp), 1) // p
    dt_e = _expand(e01, dt_c)
    cs_e = _expand(e01, cs_c)
    cl_e = cs_e[l_len - 1:l_len, :]
    x = xs * dt_e
    bgb, cgb = bg.astype(BF16), cg.astype(BF16)
    cb = lax.dot_general(cgb, bgb, _NT, preferred_element_type=F32)
    ms, xm = [], []
    for r in range(r_heads):
        seg = cs_e[:, r * p:r * p + 1] - cs_rg[r:r + 1, :]
        decay = jnp.exp(jnp.where(causal, seg, -1e30))
        ms.append((cb * decay).astype(BF16))
        xm.append(jnp.where(lane_head == r, x, 0.0).astype(BF16))
    y_diag = lax.dot_general(jnp.concatenate(ms, axis=1), jnp.concatenate(xm, axis=0), _NN,
                             preferred_element_type=F32)
    y_off = lax.dot_general(cgb, hp.astype(BF16), _NN, preferred_element_type=F32) * jnp.exp(cs_e)
    xd = (x * jnp.exp(cl_e - cs_e)).astype(BF16)
    states = lax.dot_general(bgb, xd, _TN, preferred_element_type=F32)
    h_next = hp * jnp.exp(cl_e) + states
    y = y_diag + y_off + dskip_e * xs
    return y, h_next


def _ssd_dt(dtc, dtr, small, cots=None):
    t_len, heads = dtc.shape[0], dtr.shape[0]
    nc = t_len // CHUNK
    col = pl.BlockSpec((CHUNK, LANES), lambda c: (c, 0))
    row = pl.BlockSpec((heads, CHUNK), lambda c: (0, c))
    full = [pl.BlockSpec(s.shape, lambda c: (0, 0)) for s in small]
    shapes = [jax.ShapeDtypeStruct((t_len, LANES), F32), jax.ShapeDtypeStruct((t_len, LANES), F32),
              jax.ShapeDtypeStruct((heads, t_len), F32)]
    if cots is None:
        def body(dtc_ref, dtr_ref, br, bc, ar, ac, dt_ref, csc_ref, csr_ref):
            dt_ref[...], csc_ref[...], csr_ref[...] = _dt_prep(dtc_ref[...], dtr_ref[...], br[...], bc[...],
                                                                ar[...], ac[...])
        return _call(body, name="ssd_dt", grid=(nc,), in_specs=[col, row] + full, out_specs=[col, col, row],
                     out_shape=shapes, args=(dtc, dtr, *small), sem=("parallel",))[0]

    g_dt, g_csc, g_csr, ddk, e01 = cots

    def body(dtc_ref, dtr_ref, br, bc, ar, ac, g_dt_ref, g_csc_ref, g_csr_ref, ddk_ref, e_ref,
             ddtc_ref, ddtr_ref, *dsmall):
        _, vjp = jax.vjp(_dt_prep, dtc_ref[...], dtr_ref[...], br[...], bc[...], ar[...], ac[...])
        grads = vjp((g_dt_ref[...], g_csc_ref[...], g_csr_ref[...]))
        ddtc_ref[...], ddtr_ref[...] = grads[0], grads[1]
        ddk8 = jnp.broadcast_to(ddk_ref[...], (8, ddk_ref.shape[1]))
        dskip = _exact_dot(ddk8, e_ref[...], _NT, True)[0:1, :]

        @pl.when(pl.program_id(0) == 0)
        def _():
            for r in dsmall:
                r[...] = jnp.zeros_like(r)

        for r, gr in zip(dsmall, tuple(grads[2:]) + (dskip,)):
            r[...] += gr

    acc = list(small) + [small[0]]
    return _call(body, name="d_ssd_dt", grid=(nc,),
                 in_specs=[col, row] + full + [col, col, row, pl.BlockSpec((None, 1, e01.shape[1]), lambda c: (c, 0, 0)),
                                               pl.BlockSpec(e01.shape, lambda c: (0, 0))],
                 out_specs=[col, row] + [pl.BlockSpec(s.shape, lambda c: (0, 0)) for s in acc],
                 out_shape=[shapes[0], shapes[2]] + [jax.ShapeDtypeStruct(s.shape, F32) for s in acc],
                 args=(dtc, dtr, *small, g_dt, g_csc, g_csr, ddk, e01), sem=("arbitrary",))[0]


def _ssd_specs(t_len, d_ssm, r_heads, reverse):
    rp = r_heads * HEADDIM
    nc = t_len // CHUNK
    per = next(p for p in (SSD_CHUNKS_PER_STEP, 2, 1) if nc % p == 0)
    ns, rows, gs = nc // per, per * CHUNK, SSD_GROUPS_PER_STEP
    cidx = (lambda c: ns - 1 - c) if reverse else (lambda c: c)
    b_off = d_ssm // (N_STATE * gs)
    specs = dict(
        xs=pl.BlockSpec((rows, gs * rp), lambda c, g: (cidx(c), g)),
        b=pl.BlockSpec((rows, gs * N_STATE), lambda c, g: (cidx(c), b_off + g)),
        c=pl.BlockSpec((rows, gs * N_STATE), lambda c, g: (cidx(c), b_off + N_GROUPS // gs + g)),
        grad_bc=pl.BlockSpec((rows, gs * N_STATE), lambda c, g: (cidx(c), g)),
        col=pl.BlockSpec((rows, LANES), lambda c, g: (cidx(c), 0)),
        csr=pl.BlockSpec((gs, r_heads, rows), lambda c, g: (g, 0, cidx(c))),
        e01=pl.BlockSpec((LANES, gs * rp), lambda c, g: (0, g)),
        dskip=pl.BlockSpec((1, gs * rp), lambda c, g: (0, g)),
        hprev=pl.BlockSpec((per, gs, N_STATE, rp), lambda c, g: (cidx(c), g, 0, 0)),
        ddk=pl.BlockSpec((per, 1, gs * rp), lambda c, g: (cidx(c), 0, g)),
    )
    return specs, nc, ns, per, rp


def _ssd_fwd(xbc, dt_c, cs_c, cs_r3, e01, dskip_e, *, d_ssm, r_heads, ride=None):
    t_len = xbc.shape[0]
    sp, nc, ns, per, rp = _ssd_specs(t_len, d_ssm, r_heads, False)

    def body(xs_ref, b_ref, c_ref, dt_ref, csc_ref, csr_ref, e_ref, dk_ref, y_ref, hprev_ref, h_ref):
        c, gp = pl.program_id(0), pl.program_id(1)
        groups = [gp * SSD_GROUPS_PER_STEP + gi for gi in range(SSD_GROUPS_PER_STEP)]

        @pl.when(c == 0)
        def _():
            for g in groups:
                h_ref[g] = jnp.zeros((N_STATE, rp), F32)

        hp = [h_ref[g] for g in groups]
        for s in range(per):
            r = pl.ds(s * CHUNK, CHUNK)
            for gi in range(SSD_GROUPS_PER_STEP):
                cols, bc = pl.ds(gi * rp, rp), pl.ds(gi * N_STATE, N_STATE)
                hprev_ref[s, gi] = hp[gi]
                y, hp[gi] = _ssd_chunk(r_heads, xs_ref[r, cols].astype(F32), b_ref[r, bc].astype(F32),
                                       c_ref[r, bc].astype(F32), dt_ref[r, :], csc_ref[r, :], csr_ref[gi, :, r],
                                       e_ref[:, cols], dk_ref[:, cols], hp[gi])
                y_ref[r, cols] = y
        for gi, g in enumerate(groups):
            h_ref[g] = hp[gi]

    return _call(
        body, name="ssd_fwd", grid=(ns, N_GROUPS // SSD_GROUPS_PER_STEP),
        in_specs=[sp["xs"], sp["b"], sp["c"], sp["col"], sp["col"], sp["csr"], sp["e01"], sp["dskip"]],
        out_specs=[sp["xs"], sp["hprev"]],
        out_shape=[jax.ShapeDtypeStruct((t_len, d_ssm), F32),
                   jax.ShapeDtypeStruct((nc, N_GROUPS, N_STATE, rp), F32)],
        args=(xbc, xbc, xbc, dt_c, cs_c, cs_r3, e01, dskip_e), scratch=[pltpu.VMEM((N_GROUPS, N_STATE, rp), F32)],
        sem=("arbitrary", "arbitrary"), ride=ride)


def _ssd_bwd(xbc, dt_c, cs_c, cs_r3, e01, dskip_e, hprev, dy, *, d_ssm, r_heads, ride=None):
    t_len = xbc.shape[0]
    sp, nc, ns, per, rp = _ssd_specs(t_len, d_ssm, r_heads, True)

    def body(xs_ref, b_ref, c_ref, dt_ref, csc_ref, csr_ref, e_ref, dk_ref, hprev_ref, dy_ref,
             dxs_ref, db_ref, dc_ref, ddt_ref, dcsc_ref, dcsr_ref, ddk_ref, dh_ref):
        c, gp = pl.program_id(0), pl.program_id(1)
        groups = [gp * SSD_GROUPS_PER_STEP + gi for gi in range(SSD_GROUPS_PER_STEP)]

        @pl.when(gp == 0)
        def _():
            ddt_ref[...] = jnp.zeros_like(ddt_ref)
            dcsc_ref[...] = jnp.zeros_like(dcsc_ref)

        @pl.when(c == 0)
        def _():
            for g in groups:
                dh_ref[g] = jnp.zeros((N_STATE, rp), F32)

        dh = [dh_ref[g] for g in groups]
        for s in reversed(range(per)):
            r = pl.ds(s * CHUNK, CHUNK)
            ddt_sum, dcsc_sum = ddt_ref[r, :], dcsc_ref[r, :]
            for gi in range(SSD_GROUPS_PER_STEP):
                cols, bc = pl.ds(gi * rp, rp), pl.ds(gi * N_STATE, N_STATE)
                e01 = e_ref[:, cols]
                fn = lambda xs, bg, cg, dt, csc, csr, dk, hp: _ssd_chunk(r_heads, xs, bg, cg, dt, csc, csr, e01, dk, hp)
                _, vjp = jax.vjp(fn, xs_ref[r, cols].astype(F32), b_ref[r, bc].astype(F32), c_ref[r, bc].astype(F32),
                                 dt_ref[r, :], csc_ref[r, :], csr_ref[gi, :, r], dk_ref[:, cols], hprev_ref[s, gi])
                dxs, dbg, dcg, ddt, dcsc, dcsr, ddk, dh[gi] = vjp((dy_ref[r, cols], dh[gi]))
                dxs_ref[r, cols] = dxs.astype(dxs_ref.dtype)
                db_ref[r, bc] = dbg.astype(db_ref.dtype)
                dc_ref[r, bc] = dcg.astype(dc_ref.dtype)
                ddt_sum, dcsc_sum = ddt_sum + ddt, dcsc_sum + dcsc
                dcsr_ref[gi, :, r] = dcsr
                ddk_ref[s, :, cols] = ddk
            ddt_ref[r, :], dcsc_ref[r, :] = ddt_sum, dcsc_sum
        for gi, g in enumerate(groups):
            dh_ref[g] = dh[gi]

    n_bc = N_GROUPS * N_STATE
    return _call(
        body, name="ssd_bwd", grid=(ns, N_GROUPS // SSD_GROUPS_PER_STEP),
        in_specs=[sp["xs"], sp["b"], sp["c"], sp["col"], sp["col"], sp["csr"], sp["e01"], sp["dskip"], sp["hprev"],
                  sp["xs"]],
        out_specs=[sp["xs"], sp["grad_bc"], sp["grad_bc"], sp["col"], sp["col"], sp["csr"], sp["ddk"]],
        out_shape=[jax.ShapeDtypeStruct((t_len, d_ssm), BF16), jax.ShapeDtypeStruct((t_len, n_bc), BF16),
                   jax.ShapeDtypeStruct((t_len, n_bc), BF16), jax.ShapeDtypeStruct(dt_c.shape, F32),
                   jax.ShapeDtypeStruct(cs_c.shape, F32), jax.ShapeDtypeStruct(cs_r3.shape, F32),
                   jax.ShapeDtypeStruct((nc, 1, d_ssm), F32)],
        args=(xbc, xbc, xbc, dt_c, cs_c, cs_r3, e01, dskip_e, hprev, dy),
        scratch=[pltpu.VMEM((N_GROUPS, N_STATE, rp), F32)], sem=("arbitrary", "arbitrary"), ride=ride)


def _chip_sum(src, sib, *, name):
    rows, cols = src.shape[1:]
    tr = _tile(rows, 256, BF16_ROWS)
    core = lax.axis_index("c").astype(jnp.int32).reshape(1)

    def body(c_ref, a_ref, b_ref, o_ref):
        o_ref[...] = (a_ref[...].astype(F32) + b_ref[...].astype(F32)).astype(o_ref.dtype)

    grid_spec = pltpu.PrefetchScalarGridSpec(
        num_scalar_prefetch=1, grid=(N_CHIPS, rows // tr),
        in_specs=[pl.BlockSpec((None, tr, cols), lambda q, i, c_ref: (2 * q + c_ref[0], i, 0)),
                  pl.BlockSpec((None, tr, cols), lambda q, i, c_ref: (q, i, 0))],
        out_specs=pl.BlockSpec((None, tr, cols), lambda q, i, c_ref: (q, i, 0)))
    return pl.pallas_call(
        body, name=name, grid_spec=grid_spec, out_shape=jax.ShapeDtypeStruct(sib.shape, sib.dtype),
        compiler_params=pltpu.CompilerParams(dimension_semantics=("parallel", "parallel"), vmem_limit_bytes=VMEM_LIMIT),
    )(core, src, sib)


def _adamw(w, g, m, v):
    m = ADAM_B1 * m + (1.0 - ADAM_B1) * g
    v = ADAM_B2 * v + (1.0 - ADAM_B2) * (g * g)
    m_hat = m / (1.0 - ADAM_B1 ** ADAM_STEP)
    v_hat = v / (1.0 - ADAM_B2 ** ADAM_STEP)
    delta = -ADAM_LR * (m_hat / (jnp.sqrt(v_hat) + ADAM_EPS) + ADAM_WD * w)
    return delta, m, v


def _reduce_adamw(parts, w, m, v, *, name):
    n_parts = parts.shape[0]
    rows, cols = w.shape
    tr = _tile(rows, 128, BF16_ROWS)

    def body(p_ref, w_ref, m_ref, v_ref, g_ref, d_ref, mo_ref, vo_ref):
        g = p_ref[0].astype(F32)
        for k in range(1, n_parts):
            g = g + p_ref[k].astype(F32)
        delta, mn, vn = _adamw(w_ref[...], g, m_ref[...], v_ref[...])
        g_ref[...] = g
        d_ref[...] = delta
        mo_ref[...] = mn
        vo_ref[...] = vn

    spec = pl.BlockSpec((tr, cols), lambda i: (i, 0))
    outs, _ = _call(
        body, name=name, grid=(rows // tr,),
        in_specs=[pl.BlockSpec((n_parts, tr, cols), lambda i: (0, i, 0)), spec, spec, spec],
        out_specs=[spec] * 4, out_shape=[jax.ShapeDtypeStruct((rows, cols), F32)] * 4,
        args=(parts, w, m, v), sem=("parallel",))
    return outs


def _move_rows(src, src_row, name, extra=None, extra_row=None):
    rb, n_out, cols = ROW_BLOCK, len(src_row), src.shape[1]
    assert n_out % rb == 0 and src.shape[0] % rb == 0 and src.shape[0] // rb >= 3
    n_blocks, max_b0, seg_cap = n_out // rb, src.shape[0] // rb - 3, 4

    def segments(rows_of, lo):
        segs, r = [], 0
        while r < rb:
            if rows_of[r] < 0:
                r += 1
                continue
            e = r
            while e + 1 < rb and rows_of[e + 1] == rows_of[e] + 1:
                e += 1
            segs.append((r, e + 1, rows_of[r] - r - lo))
            r = e + 1
        assert len(segs) <= seg_cap
        return segs + [(0, 0, 0)] * (seg_cap - len(segs))

    table = []
    for j in range(n_blocks):
        rows_j = list(src_row[j * rb:(j + 1) * rb])
        valid = [v for v in rows_j if v >= 0]
        b0 = min(max((min(valid) // rb) if valid else 0, 0), max_b0)
        assert not valid or max(valid) < (b0 + 3) * rb
        row = [b0] + [v for seg in segments(rows_j, b0 * rb) for v in seg]
        extra_j = [] if extra is None else list(extra_row[j * rb:(j + 1) * rb])
        if extra is not None:
            row += [v for seg in segments(extra_j, 0) for v in seg]
        need_third = bool(valid) and max(valid) >= (b0 + 2) * rb
        third = b0 + 2 if need_third or not table else table[-1][-1]
        row += [int(need_third), int(any(v >= 0 for v in extra_j)), third]
        table.append(row)
    flag_third, flag_extra, col_third = len(table[0]) - 3, len(table[0]) - 2, len(table[0]) - 1
    table = jnp.asarray(table, jnp.int32)

    def select(tbl_ref, j, first, width, col0=0):
        r = lax.broadcasted_iota(jnp.int32, (rb, width), 0)
        c = lax.broadcasted_iota(jnp.int32, (rb, width), 1) + col0
        hit = jnp.zeros((rb, width), jnp.bool_)
        for s in range(seg_cap):
            lo, hi, off = (tbl_ref[j, first + 3 * s + i] for i in range(3))
            hit = hit | ((r >= lo) & (r < hi) & (c == r + off))
        return jnp.where(hit, 1.0, 0.0).astype(BF16)

    def body(tbl_ref, *refs):
        o_ref = refs[-1]
        j = pl.program_id(0)
        sel = select(tbl_ref, j, 1, 2 * rb)
        pick = lambda m, b: lax.dot_general(m, refs[b][...], _NN, preferred_element_type=F32)
        o_ref[...] = (pick(sel[:, :rb], 0) + pick(sel[:, rb:], 1)).astype(o_ref.dtype)

        @pl.when(tbl_ref[j, flag_third] == 1)
        def _():
            o_ref[...] = (o_ref[...].astype(F32) + pick(select(tbl_ref, j, 1, rb, 2 * rb), 2)).astype(o_ref.dtype)

        if extra is not None:
            @pl.when(tbl_ref[j, flag_extra] == 1)
            def _():
                more = lax.dot_general(select(tbl_ref, j, 1 + 3 * seg_cap, extra.shape[0]), refs[3][...], _NN,
                                       preferred_element_type=F32)
                o_ref[...] = (o_ref[...].astype(F32) + more).astype(o_ref.dtype)

    in_specs = [pl.BlockSpec((rb, cols), functools.partial(lambda b, j, tbl: (tbl[j, 0] + b, 0), b)) for b in range(2)]
    in_specs.append(pl.BlockSpec((rb, cols), lambda j, tbl: (tbl[j, col_third], 0)))
    args = [src, src, src]
    if extra is not None:
        in_specs.append(pl.BlockSpec(extra.shape, lambda j, tbl: (0, 0)))
        args.append(extra)
    grid_spec = pltpu.PrefetchScalarGridSpec(num_scalar_prefetch=1, grid=(n_blocks,), in_specs=in_specs,
                                             out_specs=pl.BlockSpec((rb, cols), lambda j, tbl: (j, 0)))
    return pl.pallas_call(
        body, name=name, grid_spec=grid_spec, out_shape=jax.ShapeDtypeStruct((n_out, cols), src.dtype),
        compiler_params=pltpu.CompilerParams(dimension_semantics=("parallel",), vmem_limit_bytes=VMEM_LIMIT),
    )(table, *args)


def _cols_of(g):
    return jnp.transpose(g, (1, 0, 2)).reshape(g.shape[1], -1)


def _pad_to(a, rows, cols):
    return jnp.pad(a, ((0, rows - a.shape[0]), (0, cols - a.shape[1])))


def kernel(x, norm_mix_g, w_in, ssm_conv_w, ssm_conv_b, ssm_dt_bias, ssm_A_log, ssm_D, ssm_norm_g, sc_conv_w, w_out, norm_ffn_g, w_gate, w_up, w_down, norm_final_g, loss_target, m_norm_mix_g, m_w_in, m_ssm_conv_w, m_ssm_conv_b, m_ssm_dt_bias, m_ssm_A_log, m_ssm_D, m_ssm_norm_g, m_sc_conv_w, m_w_out, m_norm_ffn_g, m_w_gate, m_w_up, m_w_down, m_norm_final_g, v_norm_mix_g, v_w_in, v_ssm_conv_w, v_ssm_conv_b, v_ssm_dt_bias, v_ssm_A_log, v_ssm_D, v_ssm_norm_g, v_sc_conv_w, v_w_out, v_norm_ffn_g, v_w_gate, v_w_up, v_w_down, v_norm_final_g):
    t_len, d = x.shape[1], x.shape[2]
    heads = d // HEADDIM
    r_heads = heads // N_GROUPS
    d_xbc = d + 2 * N_GROUPS * N_STATE
    ff_s = w_down.shape[1]
    ff = ff_s * N_DEV
    off_xbc, off_dt = d, d + d_xbc
    off_cb = off_dt + heads
    d_in = off_cb + 3 * d
    in_s = d_in // N_DEV
    in_p = -(-in_s // (2 * BF16_ROWS)) * (2 * BF16_ROWS)
    w_main = 4 * d + d_xbc
    me = 4 * lax.axis_index("x") + 2 * lax.axis_index("y") + lax.axis_index("c")

    x2 = x[0]
    target = loss_target[0]

    tpose = lambda a: jnp.transpose(a[0])
    win_s = _pad_to(tpose(w_in).astype(BF16), in_p, d)
    wg_s, wu_s = tpose(w_gate).astype(BF16), tpose(w_up).astype(BF16)
    wo_s, wd_s = w_out[0].astype(BF16), w_down[0].astype(BF16)
    small_w = jnp.concatenate([_pad_to(ssm_conv_w[0], K_SSM, d_xbc // N_DEV),
                               _pad_to(sc_conv_w[0], K_SC + 1, d_xbc // N_DEV)], axis=0)

    g1, g2, g3 = norm_mix_g, norm_ffn_g, norm_final_g.reshape(1, d)
    gs = ssm_norm_g
    small = [_pad_to(ssm_dt_bias, 1, LANES), ssm_dt_bias.reshape(heads, 1), _pad_to(ssm_A_log, 1, LANES),
             ssm_A_log.reshape(heads, 1)]
    e01 = (lax.broadcasted_iota(jnp.int32, (LANES, d), 1) // HEADDIM
           == lax.broadcasted_iota(jnp.int32, (LANES, d), 0)).astype(BF16)
    dskip_e = jnp.repeat(ssm_D, HEADDIM, axis=1)
    tr = _tile(t_len, 256, 8)
    tr_wide = _tile(t_len, 512, 8)
    tr_ff = _tile(t_len, 128, 8)
    cw = LANES
    slab = lambda col: col // cw

    gin_1, gsm_1 = _gather_chips_relayed([win_s], [small_w], "gather_w_in_chips")
    (n1,), (gin, gsm) = _rows_call(lambda v, g: ((_rms(v, g),), ()), rows=t_len, tr=tr_wide, row_ins=[(x2, d, 0)],
                                   full_ins=[g1], row_outs=[(d, BF16)], acc_outs=[], name="norm_mix",
                                   ride=_gather_sibling([gin_1, gsm_1]))
    in_pieces = []
    for k in range(N_DEV):
        for a, b, dst, shift in ((0, off_dt, 0, 0), (off_dt, off_cb, 1, -off_dt), (off_cb, d_in, 0, -heads)):
            s, e = max(k * in_s, a), min((k + 1) * in_s, b)
            if s < e:
                in_pieces.append((k, s - k * in_s, e - s, dst, s + shift))
    ref_row = lambda t: t if t < off_dt else t + heads
    wtm = _move_rows(gin.reshape(N_DEV * in_p, d),
                     [(ref_row(t) // in_s) * in_p + ref_row(t) % in_s for t in range(w_main)], "place_w_in")
    wtdt = jnp.zeros((LANES, d), BF16)
    for k, r0, n, dst, d0 in in_pieces:
        if dst == 1:
            wtdt = lax.dynamic_update_slice(wtdt, gin[k, r0:r0 + n], (d0, 0))
    cw_ssm = _cols_of(gsm[:, :K_SSM, :])
    cw_sc = _cols_of(gsm[:, K_SSM:K_SSM + K_SC, :d // N_DEV])

    proj, (go_1, gg_1) = _matmul(n1, wtm, tb=True, out_dtype=BF16, name="proj_main",
                                 ride=_gather_chips([wo_s, wg_s]))
    dt_raw, _ = _matmul(n1, wtdt, tb=True, out_dtype=F32, name="proj_dt")
    dt_raw_t = jnp.transpose(dt_raw[:, :heads])
    (xbc,), (go, gg) = _cols_call(_conv_silu_fwd, rows=t_len, cols=d_xbc, cw=cw, col_ins=[(proj, slab(off_xbc))],
                                  par_ins=[(cw_ssm, 0), (ssm_conv_b, 0)], col_outs=[BF16], par_outs=[],
                                  name="ssm_conv", ride=_gather_sibling([go_1, gg_1]))
    dt_c, cs_c, cs_r = _ssd_dt(dt_raw, dt_raw_t, small)
    cs_r3 = cs_r.reshape(N_GROUPS, r_heads, t_len)
    up_cut = int(ff_s * W_UP_GATHER_SPLIT) // BF16_ROWS * BF16_ROWS
    down_cut = int(ff_s * W_DOWN_GATHER_SPLIT) // BF16_ROWS * BF16_ROWS
    half_cut = ff_s // 2 // BF16_ROWS * BF16_ROWS
    (y_ssd, hprev), (gu_1,) = _ssd_fwd(xbc, dt_c, cs_c, cs_r3, e01, dskip_e, d_ssm=d, r_heads=r_heads,
                                       ride=_gather_chips([wu_s], rows=(0, up_cut)))

    def gate_norm(y, z, g):
        z = z.astype(F32)
        return _rms(y * (z * _sigmoid(z)), g)

    (y_mix,), _ = _rows_call(lambda y, z, g: ((gate_norm(y, z, g),), ()), rows=t_len, tr=tr_wide,
                             row_ins=[(y_ssd, d, 0), (proj, d, 0)], full_ins=[gs], row_outs=[(d, BF16, 2 * d)],
                             acc_outs=[], name="ssm_gate_norm")
    wgt, wo = gg.reshape(ff, d), go.reshape(2 * d, d)
    sc0 = slab(d + d_xbc)
    (y_mix,), _ = _cols_call(_shortconv_fwd, rows=t_len, cols=d, cw=cw,
                             col_ins=[(proj, sc0), (proj, sc0 + slab(d)), (proj, sc0 + 2 * slab(d))],
                             par_ins=[(cw_sc, 0)], col_outs=[BF16], par_outs=[], name="shortconv",
                             into=(y_mix, slab(d)))
    h1, (gu_1, gd_1) = _matmul(y_mix, wo, out_dtype=F32, add=x2, name="out_proj", ride=_merge(
        _gather_chips([wu_s], rows=(up_cut, ff_s - up_cut), into=[gu_1]), _gather_chips([wd_s], rows=(0, down_cut))))
    (n2,), _ = _rows_call(lambda v, g: ((_rms(v, g),), ()), rows=t_len, tr=tr_wide, row_ins=[(h1, d, 0)],
                          full_ins=[g2], row_outs=[(d, BF16)], acc_outs=[], name="norm_ffn")
    g_ff, (gd_1, gu) = _matmul(n2, wgt, tb=True, out_dtype=BF16, name="ffn_gate", ride=_merge(
        _gather_chips([wd_s], rows=(down_cut, ff_s - down_cut), into=[gd_1]), _gather_sibling([gu_1])))
    wut = gu.reshape(ff, d)
    (u_ff, a_ff), (gd,) = _matmul(n2, wut, tb=True, name="ffn_up", ride=_gather_sibling([gd_1]),
                                  post=(lambda uv, gv: (uv, gv * _sigmoid(gv) * uv), [g_ff], [BF16, BF16]),
                                  tn_max=MM_TILE_N_POST)
    wd = gd.reshape(ff, d)
    h2, _ = _matmul(a_ff, wd, out_dtype=F32, add=h1, name="ffn_down")

    def head(hv, tv, g):
        def f(hh, gg_):
            e = _rms(hh, gg_) - tv
            return (0.5 / d) * jnp.sum(e * e)
        val, (dh, dg) = jax.value_and_grad(f, argnums=(0, 1))(hv, g)
        return (dh, dh), (jnp.full((1, LANES), val, F32), dg)

    (dh2, dh2_b, loss_acc, dg3), _ = _rows_call(head, rows=t_len, tr=tr, row_ins=[(h2, d, 0), (target, d, 0)],
                                                full_ins=[g3], row_outs=[(d, F32), (d, BF16)],
                                                acc_outs=[(1, LANES), (1, d)], name="loss_head")
    loss = lax.psum(loss_acc[0, 0], ("x", "y", "c"))

    def act_bwd(dav, gv, uv):
        s = _sigmoid(gv)
        return dav * uv * (s * (1.0 + gv * (1.0 - s))), dav * gv * s

    (dg_ff, du_ff), _ = _matmul(dh2_b, wd, tb=True, name="d_ffn_gate_up",
                                post=(act_bwd, [g_ff, u_ff], [BF16, BF16]), tn_max=MM_TILE_N_POST)
    dwd, _ = _matmul(a_ff, dh2_b, ta=True, out_dtype=BF16, name="d_w_down")
    dwd8 = dwd.reshape(N_DEV, ff_s, d)
    dn2, (sib_d,) = _matmul(dg_ff, wgt, out_dtype=F32, name="d_norm_ffn_out_gate", ride=_scatter_sibling([dwd8]))
    chip_d = _chip_sum(dwd8, sib_d, name="chip_sum_w_down")
    dn2, (parts_d,) = _matmul(du_ff, wut, out_dtype=F32, add=dn2, name="d_norm_ffn_out_up",
                              ride=_scatter_chips([chip_d], rows=(0, half_cut)))
    dwg, (parts_d,) = _matmul(dg_ff, n2, ta=True, out_dtype=BF16, name="d_w_gate",
                              ride=_scatter_chips([chip_d], rows=(half_cut, ff_s - half_cut), into=[parts_d]))
    dwu, _ = _matmul(du_ff, n2, ta=True, out_dtype=BF16, name="d_w_up")
    dwg8, dwu8 = dwg.reshape(N_DEV, ff_s, d), dwu.reshape(N_DEV, ff_s, d)

    def norm_bwd(v, dn, dres, g):
        _, vjp = jax.vjp(_rms, v, g)
        dv, dg = vjp(dn)
        return (dv + dres,), (dg,)

    def norm_bwd_2(v, dn, dres, g):
        (dv,), acc = norm_bwd(v, dn, dres, g)
        return (dv, dv), acc

    (dh1, dh1_b, dg2), (sib_g, sib_u) = _rows_call(norm_bwd_2, rows=t_len, tr=tr,
                                                   row_ins=[(h1, d, 0), (dn2, d, 0), (dh2, d, 0)], full_ins=[g2],
                                                   row_outs=[(d, F32), (d, BF16)], acc_outs=[(1, d)], name="d_norm_ffn",
                                                   ride=_scatter_sibling([dwg8, dwu8]))
    chip_g = _chip_sum(dwg8, sib_g, name="chip_sum_w_gate")
    chip_u = _chip_sum(dwu8, sib_u, name="chip_sum_w_up")

    dy_mix, _ = _matmul(dh1_b, wo, tb=True, out_dtype=BF16, name="d_y_mix")
    dwo, _ = _matmul(y_mix, dh1_b, ta=True, out_dtype=BF16, name="d_w_out")
    dwo8 = dwo.reshape(N_DEV, 2 * d // N_DEV, d)
    (dgb, dgc, du, dcw_sc), (sib_o,) = _cols_call(
        _shortconv_bwd, rows=t_len, cols=d, cw=cw,
        col_ins=[(proj, sc0), (proj, sc0 + slab(d)), (proj, sc0 + 2 * slab(d)), (dy_mix, slab(d))],
        par_ins=[(cw_sc, 0)], col_outs=[BF16] * 3, par_outs=[K_SC], name="d_shortconv",
        ride=_scatter_sibling([dwo8]))
    chip_o = _chip_sum(dwo8, sib_o, name="chip_sum_w_out")

    def gate_norm_bwd(y, z, dyo, g):
        _, vjp = jax.vjp(gate_norm, y, z.astype(F32), g)
        dy, dz, dg = vjp(dyo.astype(F32))
        return (dy, dz), (dg,)

    (dy_ssd, dproj, dgs), _ = _rows_call(gate_norm_bwd, rows=t_len, tr=tr,
                                         row_ins=[(y_ssd, d, 0), (proj, d, 0), (dy_mix, d, 0)], full_ins=[gs],
                                         row_outs=[(d, F32), (d, BF16, w_main)], acc_outs=[(1, d)],
                                         name="d_ssm_gate_norm")
    (dxs, dbm, dcm, g_dt, g_csc, g_csr3, ddk), (parts_g, parts_o) = _ssd_bwd(
        xbc, dt_c, cs_c, cs_r3, e01, dskip_e, hprev, dy_ssd, d_ssm=d, r_heads=r_heads,
        ride=_scatter_chips([chip_g, chip_o]))
    ddt_c, ddt_r, dbias_r, dbias_c, dalog_r, dalog_c, ddskip = _ssd_dt(
        dt_raw, dt_raw_t, small, cots=(g_dt, g_csc, g_csr3.reshape(heads, t_len), ddk, e01))
    dcw_parts, dcb_parts, col0 = [], [], 0
    for tag, dpart in (("x", dxs), ("b", dbm), ("c", dcm)):
        (dproj, dcw_p, dcb_p), _ = _cols_call(
            _conv_silu_bwd, rows=t_len, cols=dpart.shape[1], cw=cw,
            col_ins=[(proj, slab(off_xbc + col0)), (dpart, 0)], par_ins=[(cw_ssm, slab(col0)), (ssm_conv_b, slab(col0))],
            col_outs=[BF16], par_outs=[K_SSM, 1], name="d_ssm_conv_" + tag, into=(dproj, slab(off_xbc + col0)))
        dcw_parts.append(dcw_p)
        dcb_parts.append(dcb_p)
        col0 += dpart.shape[1]
    dcw_ssm, dcb_ssm = jnp.concatenate(dcw_parts, axis=1), jnp.concatenate(dcb_parts, axis=1)
    for i, part in enumerate((dgb, dgc, du)):
        dproj = lax.dynamic_update_slice(dproj, part, (0, d + d_xbc + i * d))
    ddt = ddt_c + _pad_to(jnp.transpose(ddt_r), t_len, LANES)
    dwm, (parts_u,) = _matmul(dproj, n1, ta=True, out_dtype=BF16, name="d_w_in_main",
                              ride=_scatter_chips([chip_u]))
    dwdt, _ = _matmul(ddt, n1, ta=True, out_dtype=BF16, name="d_w_in_dt")
    own_ref = [k * in_s + i if i < in_s else -1 for k in range(N_DEV) for i in range(in_p)]
    dwin8 = _move_rows(
        dwm, [-1 if g < 0 or off_dt <= g < off_cb else (g if g < off_dt else g - heads) for g in own_ref],
        "place_d_w_in", extra=dwdt, extra_row=[g - off_dt if off_dt <= g < off_cb else -1 for g in own_ref],
    ).reshape(N_DEV, in_p, d)
    dn1, (sib_in,) = _matmul(ddt, wtdt, out_dtype=F32, name="d_norm_mix_out_dt", ride=_scatter_sibling([dwin8]))
    chip_in = _chip_sum(dwin8, sib_in, name="chip_sum_w_in")
    cut = int(in_p * W_IN_SCATTER_SPLIT) // BF16_ROWS * BF16_ROWS
    dn1, (parts_in,) = _matmul(dproj, wtm, out_dtype=F32, add=dn1, name="d_norm_mix_out",
                               ride=_scatter_chips([chip_in], rows=(0, cut)))
    (dx, dg1), _ = _rows_call(norm_bwd, rows=t_len, tr=tr, row_ins=[(x2, d, 0), (dn1, d, 0), (dh1, d, 0)],
                              full_ins=[g1], row_outs=[(d, F32)], acc_outs=[(1, d)], name="d_norm_mix")

    wide = d_xbc
    rows_small = [dg1, dcb_ssm, dbias_r + _pad_to(dbias_c.reshape(1, heads), 1, LANES),
                  dalog_r + _pad_to(dalog_c.reshape(1, heads), 1, LANES), ddskip, dgs, dg2, dg3]
    packed = jnp.concatenate([_pad_to(r, 1, wide) for r in rows_small]
                             + [dcw_ssm, _pad_to(dcw_sc, K_SC, wide), jnp.zeros((1, wide), F32)], axis=0)
    p_small, parts_in = _comm(_merge(_gather_all([packed]), _scatter_chips([chip_in], rows=(cut, in_p - cut),
                                                                           into=[parts_in])), "gather_small_grads")

    conv_lo = me * (d_xbc // N_DEV)
    sc_lo = me * (d // N_DEV)

    def pack_state(vals):
        (nm, cb, dtb, al, dk, sg, nf, nfin, cws, scs) = vals
        rows = [_pad_to(a.reshape(1, -1), 1, wide) for a in (nm, cb, dtb, al, dk, sg, nf, nfin)]
        cws_full = lax.dynamic_update_slice(jnp.zeros((K_SSM, wide), F32), cws[0], (0, conv_lo))
        scs_full = lax.dynamic_update_slice(jnp.zeros((K_SC, wide), F32), scs[0], (0, sc_lo))
        return jnp.concatenate(rows + [cws_full, scs_full, jnp.zeros((1, wide), F32)], axis=0)

    w_small = pack_state((norm_mix_g, ssm_conv_b, ssm_dt_bias, ssm_A_log, ssm_D, ssm_norm_g, norm_ffn_g, norm_final_g,
                          ssm_conv_w, sc_conv_w))
    m_small = pack_state((m_norm_mix_g, m_ssm_conv_b, m_ssm_dt_bias, m_ssm_A_log, m_ssm_D, m_ssm_norm_g, m_norm_ffn_g,
                          m_norm_final_g, m_ssm_conv_w, m_sc_conv_w))
    v_small = pack_state((v_norm_mix_g, v_ssm_conv_b, v_ssm_dt_bias, v_ssm_A_log, v_ssm_D, v_ssm_norm_g, v_norm_ffn_g,
                          v_norm_final_g, v_ssm_conv_w, v_sc_conv_w))

    tin = lambda a: _pad_to(tpose(a), in_p, d)
    tin_back = lambda a: jnp.transpose(a[:in_s])[None]
    t_back = lambda a: jnp.transpose(a)[None]
    upd = {
        "w_in": [tin_back(o) for o in _reduce_adamw(parts_in, tin(w_in), tin(m_w_in), tin(v_w_in), name="adamw_w_in")],
        "w_out": [o[None] for o in _reduce_adamw(parts_o, w_out[0], m_w_out[0], v_w_out[0], name="adamw_w_out")],
        "w_gate": [t_back(o) for o in _reduce_adamw(parts_g, tpose(w_gate), tpose(m_w_gate), tpose(v_w_gate),
                                                    name="adamw_w_gate")],
        "w_up": [t_back(o) for o in _reduce_adamw(parts_u, tpose(w_up), tpose(m_w_up), tpose(v_w_up),
                                                  name="adamw_w_up")],
        "w_down": [o[None] for o in _reduce_adamw(parts_d, w_down[0], m_w_down[0], v_w_down[0], name="adamw_w_down")],
    }
    small_upd = _reduce_adamw(p_small, w_small, m_small, v_small, name="adamw_small")

    def unpack(packed_out):
        vec = lambda i, n, shape: packed_out[i, :n].reshape(shape)
        return {
            "norm_mix_g": vec(0, d, (1, d)), "ssm_conv_b": vec(1, d_xbc, (1, d_xbc)),
            "ssm_dt_bias": vec(2, heads, (1, heads)), "ssm_A_log": vec(3, heads, (1, heads)),
            "ssm_D": vec(4, heads, (1, heads)), "ssm_norm_g": vec(5, d, (1, d)), "norm_ffn_g": vec(6, d, (1, d)),
            "norm_final_g": vec(7, d, (d,)),
            "ssm_conv_w": lax.dynamic_slice(packed_out[8:8 + K_SSM], (0, conv_lo), (K_SSM, d_xbc // N_DEV))[None],
            "sc_conv_w": lax.dynamic_slice(packed_out[8 + K_SSM:8 + K_SSM + K_SC], (0, sc_lo), (K_SC, d // N_DEV))[None],
        }

    names = ["norm_mix_g", "w_in", "ssm_conv_w", "ssm_conv_b", "ssm_dt_bias", "ssm_A_log", "ssm_D", "ssm_norm_g",
             "sc_conv_w", "w_out", "norm_ffn_g", "w_gate", "w_up", "w_down", "norm_final_g"]
    outs = []
    for kind in range(4):
        small_k = unpack(small_upd[kind])
        for nm in names:
            outs.append(upd[nm][kind] if nm in upd else small_k[nm])
    return (loss, dx[None], *outs)
```

```python
import collections
import functools

import jax
import jax.numpy as jnp
from jax import lax
from jax.experimental import pallas as pl
from jax.experimental.pallas import tpu as pltpu

F32 = jnp.float32
BF16 = jnp.bfloat16

N_DEV = 8
N_CHIPS = 4
HEADDIM = 64
N_GROUPS = 8
N_STATE = 128
CHUNK = 128
K_SSM = 4
K_SC = 3
EPS = 1e-5
LANES = 128
BF16_ROWS = 16
MM_TILE_MN = 1408
MM_TILE_K = 2816
W_IN_SCATTER_SPLIT = 13 / 14
W_UP_GATHER_SPLIT = 0.7
W_DOWN_GATHER_SPLIT = 0.3
MM_TILE_N_POST = 704
SSD_CHUNKS_PER_STEP = 4
SSD_GROUPS_PER_STEP = 4
ROW_BLOCK = 256
V7X_VMEM_BYTES = 64 * 1024 * 1024
VMEM_LIMIT = (V7X_VMEM_BYTES * 3) // 4

ADAM_LR = 0.001
ADAM_B1 = 0.9
ADAM_B2 = 0.999
ADAM_EPS = 1e-08
ADAM_WD = 0.01
ADAM_STEP = 10


def _tile(n, pref, align):
    t = min(pref, n)
    t -= t % align
    while t >= align:
        if n % t == 0:
            return t
        t -= align
    return n


_Ride = collections.namedtuple("_Ride", ["ins", "out_shapes", "aliases", "nsem", "plan", "finish"], defaults=(None,))
_ANY = pl.BlockSpec(memory_space=pl.ANY)


def _coords():
    return lax.axis_index("x"), lax.axis_index("y"), lax.axis_index("c")


def _other_chips(x, y):
    return ((1 - x, y), (x, 1 - y), (1 - x, 1 - y))


def _remote(src, dst, send, recv, k, dev):
    return functools.partial(pltpu.make_async_remote_copy, src_ref=src, dst_ref=dst, send_sem=send.at[k],
                             recv_sem=recv.at[k], device_id=dev, device_id_type=pl.DeviceIdType.MESH)


def _local(src, dst, sem):
    return functools.partial(pltpu.make_async_copy, src, dst, sem)


def _start_all(plan):
    for kind, make in plan:
        if kind != "arrival":
            make().start()


def _wait_all(plan):
    for kind, make in plan:
        if kind == "local":
            make().wait()
        elif kind == "out":
            make().wait_send()
        else:
            make().wait_recv()


def _gather_chips(srcs, rows=None, into=None):
    n = len(srcs)

    def plan(ins, outs, send, recv, base):
        x, y, c = _coords()
        me = 4 * x + 2 * y + c
        cut = (lambda ref: ref) if rows is None else (lambda ref: ref.at[pl.ds(rows[0], rows[1])])
        d = []
        for a, (src, dst) in enumerate(zip(ins[:n], outs)):
            k = base + 4 * a
            d.append(("local", _local(cut(src), cut(dst.at[me]), send.at[k + 3])))
            for j, (px, py) in enumerate(_other_chips(x, y)):
                d.append(("out", _remote(cut(src), cut(dst.at[me]), send, recv, k + j, (px, py, c))))
                d.append(("arrival", _remote(cut(src), cut(dst.at[4 * px + 2 * py + c]), send, recv, k + j,
                                             (px, py, c))))
        return d
    shapes = [jax.ShapeDtypeStruct((N_DEV,) + s.shape, s.dtype) for s in srcs]
    if into is None:
        return _Ride(list(srcs), shapes, {}, 4 * n, plan)
    return _Ride(list(srcs) + list(into), shapes, {n + a: a for a in range(n)}, 4 * n, plan)


def _gather_sibling(bufs):
    def plan(ins, outs, send, recv, base):
        x, y, c = _coords()
        d = []
        for a, buf in enumerate(outs):
            for q in range(N_CHIPS):
                k = base + 4 * a + q
                d.append(("out", _remote(buf.at[2 * q + c], buf.at[2 * q + c], send, recv, k, (x, y, 1 - c))))
                d.append(("arrival", _remote(buf.at[2 * q + c], buf.at[2 * q + 1 - c], send, recv, k, (x, y, 1 - c))))
        return d
    shapes = [jax.ShapeDtypeStruct(b.shape, b.dtype) for b in bufs]
    return _Ride(list(bufs), shapes, {i: i for i in range(len(bufs))}, 4 * len(bufs), plan)


def _scatter_sibling(srcs):
    def plan(ins, outs, send, recv, base):
        x, y, c = _coords()
        d = []
        for a, (src, sib) in enumerate(zip(ins, outs)):
            for q in range(N_CHIPS):
                k = base + 4 * a + q
                d.append(("out", _remote(src.at[2 * q + 1 - c], sib.at[q], send, recv, k, (x, y, 1 - c))))
                d.append(("arrival", _remote(src.at[2 * q + 1 - c], sib.at[q], send, recv, k, (x, y, 1 - c))))
        return d
    shapes = [jax.ShapeDtypeStruct((N_CHIPS,) + s.shape[1:], s.dtype) for s in srcs]
    return _Ride(list(srcs), shapes, {}, 4 * len(srcs), plan)


def _scatter_chips(chips, rows=None, into=None):
    n = len(chips)

    def plan(ins, outs, send, recv, base):
        x, y, c = _coords()
        mine = 2 * x + y
        cut = (lambda ref: ref) if rows is None else (lambda ref: ref.at[pl.ds(rows[0], rows[1])])
        d = []
        for a, (chip, parts) in enumerate(zip(ins[:n], outs)):
            k = base + 4 * a
            d.append(("local", _local(cut(chip.at[mine]), cut(parts.at[mine]), send.at[k + 3])))
            for j, (px, py) in enumerate(_other_chips(x, y)):
                q = 2 * px + py
                d.append(("out", _remote(cut(chip.at[q]), cut(parts.at[mine]), send, recv, k + j, (px, py, c))))
                d.append(("arrival", _remote(cut(chip.at[q]), cut(parts.at[q]), send, recv, k + j, (px, py, c))))
        return d
    shapes = [jax.ShapeDtypeStruct(s.shape, s.dtype) for s in chips]
    if into is None:
        return _Ride(list(chips), shapes, {}, 4 * n, plan)
    return _Ride(list(chips) + list(into), shapes, {n + a: a for a in range(n)}, 4 * n, plan)


def _gather_all(srcs):
    def plan(ins, outs, send, recv, base):
        x, y, c = _coords()
        me = 4 * x + 2 * y + c
        d = []
        for a, (src, dst) in enumerate(zip(ins, outs)):
            k = base + N_DEV * a
            d.append(("local", _local(src, dst.at[me], send.at[k])))
            for j in range(1, N_DEV):
                px = 1 - x if (j >> 2) & 1 else x
                py = 1 - y if (j >> 1) & 1 else y
                pc = 1 - c if j & 1 else c
                d.append(("out", _remote(src, dst.at[me], send, recv, k + j, (px, py, pc))))
                d.append(("arrival", _remote(src, dst.at[4 * px + 2 * py + pc], send, recv, k + j, (px, py, pc))))
        return d
    shapes = [jax.ShapeDtypeStruct((N_DEV,) + s.shape, s.dtype) for s in srcs]
    return _Ride(list(srcs), shapes, {}, N_DEV * len(srcs), plan)


def _merge(*rides):
    ins, outs, aliases, parts, nsem = [], [], {}, [], 0
    for r in rides:
        parts.append((len(ins), len(outs), nsem, r))
        aliases.update({len(ins) + i: len(outs) + j for i, j in r.aliases.items()})
        ins += r.ins
        outs += r.out_shapes
        nsem += r.nsem

    def plan(i, o, send, recv, base):
        d = []
        for i0, o0, s0, r in parts:
            d += r.plan(i[i0:i0 + len(r.ins)], o[o0:o0 + len(r.out_shapes)], send, recv, base + s0)
        return d
    return _Ride(ins, outs, aliases, nsem, plan)


def _comm(ride, name):
    n_in, n_out = len(ride.ins), len(ride.out_shapes)

    def body(*refs):
        plan = ride.plan(refs[:n_in], refs[n_in:n_in + n_out], refs[-2], refs[-1], 0)
        _start_all(plan)
        if ride.finish is None:
            _wait_all(plan)
        else:
            ride.finish(refs[:n_in], refs[n_in:n_in + n_out], refs[-2], refs[-1])

    return pl.pallas_call(
        body, name=name, in_specs=[_ANY] * n_in, out_specs=[_ANY] * n_out, out_shape=ride.out_shapes,
        scratch_shapes=[pltpu.SemaphoreType.DMA((ride.nsem,)), pltpu.SemaphoreType.DMA((ride.nsem,))],
        input_output_aliases=dict(ride.aliases),
        compiler_params=pltpu.CompilerParams(has_side_effects=True),
    )(*ride.ins)


def _gather_relayed(big, small):
    srcs = list(big) + list(small)
    n, per = len(srcs), 10

    def places(ins, outs):
        x, y, c = _coords()
        slot = lambda dev: 4 * dev[0] + 2 * dev[1] + dev[2]
        devs = dict(me=(x, y, c), nx=(1 - x, y, c), ny=(x, 1 - y, c), dg=(1 - x, 1 - y, c), sib=(x, y, 1 - c))
        return devs, slot

    def first(ins, outs, send, recv, base):
        devs, slot = places(ins, outs)
        d = []
        for a, (src, dst) in enumerate(zip(ins, outs)):
            k, mine = base + per * a, dst.at[slot(devs["me"])]
            d.append(("local", _local(src, mine, send.at[k + 9])))
            d.append(("out", _remote(src, mine, send, recv, k, devs["nx"])))
            d.append(("out", _remote(src, mine, send, recv, k + 1, devs["ny"])))
            d.append(("out", _remote(src, mine, send, recv, k + 4, devs["sib"])))
            if a >= len(big):
                d.append(("out", _remote(src, mine, send, recv, k + 2, devs["dg"])))
        return d

    def finish(ins, outs, send, recv):
        devs, slot = places(ins, outs)
        sib_of = lambda dev: (dev[0], dev[1], 1 - dev[2])
        later = []

        def go(copy):
            copy.start()
            later.append(copy)

        for a, (src, dst) in enumerate(zip(ins, outs)):
            k, rows = per * a, src.shape[0]
            relay = a < len(big)
            half = rows // 2
            lo = lambda dev: dst.at[slot(dev)].at[pl.ds(0, half)]
            hi = lambda dev: dst.at[slot(dev)].at[pl.ds(half, rows - half)]
            whole = lambda dev: dst.at[slot(dev)]
            _remote(src, whole(devs["nx"]), send, recv, k, devs["nx"])().wait_recv()
            if relay:
                go(_remote(lo(devs["nx"]), lo(devs["nx"]), send, recv, k + 2, devs["ny"])())
            go(_remote(whole(devs["nx"]), whole(devs["nx"]), send, recv, k + 5, devs["sib"])())
            _remote(src, whole(devs["ny"]), send, recv, k + 1, devs["ny"])().wait_recv()
            if relay:
                go(_remote(hi(devs["ny"]), hi(devs["ny"]), send, recv, k + 3, devs["nx"])())
            go(_remote(whole(devs["ny"]), whole(devs["ny"]), send, recv, k + 6, devs["sib"])())
            if relay:
                _remote(lo(devs["dg"]), lo(devs["dg"]), send, recv, k + 2, devs["ny"])().wait_recv()
                go(_remote(lo(devs["dg"]), lo(devs["dg"]), send, recv, k + 7, devs["sib"])())
                _remote(hi(devs["dg"]), hi(devs["dg"]), send, recv, k + 3, devs["nx"])().wait_recv()
                go(_remote(hi(devs["dg"]), hi(devs["dg"]), send, recv, k + 8, devs["sib"])())
            else:
                _remote(src, whole(devs["dg"]), send, recv, k + 2, devs["dg"])().wait_recv()
                go(_remote(whole(devs["dg"]), whole(devs["dg"]), send, recv, k + 7, devs["sib"])())
        for a, (src, dst) in enumerate(zip(ins, outs)):
            k, rows = per * a, src.shape[0]
            half = rows // 2
            for j, dev in ((4, devs["me"]), (5, devs["nx"]), (6, devs["ny"])):
                theirs = dst.at[slot(sib_of(dev))]
                _remote(theirs, theirs, send, recv, k + j, devs["sib"])().wait_recv()
            far = dst.at[slot(sib_of(devs["dg"]))]
            if a < len(big):
                _remote(far.at[pl.ds(0, half)], far.at[pl.ds(0, half)], send, recv, k + 7, devs["sib"])().wait_recv()
                _remote(far.at[pl.ds(half, rows - half)], far.at[pl.ds(half, rows - half)], send, recv, k + 8,
                        devs["sib"])().wait_recv()
            else:
                _remote(far, far, send, recv, k + 7, devs["sib"])().wait_recv()
        for kind, make in first(ins, outs, send, recv, 0):
            (make().wait if kind == "local" else make().wait_send)()
        for copy in later:
            copy.wait_send()

    shapes = [jax.ShapeDtypeStruct((N_DEV,) + s.shape, s.dtype) for s in srcs]
    return _Ride(srcs, shapes, {}, per * n, first, finish)


def _call(body, *, name, grid, in_specs, out_specs, out_shape, args, sem, scratch=(), ride=None, base=None):
    params = pltpu.CompilerParams(dimension_semantics=sem, vmem_limit_bytes=VMEM_LIMIT)
    own_aliases = {}
    if base is not None:
        inner, n_host = body, len(args)
        body = lambda *refs: inner(*refs[:n_host], *refs[n_host + 1:])
        own_aliases[n_host] = base[1]
        args, in_specs = tuple(args) + (base[0],), list(in_specs) + [_ANY]
    if ride is None:
        res = pl.pallas_call(body, name=name, grid=grid, in_specs=in_specs, out_specs=out_specs,
                             out_shape=out_shape, scratch_shapes=list(scratch), input_output_aliases=own_aliases,
                             compiler_params=params)(*args)
        return list(res), []
    n_in, n_out, n_scr = len(args), len(out_shape), len(scratch)
    r_in, r_out = len(ride.ins), len(ride.out_shapes)

    def hosted(*refs):
        h_in, rin = refs[:n_in], refs[n_in:n_in + r_in]
        o0 = n_in + r_in
        h_out, rout = refs[o0:o0 + n_out], refs[o0 + n_out:o0 + n_out + r_out]
        s0 = o0 + n_out + r_out
        h_scr, send, recv = refs[s0:s0 + n_scr], refs[s0 + n_scr], refs[s0 + n_scr + 1]
        ids = [pl.program_id(i) for i in range(len(grid))]
        first = functools.reduce(lambda p, q: p & q, [i == 0 for i in ids])
        last = functools.reduce(lambda p, q: p & q, [i == n - 1 for i, n in zip(ids, grid)])

        @pl.when(first)
        def _():
            _start_all(ride.plan(rin, rout, send, recv, 0))

        body(*h_in, *h_out, *h_scr)

        @pl.when(last)
        def _():
            if ride.finish is None:
                _wait_all(ride.plan(rin, rout, send, recv, 0))
            else:
                ride.finish(rin, rout, send, recv)

    res = pl.pallas_call(
        hosted, name=name, grid=grid, in_specs=list(in_specs) + [_ANY] * r_in,
        out_specs=list(out_specs) + [_ANY] * r_out, out_shape=list(out_shape) + list(ride.out_shapes),
        scratch_shapes=list(scratch) + [pltpu.SemaphoreType.DMA((ride.nsem,)), pltpu.SemaphoreType.DMA((ride.nsem,))],
        input_output_aliases={**own_aliases, **{n_in + i: n_out + j for i, j in ride.aliases.items()}},
        compiler_params=params,
    )(*args, *ride.ins)
    return list(res[:n_out]), list(res[n_out:])


def _matmul(a, b, *, ta=False, tb=False, out_dtype=BF16, add=None, post=None, name, ride=None, tn_max=MM_TILE_MN):
    m = a.shape[1] if ta else a.shape[0]
    k = a.shape[0] if ta else a.shape[1]
    n = b.shape[0] if tb else b.shape[1]
    assert k == (b.shape[1] if tb else b.shape[0])
    tm, tn, tk = _tile(m, MM_TILE_MN, LANES), _tile(n, tn_max, LANES), _tile(k, MM_TILE_K, LANES)
    nk = k // tk
    dims = (((0 if ta else 1,), (1 if tb else 0,)), ((), ()))
    single = post is None
    if add is not None:
        post = (lambda r, t: (r + t,), [add], [out_dtype])
    elif post is None:
        post = (lambda r: (r,), [], [out_dtype])
    post_fn, extras, out_dtypes = post
    n_ex, n_o = len(extras), len(out_dtypes)

    def body(*refs):
        a_ref, b_ref = refs[:2]
        ex_refs, o_refs = refs[2:2 + n_ex], refs[2 + n_ex:2 + n_ex + n_o]

        def finish(r):
            for o_ref, v in zip(o_refs, post_fn(r, *[e[...].astype(F32) for e in ex_refs])):
                o_ref[...] = v.astype(o_ref.dtype)

        part = lax.dot_general(a_ref[...].astype(BF16), b_ref[...].astype(BF16), dims, preferred_element_type=F32)
        if nk == 1:
            finish(part)
            return
        acc = refs[-1]
        kk = pl.program_id(2)

        @pl.when(kk == 0)
        def _():
            acc[...] = part

        @pl.when((kk > 0) & (kk < nk - 1))
        def _():
            acc[...] += part

        @pl.when(kk == nk - 1)
        def _():
            finish(acc[...] + part)

    a_spec = (pl.BlockSpec((tk, tm), lambda i, j, kk: (kk, i)) if ta
              else pl.BlockSpec((tm, tk), lambda i, j, kk: (i, kk)))
    b_spec = (pl.BlockSpec((tn, tk), lambda i, j, kk: (j, kk)) if tb
              else pl.BlockSpec((tk, tn), lambda i, j, kk: (kk, j)))
    o_spec = pl.BlockSpec((tm, tn), lambda i, j, kk: (i, j))
    outs, rides = _call(
        body, name=name, grid=(m // tm, n // tn, nk),
        in_specs=[a_spec, b_spec] + [o_spec] * n_ex, out_specs=[o_spec] * n_o,
        out_shape=[jax.ShapeDtypeStruct((m, n), dt) for dt in out_dtypes], args=(a, b, *extras),
        scratch=[pltpu.VMEM((tm, tn), F32)] if nk > 1 else [], sem=("parallel", "parallel", "arbitrary"), ride=ride)
    return (outs[0] if single else outs), rides


def _rows_call(fn, *, rows, tr, row_ins, full_ins, row_outs, acc_outs, name, ride=None):
    nr, nf, no, na = len(row_ins), len(full_ins), len(row_outs), len(acc_outs)

    def body(*refs):
        vals = [r[...] for r in refs[:nr + nf]]
        outs, accs = fn(*vals)
        for r, v in zip(refs[nr + nf:nr + nf + no], outs):
            r[...] = v.astype(r.dtype)
        if na:
            @pl.when(pl.program_id(0) == 0)
            def _():
                for r in refs[nr + nf + no:]:
                    r[...] = jnp.zeros_like(r)
            for r, v in zip(refs[nr + nf + no:], accs):
                r[...] += v

    in_specs = [pl.BlockSpec((tr, w), functools.partial(lambda cb, i: (i, cb), cb)) for _, w, cb in row_ins]
    in_specs += [pl.BlockSpec(f.shape, lambda i: (0, 0)) for f in full_ins]
    out_specs = [pl.BlockSpec((tr, o[0]), lambda i: (i, 0)) for o in row_outs]
    out_specs += [pl.BlockSpec(s, lambda i: (0, 0)) for s in acc_outs]
    out_shape = [jax.ShapeDtypeStruct((rows, o[-1] if len(o) == 3 else o[0]), o[1]) for o in row_outs]
    out_shape += [jax.ShapeDtypeStruct(s, F32) for s in acc_outs]
    return _call(body, name=name, grid=(rows // tr,), in_specs=in_specs, out_specs=out_specs, out_shape=out_shape,
                 args=tuple(a for a, _, _ in row_ins) + tuple(full_ins), sem=("arbitrary",), ride=ride)


def _cols_call(fn, *, rows, cols, cw, col_ins, par_ins, col_outs, par_outs, name, ride=None, into=None):
    nc, npar = len(col_ins), len(par_ins)

    def body(*refs):
        vals = [r[...] for r in refs[:nc + npar]]
        outs, pouts = fn(*vals)
        for r, v in zip(refs[nc + npar:], tuple(outs) + tuple(pouts)):
            r[...] = v.astype(r.dtype)

    in_specs = [pl.BlockSpec((rows, cw), functools.partial(lambda off, j: (0, off + j), off)) for _, off in col_ins]
    in_specs += [pl.BlockSpec((p.shape[0], cw), functools.partial(lambda off, j: (0, off + j), off))
                 for p, off in par_ins]
    out_specs = [pl.BlockSpec((rows, cw), lambda j: (0, j)) for _ in col_outs]
    out_specs += [pl.BlockSpec((k, cw), lambda j: (0, j)) for k in par_outs]
    out_shape = [jax.ShapeDtypeStruct((rows, cols), dt) for dt in col_outs]
    out_shape += [jax.ShapeDtypeStruct((k, cols), F32) for k in par_outs]
    if into is not None:
        out_specs[0] = pl.BlockSpec((rows, cw), lambda j: (0, into[1] + j))
        out_shape[0] = jax.ShapeDtypeStruct(into[0].shape, into[0].dtype)
    return _call(body, name=name, grid=(cols // cw,), in_specs=in_specs, out_specs=out_specs, out_shape=out_shape,
                 args=tuple(a for a, _ in col_ins) + tuple(p for p, _ in par_ins), sem=("arbitrary",), ride=ride,
                 base=None if into is None else (into[0], 0))


def _sigmoid(v):
    return 1.0 / (1.0 + jnp.exp(-v))


def _softplus(v):
    return jnp.maximum(v, 0.0) + jnp.log(1.0 + jnp.exp(-jnp.abs(v)))


def _rms(v, g):
    return v * lax.rsqrt(jnp.mean(v * v, axis=-1, keepdims=True) + EPS) * g


def _shift_down(v, s, row):
    return jnp.where(row >= s, pltpu.roll(v, s, 0), 0.0)


def _shift_up(v, s, row):
    n = v.shape[0]
    return jnp.where(row < n - s, pltpu.roll(v, n - s, 0), 0.0)


def _causal_conv(u, w, row):
    k_taps = w.shape[0]
    acc = u * w[k_taps - 1:k_taps, :]
    for k in range(k_taps - 1):
        acc = acc + _shift_down(u, k_taps - 1 - k, row) * w[k:k + 1, :]
    return acc


def _causal_conv_bwd(u, dy, w, row):
    k_taps = w.shape[0]
    tap = lax.broadcasted_iota(jnp.int32, w.shape, 0)
    du = dy * w[k_taps - 1:k_taps, :]
    dw = jnp.where(tap == k_taps - 1, jnp.sum(dy * u, axis=0, keepdims=True), 0.0)
    for k in range(k_taps - 1):
        s = k_taps - 1 - k
        du = du + _shift_up(dy, s, row) * w[k:k + 1, :]
        dw = dw + jnp.where(tap == k, jnp.sum(dy * _shift_down(u, s, row), axis=0, keepdims=True), 0.0)
    return du, dw


def _conv_silu_fwd(u, w, b):
    u = u.astype(F32)
    row = lax.broadcasted_iota(jnp.int32, u.shape, 0)
    pre = _causal_conv(u, w, row) + b
    return (pre * _sigmoid(pre),), ()


def _conv_silu_bwd(u, dy, w, b):
    u = u.astype(F32)
    dy = dy.astype(F32)
    row = lax.broadcasted_iota(jnp.int32, u.shape, 0)
    pre = _causal_conv(u, w, row) + b
    s = _sigmoid(pre)
    dpre = dy * (s * (1.0 + pre * (1.0 - s)))
    du, dw = _causal_conv_bwd(u, dpre, w, row)
    return (du,), (dw, jnp.sum(dpre, axis=0, keepdims=True))


def _shortconv_fwd(gb, gc, u, w):
    gb, gc, u = gb.astype(F32), gc.astype(F32), u.astype(F32)
    row = lax.broadcasted_iota(jnp.int32, u.shape, 0)
    return (gb * _causal_conv(gc * u, w, row),), ()


def _shortconv_bwd(gb, gc, u, dy, w):
    gb, gc, u, dy = gb.astype(F32), gc.astype(F32), u.astype(F32), dy.astype(F32)
    row = lax.broadcasted_iota(jnp.int32, u.shape, 0)
    v = gc * u
    dgb = dy * _causal_conv(v, w, row)
    dv, dw = _causal_conv_bwd(v, dy * gb, w, row)
    return (dgb, dv * u, dv * gc), (dw,)


def _split3(v):
    hi = v.astype(BF16)
    r1 = v - hi.astype(F32)
    mid = r1.astype(BF16)
    lo = (r1 - mid.astype(F32)).astype(BF16)
    return hi, mid, lo


def _exact_dot(v, m01, dims, v_is_lhs):
    def one(p):
        return (lax.dot_general(p, m01, dims, preferred_element_type=F32) if v_is_lhs
                else lax.dot_general(m01, p, dims, preferred_element_type=F32))
    hi, mid, lo = _split3(v)
    return (one(lo) + one(mid)) + one(hi)


_NN = (((1,), (0,)), ((), ()))
_NT = (((1,), (1,)), ((), ()))
_TN = (((0,), (0,)), ((), ()))


@jax.custom_vjp
def _cumsum_rows(tril, v):
    return _exact_dot(v, tril, _NN, False)


def _cumsum_rows_fwd(tril, v):
    return _cumsum_rows(tril, v), tril


def _cumsum_rows_bwd(tril, ct):
    return None, _exact_dot(ct, tril, _TN, False)


_cumsum_rows.defvjp(_cumsum_rows_fwd, _cumsum_rows_bwd)


@jax.custom_vjp
def _cumsum_lanes(tril, v):
    return _exact_dot(v, tril, _NT, True)


def _cumsum_lanes_fwd(tril, v):
    return _cumsum_lanes(tril, v), tril


def _cumsum_lanes_bwd(tril, ct):
    return None, _exact_dot(ct, tril, _NN, True)


_cumsum_lanes.defvjp(_cumsum_lanes_fwd, _cumsum_lanes_bwd)


@jax.custom_vjp
def _expand(e01, v):
    return _exact_dot(v, e01, _NN, True)


def _expand_fwd(e01, v):
    return _expand(e01, v), e01


def _expand_bwd(e01, ct):
    return None, _exact_dot(ct, e01, _NT, True)


_expand.defvjp(_expand_fwd, _expand_bwd)


def _causal_mask(n):
    li = lax.broadcasted_iota(jnp.int32, (n, n), 0)
    si = lax.broadcasted_iota(jnp.int32, (n, n), 1)
    return si <= li


def _dt_prep(dtc, dtr, bias_r, bias_c, alog_r, alog_c):
    dt_c = _softplus(dtc + bias_r)
    dt_r = _softplus(dtr + bias_c)
    tril = jnp.where(_causal_mask(dtc.shape[0]), 1.0, 0.0).astype(BF16)
    cs_c = _cumsum_rows(tril, dt_c * (-jnp.exp(alog_r)))
    cs_r = _cumsum_lanes(tril, dt_r * (-jnp.exp(alog_c)))
    return dt_c, cs_c, cs_r


def _ssd_chunk(r_heads, xs, bg, cg, dt_c, cs_c, cs_rg, e01, dskip_e, hp):
    l_len, rp = xs.shape
    p = rp // r_heads
    causal = _causal_mask(l_len)
    lane_head = lax.broadcasted_iota(jnp.int32, (1, rp), 1) // p
    dt_e = _expand(e01, dt_c)
    cs_e = _expand(e01, cs_c)
    cl_e = cs_e[l_len - 1:l_len, :]
    x = xs * dt_e
    bgb, cgb = bg.astype(BF16), cg.astype(BF16)
    cb = lax.dot_general(cgb, bgb, _NT, preferred_element_type=F32)
    ms, xm = [], []
    for r in range(r_heads):
        seg = cs_e[:, r * p:r * p + 1] - cs_rg[r:r + 1, :]
        decay = jnp.exp(jnp.where(causal, seg, -1e30))
        ms.append((cb * decay).astype(BF16))
        xm.append(jnp.where(lane_head == r, x, 0.0).astype(BF16))
    y_diag = lax.dot_general(jnp.concatenate(ms, axis=1), jnp.concatenate(xm, axis=0), _NN,
                             preferred_element_type=F32)
    y_off = lax.dot_general(cgb, hp.astype(BF16), _NN, preferred_element_type=F32) * jnp.exp(cs_e)
    xd = (x * jnp.exp(cl_e - cs_e)).astype(BF16)
    states = lax.dot_general(bgb, xd, _TN, preferred_element_type=F32)
    h_next = hp * jnp.exp(cl_e) + states
    y = y_diag + y_off + dskip_e * xs
    return y, h_next


def _ssd_dt(dtc, dtr, small, cots=None):
    t_len, heads = dtc.shape[0], dtr.shape[0]
    nc = t_len // CHUNK
    col = pl.BlockSpec((CHUNK, LANES), lambda c: (c, 0))
    row = pl.BlockSpec((heads, CHUNK), lambda c: (0, c))
    full = [pl.BlockSpec(s.shape, lambda c: (0, 0)) for s in small]
    shapes = [jax.ShapeDtypeStruct((t_len, LANES), F32), jax.ShapeDtypeStruct((t_len, LANES), F32),
              jax.ShapeDtypeStruct((heads, t_len), F32)]
    if cots is None:
        def body(dtc_ref, dtr_ref, br, bc, ar, ac, dt_ref, csc_ref, csr_ref):
            dt_ref[...], csc_ref[...], csr_ref[...] = _dt_prep(dtc_ref[...], dtr_ref[...], br[...], bc[...],
                                                                ar[...], ac[...])
        return _call(body, name="ssd_dt", grid=(nc,), in_specs=[col, row] + full, out_specs=[col, col, row],
                     out_shape=shapes, args=(dtc, dtr, *small), sem=("parallel",))[0]

    g_dt, g_csc, g_csr, ddk, e01 = cots

    def body(dtc_ref, dtr_ref, br, bc, ar, ac, g_dt_ref, g_csc_ref, g_csr_ref, ddk_ref, e_ref,
             ddtc_ref, ddtr_ref, *dsmall):
        _, vjp = jax.vjp(_dt_prep, dtc_ref[...], dtr_ref[...], br[...], bc[...], ar[...], ac[...])
        grads = vjp((g_dt_ref[...], g_csc_ref[...], g_csr_ref[...]))
        ddtc_ref[...], ddtr_ref[...] = grads[0], grads[1]
        ddk8 = jnp.broadcast_to(ddk_ref[...], (8, ddk_ref.shape[1]))
        dskip = _exact_dot(ddk8, e_ref[...], _NT, True)[0:1, :]

        @pl.when(pl.program_id(0) == 0)
        def _():
            for r in dsmall:
                r[...] = jnp.zeros_like(r)

        for r, gr in zip(dsmall, tuple(grads[2:]) + (dskip,)):
            r[...] += gr

    acc = list(small) + [small[0]]
    return _call(body, name="d_ssd_dt", grid=(nc,),
                 in_specs=[col, row] + full + [col, col, row, pl.BlockSpec((None, 1, e01.shape[1]), lambda c: (c, 0, 0)),
                                               pl.BlockSpec(e01.shape, lambda c: (0, 0))],
                 out_specs=[col, row] + [pl.BlockSpec(s.shape, lambda c: (0, 0)) for s in acc],
                 out_shape=[shapes[0], shapes[2]] + [jax.ShapeDtypeStruct(s.shape, F32) for s in acc],
                 args=(dtc, dtr, *small, g_dt, g_csc, g_csr, ddk, e01), sem=("arbitrary",))[0]


def _ssd_specs(t_len, d_ssm, r_heads, reverse):
    rp = r_heads * HEADDIM
    nc = t_len // CHUNK
    per = next(p for p in (SSD_CHUNKS_PER_STEP, 2, 1) if nc % p == 0)
    ns, rows, gs = nc // per, per * CHUNK, SSD_GROUPS_PER_STEP
    cidx = (lambda c: ns - 1 - c) if reverse else (lambda c: c)
    b_off = d_ssm // (N_STATE * gs)
    specs = dict(
        xs=pl.BlockSpec((rows, gs * rp), lambda c, g: (cidx(c), g)),
        b=pl.BlockSpec((rows, gs * N_STATE), lambda c, g: (cidx(c), b_off + g)),
        c=pl.BlockSpec((rows, gs * N_STATE), lambda c, g: (cidx(c), b_off + N_GROUPS // gs + g)),
        grad_bc=pl.BlockSpec((rows, gs * N_STATE), lambda c, g: (cidx(c), g)),
        col=pl.BlockSpec((rows, LANES), lambda c, g: (cidx(c), 0)),
        csr=pl.BlockSpec((gs, r_heads, rows), lambda c, g: (g, 0, cidx(c))),
        e01=pl.BlockSpec((LANES, gs * rp), lambda c, g: (0, g)),
        dskip=pl.BlockSpec((1, gs * rp), lambda c, g: (0, g)),
        hprev=pl.BlockSpec((per, gs, N_STATE, rp), lambda c, g: (cidx(c), g, 0, 0)),
        ddk=pl.BlockSpec((per, 1, gs * rp), lambda c, g: (cidx(c), 0, g)),
    )
    return specs, nc, ns, per, rp


def _ssd_fwd(xbc, dt_c, cs_c, cs_r3, e01, dskip_e, *, d_ssm, r_heads, ride=None):
    t_len = xbc.shape[0]
    sp, nc, ns, per, rp = _ssd_specs(t_len, d_ssm, r_heads, False)

    def body(xs_ref, b_ref, c_ref, dt_ref, csc_ref, csr_ref, e_ref, dk_ref, y_ref, hprev_ref, h_ref):
        c, gp = pl.program_id(0), pl.program_id(1)
        groups = [gp * SSD_GROUPS_PER_STEP + gi for gi in range(SSD_GROUPS_PER_STEP)]

        @pl.when(c == 0)
        def _():
            for g in groups:
                h_ref[g] = jnp.zeros((N_STATE, rp), F32)

        hp = [h_ref[g] for g in groups]
        for s in range(per):
            r = pl.ds(s * CHUNK, CHUNK)
            for gi in range(SSD_GROUPS_PER_STEP):
                cols, bc = pl.ds(gi * rp, rp), pl.ds(gi * N_STATE, N_STATE)
                hprev_ref[s, gi] = hp[gi]
                y, hp[gi] = _ssd_chunk(r_heads, xs_ref[r, cols].astype(F32), b_ref[r, bc].astype(F32),
                                       c_ref[r, bc].astype(F32), dt_ref[r, :], csc_ref[r, :], csr_ref[gi, :, r],
                                       e_ref[:, cols], dk_ref[:, cols], hp[gi])
                y_ref[r, cols] = y
        for gi, g in enumerate(groups):
            h_ref[g] = hp[gi]

    return _call(
        body, name="ssd_fwd", grid=(ns, N_GROUPS // SSD_GROUPS_PER_STEP),
        in_specs=[sp["xs"], sp["b"], sp["c"], sp["col"], sp["col"], sp["csr"], sp["e01"], sp["dskip"]],
        out_specs=[sp["xs"], sp["hprev"]],
        out_shape=[jax.ShapeDtypeStruct((t_len, d_ssm), F32),
                   jax.ShapeDtypeStruct((nc, N_GROUPS, N_STATE, rp), F32)],
        args=(xbc, xbc, xbc, dt_c, cs_c, cs_r3, e01, dskip_e), scratch=[pltpu.VMEM((N_GROUPS, N_STATE, rp), F32)],
        sem=("arbitrary", "arbitrary"), ride=ride)


def _ssd_bwd(xbc, dt_c, cs_c, cs_r3, e01, dskip_e, hprev, dy, *, d_ssm, r_heads, ride=None):
    t_len = xbc.shape[0]
    sp, nc, ns, per, rp = _ssd_specs(t_len, d_ssm, r_heads, True)

    def body(xs_ref, b_ref, c_ref, dt_ref, csc_ref, csr_ref, e_ref, dk_ref, hprev_ref, dy_ref,
             dxs_ref, db_ref, dc_ref, ddt_ref, dcsc_ref, dcsr_ref, ddk_ref, dh_ref):
        c, gp = pl.program_id(0), pl.program_id(1)
        groups = [gp * SSD_GROUPS_PER_STEP + gi for gi in range(SSD_GROUPS_PER_STEP)]

        @pl.when(gp == 0)
        def _():
            ddt_ref[...] = jnp.zeros_like(ddt_ref)
            dcsc_ref[...] = jnp.zeros_like(dcsc_ref)

        @pl.when(c == 0)
        def _():
            for g in groups:
                dh_ref[g] = jnp.zeros((N_STATE, rp), F32)

        dh = [dh_ref[g] for g in groups]
        for s in reversed(range(per)):
            r = pl.ds(s * CHUNK, CHUNK)
            ddt_sum, dcsc_sum = ddt_ref[r, :], dcsc_ref[r, :]
            for gi in range(SSD_GROUPS_PER_STEP):
                cols, bc = pl.ds(gi * rp, rp), pl.ds(gi * N_STATE, N_STATE)
                e01 = e_ref[:, cols]
                fn = lambda xs, bg, cg, dt, csc, csr, dk, hp: _ssd_chunk(r_heads, xs, bg, cg, dt, csc, csr, e01, dk, hp)
                _, vjp = jax.vjp(fn, xs_ref[r, cols].astype(F32), b_ref[r, bc].astype(F32), c_ref[r, bc].astype(F32),
                                 dt_ref[r, :], csc_ref[r, :], csr_ref[gi, :, r], dk_ref[:, cols], hprev_ref[s, gi])
                dxs, dbg, dcg, ddt, dcsc, dcsr, ddk, dh[gi] = vjp((dy_ref[r, cols], dh[gi]))
                dxs_ref[r, cols] = dxs.astype(dxs_ref.dtype)
                db_ref[r, bc] = dbg.astype(db_ref.dtype)
                dc_ref[r, bc] = dcg.astype(dc_ref.dtype)
                ddt_sum, dcsc_sum = ddt_sum + ddt, dcsc_sum + dcsc
                dcsr_ref[gi, :, r] = dcsr
                ddk_ref[s, :, cols] = ddk
            ddt_ref[r, :], dcsc_ref[r, :] = ddt_sum, dcsc_sum
        for gi, g in enumerate(groups):
            dh_ref[g] = dh[gi]

    n_bc = N_GROUPS * N_STATE
    return _call(
        body, name="ssd_bwd", grid=(ns, N_GROUPS // SSD_GROUPS_PER_STEP),
        in_specs=[sp["xs"], sp["b"], sp["c"], sp["col"], sp["col"], sp["csr"], sp["e01"], sp["dskip"], sp["hprev"],
                  sp["xs"]],
        out_specs=[sp["xs"], sp["grad_bc"], sp["grad_bc"], sp["col"], sp["col"], sp["csr"], sp["ddk"]],
        out_shape=[jax.ShapeDtypeStruct((t_len, d_ssm), BF16), jax.ShapeDtypeStruct((t_len, n_bc), BF16),
                   jax.ShapeDtypeStruct((t_len, n_bc), BF16), jax.ShapeDtypeStruct(dt_c.shape, F32),
                   jax.ShapeDtypeStruct(cs_c.shape, F32), jax.ShapeDtypeStruct(cs_r3.shape, F32),
                   jax.ShapeDtypeStruct((nc, 1, d_ssm), F32)],
        args=(xbc, xbc, xbc, dt_c, cs_c, cs_r3, e01, dskip_e, hprev, dy),
        scratch=[pltpu.VMEM((N_GROUPS, N_STATE, rp), F32)], sem=("arbitrary", "arbitrary"), ride=ride)


def _chip_sum(src, sib, *, name):
    rows, cols = src.shape[1:]
    tr = _tile(rows, 256, BF16_ROWS)
    core = lax.axis_index("c").astype(jnp.int32).reshape(1)

    def body(c_ref, a_ref, b_ref, o_ref):
        o_ref[...] = (a_ref[...].astype(F32) + b_ref[...].astype(F32)).astype(o_ref.dtype)

    grid_spec = pltpu.PrefetchScalarGridSpec(
        num_scalar_prefetch=1, grid=(N_CHIPS, rows // tr),
        in_specs=[pl.BlockSpec((None, tr, cols), lambda q, i, c_ref: (2 * q + c_ref[0], i, 0)),
                  pl.BlockSpec((None, tr, cols), lambda q, i, c_ref: (q, i, 0))],
        out_specs=pl.BlockSpec((None, tr, cols), lambda q, i, c_ref: (q, i, 0)))
    return pl.pallas_call(
        body, name=name, grid_spec=grid_spec, out_shape=jax.ShapeDtypeStruct(sib.shape, sib.dtype),
        compiler_params=pltpu.CompilerParams(dimension_semantics=("parallel", "parallel"), vmem_limit_bytes=VMEM_LIMIT),
    )(core, src, sib)


def _adamw(w, g, m, v):
    m = ADAM_B1 * m + (1.0 - ADAM_B1) * g
    v = ADAM_B2 * v + (1.0 - ADAM_B2) * (g * g)
    m_hat = m / (1.0 - ADAM_B1 ** ADAM_STEP)
    v_hat = v / (1.0 - ADAM_B2 ** ADAM_STEP)
    delta = -ADAM_LR * (m_hat / (jnp.sqrt(v_hat) + ADAM_EPS) + ADAM_WD * w)
    return delta, m, v


def _reduce_adamw(parts, w, m, v, *, name):
    n_parts = parts.shape[0]
    rows, cols = w.shape
    tr = _tile(rows, 128, BF16_ROWS)

    def body(p_ref, w_ref, m_ref, v_ref, g_ref, d_ref, mo_ref, vo_ref):
        g = p_ref[0].astype(F32)
        for k in range(1, n_parts):
            g = g + p_ref[k].astype(F32)
        delta, mn, vn = _adamw(w_ref[...], g, m_ref[...], v_ref[...])
        g_ref[...] = g
        d_ref[...] = delta
        mo_ref[...] = mn
        vo_ref[...] = vn

    spec = pl.BlockSpec((tr, cols), lambda i: (i, 0))
    outs, _ = _call(
        body, name=name, grid=(rows // tr,),
        in_specs=[pl.BlockSpec((n_parts, tr, cols), lambda i: (0, i, 0)), spec, spec, spec],
        out_specs=[spec] * 4, out_shape=[jax.ShapeDtypeStruct((rows, cols), F32)] * 4,
        args=(parts, w, m, v), sem=("parallel",))
    return outs


def _move_rows(src, src_row, name, extra=None, extra_row=None):
    rb, n_out, cols = ROW_BLOCK, len(src_row), src.shape[1]
    assert n_out % rb == 0 and src.shape[0] % rb == 0 and src.shape[0] // rb >= 3
    n_blocks, max_b0, seg_cap = n_out // rb, src.shape[0] // rb - 3, 4

    def segments(rows_of, lo):
        segs, r = [], 0
        while r < rb:
            if rows_of[r] < 0:
                r += 1
                continue
            e = r
            while e + 1 < rb and rows_of[e + 1] == rows_of[e] + 1:
                e += 1
            segs.append((r, e + 1, rows_of[r] - r - lo))
            r = e + 1
        assert len(segs) <= seg_cap
        return segs + [(0, 0, 0)] * (seg_cap - len(segs))

    table = []
    for j in range(n_blocks):
        rows_j = list(src_row[j * rb:(j + 1) * rb])
        valid = [v for v in rows_j if v >= 0]
        b0 = min(max((min(valid) // rb) if valid else 0, 0), max_b0)
        assert not valid or max(valid) < (b0 + 3) * rb
        row = [b0] + [v for seg in segments(rows_j, b0 * rb) for v in seg]
        extra_j = [] if extra is None else list(extra_row[j * rb:(j + 1) * rb])
        if extra is not None:
            row += [v for seg in segments(extra_j, 0) for v in seg]
        need_third = bool(valid) and max(valid) >= (b0 + 2) * rb
        third = b0 + 2 if need_third or not table else table[-1][-1]
        row += [int(need_third), int(any(v >= 0 for v in extra_j)), third]
        table.append(row)
    flag_third, flag_extra, col_third = len(table[0]) - 3, len(table[0]) - 2, len(table[0]) - 1
    table = jnp.asarray(table, jnp.int32)

    def select(tbl_ref, j, first, width, col0=0):
        r = lax.broadcasted_iota(jnp.int32, (rb, width), 0)
        c = lax.broadcasted_iota(jnp.int32, (rb, width), 1) + col0
        hit = jnp.zeros((rb, width), jnp.bool_)
        for s in range(seg_cap):
            lo, hi, off = (tbl_ref[j, first + 3 * s + i] for i in range(3))
            hit = hit | ((r >= lo) & (r < hi) & (c == r + off))
        return jnp.where(hit, 1.0, 0.0).astype(BF16)

    def body(tbl_ref, *refs):
        o_ref = refs[-1]
        j = pl.program_id(0)
        sel = select(tbl_ref, j, 1, 2 * rb)
        pick = lambda m, b: lax.dot_general(m, refs[b][...], _NN, preferred_element_type=F32)
        o_ref[...] = (pick(sel[:, :rb], 0) + pick(sel[:, rb:], 1)).astype(o_ref.dtype)

        @pl.when(tbl_ref[j, flag_third] == 1)
        def _():
            o_ref[...] = (o_ref[...].astype(F32) + pick(select(tbl_ref, j, 1, rb, 2 * rb), 2)).astype(o_ref.dtype)

        if extra is not None:
            @pl.when(tbl_ref[j, flag_extra] == 1)
            def _():
                more = lax.dot_general(select(tbl_ref, j, 1 + 3 * seg_cap, extra.shape[0]), refs[3][...], _NN,
                                       preferred_element_type=F32)
                o_ref[...] = (o_ref[...].astype(F32) + more).astype(o_ref.dtype)

    in_specs = [pl.BlockSpec((rb, cols), functools.partial(lambda b, j, tbl: (tbl[j, 0] + b, 0), b)) for b in range(2)]
    in_specs.append(pl.BlockSpec((rb, cols), lambda j, tbl: (tbl[j, col_third], 0)))
    args = [src, src, src]
    if extra is not None:
        in_specs.append(pl.BlockSpec(extra.shape, lambda j, tbl: (0, 0)))
        args.append(extra)
    grid_spec = pltpu.PrefetchScalarGridSpec(num_scalar_prefetch=1, grid=(n_blocks,), in_specs=in_specs,
                                             out_specs=pl.BlockSpec((rb, cols), lambda j, tbl: (j, 0)))
    return pl.pallas_call(
        body, name=name, grid_spec=grid_spec, out_shape=jax.ShapeDtypeStruct((n_out, cols), src.dtype),
        compiler_params=pltpu.CompilerParams(dimension_semantics=("parallel",), vmem_limit_bytes=VMEM_LIMIT),
    )(table, *args)


def _cols_of(g):
    return jnp.transpose(g, (1, 0, 2)).reshape(g.shape[1], -1)


def _pad_to(a, rows, cols):
    return jnp.pad(a, ((0, rows - a.shape[0]), (0, cols - a.shape[1])))


def kernel(x, norm_mix_g, w_in, ssm_conv_w, ssm_conv_b, ssm_dt_bias, ssm_A_log, ssm_D, ssm_norm_g, sc_conv_w, w_out, norm_ffn_g, w_gate, w_up, w_down, norm_final_g, loss_target, m_norm_mix_g, m_w_in, m_ssm_conv_w, m_ssm_conv_b, m_ssm_dt_bias, m_ssm_A_log, m_ssm_D, m_ssm_norm_g, m_sc_conv_w, m_w_out, m_norm_ffn_g, m_w_gate, m_w_up, m_w_down, m_norm_final_g, v_norm_mix_g, v_w_in, v_ssm_conv_w, v_ssm_conv_b, v_ssm_dt_bias, v_ssm_A_log, v_ssm_D, v_ssm_norm_g, v_sc_conv_w, v_w_out, v_norm_ffn_g, v_w_gate, v_w_up, v_w_down, v_norm_final_g):
    t_len, d = x.shape[1], x.shape[2]
    heads = d // HEADDIM
    r_heads = heads // N_GROUPS
    d_xbc = d + 2 * N_GROUPS * N_STATE
    ff_s = w_down.shape[1]
    ff = ff_s * N_DEV
    off_xbc, off_dt = d, d + d_xbc
    off_cb = off_dt + heads
    d_in = off_cb + 3 * d
    in_s = d_in // N_DEV
    in_p = -(-in_s // (2 * BF16_ROWS)) * (2 * BF16_ROWS)
    w_main = 4 * d + d_xbc
    me = 4 * lax.axis_index("x") + 2 * lax.axis_index("y") + lax.axis_index("c")

    x2 = x[0]
    target = loss_target[0]

    tpose = lambda a: jnp.transpose(a[0])
    win_s = _pad_to(tpose(w_in).astype(BF16), in_p, d)
    wg_s, wu_s = tpose(w_gate).astype(BF16), tpose(w_up).astype(BF16)
    wo_s, wd_s = w_out[0].astype(BF16), w_down[0].astype(BF16)
    small_w = jnp.concatenate([_pad_to(ssm_conv_w[0], K_SSM, d_xbc // N_DEV),
                               _pad_to(sc_conv_w[0], K_SC + 1, d_xbc // N_DEV)], axis=0)

    g1, g2, g3 = norm_mix_g, norm_ffn_g, norm_final_g.reshape(1, d)
    gs = ssm_norm_g
    small = [_pad_to(ssm_dt_bias, 1, LANES), ssm_dt_bias.reshape(heads, 1), _pad_to(ssm_A_log, 1, LANES),
             ssm_A_log.reshape(heads, 1)]
    e01 = (lax.broadcasted_iota(jnp.int32, (LANES, d), 1) // HEADDIM
           == lax.broadcasted_iota(jnp.int32, (LANES, d), 0)).astype(BF16)
    dskip_e = jnp.repeat(ssm_D, HEADDIM, axis=1)
    tr = _tile(t_len, 256, 8)
    tr_wide = _tile(t_len, 512, 8)
    tr_ff = _tile(t_len, 128, 8)
    cw = LANES
    slab = lambda col: col // cw

    (n1,), (gin, gsm) = _rows_call(lambda v, g: ((_rms(v, g),), ()), rows=t_len, tr=tr_wide, row_ins=[(x2, d, 0)],
                                   full_ins=[g1], row_outs=[(d, BF16)], acc_outs=[], name="norm_mix",
                                   ride=_gather_relayed([win_s], [small_w]))
    in_pieces = []
    for k in range(N_DEV):
        for a, b, dst, shift in ((0, off_dt, 0, 0), (off_dt, off_cb, 1, -off_dt), (off_cb, d_in, 0, -heads)):
            s, e = max(k * in_s, a), min((k + 1) * in_s, b)
            if s < e:
                in_pieces.append((k, s - k * in_s, e - s, dst, s + shift))
    ref_row = lambda t: t if t < off_dt else t + heads
    wtm = _move_rows(gin.reshape(N_DEV * in_p, d),
                     [(ref_row(t) // in_s) * in_p + ref_row(t) % in_s for t in range(w_main)], "place_w_in")
    wtdt = jnp.zeros((LANES, d), BF16)
    for k, r0, n, dst, d0 in in_pieces:
        if dst == 1:
            wtdt = lax.dynamic_update_slice(wtdt, gin[k, r0:r0 + n], (d0, 0))
    cw_ssm = _cols_of(gsm[:, :K_SSM, :])
    cw_sc = _cols_of(gsm[:, K_SSM:K_SSM + K_SC, :d // N_DEV])

    proj, (go_1, gg_1) = _matmul(n1, wtm, tb=True, out_dtype=BF16, name="proj_main",
                                 ride=_gather_chips([wo_s, wg_s]))
    dt_raw, _ = _matmul(n1, wtdt, tb=True, out_dtype=F32, name="proj_dt")
    dt_raw_t = jnp.transpose(dt_raw[:, :heads])
    (xbc,), (go, gg) = _cols_call(_conv_silu_fwd, rows=t_len, cols=d_xbc, cw=cw, col_ins=[(proj, slab(off_xbc))],
                                  par_ins=[(cw_ssm, 0), (ssm_conv_b, 0)], col_outs=[BF16], par_outs=[],
                                  name="ssm_conv", ride=_gather_sibling([go_1, gg_1]))
    dt_c, cs_c, cs_r = _ssd_dt(dt_raw, dt_raw_t, small)
    cs_r3 = cs_r.reshape(N_GROUPS, r_heads, t_len)
    up_cut = int(ff_s * W_UP_GATHER_SPLIT) // BF16_ROWS * BF16_ROWS
    down_cut = int(ff_s * W_DOWN_GATHER_SPLIT) // BF16_ROWS * BF16_ROWS
    half_cut = ff_s // 2 // BF16_ROWS * BF16_ROWS
    (y_ssd, hprev), (gu_1,) = _ssd_fwd(xbc, dt_c, cs_c, cs_r3, e01, dskip_e, d_ssm=d, r_heads=r_heads,
                                       ride=_gather_chips([wu_s], rows=(0, up_cut)))

    def gate_norm(y, z, g):
        z = z.astype(F32)
        return _rms(y * (z * _sigmoid(z)), g)

    (y_mix,), _ = _rows_call(lambda y, z, g: ((gate_norm(y, z, g),), ()), rows=t_len, tr=tr_wide,
                             row_ins=[(y_ssd, d, 0), (proj, d, 0)], full_ins=[gs], row_outs=[(d, BF16, 2 * d)],
                             acc_outs=[], name="ssm_gate_norm")
    wgt, wo = gg.reshape(ff, d), go.reshape(2 * d, d)
    sc0 = slab(d + d_xbc)
    (y_mix,), _ = _cols_call(_shortconv_fwd, rows=t_len, cols=d, cw=cw,
                             col_ins=[(proj, sc0), (proj, sc0 + slab(d)), (proj, sc0 + 2 * slab(d))],
                             par_ins=[(cw_sc, 0)], col_outs=[BF16], par_outs=[], name="shortconv",
                             into=(y_mix, slab(d)))
    h1, (gu_1, gd_1) = _matmul(y_mix, wo, out_dtype=F32, add=x2, name="out_proj", ride=_merge(
        _gather_chips([wu_s], rows=(up_cut, ff_s - up_cut), into=[gu_1]), _gather_chips([wd_s], rows=(0, down_cut))))
    (n2,), _ = _rows_call(lambda v, g: ((_rms(v, g),), ()), rows=t_len, tr=tr_wide, row_ins=[(h1, d, 0)],
                          full_ins=[g2], row_outs=[(d, BF16)], acc_outs=[], name="norm_ffn")
    g_ff, (gd_1, gu) = _matmul(n2, wgt, tb=True, out_dtype=BF16, name="ffn_gate", ride=_merge(
        _gather_chips([wd_s], rows=(down_cut, ff_s - down_cut), into=[gd_1]), _gather_sibling([gu_1])))
    wut = gu.reshape(ff, d)
    (u_ff, a_ff), (gd,) = _matmul(n2, wut, tb=True, name="ffn_up", ride=_gather_sibling([gd_1]),
                                  post=(lambda uv, gv: (uv, gv * _sigmoid(gv) * uv), [g_ff], [BF16, BF16]),
                                  tn_max=MM_TILE_N_POST)
    wd = gd.reshape(ff, d)
    h2, _ = _matmul(a_ff, wd, out_dtype=F32, add=h1, name="ffn_down")

    def head(hv, tv, g):
        def f(hh, gg_):
            e = _rms(hh, gg_) - tv
            return (0.5 / d) * jnp.sum(e * e)
        val, (dh, dg) = jax.value_and_grad(f, argnums=(0, 1))(hv, g)
        return (dh, dh), (jnp.full((1, LANES), val, F32), dg)

    (dh2, dh2_b, loss_acc, dg3), _ = _rows_call(head, rows=t_len, tr=tr, row_ins=[(h2, d, 0), (target, d, 0)],
                                                full_ins=[g3], row_outs=[(d, F32), (d, BF16)],
                                                acc_outs=[(1, LANES), (1, d)], name="loss_head")
    loss = lax.psum(loss_acc[0, 0], ("x", "y", "c"))

    def act_bwd(dav, gv, uv):
        s = _sigmoid(gv)
        return dav * uv * (s * (1.0 + gv * (1.0 - s))), dav * gv * s

    (dg_ff, du_ff), _ = _matmul(dh2_b, wd, tb=True, name="d_ffn_gate_up",
                                post=(act_bwd, [g_ff, u_ff], [BF16, BF16]), tn_max=MM_TILE_N_POST)
    dwd, _ = _matmul(a_ff, dh2_b, ta=True, out_dtype=BF16, name="d_w_down")
    dwd8 = dwd.reshape(N_DEV, ff_s, d)
    dn2, (sib_d,) = _matmul(dg_ff, wgt, out_dtype=F32, name="d_norm_ffn_out_gate", ride=_scatter_sibling([dwd8]))
    chip_d = _chip_sum(dwd8, sib_d, name="chip_sum_w_down")
    dn2, (parts_d,) = _matmul(du_ff, wut, out_dtype=F32, add=dn2, name="d_norm_ffn_out_up",
                              ride=_scatter_chips([chip_d], rows=(0, half_cut)))
    dwg, (parts_d,) = _matmul(dg_ff, n2, ta=True, out_dtype=BF16, name="d_w_gate",
                              ride=_scatter_chips([chip_d], rows=(half_cut, ff_s - half_cut), into=[parts_d]))
    dwu, _ = _matmul(du_ff, n2, ta=True, out_dtype=BF16, name="d_w_up")
    dwg8, dwu8 = dwg.reshape(N_DEV, ff_s, d), dwu.reshape(N_DEV, ff_s, d)

    def norm_bwd(v, dn, dres, g):
        _, vjp = jax.vjp(_rms, v, g)
        dv, dg = vjp(dn)
        return (dv + dres,), (dg,)

    def norm_bwd_2(v, dn, dres, g):
        (dv,), acc = norm_bwd(v, dn, dres, g)
        return (dv, dv), acc

    (dh1, dh1_b, dg2), (sib_g, sib_u) = _rows_call(norm_bwd_2, rows=t_len, tr=tr,
                                                   row_ins=[(h1, d, 0), (dn2, d, 0), (dh2, d, 0)], full_ins=[g2],
                                                   row_outs=[(d, F32), (d, BF16)], acc_outs=[(1, d)], name="d_norm_ffn",
                                                   ride=_scatter_sibling([dwg8, dwu8]))
    chip_g = _chip_sum(dwg8, sib_g, name="chip_sum_w_gate")
    chip_u = _chip_sum(dwu8, sib_u, name="chip_sum_w_up")

    dy_mix, _ = _matmul(dh1_b, wo, tb=True, out_dtype=BF16, name="d_y_mix")
    dwo, _ = _matmul(y_mix, dh1_b, ta=True, out_dtype=BF16, name="d_w_out")
    dwo8 = dwo.reshape(N_DEV, 2 * d // N_DEV, d)
    (dgb, dgc, du, dcw_sc), (sib_o,) = _cols_call(
        _shortconv_bwd, rows=t_len, cols=d, cw=cw,
        col_ins=[(proj, sc0), (proj, sc0 + slab(d)), (proj, sc0 + 2 * slab(d)), (dy_mix, slab(d))],
        par_ins=[(cw_sc, 0)], col_outs=[BF16] * 3, par_outs=[K_SC], name="d_shortconv",
        ride=_scatter_sibling([dwo8]))
    chip_o = _chip_sum(dwo8, sib_o, name="chip_sum_w_out")

    def gate_norm_bwd(y, z, dyo, g):
        _, vjp = jax.vjp(gate_norm, y, z.astype(F32), g)
        dy, dz, dg = vjp(dyo.astype(F32))
        return (dy, dz), (dg,)

    (dy_ssd, dproj, dgs), _ = _rows_call(gate_norm_bwd, rows=t_len, tr=tr,
                                         row_ins=[(y_ssd, d, 0), (proj, d, 0), (dy_mix, d, 0)], full_ins=[gs],
                                         row_outs=[(d, F32), (d, BF16, w_main)], acc_outs=[(1, d)],
                                         name="d_ssm_gate_norm")
    (dxs, dbm, dcm, g_dt, g_csc, g_csr3, ddk), (parts_g, parts_o) = _ssd_bwd(
        xbc, dt_c, cs_c, cs_r3, e01, dskip_e, hprev, dy_ssd, d_ssm=d, r_heads=r_heads,
        ride=_scatter_chips([chip_g, chip_o]))
    ddt_c, ddt_r, dbias_r, dbias_c, dalog_r, dalog_c, ddskip = _ssd_dt(
        dt_raw, dt_raw_t, small, cots=(g_dt, g_csc, g_csr3.reshape(heads, t_len), ddk, e01))
    dcw_parts, dcb_parts, col0 = [], [], 0
    for tag, dpart in (("x", dxs), ("b", dbm), ("c", dcm)):
        (dproj, dcw_p, dcb_p), _ = _cols_call(
            _conv_silu_bwd, rows=t_len, cols=dpart.shape[1], cw=cw,
            col_ins=[(proj, slab(off_xbc + col0)), (dpart, 0)], par_ins=[(cw_ssm, slab(col0)), (ssm_conv_b, slab(col0))],
            col_outs=[BF16], par_outs=[K_SSM, 1], name="d_ssm_conv_" + tag, into=(dproj, slab(off_xbc + col0)))
        dcw_parts.append(dcw_p)
        dcb_parts.append(dcb_p)
        col0 += dpart.shape[1]
    dcw_ssm, dcb_ssm = jnp.concatenate(dcw_parts, axis=1), jnp.concatenate(dcb_parts, axis=1)
    for i, part in enumerate((dgb, dgc, du)):
        dproj = lax.dynamic_update_slice(dproj, part, (0, d + d_xbc + i * d))
    ddt = ddt_c + _pad_to(jnp.transpose(ddt_r), t_len, LANES)
    dwm, (parts_u,) = _matmul(dproj, n1, ta=True, out_dtype=BF16, name="d_w_in_main",
                              ride=_scatter_chips([chip_u]))
    dwdt, _ = _matmul(ddt, n1, ta=True, out_dtype=BF16, name="d_w_in_dt")
    own_ref = [k * in_s + i if i < in_s else -1 for k in range(N_DEV) for i in range(in_p)]
    dwin8 = _move_rows(
        dwm, [-1 if g < 0 or off_dt <= g < off_cb else (g if g < off_dt else g - heads) for g in own_ref],
        "place_d_w_in", extra=dwdt, extra_row=[g - off_dt if off_dt <= g < off_cb else -1 for g in own_ref],
    ).reshape(N_DEV, in_p, d)
    dn1, (sib_in,) = _matmul(ddt, wtdt, out_dtype=F32, name="d_norm_mix_out_dt", ride=_scatter_sibling([dwin8]))
    chip_in = _chip_sum(dwin8, sib_in, name="chip_sum_w_in")
    cut = int(in_p * W_IN_SCATTER_SPLIT) // BF16_ROWS * BF16_ROWS
    dn1, (parts_in,) = _matmul(dproj, wtm, out_dtype=F32, add=dn1, name="d_norm_mix_out",
                               ride=_scatter_chips([chip_in], rows=(0, cut)))
    (dx, dg1), _ = _rows_call(norm_bwd, rows=t_len, tr=tr, row_ins=[(x2, d, 0), (dn1, d, 0), (dh1, d, 0)],
                              full_ins=[g1], row_outs=[(d, F32)], acc_outs=[(1, d)], name="d_norm_mix")

    wide = d_xbc
    rows_small = [dg1, dcb_ssm, dbias_r + _pad_to(dbias_c.reshape(1, heads), 1, LANES),
                  dalog_r + _pad_to(dalog_c.reshape(1, heads), 1, LANES), ddskip, dgs, dg2, dg3]
    packed = jnp.concatenate([_pad_to(r, 1, wide) for r in rows_small]
                             + [dcw_ssm, _pad_to(dcw_sc, K_SC, wide), jnp.zeros((1, wide), F32)], axis=0)
    p_small, parts_in = _comm(_merge(_gather_all([packed]), _scatter_chips([chip_in], rows=(cut, in_p - cut),
                                                                           into=[parts_in])), "gather_small_grads")

    conv_lo = me * (d_xbc // N_DEV)
    sc_lo = me * (d // N_DEV)

    def pack_state(vals):
        (nm, cb, dtb, al, dk, sg, nf, nfin, cws, scs) = vals
        rows = [_pad_to(a.reshape(1, -1), 1, wide) for a in (nm, cb, dtb, al, dk, sg, nf, nfin)]
        cws_full = lax.dynamic_update_slice(jnp.zeros((K_SSM, wide), F32), cws[0], (0, conv_lo))
        scs_full = lax.dynamic_update_slice(jnp.zeros((K_SC, wide), F32), scs[0], (0, sc_lo))
        return jnp.concatenate(rows + [cws_full, scs_full, jnp.zeros((1, wide), F32)], axis=0)

    w_small = pack_state((norm_mix_g, ssm_conv_b, ssm_dt_bias, ssm_A_log, ssm_D, ssm_norm_g, norm_ffn_g, norm_final_g,
                          ssm_conv_w, sc_conv_w))
    m_small = pack_state((m_norm_mix_g, m_ssm_conv_b, m_ssm_dt_bias, m_ssm_A_log, m_ssm_D, m_ssm_norm_g, m_norm_ffn_g,
                          m_norm_final_g, m_ssm_conv_w, m_sc_conv_w))
    v_small = pack_state((v_norm_mix_g, v_ssm_conv_b, v_ssm_dt_bias, v_ssm_A_log, v_ssm_D, v_ssm_norm_g, v_norm_ffn_g,
                          v_norm_final_g, v_ssm_conv_w, v_sc_conv_w))

    tin = lambda a: _pad_to(tpose(a), in_p, d)
    tin_back = lambda a: jnp.transpose(a[:in_s])[None]
    t_back = lambda a: jnp.transpose(a)[None]
    upd = {
        "w_in": [tin_back(o) for o in _reduce_adamw(parts_in, tin(w_in), tin(m_w_in), tin(v_w_in), name="adamw_w_in")],
        "w_out": [o[None] for o in _reduce_adamw(parts_o, w_out[0], m_w_out[0], v_w_out[0], name="adamw_w_out")],
        "w_gate": [t_back(o) for o in _reduce_adamw(parts_g, tpose(w_gate), tpose(m_w_gate), tpose(v_w_gate),
                                                    name="adamw_w_gate")],
        "w_up": [t_back(o) for o in _reduce_adamw(parts_u, tpose(w_up), tpose(m_w_up), tpose(v_w_up),
                                                  name="adamw_w_up")],
        "w_down": [o[None] for o in _reduce_adamw(parts_d, w_down[0], m_w_down[0], v_w_down[0], name="adamw_w_down")],
    }
    small_upd = _reduce_adamw(p_small, w_small, m_small, v_small, name="adamw_small")

    def unpack(packed_out):
        vec = lambda i, n, shape: packed_out[i, :n].reshape(shape)
        return {
            "norm_mix_g": vec(0, d, (1, d)), "ssm_conv_b": vec(1, d_xbc, (1, d_xbc)),
            "ssm_dt_bias": vec(2, heads, (1, heads)), "ssm_A_log": vec(3, heads, (1, heads)),
            "ssm_D": vec(4, heads, (1, heads)), "ssm_norm_g": vec(5, d, (1, d)), "norm_ffn_g": vec(6, d, (1, d)),
            "norm_final_g": vec(7, d, (d,)),
            "ssm_conv_w": lax.dynamic_slice(packed_out[8:8 + K_SSM], (0, conv_lo), (K_SSM, d_xbc // N_DEV))[None],
            "sc_conv_w": lax.dynamic_slice(packed_out[8 + K_SSM:8 + K_SSM + K_SC], (0, sc_lo), (K_SC, d // N_DEV))[None],
        }

    names = ["norm_mix_g", "w_in", "ssm_conv_w", "ssm_conv_b", "ssm_dt_bias", "ssm_A_log", "ssm_D", "ssm_norm_g",
             "sc_conv_w", "w_out", "norm_ffn_g", "w_gate", "w_up", "w_down", "norm_final_g"]
    outs = []
    for kind in range(4):
        small_k = unpack(small_upd[kind])
        for nm in names:
            outs.append(upd[nm][kind] if nm in upd else small_k[nm])
    return (loss, dx[None], *outs)
```

```python
import collections
import functools

import jax
import jax.numpy as jnp
from jax import lax
from jax.experimental import pallas as pl
from jax.experimental.pallas import tpu as pltpu

F32 = jnp.float32
BF16 = jnp.bfloat16

N_DEV = 8
N_CHIPS = 4
HEADDIM = 64
N_GROUPS = 8
N_STATE = 128
CHUNK = 128
K_SSM = 4
K_SC = 3
EPS = 1e-5
LANES = 128
BF16_ROWS = 16
MM_TILE_MN = 1408
MM_TILE_K = 2816
RELAY_AT = 0.45
W_UP_GATHER_SPLIT = 0.7
W_DOWN_GATHER_SPLIT = 0.3
MM_TILE_N_POST = 704
SSD_CHUNKS_PER_STEP = 4
SSD_GROUPS_PER_STEP = 4
ROW_BLOCK = 256
V7X_VMEM_BYTES = 64 * 1024 * 1024
VMEM_LIMIT = (V7X_VMEM_BYTES * 3) // 4

ADAM_LR = 0.001
ADAM_B1 = 0.9
ADAM_B2 = 0.999
ADAM_EPS = 1e-08
ADAM_WD = 0.01
ADAM_STEP = 10


def _tile(n, pref, align):
    t = min(pref, n)
    t -= t % align
    while t >= align:
        if n % t == 0:
            return t
        t -= align
    return n


_Ride = collections.namedtuple("_Ride", ["ins", "out_shapes", "aliases", "nsem", "plan", "finish", "mid"],
                               defaults=(None, None))
_ANY = pl.BlockSpec(memory_space=pl.ANY)


def _coords():
    return lax.axis_index("x"), lax.axis_index("y"), lax.axis_index("c")


def _other_chips(x, y):
    return ((1 - x, y), (x, 1 - y), (1 - x, 1 - y))


def _remote(src, dst, send, recv, k, dev):
    return functools.partial(pltpu.make_async_remote_copy, src_ref=src, dst_ref=dst, send_sem=send.at[k],
                             recv_sem=recv.at[k], device_id=dev, device_id_type=pl.DeviceIdType.MESH)


def _local(src, dst, sem):
    return functools.partial(pltpu.make_async_copy, src, dst, sem)


def _start_all(plan):
    for kind, make in plan:
        if kind != "arrival":
            make().start()


def _wait_all(plan):
    for kind, make in plan:
        if kind == "local":
            make().wait()
        elif kind == "out":
            make().wait_send()
        else:
            make().wait_recv()


def _gather_chips(srcs, rows=None, into=None):
    n = len(srcs)

    def plan(ins, outs, send, recv, base):
        x, y, c = _coords()
        me = 4 * x + 2 * y + c
        cut = (lambda ref: ref) if rows is None else (lambda ref: ref.at[pl.ds(rows[0], rows[1])])
        d = []
        for a, (src, dst) in enumerate(zip(ins[:n], outs)):
            k = base + 4 * a
            d.append(("local", _local(cut(src), cut(dst.at[me]), send.at[k + 3])))
            for j, (px, py) in enumerate(_other_chips(x, y)):
                d.append(("out", _remote(cut(src), cut(dst.at[me]), send, recv, k + j, (px, py, c))))
                d.append(("arrival", _remote(cut(src), cut(dst.at[4 * px + 2 * py + c]), send, recv, k + j,
                                             (px, py, c))))
        return d
    shapes = [jax.ShapeDtypeStruct((N_DEV,) + s.shape, s.dtype) for s in srcs]
    if into is None:
        return _Ride(list(srcs), shapes, {}, 4 * n, plan)
    return _Ride(list(srcs) + list(into), shapes, {n + a: a for a in range(n)}, 4 * n, plan)


def _gather_sibling(bufs):
    def plan(ins, outs, send, recv, base):
        x, y, c = _coords()
        d = []
        for a, buf in enumerate(outs):
            for q in range(N_CHIPS):
                k = base + 4 * a + q
                d.append(("out", _remote(buf.at[2 * q + c], buf.at[2 * q + c], send, recv, k, (x, y, 1 - c))))
                d.append(("arrival", _remote(buf.at[2 * q + c], buf.at[2 * q + 1 - c], send, recv, k, (x, y, 1 - c))))
        return d
    shapes = [jax.ShapeDtypeStruct(b.shape, b.dtype) for b in bufs]
    return _Ride(list(bufs), shapes, {i: i for i in range(len(bufs))}, 4 * len(bufs), plan)


def _scatter_sibling(srcs):
    def plan(ins, outs, send, recv, base):
        x, y, c = _coords()
        d = []
        for a, (src, sib) in enumerate(zip(ins, outs)):
            for q in range(N_CHIPS):
                k = base + 4 * a + q
                d.append(("out", _remote(src.at[2 * q + 1 - c], sib.at[q], send, recv, k, (x, y, 1 - c))))
                d.append(("arrival", _remote(src.at[2 * q + 1 - c], sib.at[q], send, recv, k, (x, y, 1 - c))))
        return d
    shapes = [jax.ShapeDtypeStruct((N_CHIPS,) + s.shape[1:], s.dtype) for s in srcs]
    return _Ride(list(srcs), shapes, {}, 4 * len(srcs), plan)


def _scatter_chips(chips, rows=None, into=None):
    n = len(chips)

    def plan(ins, outs, send, recv, base):
        x, y, c = _coords()
        mine = 2 * x + y
        cut = (lambda ref: ref) if rows is None else (lambda ref: ref.at[pl.ds(rows[0], rows[1])])
        d = []
        for a, (chip, parts) in enumerate(zip(ins[:n], outs)):
            k = base + 4 * a
            d.append(("local", _local(cut(chip.at[mine]), cut(parts.at[mine]), send.at[k + 3])))
            for j, (px, py) in enumerate(_other_chips(x, y)):
                q = 2 * px + py
                d.append(("out", _remote(cut(chip.at[q]), cut(parts.at[mine]), send, recv, k + j, (px, py, c))))
                d.append(("arrival", _remote(cut(chip.at[q]), cut(parts.at[q]), send, recv, k + j, (px, py, c))))
        return d
    shapes = [jax.ShapeDtypeStruct(s.shape, s.dtype) for s in chips]
    if into is None:
        return _Ride(list(chips), shapes, {}, 4 * n, plan)
    return _Ride(list(chips) + list(into), shapes, {n + a: a for a in range(n)}, 4 * n, plan)


def _scatter_chips_relayed(chips):
    n, per = len(chips), 8

    def places():
        x, y, c = _coords()
        return dict(nx=(1 - x, y, c), ny=(x, 1 - y, c), dg=(1 - x, 1 - y, c)), 2 * x + y, lambda dev: 2 * dev[0] + dev[1]

    def halves(ref, rows):
        return ref.at[pl.ds(0, rows // 2)], ref.at[pl.ds(rows // 2, rows - rows // 2)]

    def first(ins, outs, send, recv, base):
        devs, mine, chip_of = places()
        d = []
        for a, chip in enumerate(ins):
            parts, stage, k, rows = outs[2 * a], outs[2 * a + 1], base + per * a, chip.shape[1]
            lo, hi = halves(chip.at[chip_of(devs["dg"])], rows)
            d.append(("out", _remote(lo, stage.at[0], send, recv, k + 2, devs["nx"])))
            d.append(("out", _remote(hi, stage.at[1], send, recv, k + 3, devs["ny"])))
            d.append(("local", _local(chip.at[mine], parts.at[mine], send.at[k + 7])))
            d.append(("out", _remote(chip.at[chip_of(devs["nx"])], parts.at[mine], send, recv, k, devs["nx"])))
            d.append(("out", _remote(chip.at[chip_of(devs["ny"])], parts.at[mine], send, recv, k + 1, devs["ny"])))
        return d

    def passed_on(outs, a, rows, send, recv, devs, chip_of):
        parts, stage, k = outs[2 * a], outs[2 * a + 1], per * a
        return (_remote(stage.at[0], halves(parts.at[chip_of(devs["nx"])], rows)[0], send, recv, k + 5, devs["ny"]),
                _remote(stage.at[1], halves(parts.at[chip_of(devs["ny"])], rows)[1], send, recv, k + 6, devs["nx"]))

    def mid(ins, outs, send, recv):
        devs, mine, chip_of = places()
        for a, chip in enumerate(ins):
            stage, k, rows = outs[2 * a + 1], per * a, chip.shape[1]
            from_x, from_y = passed_on(outs, a, rows, send, recv, devs, chip_of)
            _remote(stage.at[0], stage.at[0], send, recv, k + 2, devs["nx"])().wait_recv()
            from_x().start()
            _remote(stage.at[1], stage.at[1], send, recv, k + 3, devs["ny"])().wait_recv()
            from_y().start()

    def finish(ins, outs, send, recv):
        devs, mine, chip_of = places()
        for a, chip in enumerate(ins):
            parts, stage, k, rows = outs[2 * a], outs[2 * a + 1], per * a, chip.shape[1]
            _remote(chip.at[0], parts.at[chip_of(devs["nx"])], send, recv, k, devs["nx"])().wait_recv()
            _remote(chip.at[0], parts.at[chip_of(devs["ny"])], send, recv, k + 1, devs["ny"])().wait_recv()
            lo, hi = halves(parts.at[chip_of(devs["dg"])], rows)
            _remote(stage.at[0], lo, send, recv, k + 5, devs["ny"])().wait_recv()
            _remote(stage.at[1], hi, send, recv, k + 6, devs["nx"])().wait_recv()
        for kind, make in first(ins, outs, send, recv, 0):
            (make().wait if kind == "local" else make().wait_send)()
        for a, chip in enumerate(ins):
            for make in passed_on(outs, a, chip.shape[1], send, recv, devs, chip_of):
                make().wait_send()

    shapes = []
    for s in chips:
        shapes += [jax.ShapeDtypeStruct(s.shape, s.dtype),
                   jax.ShapeDtypeStruct((2, s.shape[1] - s.shape[1] // 2) + s.shape[2:], s.dtype)]
    return _Ride(list(chips), shapes, {}, per * n, first, finish, mid)


def _gather_all(srcs):
    def plan(ins, outs, send, recv, base):
        x, y, c = _coords()
        me = 4 * x + 2 * y + c
        d = []
        for a, (src, dst) in enumerate(zip(ins, outs)):
            k = base + N_DEV * a
            d.append(("local", _local(src, dst.at[me], send.at[k])))
            for j in range(1, N_DEV):
                px = 1 - x if (j >> 2) & 1 else x
                py = 1 - y if (j >> 1) & 1 else y
                pc = 1 - c if j & 1 else c
                d.append(("out", _remote(src, dst.at[me], send, recv, k + j, (px, py, pc))))
                d.append(("arrival", _remote(src, dst.at[4 * px + 2 * py + pc], send, recv, k + j, (px, py, pc))))
        return d
    shapes = [jax.ShapeDtypeStruct((N_DEV,) + s.shape, s.dtype) for s in srcs]
    return _Ride(list(srcs), shapes, {}, N_DEV * len(srcs), plan)


def _merge(*rides):
    ins, outs, aliases, parts, nsem = [], [], {}, [], 0
    for r in rides:
        parts.append((len(ins), len(outs), nsem, r))
        aliases.update({len(ins) + i: len(outs) + j for i, j in r.aliases.items()})
        ins += r.ins
        outs += r.out_shapes
        nsem += r.nsem

    def plan(i, o, send, recv, base):
        d = []
        for i0, o0, s0, r in parts:
            d += r.plan(i[i0:i0 + len(r.ins)], o[o0:o0 + len(r.out_shapes)], send, recv, base + s0)
        return d
    return _Ride(ins, outs, aliases, nsem, plan)


def _comm(ride, name):
    n_in, n_out = len(ride.ins), len(ride.out_shapes)

    def body(*refs):
        plan = ride.plan(refs[:n_in], refs[n_in:n_in + n_out], refs[-2], refs[-1], 0)
        _start_all(plan)
        if ride.mid is not None:
            ride.mid(refs[:n_in], refs[n_in:n_in + n_out], refs[-2], refs[-1])
        if ride.finish is None:
            _wait_all(plan)
        else:
            ride.finish(refs[:n_in], refs[n_in:n_in + n_out], refs[-2], refs[-1])

    return pl.pallas_call(
        body, name=name, in_specs=[_ANY] * n_in, out_specs=[_ANY] * n_out, out_shape=ride.out_shapes,
        scratch_shapes=[pltpu.SemaphoreType.DMA((ride.nsem,)), pltpu.SemaphoreType.DMA((ride.nsem,))],
        input_output_aliases=dict(ride.aliases),
        compiler_params=pltpu.CompilerParams(has_side_effects=True),
    )(*ride.ins)


def _gather_relayed(big, small):
    srcs = list(big) + list(small)
    n, per = len(srcs), 10

    def places(ins, outs):
        x, y, c = _coords()
        slot = lambda dev: 4 * dev[0] + 2 * dev[1] + dev[2]
        devs = dict(me=(x, y, c), nx=(1 - x, y, c), ny=(x, 1 - y, c), dg=(1 - x, 1 - y, c), sib=(x, y, 1 - c))
        return devs, slot

    def first(ins, outs, send, recv, base):
        devs, slot = places(ins, outs)
        d = []
        for a, (src, dst) in enumerate(zip(ins, outs)):
            k, mine = base + per * a, dst.at[slot(devs["me"])]
            d.append(("local", _local(src, mine, send.at[k + 9])))
            d.append(("out", _remote(src, mine, send, recv, k, devs["nx"])))
            d.append(("out", _remote(src, mine, send, recv, k + 1, devs["ny"])))
            d.append(("out", _remote(src, mine, send, recv, k + 4, devs["sib"])))
            if a >= len(big):
                d.append(("out", _remote(src, mine, send, recv, k + 2, devs["dg"])))
        return d

    def finish(ins, outs, send, recv):
        devs, slot = places(ins, outs)
        sib_of = lambda dev: (dev[0], dev[1], 1 - dev[2])
        later = []

        def go(copy):
            copy.start()
            later.append(copy)

        for a, (src, dst) in enumerate(zip(ins, outs)):
            k, rows = per * a, src.shape[0]
            relay = a < len(big)
            half = rows // 2
            lo = lambda dev: dst.at[slot(dev)].at[pl.ds(0, half)]
            hi = lambda dev: dst.at[slot(dev)].at[pl.ds(half, rows - half)]
            whole = lambda dev: dst.at[slot(dev)]
            _remote(src, whole(devs["nx"]), send, recv, k, devs["nx"])().wait_recv()
            if relay:
                go(_remote(lo(devs["nx"]), lo(devs["nx"]), send, recv, k + 2, devs["ny"])())
            go(_remote(whole(devs["nx"]), whole(devs["nx"]), send, recv, k + 5, devs["sib"])())
            _remote(src, whole(devs["ny"]), send, recv, k + 1, devs["ny"])().wait_recv()
            if relay:
                go(_remote(hi(devs["ny"]), hi(devs["ny"]), send, recv, k + 3, devs["nx"])())
            go(_remote(whole(devs["ny"]), whole(devs["ny"]), send, recv, k + 6, devs["sib"])())
            if relay:
                _remote(lo(devs["dg"]), lo(devs["dg"]), send, recv, k + 2, devs["ny"])().wait_recv()
                go(_remote(lo(devs["dg"]), lo(devs["dg"]), send, recv, k + 7, devs["sib"])())
                _remote(hi(devs["dg"]), hi(devs["dg"]), send, recv, k + 3, devs["nx"])().wait_recv()
                go(_remote(hi(devs["dg"]), hi(devs["dg"]), send, recv, k + 8, devs["sib"])())
            else:
                _remote(src, whole(devs["dg"]), send, recv, k + 2, devs["dg"])().wait_recv()
                go(_remote(whole(devs["dg"]), whole(devs["dg"]), send, recv, k + 7, devs["sib"])())
        for a, (src, dst) in enumerate(zip(ins, outs)):
            k, rows = per * a, src.shape[0]
            half = rows // 2
            for j, dev in ((4, devs["me"]), (5, devs["nx"]), (6, devs["ny"])):
                theirs = dst.at[slot(sib_of(dev))]
                _remote(theirs, theirs, send, recv, k + j, devs["sib"])().wait_recv()
            far = dst.at[slot(sib_of(devs["dg"]))]
            if a < len(big):
                _remote(far.at[pl.ds(0, half)], far.at[pl.ds(0, half)], send, recv, k + 7, devs["sib"])().wait_recv()
                _remote(far.at[pl.ds(half, rows - half)], far.at[pl.ds(half, rows - half)], send, recv, k + 8,
                        devs["sib"])().wait_recv()
            else:
                _remote(far, far, send, recv, k + 7, devs["sib"])().wait_recv()
        for kind, make in first(ins, outs, send, recv, 0):
            (make().wait if kind == "local" else make().wait_send)()
        for copy in later:
            copy.wait_send()

    shapes = [jax.ShapeDtypeStruct((N_DEV,) + s.shape, s.dtype) for s in srcs]
    return _Ride(srcs, shapes, {}, per * n, first, finish)


def _call(body, *, name, grid, in_specs, out_specs, out_shape, args, sem, scratch=(), ride=None, base=None):
    params = pltpu.CompilerParams(dimension_semantics=sem, vmem_limit_bytes=VMEM_LIMIT)
    own_aliases = {}
    if base is not None:
        inner, n_host = body, len(args)
        body = lambda *refs: inner(*refs[:n_host], *refs[n_host + 1:])
        own_aliases[n_host] = base[1]
        args, in_specs = tuple(args) + (base[0],), list(in_specs) + [_ANY]
    if ride is None:
        res = pl.pallas_call(body, name=name, grid=grid, in_specs=in_specs, out_specs=out_specs,
                             out_shape=out_shape, scratch_shapes=list(scratch), input_output_aliases=own_aliases,
                             compiler_params=params)(*args)
        return list(res), []
    n_in, n_out, n_scr = len(args), len(out_shape), len(scratch)
    r_in, r_out = len(ride.ins), len(ride.out_shapes)

    def hosted(*refs):
        h_in, rin = refs[:n_in], refs[n_in:n_in + r_in]
        o0 = n_in + r_in
        h_out, rout = refs[o0:o0 + n_out], refs[o0 + n_out:o0 + n_out + r_out]
        s0 = o0 + n_out + r_out
        h_scr, send, recv = refs[s0:s0 + n_scr], refs[s0 + n_scr], refs[s0 + n_scr + 1]
        ids = [pl.program_id(i) for i in range(len(grid))]
        first = functools.reduce(lambda p, q: p & q, [i == 0 for i in ids])
        last = functools.reduce(lambda p, q: p & q, [i == n - 1 for i, n in zip(ids, grid)])

        @pl.when(first)
        def _():
            _start_all(ride.plan(rin, rout, send, recv, 0))

        if ride.mid is not None:
            steps = functools.reduce(lambda p, q: p * q, grid)
            linear = functools.reduce(lambda p, iq: p * iq[1] + iq[0], zip(ids, grid), 0)

            @pl.when(linear == int(steps * RELAY_AT))
            def _():
                ride.mid(rin, rout, send, recv)

        body(*h_in, *h_out, *h_scr)

        @pl.when(last)
        def _():
            if ride.finish is None:
                _wait_all(ride.plan(rin, rout, send, recv, 0))
            else:
                ride.finish(rin, rout, send, recv)

    res = pl.pallas_call(
        hosted, name=name, grid=grid, in_specs=list(in_specs) + [_ANY] * r_in,
        out_specs=list(out_specs) + [_ANY] * r_out, out_shape=list(out_shape) + list(ride.out_shapes),
        scratch_shapes=list(scratch) + [pltpu.SemaphoreType.DMA((ride.nsem,)), pltpu.SemaphoreType.DMA((ride.nsem,))],
        input_output_aliases={**own_aliases, **{n_in + i: n_out + j for i, j in ride.aliases.items()}},
        compiler_params=params,
    )(*args, *ride.ins)
    return list(res[:n_out]), list(res[n_out:])


def _matmul(a, b, *, ta=False, tb=False, out_dtype=BF16, add=None, post=None, name, ride=None, tn_max=MM_TILE_MN):
    m = a.shape[1] if ta else a.shape[0]
    k = a.shape[0] if ta else a.shape[1]
    n = b.shape[0] if tb else b.shape[1]
    assert k == (b.shape[1] if tb else b.shape[0])
    tm, tn, tk = _tile(m, MM_TILE_MN, LANES), _tile(n, tn_max, LANES), _tile(k, MM_TILE_K, LANES)
    nk = k // tk
    dims = (((0 if ta else 1,), (1 if tb else 0,)), ((), ()))
    single = post is None
    if add is not None:
        post = (lambda r, t: (r + t,), [add], [out_dtype])
    elif post is None:
        post = (lambda r: (r,), [], [out_dtype])
    post_fn, extras, out_dtypes = post
    n_ex, n_o = len(extras), len(out_dtypes)

    def body(*refs):
        a_ref, b_ref = refs[:2]
        ex_refs, o_refs = refs[2:2 + n_ex], refs[2 + n_ex:2 + n_ex + n_o]

        def finish(r):
            for o_ref, v in zip(o_refs, post_fn(r, *[e[...].astype(F32) for e in ex_refs])):
                o_ref[...] = v.astype(o_ref.dtype)

        part = lax.dot_general(a_ref[...].astype(BF16), b_ref[...].astype(BF16), dims, preferred_element_type=F32)
        if nk == 1:
            finish(part)
            return
        acc = refs[-1]
        kk = pl.program_id(2)

        @pl.when(kk == 0)
        def _():
            acc[...] = part

        @pl.when((kk > 0) & (kk < nk - 1))
        def _():
            acc[...] += part

        @pl.when(kk == nk - 1)
        def _():
            finish(acc[...] + part)

    a_spec = (pl.BlockSpec((tk, tm), lambda i, j, kk: (kk, i)) if ta
              else pl.BlockSpec((tm, tk), lambda i, j, kk: (i, kk)))
    b_spec = (pl.BlockSpec((tn, tk), lambda i, j, kk: (j, kk)) if tb
              else pl.BlockSpec((tk, tn), lambda i, j, kk: (kk, j)))
    o_spec = pl.BlockSpec((tm, tn), lambda i, j, kk: (i, j))
    outs, rides = _call(
        body, name=name, grid=(m // tm, n // tn, nk),
        in_specs=[a_spec, b_spec] + [o_spec] * n_ex, out_specs=[o_spec] * n_o,
        out_shape=[jax.ShapeDtypeStruct((m, n), dt) for dt in out_dtypes], args=(a, b, *extras),
        scratch=[pltpu.VMEM((tm, tn), F32)] if nk > 1 else [], sem=("parallel", "parallel", "arbitrary"), ride=ride)
    return (outs[0] if single else outs), rides


def _rows_call(fn, *, rows, tr, row_ins, full_ins, row_outs, acc_outs, name, ride=None):
    nr, nf, no, na = len(row_ins), len(full_ins), len(row_outs), len(acc_outs)

    def body(*refs):
        vals = [r[...] for r in refs[:nr + nf]]
        outs, accs = fn(*vals)
        for r, v in zip(refs[nr + nf:nr + nf + no], outs):
            r[...] = v.astype(r.dtype)
        if na:
            @pl.when(pl.program_id(0) == 0)
            def _():
                for r in refs[nr + nf + no:]:
                    r[...] = jnp.zeros_like(r)
            for r, v in zip(refs[nr + nf + no:], accs):
                r[...] += v

    in_specs = [pl.BlockSpec((tr, w), functools.partial(lambda cb, i: (i, cb), cb)) for _, w, cb in row_ins]
    in_specs += [pl.BlockSpec(f.shape, lambda i: (0, 0)) for f in full_ins]
    out_specs = [pl.BlockSpec((tr, o[0]), lambda i: (i, 0)) for o in row_outs]
    out_specs += [pl.BlockSpec(s, lambda i: (0, 0)) for s in acc_outs]
    out_shape = [jax.ShapeDtypeStruct((rows, o[-1] if len(o) == 3 else o[0]), o[1]) for o in row_outs]
    out_shape += [jax.ShapeDtypeStruct(s, F32) for s in acc_outs]
    return _call(body, name=name, grid=(rows // tr,), in_specs=in_specs, out_specs=out_specs, out_shape=out_shape,
                 args=tuple(a for a, _, _ in row_ins) + tuple(full_ins), sem=("arbitrary",), ride=ride)


def _cols_call(fn, *, rows, cols, cw, col_ins, par_ins, col_outs, par_outs, name, ride=None, into=None):
    nc, npar = len(col_ins), len(par_ins)

    def body(*refs):
        vals = [r[...] for r in refs[:nc + npar]]
        outs, pouts = fn(*vals)
        for r, v in zip(refs[nc + npar:], tuple(outs) + tuple(pouts)):
            r[...] = v.astype(r.dtype)

    in_specs = [pl.BlockSpec((rows, cw), functools.partial(lambda off, j: (0, off + j), off)) for _, off in col_ins]
    in_specs += [pl.BlockSpec((p.shape[0], cw), functools.partial(lambda off, j: (0, off + j), off))
                 for p, off in par_ins]
    out_specs = [pl.BlockSpec((rows, cw), lambda j: (0, j)) for _ in col_outs]
    out_specs += [pl.BlockSpec((k, cw), lambda j: (0, j)) for k in par_outs]
    out_shape = [jax.ShapeDtypeStruct((rows, cols), dt) for dt in col_outs]
    out_shape += [jax.ShapeDtypeStruct((k, cols), F32) for k in par_outs]
    if into is not None:
        out_specs[0] = pl.BlockSpec((rows, cw), lambda j: (0, into[1] + j))
        out_shape[0] = jax.ShapeDtypeStruct(into[0].shape, into[0].dtype)
    return _call(body, name=name, grid=(cols // cw,), in_specs=in_specs, out_specs=out_specs, out_shape=out_shape,
                 args=tuple(a for a, _ in col_ins) + tuple(p for p, _ in par_ins), sem=("arbitrary",), ride=ride,
                 base=None if into is None else (into[0], 0))


def _sigmoid(v):
    return 1.0 / (1.0 + jnp.exp(-v))


def _softplus(v):
    return jnp.maximum(v, 0.0) + jnp.log(1.0 + jnp.exp(-jnp.abs(v)))


def _rms(v, g):
    return v * lax.rsqrt(jnp.mean(v * v, axis=-1, keepdims=True) + EPS) * g


def _shift_down(v, s, row):
    return jnp.where(row >= s, pltpu.roll(v, s, 0), 0.0)


def _shift_up(v, s, row):
    n = v.shape[0]
    return jnp.where(row < n - s, pltpu.roll(v, n - s, 0), 0.0)


def _causal_conv(u, w, row):
    k_taps = w.shape[0]
    acc = u * w[k_taps - 1:k_taps, :]
    for k in range(k_taps - 1):
        acc = acc + _shift_down(u, k_taps - 1 - k, row) * w[k:k + 1, :]
    return acc


def _causal_conv_bwd(u, dy, w, row):
    k_taps = w.shape[0]
    tap = lax.broadcasted_iota(jnp.int32, w.shape, 0)
    du = dy * w[k_taps - 1:k_taps, :]
    dw = jnp.where(tap == k_taps - 1, jnp.sum(dy * u, axis=0, keepdims=True), 0.0)
    for k in range(k_taps - 1):
        s = k_taps - 1 - k
        du = du + _shift_up(dy, s, row) * w[k:k + 1, :]
        dw = dw + jnp.where(tap == k, jnp.sum(dy * _shift_down(u, s, row), axis=0, keepdims=True), 0.0)
    return du, dw


def _conv_silu_fwd(u, w, b):
    u = u.astype(F32)
    row = lax.broadcasted_iota(jnp.int32, u.shape, 0)
    pre = _causal_conv(u, w, row) + b
    return (pre * _sigmoid(pre),), ()


def _conv_silu_bwd(u, dy, w, b):
    u = u.astype(F32)
    dy = dy.astype(F32)
    row = lax.broadcasted_iota(jnp.int32, u.shape, 0)
    pre = _causal_conv(u, w, row) + b
    s = _sigmoid(pre)
    dpre = dy * (s * (1.0 + pre * (1.0 - s)))
    du, dw = _causal_conv_bwd(u, dpre, w, row)
    return (du,), (dw, jnp.sum(dpre, axis=0, keepdims=True))


def _shortconv_fwd(gb, gc, u, w):
    gb, gc, u = gb.astype(F32), gc.astype(F32), u.astype(F32)
    row = lax.broadcasted_iota(jnp.int32, u.shape, 0)
    return (gb * _causal_conv(gc * u, w, row),), ()


def _shortconv_bwd(gb, gc, u, dy, w):
    gb, gc, u, dy = gb.astype(F32), gc.astype(F32), u.astype(F32), dy.astype(F32)
    row = lax.broadcasted_iota(jnp.int32, u.shape, 0)
    v = gc * u
    dgb = dy * _causal_conv(v, w, row)
    dv, dw = _causal_conv_bwd(v, dy * gb, w, row)
    return (dgb, dv * u, dv * gc), (dw,)


def _split3(v):
    hi = v.astype(BF16)
    r1 = v - hi.astype(F32)
    mid = r1.astype(BF16)
    lo = (r1 - mid.astype(F32)).astype(BF16)
    return hi, mid, lo


def _exact_dot(v, m01, dims, v_is_lhs):
    def one(p):
        return (lax.dot_general(p, m01, dims, preferred_element_type=F32) if v_is_lhs
                else lax.dot_general(m01, p, dims, preferred_element_type=F32))
    hi, mid, lo = _split3(v)
    return (one(lo) + one(mid)) + one(hi)


_NN = (((1,), (0,)), ((), ()))
_NT = (((1,), (1,)), ((), ()))
_TN = (((0,), (0,)), ((), ()))


@jax.custom_vjp
def _cumsum_rows(tril, v):
    return _exact_dot(v, tril, _NN, False)


def _cumsum_rows_fwd(tril, v):
    return _cumsum_rows(tril, v), tril


def _cumsum_rows_bwd(tril, ct):
    return None, _exact_dot(ct, tril, _TN, False)


_cumsum_rows.defvjp(_cumsum_rows_fwd, _cumsum_rows_bwd)


@jax.custom_vjp
def _cumsum_lanes(tril, v):
    return _exact_dot(v, tril, _NT, True)


def _cumsum_lanes_fwd(tril, v):
    return _cumsum_lanes(tril, v), tril


def _cumsum_lanes_bwd(tril, ct):
    return None, _exact_dot(ct, tril, _NN, True)


_cumsum_lanes.defvjp(_cumsum_lanes_fwd, _cumsum_lanes_bwd)


@jax.custom_vjp
def _expand(e01, v):
    return _exact_dot(v, e01, _NN, True)


def _expand_fwd(e01, v):
    return _expand(e01, v), e01


def _expand_bwd(e01, ct):
    return None, _exact_dot(ct, e01, _NT, True)


_expand.defvjp(_expand_fwd, _expand_bwd)


def _causal_mask(n):
    li = lax.broadcasted_iota(jnp.int32, (n, n), 0)
    si = lax.broadcasted_iota(jnp.int32, (n, n), 1)
    return si <= li


def _dt_prep(dtc, dtr, bias_r, bias_c, alog_r, alog_c):
    dt_c = _softplus(dtc + bias_r)
    dt_r = _softplus(dtr + bias_c)
    tril = jnp.where(_causal_mask(dtc.shape[0]), 1.0, 0.0).astype(BF16)
    cs_c = _cumsum_rows(tril, dt_c * (-jnp.exp(alog_r)))
    cs_r = _cumsum_lanes(tril, dt_r * (-jnp.exp(alog_c)))
    return dt_c, cs_c, cs_r


def _ssd_chunk(r_heads, xs, bg, cg, dt_c, cs_c, cs_rg, e01, dskip_e, hp):
    l_len, rp = xs.shape
    p = rp // r_heads
    causal = _causal_mask(l_len)
    lane_head = lax.broadcasted_iota(jnp.int32, (1, rp), 1) // p
    dt_e = _expand(e01, dt_c)
    cs_e = _expand(e01, cs_c)
    cl_e = cs_e[l_len - 1:l_len, :]
    x = xs * dt_e
    bgb, cgb = bg.astype(BF16), cg.astype(BF16)
    cb = lax.dot_general(cgb, bgb, _NT, preferred_element_type=F32)
    ms, xm = [], []
    for r in range(r_heads):
        seg = cs_e[:, r * p:r * p + 1] - cs_rg[r:r + 1, :]
        decay = jnp.exp(jnp.where(causal, seg, -1e30))
        ms.append((cb * decay).astype(BF16))
        xm.append(jnp.where(lane_head == r, x, 0.0).astype(BF16))
    y_diag = lax.dot_general(jnp.concatenate(ms, axis=1), jnp.concatenate(xm, axis=0), _NN,
                             preferred_element_type=F32)
    y_off = lax.dot_general(cgb, hp.astype(BF16), _NN, preferred_element_type=F32) * jnp.exp(cs_e)
    xd = (x * jnp.exp(cl_e - cs_e)).astype(BF16)
    states = lax.dot_general(bgb, xd, _TN, preferred_element_type=F32)
    h_next = hp * jnp.exp(cl_e) + states
    y = y_diag + y_off + dskip_e * xs
    return y, h_next


def _ssd_dt(dtc, dtr, small, cots=None):
    t_len, heads = dtc.shape[0], dtr.shape[0]
    nc = t_len // CHUNK
    col = pl.BlockSpec((CHUNK, LANES), lambda c: (c, 0))
    row = pl.BlockSpec((heads, CHUNK), lambda c: (0, c))
    full = [pl.BlockSpec(s.shape, lambda c: (0, 0)) for s in small]
    shapes = [jax.ShapeDtypeStruct((t_len, LANES), F32), jax.ShapeDtypeStruct((t_len, LANES), F32),
              jax.ShapeDtypeStruct((heads, t_len), F32)]
    if cots is None:
        def body(dtc_ref, dtr_ref, br, bc, ar, ac, dt_ref, csc_ref, csr_ref):
            dt_ref[...], csc_ref[...], csr_ref[...] = _dt_prep(dtc_ref[...], dtr_ref[...], br[...], bc[...],
                                                                ar[...], ac[...])
        return _call(body, name="ssd_dt", grid=(nc,), in_specs=[col, row] + full, out_specs=[col, col, row],
                     out_shape=shapes, args=(dtc, dtr, *small), sem=("parallel",))[0]

    g_dt, g_csc, g_csr, ddk, e01 = cots

    def body(dtc_ref, dtr_ref, br, bc, ar, ac, g_dt_ref, g_csc_ref, g_csr_ref, ddk_ref, e_ref,
             ddtc_ref, ddtr_ref, *dsmall):
        _, vjp = jax.vjp(_dt_prep, dtc_ref[...], dtr_ref[...], br[...], bc[...], ar[...], ac[...])
        grads = vjp((g_dt_ref[...], g_csc_ref[...], g_csr_ref[...]))
        ddtc_ref[...], ddtr_ref[...] = grads[0], grads[1]
        ddk8 = jnp.broadcast_to(ddk_ref[...], (8, ddk_ref.shape[1]))
        dskip = _exact_dot(ddk8, e_ref[...], _NT, True)[0:1, :]

        @pl.when(pl.program_id(0) == 0)
        def _():
            for r in dsmall:
                r[...] = jnp.zeros_like(r)

        for r, gr in zip(dsmall, tuple(grads[2:]) + (dskip,)):
            r[...] += gr

    acc = list(small) + [small[0]]
    return _call(body, name="d_ssd_dt", grid=(nc,),
                 in_specs=[col, row] + full + [col, col, row, pl.BlockSpec((None, 1, e01.shape[1]), lambda c: (c, 0, 0)),
                                               pl.BlockSpec(e01.shape, lambda c: (0, 0))],
                 out_specs=[col, row] + [pl.BlockSpec(s.shape, lambda c: (0, 0)) for s in acc],
                 out_shape=[shapes[0], shapes[2]] + [jax.ShapeDtypeStruct(s.shape, F32) for s in acc],
                 args=(dtc, dtr, *small, g_dt, g_csc, g_csr, ddk, e01), sem=("arbitrary",))[0]


def _ssd_specs(t_len, d_ssm, r_heads, reverse):
    rp = r_heads * HEADDIM
    nc = t_len // CHUNK
    per = next(p for p in (SSD_CHUNKS_PER_STEP, 2, 1) if nc % p == 0)
    ns, rows, gs = nc // per, per * CHUNK, SSD_GROUPS_PER_STEP
    cidx = (lambda c: ns - 1 - c) if reverse else (lambda c: c)
    b_off = d_ssm // (N_STATE * gs)
    specs = dict(
        xs=pl.BlockSpec((rows, gs * rp), lambda c, g: (cidx(c), g)),
        b=pl.BlockSpec((rows, gs * N_STATE), lambda c, g: (cidx(c), b_off + g)),
        c=pl.BlockSpec((rows, gs * N_STATE), lambda c, g: (cidx(c), b_off + N_GROUPS // gs + g)),
        grad_bc=pl.BlockSpec((rows, gs * N_STATE), lambda c, g: (cidx(c), g)),
        col=pl.BlockSpec((rows, LANES), lambda c, g: (cidx(c), 0)),
        csr=pl.BlockSpec((gs, r_heads, rows), lambda c, g: (g, 0, cidx(c))),
        e01=pl.BlockSpec((LANES, gs * rp), lambda c, g: (0, g)),
        dskip=pl.BlockSpec((1, gs * rp), lambda c, g: (0, g)),
        hprev=pl.BlockSpec((per, gs, N_STATE, rp), lambda c, g: (cidx(c), g, 0, 0)),
        ddk=pl.BlockSpec((per, 1, gs * rp), lambda c, g: (cidx(c), 0, g)),
    )
    return specs, nc, ns, per, rp


def _ssd_fwd(xbc, dt_c, cs_c, cs_r3, e01, dskip_e, *, d_ssm, r_heads, ride=None):
    t_len = xbc.shape[0]
    sp, nc, ns, per, rp = _ssd_specs(t_len, d_ssm, r_heads, False)

    def body(xs_ref, b_ref, c_ref, dt_ref, csc_ref, csr_ref, e_ref, dk_ref, y_ref, hprev_ref, h_ref):
        c, gp = pl.program_id(0), pl.program_id(1)
        groups = [gp * SSD_GROUPS_PER_STEP + gi for gi in range(SSD_GROUPS_PER_STEP)]

        @pl.when(c == 0)
        def _():
            for g in groups:
                h_ref[g] = jnp.zeros((N_STATE, rp), F32)

        hp = [h_ref[g] for g in groups]
        for s in range(per):
            r = pl.ds(s * CHUNK, CHUNK)
            for gi in range(SSD_GROUPS_PER_STEP):
                cols, bc = pl.ds(gi * rp, rp), pl.ds(gi * N_STATE, N_STATE)
                hprev_ref[s, gi] = hp[gi]
                y, hp[gi] = _ssd_chunk(r_heads, xs_ref[r, cols].astype(F32), b_ref[r, bc].astype(F32),
                                       c_ref[r, bc].astype(F32), dt_ref[r, :], csc_ref[r, :], csr_ref[gi, :, r],
                                       e_ref[:, cols], dk_ref[:, cols], hp[gi])
                y_ref[r, cols] = y
        for gi, g in enumerate(groups):
            h_ref[g] = hp[gi]

    return _call(
        body, name="ssd_fwd", grid=(ns, N_GROUPS // SSD_GROUPS_PER_STEP),
        in_specs=[sp["xs"], sp["b"], sp["c"], sp["col"], sp["col"], sp["csr"], sp["e01"], sp["dskip"]],
        out_specs=[sp["xs"], sp["hprev"]],
        out_shape=[jax.ShapeDtypeStruct((t_len, d_ssm), F32),
                   jax.ShapeDtypeStruct((nc, N_GROUPS, N_STATE, rp), F32)],
        args=(xbc, xbc, xbc, dt_c, cs_c, cs_r3, e01, dskip_e), scratch=[pltpu.VMEM((N_GROUPS, N_STATE, rp), F32)],
        sem=("arbitrary", "arbitrary"), ride=ride)


def _ssd_bwd(xbc, dt_c, cs_c, cs_r3, e01, dskip_e, hprev, dy, *, d_ssm, r_heads, ride=None):
    t_len = xbc.shape[0]
    sp, nc, ns, per, rp = _ssd_specs(t_len, d_ssm, r_heads, True)

    def body(xs_ref, b_ref, c_ref, dt_ref, csc_ref, csr_ref, e_ref, dk_ref, hprev_ref, dy_ref,
             dxs_ref, db_ref, dc_ref, ddt_ref, dcsc_ref, dcsr_ref, ddk_ref, dh_ref):
        c, gp = pl.program_id(0), pl.program_id(1)
        groups = [gp * SSD_GROUPS_PER_STEP + gi for gi in range(SSD_GROUPS_PER_STEP)]

        @pl.when(gp == 0)
        def _():
            ddt_ref[...] = jnp.zeros_like(ddt_ref)
            dcsc_ref[...] = jnp.zeros_like(dcsc_ref)

        @pl.when(c == 0)
        def _():
            for g in groups:
                dh_ref[g] = jnp.zeros((N_STATE, rp), F32)

        dh = [dh_ref[g] for g in groups]
        for s in reversed(range(per)):
            r = pl.ds(s * CHUNK, CHUNK)
            ddt_sum, dcsc_sum = ddt_ref[r, :], dcsc_ref[r, :]
            for gi in range(SSD_GROUPS_PER_STEP):
                cols, bc = pl.ds(gi * rp, rp), pl.ds(gi * N_STATE, N_STATE)
                e01 = e_ref[:, cols]
                fn = lambda xs, bg, cg, dt, csc, csr, dk, hp: _ssd_chunk(r_heads, xs, bg, cg, dt, csc, csr, e01, dk, hp)
                _, vjp = jax.vjp(fn, xs_ref[r, cols].astype(F32), b_ref[r, bc].astype(F32), c_ref[r, bc].astype(F32),
                                 dt_ref[r, :], csc_ref[r, :], csr_ref[gi, :, r], dk_ref[:, cols], hprev_ref[s, gi])
                dxs, dbg, dcg, ddt, dcsc, dcsr, ddk, dh[gi] = vjp((dy_ref[r, cols], dh[gi]))
                dxs_ref[r, cols] = dxs.astype(dxs_ref.dtype)
                db_ref[r, bc] = dbg.astype(db_ref.dtype)
                dc_ref[r, bc] = dcg.astype(dc_ref.dtype)
                ddt_sum, dcsc_sum = ddt_sum + ddt, dcsc_sum + dcsc
                dcsr_ref[gi, :, r] = dcsr
                ddk_ref[s, :, cols] = ddk
            ddt_ref[r, :], dcsc_ref[r, :] = ddt_sum, dcsc_sum
        for gi, g in enumerate(groups):
            dh_ref[g] = dh[gi]

    n_bc = N_GROUPS * N_STATE
    return _call(
        body, name="ssd_bwd", grid=(ns, N_GROUPS // SSD_GROUPS_PER_STEP),
        in_specs=[sp["xs"], sp["b"], sp["c"], sp["col"], sp["col"], sp["csr"], sp["e01"], sp["dskip"], sp["hprev"],
                  sp["xs"]],
        out_specs=[sp["xs"], sp["grad_bc"], sp["grad_bc"], sp["col"], sp["col"], sp["csr"], sp["ddk"]],
        out_shape=[jax.ShapeDtypeStruct((t_len, d_ssm), BF16), jax.ShapeDtypeStruct((t_len, n_bc), BF16),
                   jax.ShapeDtypeStruct((t_len, n_bc), BF16), jax.ShapeDtypeStruct(dt_c.shape, F32),
                   jax.ShapeDtypeStruct(cs_c.shape, F32), jax.ShapeDtypeStruct(cs_r3.shape, F32),
                   jax.ShapeDtypeStruct((nc, 1, d_ssm), F32)],
        args=(xbc, xbc, xbc, dt_c, cs_c, cs_r3, e01, dskip_e, hprev, dy),
        scratch=[pltpu.VMEM((N_GROUPS, N_STATE, rp), F32)], sem=("arbitrary", "arbitrary"), ride=ride)


def _chip_sum(src, sib, *, name):
    rows, cols = src.shape[1:]
    tr = _tile(rows, 256, BF16_ROWS)
    core = lax.axis_index("c").astype(jnp.int32).reshape(1)

    def body(c_ref, a_ref, b_ref, o_ref):
        o_ref[...] = (a_ref[...].astype(F32) + b_ref[...].astype(F32)).astype(o_ref.dtype)

    grid_spec = pltpu.PrefetchScalarGridSpec(
        num_scalar_prefetch=1, grid=(N_CHIPS, rows // tr),
        in_specs=[pl.BlockSpec((None, tr, cols), lambda q, i, c_ref: (2 * q + c_ref[0], i, 0)),
                  pl.BlockSpec((None, tr, cols), lambda q, i, c_ref: (q, i, 0))],
        out_specs=pl.BlockSpec((None, tr, cols), lambda q, i, c_ref: (q, i, 0)))
    return pl.pallas_call(
        body, name=name, grid_spec=grid_spec, out_shape=jax.ShapeDtypeStruct(sib.shape, sib.dtype),
        compiler_params=pltpu.CompilerParams(dimension_semantics=("parallel", "parallel"), vmem_limit_bytes=VMEM_LIMIT),
    )(core, src, sib)


def _adamw(w, g, m, v):
    m = ADAM_B1 * m + (1.0 - ADAM_B1) * g
    v = ADAM_B2 * v + (1.0 - ADAM_B2) * (g * g)
    m_hat = m / (1.0 - ADAM_B1 ** ADAM_STEP)
    v_hat = v / (1.0 - ADAM_B2 ** ADAM_STEP)
    delta = -ADAM_LR * (m_hat / (jnp.sqrt(v_hat) + ADAM_EPS) + ADAM_WD * w)
    return delta, m, v


def _reduce_adamw(parts, w, m, v, *, name):
    n_parts = parts.shape[0]
    rows, cols = w.shape
    tr = _tile(rows, 128, BF16_ROWS)

    def body(p_ref, w_ref, m_ref, v_ref, g_ref, d_ref, mo_ref, vo_ref):
        g = p_ref[0].astype(F32)
        for k in range(1, n_parts):
            g = g + p_ref[k].astype(F32)
        delta, mn, vn = _adamw(w_ref[...], g, m_ref[...], v_ref[...])
        g_ref[...] = g
        d_ref[...] = delta
        mo_ref[...] = mn
        vo_ref[...] = vn

    spec = pl.BlockSpec((tr, cols), lambda i: (i, 0))
    outs, _ = _call(
        body, name=name, grid=(rows // tr,),
        in_specs=[pl.BlockSpec((n_parts, tr, cols), lambda i: (0, i, 0)), spec, spec, spec],
        out_specs=[spec] * 4, out_shape=[jax.ShapeDtypeStruct((rows, cols), F32)] * 4,
        args=(parts, w, m, v), sem=("parallel",))
    return outs


def _move_rows(src, src_row, name, extra=None, extra_row=None):
    rb, n_out, cols = ROW_BLOCK, len(src_row), src.shape[1]
    assert n_out % rb == 0 and src.shape[0] % rb == 0 and src.shape[0] // rb >= 3
    n_blocks, max_b0, seg_cap = n_out // rb, src.shape[0] // rb - 3, 4

    def segments(rows_of, lo):
        segs, r = [], 0
        while r < rb:
            if rows_of[r] < 0:
                r += 1
                continue
            e = r
            while e + 1 < rb and rows_of[e + 1] == rows_of[e] + 1:
                e += 1
            segs.append((r, e + 1, rows_of[r] - r - lo))
            r = e + 1
        assert len(segs) <= seg_cap
        return segs + [(0, 0, 0)] * (seg_cap - len(segs))

    table = []
    for j in range(n_blocks):
        rows_j = list(src_row[j * rb:(j + 1) * rb])
        valid = [v for v in rows_j if v >= 0]
        b0 = min(max((min(valid) // rb) if valid else 0, 0), max_b0)
        assert not valid or max(valid) < (b0 + 3) * rb
        row = [b0] + [v for seg in segments(rows_j, b0 * rb) for v in seg]
        extra_j = [] if extra is None else list(extra_row[j * rb:(j + 1) * rb])
        if extra is not None:
            row += [v for seg in segments(extra_j, 0) for v in seg]
        need_third = bool(valid) and max(valid) >= (b0 + 2) * rb
        third = b0 + 2 if need_third or not table else table[-1][-1]
        row += [int(need_third), int(any(v >= 0 for v in extra_j)), third]
        table.append(row)
    flag_third, flag_extra, col_third = len(table[0]) - 3, len(table[0]) - 2, len(table[0]) - 1
    table = jnp.asarray(table, jnp.int32)

    def select(tbl_ref, j, first, width, col0=0):
        r = lax.broadcasted_iota(jnp.int32, (rb, width), 0)
        c = lax.broadcasted_iota(jnp.int32, (rb, width), 1) + col0
        hit = jnp.zeros((rb, width), jnp.bool_)
        for s in range(seg_cap):
            lo, hi, off = (tbl_ref[j, first + 3 * s + i] for i in range(3))
            hit = hit | ((r >= lo) & (r < hi) & (c == r + off))
        return jnp.where(hit, 1.0, 0.0).astype(BF16)

    def body(tbl_ref, *refs):
        o_ref = refs[-1]
        j = pl.program_id(0)
        sel = select(tbl_ref, j, 1, 2 * rb)
        pick = lambda m, b: lax.dot_general(m, refs[b][...], _NN, preferred_element_type=F32)
        o_ref[...] = (pick(sel[:, :rb], 0) + pick(sel[:, rb:], 1)).astype(o_ref.dtype)

        @pl.when(tbl_ref[j, flag_third] == 1)
        def _():
            o_ref[...] = (o_ref[...].astype(F32) + pick(select(tbl_ref, j, 1, rb, 2 * rb), 2)).astype(o_ref.dtype)

        if extra is not None:
            @pl.when(tbl_ref[j, flag_extra] == 1)
            def _():
                more = lax.dot_general(select(tbl_ref, j, 1 + 3 * seg_cap, extra.shape[0]), refs[3][...], _NN,
                                       preferred_element_type=F32)
                o_ref[...] = (o_ref[...].astype(F32) + more).astype(o_ref.dtype)

    in_specs = [pl.BlockSpec((rb, cols), functools.partial(lambda b, j, tbl: (tbl[j, 0] + b, 0), b)) for b in range(2)]
    in_specs.append(pl.BlockSpec((rb, cols), lambda j, tbl: (tbl[j, col_third], 0)))
    args = [src, src, src]
    if extra is not None:
        in_specs.append(pl.BlockSpec(extra.shape, lambda j, tbl: (0, 0)))
        args.append(extra)
    grid_spec = pltpu.PrefetchScalarGridSpec(num_scalar_prefetch=1, grid=(n_blocks,), in_specs=in_specs,
                                             out_specs=pl.BlockSpec((rb, cols), lambda j, tbl: (j, 0)))
    return pl.pallas_call(
        body, name=name, grid_spec=grid_spec, out_shape=jax.ShapeDtypeStruct((n_out, cols), src.dtype),
        compiler_params=pltpu.CompilerParams(dimension_semantics=("parallel",), vmem_limit_bytes=VMEM_LIMIT),
    )(table, *args)


def _cols_of(g):
    return jnp.transpose(g, (1, 0, 2)).reshape(g.shape[1], -1)


def _pad_to(a, rows, cols):
    return jnp.pad(a, ((0, rows - a.shape[0]), (0, cols - a.shape[1])))


def kernel(x, norm_mix_g, w_in, ssm_conv_w, ssm_conv_b, ssm_dt_bias, ssm_A_log, ssm_D, ssm_norm_g, sc_conv_w, w_out, norm_ffn_g, w_gate, w_up, w_down, norm_final_g, loss_target, m_norm_mix_g, m_w_in, m_ssm_conv_w, m_ssm_conv_b, m_ssm_dt_bias, m_ssm_A_log, m_ssm_D, m_ssm_norm_g, m_sc_conv_w, m_w_out, m_norm_ffn_g, m_w_gate, m_w_up, m_w_down, m_norm_final_g, v_norm_mix_g, v_w_in, v_ssm_conv_w, v_ssm_conv_b, v_ssm_dt_bias, v_ssm_A_log, v_ssm_D, v_ssm_norm_g, v_sc_conv_w, v_w_out, v_norm_ffn_g, v_w_gate, v_w_up, v_w_down, v_norm_final_g):
    t_len, d = x.shape[1], x.shape[2]
    heads = d // HEADDIM
    r_heads = heads // N_GROUPS
    d_xbc = d + 2 * N_GROUPS * N_STATE
    ff_s = w_down.shape[1]
    ff = ff_s * N_DEV
    off_xbc, off_dt = d, d + d_xbc
    off_cb = off_dt + heads
    d_in = off_cb + 3 * d
    in_s = d_in // N_DEV
    in_p = -(-in_s // (2 * BF16_ROWS)) * (2 * BF16_ROWS)
    w_main = 4 * d + d_xbc
    me = 4 * lax.axis_index("x") + 2 * lax.axis_index("y") + lax.axis_index("c")

    x2 = x[0]
    target = loss_target[0]

    tpose = lambda a: jnp.transpose(a[0])
    win_s = _pad_to(tpose(w_in).astype(BF16), in_p, d)
    wg_s, wu_s = tpose(w_gate).astype(BF16), tpose(w_up).astype(BF16)
    wo_s, wd_s = w_out[0].astype(BF16), w_down[0].astype(BF16)
    small_w = jnp.concatenate([_pad_to(ssm_conv_w[0], K_SSM, d_xbc // N_DEV),
                               _pad_to(sc_conv_w[0], K_SC + 1, d_xbc // N_DEV)], axis=0)

    g1, g2, g3 = norm_mix_g, norm_ffn_g, norm_final_g.reshape(1, d)
    gs = ssm_norm_g
    small = [_pad_to(ssm_dt_bias, 1, LANES), ssm_dt_bias.reshape(heads, 1), _pad_to(ssm_A_log, 1, LANES),
             ssm_A_log.reshape(heads, 1)]
    e01 = (lax.broadcasted_iota(jnp.int32, (LANES, d), 1) // HEADDIM
           == lax.broadcasted_iota(jnp.int32, (LANES, d), 0)).astype(BF16)
    dskip_e = jnp.repeat(ssm_D, HEADDIM, axis=1)
    tr = _tile(t_len, 256, 8)
    tr_wide = _tile(t_len, 512, 8)
    tr_ff = _tile(t_len, 128, 8)
    cw = LANES
    slab = lambda col: col // cw

    (n1,), (gin, gsm) = _rows_call(lambda v, g: ((_rms(v, g),), ()), rows=t_len, tr=tr_wide, row_ins=[(x2, d, 0)],
                                   full_ins=[g1], row_outs=[(d, BF16)], acc_outs=[], name="norm_mix",
                                   ride=_gather_relayed([win_s], [small_w]))
    in_pieces = []
    for k in range(N_DEV):
        for a, b, dst, shift in ((0, off_dt, 0, 0), (off_dt, off_cb, 1, -off_dt), (off_cb, d_in, 0, -heads)):
            s, e = max(k * in_s, a), min((k + 1) * in_s, b)
            if s < e:
                in_pieces.append((k, s - k * in_s, e - s, dst, s + shift))
    ref_row = lambda t: t if t < off_dt else t + heads
    wtm = _move_rows(gin.reshape(N_DEV * in_p, d),
                     [(ref_row(t) // in_s) * in_p + ref_row(t) % in_s for t in range(w_main)], "place_w_in")
    wtdt = jnp.zeros((LANES, d), BF16)
    for k, r0, n, dst, d0 in in_pieces:
        if dst == 1:
            wtdt = lax.dynamic_update_slice(wtdt, gin[k, r0:r0 + n], (d0, 0))
    cw_ssm = _cols_of(gsm[:, :K_SSM, :])
    cw_sc = _cols_of(gsm[:, K_SSM:K_SSM + K_SC, :d // N_DEV])

    proj, (go_1, gg_1) = _matmul(n1, wtm, tb=True, out_dtype=BF16, name="proj_main",
                                 ride=_gather_chips([wo_s, wg_s]))
    dt_raw, _ = _matmul(n1, wtdt, tb=True, out_dtype=F32, name="proj_dt")
    dt_raw_t = jnp.transpose(dt_raw[:, :heads])
    (xbc,), (go, gg) = _cols_call(_conv_silu_fwd, rows=t_len, cols=d_xbc, cw=cw, col_ins=[(proj, slab(off_xbc))],
                                  par_ins=[(cw_ssm, 0), (ssm_conv_b, 0)], col_outs=[BF16], par_outs=[],
                                  name="ssm_conv", ride=_gather_sibling([go_1, gg_1]))
    dt_c, cs_c, cs_r = _ssd_dt(dt_raw, dt_raw_t, small)
    cs_r3 = cs_r.reshape(N_GROUPS, r_heads, t_len)
    up_cut = int(ff_s * W_UP_GATHER_SPLIT) // BF16_ROWS * BF16_ROWS
    down_cut = int(ff_s * W_DOWN_GATHER_SPLIT) // BF16_ROWS * BF16_ROWS
    half_cut = ff_s // 2 // BF16_ROWS * BF16_ROWS
    (y_ssd, hprev), (gu_1,) = _ssd_fwd(xbc, dt_c, cs_c, cs_r3, e01, dskip_e, d_ssm=d, r_heads=r_heads,
                                       ride=_gather_chips([wu_s], rows=(0, up_cut)))

    def gate_norm(y, z, g):
        z = z.astype(F32)
        return _rms(y * (z * _sigmoid(z)), g)

    (y_mix,), _ = _rows_call(lambda y, z, g: ((gate_norm(y, z, g),), ()), rows=t_len, tr=tr_wide,
                             row_ins=[(y_ssd, d, 0), (proj, d, 0)], full_ins=[gs], row_outs=[(d, BF16, 2 * d)],
                             acc_outs=[], name="ssm_gate_norm")
    wgt, wo = gg.reshape(ff, d), go.reshape(2 * d, d)
    sc0 = slab(d + d_xbc)
    (y_mix,), _ = _cols_call(_shortconv_fwd, rows=t_len, cols=d, cw=cw,
                             col_ins=[(proj, sc0), (proj, sc0 + slab(d)), (proj, sc0 + 2 * slab(d))],
                             par_ins=[(cw_sc, 0)], col_outs=[BF16], par_outs=[], name="shortconv",
                             into=(y_mix, slab(d)))
    h1, (gu_1, gd_1) = _matmul(y_mix, wo, out_dtype=F32, add=x2, name="out_proj", ride=_merge(
        _gather_chips([wu_s], rows=(up_cut, ff_s - up_cut), into=[gu_1]), _gather_chips([wd_s], rows=(0, down_cut))))
    (n2,), _ = _rows_call(lambda v, g: ((_rms(v, g),), ()), rows=t_len, tr=tr_wide, row_ins=[(h1, d, 0)],
                          full_ins=[g2], row_outs=[(d, BF16)], acc_outs=[], name="norm_ffn")
    g_ff, (gd_1, gu) = _matmul(n2, wgt, tb=True, out_dtype=BF16, name="ffn_gate", ride=_merge(
        _gather_chips([wd_s], rows=(down_cut, ff_s - down_cut), into=[gd_1]), _gather_sibling([gu_1])))
    wut = gu.reshape(ff, d)
    (u_ff, a_ff), (gd,) = _matmul(n2, wut, tb=True, name="ffn_up", ride=_gather_sibling([gd_1]),
                                  post=(lambda uv, gv: (uv, gv * _sigmoid(gv) * uv), [g_ff], [BF16, BF16]),
                                  tn_max=MM_TILE_N_POST)
    wd = gd.reshape(ff, d)
    h2, _ = _matmul(a_ff, wd, out_dtype=F32, add=h1, name="ffn_down")

    def head(hv, tv, g):
        def f(hh, gg_):
            e = _rms(hh, gg_) - tv
            return (0.5 / d) * jnp.sum(e * e)
        val, (dh, dg) = jax.value_and_grad(f, argnums=(0, 1))(hv, g)
        return (dh, dh), (jnp.full((1, LANES), val, F32), dg)

    (dh2, dh2_b, loss_acc, dg3), _ = _rows_call(head, rows=t_len, tr=tr, row_ins=[(h2, d, 0), (target, d, 0)],
                                                full_ins=[g3], row_outs=[(d, F32), (d, BF16)],
                                                acc_outs=[(1, LANES), (1, d)], name="loss_head")
    loss = lax.psum(loss_acc[0, 0], ("x", "y", "c"))

    def act_bwd(dav, gv, uv):
        s = _sigmoid(gv)
        return dav * uv * (s * (1.0 + gv * (1.0 - s))), dav * gv * s

    (dg_ff, du_ff), _ = _matmul(dh2_b, wd, tb=True, name="d_ffn_gate_up",
                                post=(act_bwd, [g_ff, u_ff], [BF16, BF16]), tn_max=MM_TILE_N_POST)
    dwd, _ = _matmul(a_ff, dh2_b, ta=True, out_dtype=BF16, name="d_w_down")
    dwd8 = dwd.reshape(N_DEV, ff_s, d)
    dn2, (sib_d,) = _matmul(dg_ff, wgt, out_dtype=F32, name="d_norm_ffn_out_gate", ride=_scatter_sibling([dwd8]))
    chip_d = _chip_sum(dwd8, sib_d, name="chip_sum_w_down")
    dn2, (parts_d,) = _matmul(du_ff, wut, out_dtype=F32, add=dn2, name="d_norm_ffn_out_up",
                              ride=_scatter_chips([chip_d], rows=(0, half_cut)))
    dwg, (parts_d,) = _matmul(dg_ff, n2, ta=True, out_dtype=BF16, name="d_w_gate",
                              ride=_scatter_chips([chip_d], rows=(half_cut, ff_s - half_cut), into=[parts_d]))
    dwu, _ = _matmul(du_ff, n2, ta=True, out_dtype=BF16, name="d_w_up")
    dwg8, dwu8 = dwg.reshape(N_DEV, ff_s, d), dwu.reshape(N_DEV, ff_s, d)

    def norm_bwd(v, dn, dres, g):
        _, vjp = jax.vjp(_rms, v, g)
        dv, dg = vjp(dn)
        return (dv + dres,), (dg,)

    def norm_bwd_2(v, dn, dres, g):
        (dv,), acc = norm_bwd(v, dn, dres, g)
        return (dv, dv), acc

    (dh1, dh1_b, dg2), (sib_g, sib_u) = _rows_call(norm_bwd_2, rows=t_len, tr=tr,
                                                   row_ins=[(h1, d, 0), (dn2, d, 0), (dh2, d, 0)], full_ins=[g2],
                                                   row_outs=[(d, F32), (d, BF16)], acc_outs=[(1, d)], name="d_norm_ffn",
                                                   ride=_scatter_sibling([dwg8, dwu8]))
    chip_g = _chip_sum(dwg8, sib_g, name="chip_sum_w_gate")
    chip_u = _chip_sum(dwu8, sib_u, name="chip_sum_w_up")

    dy_mix, _ = _matmul(dh1_b, wo, tb=True, out_dtype=BF16, name="d_y_mix")
    dwo, _ = _matmul(y_mix, dh1_b, ta=True, out_dtype=BF16, name="d_w_out")
    dwo8 = dwo.reshape(N_DEV, 2 * d // N_DEV, d)
    (dgb, dgc, du, dcw_sc), (sib_o,) = _cols_call(
        _shortconv_bwd, rows=t_len, cols=d, cw=cw,
        col_ins=[(proj, sc0), (proj, sc0 + slab(d)), (proj, sc0 + 2 * slab(d)), (dy_mix, slab(d))],
        par_ins=[(cw_sc, 0)], col_outs=[BF16] * 3, par_outs=[K_SC], name="d_shortconv",
        ride=_scatter_sibling([dwo8]))
    chip_o = _chip_sum(dwo8, sib_o, name="chip_sum_w_out")

    def gate_norm_bwd(y, z, dyo, g):
        _, vjp = jax.vjp(gate_norm, y, z.astype(F32), g)
        dy, dz, dg = vjp(dyo.astype(F32))
        return (dy, dz), (dg,)

    (dy_ssd, dproj, dgs), _ = _rows_call(gate_norm_bwd, rows=t_len, tr=tr,
                                         row_ins=[(y_ssd, d, 0), (proj, d, 0), (dy_mix, d, 0)], full_ins=[gs],
                                         row_outs=[(d, F32), (d, BF16, w_main)], acc_outs=[(1, d)],
                                         name="d_ssm_gate_norm")
    (dxs, dbm, dcm, g_dt, g_csc, g_csr3, ddk), (parts_g, parts_o) = _ssd_bwd(
        xbc, dt_c, cs_c, cs_r3, e01, dskip_e, hprev, dy_ssd, d_ssm=d, r_heads=r_heads,
        ride=_scatter_chips([chip_g, chip_o]))
    ddt_c, ddt_r, dbias_r, dbias_c, dalog_r, dalog_c, ddskip = _ssd_dt(
        dt_raw, dt_raw_t, small, cots=(g_dt, g_csc, g_csr3.reshape(heads, t_len), ddk, e01))
    dcw_parts, dcb_parts, col0 = [], [], 0
    for tag, dpart in (("x", dxs), ("b", dbm), ("c", dcm)):
        (dproj, dcw_p, dcb_p), _ = _cols_call(
            _conv_silu_bwd, rows=t_len, cols=dpart.shape[1], cw=cw,
            col_ins=[(proj, slab(off_xbc + col0)), (dpart, 0)], par_ins=[(cw_ssm, slab(col0)), (ssm_conv_b, slab(col0))],
            col_outs=[BF16], par_outs=[K_SSM, 1], name="d_ssm_conv_" + tag, into=(dproj, slab(off_xbc + col0)))
        dcw_parts.append(dcw_p)
        dcb_parts.append(dcb_p)
        col0 += dpart.shape[1]
    dcw_ssm, dcb_ssm = jnp.concatenate(dcw_parts, axis=1), jnp.concatenate(dcb_parts, axis=1)
    for i, part in enumerate((dgb, dgc, du)):
        dproj = lax.dynamic_update_slice(dproj, part, (0, d + d_xbc + i * d))
    ddt = ddt_c + _pad_to(jnp.transpose(ddt_r), t_len, LANES)
    dwm, (parts_u,) = _matmul(dproj, n1, ta=True, out_dtype=BF16, name="d_w_in_main",
                              ride=_scatter_chips([chip_u]))
    dwdt, _ = _matmul(ddt, n1, ta=True, out_dtype=BF16, name="d_w_in_dt")
    own_ref = [k * in_s + i if i < in_s else -1 for k in range(N_DEV) for i in range(in_p)]
    dwin8 = _move_rows(
        dwm, [-1 if g < 0 or off_dt <= g < off_cb else (g if g < off_dt else g - heads) for g in own_ref],
        "place_d_w_in", extra=dwdt, extra_row=[g - off_dt if off_dt <= g < off_cb else -1 for g in own_ref],
    ).reshape(N_DEV, in_p, d)
    dn1, (sib_in,) = _matmul(ddt, wtdt, out_dtype=F32, name="d_norm_mix_out_dt", ride=_scatter_sibling([dwin8]))
    chip_in = _chip_sum(dwin8, sib_in, name="chip_sum_w_in")
    dn1, (parts_in, _) = _matmul(dproj, wtm, out_dtype=F32, add=dn1, name="d_norm_mix_out",
                                 ride=_scatter_chips_relayed([chip_in]))
    (dx, dg1), _ = _rows_call(norm_bwd, rows=t_len, tr=tr, row_ins=[(x2, d, 0), (dn1, d, 0), (dh1, d, 0)],
                              full_ins=[g1], row_outs=[(d, F32)], acc_outs=[(1, d)], name="d_norm_mix")

    wide = d_xbc
    rows_small = [dg1, dcb_ssm, dbias_r + _pad_to(dbias_c.reshape(1, heads), 1, LANES),
                  dalog_r + _pad_to(dalog_c.reshape(1, heads), 1, LANES), ddskip, dgs, dg2, dg3]
    packed = jnp.concatenate([_pad_to(r, 1, wide) for r in rows_small]
                             + [dcw_ssm, _pad_to(dcw_sc, K_SC, wide), jnp.zeros((1, wide), F32)], axis=0)
    (p_small,) = _comm(_gather_all([packed]), "gather_small_grads")

    conv_lo = me * (d_xbc // N_DEV)
    sc_lo = me * (d // N_DEV)

    def pack_state(vals):
        (nm, cb, dtb, al, dk, sg, nf, nfin, cws, scs) = vals
        rows = [_pad_to(a.reshape(1, -1), 1, wide) for a in (nm, cb, dtb, al, dk, sg, nf, nfin)]
        cws_full = lax.dynamic_update_slice(jnp.zeros((K_SSM, wide), F32), cws[0], (0, conv_lo))
        scs_full = lax.dynamic_update_slice(jnp.zeros((K_SC, wide), F32), scs[0], (0, sc_lo))
        return jnp.concatenate(rows + [cws_full, scs_full, jnp.zeros((1, wide), F32)], axis=0)

    w_small = pack_state((norm_mix_g, ssm_conv_b, ssm_dt_bias, ssm_A_log, ssm_D, ssm_norm_g, norm_ffn_g, norm_final_g,
                          ssm_conv_w, sc_conv_w))
    m_small = pack_state((m_norm_mix_g, m_ssm_conv_b, m_ssm_dt_bias, m_ssm_A_log, m_ssm_D, m_ssm_norm_g, m_norm_ffn_g,
                          m_norm_final_g, m_ssm_conv_w, m_sc_conv_w))
    v_small = pack_state((v_norm_mix_g, v_ssm_conv_b, v_ssm_dt_bias, v_ssm_A_log, v_ssm_D, v_ssm_norm_g, v_norm_ffn_g,
                          v_norm_final_g, v_ssm_conv_w, v_sc_conv_w))

    tin = lambda a: _pad_to(tpose(a), in_p, d)
    tin_back = lambda a: jnp.transpose(a[:in_s])[None]
    t_back = lambda a: jnp.transpose(a)[None]
    upd = {
        "w_in": [tin_back(o) for o in _reduce_adamw(parts_in, tin(w_in), tin(m_w_in), tin(v_w_in), name="adamw_w_in")],
        "w_out": [o[None] for o in _reduce_adamw(parts_o, w_out[0], m_w_out[0], v_w_out[0], name="adamw_w_out")],
        "w_gate": [t_back(o) for o in _reduce_adamw(parts_g, tpose(w_gate), tpose(m_w_gate), tpose(v_w_gate),
                                                    name="adamw_w_gate")],
        "w_up": [t_back(o) for o in _reduce_adamw(parts_u, tpose(w_up), tpose(m_w_up), tpose(v_w_up),
                                                  name="adamw_w_up")],
        "w_down": [o[None] for o in _reduce_adamw(parts_d, w_down[0], m_w_down[0], v_w_down[0], name="adamw_w_down")],
    }
    small_upd = _reduce_adamw(p_small, w_small, m_small, v_small, name="adamw_small")

    def unpack(packed_out):
        vec = lambda i, n, shape: packed_out[i, :n].reshape(shape)
        return {
            "norm_mix_g": vec(0, d, (1, d)), "ssm_conv_b": vec(1, d_xbc, (1, d_xbc)),
            "ssm_dt_bias": vec(2, heads, (1, heads)), "ssm_A_log": vec(3, heads, (1, heads)),
            "ssm_D": vec(4, heads, (1, heads)), "ssm_norm_g": vec(5, d, (1, d)), "norm_ffn_g": vec(6, d, (1, d)),
            "norm_final_g": vec(7, d, (d,)),
            "ssm_conv_w": lax.dynamic_slice(packed_out[8:8 + K_SSM], (0, conv_lo), (K_SSM, d_xbc // N_DEV))[None],
            "sc_conv_w": lax.dynamic_slice(packed_out[8 + K_SSM:8 + K_SSM + K_SC], (0, sc_lo), (K_SC, d // N_DEV))[None],
        }

    names = ["norm_mix_g", "w_in", "ssm_conv_w", "ssm_conv_b", "ssm_dt_bias", "ssm_A_log", "ssm_D", "ssm_norm_g",
             "sc_conv_w", "w_out", "norm_ffn_g", "w_gate", "w_up", "w_down", "norm_final_g"]
    outs = []
    for kind in range(4):
        small_k = unpack(small_upd[kind])
        for nm in names:
            outs.append(upd[nm][kind] if nm in upd else small_k[nm])
    return (loss, dx[None], *outs)
```

```python
import collections
import functools

import jax
import jax.numpy as jnp
from jax import lax
from jax.experimental import pallas as pl
from jax.experimental.pallas import tpu as pltpu

F32 = jnp.float32
BF16 = jnp.bfloat16

N_DEV = 8
N_CHIPS = 4
HEADDIM = 64
N_GROUPS = 8
N_STATE = 128
CHUNK = 128
K_SSM = 4
K_SC = 3
EPS = 1e-5
LANES = 128
BF16_ROWS = 16
MM_TILE_MN = 1408
MM_TILE_K = 2816
W_IN_SCATTER_SPLIT = 13 / 14
W_UP_GATHER_SPLIT = 0.7
W_DOWN_GATHER_SPLIT = 0.3
MM_TILE_N_POST = 704
SSD_CHUNKS_PER_STEP = 4
SSD_GROUPS_PER_STEP = 4
ROW_BLOCK = 256
V7X_VMEM_BYTES = 64 * 1024 * 1024
VMEM_LIMIT = (V7X_VMEM_BYTES * 3) // 4

ADAM_LR = 0.001
ADAM_B1 = 0.9
ADAM_B2 = 0.999
ADAM_EPS = 1e-08
ADAM_WD = 0.01
ADAM_STEP = 10


def _tile(n, pref, align):
    t = min(pref, n)
    t -= t % align
    while t >= align:
        if n % t == 0:
            return t
        t -= align
    return n


_Ride = collections.namedtuple("_Ride", ["ins", "out_shapes", "aliases", "nsem", "plan", "finish"], defaults=(None,))
_ANY = pl.BlockSpec(memory_space=pl.ANY)


def _coords():
    return lax.axis_index("x"), lax.axis_index("y"), lax.axis_index("c")


def _other_chips(x, y):
    return ((1 - x, y), (x, 1 - y), (1 - x, 1 - y))


def _remote(src, dst, send, recv, k, dev):
    return functools.partial(pltpu.make_async_remote_copy, src_ref=src, dst_ref=dst, send_sem=send.at[k],
                             recv_sem=recv.at[k], device_id=dev, device_id_type=pl.DeviceIdType.MESH)


def _local(src, dst, sem):
    return functools.partial(pltpu.make_async_copy, src, dst, sem)


def _start_all(plan):
    for kind, make in plan:
        if kind != "arrival":
            make().start()


def _wait_all(plan):
    for kind, make in plan:
        if kind == "local":
            make().wait()
        elif kind == "out":
            make().wait_send()
        else:
            make().wait_recv()


def _gather_chips(srcs, rows=None, into=None):
    n = len(srcs)

    def plan(ins, outs, send, recv, base):
        x, y, c = _coords()
        me = 4 * x + 2 * y + c
        cut = (lambda ref: ref) if rows is None else (lambda ref: ref.at[pl.ds(rows[0], rows[1])])
        d = []
        for a, (src, dst) in enumerate(zip(ins[:n], outs)):
            k = base + 4 * a
            d.append(("local", _local(cut(src), cut(dst.at[me]), send.at[k + 3])))
            for j, (px, py) in enumerate(_other_chips(x, y)):
                d.append(("out", _remote(cut(src), cut(dst.at[me]), send, recv, k + j, (px, py, c))))
                d.append(("arrival", _remote(cut(src), cut(dst.at[4 * px + 2 * py + c]), send, recv, k + j,
                                             (px, py, c))))
        return d
    shapes = [jax.ShapeDtypeStruct((N_DEV,) + s.shape, s.dtype) for s in srcs]
    if into is None:
        return _Ride(list(srcs), shapes, {}, 4 * n, plan)
    return _Ride(list(srcs) + list(into), shapes, {n + a: a for a in range(n)}, 4 * n, plan)


def _gather_sibling(bufs):
    def plan(ins, outs, send, recv, base):
        x, y, c = _coords()
        d = []
        for a, buf in enumerate(outs):
            for q in range(N_CHIPS):
                k = base + 4 * a + q
                d.append(("out", _remote(buf.at[2 * q + c], buf.at[2 * q + c], send, recv, k, (x, y, 1 - c))))
                d.append(("arrival", _remote(buf.at[2 * q + c], buf.at[2 * q + 1 - c], send, recv, k, (x, y, 1 - c))))
        return d
    shapes = [jax.ShapeDtypeStruct(b.shape, b.dtype) for b in bufs]
    return _Ride(list(bufs), shapes, {i: i for i in range(len(bufs))}, 4 * len(bufs), plan)


def _scatter_sibling(srcs):
    def plan(ins, outs, send, recv, base):
        x, y, c = _coords()
        d = []
        for a, (src, sib) in enumerate(zip(ins, outs)):
            for q in range(N_CHIPS):
                k = base + 4 * a + q
                d.append(("out", _remote(src.at[2 * q + 1 - c], sib.at[q], send, recv, k, (x, y, 1 - c))))
                d.append(("arrival", _remote(src.at[2 * q + 1 - c], sib.at[q], send, recv, k, (x, y, 1 - c))))
        return d
    shapes = [jax.ShapeDtypeStruct((N_CHIPS,) + s.shape[1:], s.dtype) for s in srcs]
    return _Ride(list(srcs), shapes, {}, 4 * len(srcs), plan)


def _scatter_chips(chips, rows=None, into=None):
    n = len(chips)

    def plan(ins, outs, send, recv, base):
        x, y, c = _coords()
        mine = 2 * x + y
        cut = (lambda ref: ref) if rows is None else (lambda ref: ref.at[pl.ds(rows[0], rows[1])])
        d = []
        for a, (chip, parts) in enumerate(zip(ins[:n], outs)):
            k = base + 4 * a
            d.append(("local", _local(cut(chip.at[mine]), cut(parts.at[mine]), send.at[k + 3])))
            for j, (px, py) in enumerate(_other_chips(x, y)):
                q = 2 * px + py
                d.append(("out", _remote(cut(chip.at[q]), cut(parts.at[mine]), send, recv, k + j, (px, py, c))))
                d.append(("arrival", _remote(cut(chip.at[q]), cut(parts.at[q]), send, recv, k + j, (px, py, c))))
        return d
    shapes = [jax.ShapeDtypeStruct(s.shape, s.dtype) for s in chips]
    if into is None:
        return _Ride(list(chips), shapes, {}, 4 * n, plan)
    return _Ride(list(chips) + list(into), shapes, {n + a: a for a in range(n)}, 4 * n, plan)


def _gather_all(srcs):
    def plan(ins, outs, send, recv, base):
        x, y, c = _coords()
        me = 4 * x + 2 * y + c
        d = []
        for a, (src, dst) in enumerate(zip(ins, outs)):
            k = base + N_DEV * a
            d.append(("local", _local(src, dst.at[me], send.at[k])))
            for j in range(1, N_DEV):
                px = 1 - x if (j >> 2) & 1 else x
                py = 1 - y if (j >> 1) & 1 else y
                pc = 1 - c if j & 1 else c
                d.append(("out", _remote(src, dst.at[me], send, recv, k + j, (px, py, pc))))
                d.append(("arrival", _remote(src, dst.at[4 * px + 2 * py + pc], send, recv, k + j, (px, py, pc))))
        return d
    shapes = [jax.ShapeDtypeStruct((N_DEV,) + s.shape, s.dtype) for s in srcs]
    return _Ride(list(srcs), shapes, {}, N_DEV * len(srcs), plan)


def _merge(*rides):
    ins, outs, aliases, parts, nsem = [], [], {}, [], 0
    for r in rides:
        parts.append((len(ins), len(outs), nsem, r))
        aliases.update({len(ins) + i: len(outs) + j for i, j in r.aliases.items()})
        ins += r.ins
        outs += r.out_shapes
        nsem += r.nsem

    def plan(i, o, send, recv, base):
        d = []
        for i0, o0, s0, r in parts:
            d += r.plan(i[i0:i0 + len(r.ins)], o[o0:o0 + len(r.out_shapes)], send, recv, base + s0)
        return d
    return _Ride(ins, outs, aliases, nsem, plan)


def _comm(ride, name):
    n_in, n_out = len(ride.ins), len(ride.out_shapes)

    def body(*refs):
        plan = ride.plan(refs[:n_in], refs[n_in:n_in + n_out], refs[-2], refs[-1], 0)
        _start_all(plan)
        if ride.finish is None:
            _wait_all(plan)
        else:
            ride.finish(refs[:n_in], refs[n_in:n_in + n_out], refs[-2], refs[-1])

    return pl.pallas_call(
        body, name=name, in_specs=[_ANY] * n_in, out_specs=[_ANY] * n_out, out_shape=ride.out_shapes,
        scratch_shapes=[pltpu.SemaphoreType.DMA((ride.nsem,)), pltpu.SemaphoreType.DMA((ride.nsem,))],
        input_output_aliases=dict(ride.aliases),
        compiler_params=pltpu.CompilerParams(has_side_effects=True),
    )(*ride.ins)


def _gather_relayed(big, small):
    srcs = list(big) + list(small)
    n, per = len(srcs), 10

    def places(ins, outs):
        x, y, c = _coords()
        slot = lambda dev: 4 * dev[0] + 2 * dev[1] + dev[2]
        devs = dict(me=(x, y, c), nx=(1 - x, y, c), ny=(x, 1 - y, c), dg=(1 - x, 1 - y, c), sib=(x, y, 1 - c))
        return devs, slot

    def first(ins, outs, send, recv, base):
        devs, slot = places(ins, outs)
        d = []
        for a, (src, dst) in enumerate(zip(ins, outs)):
            k, mine = base + per * a, dst.at[slot(devs["me"])]
            d.append(("local", _local(src, mine, send.at[k + 9])))
            d.append(("out", _remote(src, mine, send, recv, k, devs["nx"])))
            d.append(("out", _remote(src, mine, send, recv, k + 1, devs["ny"])))
            d.append(("out", _remote(src, mine, send, recv, k + 4, devs["sib"])))
            if a >= len(big):
                d.append(("out", _remote(src, mine, send, recv, k + 2, devs["dg"])))
        return d

    def finish(ins, outs, send, recv):
        devs, slot = places(ins, outs)
        sib_of = lambda dev: (dev[0], dev[1], 1 - dev[2])
        later = []

        def go(copy):
            copy.start()
            later.append(copy)

        for a, (src, dst) in enumerate(zip(ins, outs)):
            k, rows = per * a, src.shape[0]
            relay = a < len(big)
            half = rows // 2
            lo = lambda dev: dst.at[slot(dev)].at[pl.ds(0, half)]
            hi = lambda dev: dst.at[slot(dev)].at[pl.ds(half, rows - half)]
            whole = lambda dev: dst.at[slot(dev)]
            _remote(src, whole(devs["nx"]), send, recv, k, devs["nx"])().wait_recv()
            if relay:
                go(_remote(lo(devs["nx"]), lo(devs["nx"]), send, recv, k + 2, devs["ny"])())
            go(_remote(whole(devs["nx"]), whole(devs["nx"]), send, recv, k + 5, devs["sib"])())
            _remote(src, whole(devs["ny"]), send, recv, k + 1, devs["ny"])().wait_recv()
            if relay:
                go(_remote(hi(devs["ny"]), hi(devs["ny"]), send, recv, k + 3, devs["nx"])())
            go(_remote(whole(devs["ny"]), whole(devs["ny"]), send, recv, k + 6, devs["sib"])())
            if relay:
                _remote(lo(devs["dg"]), lo(devs["dg"]), send, recv, k + 2, devs["ny"])().wait_recv()
                go(_remote(lo(devs["dg"]), lo(devs["dg"]), send, recv, k + 7, devs["sib"])())
                _remote(hi(devs["dg"]), hi(devs["dg"]), send, recv, k + 3, devs["nx"])().wait_recv()
                go(_remote(hi(devs["dg"]), hi(devs["dg"]), send, recv, k + 8, devs["sib"])())
            else:
                _remote(src, whole(devs["dg"]), send, recv, k + 2, devs["dg"])().wait_recv()
                go(_remote(whole(devs["dg"]), whole(devs["dg"]), send, recv, k + 7, devs["sib"])())
        for a, (src, dst) in enumerate(zip(ins, outs)):
            k, rows = per * a, src.shape[0]
            half = rows // 2
            for j, dev in ((4, devs["me"]), (5, devs["nx"]), (6, devs["ny"])):
                theirs = dst.at[slot(sib_of(dev))]
                _remote(theirs, theirs, send, recv, k + j, devs["sib"])().wait_recv()
            far = dst.at[slot(sib_of(devs["dg"]))]
            if a < len(big):
                _remote(far.at[pl.ds(0, half)], far.at[pl.ds(0, half)], send, recv, k + 7, devs["sib"])().wait_recv()
                _remote(far.at[pl.ds(half, rows - half)], far.at[pl.ds(half, rows - half)], send, recv, k + 8,
                        devs["sib"])().wait_recv()
            else:
                _remote(far, far, send, recv, k + 7, devs["sib"])().wait_recv()
        for kind, make in first(ins, outs, send, recv, 0):
            (make().wait if kind == "local" else make().wait_send)()
        for copy in later:
            copy.wait_send()

    shapes = [jax.ShapeDtypeStruct((N_DEV,) + s.shape, s.dtype) for s in srcs]
    return _Ride(srcs, shapes, {}, per * n, first, finish)


def _call(body, *, name, grid, in_specs, out_specs, out_shape, args, sem, scratch=(), ride=None, base=None):
    params = pltpu.CompilerParams(dimension_semantics=sem, vmem_limit_bytes=VMEM_LIMIT)
    own_aliases = {}
    if base is not None:
        inner, n_host = body, len(args)
        body = lambda *refs: inner(*refs[:n_host], *refs[n_host + 1:])
        own_aliases[n_host] = base[1]
        args, in_specs = tuple(args) + (base[0],), list(in_specs) + [_ANY]
    if ride is None:
        res = pl.pallas_call(body, name=name, grid=grid, in_specs=in_specs, out_specs=out_specs,
                             out_shape=out_shape, scratch_shapes=list(scratch), input_output_aliases=own_aliases,
                             compiler_params=params)(*args)
        return list(res), []
    n_in, n_out, n_scr = len(args), len(out_shape), len(scratch)
    r_in, r_out = len(ride.ins), len(ride.out_shapes)

    def hosted(*refs):
        h_in, rin = refs[:n_in], refs[n_in:n_in + r_in]
        o0 = n_in + r_in
        h_out, rout = refs[o0:o0 + n_out], refs[o0 + n_out:o0 + n_out + r_out]
        s0 = o0 + n_out + r_out
        h_scr, send, recv = refs[s0:s0 + n_scr], refs[s0 + n_scr], refs[s0 + n_scr + 1]
        ids = [pl.program_id(i) for i in range(len(grid))]
        first = functools.reduce(lambda p, q: p & q, [i == 0 for i in ids])
        last = functools.reduce(lambda p, q: p & q, [i == n - 1 for i, n in zip(ids, grid)])

        @pl.when(first)
        def _():
            _start_all(ride.plan(rin, rout, send, recv, 0))

        body(*h_in, *h_out, *h_scr)

        @pl.when(last)
        def _():
            if ride.finish is None:
                _wait_all(ride.plan(rin, rout, send, recv, 0))
            else:
                ride.finish(rin, rout, send, recv)

    res = pl.pallas_call(
        hosted, name=name, grid=grid, in_specs=list(in_specs) + [_ANY] * r_in,
        out_specs=list(out_specs) + [_ANY] * r_out, out_shape=list(out_shape) + list(ride.out_shapes),
        scratch_shapes=list(scratch) + [pltpu.SemaphoreType.DMA((ride.nsem,)), pltpu.SemaphoreType.DMA((ride.nsem,))],
        input_output_aliases={**own_aliases, **{n_in + i: n_out + j for i, j in ride.aliases.items()}},
        compiler_params=params,
    )(*args, *ride.ins)
    return list(res[:n_out]), list(res[n_out:])


def _matmul(a, b, *, ta=False, tb=False, out_dtype=BF16, add=None, post=None, name, ride=None, tn_max=MM_TILE_MN):
    m = a.shape[1] if ta else a.shape[0]
    k = a.shape[0] if ta else a.shape[1]
    n = b.shape[0] if tb else b.shape[1]
    assert k == (b.shape[1] if tb else b.shape[0])
    tm, tn, tk = _tile(m, MM_TILE_MN, LANES), _tile(n, tn_max, LANES), _tile(k, MM_TILE_K, LANES)
    nk = k // tk
    dims = (((0 if ta else 1,), (1 if tb else 0,)), ((), ()))
    single = post is None
    if add is not None:
        post = (lambda r, t: (r + t,), [add], [out_dtype])
    elif post is None:
        post = (lambda r: (r,), [], [out_dtype])
    post_fn, extras, out_dtypes = post
    n_ex, n_o = len(extras), len(out_dtypes)

    def body(*refs):
        a_ref, b_ref = refs[:2]
        ex_refs, o_refs = refs[2:2 + n_ex], refs[2 + n_ex:2 + n_ex + n_o]

        def finish(r):
            for o_ref, v in zip(o_refs, post_fn(r, *[e[...].astype(F32) for e in ex_refs])):
                o_ref[...] = v.astype(o_ref.dtype)

        part = lax.dot_general(a_ref[...].astype(BF16), b_ref[...].astype(BF16), dims, preferred_element_type=F32)
        if nk == 1:
            finish(part)
            return
        acc = refs[-1]
        kk = pl.program_id(2)

        @pl.when(kk == 0)
        def _():
            acc[...] = part

        @pl.when((kk > 0) & (kk < nk - 1))
        def _():
            acc[...] += part

        @pl.when(kk == nk - 1)
        def _():
            finish(acc[...] + part)

    a_spec = (pl.BlockSpec((tk, tm), lambda i, j, kk: (kk, i)) if ta
              else pl.BlockSpec((tm, tk), lambda i, j, kk: (i, kk)))
    b_spec = (pl.BlockSpec((tn, tk), lambda i, j, kk: (j, kk)) if tb
              else pl.BlockSpec((tk, tn), lambda i, j, kk: (kk, j)))
    o_spec = pl.BlockSpec((tm, tn), lambda i, j, kk: (i, j))
    outs, rides = _call(
        body, name=name, grid=(m // tm, n // tn, nk),
        in_specs=[a_spec, b_spec] + [o_spec] * n_ex, out_specs=[o_spec] * n_o,
        out_shape=[jax.ShapeDtypeStruct((m, n), dt) for dt in out_dtypes], args=(a, b, *extras),
        scratch=[pltpu.VMEM((tm, tn), F32)] if nk > 1 else [], sem=("parallel", "parallel", "arbitrary"), ride=ride)
    return (outs[0] if single else outs), rides


def _rows_call(fn, *, rows, tr, row_ins, full_ins, row_outs, acc_outs, name, ride=None):
    nr, nf, no, na = len(row_ins), len(full_ins), len(row_outs), len(acc_outs)

    def body(*refs):
        vals = [r[...] for r in refs[:nr + nf]]
        outs, accs = fn(*vals)
        for r, v in zip(refs[nr + nf:nr + nf + no], outs):
            r[...] = v.astype(r.dtype)
        if na:
            @pl.when(pl.program_id(0) == 0)
            def _():
                for r in refs[nr + nf + no:]:
                    r[...] = jnp.zeros_like(r)
            for r, v in zip(refs[nr + nf + no:], accs):
                r[...] += v

    in_specs = [pl.BlockSpec((tr, w), functools.partial(lambda cb, i: (i, cb), cb)) for _, w, cb in row_ins]
    in_specs += [pl.BlockSpec(f.shape, lambda i: (0, 0)) for f in full_ins]
    out_specs = [pl.BlockSpec((tr, o[0]), lambda i: (i, 0)) for o in row_outs]
    out_specs += [pl.BlockSpec(s, lambda i: (0, 0)) for s in acc_outs]
    out_shape = [jax.ShapeDtypeStruct((rows, o[-1] if len(o) == 3 else o[0]), o[1]) for o in row_outs]
    out_shape += [jax.ShapeDtypeStruct(s, F32) for s in acc_outs]
    return _call(body, name=name, grid=(rows // tr,), in_specs=in_specs, out_specs=out_specs, out_shape=out_shape,
                 args=tuple(a for a, _, _ in row_ins) + tuple(full_ins), sem=("arbitrary",), ride=ride)


def _cols_call(fn, *, rows, cols, cw, col_ins, par_ins, col_outs, par_outs, name, ride=None, into=None):
    nc, npar = len(col_ins), len(par_ins)

    def body(*refs):
        vals = [r[...] for r in refs[:nc + npar]]
        outs, pouts = fn(*vals)
        for r, v in zip(refs[nc + npar:], tuple(outs) + tuple(pouts)):
            r[...] = v.astype(r.dtype)

    in_specs = [pl.BlockSpec((rows, cw), functools.partial(lambda off, j: (0, off + j), off)) for _, off in col_ins]
    in_specs += [pl.BlockSpec((p.shape[0], cw), functools.partial(lambda off, j: (0, off + j), off))
                 for p, off in par_ins]
    out_specs = [pl.BlockSpec((rows, cw), lambda j: (0, j)) for _ in col_outs]
    out_specs += [pl.BlockSpec((k, cw), lambda j: (0, j)) for k in par_outs]
    out_shape = [jax.ShapeDtypeStruct((rows, cols), dt) for dt in col_outs]
    out_shape += [jax.ShapeDtypeStruct((k, cols), F32) for k in par_outs]
    if into is not None:
        out_specs[0] = pl.BlockSpec((rows, cw), lambda j: (0, into[1] + j))
        out_shape[0] = jax.ShapeDtypeStruct(into[0].shape, into[0].dtype)
    return _call(body, name=name, grid=(cols // cw,), in_specs=in_specs, out_specs=out_specs, out_shape=out_shape,
                 args=tuple(a for a, _ in col_ins) + tuple(p for p, _ in par_ins), sem=("arbitrary",), ride=ride,
                 base=None if into is None else (into[0], 0))


def _sigmoid(v):
    return 1.0 / (1.0 + jnp.exp(-v))


def _softplus(v):
    return jnp.maximum(v, 0.0) + jnp.log(1.0 + jnp.exp(-jnp.abs(v)))


def _rms(v, g):
    return v * lax.rsqrt(jnp.mean(v * v, axis=-1, keepdims=True) + EPS) * g


def _shift_down(v, s, row):
    return jnp.where(row >= s, pltpu.roll(v, s, 0), 0.0)


def _shift_up(v, s, row):
    n = v.shape[0]
    return jnp.where(row < n - s, pltpu.roll(v, n - s, 0), 0.0)


def _causal_conv(u, w, row):
    k_taps = w.shape[0]
    acc = u * w[k_taps - 1:k_taps, :]
    for k in range(k_taps - 1):
        acc = acc + _shift_down(u, k_taps - 1 - k, row) * w[k:k + 1, :]
    return acc


def _causal_conv_bwd(u, dy, w, row):
    k_taps = w.shape[0]
    tap = lax.broadcasted_iota(jnp.int32, w.shape, 0)
    du = dy * w[k_taps - 1:k_taps, :]
    dw = jnp.where(tap == k_taps - 1, jnp.sum(dy * u, axis=0, keepdims=True), 0.0)
    for k in range(k_taps - 1):
        s = k_taps - 1 - k
        du = du + _shift_up(dy, s, row) * w[k:k + 1, :]
        dw = dw + jnp.where(tap == k, jnp.sum(dy * _shift_down(u, s, row), axis=0, keepdims=True), 0.0)
    return du, dw


def _conv_silu_fwd(u, w, b):
    u = u.astype(F32)
    row = lax.broadcasted_iota(jnp.int32, u.shape, 0)
    pre = _causal_conv(u, w, row) + b
    return (pre * _sigmoid(pre),), ()


def _conv_silu_bwd(u, dy, w, b):
    u = u.astype(F32)
    dy = dy.astype(F32)
    row = lax.broadcasted_iota(jnp.int32, u.shape, 0)
    pre = _causal_conv(u, w, row) + b
    s = _sigmoid(pre)
    dpre = dy * (s * (1.0 + pre * (1.0 - s)))
    du, dw = _causal_conv_bwd(u, dpre, w, row)
    return (du,), (dw, jnp.sum(dpre, axis=0, keepdims=True))


def _shortconv_fwd(gb, gc, u, w):
    gb, gc, u = gb.astype(F32), gc.astype(F32), u.astype(F32)
    row = lax.broadcasted_iota(jnp.int32, u.shape, 0)
    return (gb * _causal_conv(gc * u, w, row),), ()


def _shortconv_bwd(gb, gc, u, dy, w):
    gb, gc, u, dy = gb.astype(F32), gc.astype(F32), u.astype(F32), dy.astype(F32)
    row = lax.broadcasted_iota(jnp.int32, u.shape, 0)
    v = gc * u
    dgb = dy * _causal_conv(v, w, row)
    dv, dw = _causal_conv_bwd(v, dy * gb, w, row)
    return (dgb, dv * u, dv * gc), (dw,)


def _split3(v):
    hi = v.astype(BF16)
    r1 = v - hi.astype(F32)
    mid = r1.astype(BF16)
    lo = (r1 - mid.astype(F32)).astype(BF16)
    return hi, mid, lo


def _exact_dot(v, m01, dims, v_is_lhs):
    def one(p):
        return (lax.dot_general(p, m01, dims, preferred_element_type=F32) if v_is_lhs
                else lax.dot_general(m01, p, dims, preferred_element_type=F32))
    hi, mid, lo = _split3(v)
    return (one(lo) + one(mid)) + one(hi)


_NN = (((1,), (0,)), ((), ()))
_NT = (((1,), (1,)), ((), ()))
_TN = (((0,), (0,)), ((), ()))


@functools.partial(jax.custom_vjp, nondiff_argnums=(2,))
def _bdot(a, b, dims):
    return lax.dot_general(a.astype(BF16), b.astype(BF16), dims, preferred_element_type=F32)


def _bdot_fwd(a, b, dims):
    return _bdot(a, b, dims), (a, b)


def _bdot_bwd(dims, res, g):
    a, b = res
    ab, bb, gb = a.astype(BF16), b.astype(BF16), g.astype(BF16)
    dot = lambda p, q, d: lax.dot_general(p, q, d, preferred_element_type=F32)
    if dims == _NN:
        da, db = dot(gb, bb, _NT), dot(ab, gb, _TN)
    elif dims == _NT:
        da, db = dot(gb, bb, _NN), dot(gb, ab, _TN)
    else:
        da, db = dot(bb, gb, _NT), dot(ab, gb, _NN)
    return da.astype(a.dtype), db.astype(b.dtype)


_bdot.defvjp(_bdot_fwd, _bdot_bwd)


@jax.custom_vjp
def _cumsum_rows(tril, v):
    return _exact_dot(v, tril, _NN, False)


def _cumsum_rows_fwd(tril, v):
    return _cumsum_rows(tril, v), tril


def _cumsum_rows_bwd(tril, ct):
    return None, _exact_dot(ct, tril, _TN, False)


_cumsum_rows.defvjp(_cumsum_rows_fwd, _cumsum_rows_bwd)


@jax.custom_vjp
def _cumsum_lanes(tril, v):
    return _exact_dot(v, tril, _NT, True)


def _cumsum_lanes_fwd(tril, v):
    return _cumsum_lanes(tril, v), tril


def _cumsum_lanes_bwd(tril, ct):
    return None, _exact_dot(ct, tril, _NN, True)


_cumsum_lanes.defvjp(_cumsum_lanes_fwd, _cumsum_lanes_bwd)


@jax.custom_vjp
def _expand(e01, v):
    return _exact_dot(v, e01, _NN, True)


def _expand_fwd(e01, v):
    return _expand(e01, v), e01


def _expand_bwd(e01, ct):
    return None, _exact_dot(ct, e01, _NT, True)


_expand.defvjp(_expand_fwd, _expand_bwd)


def _causal_mask(n):
    li = lax.broadcasted_iota(jnp.int32, (n, n), 0)
    si = lax.broadcasted_iota(jnp.int32, (n, n), 1)
    return si <= li


def _dt_prep(dtc, dtr, bias_r, bias_c, alog_r, alog_c):
    dt_c = _softplus(dtc + bias_r)
    dt_r = _softplus(dtr + bias_c)
    tril = jnp.where(_causal_mask(dtc.shape[0]), 1.0, 0.0).astype(BF16)
    cs_c = _cumsum_rows(tril, dt_c * (-jnp.exp(alog_r)))
    cs_r = _cumsum_lanes(tril, dt_r * (-jnp.exp(alog_c)))
    return dt_c, cs_c, cs_r


def _ssd_chunk(r_heads, xs, bg, cg, dt_c, cs_c, cs_rg, e01, dskip_e, hp):
    l_len, rp = xs.shape
    p = rp // r_heads
    causal = _causal_mask(l_len)
    lane_head = lax.broadcasted_iota(jnp.int32, (1, rp), 1) // p
    dt_e = _expand(e01, dt_c)
    cs_e = _expand(e01, cs_c)
    cl_e = cs_e[l_len - 1:l_len, :]
    x = xs * dt_e
    cb = _bdot(cg, bg, _NT)
    ms, xm = [], []
    for r in range(r_heads):
        seg = cs_e[:, r * p:r * p + 1] - cs_rg[r:r + 1, :]
        decay = jnp.exp(jnp.where(causal, seg, -1e30))
        ms.append((cb * decay).astype(BF16))
        xm.append(jnp.where(lane_head == r, x, 0.0).astype(BF16))
    y_diag = _bdot(jnp.concatenate(ms, axis=1), jnp.concatenate(xm, axis=0), _NN)
    y_off = _bdot(cg, hp, _NN) * jnp.exp(cs_e)
    states = _bdot(bg, x * jnp.exp(cl_e - cs_e), _TN)
    h_next = hp * jnp.exp(cl_e) + states
    y = y_diag + y_off + dskip_e * xs
    return y, h_next


def _ssd_dt(dtc, dtr, small, cots=None):
    t_len, heads = dtc.shape[0], dtr.shape[0]
    nc = t_len // CHUNK
    col = pl.BlockSpec((CHUNK, LANES), lambda c: (c, 0))
    row = pl.BlockSpec((heads, CHUNK), lambda c: (0, c))
    full = [pl.BlockSpec(s.shape, lambda c: (0, 0)) for s in small]
    shapes = [jax.ShapeDtypeStruct((t_len, LANES), F32), jax.ShapeDtypeStruct((t_len, LANES), F32),
              jax.ShapeDtypeStruct((heads, t_len), F32)]
    if cots is None:
        def body(dtc_ref, dtr_ref, br, bc, ar, ac, dt_ref, csc_ref, csr_ref):
            dt_ref[...], csc_ref[...], csr_ref[...] = _dt_prep(dtc_ref[...], dtr_ref[...], br[...], bc[...],
                                                                ar[...], ac[...])
        return _call(body, name="ssd_dt", grid=(nc,), in_specs=[col, row] + full, out_specs=[col, col, row],
                     out_shape=shapes, args=(dtc, dtr, *small), sem=("parallel",))[0]

    g_dt, g_csc, g_csr, ddk, e01 = cots

    def body(dtc_ref, dtr_ref, br, bc, ar, ac, g_dt_ref, g_csc_ref, g_csr_ref, ddk_ref, e_ref,
             ddtc_ref, ddtr_ref, *dsmall):
        _, vjp = jax.vjp(_dt_prep, dtc_ref[...], dtr_ref[...], br[...], bc[...], ar[...], ac[...])
        grads = vjp((g_dt_ref[...], g_csc_ref[...], g_csr_ref[...]))
        ddtc_ref[...], ddtr_ref[...] = grads[0], grads[1]
        ddk8 = jnp.broadcast_to(ddk_ref[...], (8, ddk_ref.shape[1]))
        dskip = _exact_dot(ddk8, e_ref[...], _NT, True)[0:1, :]

        @pl.when(pl.program_id(0) == 0)
        def _():
            for r in dsmall:
                r[...] = jnp.zeros_like(r)

        for r, gr in zip(dsmall, tuple(grads[2:]) + (dskip,)):
            r[...] += gr

    acc = list(small) + [small[0]]
    return _call(body, name="d_ssd_dt", grid=(nc,),
                 in_specs=[col, row] + full + [col, col, row, pl.BlockSpec((None, 1, e01.shape[1]), lambda c: (c, 0, 0)),
                                               pl.BlockSpec(e01.shape, lambda c: (0, 0))],
                 out_specs=[col, row] + [pl.BlockSpec(s.shape, lambda c: (0, 0)) for s in acc],
                 out_shape=[shapes[0], shapes[2]] + [jax.ShapeDtypeStruct(s.shape, F32) for s in acc],
                 args=(dtc, dtr, *small, g_dt, g_csc, g_csr, ddk, e01), sem=("arbitrary",))[0]


def _ssd_specs(t_len, d_ssm, r_heads, reverse):
    rp = r_heads * HEADDIM
    nc = t_len // CHUNK
    per = next(p for p in (SSD_CHUNKS_PER_STEP, 2, 1) if nc % p == 0)
    ns, rows, gs = nc // per, per * CHUNK, SSD_GROUPS_PER_STEP
    cidx = (lambda c: ns - 1 - c) if reverse else (lambda c: c)
    b_off = d_ssm // (N_STATE * gs)
    specs = dict(
        xs=pl.BlockSpec((rows, gs * rp), lambda c, g: (cidx(c), g)),
        b=pl.BlockSpec((rows, gs * N_STATE), lambda c, g: (cidx(c), b_off + g)),
        c=pl.BlockSpec((rows, gs * N_STATE), lambda c, g: (cidx(c), b_off + N_GROUPS // gs + g)),
        grad_bc=pl.BlockSpec((rows, gs * N_STATE), lambda c, g: (cidx(c), g)),
        col=pl.BlockSpec((rows, LANES), lambda c, g: (cidx(c), 0)),
        csr=pl.BlockSpec((gs, r_heads, rows), lambda c, g: (g, 0, cidx(c))),
        e01=pl.BlockSpec((LANES, gs * rp), lambda c, g: (0, g)),
        dskip=pl.BlockSpec((1, gs * rp), lambda c, g: (0, g)),
        hprev=pl.BlockSpec((per, gs, N_STATE, rp), lambda c, g: (cidx(c), g, 0, 0)),
        ddk=pl.BlockSpec((per, 1, gs * rp), lambda c, g: (cidx(c), 0, g)),
    )
    return specs, nc, ns, per, rp


def _ssd_fwd(xbc, dt_c, cs_c, cs_r3, e01, dskip_e, *, d_ssm, r_heads, ride=None):
    t_len = xbc.shape[0]
    sp, nc, ns, per, rp = _ssd_specs(t_len, d_ssm, r_heads, False)

    def body(xs_ref, b_ref, c_ref, dt_ref, csc_ref, csr_ref, e_ref, dk_ref, y_ref, hprev_ref, h_ref):
        c, gp = pl.program_id(0), pl.program_id(1)
        groups = [gp * SSD_GROUPS_PER_STEP + gi for gi in range(SSD_GROUPS_PER_STEP)]

        @pl.when(c == 0)
        def _():
            for g in groups:
                h_ref[g] = jnp.zeros((N_STATE, rp), F32)

        hp = [h_ref[g] for g in groups]
        for s in range(per):
            r = pl.ds(s * CHUNK, CHUNK)
            for gi in range(SSD_GROUPS_PER_STEP):
                cols, bc = pl.ds(gi * rp, rp), pl.ds(gi * N_STATE, N_STATE)
                hprev_ref[s, gi] = hp[gi]
                y, hp[gi] = _ssd_chunk(r_heads, xs_ref[r, cols].astype(F32), b_ref[r, bc].astype(F32),
                                       c_ref[r, bc].astype(F32), dt_ref[r, :], csc_ref[r, :], csr_ref[gi, :, r],
                                       e_ref[:, cols], dk_ref[:, cols], hp[gi])
                y_ref[r, cols] = y
        for gi, g in enumerate(groups):
            h_ref[g] = hp[gi]

    return _call(
        body, name="ssd_fwd", grid=(ns, N_GROUPS // SSD_GROUPS_PER_STEP),
        in_specs=[sp["xs"], sp["b"], sp["c"], sp["col"], sp["col"], sp["csr"], sp["e01"], sp["dskip"]],
        out_specs=[sp["xs"], sp["hprev"]],
        out_shape=[jax.ShapeDtypeStruct((t_len, d_ssm), F32),
                   jax.ShapeDtypeStruct((nc, N_GROUPS, N_STATE, rp), F32)],
        args=(xbc, xbc, xbc, dt_c, cs_c, cs_r3, e01, dskip_e), scratch=[pltpu.VMEM((N_GROUPS, N_STATE, rp), F32)],
        sem=("arbitrary", "arbitrary"), ride=ride)


def _ssd_bwd(xbc, dt_c, cs_c, cs_r3, e01, dskip_e, hprev, dy, *, d_ssm, r_heads, ride=None):
    t_len = xbc.shape[0]
    sp, nc, ns, per, rp = _ssd_specs(t_len, d_ssm, r_heads, True)

    def body(xs_ref, b_ref, c_ref, dt_ref, csc_ref, csr_ref, e_ref, dk_ref, hprev_ref, dy_ref,
             dxs_ref, db_ref, dc_ref, ddt_ref, dcsc_ref, dcsr_ref, ddk_ref, dh_ref):
        c, gp = pl.program_id(0), pl.program_id(1)
        groups = [gp * SSD_GROUPS_PER_STEP + gi for gi in range(SSD_GROUPS_PER_STEP)]

        @pl.when(gp == 0)
        def _():
            ddt_ref[...] = jnp.zeros_like(ddt_ref)
            dcsc_ref[...] = jnp.zeros_like(dcsc_ref)

        @pl.when(c == 0)
        def _():
            for g in groups:
                dh_ref[g] = jnp.zeros((N_STATE, rp), F32)

        dh = [dh_ref[g] for g in groups]
        for s in reversed(range(per)):
            r = pl.ds(s * CHUNK, CHUNK)
            ddt_sum, dcsc_sum = ddt_ref[r, :], dcsc_ref[r, :]
            for gi in range(SSD_GROUPS_PER_STEP):
                cols, bc = pl.ds(gi * rp, rp), pl.ds(gi * N_STATE, N_STATE)
                e01 = e_ref[:, cols]
                fn = lambda xs, bg, cg, dt, csc, csr, dk, hp: _ssd_chunk(r_heads, xs, bg, cg, dt, csc, csr, e01, dk, hp)
                _, vjp = jax.vjp(fn, xs_ref[r, cols].astype(F32), b_ref[r, bc].astype(F32), c_ref[r, bc].astype(F32),
                                 dt_ref[r, :], csc_ref[r, :], csr_ref[gi, :, r], dk_ref[:, cols], hprev_ref[s, gi])
                dxs, dbg, dcg, ddt, dcsc, dcsr, ddk, dh[gi] = vjp((dy_ref[r, cols], dh[gi]))
                dxs_ref[r, cols] = dxs.astype(dxs_ref.dtype)
                db_ref[r, bc] = dbg.astype(db_ref.dtype)
                dc_ref[r, bc] = dcg.astype(dc_ref.dtype)
                ddt_sum, dcsc_sum = ddt_sum + ddt, dcsc_sum + dcsc
                dcsr_ref[gi, :, r] = dcsr
                ddk_ref[s, :, cols] = ddk
            ddt_ref[r, :], dcsc_ref[r, :] = ddt_sum, dcsc_sum
        for gi, g in enumerate(groups):
            dh_ref[g] = dh[gi]

    n_bc = N_GROUPS * N_STATE
    return _call(
        body, name="ssd_bwd", grid=(ns, N_GROUPS // SSD_GROUPS_PER_STEP),
        in_specs=[sp["xs"], sp["b"], sp["c"], sp["col"], sp["col"], sp["csr"], sp["e01"], sp["dskip"], sp["hprev"],
                  sp["xs"]],
        out_specs=[sp["xs"], sp["grad_bc"], sp["grad_bc"], sp["col"], sp["col"], sp["csr"], sp["ddk"]],
        out_shape=[jax.ShapeDtypeStruct((t_len, d_ssm), BF16), jax.ShapeDtypeStruct((t_len, n_bc), BF16),
                   jax.ShapeDtypeStruct((t_len, n_bc), BF16), jax.ShapeDtypeStruct(dt_c.shape, F32),
                   jax.ShapeDtypeStruct(cs_c.shape, F32), jax.ShapeDtypeStruct(cs_r3.shape, F32),
                   jax.ShapeDtypeStruct((nc, 1, d_ssm), F32)],
        args=(xbc, xbc, xbc, dt_c, cs_c, cs_r3, e01, dskip_e, hprev, dy),
        scratch=[pltpu.VMEM((N_GROUPS, N_STATE, rp), F32)], sem=("arbitrary", "arbitrary"), ride=ride)


def _chip_sum(src, sib, *, name):
    rows, cols = src.shape[1:]
    tr = _tile(rows, 256, BF16_ROWS)
    core = lax.axis_index("c").astype(jnp.int32).reshape(1)

    def body(c_ref, a_ref, b_ref, o_ref):
        o_ref[...] = (a_ref[...].astype(F32) + b_ref[...].astype(F32)).astype(o_ref.dtype)

    grid_spec = pltpu.PrefetchScalarGridSpec(
        num_scalar_prefetch=1, grid=(N_CHIPS, rows // tr),
        in_specs=[pl.BlockSpec((None, tr, cols), lambda q, i, c_ref: (2 * q + c_ref[0], i, 0)),
                  pl.BlockSpec((None, tr, cols), lambda q, i, c_ref: (q, i, 0))],
        out_specs=pl.BlockSpec((None, tr, cols), lambda q, i, c_ref: (q, i, 0)))
    return pl.pallas_call(
        body, name=name, grid_spec=grid_spec, out_shape=jax.ShapeDtypeStruct(sib.shape, sib.dtype),
        compiler_params=pltpu.CompilerParams(dimension_semantics=("parallel", "parallel"), vmem_limit_bytes=VMEM_LIMIT),
    )(core, src, sib)


def _adamw(w, g, m, v):
    m = ADAM_B1 * m + (1.0 - ADAM_B1) * g
    v = ADAM_B2 * v + (1.0 - ADAM_B2) * (g * g)
    m_hat = m / (1.0 - ADAM_B1 ** ADAM_STEP)
    v_hat = v / (1.0 - ADAM_B2 ** ADAM_STEP)
    delta = -ADAM_LR * (m_hat / (jnp.sqrt(v_hat) + ADAM_EPS) + ADAM_WD * w)
    return delta, m, v


def _reduce_adamw(parts, w, m, v, *, name):
    n_parts = parts.shape[0]
    rows, cols = w.shape
    tr = _tile(rows, 128, BF16_ROWS)

    def body(p_ref, w_ref, m_ref, v_ref, g_ref, d_ref, mo_ref, vo_ref):
        g = p_ref[0].astype(F32)
        for k in range(1, n_parts):
            g = g + p_ref[k].astype(F32)
        delta, mn, vn = _adamw(w_ref[...], g, m_ref[...], v_ref[...])
        g_ref[...] = g
        d_ref[...] = delta
        mo_ref[...] = mn
        vo_ref[...] = vn

    spec = pl.BlockSpec((tr, cols), lambda i: (i, 0))
    outs, _ = _call(
        body, name=name, grid=(rows // tr,),
        in_specs=[pl.BlockSpec((n_parts, tr, cols), lambda i: (0, i, 0)), spec, spec, spec],
        out_specs=[spec] * 4, out_shape=[jax.ShapeDtypeStruct((rows, cols), F32)] * 4,
        args=(parts, w, m, v), sem=("parallel",))
    return outs


def _move_rows(src, src_row, name, extra=None, extra_row=None):
    rb, n_out, cols = ROW_BLOCK, len(src_row), src.shape[1]
    assert n_out % rb == 0 and src.shape[0] % rb == 0 and src.shape[0] // rb >= 3
    n_blocks, max_b0, seg_cap = n_out // rb, src.shape[0] // rb - 3, 4

    def segments(rows_of, lo):
        segs, r = [], 0
        while r < rb:
            if rows_of[r] < 0:
                r += 1
                continue
            e = r
            while e + 1 < rb and rows_of[e + 1] == rows_of[e] + 1:
                e += 1
            segs.append((r, e + 1, rows_of[r] - r - lo))
            r = e + 1
        assert len(segs) <= seg_cap
        return segs + [(0, 0, 0)] * (seg_cap - len(segs))

    table = []
    for j in range(n_blocks):
        rows_j = list(src_row[j * rb:(j + 1) * rb])
        valid = [v for v in rows_j if v >= 0]
        b0 = min(max((min(valid) // rb) if valid else 0, 0), max_b0)
        assert not valid or max(valid) < (b0 + 3) * rb
        row = [b0] + [v for seg in segments(rows_j, b0 * rb) for v in seg]
        extra_j = [] if extra is None else list(extra_row[j * rb:(j + 1) * rb])
        if extra is not None:
            row += [v for seg in segments(extra_j, 0) for v in seg]
        need_third = bool(valid) and max(valid) >= (b0 + 2) * rb
        third = b0 + 2 if need_third or not table else table[-1][-1]
        row += [int(need_third), int(any(v >= 0 for v in extra_j)), third]
        table.append(row)
    flag_third, flag_extra, col_third = len(table[0]) - 3, len(table[0]) - 2, len(table[0]) - 1
    table = jnp.asarray(table, jnp.int32)

    def select(tbl_ref, j, first, width, col0=0):
        r = lax.broadcasted_iota(jnp.int32, (rb, width), 0)
        c = lax.broadcasted_iota(jnp.int32, (rb, width), 1) + col0
        hit = jnp.zeros((rb, width), jnp.bool_)
        for s in range(seg_cap):
            lo, hi, off = (tbl_ref[j, first + 3 * s + i] for i in range(3))
            hit = hit | ((r >= lo) & (r < hi) & (c == r + off))
        return jnp.where(hit, 1.0, 0.0).astype(BF16)

    def body(tbl_ref, *refs):
        o_ref = refs[-1]
        j = pl.program_id(0)
        sel = select(tbl_ref, j, 1, 2 * rb)
        pick = lambda m, b: lax.dot_general(m, refs[b][...], _NN, preferred_element_type=F32)
        o_ref[...] = (pick(sel[:, :rb], 0) + pick(sel[:, rb:], 1)).astype(o_ref.dtype)

        @pl.when(tbl_ref[j, flag_third] == 1)
        def _():
            o_ref[...] = (o_ref[...].astype(F32) + pick(select(tbl_ref, j, 1, rb, 2 * rb), 2)).astype(o_ref.dtype)

        if extra is not None:
            @pl.when(tbl_ref[j, flag_extra] == 1)
            def _():
                more = lax.dot_general(select(tbl_ref, j, 1 + 3 * seg_cap, extra.shape[0]), refs[3][...], _NN,
                                       preferred_element_type=F32)
                o_ref[...] = (o_ref[...].astype(F32) + more).astype(o_ref.dtype)

    in_specs = [pl.BlockSpec((rb, cols), functools.partial(lambda b, j, tbl: (tbl[j, 0] + b, 0), b)) for b in range(2)]
    in_specs.append(pl.BlockSpec((rb, cols), lambda j, tbl: (tbl[j, col_third], 0)))
    args = [src, src, src]
    if extra is not None:
        in_specs.append(pl.BlockSpec(extra.shape, lambda j, tbl: (0, 0)))
        args.append(extra)
    grid_spec = pltpu.PrefetchScalarGridSpec(num_scalar_prefetch=1, grid=(n_blocks,), in_specs=in_specs,
                                             out_specs=pl.BlockSpec((rb, cols), lambda j, tbl: (j, 0)))
    return pl.pallas_call(
        body, name=name, grid_spec=grid_spec, out_shape=jax.ShapeDtypeStruct((n_out, cols), src.dtype),
        compiler_params=pltpu.CompilerParams(dimension_semantics=("parallel",), vmem_limit_bytes=VMEM_LIMIT),
    )(table, *args)


def _cols_of(g):
    return jnp.transpose(g, (1, 0, 2)).reshape(g.shape[1], -1)


def _pad_to(a, rows, cols):
    return jnp.pad(a, ((0, rows - a.shape[0]), (0, cols - a.shape[1])))


def kernel(x, norm_mix_g, w_in, ssm_conv_w, ssm_conv_b, ssm_dt_bias, ssm_A_log, ssm_D, ssm_norm_g, sc_conv_w, w_out, norm_ffn_g, w_gate, w_up, w_down, norm_final_g, loss_target, m_norm_mix_g, m_w_in, m_ssm_conv_w, m_ssm_conv_b, m_ssm_dt_bias, m_ssm_A_log, m_ssm_D, m_ssm_norm_g, m_sc_conv_w, m_w_out, m_norm_ffn_g, m_w_gate, m_w_up, m_w_down, m_norm_final_g, v_norm_mix_g, v_w_in, v_ssm_conv_w, v_ssm_conv_b, v_ssm_dt_bias, v_ssm_A_log, v_ssm_D, v_ssm_norm_g, v_sc_conv_w, v_w_out, v_norm_ffn_g, v_w_gate, v_w_up, v_w_down, v_norm_final_g):
    t_len, d = x.shape[1], x.shape[2]
    heads = d // HEADDIM
    r_heads = heads // N_GROUPS
    d_xbc = d + 2 * N_GROUPS * N_STATE
    ff_s = w_down.shape[1]
    ff = ff_s * N_DEV
    off_xbc, off_dt = d, d + d_xbc
    off_cb = off_dt + heads
    d_in = off_cb + 3 * d
    in_s = d_in // N_DEV
    in_p = -(-in_s // (2 * BF16_ROWS)) * (2 * BF16_ROWS)
    w_main = 4 * d + d_xbc
    me = 4 * lax.axis_index("x") + 2 * lax.axis_index("y") + lax.axis_index("c")

    x2 = x[0]
    target = loss_target[0]

    tpose = lambda a: jnp.transpose(a[0])
    win_s = _pad_to(tpose(w_in).astype(BF16), in_p, d)
    wg_s, wu_s = tpose(w_gate).astype(BF16), tpose(w_up).astype(BF16)
    wo_s, wd_s = w_out[0].astype(BF16), w_down[0].astype(BF16)
    small_w = jnp.concatenate([_pad_to(ssm_conv_w[0], K_SSM, d_xbc // N_DEV),
                               _pad_to(sc_conv_w[0], K_SC + 1, d_xbc // N_DEV)], axis=0)

    g1, g2, g3 = norm_mix_g, norm_ffn_g, norm_final_g.reshape(1, d)
    gs = ssm_norm_g
    small = [_pad_to(ssm_dt_bias, 1, LANES), ssm_dt_bias.reshape(heads, 1), _pad_to(ssm_A_log, 1, LANES),
             ssm_A_log.reshape(heads, 1)]
    e01 = (lax.broadcasted_iota(jnp.int32, (LANES, d), 1) // HEADDIM
           == lax.broadcasted_iota(jnp.int32, (LANES, d), 0)).astype(BF16)
    dskip_e = jnp.repeat(ssm_D, HEADDIM, axis=1)
    tr = _tile(t_len, 256, 8)
    tr_wide = _tile(t_len, 512, 8)
    tr_ff = _tile(t_len, 128, 8)
    cw = LANES
    slab = lambda col: col // cw

    (n1,), (gin, gsm) = _rows_call(lambda v, g: ((_rms(v, g),), ()), rows=t_len, tr=tr_wide, row_ins=[(x2, d, 0)],
                                   full_ins=[g1], row_outs=[(d, BF16)], acc_outs=[], name="norm_mix",
                                   ride=_gather_relayed([win_s], [small_w]))
    in_pieces = []
    for k in range(N_DEV):
        for a, b, dst, shift in ((0, off_dt, 0, 0), (off_dt, off_cb, 1, -off_dt), (off_cb, d_in, 0, -heads)):
            s, e = max(k * in_s, a), min((k + 1) * in_s, b)
            if s < e:
                in_pieces.append((k, s - k * in_s, e - s, dst, s + shift))
    ref_row = lambda t: t if t < off_dt else t + heads
    wtm = _move_rows(gin.reshape(N_DEV * in_p, d),
                     [(ref_row(t) // in_s) * in_p + ref_row(t) % in_s for t in range(w_main)], "place_w_in")
    wtdt = jnp.zeros((LANES, d), BF16)
    for k, r0, n, dst, d0 in in_pieces:
        if dst == 1:
            wtdt = lax.dynamic_update_slice(wtdt, gin[k, r0:r0 + n], (d0, 0))
    cw_ssm = _cols_of(gsm[:, :K_SSM, :])
    cw_sc = _cols_of(gsm[:, K_SSM:K_SSM + K_SC, :d // N_DEV])

    proj, (go_1, gg_1) = _matmul(n1, wtm, tb=True, out_dtype=BF16, name="proj_main",
                                 ride=_gather_chips([wo_s, wg_s]))
    dt_raw, _ = _matmul(n1, wtdt, tb=True, out_dtype=F32, name="proj_dt")
    dt_raw_t = jnp.transpose(dt_raw[:, :heads])
    (xbc,), (go, gg) = _cols_call(_conv_silu_fwd, rows=t_len, cols=d_xbc, cw=cw, col_ins=[(proj, slab(off_xbc))],
                                  par_ins=[(cw_ssm, 0), (ssm_conv_b, 0)], col_outs=[BF16], par_outs=[],
                                  name="ssm_conv", ride=_gather_sibling([go_1, gg_1]))
    dt_c, cs_c, cs_r = _ssd_dt(dt_raw, dt_raw_t, small)
    cs_r3 = cs_r.reshape(N_GROUPS, r_heads, t_len)
    up_cut = int(ff_s * W_UP_GATHER_SPLIT) // BF16_ROWS * BF16_ROWS
    down_cut = int(ff_s * W_DOWN_GATHER_SPLIT) // BF16_ROWS * BF16_ROWS
    half_cut = ff_s // 2 // BF16_ROWS * BF16_ROWS
    (y_ssd, hprev), (gu_1,) = _ssd_fwd(xbc, dt_c, cs_c, cs_r3, e01, dskip_e, d_ssm=d, r_heads=r_heads,
                                       ride=_gather_chips([wu_s], rows=(0, up_cut)))

    def gate_norm(y, z, g):
        z = z.astype(F32)
        return _rms(y * (z * _sigmoid(z)), g)

    (y_mix,), _ = _rows_call(lambda y, z, g: ((gate_norm(y, z, g),), ()), rows=t_len, tr=tr_wide,
                             row_ins=[(y_ssd, d, 0), (proj, d, 0)], full_ins=[gs], row_outs=[(d, BF16, 2 * d)],
                             acc_outs=[], name="ssm_gate_norm")
    wgt, wo = gg.reshape(ff, d), go.reshape(2 * d, d)
    sc0 = slab(d + d_xbc)
    (y_mix,), _ = _cols_call(_shortconv_fwd, rows=t_len, cols=d, cw=cw,
                             col_ins=[(proj, sc0), (proj, sc0 + slab(d)), (proj, sc0 + 2 * slab(d))],
                             par_ins=[(cw_sc, 0)], col_outs=[BF16], par_outs=[], name="shortconv",
                             into=(y_mix, slab(d)))
    h1, (gu_1, gd_1) = _matmul(y_mix, wo, out_dtype=F32, add=x2, name="out_proj", ride=_merge(
        _gather_chips([wu_s], rows=(up_cut, ff_s - up_cut), into=[gu_1]), _gather_chips([wd_s], rows=(0, down_cut))))
    (n2,), _ = _rows_call(lambda v, g: ((_rms(v, g),), ()), rows=t_len, tr=tr_wide, row_ins=[(h1, d, 0)],
                          full_ins=[g2], row_outs=[(d, BF16)], acc_outs=[], name="norm_ffn")
    g_ff, (gd_1, gu) = _matmul(n2, wgt, tb=True, out_dtype=BF16, name="ffn_gate", ride=_merge(
        _gather_chips([wd_s], rows=(down_cut, ff_s - down_cut), into=[gd_1]), _gather_sibling([gu_1])))
    wut = gu.reshape(ff, d)
    (u_ff, a_ff), (gd,) = _matmul(n2, wut, tb=True, name="ffn_up", ride=_gather_sibling([gd_1]),
                                  post=(lambda uv, gv: (uv, gv * _sigmoid(gv) * uv), [g_ff], [BF16, BF16]),
                                  tn_max=MM_TILE_N_POST)
    wd = gd.reshape(ff, d)
    h2, _ = _matmul(a_ff, wd, out_dtype=F32, add=h1, name="ffn_down")

    def head(hv, tv, g):
        def f(hh, gg_):
            e = _rms(hh, gg_) - tv
            return (0.5 / d) * jnp.sum(e * e)
        val, (dh, dg) = jax.value_and_grad(f, argnums=(0, 1))(hv, g)
        return (dh, dh), (jnp.full((1, LANES), val, F32), dg)

    (dh2, dh2_b, loss_acc, dg3), _ = _rows_call(head, rows=t_len, tr=tr, row_ins=[(h2, d, 0), (target, d, 0)],
                                                full_ins=[g3], row_outs=[(d, F32), (d, BF16)],
                                                acc_outs=[(1, LANES), (1, d)], name="loss_head")
    loss = lax.psum(loss_acc[0, 0], ("x", "y", "c"))

    def act_bwd(dav, gv, uv):
        s = _sigmoid(gv)
        return dav * uv * (s * (1.0 + gv * (1.0 - s))), dav * gv * s

    (dg_ff, du_ff), _ = _matmul(dh2_b, wd, tb=True, name="d_ffn_gate_up",
                                post=(act_bwd, [g_ff, u_ff], [BF16, BF16]), tn_max=MM_TILE_N_POST)
    dwd, _ = _matmul(a_ff, dh2_b, ta=True, out_dtype=BF16, name="d_w_down")
    dwd8 = dwd.reshape(N_DEV, ff_s, d)
    dn2, (sib_d,) = _matmul(dg_ff, wgt, out_dtype=F32, name="d_norm_ffn_out_gate", ride=_scatter_sibling([dwd8]))
    chip_d = _chip_sum(dwd8, sib_d, name="chip_sum_w_down")
    dn2, (parts_d,) = _matmul(du_ff, wut, out_dtype=F32, add=dn2, name="d_norm_ffn_out_up",
                              ride=_scatter_chips([chip_d], rows=(0, half_cut)))
    dwg, (parts_d,) = _matmul(dg_ff, n2, ta=True, out_dtype=BF16, name="d_w_gate",
                              ride=_scatter_chips([chip_d], rows=(half_cut, ff_s - half_cut), into=[parts_d]))
    dwu, _ = _matmul(du_ff, n2, ta=True, out_dtype=BF16, name="d_w_up")
    dwg8, dwu8 = dwg.reshape(N_DEV, ff_s, d), dwu.reshape(N_DEV, ff_s, d)

    def norm_bwd(v, dn, dres, g):
        _, vjp = jax.vjp(_rms, v, g)
        dv, dg = vjp(dn)
        return (dv + dres,), (dg,)

    def norm_bwd_2(v, dn, dres, g):
        (dv,), acc = norm_bwd(v, dn, dres, g)
        return (dv, dv), acc

    (dh1, dh1_b, dg2), (sib_g, sib_u) = _rows_call(norm_bwd_2, rows=t_len, tr=tr,
                                                   row_ins=[(h1, d, 0), (dn2, d, 0), (dh2, d, 0)], full_ins=[g2],
                                                   row_outs=[(d, F32), (d, BF16)], acc_outs=[(1, d)], name="d_norm_ffn",
                                                   ride=_scatter_sibling([dwg8, dwu8]))
    chip_g = _chip_sum(dwg8, sib_g, name="chip_sum_w_gate")
    chip_u = _chip_sum(dwu8, sib_u, name="chip_sum_w_up")

    dy_mix, _ = _matmul(dh1_b, wo, tb=True, out_dtype=BF16, name="d_y_mix")
    dwo, _ = _matmul(y_mix, dh1_b, ta=True, out_dtype=BF16, name="d_w_out")
    dwo8 = dwo.reshape(N_DEV, 2 * d // N_DEV, d)
    (dgb, dgc, du, dcw_sc), (sib_o,) = _cols_call(
        _shortconv_bwd, rows=t_len, cols=d, cw=cw,
        col_ins=[(proj, sc0), (proj, sc0 + slab(d)), (proj, sc0 + 2 * slab(d)), (dy_mix, slab(d))],
        par_ins=[(cw_sc, 0)], col_outs=[BF16] * 3, par_outs=[K_SC], name="d_shortconv",
        ride=_scatter_sibling([dwo8]))
    chip_o = _chip_sum(dwo8, sib_o, name="chip_sum_w_out")

    def gate_norm_bwd(y, z, dyo, g):
        _, vjp = jax.vjp(gate_norm, y, z.astype(F32), g)
        dy, dz, dg = vjp(dyo.astype(F32))
        return (dy, dz), (dg,)

    (dy_ssd, dproj, dgs), _ = _rows_call(gate_norm_bwd, rows=t_len, tr=tr,
                                         row_ins=[(y_ssd, d, 0), (proj, d, 0), (dy_mix, d, 0)], full_ins=[gs],
                                         row_outs=[(d, F32), (d, BF16, w_main)], acc_outs=[(1, d)],
                                         name="d_ssm_gate_norm")
    (dxs, dbm, dcm, g_dt, g_csc, g_csr3, ddk), (parts_g,) = _ssd_bwd(
        xbc, dt_c, cs_c, cs_r3, e01, dskip_e, hprev, dy_ssd, d_ssm=d, r_heads=r_heads,
        ride=_scatter_chips([chip_g]))
    ddt_c, ddt_r, dbias_r, dbias_c, dalog_r, dalog_c, ddskip = _ssd_dt(
        dt_raw, dt_raw_t, small, cots=(g_dt, g_csc, g_csr3.reshape(heads, t_len), ddk, e01))
    dcw_parts, dcb_parts, col0 = [], [], 0
    for tag, dpart in (("x", dxs), ("b", dbm), ("c", dcm)):
        (dproj, dcw_p, dcb_p), _ = _cols_call(
            _conv_silu_bwd, rows=t_len, cols=dpart.shape[1], cw=cw,
            col_ins=[(proj, slab(off_xbc + col0)), (dpart, 0)], par_ins=[(cw_ssm, slab(col0)), (ssm_conv_b, slab(col0))],
            col_outs=[BF16], par_outs=[K_SSM, 1], name="d_ssm_conv_" + tag, into=(dproj, slab(off_xbc + col0)))
        dcw_parts.append(dcw_p)
        dcb_parts.append(dcb_p)
        col0 += dpart.shape[1]
    dcw_ssm, dcb_ssm = jnp.concatenate(dcw_parts, axis=1), jnp.concatenate(dcb_parts, axis=1)
    for i, part in enumerate((dgb, dgc, du)):
        dproj = lax.dynamic_update_slice(dproj, part, (0, d + d_xbc + i * d))
    ddt = ddt_c + _pad_to(jnp.transpose(ddt_r), t_len, LANES)
    dwm, (parts_u, parts_o) = _matmul(dproj, n1, ta=True, out_dtype=BF16, name="d_w_in_main",
                                      ride=_scatter_chips([chip_u, chip_o]))
    dwdt, _ = _matmul(ddt, n1, ta=True, out_dtype=BF16, name="d_w_in_dt")
    own_ref = [k * in_s + i if i < in_s else -1 for k in range(N_DEV) for i in range(in_p)]
    dwin8 = _move_rows(
        dwm, [-1 if g < 0 or off_dt <= g < off_cb else (g if g < off_dt else g - heads) for g in own_ref],
        "place_d_w_in", extra=dwdt, extra_row=[g - off_dt if off_dt <= g < off_cb else -1 for g in own_ref],
    ).reshape(N_DEV, in_p, d)
    dn1, (sib_in,) = _matmul(ddt, wtdt, out_dtype=F32, name="d_norm_mix_out_dt", ride=_scatter_sibling([dwin8]))
    chip_in = _chip_sum(dwin8, sib_in, name="chip_sum_w_in")
    cut = int(in_p * W_IN_SCATTER_SPLIT) // BF16_ROWS * BF16_ROWS
    dn1, (parts_in,) = _matmul(dproj, wtm, out_dtype=F32, add=dn1, name="d_norm_mix_out",
                               ride=_scatter_chips([chip_in], rows=(0, cut)))
    (dx, dg1), _ = _rows_call(norm_bwd, rows=t_len, tr=tr, row_ins=[(x2, d, 0), (dn1, d, 0), (dh1, d, 0)],
                              full_ins=[g1], row_outs=[(d, F32)], acc_outs=[(1, d)], name="d_norm_mix")

    wide = d_xbc
    rows_small = [dg1, dcb_ssm, dbias_r + _pad_to(dbias_c.reshape(1, heads), 1, LANES),
                  dalog_r + _pad_to(dalog_c.reshape(1, heads), 1, LANES), ddskip, dgs, dg2, dg3]
    packed = jnp.concatenate([_pad_to(r, 1, wide) for r in rows_small]
                             + [dcw_ssm, _pad_to(dcw_sc, K_SC, wide), jnp.zeros((1, wide), F32)], axis=0)
    p_small, parts_in = _comm(_merge(_gather_all([packed]), _scatter_chips([chip_in], rows=(cut, in_p - cut),
                                                                           into=[parts_in])), "gather_small_grads")

    conv_lo = me * (d_xbc // N_DEV)
    sc_lo = me * (d // N_DEV)

    def pack_state(vals):
        (nm, cb, dtb, al, dk, sg, nf, nfin, cws, scs) = vals
        rows = [_pad_to(a.reshape(1, -1), 1, wide) for a in (nm, cb, dtb, al, dk, sg, nf, nfin)]
        cws_full = lax.dynamic_update_slice(jnp.zeros((K_SSM, wide), F32), cws[0], (0, conv_lo))
        scs_full = lax.dynamic_update_slice(jnp.zeros((K_SC, wide), F32), scs[0], (0, sc_lo))
        return jnp.concatenate(rows + [cws_full, scs_full, jnp.zeros((1, wide), F32)], axis=0)

    w_small = pack_state((norm_mix_g, ssm_conv_b, ssm_dt_bias, ssm_A_log, ssm_D, ssm_norm_g, norm_ffn_g, norm_final_g,
                          ssm_conv_w, sc_conv_w))
    m_small = pack_state((m_norm_mix_g, m_ssm_conv_b, m_ssm_dt_bias, m_ssm_A_log, m_ssm_D, m_ssm_norm_g, m_norm_ffn_g,
                          m_norm_final_g, m_ssm_conv_w, m_sc_conv_w))
    v_small = pack_state((v_norm_mix_g, v_ssm_conv_b, v_ssm_dt_bias, v_ssm_A_log, v_ssm_D, v_ssm_norm_g, v_norm_ffn_g,
                          v_norm_final_g, v_ssm_conv_w, v_sc_conv_w))

    tin = lambda a: _pad_to(tpose(a), in_p, d)
    tin_back = lambda a: jnp.transpose(a[:in_s])[None]
    t_back = lambda a: jnp.transpose(a)[None]
    upd = {
        "w_in": [tin_back(o) for o in _reduce_adamw(parts_in, tin(w_in), tin(m_w_in), tin(v_w_in), name="adamw_w_in")],
        "w_out": [o[None] for o in _reduce_adamw(parts_o, w_out[0], m_w_out[0], v_w_out[0], name="adamw_w_out")],
        "w_gate": [t_back(o) for o in _reduce_adamw(parts_g, tpose(w_gate), tpose(m_w_gate), tpose(v_w_gate),
                                                    name="adamw_w_gate")],
        "w_up": [t_back(o) for o in _reduce_adamw(parts_u, tpose(w_up), tpose(m_w_up), tpose(v_w_up),
                                                  name="adamw_w_up")],
        "w_down": [o[None] for o in _reduce_adamw(parts_d, w_down[0], m_w_down[0], v_w_down[0], name="adamw_w_down")],
    }
    small_upd = _reduce_adamw(p_small, w_small, m_small, v_small, name="adamw_small")

    def unpack(packed_out):
        vec = lambda i, n, shape: packed_out[i, :n].reshape(shape)
        return {
            "norm_mix_g": vec(0, d, (1, d)), "ssm_conv_b": vec(1, d_xbc, (1, d_xbc)),
            "ssm_dt_bias": vec(2, heads, (1, heads)), "ssm_A_log": vec(3, heads, (1, heads)),
            "ssm_D": vec(4, heads, (1, heads)), "ssm_norm_g": vec(5, d, (1, d)), "norm_ffn_g": vec(6, d, (1, d)),
            "norm_final_g": vec(7, d, (d,)),
            "ssm_conv_w": lax.dynamic_slice(packed_out[8:8 + K_SSM], (0, conv_lo), (K_SSM, d_xbc // N_DEV))[None],
            "sc_conv_w": lax.dynamic_slice(packed_out[8 + K_SSM:8 + K_SSM + K_SC], (0, sc_lo), (K_SC, d // N_DEV))[None],
        }

    names = ["norm_mix_g", "w_in", "ssm_conv_w", "ssm_conv_b", "ssm_dt_bias", "ssm_A_log", "ssm_D", "ssm_norm_g",
             "sc_conv_w", "w_out", "norm_ffn_g", "w_gate", "w_up", "w_down", "norm_final_g"]
    outs = []
    for kind in range(4):
        small_k = unpack(small_upd[kind])
        for nm in names:
            outs.append(upd[nm][kind] if nm in upd else small_k[nm])
    return (loss, dx[None], *outs)
```

```python
import collections
import functools

import jax
import jax.numpy as jnp
from jax import lax
from jax.experimental import pallas as pl
from jax.experimental.pallas import tpu as pltpu

F32 = jnp.float32
BF16 = jnp.bfloat16

N_DEV = 8
N_CHIPS = 4
HEADDIM = 64
N_GROUPS = 8
N_STATE = 128
CHUNK = 128
K_SSM = 4
K_SC = 3
EPS = 1e-5
LANES = 128
BF16_ROWS = 16
MM_TILE_MN = 1408
MM_TILE_K = 2816
W_IN_SCATTER_SPLIT = 13 / 14
W_UP_GATHER_SPLIT = 0.7
W_DOWN_GATHER_SPLIT = 0.3
MM_TILE_N_POST = 704
SSD_CHUNKS_PER_STEP = 4
SSD_GROUPS_PER_STEP = 4
ROW_BLOCK = 256
V7X_VMEM_BYTES = 64 * 1024 * 1024
VMEM_LIMIT = (V7X_VMEM_BYTES * 3) // 4

ADAM_LR = 0.001
ADAM_B1 = 0.9
ADAM_B2 = 0.999
ADAM_EPS = 1e-08
ADAM_WD = 0.01
ADAM_STEP = 10


def _tile(n, pref, align):
    t = min(pref, n)
    t -= t % align
    while t >= align:
        if n % t == 0:
            return t
        t -= align
    return n


_Ride = collections.namedtuple("_Ride", ["ins", "out_shapes", "aliases", "nsem", "plan", "finish"], defaults=(None,))
_ANY = pl.BlockSpec(memory_space=pl.ANY)


def _coords():
    return lax.axis_index("x"), lax.axis_index("y"), lax.axis_index("c")


def _other_chips(x, y):
    return ((1 - x, y), (x, 1 - y), (1 - x, 1 - y))


def _remote(src, dst, send, recv, k, dev):
    return functools.partial(pltpu.make_async_remote_copy, src_ref=src, dst_ref=dst, send_sem=send.at[k],
                             recv_sem=recv.at[k], device_id=dev, device_id_type=pl.DeviceIdType.MESH)


def _local(src, dst, sem):
    return functools.partial(pltpu.make_async_copy, src, dst, sem)


def _start_all(plan):
    for kind, make in plan:
        if kind != "arrival":
            make().start()


def _wait_all(plan):
    for kind, make in plan:
        if kind == "local":
            make().wait()
        elif kind == "out":
            make().wait_send()
        else:
            make().wait_recv()


def _gather_chips(srcs, rows=None, into=None):
    n = len(srcs)

    def plan(ins, outs, send, recv, base):
        x, y, c = _coords()
        me = 4 * x + 2 * y + c
        cut = (lambda ref: ref) if rows is None else (lambda ref: ref.at[pl.ds(rows[0], rows[1])])
        d = []
        for a, (src, dst) in enumerate(zip(ins[:n], outs)):
            k = base + 4 * a
            d.append(("local", _local(cut(src), cut(dst.at[me]), send.at[k + 3])))
            for j, (px, py) in enumerate(_other_chips(x, y)):
                d.append(("out", _remote(cut(src), cut(dst.at[me]), send, recv, k + j, (px, py, c))))
                d.append(("arrival", _remote(cut(src), cut(dst.at[4 * px + 2 * py + c]), send, recv, k + j,
                                             (px, py, c))))
        return d
    shapes = [jax.ShapeDtypeStruct((N_DEV,) + s.shape, s.dtype) for s in srcs]
    if into is None:
        return _Ride(list(srcs), shapes, {}, 4 * n, plan)
    return _Ride(list(srcs) + list(into), shapes, {n + a: a for a in range(n)}, 4 * n, plan)


def _gather_sibling(bufs):
    def plan(ins, outs, send, recv, base):
        x, y, c = _coords()
        d = []
        for a, buf in enumerate(outs):
            for q in range(N_CHIPS):
                k = base + 4 * a + q
                d.append(("out", _remote(buf.at[2 * q + c], buf.at[2 * q + c], send, recv, k, (x, y, 1 - c))))
                d.append(("arrival", _remote(buf.at[2 * q + c], buf.at[2 * q + 1 - c], send, recv, k, (x, y, 1 - c))))
        return d
    shapes = [jax.ShapeDtypeStruct(b.shape, b.dtype) for b in bufs]
    return _Ride(list(bufs), shapes, {i: i for i in range(len(bufs))}, 4 * len(bufs), plan)


def _scatter_sibling(srcs):
    def plan(ins, outs, send, recv, base):
        x, y, c = _coords()
        d = []
        for a, (src, sib) in enumerate(zip(ins, outs)):
            for q in range(N_CHIPS):
                k = base + 4 * a + q
                d.append(("out", _remote(src.at[2 * q + 1 - c], sib.at[q], send, recv, k, (x, y, 1 - c))))
                d.append(("arrival", _remote(src.at[2 * q + 1 - c], sib.at[q], send, recv, k, (x, y, 1 - c))))
        return d
    shapes = [jax.ShapeDtypeStruct((N_CHIPS,) + s.shape[1:], s.dtype) for s in srcs]
    return _Ride(list(srcs), shapes, {}, 4 * len(srcs), plan)


def _scatter_chips(chips, rows=None, into=None):
    n = len(chips)

    def plan(ins, outs, send, recv, base):
        x, y, c = _coords()
        mine = 2 * x + y
        cut = (lambda ref: ref) if rows is None else (lambda ref: ref.at[pl.ds(rows[0], rows[1])])
        d = []
        for a, (chip, parts) in enumerate(zip(ins[:n], outs)):
            k = base + 4 * a
            d.append(("local", _local(cut(chip.at[mine]), cut(parts.at[mine]), send.at[k + 3])))
            for j, (px, py) in enumerate(_other_chips(x, y)):
                q = 2 * px + py
                d.append(("out", _remote(cut(chip.at[q]), cut(parts.at[mine]), send, recv, k + j, (px, py, c))))
                d.append(("arrival", _remote(cut(chip.at[q]), cut(parts.at[q]), send, recv, k + j, (px, py, c))))
        return d
    shapes = [jax.ShapeDtypeStruct(s.shape, s.dtype) for s in chips]
    if into is None:
        return _Ride(list(chips), shapes, {}, 4 * n, plan)
    return _Ride(list(chips) + list(into), shapes, {n + a: a for a in range(n)}, 4 * n, plan)


def _gather_all(srcs):
    def plan(ins, outs, send, recv, base):
        x, y, c = _coords()
        me = 4 * x + 2 * y + c
        d = []
        for a, (src, dst) in enumerate(zip(ins, outs)):
            k = base + N_DEV * a
            d.append(("local", _local(src, dst.at[me], send.at[k])))
            for j in range(1, N_DEV):
                px = 1 - x if (j >> 2) & 1 else x
                py = 1 - y if (j >> 1) & 1 else y
                pc = 1 - c if j & 1 else c
                d.append(("out", _remote(src, dst.at[me], send, recv, k + j, (px, py, pc))))
                d.append(("arrival", _remote(src, dst.at[4 * px + 2 * py + pc], send, recv, k + j, (px, py, pc))))
        return d
    shapes = [jax.ShapeDtypeStruct((N_DEV,) + s.shape, s.dtype) for s in srcs]
    return _Ride(list(srcs), shapes, {}, N_DEV * len(srcs), plan)


def _merge(*rides):
    ins, outs, aliases, parts, nsem = [], [], {}, [], 0
    for r in rides:
        parts.append((len(ins), len(outs), nsem, r))
        aliases.update({len(ins) + i: len(outs) + j for i, j in r.aliases.items()})
        ins += r.ins
        outs += r.out_shapes
        nsem += r.nsem

    def plan(i, o, send, recv, base):
        d = []
        for i0, o0, s0, r in parts:
            d += r.plan(i[i0:i0 + len(r.ins)], o[o0:o0 + len(r.out_shapes)], send, recv, base + s0)
        return d
    return _Ride(ins, outs, aliases, nsem, plan)


def _comm(ride, name):
    n_in, n_out = len(ride.ins), len(ride.out_shapes)

    def body(*refs):
        plan = ride.plan(refs[:n_in], refs[n_in:n_in + n_out], refs[-2], refs[-1], 0)
        _start_all(plan)
        if ride.finish is None:
            _wait_all(plan)
        else:
            ride.finish(refs[:n_in], refs[n_in:n_in + n_out], refs[-2], refs[-1])

    return pl.pallas_call(
        body, name=name, in_specs=[_ANY] * n_in, out_specs=[_ANY] * n_out, out_shape=ride.out_shapes,
        scratch_shapes=[pltpu.SemaphoreType.DMA((ride.nsem,)), pltpu.SemaphoreType.DMA((ride.nsem,))],
        input_output_aliases=dict(ride.aliases),
        compiler_params=pltpu.CompilerParams(has_side_effects=True),
    )(*ride.ins)


def _gather_relayed(big, small):
    srcs = list(big) + list(small)
    n, per = len(srcs), 10

    def places(ins, outs):
        x, y, c = _coords()
        slot = lambda dev: 4 * dev[0] + 2 * dev[1] + dev[2]
        devs = dict(me=(x, y, c), nx=(1 - x, y, c), ny=(x, 1 - y, c), dg=(1 - x, 1 - y, c), sib=(x, y, 1 - c))
        return devs, slot

    def first(ins, outs, send, recv, base):
        devs, slot = places(ins, outs)
        d = []
        for a, (src, dst) in enumerate(zip(ins, outs)):
            k, mine = base + per * a, dst.at[slot(devs["me"])]
            d.append(("local", _local(src, mine, send.at[k + 9])))
            d.append(("out", _remote(src, mine, send, recv, k, devs["nx"])))
            d.append(("out", _remote(src, mine, send, recv, k + 1, devs["ny"])))
            d.append(("out", _remote(src, mine, send, recv, k + 4, devs["sib"])))
            if a >= len(big):
                d.append(("out", _remote(src, mine, send, recv, k + 2, devs["dg"])))
        return d

    def finish(ins, outs, send, recv):
        devs, slot = places(ins, outs)
        sib_of = lambda dev: (dev[0], dev[1], 1 - dev[2])
        later = []

        def go(copy):
            copy.start()
            later.append(copy)

        for a, (src, dst) in enumerate(zip(ins, outs)):
            k, rows = per * a, src.shape[0]
            relay = a < len(big)
            half = rows // 2
            lo = lambda dev: dst.at[slot(dev)].at[pl.ds(0, half)]
            hi = lambda dev: dst.at[slot(dev)].at[pl.ds(half, rows - half)]
            whole = lambda dev: dst.at[slot(dev)]
            _remote(src, whole(devs["nx"]), send, recv, k, devs["nx"])().wait_recv()
            if relay:
                go(_remote(lo(devs["nx"]), lo(devs["nx"]), send, recv, k + 2, devs["ny"])())
            go(_remote(whole(devs["nx"]), whole(devs["nx"]), send, recv, k + 5, devs["sib"])())
            _remote(src, whole(devs["ny"]), send, recv, k + 1, devs["ny"])().wait_recv()
            if relay:
                go(_remote(hi(devs["ny"]), hi(devs["ny"]), send, recv, k + 3, devs["nx"])())
            go(_remote(whole(devs["ny"]), whole(devs["ny"]), send, recv, k + 6, devs["sib"])())
            if relay:
                _remote(lo(devs["dg"]), lo(devs["dg"]), send, recv, k + 2, devs["ny"])().wait_recv()
                go(_remote(lo(devs["dg"]), lo(devs["dg"]), send, recv, k + 7, devs["sib"])())
                _remote(hi(devs["dg"]), hi(devs["dg"]), send, recv, k + 3, devs["nx"])().wait_recv()
                go(_remote(hi(devs["dg"]), hi(devs["dg"]), send, recv, k + 8, devs["sib"])())
            else:
                _remote(src, whole(devs["dg"]), send, recv, k + 2, devs["dg"])().wait_recv()
                go(_remote(whole(devs["dg"]), whole(devs["dg"]), send, recv, k + 7, devs["sib"])())
        for a, (src, dst) in enumerate(zip(ins, outs)):
            k, rows = per * a, src.shape[0]
            half = rows // 2
            for j, dev in ((4, devs["me"]), (5, devs["nx"]), (6, devs["ny"])):
                theirs = dst.at[slot(sib_of(dev))]
                _remote(theirs, theirs, send, recv, k + j, devs["sib"])().wait_recv()
            far = dst.at[slot(sib_of(devs["dg"]))]
            if a < len(big):
                _remote(far.at[pl.ds(0, half)], far.at[pl.ds(0, half)], send, recv, k + 7, devs["sib"])().wait_recv()
                _remote(far.at[pl.ds(half, rows - half)], far.at[pl.ds(half, rows - half)], send, recv, k + 8,
                        devs["sib"])().wait_recv()
            else:
                _remote(far, far, send, recv, k + 7, devs["sib"])().wait_recv()
        for kind, make in first(ins, outs, send, recv, 0):
            (make().wait if kind == "local" else make().wait_send)()
        for copy in later:
            copy.wait_send()

    shapes = [jax.ShapeDtypeStruct((N_DEV,) + s.shape, s.dtype) for s in srcs]
    return _Ride(srcs, shapes, {}, per * n, first, finish)


def _call(body, *, name, grid, in_specs, out_specs, out_shape, args, sem, scratch=(), ride=None, base=None):
    params = pltpu.CompilerParams(dimension_semantics=sem, vmem_limit_bytes=VMEM_LIMIT)
    own_aliases = {}
    if base is not None:
        inner, n_host = body, len(args)
        body = lambda *refs: inner(*refs[:n_host], *refs[n_host + 1:])
        own_aliases[n_host] = base[1]
        args, in_specs = tuple(args) + (base[0],), list(in_specs) + [_ANY]
    if ride is None:
        res = pl.pallas_call(body, name=name, grid=grid, in_specs=in_specs, out_specs=out_specs,
                             out_shape=out_shape, scratch_shapes=list(scratch), input_output_aliases=own_aliases,
                             compiler_params=params)(*args)
        return list(res), []
    n_in, n_out, n_scr = len(args), len(out_shape), len(scratch)
    r_in, r_out = len(ride.ins), len(ride.out_shapes)

    def hosted(*refs):
        h_in, rin = refs[:n_in], refs[n_in:n_in + r_in]
        o0 = n_in + r_in
        h_out, rout = refs[o0:o0 + n_out], refs[o0 + n_out:o0 + n_out + r_out]
        s0 = o0 + n_out + r_out
        h_scr, send, recv = refs[s0:s0 + n_scr], refs[s0 + n_scr], refs[s0 + n_scr + 1]
        ids = [pl.program_id(i) for i in range(len(grid))]
        first = functools.reduce(lambda p, q: p & q, [i == 0 for i in ids])
        last = functools.reduce(lambda p, q: p & q, [i == n - 1 for i, n in zip(ids, grid)])

        @pl.when(first)
        def _():
            _start_all(ride.plan(rin, rout, send, recv, 0))

        body(*h_in, *h_out, *h_scr)

        @pl.when(last)
        def _():
            if ride.finish is None:
                _wait_all(ride.plan(rin, rout, send, recv, 0))
            else:
                ride.finish(rin, rout, send, recv)

    res = pl.pallas_call(
        hosted, name=name, grid=grid, in_specs=list(in_specs) + [_ANY] * r_in,
        out_specs=list(out_specs) + [_ANY] * r_out, out_shape=list(out_shape) + list(ride.out_shapes),
        scratch_shapes=list(scratch) + [pltpu.SemaphoreType.DMA((ride.nsem,)), pltpu.SemaphoreType.DMA((ride.nsem,))],
        input_output_aliases={**own_aliases, **{n_in + i: n_out + j for i, j in ride.aliases.items()}},
        compiler_params=params,
    )(*args, *ride.ins)
    return list(res[:n_out]), list(res[n_out:])


def _matmul(a, b, *, ta=False, tb=False, out_dtype=BF16, add=None, post=None, name, ride=None, tn_max=MM_TILE_MN):
    m = a.shape[1] if ta else a.shape[0]
    k = a.shape[0] if ta else a.shape[1]
    n = b.shape[0] if tb else b.shape[1]
    assert k == (b.shape[1] if tb else b.shape[0])
    tm, tn, tk = _tile(m, MM_TILE_MN, LANES), _tile(n, tn_max, LANES), _tile(k, MM_TILE_K, LANES)
    nk = k // tk
    dims = (((0 if ta else 1,), (1 if tb else 0,)), ((), ()))
    single = post is None
    if add is not None:
        post = (lambda r, t: (r + t,), [add], [out_dtype])
    elif post is None:
        post = (lambda r: (r,), [], [out_dtype])
    post_fn, extras, out_dtypes = post
    n_ex, n_o = len(extras), len(out_dtypes)

    def body(*refs):
        a_ref, b_ref = refs[:2]
        ex_refs, o_refs = refs[2:2 + n_ex], refs[2 + n_ex:2 + n_ex + n_o]

        def finish(r):
            for o_ref, v in zip(o_refs, post_fn(r, *[e[...].astype(F32) for e in ex_refs])):
                o_ref[...] = v.astype(o_ref.dtype)

        part = lax.dot_general(a_ref[...].astype(BF16), b_ref[...].astype(BF16), dims, preferred_element_type=F32)
        if nk == 1:
            finish(part)
            return
        acc = refs[-1]
        kk = pl.program_id(2)

        @pl.when(kk == 0)
        def _():
            acc[...] = part

        @pl.when((kk > 0) & (kk < nk - 1))
        def _():
            acc[...] += part

        @pl.when(kk == nk - 1)
        def _():
            finish(acc[...] + part)

    a_spec = (pl.BlockSpec((tk, tm), lambda i, j, kk: (kk, i)) if ta
              else pl.BlockSpec((tm, tk), lambda i, j, kk: (i, kk)))
    b_spec = (pl.BlockSpec((tn, tk), lambda i, j, kk: (j, kk)) if tb
              else pl.BlockSpec((tk, tn), lambda i, j, kk: (kk, j)))
    o_spec = pl.BlockSpec((tm, tn), lambda i, j, kk: (i, j))
    outs, rides = _call(
        body, name=name, grid=(m // tm, n // tn, nk),
        in_specs=[a_spec, b_spec] + [o_spec] * n_ex, out_specs=[o_spec] * n_o,
        out_shape=[jax.ShapeDtypeStruct((m, n), dt) for dt in out_dtypes], args=(a, b, *extras),
        scratch=[pltpu.VMEM((tm, tn), F32)] if nk > 1 else [], sem=("parallel", "parallel", "arbitrary"), ride=ride)
    return (outs[0] if single else outs), rides


def _rows_call(fn, *, rows, tr, row_ins, full_ins, row_outs, acc_outs, name, ride=None):
    nr, nf, no, na = len(row_ins), len(full_ins), len(row_outs), len(acc_outs)

    def body(*refs):
        vals = [r[...] for r in refs[:nr + nf]]
        outs, accs = fn(*vals)
        for r, v in zip(refs[nr + nf:nr + nf + no], outs):
            r[...] = v.astype(r.dtype)
        if na:
            @pl.when(pl.program_id(0) == 0)
            def _():
                for r in refs[nr + nf + no:]:
                    r[...] = jnp.zeros_like(r)
            for r, v in zip(refs[nr + nf + no:], accs):
                r[...] += v

    in_specs = [pl.BlockSpec((tr, w), functools.partial(lambda cb, i: (i, cb), cb)) for _, w, cb in row_ins]
    in_specs += [pl.BlockSpec(f.shape, lambda i: (0, 0)) for f in full_ins]
    out_specs = [pl.BlockSpec((tr, o[0]), lambda i: (i, 0)) for o in row_outs]
    out_specs += [pl.BlockSpec(s, lambda i: (0, 0)) for s in acc_outs]
    out_shape = [jax.ShapeDtypeStruct((rows, o[-1] if len(o) == 3 else o[0]), o[1]) for o in row_outs]
    out_shape += [jax.ShapeDtypeStruct(s, F32) for s in acc_outs]
    return _call(body, name=name, grid=(rows // tr,), in_specs=in_specs, out_specs=out_specs, out_shape=out_shape,
                 args=tuple(a for a, _, _ in row_ins) + tuple(full_ins), sem=("arbitrary",), ride=ride)


def _cols_call(fn, *, rows, cols, cw, col_ins, par_ins, col_outs, par_outs, name, ride=None, into=None):
    nc, npar = len(col_ins), len(par_ins)

    def body(*refs):
        vals = [r[...] for r in refs[:nc + npar]]
        outs, pouts = fn(*vals)
        for r, v in zip(refs[nc + npar:], tuple(outs) + tuple(pouts)):
            r[...] = v.astype(r.dtype)

    in_specs = [pl.BlockSpec((rows, cw), functools.partial(lambda off, j: (0, off + j), off)) for _, off in col_ins]
    in_specs += [pl.BlockSpec((p.shape[0], cw), functools.partial(lambda off, j: (0, off + j), off))
                 for p, off in par_ins]
    out_specs = [pl.BlockSpec((rows, cw), lambda j: (0, j)) for _ in col_outs]
    out_specs += [pl.BlockSpec((k, cw), lambda j: (0, j)) for k in par_outs]
    out_shape = [jax.ShapeDtypeStruct((rows, cols), dt) for dt in col_outs]
    out_shape += [jax.ShapeDtypeStruct((k, cols), F32) for k in par_outs]
    if into is not None:
        out_specs[0] = pl.BlockSpec((rows, cw), lambda j: (0, into[1] + j))
        out_shape[0] = jax.ShapeDtypeStruct(into[0].shape, into[0].dtype)
    return _call(body, name=name, grid=(cols // cw,), in_specs=in_specs, out_specs=out_specs, out_shape=out_shape,
                 args=tuple(a for a, _ in col_ins) + tuple(p for p, _ in par_ins), sem=("arbitrary",), ride=ride,
                 base=None if into is None else (into[0], 0))


def _sigmoid(v):
    return 1.0 / (1.0 + jnp.exp(-v))


def _softplus(v):
    return jnp.maximum(v, 0.0) + jnp.log(1.0 + jnp.exp(-jnp.abs(v)))


def _rms(v, g):
    return v * lax.rsqrt(jnp.mean(v * v, axis=-1, keepdims=True) + EPS) * g


def _shift_down(v, s, row):
    return jnp.where(row >= s, pltpu.roll(v, s, 0), 0.0)


def _shift_up(v, s, row):
    n = v.shape[0]
    return jnp.where(row < n - s, pltpu.roll(v, n - s, 0), 0.0)


def _causal_conv(u, w, row):
    k_taps = w.shape[0]
    acc = u * w[k_taps - 1:k_taps, :]
    for k in range(k_taps - 1):
        acc = acc + _shift_down(u, k_taps - 1 - k, row) * w[k:k + 1, :]
    return acc


def _causal_conv_bwd(u, dy, w, row):
    k_taps = w.shape[0]
    tap = lax.broadcasted_iota(jnp.int32, w.shape, 0)
    du = dy * w[k_taps - 1:k_taps, :]
    dw = jnp.where(tap == k_taps - 1, jnp.sum(dy * u, axis=0, keepdims=True), 0.0)
    for k in range(k_taps - 1):
        s = k_taps - 1 - k
        du = du + _shift_up(dy, s, row) * w[k:k + 1, :]
        dw = dw + jnp.where(tap == k, jnp.sum(dy * _shift_down(u, s, row), axis=0, keepdims=True), 0.0)
    return du, dw


def _conv_silu_fwd(u, w, b):
    u = u.astype(F32)
    row = lax.broadcasted_iota(jnp.int32, u.shape, 0)
    pre = _causal_conv(u, w, row) + b
    return (pre * _sigmoid(pre),), ()


def _conv_silu_bwd(u, dy, w, b):
    u = u.astype(F32)
    dy = dy.astype(F32)
    row = lax.broadcasted_iota(jnp.int32, u.shape, 0)
    pre = _causal_conv(u, w, row) + b
    s = _sigmoid(pre)
    dpre = dy * (s * (1.0 + pre * (1.0 - s)))
    du, dw = _causal_conv_bwd(u, dpre, w, row)
    return (du,), (dw, jnp.sum(dpre, axis=0, keepdims=True))


def _shortconv_fwd(gb, gc, u, w):
    gb, gc, u = gb.astype(F32), gc.astype(F32), u.astype(F32)
    row = lax.broadcasted_iota(jnp.int32, u.shape, 0)
    return (gb * _causal_conv(gc * u, w, row),), ()


def _shortconv_bwd(gb, gc, u, dy, w):
    gb, gc, u, dy = gb.astype(F32), gc.astype(F32), u.astype(F32), dy.astype(F32)
    row = lax.broadcasted_iota(jnp.int32, u.shape, 0)
    v = gc * u
    dgb = dy * _causal_conv(v, w, row)
    dv, dw = _causal_conv_bwd(v, dy * gb, w, row)
    return (dgb, dv * u, dv * gc), (dw,)


def _split3(v):
    hi = v.astype(BF16)
    r1 = v - hi.astype(F32)
    mid = r1.astype(BF16)
    lo = (r1 - mid.astype(F32)).astype(BF16)
    return hi, mid, lo


def _exact_dot(v, m01, dims, v_is_lhs):
    def one(p):
        return (lax.dot_general(p, m01, dims, preferred_element_type=F32) if v_is_lhs
                else lax.dot_general(m01, p, dims, preferred_element_type=F32))
    hi, mid, lo = _split3(v)
    return (one(lo) + one(mid)) + one(hi)


_NN = (((1,), (0,)), ((), ()))
_NT = (((1,), (1,)), ((), ()))
_TN = (((0,), (0,)), ((), ()))


@functools.partial(jax.custom_vjp, nondiff_argnums=(2,))
def _bdot(a, b, dims):
    return lax.dot_general(a.astype(BF16), b.astype(BF16), dims, preferred_element_type=F32)


def _bdot_fwd(a, b, dims):
    return _bdot(a, b, dims), (a, b)


def _bdot_bwd(dims, res, g):
    a, b = res
    ab, bb, gb = a.astype(BF16), b.astype(BF16), g.astype(BF16)
    dot = lambda p, q, d: lax.dot_general(p, q, d, preferred_element_type=F32)
    if dims == _NN:
        da, db = dot(gb, bb, _NT), dot(ab, gb, _TN)
    elif dims == _NT:
        da, db = dot(gb, bb, _NN), dot(gb, ab, _TN)
    else:
        da, db = dot(bb, gb, _NT), dot(ab, gb, _NN)
    return da.astype(a.dtype), db.astype(b.dtype)


_bdot.defvjp(_bdot_fwd, _bdot_bwd)


@jax.custom_vjp
def _cumsum_rows(tril, v):
    return _exact_dot(v, tril, _NN, False)


def _cumsum_rows_fwd(tril, v):
    return _cumsum_rows(tril, v), tril


def _cumsum_rows_bwd(tril, ct):
    return None, _exact_dot(ct, tril, _TN, False)


_cumsum_rows.defvjp(_cumsum_rows_fwd, _cumsum_rows_bwd)


@jax.custom_vjp
def _cumsum_lanes(tril, v):
    return _exact_dot(v, tril, _NT, True)


def _cumsum_lanes_fwd(tril, v):
    return _cumsum_lanes(tril, v), tril


def _cumsum_lanes_bwd(tril, ct):
    return None, _exact_dot(ct, tril, _NN, True)


_cumsum_lanes.defvjp(_cumsum_lanes_fwd, _cumsum_lanes_bwd)


@jax.custom_vjp
def _expand(e01, v):
    return _exact_dot(v, e01, _NN, True)


def _expand_fwd(e01, v):
    return _expand(e01, v), e01


def _expand_bwd(e01, ct):
    return None, _exact_dot(ct, e01, _NT, True)


_expand.defvjp(_expand_fwd, _expand_bwd)


def _causal_mask(n):
    li = lax.broadcasted_iota(jnp.int32, (n, n), 0)
    si = lax.broadcasted_iota(jnp.int32, (n, n), 1)
    return si <= li


def _dt_prep(dtc, dtr, bias_r, bias_c, alog_r, alog_c):
    dt_c = _softplus(dtc + bias_r)
    dt_r = _softplus(dtr + bias_c)
    tril = jnp.where(_causal_mask(dtc.shape[0]), 1.0, 0.0).astype(BF16)
    cs_c = _cumsum_rows(tril, dt_c * (-jnp.exp(alog_r)))
    cs_r = _cumsum_lanes(tril, dt_r * (-jnp.exp(alog_c)))
    return dt_c, cs_c, cs_r


def _ssd_chunk(r_heads, xs, bg, cg, dt_c, cs_c, cs_rg, e01, dskip_e, hp):
    l_len, rp = xs.shape
    p = rp // r_heads
    causal = _causal_mask(l_len)
    lane_head = lax.broadcasted_iota(jnp.int32, (1, rp), 1) // p
    dt_e = _expand(e01, dt_c)
    cs_e = _expand(e01, cs_c)
    cl_e = cs_e[l_len - 1:l_len, :]
    x = xs * dt_e
    cb = _bdot(cg, bg, _NT)
    ms, xm = [], []
    for r in range(r_heads):
        seg = cs_e[:, r * p:r * p + 1] - cs_rg[r:r + 1, :]
        decay = jnp.exp(jnp.where(causal, seg, -1e30))
        ms.append((cb * decay).astype(BF16))
        xm.append(jnp.where(lane_head == r, x, 0.0).astype(BF16))
    y_diag = _bdot(jnp.concatenate(ms, axis=1), jnp.concatenate(xm, axis=0), _NN)
    y_off = _bdot(cg, hp, _NN) * jnp.exp(cs_e)
    states = _bdot(bg, x * jnp.exp(cl_e - cs_e), _TN)
    h_next = hp * jnp.exp(cl_e) + states
    y = y_diag + y_off + dskip_e * xs
    return y, h_next


def _ssd_dt(dtc, dtr, small, cots=None):
    t_len, heads = dtc.shape[0], dtr.shape[0]
    nc = t_len // CHUNK
    col = pl.BlockSpec((CHUNK, LANES), lambda c: (c, 0))
    row = pl.BlockSpec((heads, CHUNK), lambda c: (0, c))
    full = [pl.BlockSpec(s.shape, lambda c: (0, 0)) for s in small]
    shapes = [jax.ShapeDtypeStruct((t_len, LANES), F32), jax.ShapeDtypeStruct((t_len, LANES), F32),
              jax.ShapeDtypeStruct((heads, t_len), F32)]
    if cots is None:
        def body(dtc_ref, dtr_ref, br, bc, ar, ac, dt_ref, csc_ref, csr_ref):
            dt_ref[...], csc_ref[...], csr_ref[...] = _dt_prep(dtc_ref[...], dtr_ref[...], br[...], bc[...],
                                                                ar[...], ac[...])
        return _call(body, name="ssd_dt", grid=(nc,), in_specs=[col, row] + full, out_specs=[col, col, row],
                     out_shape=shapes, args=(dtc, dtr, *small), sem=("parallel",))[0]

    g_dt, g_csc, g_csr, ddk, e01 = cots

    def body(dtc_ref, dtr_ref, br, bc, ar, ac, g_dt_ref, g_csc_ref, g_csr_ref, ddk_ref, e_ref,
             ddtc_ref, ddtr_ref, *dsmall):
        _, vjp = jax.vjp(_dt_prep, dtc_ref[...], dtr_ref[...], br[...], bc[...], ar[...], ac[...])
        grads = vjp((g_dt_ref[...], g_csc_ref[...], g_csr_ref[...]))
        ddtc_ref[...], ddtr_ref[...] = grads[0], grads[1]
        ddk8 = jnp.broadcast_to(ddk_ref[...], (8, ddk_ref.shape[1]))
        dskip = _exact_dot(ddk8, e_ref[...], _NT, True)[0:1, :]

        @pl.when(pl.program_id(0) == 0)
        def _():
            for r in dsmall:
                r[...] = jnp.zeros_like(r)

        for r, gr in zip(dsmall, tuple(grads[2:]) + (dskip,)):
            r[...] += gr

    acc = list(small) + [small[0]]
    return _call(body, name="d_ssd_dt", grid=(nc,),
                 in_specs=[col, row] + full + [col, col, row, pl.BlockSpec((None, 1, e01.shape[1]), lambda c: (c, 0, 0)),
                                               pl.BlockSpec(e01.shape, lambda c: (0, 0))],
                 out_specs=[col, row] + [pl.BlockSpec(s.shape, lambda c: (0, 0)) for s in acc],
                 out_shape=[shapes[0], shapes[2]] + [jax.ShapeDtypeStruct(s.shape, F32) for s in acc],
                 args=(dtc, dtr, *small, g_dt, g_csc, g_csr, ddk, e01), sem=("arbitrary",))[0]


def _ssd_specs(t_len, d_ssm, r_heads, reverse):
    rp = r_heads * HEADDIM
    nc = t_len // CHUNK
    per = next(p for p in (SSD_CHUNKS_PER_STEP, 2, 1) if nc % p == 0)
    ns, rows, gs = nc // per, per * CHUNK, SSD_GROUPS_PER_STEP
    cidx = (lambda c: ns - 1 - c) if reverse else (lambda c: c)
    b_off = d_ssm // (N_STATE * gs)
    specs = dict(
        xs=pl.BlockSpec((rows, gs * rp), lambda c, g: (cidx(c), g)),
        b=pl.BlockSpec((rows, gs * N_STATE), lambda c, g: (cidx(c), b_off + g)),
        c=pl.BlockSpec((rows, gs * N_STATE), lambda c, g: (cidx(c), b_off + N_GROUPS // gs + g)),
        grad_bc=pl.BlockSpec((rows, gs * N_STATE), lambda c, g: (cidx(c), g)),
        col=pl.BlockSpec((rows, LANES), lambda c, g: (cidx(c), 0)),
        csr=pl.BlockSpec((gs, r_heads, rows), lambda c, g: (g, 0, cidx(c))),
        e01=pl.BlockSpec((LANES, gs * rp), lambda c, g: (0, g)),
        dskip=pl.BlockSpec((1, gs * rp), lambda c, g: (0, g)),
        hprev=pl.BlockSpec((per, gs, N_STATE, rp), lambda c, g: (cidx(c), g, 0, 0)),
        ddk=pl.BlockSpec((per, 1, gs * rp), lambda c, g: (cidx(c), 0, g)),
    )
    return specs, nc, ns, per, rp


def _ssd_fwd(xbc, dt_c, cs_c, cs_r3, e01, dskip_e, *, d_ssm, r_heads, ride=None):
    t_len = xbc.shape[0]
    sp, nc, ns, per, rp = _ssd_specs(t_len, d_ssm, r_heads, False)

    def body(xs_ref, b_ref, c_ref, dt_ref, csc_ref, csr_ref, e_ref, dk_ref, y_ref, hprev_ref, h_ref):
        c, gp = pl.program_id(0), pl.program_id(1)
        groups = [gp * SSD_GROUPS_PER_STEP + gi for gi in range(SSD_GROUPS_PER_STEP)]

        @pl.when(c == 0)
        def _():
            for g in groups:
                h_ref[g] = jnp.zeros((N_STATE, rp), F32)

        hp = [h_ref[g] for g in groups]
        for s in range(per):
            r = pl.ds(s * CHUNK, CHUNK)
            for gi in range(SSD_GROUPS_PER_STEP):
                cols, bc = pl.ds(gi * rp, rp), pl.ds(gi * N_STATE, N_STATE)
                hprev_ref[s, gi] = hp[gi]
                y, hp[gi] = _ssd_chunk(r_heads, xs_ref[r, cols].astype(F32), b_ref[r, bc].astype(F32),
                                       c_ref[r, bc].astype(F32), dt_ref[r, :], csc_ref[r, :], csr_ref[gi, :, r],
                                       e_ref[:, cols], dk_ref[:, cols], hp[gi])
                y_ref[r, cols] = y
        for gi, g in enumerate(groups):
            h_ref[g] = hp[gi]

    return _call(
        body, name="ssd_fwd", grid=(ns, N_GROUPS // SSD_GROUPS_PER_STEP),
        in_specs=[sp["xs"], sp["b"], sp["c"], sp["col"], sp["col"], sp["csr"], sp["e01"], sp["dskip"]],
        out_specs=[sp["xs"], sp["hprev"]],
        out_shape=[jax.ShapeDtypeStruct((t_len, d_ssm), F32),
                   jax.ShapeDtypeStruct((nc, N_GROUPS, N_STATE, rp), F32)],
        args=(xbc, xbc, xbc, dt_c, cs_c, cs_r3, e01, dskip_e), scratch=[pltpu.VMEM((N_GROUPS, N_STATE, rp), F32)],
        sem=("arbitrary", "arbitrary"), ride=ride)


def _ssd_bwd(xbc, dt_c, cs_c, cs_r3, e01, dskip_e, hprev, dy, *, d_ssm, r_heads, ride=None):
    t_len = xbc.shape[0]
    sp, nc, ns, per, rp = _ssd_specs(t_len, d_ssm, r_heads, True)

    def body(xs_ref, b_ref, c_ref, dt_ref, csc_ref, csr_ref, e_ref, dk_ref, hprev_ref, dy_ref,
             dxs_ref, db_ref, dc_ref, ddt_ref, dcsc_ref, dcsr_ref, ddk_ref, dh_ref):
        c, gp = pl.program_id(0), pl.program_id(1)
        groups = [gp * SSD_GROUPS_PER_STEP + gi for gi in range(SSD_GROUPS_PER_STEP)]

        @pl.when(gp == 0)
        def _():
            ddt_ref[...] = jnp.zeros_like(ddt_ref)
            dcsc_ref[...] = jnp.zeros_like(dcsc_ref)

        @pl.when(c == 0)
        def _():
            for g in groups:
                dh_ref[g] = jnp.zeros((N_STATE, rp), F32)

        dh = [dh_ref[g] for g in groups]
        for s in reversed(range(per)):
            r = pl.ds(s * CHUNK, CHUNK)
            ddt_sum, dcsc_sum = ddt_ref[r, :], dcsc_ref[r, :]
            for gi in range(SSD_GROUPS_PER_STEP):
                cols, bc = pl.ds(gi * rp, rp), pl.ds(gi * N_STATE, N_STATE)
                e01 = e_ref[:, cols]
                fn = lambda xs, bg, cg, dt, csc, csr, dk, hp: _ssd_chunk(r_heads, xs, bg, cg, dt, csc, csr, e01, dk, hp)
                _, vjp = jax.vjp(fn, xs_ref[r, cols].astype(F32), b_ref[r, bc].astype(F32), c_ref[r, bc].astype(F32),
                                 dt_ref[r, :], csc_ref[r, :], csr_ref[gi, :, r], dk_ref[:, cols], hprev_ref[s, gi])
                dxs, dbg, dcg, ddt, dcsc, dcsr, ddk, dh[gi] = vjp((dy_ref[r, cols], dh[gi]))
                dxs_ref[r, cols] = dxs.astype(dxs_ref.dtype)
                db_ref[r, bc] = dbg.astype(db_ref.dtype)
                dc_ref[r, bc] = dcg.astype(dc_ref.dtype)
                ddt_sum, dcsc_sum = ddt_sum + ddt, dcsc_sum + dcsc
                dcsr_ref[gi, :, r] = dcsr
                ddk_ref[s, :, cols] = ddk
            ddt_ref[r, :], dcsc_ref[r, :] = ddt_sum, dcsc_sum
        for gi, g in enumerate(groups):
            dh_ref[g] = dh[gi]

    n_bc = N_GROUPS * N_STATE
    return _call(
        body, name="ssd_bwd", grid=(ns, N_GROUPS // SSD_GROUPS_PER_STEP),
        in_specs=[sp["xs"], sp["b"], sp["c"], sp["col"], sp["col"], sp["csr"], sp["e01"], sp["dskip"], sp["hprev"],
                  sp["xs"]],
        out_specs=[sp["xs"], sp["grad_bc"], sp["grad_bc"], sp["col"], sp["col"], sp["csr"], sp["ddk"]],
        out_shape=[jax.ShapeDtypeStruct((t_len, d_ssm), BF16), jax.ShapeDtypeStruct((t_len, n_bc), BF16),
                   jax.ShapeDtypeStruct((t_len, n_bc), BF16), jax.ShapeDtypeStruct(dt_c.shape, F32),
                   jax.ShapeDtypeStruct(cs_c.shape, F32), jax.ShapeDtypeStruct(cs_r3.shape, F32),
                   jax.ShapeDtypeStruct((nc, 1, d_ssm), F32)],
        args=(xbc, xbc, xbc, dt_c, cs_c, cs_r3, e01, dskip_e, hprev, dy),
        scratch=[pltpu.VMEM((N_GROUPS, N_STATE, rp), F32)], sem=("arbitrary", "arbitrary"), ride=ride)


def _chip_sum(src, sib, *, name):
    rows, cols = src.shape[1:]
    tr = _tile(rows, 256, BF16_ROWS)
    core = lax.axis_index("c").astype(jnp.int32).reshape(1)

    def body(c_ref, a_ref, b_ref, o_ref):
        o_ref[...] = (a_ref[...].astype(F32) + b_ref[...].astype(F32)).astype(o_ref.dtype)

    grid_spec = pltpu.PrefetchScalarGridSpec(
        num_scalar_prefetch=1, grid=(N_CHIPS, rows // tr),
        in_specs=[pl.BlockSpec((None, tr, cols), lambda q, i, c_ref: (2 * q + c_ref[0], i, 0)),
                  pl.BlockSpec((None, tr, cols), lambda q, i, c_ref: (q, i, 0))],
        out_specs=pl.BlockSpec((None, tr, cols), lambda q, i, c_ref: (q, i, 0)))
    return pl.pallas_call(
        body, name=name, grid_spec=grid_spec, out_shape=jax.ShapeDtypeStruct(sib.shape, sib.dtype),
        compiler_params=pltpu.CompilerParams(dimension_semantics=("parallel", "parallel"), vmem_limit_bytes=VMEM_LIMIT),
    )(core, src, sib)


def _adamw(w, g, m, v):
    m = ADAM_B1 * m + (1.0 - ADAM_B1) * g
    v = ADAM_B2 * v + (1.0 - ADAM_B2) * (g * g)
    m_hat = m / (1.0 - ADAM_B1 ** ADAM_STEP)
    v_hat = v / (1.0 - ADAM_B2 ** ADAM_STEP)
    delta = -ADAM_LR * (m_hat / (jnp.sqrt(v_hat) + ADAM_EPS) + ADAM_WD * w)
    return delta, m, v


def _reduce_adamw(parts, w, m, v, *, name):
    n_parts = parts.shape[0]
    rows, cols = w.shape
    tr = _tile(rows, 128, BF16_ROWS)

    def body(p_ref, w_ref, m_ref, v_ref, g_ref, d_ref, mo_ref, vo_ref):
        g = p_ref[0].astype(F32)
        for k in range(1, n_parts):
            g = g + p_ref[k].astype(F32)
        delta, mn, vn = _adamw(w_ref[...], g, m_ref[...], v_ref[...])
        g_ref[...] = g
        d_ref[...] = delta
        mo_ref[...] = mn
        vo_ref[...] = vn

    spec = pl.BlockSpec((tr, cols), lambda i: (i, 0))
    outs, _ = _call(
        body, name=name, grid=(rows // tr,),
        in_specs=[pl.BlockSpec((n_parts, tr, cols), lambda i: (0, i, 0)), spec, spec, spec],
        out_specs=[spec] * 4, out_shape=[jax.ShapeDtypeStruct((rows, cols), F32)] * 4,
        args=(parts, w, m, v), sem=("parallel",))
    return outs


def _move_rows(src, src_row, name, extra=None, extra_row=None):
    rb, n_out, cols = ROW_BLOCK, len(src_row), src.shape[1]
    assert n_out % rb == 0 and src.shape[0] % rb == 0 and src.shape[0] // rb >= 3
    n_blocks, max_b0, seg_cap = n_out // rb, src.shape[0] // rb - 3, 4

    def segments(rows_of, lo):
        segs, r = [], 0
        while r < rb:
            if rows_of[r] < 0:
                r += 1
                continue
            e = r
            while e + 1 < rb and rows_of[e + 1] == rows_of[e] + 1:
                e += 1
            segs.append((r, e + 1, rows_of[r] - r - lo))
            r = e + 1
        assert len(segs) <= seg_cap
        return segs + [(0, 0, 0)] * (seg_cap - len(segs))

    table = []
    for j in range(n_blocks):
        rows_j = list(src_row[j * rb:(j + 1) * rb])
        valid = [v for v in rows_j if v >= 0]
        b0 = min(max((min(valid) // rb) if valid else 0, 0), max_b0)
        assert not valid or max(valid) < (b0 + 3) * rb
        row = [b0] + [v for seg in segments(rows_j, b0 * rb) for v in seg]
        extra_j = [] if extra is None else list(extra_row[j * rb:(j + 1) * rb])
        if extra is not None:
            row += [v for seg in segments(extra_j, 0) for v in seg]
        need_third = bool(valid) and max(valid) >= (b0 + 2) * rb
        third = b0 + 2 if need_third or not table else table[-1][-1]
        row += [int(need_third), int(any(v >= 0 for v in extra_j)), third]
        table.append(row)
    flag_third, flag_extra, col_third = len(table[0]) - 3, len(table[0]) - 2, len(table[0]) - 1
    table = jnp.asarray(table, jnp.int32)

    def select(tbl_ref, j, first, width, col0=0):
        r = lax.broadcasted_iota(jnp.int32, (rb, width), 0)
        c = lax.broadcasted_iota(jnp.int32, (rb, width), 1) + col0
        hit = jnp.zeros((rb, width), jnp.bool_)
        for s in range(seg_cap):
            lo, hi, off = (tbl_ref[j, first + 3 * s + i] for i in range(3))
            hit = hit | ((r >= lo) & (r < hi) & (c == r + off))
        return jnp.where(hit, 1.0, 0.0).astype(BF16)

    def body(tbl_ref, *refs):
        o_ref = refs[-1]
        j = pl.program_id(0)
        sel = select(tbl_ref, j, 1, 2 * rb)
        pick = lambda m, b: lax.dot_general(m, refs[b][...], _NN, preferred_element_type=F32)
        o_ref[...] = (pick(sel[:, :rb], 0) + pick(sel[:, rb:], 1)).astype(o_ref.dtype)

        @pl.when(tbl_ref[j, flag_third] == 1)
        def _():
            o_ref[...] = (o_ref[...].astype(F32) + pick(select(tbl_ref, j, 1, rb, 2 * rb), 2)).astype(o_ref.dtype)

        if extra is not None:
            @pl.when(tbl_ref[j, flag_extra] == 1)
            def _():
                more = lax.dot_general(select(tbl_ref, j, 1 + 3 * seg_cap, extra.shape[0]), refs[3][...], _NN,
                                       preferred_element_type=F32)
                o_ref[...] = (o_ref[...].astype(F32) + more).astype(o_ref.dtype)

    in_specs = [pl.BlockSpec((rb, cols), functools.partial(lambda b, j, tbl: (tbl[j, 0] + b, 0), b)) for b in range(2)]
    in_specs.append(pl.BlockSpec((rb, cols), lambda j, tbl: (tbl[j, col_third], 0)))
    args = [src, src, src]
    if extra is not None:
        in_specs.append(pl.BlockSpec(extra.shape, lambda j, tbl: (0, 0)))
        args.append(extra)
    grid_spec = pltpu.PrefetchScalarGridSpec(num_scalar_prefetch=1, grid=(n_blocks,), in_specs=in_specs,
                                             out_specs=pl.BlockSpec((rb, cols), lambda j, tbl: (j, 0)))
    return pl.pallas_call(
        body, name=name, grid_spec=grid_spec, out_shape=jax.ShapeDtypeStruct((n_out, cols), src.dtype),
        compiler_params=pltpu.CompilerParams(dimension_semantics=("parallel",), vmem_limit_bytes=VMEM_LIMIT),
    )(table, *args)


def _cols_of(g):
    return jnp.transpose(g, (1, 0, 2)).reshape(g.shape[1], -1)


def _pad_to(a, rows, cols):
    return jnp.pad(a, ((0, rows - a.shape[0]), (0, cols - a.shape[1])))


def kernel(x, norm_mix_g, w_in, ssm_conv_w, ssm_conv_b, ssm_dt_bias, ssm_A_log, ssm_D, ssm_norm_g, sc_conv_w, w_out, norm_ffn_g, w_gate, w_up, w_down, norm_final_g, loss_target, m_norm_mix_g, m_w_in, m_ssm_conv_w, m_ssm_conv_b, m_ssm_dt_bias, m_ssm_A_log, m_ssm_D, m_ssm_norm_g, m_sc_conv_w, m_w_out, m_norm_ffn_g, m_w_gate, m_w_up, m_w_down, m_norm_final_g, v_norm_mix_g, v_w_in, v_ssm_conv_w, v_ssm_conv_b, v_ssm_dt_bias, v_ssm_A_log, v_ssm_D, v_ssm_norm_g, v_sc_conv_w, v_w_out, v_norm_ffn_g, v_w_gate, v_w_up, v_w_down, v_norm_final_g):
    t_len, d = x.shape[1], x.shape[2]
    heads = d // HEADDIM
    r_heads = heads // N_GROUPS
    d_xbc = d + 2 * N_GROUPS * N_STATE
    ff_s = w_down.shape[1]
    ff = ff_s * N_DEV
    off_xbc, off_dt = d, d + d_xbc
    off_cb = off_dt + heads
    d_in = off_cb + 3 * d
    in_s = d_in // N_DEV
    in_p = -(-in_s // (2 * BF16_ROWS)) * (2 * BF16_ROWS)
    w_main = 4 * d + d_xbc
    me = 4 * lax.axis_index("x") + 2 * lax.axis_index("y") + lax.axis_index("c")

    x2 = x[0]
    target = loss_target[0]

    tpose = lambda a: jnp.transpose(a[0])
    win_s = _pad_to(tpose(w_in).astype(BF16), in_p, d)
    wg_s, wu_s = tpose(w_gate).astype(BF16), tpose(w_up).astype(BF16)
    wo_s, wd_s = w_out[0].astype(BF16), w_down[0].astype(BF16)
    small_w = jnp.concatenate([_pad_to(ssm_conv_w[0], K_SSM, d_xbc // N_DEV),
                               _pad_to(sc_conv_w[0], K_SC + 1, d_xbc // N_DEV)], axis=0)

    g1, g2, g3 = norm_mix_g, norm_ffn_g, norm_final_g.reshape(1, d)
    gs = ssm_norm_g
    small = [_pad_to(ssm_dt_bias, 1, LANES), ssm_dt_bias.reshape(heads, 1), _pad_to(ssm_A_log, 1, LANES),
             ssm_A_log.reshape(heads, 1)]
    e01 = (lax.broadcasted_iota(jnp.int32, (LANES, d), 1) // HEADDIM
           == lax.broadcasted_iota(jnp.int32, (LANES, d), 0)).astype(BF16)
    dskip_e = jnp.repeat(ssm_D, HEADDIM, axis=1)
    tr = _tile(t_len, 256, 8)
    tr_wide = _tile(t_len, 512, 8)
    tr_ff = _tile(t_len, 128, 8)
    cw = LANES
    slab = lambda col: col // cw

    (n1,), (gin, gsm) = _rows_call(lambda v, g: ((_rms(v, g),), ()), rows=t_len, tr=tr_wide, row_ins=[(x2, d, 0)],
                                   full_ins=[g1], row_outs=[(d, BF16)], acc_outs=[], name="norm_mix",
                                   ride=_gather_relayed([win_s], [small_w]))
    in_pieces = []
    for k in range(N_DEV):
        for a, b, dst, shift in ((0, off_dt, 0, 0), (off_dt, off_cb, 1, -off_dt), (off_cb, d_in, 0, -heads)):
            s, e = max(k * in_s, a), min((k + 1) * in_s, b)
            if s < e:
                in_pieces.append((k, s - k * in_s, e - s, dst, s + shift))
    ref_row = lambda t: t if t < off_dt else t + heads
    wtm = _move_rows(gin.reshape(N_DEV * in_p, d),
                     [(ref_row(t) // in_s) * in_p + ref_row(t) % in_s for t in range(w_main)], "place_w_in")
    wtdt = jnp.zeros((LANES, d), BF16)
    for k, r0, n, dst, d0 in in_pieces:
        if dst == 1:
            wtdt = lax.dynamic_update_slice(wtdt, gin[k, r0:r0 + n], (d0, 0))
    cw_ssm = _cols_of(gsm[:, :K_SSM, :])
    cw_sc = _cols_of(gsm[:, K_SSM:K_SSM + K_SC, :d // N_DEV])

    proj, (go_1, gg_1) = _matmul(n1, wtm, tb=True, out_dtype=BF16, name="proj_main",
                                 ride=_gather_chips([wo_s, wg_s]))
    dt_raw, _ = _matmul(n1, wtdt, tb=True, out_dtype=F32, name="proj_dt")
    dt_raw_t = jnp.transpose(dt_raw[:, :heads])
    (xbc,), (go, gg) = _cols_call(_conv_silu_fwd, rows=t_len, cols=d_xbc, cw=cw, col_ins=[(proj, slab(off_xbc))],
                                  par_ins=[(cw_ssm, 0), (ssm_conv_b, 0)], col_outs=[BF16], par_outs=[],
                                  name="ssm_conv", ride=_gather_sibling([go_1, gg_1]))
    dt_c, cs_c, cs_r = _ssd_dt(dt_raw, dt_raw_t, small)
    cs_r3 = cs_r.reshape(N_GROUPS, r_heads, t_len)
    up_cut = int(ff_s * W_UP_GATHER_SPLIT) // BF16_ROWS * BF16_ROWS
    down_cut = int(ff_s * W_DOWN_GATHER_SPLIT) // BF16_ROWS * BF16_ROWS
    half_cut = ff_s // 2 // BF16_ROWS * BF16_ROWS
    (y_ssd, hprev), (gu_1,) = _ssd_fwd(xbc, dt_c, cs_c, cs_r3, e01, dskip_e, d_ssm=d, r_heads=r_heads,
                                       ride=_gather_chips([wu_s], rows=(0, up_cut)))

    def gate_norm(y, z, g):
        z = z.astype(F32)
        return _rms(y * (z * _sigmoid(z)), g)

    (y_mix,), _ = _rows_call(lambda y, z, g: ((gate_norm(y, z, g),), ()), rows=t_len, tr=tr_wide,
                             row_ins=[(y_ssd, d, 0), (proj, d, 0)], full_ins=[gs], row_outs=[(d, BF16, 2 * d)],
                             acc_outs=[], name="ssm_gate_norm")
    wgt, wo = gg.reshape(ff, d), go.reshape(2 * d, d)
    sc0 = slab(d + d_xbc)
    (y_mix,), _ = _cols_call(_shortconv_fwd, rows=t_len, cols=d, cw=cw,
                             col_ins=[(proj, sc0), (proj, sc0 + slab(d)), (proj, sc0 + 2 * slab(d))],
                             par_ins=[(cw_sc, 0)], col_outs=[BF16], par_outs=[], name="shortconv",
                             into=(y_mix, slab(d)))
    h1, (gu_1, gd_1) = _matmul(y_mix, wo, out_dtype=F32, add=x2, name="out_proj", ride=_merge(
        _gather_chips([wu_s], rows=(up_cut, ff_s - up_cut), into=[gu_1]), _gather_chips([wd_s], rows=(0, down_cut))))
    (n2,), _ = _rows_call(lambda v, g: ((_rms(v, g),), ()), rows=t_len, tr=tr_wide, row_ins=[(h1, d, 0)],
                          full_ins=[g2], row_outs=[(d, BF16)], acc_outs=[], name="norm_ffn")
    g_ff, (gd_1, gu) = _matmul(n2, wgt, tb=True, out_dtype=BF16, name="ffn_gate", ride=_merge(
        _gather_chips([wd_s], rows=(down_cut, ff_s - down_cut), into=[gd_1]), _gather_sibling([gu_1])))
    wut = gu.reshape(ff, d)
    (u_ff, a_ff), (gd,) = _matmul(n2, wut, tb=True, name="ffn_up", ride=_gather_sibling([gd_1]),
                                  post=(lambda uv, gv: (uv, gv * _sigmoid(gv) * uv), [g_ff], [BF16, BF16]),
                                  tn_max=MM_TILE_N_POST)
    wd = gd.reshape(ff, d)
    h2, _ = _matmul(a_ff, wd, out_dtype=F32, add=h1, name="ffn_down")

    def head(hv, tv, g):
        def f(hh, gg_):
            e = _rms(hh, gg_) - tv
            return (0.5 / d) * jnp.sum(e * e)
        val, (dh, dg) = jax.value_and_grad(f, argnums=(0, 1))(hv, g)
        return (dh, dh), (jnp.full((1, LANES), val, F32), dg)

    (dh2, dh2_b, loss_acc, dg3), _ = _rows_call(head, rows=t_len, tr=tr, row_ins=[(h2, d, 0), (target, d, 0)],
                                                full_ins=[g3], row_outs=[(d, F32), (d, BF16)],
                                                acc_outs=[(1, LANES), (1, d)], name="loss_head")
    loss = lax.psum(loss_acc[0, 0], ("x", "y", "c"))

    def act_bwd(dav, gv, uv):
        s = _sigmoid(gv)
        return dav * uv * (s * (1.0 + gv * (1.0 - s))), dav * gv * s

    (dg_ff, du_ff), _ = _matmul(dh2_b, wd, tb=True, name="d_ffn_gate_up",
                                post=(act_bwd, [g_ff, u_ff], [BF16, BF16]), tn_max=MM_TILE_N_POST)
    dwd, _ = _matmul(a_ff, dh2_b, ta=True, out_dtype=BF16, name="d_w_down")
    dwd8 = dwd.reshape(N_DEV, ff_s, d)
    dn2, (sib_d,) = _matmul(dg_ff, wgt, out_dtype=F32, name="d_norm_ffn_out_gate", ride=_scatter_sibling([dwd8]))
    chip_d = _chip_sum(dwd8, sib_d, name="chip_sum_w_down")
    dn2, (parts_d,) = _matmul(du_ff, wut, out_dtype=F32, add=dn2, name="d_norm_ffn_out_up",
                              ride=_scatter_chips([chip_d], rows=(0, half_cut)))
    dwg, (parts_d,) = _matmul(dg_ff, n2, ta=True, out_dtype=BF16, name="d_w_gate",
                              ride=_scatter_chips([chip_d], rows=(half_cut, ff_s - half_cut), into=[parts_d]))
    dwu, _ = _matmul(du_ff, n2, ta=True, out_dtype=BF16, name="d_w_up")
    dwg8, dwu8 = dwg.reshape(N_DEV, ff_s, d), dwu.reshape(N_DEV, ff_s, d)

    def norm_bwd(v, dn, dres, g):
        _, vjp = jax.vjp(_rms, v, g)
        dv, dg = vjp(dn)
        return (dv + dres,), (dg,)

    def norm_bwd_2(v, dn, dres, g):
        (dv,), acc = norm_bwd(v, dn, dres, g)
        return (dv, dv), acc

    (dh1, dh1_b, dg2), (sib_g, sib_u) = _rows_call(norm_bwd_2, rows=t_len, tr=tr,
                                                   row_ins=[(h1, d, 0), (dn2, d, 0), (dh2, d, 0)], full_ins=[g2],
                                                   row_outs=[(d, F32), (d, BF16)], acc_outs=[(1, d)], name="d_norm_ffn",
                                                   ride=_scatter_sibling([dwg8, dwu8]))
    chip_g = _chip_sum(dwg8, sib_g, name="chip_sum_w_gate")
    chip_u = _chip_sum(dwu8, sib_u, name="chip_sum_w_up")

    dy_mix, _ = _matmul(dh1_b, wo, tb=True, out_dtype=BF16, name="d_y_mix")
    dwo, _ = _matmul(y_mix, dh1_b, ta=True, out_dtype=BF16, name="d_w_out")
    dwo8 = dwo.reshape(N_DEV, 2 * d // N_DEV, d)
    def gate_norm_bwd(y, z, dyo, g):
        _, vjp = jax.vjp(gate_norm, y, z.astype(F32), g)
        dy, dz, dg = vjp(dyo.astype(F32))
        return (dy, dz), (dg,)

    (dy_ssd, dproj, dgs), _ = _rows_call(gate_norm_bwd, rows=t_len, tr=tr,
                                         row_ins=[(y_ssd, d, 0), (proj, d, 0), (dy_mix, d, 0)], full_ins=[gs],
                                         row_outs=[(d, F32), (d, BF16, w_main)], acc_outs=[(1, d)],
                                         name="d_ssm_gate_norm")
    (dproj, dgc, du, dcw_sc), (sib_o,) = _cols_call(
        _shortconv_bwd, rows=t_len, cols=d, cw=cw,
        col_ins=[(proj, sc0), (proj, sc0 + slab(d)), (proj, sc0 + 2 * slab(d)), (dy_mix, slab(d))],
        par_ins=[(cw_sc, 0)], col_outs=[BF16] * 3, par_outs=[K_SC], name="d_shortconv",
        ride=_scatter_sibling([dwo8]), into=(dproj, sc0))
    chip_o = _chip_sum(dwo8, sib_o, name="chip_sum_w_out")
    (dxs, dbm, dcm, g_dt, g_csc, g_csr3, ddk), (parts_g,) = _ssd_bwd(
        xbc, dt_c, cs_c, cs_r3, e01, dskip_e, hprev, dy_ssd, d_ssm=d, r_heads=r_heads,
        ride=_scatter_chips([chip_g]))
    ddt_c, ddt_r, dbias_r, dbias_c, dalog_r, dalog_c, ddskip = _ssd_dt(
        dt_raw, dt_raw_t, small, cots=(g_dt, g_csc, g_csr3.reshape(heads, t_len), ddk, e01))
    dcw_parts, dcb_parts, col0 = [], [], 0
    for tag, dpart in (("x", dxs), ("b", dbm), ("c", dcm)):
        (dproj, dcw_p, dcb_p), _ = _cols_call(
            _conv_silu_bwd, rows=t_len, cols=dpart.shape[1], cw=cw,
            col_ins=[(proj, slab(off_xbc + col0)), (dpart, 0)], par_ins=[(cw_ssm, slab(col0)), (ssm_conv_b, slab(col0))],
            col_outs=[BF16], par_outs=[K_SSM, 1], name="d_ssm_conv_" + tag, into=(dproj, slab(off_xbc + col0)))
        dcw_parts.append(dcw_p)
        dcb_parts.append(dcb_p)
        col0 += dpart.shape[1]
    dcw_ssm, dcb_ssm = jnp.concatenate(dcw_parts, axis=1), jnp.concatenate(dcb_parts, axis=1)
    for i, part in ((1, dgc), (2, du)):
        dproj = lax.dynamic_update_slice(dproj, part, (0, d + d_xbc + i * d))
    ddt = ddt_c + _pad_to(jnp.transpose(ddt_r), t_len, LANES)
    dwm, (parts_u, parts_o) = _matmul(dproj, n1, ta=True, out_dtype=BF16, name="d_w_in_main",
                                      ride=_scatter_chips([chip_u, chip_o]))
    dwdt, _ = _matmul(ddt, n1, ta=True, out_dtype=BF16, name="d_w_in_dt")
    own_ref = [k * in_s + i if i < in_s else -1 for k in range(N_DEV) for i in range(in_p)]
    dwin8 = _move_rows(
        dwm, [-1 if g < 0 or off_dt <= g < off_cb else (g if g < off_dt else g - heads) for g in own_ref],
        "place_d_w_in", extra=dwdt, extra_row=[g - off_dt if off_dt <= g < off_cb else -1 for g in own_ref],
    ).reshape(N_DEV, in_p, d)
    dn1, (sib_in,) = _matmul(ddt, wtdt, out_dtype=F32, name="d_norm_mix_out_dt", ride=_scatter_sibling([dwin8]))
    chip_in = _chip_sum(dwin8, sib_in, name="chip_sum_w_in")
    cut = int(in_p * W_IN_SCATTER_SPLIT) // BF16_ROWS * BF16_ROWS
    dn1, (parts_in,) = _matmul(dproj, wtm, out_dtype=F32, add=dn1, name="d_norm_mix_out",
                               ride=_scatter_chips([chip_in], rows=(0, cut)))
    (dx, dg1), _ = _rows_call(norm_bwd, rows=t_len, tr=tr, row_ins=[(x2, d, 0), (dn1, d, 0), (dh1, d, 0)],
                              full_ins=[g1], row_outs=[(d, F32)], acc_outs=[(1, d)], name="d_norm_mix")

    wide = d_xbc
    rows_small = [dg1, dcb_ssm, dbias_r + _pad_to(dbias_c.reshape(1, heads), 1, LANES),
                  dalog_r + _pad_to(dalog_c.reshape(1, heads), 1, LANES), ddskip, dgs, dg2, dg3]
    packed = jnp.concatenate([_pad_to(r, 1, wide) for r in rows_small]
                             + [dcw_ssm, _pad_to(dcw_sc, K_SC, wide), jnp.zeros((1, wide), F32)], axis=0)
    p_small, parts_in = _comm(_merge(_gather_all([packed]), _scatter_chips([chip_in], rows=(cut, in_p - cut),
                                                                           into=[parts_in])), "gather_small_grads")

    conv_lo = me * (d_xbc // N_DEV)
    sc_lo = me * (d // N_DEV)

    def pack_state(vals):
        (nm, cb, dtb, al, dk, sg, nf, nfin, cws, scs) = vals
        rows = [_pad_to(a.reshape(1, -1), 1, wide) for a in (nm, cb, dtb, al, dk, sg, nf, nfin)]
        cws_full = lax.dynamic_update_slice(jnp.zeros((K_SSM, wide), F32), cws[0], (0, conv_lo))
        scs_full = lax.dynamic_update_slice(jnp.zeros((K_SC, wide), F32), scs[0], (0, sc_lo))
        return jnp.concatenate(rows + [cws_full, scs_full, jnp.zeros((1, wide), F32)], axis=0)

    w_small = pack_state((norm_mix_g, ssm_conv_b, ssm_dt_bias, ssm_A_log, ssm_D, ssm_norm_g, norm_ffn_g, norm_final_g,
                          ssm_conv_w, sc_conv_w))
    m_small = pack_state((m_norm_mix_g, m_ssm_conv_b, m_ssm_dt_bias, m_ssm_A_log, m_ssm_D, m_ssm_norm_g, m_norm_ffn_g,
                          m_norm_final_g, m_ssm_conv_w, m_sc_conv_w))
    v_small = pack_state((v_norm_mix_g, v_ssm_conv_b, v_ssm_dt_bias, v_ssm_A_log, v_ssm_D, v_ssm_norm_g, v_norm_ffn_g,
                          v_norm_final_g, v_ssm_conv_w, v_sc_conv_w))

    tin = lambda a: _pad_to(tpose(a), in_p, d)
    tin_back = lambda a: jnp.transpose(a[:in_s])[None]
    t_back = lambda a: jnp.transpose(a)[None]
    upd = {
        "w_in": [tin_back(o) for o in _reduce_adamw(parts_in, tin(w_in), tin(m_w_in), tin(v_w_in), name="adamw_w_in")],
        "w_out": [o[None] for o in _reduce_adamw(parts_o, w_out[0], m_w_out[0], v_w_out[0], name="adamw_w_out")],
        "w_gate": [t_back(o) for o in _reduce_adamw(parts_g, tpose(w_gate), tpose(m_w_gate), tpose(v_w_gate),
                                                    name="adamw_w_gate")],
        "w_up": [t_back(o) for o in _reduce_adamw(parts_u, tpose(w_up), tpose(m_w_up), tpose(v_w_up),
                                                  name="adamw_w_up")],
        "w_down": [o[None] for o in _reduce_adamw(parts_d, w_down[0], m_w_down[0], v_w_down[0], name="adamw_w_down")],
    }
    small_upd = _reduce_adamw(p_small, w_small, m_small, v_small, name="adamw_small")

    def unpack(packed_out):
        vec = lambda i, n, shape: packed_out[i, :n].reshape(shape)
        return {
            "norm_mix_g": vec(0, d, (1, d)), "ssm_conv_b": vec(1, d_xbc, (1, d_xbc)),
            "ssm_dt_bias": vec(2, heads, (1, heads)), "ssm_A_log": vec(3, heads, (1, heads)),
            "ssm_D": vec(4, heads, (1, heads)), "ssm_norm_g": vec(5, d, (1, d)), "norm_ffn_g": vec(6, d, (1, d)),
            "norm_final_g": vec(7, d, (d,)),
            "ssm_conv_w": lax.dynamic_slice(packed_out[8:8 + K_SSM], (0, conv_lo), (K_SSM, d_xbc // N_DEV))[None],
            "sc_conv_w": lax.dynamic_slice(packed_out[8 + K_SSM:8 + K_SSM + K_SC], (0, sc_lo), (K_SC, d // N_DEV))[None],
        }

    names = ["norm_mix_g", "w_in", "ssm_conv_w", "ssm_conv_b", "ssm_dt_bias", "ssm_A_log", "ssm_D", "ssm_norm_g",
             "sc_conv_w", "w_out", "norm_ffn_g", "w_gate", "w_up", "w_down", "norm_final_g"]
    outs = []
    for kind in range(4):
        small_k = unpack(small_upd[kind])
        for nm in names:
            outs.append(upd[nm][kind] if nm in upd else small_k[nm])
    return (loss, dx[None], *outs)
```

```python
import collections
import functools

import jax
import jax.numpy as jnp
from jax import lax
from jax.experimental import pallas as pl
from jax.experimental.pallas import tpu as pltpu

F32 = jnp.float32
BF16 = jnp.bfloat16

N_DEV = 8
N_CHIPS = 4
HEADDIM = 64
N_GROUPS = 8
N_STATE = 128
CHUNK = 128
K_SSM = 4
K_SC = 3
EPS = 1e-5
LANES = 128
BF16_ROWS = 16
MM_TILE_MN = 1408
MM_TILE_K = 2816
W_IN_SCATTER_SPLIT = 13 / 14
W_UP_GATHER_SPLIT = 0.7
W_DOWN_GATHER_SPLIT = 0.3
MM_TILE_N_POST = 704
SSD_CHUNKS_PER_STEP = 4
SSD_GROUPS_PER_STEP = 4
ROW_BLOCK = 256
V7X_VMEM_BYTES = 64 * 1024 * 1024
VMEM_LIMIT = (V7X_VMEM_BYTES * 3) // 4

ADAM_LR = 0.001
ADAM_B1 = 0.9
ADAM_B2 = 0.999
ADAM_EPS = 1e-08
ADAM_WD = 0.01
ADAM_STEP = 10


def _tile(n, pref, align):
    t = min(pref, n)
    t -= t % align
    while t >= align:
        if n % t == 0:
            return t
        t -= align
    return n


_Ride = collections.namedtuple("_Ride", ["ins", "out_shapes", "aliases", "nsem", "plan", "finish"], defaults=(None,))
_ANY = pl.BlockSpec(memory_space=pl.ANY)


def _coords():
    return lax.axis_index("x"), lax.axis_index("y"), lax.axis_index("c")


def _other_chips(x, y):
    return ((1 - x, y), (x, 1 - y), (1 - x, 1 - y))


def _remote(src, dst, send, recv, k, dev):
    return functools.partial(pltpu.make_async_remote_copy, src_ref=src, dst_ref=dst, send_sem=send.at[k],
                             recv_sem=recv.at[k], device_id=dev, device_id_type=pl.DeviceIdType.MESH)


def _local(src, dst, sem):
    return functools.partial(pltpu.make_async_copy, src, dst, sem)


def _start_all(plan):
    for kind, make in plan:
        if kind != "arrival":
            make().start()


def _wait_all(plan):
    for kind, make in plan:
        if kind == "local":
            make().wait()
        elif kind == "out":
            make().wait_send()
        else:
            make().wait_recv()


def _gather_chips(srcs, rows=None, into=None):
    n = len(srcs)

    def plan(ins, outs, send, recv, base):
        x, y, c = _coords()
        me = 4 * x + 2 * y + c
        cut = (lambda ref: ref) if rows is None else (lambda ref: ref.at[pl.ds(rows[0], rows[1])])
        d = []
        for a, (src, dst) in enumerate(zip(ins[:n], outs)):
            k = base + 4 * a
            d.append(("local", _local(cut(src), cut(dst.at[me]), send.at[k + 3])))
            for j, (px, py) in enumerate(_other_chips(x, y)):
                d.append(("out", _remote(cut(src), cut(dst.at[me]), send, recv, k + j, (px, py, c))))
                d.append(("arrival", _remote(cut(src), cut(dst.at[4 * px + 2 * py + c]), send, recv, k + j,
                                             (px, py, c))))
        return d
    shapes = [jax.ShapeDtypeStruct((N_DEV,) + s.shape, s.dtype) for s in srcs]
    if into is None:
        return _Ride(list(srcs), shapes, {}, 4 * n, plan)
    return _Ride(list(srcs) + list(into), shapes, {n + a: a for a in range(n)}, 4 * n, plan)


def _gather_sibling(bufs):
    def plan(ins, outs, send, recv, base):
        x, y, c = _coords()
        d = []
        for a, buf in enumerate(outs):
            for q in range(N_CHIPS):
                k = base + 4 * a + q
                d.append(("out", _remote(buf.at[2 * q + c], buf.at[2 * q + c], send, recv, k, (x, y, 1 - c))))
                d.append(("arrival", _remote(buf.at[2 * q + c], buf.at[2 * q + 1 - c], send, recv, k, (x, y, 1 - c))))
        return d
    shapes = [jax.ShapeDtypeStruct(b.shape, b.dtype) for b in bufs]
    return _Ride(list(bufs), shapes, {i: i for i in range(len(bufs))}, 4 * len(bufs), plan)


def _scatter_sibling(srcs):
    def plan(ins, outs, send, recv, base):
        x, y, c = _coords()
        d = []
        for a, (src, sib) in enumerate(zip(ins, outs)):
            for q in range(N_CHIPS):
                k = base + 4 * a + q
                d.append(("out", _remote(src.at[2 * q + 1 - c], sib.at[q], send, recv, k, (x, y, 1 - c))))
                d.append(("arrival", _remote(src.at[2 * q + 1 - c], sib.at[q], send, recv, k, (x, y, 1 - c))))
        return d
    shapes = [jax.ShapeDtypeStruct((N_CHIPS,) + s.shape[1:], s.dtype) for s in srcs]
    return _Ride(list(srcs), shapes, {}, 4 * len(srcs), plan)


def _scatter_chips(chips, rows=None, into=None):
    n = len(chips)

    def plan(ins, outs, send, recv, base):
        x, y, c = _coords()
        mine = 2 * x + y
        cut = (lambda ref: ref) if rows is None else (lambda ref: ref.at[pl.ds(rows[0], rows[1])])
        d = []
        for a, (chip, parts) in enumerate(zip(ins[:n], outs)):
            k = base + 4 * a
            d.append(("local", _local(cut(chip.at[mine]), cut(parts.at[mine]), send.at[k + 3])))
            for j, (px, py) in enumerate(_other_chips(x, y)):
                q = 2 * px + py
                d.append(("out", _remote(cut(chip.at[q]), cut(parts.at[mine]), send, recv, k + j, (px, py, c))))
                d.append(("arrival", _remote(cut(chip.at[q]), cut(parts.at[q]), send, recv, k + j, (px, py, c))))
        return d
    shapes = [jax.ShapeDtypeStruct(s.shape, s.dtype) for s in chips]
    if into is None:
        return _Ride(list(chips), shapes, {}, 4 * n, plan)
    return _Ride(list(chips) + list(into), shapes, {n + a: a for a in range(n)}, 4 * n, plan)


def _gather_all(srcs):
    def plan(ins, outs, send, recv, base):
        x, y, c = _coords()
        me = 4 * x + 2 * y + c
        d = []
        for a, (src, dst) in enumerate(zip(ins, outs)):
            k = base + N_DEV * a
            d.append(("local", _local(src, dst.at[me], send.at[k])))
            for j in range(1, N_DEV):
                px = 1 - x if (j >> 2) & 1 else x
                py = 1 - y if (j >> 1) & 1 else y
                pc = 1 - c if j & 1 else c
                d.append(("out", _remote(src, dst.at[me], send, recv, k + j, (px, py, pc))))
                d.append(("arrival", _remote(src, dst.at[4 * px + 2 * py + pc], send, recv, k + j, (px, py, pc))))
        return d
    shapes = [jax.ShapeDtypeStruct((N_DEV,) + s.shape, s.dtype) for s in srcs]
    return _Ride(list(srcs), shapes, {}, N_DEV * len(srcs), plan)


def _merge(*rides):
    ins, outs, aliases, parts, nsem = [], [], {}, [], 0
    for r in rides:
        parts.append((len(ins), len(outs), nsem, r))
        aliases.update({len(ins) + i: len(outs) + j for i, j in r.aliases.items()})
        ins += r.ins
        outs += r.out_shapes
        nsem += r.nsem

    def plan(i, o, send, recv, base):
        d = []
        for i0, o0, s0, r in parts:
            d += r.plan(i[i0:i0 + len(r.ins)], o[o0:o0 + len(r.out_shapes)], send, recv, base + s0)
        return d
    return _Ride(ins, outs, aliases, nsem, plan)


def _comm(ride, name):
    n_in, n_out = len(ride.ins), len(ride.out_shapes)

    def body(*refs):
        plan = ride.plan(refs[:n_in], refs[n_in:n_in + n_out], refs[-2], refs[-1], 0)
        _start_all(plan)
        if ride.finish is None:
            _wait_all(plan)
        else:
            ride.finish(refs[:n_in], refs[n_in:n_in + n_out], refs[-2], refs[-1])

    return pl.pallas_call(
        body, name=name, in_specs=[_ANY] * n_in, out_specs=[_ANY] * n_out, out_shape=ride.out_shapes,
        scratch_shapes=[pltpu.SemaphoreType.DMA((ride.nsem,)), pltpu.SemaphoreType.DMA((ride.nsem,))],
        input_output_aliases=dict(ride.aliases),
        compiler_params=pltpu.CompilerParams(has_side_effects=True),
    )(*ride.ins)


def _gather_relayed(big, small):
    srcs = list(big) + list(small)
    n, per = len(srcs), 10

    def places(ins, outs):
        x, y, c = _coords()
        slot = lambda dev: 4 * dev[0] + 2 * dev[1] + dev[2]
        devs = dict(me=(x, y, c), nx=(1 - x, y, c), ny=(x, 1 - y, c), dg=(1 - x, 1 - y, c), sib=(x, y, 1 - c))
        return devs, slot

    def first(ins, outs, send, recv, base):
        devs, slot = places(ins, outs)
        d = []
        for a, (src, dst) in enumerate(zip(ins, outs)):
            k, mine = base + per * a, dst.at[slot(devs["me"])]
            d.append(("local", _local(src, mine, send.at[k + 9])))
            d.append(("out", _remote(src, mine, send, recv, k, devs["nx"])))
            d.append(("out", _remote(src, mine, send, recv, k + 1, devs["ny"])))
            d.append(("out", _remote(src, mine, send, recv, k + 4, devs["sib"])))
            if a >= len(big):
                d.append(("out", _remote(src, mine, send, recv, k + 2, devs["dg"])))
        return d

    def finish(ins, outs, send, recv):
        devs, slot = places(ins, outs)
        sib_of = lambda dev: (dev[0], dev[1], 1 - dev[2])
        later = []

        def go(copy):
            copy.start()
            later.append(copy)

        for a, (src, dst) in enumerate(zip(ins, outs)):
            k, rows = per * a, src.shape[0]
            relay = a < len(big)
            half = rows // 2
            lo = lambda dev: dst.at[slot(dev)].at[pl.ds(0, half)]
            hi = lambda dev: dst.at[slot(dev)].at[pl.ds(half, rows - half)]
            whole = lambda dev: dst.at[slot(dev)]
            _remote(src, whole(devs["nx"]), send, recv, k, devs["nx"])().wait_recv()
            if relay:
                go(_remote(lo(devs["nx"]), lo(devs["nx"]), send, recv, k + 2, devs["ny"])())
            go(_remote(whole(devs["nx"]), whole(devs["nx"]), send, recv, k + 5, devs["sib"])())
            _remote(src, whole(devs["ny"]), send, recv, k + 1, devs["ny"])().wait_recv()
            if relay:
                go(_remote(hi(devs["ny"]), hi(devs["ny"]), send, recv, k + 3, devs["nx"])())
            go(_remote(whole(devs["ny"]), whole(devs["ny"]), send, recv, k + 6, devs["sib"])())
            if relay:
                _remote(lo(devs["dg"]), lo(devs["dg"]), send, recv, k + 2, devs["ny"])().wait_recv()
                go(_remote(lo(devs["dg"]), lo(devs["dg"]), send, recv, k + 7, devs["sib"])())
                _remote(hi(devs["dg"]), hi(devs["dg"]), send, recv, k + 3, devs["nx"])().wait_recv()
                go(_remote(hi(devs["dg"]), hi(devs["dg"]), send, recv, k + 8, devs["sib"])())
            else:
                _remote(src, whole(devs["dg"]), send, recv, k + 2, devs["dg"])().wait_recv()
                go(_remote(whole(devs["dg"]), whole(devs["dg"]), send, recv, k + 7, devs["sib"])())
        for a, (src, dst) in enumerate(zip(ins, outs)):
            k, rows = per * a, src.shape[0]
            half = rows // 2
            for j, dev in ((4, devs["me"]), (5, devs["nx"]), (6, devs["ny"])):
                theirs = dst.at[slot(sib_of(dev))]
                _remote(theirs, theirs, send, recv, k + j, devs["sib"])().wait_recv()
            far = dst.at[slot(sib_of(devs["dg"]))]
            if a < len(big):
                _remote(far.at[pl.ds(0, half)], far.at[pl.ds(0, half)], send, recv, k + 7, devs["sib"])().wait_recv()
                _remote(far.at[pl.ds(half, rows - half)], far.at[pl.ds(half, rows - half)], send, recv, k + 8,
                        devs["sib"])().wait_recv()
            else:
                _remote(far, far, send, recv, k + 7, devs["sib"])().wait_recv()
        for kind, make in first(ins, outs, send, recv, 0):
            (make().wait if kind == "local" else make().wait_send)()
        for copy in later:
            copy.wait_send()

    shapes = [jax.ShapeDtypeStruct((N_DEV,) + s.shape, s.dtype) for s in srcs]
    return _Ride(srcs, shapes, {}, per * n, first, finish)


def _call(body, *, name, grid, in_specs, out_specs, out_shape, args, sem, scratch=(), ride=None, base=None):
    params = pltpu.CompilerParams(dimension_semantics=sem, vmem_limit_bytes=VMEM_LIMIT)
    own_aliases = {}
    if base is not None:
        inner, n_host = body, len(args)
        body = lambda *refs: inner(*refs[:n_host], *refs[n_host + 1:])
        own_aliases[n_host] = base[1]
        args, in_specs = tuple(args) + (base[0],), list(in_specs) + [_ANY]
    if ride is None:
        res = pl.pallas_call(body, name=name, grid=grid, in_specs=in_specs, out_specs=out_specs,
                             out_shape=out_shape, scratch_shapes=list(scratch), input_output_aliases=own_aliases,
                             compiler_params=params)(*args)
        return list(res), []
    n_in, n_out, n_scr = len(args), len(out_shape), len(scratch)
    r_in, r_out = len(ride.ins), len(ride.out_shapes)

    def hosted(*refs):
        h_in, rin = refs[:n_in], refs[n_in:n_in + r_in]
        o0 = n_in + r_in
        h_out, rout = refs[o0:o0 + n_out], refs[o0 + n_out:o0 + n_out + r_out]
        s0 = o0 + n_out + r_out
        h_scr, send, recv = refs[s0:s0 + n_scr], refs[s0 + n_scr], refs[s0 + n_scr + 1]
        ids = [pl.program_id(i) for i in range(len(grid))]
        first = functools.reduce(lambda p, q: p & q, [i == 0 for i in ids])
        last = functools.reduce(lambda p, q: p & q, [i == n - 1 for i, n in zip(ids, grid)])

        @pl.when(first)
        def _():
            _start_all(ride.plan(rin, rout, send, recv, 0))

        body(*h_in, *h_out, *h_scr)

        @pl.when(last)
        def _():
            if ride.finish is None:
                _wait_all(ride.plan(rin, rout, send, recv, 0))
            else:
                ride.finish(rin, rout, send, recv)

    res = pl.pallas_call(
        hosted, name=name, grid=grid, in_specs=list(in_specs) + [_ANY] * r_in,
        out_specs=list(out_specs) + [_ANY] * r_out, out_shape=list(out_shape) + list(ride.out_shapes),
        scratch_shapes=list(scratch) + [pltpu.SemaphoreType.DMA((ride.nsem,)), pltpu.SemaphoreType.DMA((ride.nsem,))],
        input_output_aliases={**own_aliases, **{n_in + i: n_out + j for i, j in ride.aliases.items()}},
        compiler_params=params,
    )(*args, *ride.ins)
    return list(res[:n_out]), list(res[n_out:])


def _matmul(a, b, *, ta=False, tb=False, out_dtype=BF16, add=None, post=None, name, ride=None, tn_max=MM_TILE_MN):
    m = a.shape[1] if ta else a.shape[0]
    k = a.shape[0] if ta else a.shape[1]
    n = b.shape[0] if tb else b.shape[1]
    assert k == (b.shape[1] if tb else b.shape[0])
    tm, tn, tk = _tile(m, MM_TILE_MN, LANES), _tile(n, tn_max, LANES), _tile(k, MM_TILE_K, LANES)
    nk = k // tk
    dims = (((0 if ta else 1,), (1 if tb else 0,)), ((), ()))
    single = post is None
    if add is not None:
        post = (lambda r, t: (r + t,), [add], [out_dtype])
    elif post is None:
        post = (lambda r: (r,), [], [out_dtype])
    post_fn, extras, out_dtypes = post
    n_ex, n_o = len(extras), len(out_dtypes)

    def body(*refs):
        a_ref, b_ref = refs[:2]
        ex_refs, o_refs = refs[2:2 + n_ex], refs[2 + n_ex:2 + n_ex + n_o]

        def finish(r):
            for o_ref, v in zip(o_refs, post_fn(r, *[e[...].astype(F32) for e in ex_refs])):
                o_ref[...] = v.astype(o_ref.dtype)

        part = lax.dot_general(a_ref[...].astype(BF16), b_ref[...].astype(BF16), dims, preferred_element_type=F32)
        if nk == 1:
            finish(part)
            return
        acc = refs[-1]
        kk = pl.program_id(2)

        @pl.when(kk == 0)
        def _():
            acc[...] = part

        @pl.when((kk > 0) & (kk < nk - 1))
        def _():
            acc[...] += part

        @pl.when(kk == nk - 1)
        def _():
            finish(acc[...] + part)

    a_spec = (pl.BlockSpec((tk, tm), lambda i, j, kk: (kk, i)) if ta
              else pl.BlockSpec((tm, tk), lambda i, j, kk: (i, kk)))
    b_spec = (pl.BlockSpec((tn, tk), lambda i, j, kk: (j, kk)) if tb
              else pl.BlockSpec((tk, tn), lambda i, j, kk: (kk, j)))
    o_spec = pl.BlockSpec((tm, tn), lambda i, j, kk: (i, j))
    outs, rides = _call(
        body, name=name, grid=(m // tm, n // tn, nk),
        in_specs=[a_spec, b_spec] + [o_spec] * n_ex, out_specs=[o_spec] * n_o,
        out_shape=[jax.ShapeDtypeStruct((m, n), dt) for dt in out_dtypes], args=(a, b, *extras),
        scratch=[pltpu.VMEM((tm, tn), F32)] if nk > 1 else [], sem=("parallel", "parallel", "arbitrary"), ride=ride)
    return (outs[0] if single else outs), rides


def _rows_call(fn, *, rows, tr, row_ins, full_ins, row_outs, acc_outs, name, ride=None):
    nr, nf, no, na = len(row_ins), len(full_ins), len(row_outs), len(acc_outs)

    def body(*refs):
        vals = [r[...] for r in refs[:nr + nf]]
        outs, accs = fn(*vals)
        for r, v in zip(refs[nr + nf:nr + nf + no], outs):
            r[...] = v.astype(r.dtype)
        if na:
            @pl.when(pl.program_id(0) == 0)
            def _():
                for r in refs[nr + nf + no:]:
                    r[...] = jnp.zeros_like(r)
            for r, v in zip(refs[nr + nf + no:], accs):
                r[...] += v

    in_specs = [pl.BlockSpec((tr, w), functools.partial(lambda cb, i: (i, cb), cb)) for _, w, cb in row_ins]
    in_specs += [pl.BlockSpec(f.shape, lambda i: (0, 0)) for f in full_ins]
    out_specs = [pl.BlockSpec((tr, o[0]), lambda i: (i, 0)) for o in row_outs]
    out_specs += [pl.BlockSpec(s, lambda i: (0, 0)) for s in acc_outs]
    out_shape = [jax.ShapeDtypeStruct((rows, o[-1] if len(o) == 3 else o[0]), o[1]) for o in row_outs]
    out_shape += [jax.ShapeDtypeStruct(s, F32) for s in acc_outs]
    return _call(body, name=name, grid=(rows // tr,), in_specs=in_specs, out_specs=out_specs, out_shape=out_shape,
                 args=tuple(a for a, _, _ in row_ins) + tuple(full_ins), sem=("arbitrary",), ride=ride)


def _cols_call(fn, *, rows, cols, cw, col_ins, par_ins, col_outs, par_outs, name, ride=None, into=None):
    nc, npar = len(col_ins), len(par_ins)

    def body(*refs):
        vals = [r[...] for r in refs[:nc + npar]]
        outs, pouts = fn(*vals)
        for r, v in zip(refs[nc + npar:], tuple(outs) + tuple(pouts)):
            r[...] = v.astype(r.dtype)

    in_specs = [pl.BlockSpec((rows, cw), functools.partial(lambda off, j: (0, off + j), off)) for _, off in col_ins]
    in_specs += [pl.BlockSpec((p.shape[0], cw), functools.partial(lambda off, j: (0, off + j), off))
                 for p, off in par_ins]
    out_specs = [pl.BlockSpec((rows, cw), lambda j: (0, j)) for _ in col_outs]
    out_specs += [pl.BlockSpec((k, cw), lambda j: (0, j)) for k in par_outs]
    out_shape = [jax.ShapeDtypeStruct((rows, cols), dt) for dt in col_outs]
    out_shape += [jax.ShapeDtypeStruct((k, cols), F32) for k in par_outs]
    if into is not None:
        out_specs[0] = pl.BlockSpec((rows, cw), lambda j: (0, into[1] + j))
        out_shape[0] = jax.ShapeDtypeStruct(into[0].shape, into[0].dtype)
    return _call(body, name=name, grid=(cols // cw,), in_specs=in_specs, out_specs=out_specs, out_shape=out_shape,
                 args=tuple(a for a, _ in col_ins) + tuple(p for p, _ in par_ins), sem=("arbitrary",), ride=ride,
                 base=None if into is None else (into[0], 0))


def _sigmoid(v):
    return 1.0 / (1.0 + jnp.exp(-v))


def _softplus(v):
    return jnp.maximum(v, 0.0) + jnp.log(1.0 + jnp.exp(-jnp.abs(v)))


def _rms(v, g):
    return v * lax.rsqrt(jnp.mean(v * v, axis=-1, keepdims=True) + EPS) * g


def _shift_down(v, s, row):
    return jnp.where(row >= s, pltpu.roll(v, s, 0), 0.0)


def _shift_up(v, s, row):
    n = v.shape[0]
    return jnp.where(row < n - s, pltpu.roll(v, n - s, 0), 0.0)


def _causal_conv(u, w, row):
    k_taps = w.shape[0]
    acc = u * w[k_taps - 1:k_taps, :]
    for k in range(k_taps - 1):
        acc = acc + _shift_down(u, k_taps - 1 - k, row) * w[k:k + 1, :]
    return acc


def _causal_conv_bwd(u, dy, w, row):
    k_taps = w.shape[0]
    tap = lax.broadcasted_iota(jnp.int32, w.shape, 0)
    du = dy * w[k_taps - 1:k_taps, :]
    dw = jnp.where(tap == k_taps - 1, jnp.sum(dy * u, axis=0, keepdims=True), 0.0)
    for k in range(k_taps - 1):
        s = k_taps - 1 - k
        du = du + _shift_up(dy, s, row) * w[k:k + 1, :]
        dw = dw + jnp.where(tap == k, jnp.sum(dy * _shift_down(u, s, row), axis=0, keepdims=True), 0.0)
    return du, dw


def _conv_silu_fwd(u, w, b):
    u = u.astype(F32)
    row = lax.broadcasted_iota(jnp.int32, u.shape, 0)
    pre = _causal_conv(u, w, row) + b
    return (pre * _sigmoid(pre),), ()


def _conv_silu_bwd(u, dy, w, b):
    u = u.astype(F32)
    dy = dy.astype(F32)
    row = lax.broadcasted_iota(jnp.int32, u.shape, 0)
    pre = _causal_conv(u, w, row) + b
    s = _sigmoid(pre)
    dpre = dy * (s * (1.0 + pre * (1.0 - s)))
    du, dw = _causal_conv_bwd(u, dpre, w, row)
    return (du,), (dw, jnp.sum(dpre, axis=0, keepdims=True))


def _shortconv_fwd(gb, gc, u, w):
    gb, gc, u = gb.astype(F32), gc.astype(F32), u.astype(F32)
    row = lax.broadcasted_iota(jnp.int32, u.shape, 0)
    return (gb * _causal_conv(gc * u, w, row),), ()


def _shortconv_bwd(gb, gc, u, dy, w):
    gb, gc, u, dy = gb.astype(F32), gc.astype(F32), u.astype(F32), dy.astype(F32)
    row = lax.broadcasted_iota(jnp.int32, u.shape, 0)
    v = gc * u
    dgb = dy * _causal_conv(v, w, row)
    dv, dw = _causal_conv_bwd(v, dy * gb, w, row)
    return (dgb, dv * u, dv * gc), (dw,)


def _split3(v):
    hi = v.astype(BF16)
    r1 = v - hi.astype(F32)
    mid = r1.astype(BF16)
    lo = (r1 - mid.astype(F32)).astype(BF16)
    return hi, mid, lo


def _exact_dot(v, m01, dims, v_is_lhs):
    def one(p):
        return (lax.dot_general(p, m01, dims, preferred_element_type=F32) if v_is_lhs
                else lax.dot_general(m01, p, dims, preferred_element_type=F32))
    hi, mid, lo = _split3(v)
    return (one(lo) + one(mid)) + one(hi)


_NN = (((1,), (0,)), ((), ()))
_NT = (((1,), (1,)), ((), ()))
_TN = (((0,), (0,)), ((), ()))


@functools.partial(jax.custom_vjp, nondiff_argnums=(2,))
def _bdot(a, b, dims):
    return lax.dot_general(a.astype(BF16), b.astype(BF16), dims, preferred_element_type=F32)


def _bdot_fwd(a, b, dims):
    return _bdot(a, b, dims), (a, b)


def _bdot_bwd(dims, res, g):
    a, b = res
    ab, bb, gb = a.astype(BF16), b.astype(BF16), g.astype(BF16)
    dot = lambda p, q, d: lax.dot_general(p, q, d, preferred_element_type=F32)
    if dims == _NN:
        da, db = dot(gb, bb, _NT), dot(ab, gb, _TN)
    elif dims == _NT:
        da, db = dot(gb, bb, _NN), dot(gb, ab, _TN)
    else:
        da, db = dot(bb, gb, _NT), dot(ab, gb, _NN)
    return da.astype(a.dtype), db.astype(b.dtype)


_bdot.defvjp(_bdot_fwd, _bdot_bwd)


@jax.custom_vjp
def _cumsum_rows(tril, v):
    return _exact_dot(v, tril, _NN, False)


def _cumsum_rows_fwd(tril, v):
    return _cumsum_rows(tril, v), tril


def _cumsum_rows_bwd(tril, ct):
    return None, _exact_dot(ct, tril, _TN, False)


_cumsum_rows.defvjp(_cumsum_rows_fwd, _cumsum_rows_bwd)


@jax.custom_vjp
def _cumsum_lanes(tril, v):
    return _exact_dot(v, tril, _NT, True)


def _cumsum_lanes_fwd(tril, v):
    return _cumsum_lanes(tril, v), tril


def _cumsum_lanes_bwd(tril, ct):
    return None, _exact_dot(ct, tril, _NN, True)


_cumsum_lanes.defvjp(_cumsum_lanes_fwd, _cumsum_lanes_bwd)


@jax.custom_vjp
def _expand(e01, v):
    return _exact_dot(v, e01, _NN, True)


def _expand_fwd(e01, v):
    return _expand(e01, v), e01


def _expand_bwd(e01, ct):
    return None, _exact_dot(ct, e01, _NT, True)


_expand.defvjp(_expand_fwd, _expand_bwd)


def _causal_mask(n):
    li = lax.broadcasted_iota(jnp.int32, (n, n), 0)
    si = lax.broadcasted_iota(jnp.int32, (n, n), 1)
    return si <= li


def _dt_prep(dtc, dtr, bias_r, bias_c, alog_r, alog_c):
    dt_c = _softplus(dtc + bias_r)
    dt_r = _softplus(dtr + bias_c)
    tril = jnp.where(_causal_mask(dtc.shape[0]), 1.0, 0.0).astype(BF16)
    cs_c = _cumsum_rows(tril, dt_c * (-jnp.exp(alog_r)))
    cs_r = _cumsum_lanes(tril, dt_r * (-jnp.exp(alog_c)))
    return dt_c, cs_c, cs_r


def _ssd_chunk(r_heads, xs, bg, cg, dt_c, cs_c, cs_rg, e01, dskip_e, hp):
    l_len, rp = xs.shape
    p = rp // r_heads
    causal = _causal_mask(l_len)
    lane_head = lax.broadcasted_iota(jnp.int32, (1, rp), 1) // p
    dt_e = _expand(e01, dt_c)
    cs_e = _expand(e01, cs_c)
    cl_e = cs_e[l_len - 1:l_len, :]
    x = xs * dt_e
    cb = _bdot(cg, bg, _NT)
    ms, xm = [], []
    for r in range(r_heads):
        seg = cs_e[:, r * p:r * p + 1] - cs_rg[r:r + 1, :]
        decay = jnp.exp(jnp.where(causal, seg, -1e30))
        ms.append((cb * decay).astype(BF16))
        xm.append(jnp.where(lane_head == r, x, 0.0).astype(BF16))
    y_diag = _bdot(jnp.concatenate(ms, axis=1), jnp.concatenate(xm, axis=0), _NN)
    y_off = _bdot(cg, hp, _NN) * jnp.exp(cs_e)
    states = _bdot(bg, x * jnp.exp(cl_e - cs_e), _TN)
    h_next = hp * jnp.exp(cl_e) + states
    y = y_diag + y_off + dskip_e * xs
    return y, h_next


def _ssd_dt(dtc, dtr, small, cots=None):
    t_len, heads = dtc.shape[0], dtr.shape[0]
    nc = t_len // CHUNK
    col = pl.BlockSpec((CHUNK, LANES), lambda c: (c, 0))
    row = pl.BlockSpec((heads, CHUNK), lambda c: (0, c))
    full = [pl.BlockSpec(s.shape, lambda c: (0, 0)) for s in small]
    shapes = [jax.ShapeDtypeStruct((t_len, LANES), F32), jax.ShapeDtypeStruct((t_len, LANES), F32),
              jax.ShapeDtypeStruct((heads, t_len), F32)]
    if cots is None:
        def body(dtc_ref, dtr_ref, br, bc, ar, ac, dt_ref, csc_ref, csr_ref):
            dt_ref[...], csc_ref[...], csr_ref[...] = _dt_prep(dtc_ref[...], dtr_ref[...], br[...], bc[...],
                                                                ar[...], ac[...])
        return _call(body, name="ssd_dt", grid=(nc,), in_specs=[col, row] + full, out_specs=[col, col, row],
                     out_shape=shapes, args=(dtc, dtr, *small), sem=("parallel",))[0]

    g_dt, g_csc, g_csr, ddk, e01 = cots

    def body(dtc_ref, dtr_ref, br, bc, ar, ac, g_dt_ref, g_csc_ref, g_csr_ref, ddk_ref, e_ref,
             ddtc_ref, ddtr_ref, *dsmall):
        _, vjp = jax.vjp(_dt_prep, dtc_ref[...], dtr_ref[...], br[...], bc[...], ar[...], ac[...])
        grads = vjp((g_dt_ref[...], g_csc_ref[...], g_csr_ref[...]))
        ddtc_ref[...], ddtr_ref[...] = grads[0], grads[1]
        ddk8 = jnp.broadcast_to(ddk_ref[...], (8, ddk_ref.shape[1]))
        dskip = _exact_dot(ddk8, e_ref[...], _NT, True)[0:1, :]

        @pl.when(pl.program_id(0) == 0)
        def _():
            for r in dsmall:
                r[...] = jnp.zeros_like(r)

        for r, gr in zip(dsmall, tuple(grads[2:]) + (dskip,)):
            r[...] += gr

    acc = list(small) + [small[0]]
    return _call(body, name="d_ssd_dt", grid=(nc,),
                 in_specs=[col, row] + full + [col, col, row, pl.BlockSpec((None, 1, e01.shape[1]), lambda c: (c, 0, 0)),
                                               pl.BlockSpec(e01.shape, lambda c: (0, 0))],
                 out_specs=[col, row] + [pl.BlockSpec(s.shape, lambda c: (0, 0)) for s in acc],
                 out_shape=[shapes[0], shapes[2]] + [jax.ShapeDtypeStruct(s.shape, F32) for s in acc],
                 args=(dtc, dtr, *small, g_dt, g_csc, g_csr, ddk, e01), sem=("arbitrary",))[0]


def _ssd_specs(t_len, d_ssm, r_heads, reverse):
    rp = r_heads * HEADDIM
    nc = t_len // CHUNK
    per = next(p for p in (SSD_CHUNKS_PER_STEP, 2, 1) if nc % p == 0)
    ns, rows, gs = nc // per, per * CHUNK, SSD_GROUPS_PER_STEP
    cidx = (lambda c: ns - 1 - c) if reverse else (lambda c: c)
    b_off = d_ssm // (N_STATE * gs)
    specs = dict(
        xs=pl.BlockSpec((rows, gs * rp), lambda c, g: (cidx(c), g)),
        b=pl.BlockSpec((rows, gs * N_STATE), lambda c, g: (cidx(c), b_off + g)),
        c=pl.BlockSpec((rows, gs * N_STATE), lambda c, g: (cidx(c), b_off + N_GROUPS // gs + g)),
        grad_bc=pl.BlockSpec((rows, gs * N_STATE), lambda c, g: (cidx(c), g)),
        col=pl.BlockSpec((rows, LANES), lambda c, g: (cidx(c), 0)),
        csr=pl.BlockSpec((gs, r_heads, rows), lambda c, g: (g, 0, cidx(c))),
        e01=pl.BlockSpec((LANES, gs * rp), lambda c, g: (0, g)),
        dskip=pl.BlockSpec((1, gs * rp), lambda c, g: (0, g)),
        hprev=pl.BlockSpec((per, gs, N_STATE, rp), lambda c, g: (cidx(c), g, 0, 0)),
        ddk=pl.BlockSpec((per, 1, gs * rp), lambda c, g: (cidx(c), 0, g)),
    )
    return specs, nc, ns, per, rp


def _ssd_fwd(xbc, dt_c, cs_c, cs_r3, e01, dskip_e, *, d_ssm, r_heads, ride=None):
    t_len = xbc.shape[0]
    sp, nc, ns, per, rp = _ssd_specs(t_len, d_ssm, r_heads, False)

    def body(xs_ref, b_ref, c_ref, dt_ref, csc_ref, csr_ref, e_ref, dk_ref, y_ref, hprev_ref, h_ref):
        c, gp = pl.program_id(0), pl.program_id(1)
        groups = [gp * SSD_GROUPS_PER_STEP + gi for gi in range(SSD_GROUPS_PER_STEP)]

        @pl.when(c == 0)
        def _():
            for g in groups:
                h_ref[g] = jnp.zeros((N_STATE, rp), F32)

        hp = [h_ref[g] for g in groups]
        for s in range(per):
            r = pl.ds(s * CHUNK, CHUNK)
            for gi in range(SSD_GROUPS_PER_STEP):
                cols, bc = pl.ds(gi * rp, rp), pl.ds(gi * N_STATE, N_STATE)
                hprev_ref[s, gi] = hp[gi]
                y, hp[gi] = _ssd_chunk(r_heads, xs_ref[r, cols].astype(F32), b_ref[r, bc].astype(F32),
                                       c_ref[r, bc].astype(F32), dt_ref[r, :], csc_ref[r, :], csr_ref[gi, :, r],
                                       e_ref[:, cols], dk_ref[:, cols], hp[gi])
                y_ref[r, cols] = y
        for gi, g in enumerate(groups):
            h_ref[g] = hp[gi]

    return _call(
        body, name="ssd_fwd", grid=(ns, N_GROUPS // SSD_GROUPS_PER_STEP),
        in_specs=[sp["xs"], sp["b"], sp["c"], sp["col"], sp["col"], sp["csr"], sp["e01"], sp["dskip"]],
        out_specs=[sp["xs"], sp["hprev"]],
        out_shape=[jax.ShapeDtypeStruct((t_len, d_ssm), F32),
                   jax.ShapeDtypeStruct((nc, N_GROUPS, N_STATE, rp), F32)],
        args=(xbc, xbc, xbc, dt_c, cs_c, cs_r3, e01, dskip_e), scratch=[pltpu.VMEM((N_GROUPS, N_STATE, rp), F32)],
        sem=("arbitrary", "arbitrary"), ride=ride)


def _ssd_bwd(xbc, dt_c, cs_c, cs_r3, e01, dskip_e, hprev, dy, *, d_ssm, r_heads, ride=None):
    t_len = xbc.shape[0]
    sp, nc, ns, per, rp = _ssd_specs(t_len, d_ssm, r_heads, True)

    def body(xs_ref, b_ref, c_ref, dt_ref, csc_ref, csr_ref, e_ref, dk_ref, hprev_ref, dy_ref,
             dxs_ref, db_ref, dc_ref, ddt_ref, dcsc_ref, dcsr_ref, ddk_ref, dh_ref):
        c, gp = pl.program_id(0), pl.program_id(1)
        groups = [gp * SSD_GROUPS_PER_STEP + gi for gi in range(SSD_GROUPS_PER_STEP)]

        @pl.when(gp == 0)
        def _():
            ddt_ref[...] = jnp.zeros_like(ddt_ref)
            dcsc_ref[...] = jnp.zeros_like(dcsc_ref)

        @pl.when(c == 0)
        def _():
            for g in groups:
                dh_ref[g] = jnp.zeros((N_STATE, rp), F32)

        dh = [dh_ref[g] for g in groups]
        for s in reversed(range(per)):
            r = pl.ds(s * CHUNK, CHUNK)
            ddt_sum, dcsc_sum = ddt_ref[r, :], dcsc_ref[r, :]
            for gi in range(SSD_GROUPS_PER_STEP):
                cols, bc = pl.ds(gi * rp, rp), pl.ds(gi * N_STATE, N_STATE)
                e01 = e_ref[:, cols]
                fn = lambda xs, bg, cg, dt, csc, csr, dk, hp: _ssd_chunk(r_heads, xs, bg, cg, dt, csc, csr, e01, dk, hp)
                _, vjp = jax.vjp(fn, xs_ref[r, cols].astype(F32), b_ref[r, bc].astype(F32), c_ref[r, bc].astype(F32),
                                 dt_ref[r, :], csc_ref[r, :], csr_ref[gi, :, r], dk_ref[:, cols], hprev_ref[s, gi])
                dxs, dbg, dcg, ddt, dcsc, dcsr, ddk, dh[gi] = vjp((dy_ref[r, cols], dh[gi]))
                dxs_ref[r, cols] = dxs.astype(dxs_ref.dtype)
                db_ref[r, bc] = dbg.astype(db_ref.dtype)
                dc_ref[r, bc] = dcg.astype(dc_ref.dtype)
                ddt_sum, dcsc_sum = ddt_sum + ddt, dcsc_sum + dcsc
                dcsr_ref[gi, :, r] = dcsr
                ddk_ref[s, :, cols] = ddk
            ddt_ref[r, :], dcsc_ref[r, :] = ddt_sum, dcsc_sum
        for gi, g in enumerate(groups):
            dh_ref[g] = dh[gi]

    n_bc = N_GROUPS * N_STATE
    return _call(
        body, name="ssd_bwd", grid=(ns, N_GROUPS // SSD_GROUPS_PER_STEP),
        in_specs=[sp["xs"], sp["b"], sp["c"], sp["col"], sp["col"], sp["csr"], sp["e01"], sp["dskip"], sp["hprev"],
                  sp["xs"]],
        out_specs=[sp["xs"], sp["grad_bc"], sp["grad_bc"], sp["col"], sp["col"], sp["csr"], sp["ddk"]],
        out_shape=[jax.ShapeDtypeStruct((t_len, d_ssm), BF16), jax.ShapeDtypeStruct((t_len, n_bc), BF16),
                   jax.ShapeDtypeStruct((t_len, n_bc), BF16), jax.ShapeDtypeStruct(dt_c.shape, F32),
                   jax.ShapeDtypeStruct(cs_c.shape, F32), jax.ShapeDtypeStruct(cs_r3.shape, F32),
                   jax.ShapeDtypeStruct((nc, 1, d_ssm), F32)],
        args=(xbc, xbc, xbc, dt_c, cs_c, cs_r3, e01, dskip_e, hprev, dy),
        scratch=[pltpu.VMEM((N_GROUPS, N_STATE, rp), F32)], sem=("arbitrary", "arbitrary"), ride=ride)


def _chip_sum(src, sib, *, name):
    rows, cols = src.shape[1:]
    tr = _tile(rows, 256, BF16_ROWS)
    core = lax.axis_index("c").astype(jnp.int32).reshape(1)

    def body(c_ref, a_ref, b_ref, o_ref):
        o_ref[...] = (a_ref[...].astype(F32) + b_ref[...].astype(F32)).astype(o_ref.dtype)

    grid_spec = pltpu.PrefetchScalarGridSpec(
        num_scalar_prefetch=1, grid=(N_CHIPS, rows // tr),
        in_specs=[pl.BlockSpec((None, tr, cols), lambda q, i, c_ref: (2 * q + c_ref[0], i, 0)),
                  pl.BlockSpec((None, tr, cols), lambda q, i, c_ref: (q, i, 0))],
        out_specs=pl.BlockSpec((None, tr, cols), lambda q, i, c_ref: (q, i, 0)))
    return pl.pallas_call(
        body, name=name, grid_spec=grid_spec, out_shape=jax.ShapeDtypeStruct(sib.shape, sib.dtype),
        compiler_params=pltpu.CompilerParams(dimension_semantics=("parallel", "parallel"), vmem_limit_bytes=VMEM_LIMIT),
    )(core, src, sib)


def _adamw(w, g, m, v):
    m = ADAM_B1 * m + (1.0 - ADAM_B1) * g
    v = ADAM_B2 * v + (1.0 - ADAM_B2) * (g * g)
    m_hat = m / (1.0 - ADAM_B1 ** ADAM_STEP)
    v_hat = v / (1.0 - ADAM_B2 ** ADAM_STEP)
    delta = -ADAM_LR * (m_hat / (jnp.sqrt(v_hat) + ADAM_EPS) + ADAM_WD * w)
    return delta, m, v


def _reduce_adamw(parts, w, m, v, *, name):
    n_parts = parts.shape[0]
    rows, cols = w.shape
    tr = _tile(rows, 128, BF16_ROWS)

    def body(p_ref, w_ref, m_ref, v_ref, g_ref, d_ref, mo_ref, vo_ref):
        g = p_ref[0].astype(F32)
        for k in range(1, n_parts):
            g = g + p_ref[k].astype(F32)
        delta, mn, vn = _adamw(w_ref[...], g, m_ref[...], v_ref[...])
        g_ref[...] = g
        d_ref[...] = delta
        mo_ref[...] = mn
        vo_ref[...] = vn

    spec = pl.BlockSpec((tr, cols), lambda i: (i, 0))
    outs, _ = _call(
        body, name=name, grid=(rows // tr,),
        in_specs=[pl.BlockSpec((n_parts, tr, cols), lambda i: (0, i, 0)), spec, spec, spec],
        out_specs=[spec] * 4, out_shape=[jax.ShapeDtypeStruct((rows, cols), F32)] * 4,
        args=(parts, w, m, v), sem=("parallel",))
    return outs


def _move_rows(src, src_row, name, extra=None, extra_row=None):
    rb, n_out, cols = ROW_BLOCK, len(src_row), src.shape[1]
    assert n_out % rb == 0 and src.shape[0] % rb == 0 and src.shape[0] // rb >= 3
    n_blocks, max_b0, seg_cap = n_out // rb, src.shape[0] // rb - 3, 4

    def segments(rows_of, lo):
        segs, r = [], 0
        while r < rb:
            if rows_of[r] < 0:
                r += 1
                continue
            e = r
            while e + 1 < rb and rows_of[e + 1] == rows_of[e] + 1:
                e += 1
            segs.append((r, e + 1, rows_of[r] - r - lo))
            r = e + 1
        assert len(segs) <= seg_cap
        return segs + [(0, 0, 0)] * (seg_cap - len(segs))

    table = []
    for j in range(n_blocks):
        rows_j = list(src_row[j * rb:(j + 1) * rb])
        valid = [v for v in rows_j if v >= 0]
        b0 = min(max((min(valid) // rb) if valid else 0, 0), max_b0)
        assert not valid or max(valid) < (b0 + 3) * rb
        row = [b0] + [v for seg in segments(rows_j, b0 * rb) for v in seg]
        extra_j = [] if extra is None else list(extra_row[j * rb:(j + 1) * rb])
        if extra is not None:
            row += [v for seg in segments(extra_j, 0) for v in seg]
        need_third = bool(valid) and max(valid) >= (b0 + 2) * rb
        third = b0 + 2 if need_third or not table else table[-1][-1]
        row += [int(need_third), int(any(v >= 0 for v in extra_j)), third]
        table.append(row)
    flag_third, flag_extra, col_third = len(table[0]) - 3, len(table[0]) - 2, len(table[0]) - 1
    table = jnp.asarray(table, jnp.int32)

    def select(tbl_ref, j, first, width, col0=0):
        r = lax.broadcasted_iota(jnp.int32, (rb, width), 0)
        c = lax.broadcasted_iota(jnp.int32, (rb, width), 1) + col0
        hit = jnp.zeros((rb, width), jnp.bool_)
        for s in range(seg_cap):
            lo, hi, off = (tbl_ref[j, first + 3 * s + i] for i in range(3))
            hit = hit | ((r >= lo) & (r < hi) & (c == r + off))
        return jnp.where(hit, 1.0, 0.0).astype(BF16)

    def body(tbl_ref, *refs):
        o_ref = refs[-1]
        j = pl.program_id(0)
        sel = select(tbl_ref, j, 1, 2 * rb)
        pick = lambda m, b: lax.dot_general(m, refs[b][...], _NN, preferred_element_type=F32)
        o_ref[...] = (pick(sel[:, :rb], 0) + pick(sel[:, rb:], 1)).astype(o_ref.dtype)

        @pl.when(tbl_ref[j, flag_third] == 1)
        def _():
            o_ref[...] = (o_ref[...].astype(F32) + pick(select(tbl_ref, j, 1, rb, 2 * rb), 2)).astype(o_ref.dtype)

        if extra is not None:
            @pl.when(tbl_ref[j, flag_extra] == 1)
            def _():
                more = lax.dot_general(select(tbl_ref, j, 1 + 3 * seg_cap, extra.shape[0]), refs[3][...], _NN,
                                       preferred_element_type=F32)
                o_ref[...] = (o_ref[...].astype(F32) + more).astype(o_ref.dtype)

    in_specs = [pl.BlockSpec((rb, cols), functools.partial(lambda b, j, tbl: (tbl[j, 0] + b, 0), b)) for b in range(2)]
    in_specs.append(pl.BlockSpec((rb, cols), lambda j, tbl: (tbl[j, col_third], 0)))
    args = [src, src, src]
    if extra is not None:
        in_specs.append(pl.BlockSpec(extra.shape, lambda j, tbl: (0, 0)))
        args.append(extra)
    grid_spec = pltpu.PrefetchScalarGridSpec(num_scalar_prefetch=1, grid=(n_blocks,), in_specs=in_specs,
                                             out_specs=pl.BlockSpec((rb, cols), lambda j, tbl: (j, 0)))
    return pl.pallas_call(
        body, name=name, grid_spec=grid_spec, out_shape=jax.ShapeDtypeStruct((n_out, cols), src.dtype),
        compiler_params=pltpu.CompilerParams(dimension_semantics=("parallel",), vmem_limit_bytes=VMEM_LIMIT),
    )(table, *args)


def _cols_of(g):
    return jnp.transpose(g, (1, 0, 2)).reshape(g.shape[1], -1)


def _pad_to(a, rows, cols):
    return jnp.pad(a, ((0, rows - a.shape[0]), (0, cols - a.shape[1])))


def kernel(x, norm_mix_g, w_in, ssm_conv_w, ssm_conv_b, ssm_dt_bias, ssm_A_log, ssm_D, ssm_norm_g, sc_conv_w, w_out, norm_ffn_g, w_gate, w_up, w_down, norm_final_g, loss_target, m_norm_mix_g, m_w_in, m_ssm_conv_w, m_ssm_conv_b, m_ssm_dt_bias, m_ssm_A_log, m_ssm_D, m_ssm_norm_g, m_sc_conv_w, m_w_out, m_norm_ffn_g, m_w_gate, m_w_up, m_w_down, m_norm_final_g, v_norm_mix_g, v_w_in, v_ssm_conv_w, v_ssm_conv_b, v_ssm_dt_bias, v_ssm_A_log, v_ssm_D, v_ssm_norm_g, v_sc_conv_w, v_w_out, v_norm_ffn_g, v_w_gate, v_w_up, v_w_down, v_norm_final_g):
    t_len, d = x.shape[1], x.shape[2]
    heads = d // HEADDIM
    r_heads = heads // N_GROUPS
    d_xbc = d + 2 * N_GROUPS * N_STATE
    ff_s = w_down.shape[1]
    ff = ff_s * N_DEV
    off_xbc, off_dt = d, d + d_xbc
    off_cb = off_dt + heads
    d_in = off_cb + 3 * d
    in_s = d_in // N_DEV
    in_p = -(-in_s // (2 * BF16_ROWS)) * (2 * BF16_ROWS)
    w_main = 4 * d + d_xbc
    me = 4 * lax.axis_index("x") + 2 * lax.axis_index("y") + lax.axis_index("c")

    x2 = x[0]
    target = loss_target[0]

    tpose = lambda a: jnp.transpose(a[0])
    win_s = _pad_to(tpose(w_in).astype(BF16), in_p, d)
    wg_s, wu_s = tpose(w_gate).astype(BF16), tpose(w_up).astype(BF16)
    wo_s, wd_s = w_out[0].astype(BF16), w_down[0].astype(BF16)
    small_w = jnp.concatenate([_pad_to(ssm_conv_w[0], K_SSM, d_xbc // N_DEV),
                               _pad_to(sc_conv_w[0], K_SC + 1, d_xbc // N_DEV)], axis=0)

    g1, g2, g3 = norm_mix_g, norm_ffn_g, norm_final_g.reshape(1, d)
    gs = ssm_norm_g
    small = [_pad_to(ssm_dt_bias, 1, LANES), ssm_dt_bias.reshape(heads, 1), _pad_to(ssm_A_log, 1, LANES),
             ssm_A_log.reshape(heads, 1)]
    e01 = (lax.broadcasted_iota(jnp.int32, (LANES, d), 1) // HEADDIM
           == lax.broadcasted_iota(jnp.int32, (LANES, d), 0)).astype(BF16)
    dskip_e = jnp.repeat(ssm_D, HEADDIM, axis=1)
    tr = _tile(t_len, 256, 8)
    tr_wide = _tile(t_len, 512, 8)
    tr_ff = _tile(t_len, 128, 8)
    cw = LANES
    slab = lambda col: col // cw

    (n1,), (gin, gsm) = _rows_call(lambda v, g: ((_rms(v, g),), ()), rows=t_len, tr=tr_wide, row_ins=[(x2, d, 0)],
                                   full_ins=[g1], row_outs=[(d, BF16)], acc_outs=[], name="norm_mix",
                                   ride=_gather_relayed([win_s], [small_w]))
    in_pieces = []
    for k in range(N_DEV):
        for a, b, dst, shift in ((0, off_dt, 0, 0), (off_dt, off_cb, 1, -off_dt), (off_cb, d_in, 0, -heads)):
            s, e = max(k * in_s, a), min((k + 1) * in_s, b)
            if s < e:
                in_pieces.append((k, s - k * in_s, e - s, dst, s + shift))
    ref_row = lambda t: t if t < off_dt else t + heads
    wtm = _move_rows(gin.reshape(N_DEV * in_p, d),
                     [(ref_row(t) // in_s) * in_p + ref_row(t) % in_s for t in range(w_main)], "place_w_in")
    wtdt = jnp.zeros((LANES, d), BF16)
    for k, r0, n, dst, d0 in in_pieces:
        if dst == 1:
            wtdt = lax.dynamic_update_slice(wtdt, gin[k, r0:r0 + n], (d0, 0))
    cw_ssm = _cols_of(gsm[:, :K_SSM, :])
    cw_sc = _cols_of(gsm[:, K_SSM:K_SSM + K_SC, :d // N_DEV])

    proj, (go_1, gg_1) = _matmul(n1, wtm, tb=True, out_dtype=BF16, name="proj_main",
                                 ride=_gather_chips([wo_s, wg_s]))
    dt_raw, _ = _matmul(n1, wtdt, tb=True, out_dtype=F32, name="proj_dt")
    dt_raw_t = jnp.transpose(dt_raw[:, :heads])
    (xbc,), (go, gg) = _cols_call(_conv_silu_fwd, rows=t_len, cols=d_xbc, cw=cw, col_ins=[(proj, slab(off_xbc))],
                                  par_ins=[(cw_ssm, 0), (ssm_conv_b, 0)], col_outs=[BF16], par_outs=[],
                                  name="ssm_conv", ride=_gather_sibling([go_1, gg_1]))
    dt_c, cs_c, cs_r = _ssd_dt(dt_raw, dt_raw_t, small)
    cs_r3 = cs_r.reshape(N_GROUPS, r_heads, t_len)
    up_cut = int(ff_s * W_UP_GATHER_SPLIT) // BF16_ROWS * BF16_ROWS
    down_cut = int(ff_s * W_DOWN_GATHER_SPLIT) // BF16_ROWS * BF16_ROWS
    half_cut = ff_s // 2 // BF16_ROWS * BF16_ROWS
    (y_ssd, hprev), (gu_1,) = _ssd_fwd(xbc, dt_c, cs_c, cs_r3, e01, dskip_e, d_ssm=d, r_heads=r_heads,
                                       ride=_gather_chips([wu_s], rows=(0, up_cut)))

    def gate_norm(y, z, g):
        z = z.astype(F32)
        return _rms(y * (z * _sigmoid(z)), g)

    (y_mix,), _ = _rows_call(lambda y, z, g: ((gate_norm(y, z, g),), ()), rows=t_len, tr=tr_wide,
                             row_ins=[(y_ssd, d, 0), (proj, d, 0)], full_ins=[gs], row_outs=[(d, BF16, 2 * d)],
                             acc_outs=[], name="ssm_gate_norm")
    wgt, wo = gg.reshape(ff, d), go.reshape(2 * d, d)
    sc0 = slab(d + d_xbc)
    (y_mix,), _ = _cols_call(_shortconv_fwd, rows=t_len, cols=d, cw=cw,
                             col_ins=[(proj, sc0), (proj, sc0 + slab(d)), (proj, sc0 + 2 * slab(d))],
                             par_ins=[(cw_sc, 0)], col_outs=[BF16], par_outs=[], name="shortconv",
                             into=(y_mix, slab(d)))
    h1, (gu_1, gd_1) = _matmul(y_mix, wo, out_dtype=F32, add=x2, name="out_proj", ride=_merge(
        _gather_chips([wu_s], rows=(up_cut, ff_s - up_cut), into=[gu_1]), _gather_chips([wd_s], rows=(0, down_cut))))
    (n2,), _ = _rows_call(lambda v, g: ((_rms(v, g),), ()), rows=t_len, tr=tr_wide, row_ins=[(h1, d, 0)],
                          full_ins=[g2], row_outs=[(d, BF16)], acc_outs=[], name="norm_ffn")
    g_ff, (gd_1, gu) = _matmul(n2, wgt, tb=True, out_dtype=BF16, name="ffn_gate", ride=_merge(
        _gather_chips([wd_s], rows=(down_cut, ff_s - down_cut), into=[gd_1]), _gather_sibling([gu_1])))
    wut = gu.reshape(ff, d)
    (u_ff, a_ff), (gd,) = _matmul(n2, wut, tb=True, name="ffn_up", ride=_gather_sibling([gd_1]),
                                  post=(lambda uv, gv: (uv, gv * _sigmoid(gv) * uv), [g_ff], [BF16, BF16]),
                                  tn_max=MM_TILE_N_POST)
    wd = gd.reshape(ff, d)
    h2, _ = _matmul(a_ff, wd, out_dtype=F32, add=h1, name="ffn_down")

    def head(hv, tv, g):
        def f(hh, gg_):
            e = _rms(hh, gg_) - tv
            return (0.5 / d) * jnp.sum(e * e)
        val, (dh, dg) = jax.value_and_grad(f, argnums=(0, 1))(hv, g)
        return (dh, dh), (jnp.full((1, LANES), val, F32), dg)

    (dh2, dh2_b, loss_acc, dg3), _ = _rows_call(head, rows=t_len, tr=tr, row_ins=[(h2, d, 0), (target, d, 0)],
                                                full_ins=[g3], row_outs=[(d, F32), (d, BF16)],
                                                acc_outs=[(1, LANES), (1, d)], name="loss_head")

    def act_bwd(dav, gv, uv):
        s = _sigmoid(gv)
        return dav * uv * (s * (1.0 + gv * (1.0 - s))), dav * gv * s

    (dg_ff, du_ff), _ = _matmul(dh2_b, wd, tb=True, name="d_ffn_gate_up",
                                post=(act_bwd, [g_ff, u_ff], [BF16, BF16]), tn_max=MM_TILE_N_POST)
    dwd, _ = _matmul(a_ff, dh2_b, ta=True, out_dtype=BF16, name="d_w_down")
    dwd8 = dwd.reshape(N_DEV, ff_s, d)
    dn2, (sib_d,) = _matmul(dg_ff, wgt, out_dtype=F32, name="d_norm_ffn_out_gate", ride=_scatter_sibling([dwd8]))
    chip_d = _chip_sum(dwd8, sib_d, name="chip_sum_w_down")
    dn2, (parts_d,) = _matmul(du_ff, wut, out_dtype=F32, add=dn2, name="d_norm_ffn_out_up",
                              ride=_scatter_chips([chip_d], rows=(0, half_cut)))
    dwg, (parts_d,) = _matmul(dg_ff, n2, ta=True, out_dtype=BF16, name="d_w_gate",
                              ride=_scatter_chips([chip_d], rows=(half_cut, ff_s - half_cut), into=[parts_d]))
    dwu, _ = _matmul(du_ff, n2, ta=True, out_dtype=BF16, name="d_w_up")
    dwg8, dwu8 = dwg.reshape(N_DEV, ff_s, d), dwu.reshape(N_DEV, ff_s, d)

    def norm_bwd(v, dn, dres, g):
        _, vjp = jax.vjp(_rms, v, g)
        dv, dg = vjp(dn)
        return (dv + dres,), (dg,)

    def norm_bwd_2(v, dn, dres, g):
        (dv,), acc = norm_bwd(v, dn, dres, g)
        return (dv, dv), acc

    (dh1, dh1_b, dg2), (sib_g, sib_u) = _rows_call(norm_bwd_2, rows=t_len, tr=tr,
                                                   row_ins=[(h1, d, 0), (dn2, d, 0), (dh2, d, 0)], full_ins=[g2],
                                                   row_outs=[(d, F32), (d, BF16)], acc_outs=[(1, d)], name="d_norm_ffn",
                                                   ride=_scatter_sibling([dwg8, dwu8]))
    chip_g = _chip_sum(dwg8, sib_g, name="chip_sum_w_gate")
    chip_u = _chip_sum(dwu8, sib_u, name="chip_sum_w_up")

    dy_mix, _ = _matmul(dh1_b, wo, tb=True, out_dtype=BF16, name="d_y_mix")
    dwo, _ = _matmul(y_mix, dh1_b, ta=True, out_dtype=BF16, name="d_w_out")
    dwo8 = dwo.reshape(N_DEV, 2 * d // N_DEV, d)
    def gate_norm_bwd(y, z, dyo, g):
        _, vjp = jax.vjp(gate_norm, y, z.astype(F32), g)
        dy, dz, dg = vjp(dyo.astype(F32))
        return (dy, dz), (dg,)

    (dy_ssd, dproj, dgs), _ = _rows_call(gate_norm_bwd, rows=t_len, tr=tr,
                                         row_ins=[(y_ssd, d, 0), (proj, d, 0), (dy_mix, d, 0)], full_ins=[gs],
                                         row_outs=[(d, F32), (d, BF16, w_main)], acc_outs=[(1, d)],
                                         name="d_ssm_gate_norm")
    (dproj, dgc, du, dcw_sc), (sib_o,) = _cols_call(
        _shortconv_bwd, rows=t_len, cols=d, cw=cw,
        col_ins=[(proj, sc0), (proj, sc0 + slab(d)), (proj, sc0 + 2 * slab(d)), (dy_mix, slab(d))],
        par_ins=[(cw_sc, 0)], col_outs=[BF16] * 3, par_outs=[K_SC], name="d_shortconv",
        ride=_scatter_sibling([dwo8]), into=(dproj, sc0))
    chip_o = _chip_sum(dwo8, sib_o, name="chip_sum_w_out")
    (dxs, dbm, dcm, g_dt, g_csc, g_csr3, ddk), (parts_g,) = _ssd_bwd(
        xbc, dt_c, cs_c, cs_r3, e01, dskip_e, hprev, dy_ssd, d_ssm=d, r_heads=r_heads,
        ride=_scatter_chips([chip_g]))
    ddt_c, ddt_r, dbias_r, dbias_c, dalog_r, dalog_c, ddskip = _ssd_dt(
        dt_raw, dt_raw_t, small, cots=(g_dt, g_csc, g_csr3.reshape(heads, t_len), ddk, e01))
    dcw_parts, dcb_parts, col0 = [], [], 0
    for tag, dpart in (("x", dxs), ("b", dbm), ("c", dcm)):
        (dproj, dcw_p, dcb_p), _ = _cols_call(
            _conv_silu_bwd, rows=t_len, cols=dpart.shape[1], cw=cw,
            col_ins=[(proj, slab(off_xbc + col0)), (dpart, 0)], par_ins=[(cw_ssm, slab(col0)), (ssm_conv_b, slab(col0))],
            col_outs=[BF16], par_outs=[K_SSM, 1], name="d_ssm_conv_" + tag, into=(dproj, slab(off_xbc + col0)))
        dcw_parts.append(dcw_p)
        dcb_parts.append(dcb_p)
        col0 += dpart.shape[1]
    dcw_ssm, dcb_ssm = jnp.concatenate(dcw_parts, axis=1), jnp.concatenate(dcb_parts, axis=1)
    for i, part in ((1, dgc), (2, du)):
        dproj = lax.dynamic_update_slice(dproj, part, (0, d + d_xbc + i * d))
    ddt = ddt_c + _pad_to(jnp.transpose(ddt_r), t_len, LANES)
    dwm, (parts_u, parts_o) = _matmul(dproj, n1, ta=True, out_dtype=BF16, name="d_w_in_main",
                                      ride=_scatter_chips([chip_u, chip_o]))
    dwdt, _ = _matmul(ddt, n1, ta=True, out_dtype=BF16, name="d_w_in_dt")
    own_ref = [k * in_s + i if i < in_s else -1 for k in range(N_DEV) for i in range(in_p)]
    dwin8 = _move_rows(
        dwm, [-1 if g < 0 or off_dt <= g < off_cb else (g if g < off_dt else g - heads) for g in own_ref],
        "place_d_w_in", extra=dwdt, extra_row=[g - off_dt if off_dt <= g < off_cb else -1 for g in own_ref],
    ).reshape(N_DEV, in_p, d)
    dn1, (sib_in,) = _matmul(ddt, wtdt, out_dtype=F32, name="d_norm_mix_out_dt", ride=_scatter_sibling([dwin8]))
    chip_in = _chip_sum(dwin8, sib_in, name="chip_sum_w_in")
    cut = int(in_p * W_IN_SCATTER_SPLIT) // BF16_ROWS * BF16_ROWS
    dn1, (parts_in,) = _matmul(dproj, wtm, out_dtype=F32, add=dn1, name="d_norm_mix_out",
                               ride=_scatter_chips([chip_in], rows=(0, cut)))
    (dx, dg1), _ = _rows_call(norm_bwd, rows=t_len, tr=tr, row_ins=[(x2, d, 0), (dn1, d, 0), (dh1, d, 0)],
                              full_ins=[g1], row_outs=[(d, F32)], acc_outs=[(1, d)], name="d_norm_mix")

    wide = d_xbc
    rows_small = [dg1, dcb_ssm, dbias_r + _pad_to(dbias_c.reshape(1, heads), 1, LANES),
                  dalog_r + _pad_to(dalog_c.reshape(1, heads), 1, LANES), ddskip, dgs, dg2, dg3]
    packed = jnp.concatenate([_pad_to(r, 1, wide) for r in rows_small]
                             + [dcw_ssm, _pad_to(dcw_sc, K_SC, wide), _pad_to(loss_acc, 1, wide)], axis=0)
    p_small, parts_in = _comm(_merge(_gather_all([packed]), _scatter_chips([chip_in], rows=(cut, in_p - cut),
                                                                           into=[parts_in])), "gather_small_grads")

    conv_lo = me * (d_xbc // N_DEV)
    sc_lo = me * (d // N_DEV)

    def pack_state(vals):
        (nm, cb, dtb, al, dk, sg, nf, nfin, cws, scs) = vals
        rows = [_pad_to(a.reshape(1, -1), 1, wide) for a in (nm, cb, dtb, al, dk, sg, nf, nfin)]
        cws_full = lax.dynamic_update_slice(jnp.zeros((K_SSM, wide), F32), cws[0], (0, conv_lo))
        scs_full = lax.dynamic_update_slice(jnp.zeros((K_SC, wide), F32), scs[0], (0, sc_lo))
        return jnp.concatenate(rows + [cws_full, scs_full, jnp.zeros((1, wide), F32)], axis=0)

    w_small = pack_state((norm_mix_g, ssm_conv_b, ssm_dt_bias, ssm_A_log, ssm_D, ssm_norm_g, norm_ffn_g, norm_final_g,
                          ssm_conv_w, sc_conv_w))
    m_small = pack_state((m_norm_mix_g, m_ssm_conv_b, m_ssm_dt_bias, m_ssm_A_log, m_ssm_D, m_ssm_norm_g, m_norm_ffn_g,
                          m_norm_final_g, m_ssm_conv_w, m_sc_conv_w))
    v_small = pack_state((v_norm_mix_g, v_ssm_conv_b, v_ssm_dt_bias, v_ssm_A_log, v_ssm_D, v_ssm_norm_g, v_norm_ffn_g,
                          v_norm_final_g, v_ssm_conv_w, v_sc_conv_w))

    tin = lambda a: _pad_to(tpose(a), in_p, d)
    tin_back = lambda a: jnp.transpose(a[:in_s])[None]
    t_back = lambda a: jnp.transpose(a)[None]
    upd = {
        "w_in": [tin_back(o) for o in _reduce_adamw(parts_in, tin(w_in), tin(m_w_in), tin(v_w_in), name="adamw_w_in")],
        "w_out": [o[None] for o in _reduce_adamw(parts_o, w_out[0], m_w_out[0], v_w_out[0], name="adamw_w_out")],
        "w_gate": [t_back(o) for o in _reduce_adamw(parts_g, tpose(w_gate), tpose(m_w_gate), tpose(v_w_gate),
                                                    name="adamw_w_gate")],
        "w_up": [t_back(o) for o in _reduce_adamw(parts_u, tpose(w_up), tpose(m_w_up), tpose(v_w_up),
                                                  name="adamw_w_up")],
        "w_down": [o[None] for o in _reduce_adamw(parts_d, w_down[0], m_w_down[0], v_w_down[0], name="adamw_w_down")],
    }
    small_upd = _reduce_adamw(p_small, w_small, m_small, v_small, name="adamw_small")
    loss = small_upd[0][8 + K_SSM + K_SC, 0]

    def unpack(packed_out):
        vec = lambda i, n, shape: packed_out[i, :n].reshape(shape)
        return {
            "norm_mix_g": vec(0, d, (1, d)), "ssm_conv_b": vec(1, d_xbc, (1, d_xbc)),
            "ssm_dt_bias": vec(2, heads, (1, heads)), "ssm_A_log": vec(3, heads, (1, heads)),
            "ssm_D": vec(4, heads, (1, heads)), "ssm_norm_g": vec(5, d, (1, d)), "norm_ffn_g": vec(6, d, (1, d)),
            "norm_final_g": vec(7, d, (d,)),
            "ssm_conv_w": lax.dynamic_slice(packed_out[8:8 + K_SSM], (0, conv_lo), (K_SSM, d_xbc // N_DEV))[None],
            "sc_conv_w": lax.dynamic_slice(packed_out[8 + K_SSM:8 + K_SSM + K_SC], (0, sc_lo), (K_SC, d // N_DEV))[None],
        }

    names = ["norm_mix_g", "w_in", "ssm_conv_w", "ssm_conv_b", "ssm_dt_bias", "ssm_A_log", "ssm_D", "ssm_norm_g",
             "sc_conv_w", "w_out", "norm_ffn_g", "w_gate", "w_up", "w_down", "norm_final_g"]
    outs = []
    for kind in range(4):
        small_k = unpack(small_upd[kind])
        for nm in names:
            outs.append(upd[nm][kind] if nm in upd else small_k[nm])
    return (loss, dx[None], *outs)
```

```python
import collections
import functools

import jax
import jax.numpy as jnp
from jax import lax
from jax.experimental import pallas as pl
from jax.experimental.pallas import tpu as pltpu

F32 = jnp.float32
BF16 = jnp.bfloat16

N_DEV = 8
N_CHIPS = 4
HEADDIM = 64
N_GROUPS = 8
N_STATE = 128
CHUNK = 128
K_SSM = 4
K_SC = 3
EPS = 1e-5
LANES = 128
BF16_ROWS = 16
MM_TILE_MN = 1408
MM_TILE_K = 2816
W_IN_SCATTER_SPLIT = 13 / 14
W_UP_GATHER_SPLIT = 0.7
W_DOWN_GATHER_SPLIT = 0.3
MM_TILE_N_POST = 704
SSD_CHUNKS_PER_STEP = 8
SSD_GROUPS_PER_STEP = 4
ROW_BLOCK = 256
V7X_VMEM_BYTES = 64 * 1024 * 1024
VMEM_LIMIT = (V7X_VMEM_BYTES * 3) // 4

ADAM_LR = 0.001
ADAM_B1 = 0.9
ADAM_B2 = 0.999
ADAM_EPS = 1e-08
ADAM_WD = 0.01
ADAM_STEP = 10


def _tile(n, pref, align):
    t = min(pref, n)
    t -= t % align
    while t >= align:
        if n % t == 0:
            return t
        t -= align
    return n


_Ride = collections.namedtuple("_Ride", ["ins", "out_shapes", "aliases", "nsem", "plan", "finish"], defaults=(None,))
_ANY = pl.BlockSpec(memory_space=pl.ANY)


def _coords():
    return lax.axis_index("x"), lax.axis_index("y"), lax.axis_index("c")


def _other_chips(x, y):
    return ((1 - x, y), (x, 1 - y), (1 - x, 1 - y))


def _remote(src, dst, send, recv, k, dev):
    return functools.partial(pltpu.make_async_remote_copy, src_ref=src, dst_ref=dst, send_sem=send.at[k],
                             recv_sem=recv.at[k], device_id=dev, device_id_type=pl.DeviceIdType.MESH)


def _local(src, dst, sem):
    return functools.partial(pltpu.make_async_copy, src, dst, sem)


def _start_all(plan):
    for kind, make in plan:
        if kind != "arrival":
            make().start()


def _wait_all(plan):
    for kind, make in plan:
        if kind == "local":
            make().wait()
        elif kind == "out":
            make().wait_send()
        else:
            make().wait_recv()


def _gather_chips(srcs, rows=None, into=None):
    n = len(srcs)

    def plan(ins, outs, send, recv, base):
        x, y, c = _coords()
        me = 4 * x + 2 * y + c
        cut = (lambda ref: ref) if rows is None else (lambda ref: ref.at[pl.ds(rows[0], rows[1])])
        d = []
        for a, (src, dst) in enumerate(zip(ins[:n], outs)):
            k = base + 4 * a
            d.append(("local", _local(cut(src), cut(dst.at[me]), send.at[k + 3])))
            for j, (px, py) in enumerate(_other_chips(x, y)):
                d.append(("out", _remote(cut(src), cut(dst.at[me]), send, recv, k + j, (px, py, c))))
                d.append(("arrival", _remote(cut(src), cut(dst.at[4 * px + 2 * py + c]), send, recv, k + j,
                                             (px, py, c))))
        return d
    shapes = [jax.ShapeDtypeStruct((N_DEV,) + s.shape, s.dtype) for s in srcs]
    if into is None:
        return _Ride(list(srcs), shapes, {}, 4 * n, plan)
    return _Ride(list(srcs) + list(into), shapes, {n + a: a for a in range(n)}, 4 * n, plan)


def _gather_sibling(bufs):
    def plan(ins, outs, send, recv, base):
        x, y, c = _coords()
        d = []
        for a, buf in enumerate(outs):
            for q in range(N_CHIPS):
                k = base + 4 * a + q
                d.append(("out", _remote(buf.at[2 * q + c], buf.at[2 * q + c], send, recv, k, (x, y, 1 - c))))
                d.append(("arrival", _remote(buf.at[2 * q + c], buf.at[2 * q + 1 - c], send, recv, k, (x, y, 1 - c))))
        return d
    shapes = [jax.ShapeDtypeStruct(b.shape, b.dtype) for b in bufs]
    return _Ride(list(bufs), shapes, {i: i for i in range(len(bufs))}, 4 * len(bufs), plan)


def _scatter_sibling(srcs):
    def plan(ins, outs, send, recv, base):
        x, y, c = _coords()
        d = []
        for a, (src, sib) in enumerate(zip(ins, outs)):
            for q in range(N_CHIPS):
                k = base + 4 * a + q
                d.append(("out", _remote(src.at[2 * q + 1 - c], sib.at[q], send, recv, k, (x, y, 1 - c))))
                d.append(("arrival", _remote(src.at[2 * q + 1 - c], sib.at[q], send, recv, k, (x, y, 1 - c))))
        return d
    shapes = [jax.ShapeDtypeStruct((N_CHIPS,) + s.shape[1:], s.dtype) for s in srcs]
    return _Ride(list(srcs), shapes, {}, 4 * len(srcs), plan)


def _scatter_chips(chips, rows=None, into=None):
    n = len(chips)

    def plan(ins, outs, send, recv, base):
        x, y, c = _coords()
        mine = 2 * x + y
        cut = (lambda ref: ref) if rows is None else (lambda ref: ref.at[pl.ds(rows[0], rows[1])])
        d = []
        for a, (chip, parts) in enumerate(zip(ins[:n], outs)):
            k = base + 4 * a
            d.append(("local", _local(cut(chip.at[mine]), cut(parts.at[mine]), send.at[k + 3])))
            for j, (px, py) in enumerate(_other_chips(x, y)):
                q = 2 * px + py
                d.append(("out", _remote(cut(chip.at[q]), cut(parts.at[mine]), send, recv, k + j, (px, py, c))))
                d.append(("arrival", _remote(cut(chip.at[q]), cut(parts.at[q]), send, recv, k + j, (px, py, c))))
        return d
    shapes = [jax.ShapeDtypeStruct(s.shape, s.dtype) for s in chips]
    if into is None:
        return _Ride(list(chips), shapes, {}, 4 * n, plan)
    return _Ride(list(chips) + list(into), shapes, {n + a: a for a in range(n)}, 4 * n, plan)


def _gather_all(srcs):
    def plan(ins, outs, send, recv, base):
        x, y, c = _coords()
        me = 4 * x + 2 * y + c
        d = []
        for a, (src, dst) in enumerate(zip(ins, outs)):
            k = base + N_DEV * a
            d.append(("local", _local(src, dst.at[me], send.at[k])))
            for j in range(1, N_DEV):
                px = 1 - x if (j >> 2) & 1 else x
                py = 1 - y if (j >> 1) & 1 else y
                pc = 1 - c if j & 1 else c
                d.append(("out", _remote(src, dst.at[me], send, recv, k + j, (px, py, pc))))
                d.append(("arrival", _remote(src, dst.at[4 * px + 2 * py + pc], send, recv, k + j, (px, py, pc))))
        return d
    shapes = [jax.ShapeDtypeStruct((N_DEV,) + s.shape, s.dtype) for s in srcs]
    return _Ride(list(srcs), shapes, {}, N_DEV * len(srcs), plan)


def _merge(*rides):
    ins, outs, aliases, parts, nsem = [], [], {}, [], 0
    for r in rides:
        parts.append((len(ins), len(outs), nsem, r))
        aliases.update({len(ins) + i: len(outs) + j for i, j in r.aliases.items()})
        ins += r.ins
        outs += r.out_shapes
        nsem += r.nsem

    def plan(i, o, send, recv, base):
        d = []
        for i0, o0, s0, r in parts:
            d += r.plan(i[i0:i0 + len(r.ins)], o[o0:o0 + len(r.out_shapes)], send, recv, base + s0)
        return d
    return _Ride(ins, outs, aliases, nsem, plan)


def _comm(ride, name):
    n_in, n_out = len(ride.ins), len(ride.out_shapes)

    def body(*refs):
        plan = ride.plan(refs[:n_in], refs[n_in:n_in + n_out], refs[-2], refs[-1], 0)
        _start_all(plan)
        if ride.finish is None:
            _wait_all(plan)
        else:
            ride.finish(refs[:n_in], refs[n_in:n_in + n_out], refs[-2], refs[-1])

    return pl.pallas_call(
        body, name=name, in_specs=[_ANY] * n_in, out_specs=[_ANY] * n_out, out_shape=ride.out_shapes,
        scratch_shapes=[pltpu.SemaphoreType.DMA((ride.nsem,)), pltpu.SemaphoreType.DMA((ride.nsem,))],
        input_output_aliases=dict(ride.aliases),
        compiler_params=pltpu.CompilerParams(has_side_effects=True),
    )(*ride.ins)


def _gather_relayed(big, small):
    srcs = list(big) + list(small)
    n, per = len(srcs), 10

    def places(ins, outs):
        x, y, c = _coords()
        slot = lambda dev: 4 * dev[0] + 2 * dev[1] + dev[2]
        devs = dict(me=(x, y, c), nx=(1 - x, y, c), ny=(x, 1 - y, c), dg=(1 - x, 1 - y, c), sib=(x, y, 1 - c))
        return devs, slot

    def first(ins, outs, send, recv, base):
        devs, slot = places(ins, outs)
        d = []
        for a, (src, dst) in enumerate(zip(ins, outs)):
            k, mine = base + per * a, dst.at[slot(devs["me"])]
            d.append(("local", _local(src, mine, send.at[k + 9])))
            d.append(("out", _remote(src, mine, send, recv, k, devs["nx"])))
            d.append(("out", _remote(src, mine, send, recv, k + 1, devs["ny"])))
            d.append(("out", _remote(src, mine, send, recv, k + 4, devs["sib"])))
            if a >= len(big):
                d.append(("out", _remote(src, mine, send, recv, k + 2, devs["dg"])))
        return d

    def finish(ins, outs, send, recv):
        devs, slot = places(ins, outs)
        sib_of = lambda dev: (dev[0], dev[1], 1 - dev[2])
        later = []

        def go(copy):
            copy.start()
            later.append(copy)

        for a, (src, dst) in enumerate(zip(ins, outs)):
            k, rows = per * a, src.shape[0]
            relay = a < len(big)
            half = rows // 2
            lo = lambda dev: dst.at[slot(dev)].at[pl.ds(0, half)]
            hi = lambda dev: dst.at[slot(dev)].at[pl.ds(half, rows - half)]
            whole = lambda dev: dst.at[slot(dev)]
            _remote(src, whole(devs["nx"]), send, recv, k, devs["nx"])().wait_recv()
            if relay:
                go(_remote(lo(devs["nx"]), lo(devs["nx"]), send, recv, k + 2, devs["ny"])())
            go(_remote(whole(devs["nx"]), whole(devs["nx"]), send, recv, k + 5, devs["sib"])())
            _remote(src, whole(devs["ny"]), send, recv, k + 1, devs["ny"])().wait_recv()
            if relay:
                go(_remote(hi(devs["ny"]), hi(devs["ny"]), send, recv, k + 3, devs["nx"])())
            go(_remote(whole(devs["ny"]), whole(devs["ny"]), send, recv, k + 6, devs["sib"])())
            if relay:
                _remote(lo(devs["dg"]), lo(devs["dg"]), send, recv, k + 2, devs["ny"])().wait_recv()
                go(_remote(lo(devs["dg"]), lo(devs["dg"]), send, recv, k + 7, devs["sib"])())
                _remote(hi(devs["dg"]), hi(devs["dg"]), send, recv, k + 3, devs["nx"])().wait_recv()
                go(_remote(hi(devs["dg"]), hi(devs["dg"]), send, recv, k + 8, devs["sib"])())
            else:
                _remote(src, whole(devs["dg"]), send, recv, k + 2, devs["dg"])().wait_recv()
                go(_remote(whole(devs["dg"]), whole(devs["dg"]), send, recv, k + 7, devs["sib"])())
        for a, (src, dst) in enumerate(zip(ins, outs)):
            k, rows = per * a, src.shape[0]
            half = rows // 2
            for j, dev in ((4, devs["me"]), (5, devs["nx"]), (6, devs["ny"])):
                theirs = dst.at[slot(sib_of(dev))]
                _remote(theirs, theirs, send, recv, k + j, devs["sib"])().wait_recv()
            far = dst.at[slot(sib_of(devs["dg"]))]
            if a < len(big):
                _remote(far.at[pl.ds(0, half)], far.at[pl.ds(0, half)], send, recv, k + 7, devs["sib"])().wait_recv()
                _remote(far.at[pl.ds(half, rows - half)], far.at[pl.ds(half, rows - half)], send, recv, k + 8,
                        devs["sib"])().wait_recv()
            else:
                _remote(far, far, send, recv, k + 7, devs["sib"])().wait_recv()
        for kind, make in first(ins, outs, send, recv, 0):
            (make().wait if kind == "local" else make().wait_send)()
        for copy in later:
            copy.wait_send()

    shapes = [jax.ShapeDtypeStruct((N_DEV,) + s.shape, s.dtype) for s in srcs]
    return _Ride(srcs, shapes, {}, per * n, first, finish)


def _call(body, *, name, grid, in_specs, out_specs, out_shape, args, sem, scratch=(), ride=None, base=None):
    params = pltpu.CompilerParams(dimension_semantics=sem, vmem_limit_bytes=VMEM_LIMIT)
    own_aliases = {}
    if base is not None:
        inner, n_host = body, len(args)
        body = lambda *refs: inner(*refs[:n_host], *refs[n_host + 1:])
        own_aliases[n_host] = base[1]
        args, in_specs = tuple(args) + (base[0],), list(in_specs) + [_ANY]
    if ride is None:
        res = pl.pallas_call(body, name=name, grid=grid, in_specs=in_specs, out_specs=out_specs,
                             out_shape=out_shape, scratch_shapes=list(scratch), input_output_aliases=own_aliases,
                             compiler_params=params)(*args)
        return list(res), []
    n_in, n_out, n_scr = len(args), len(out_shape), len(scratch)
    r_in, r_out = len(ride.ins), len(ride.out_shapes)

    def hosted(*refs):
        h_in, rin = refs[:n_in], refs[n_in:n_in + r_in]
        o0 = n_in + r_in
        h_out, rout = refs[o0:o0 + n_out], refs[o0 + n_out:o0 + n_out + r_out]
        s0 = o0 + n_out + r_out
        h_scr, send, recv = refs[s0:s0 + n_scr], refs[s0 + n_scr], refs[s0 + n_scr + 1]
        ids = [pl.program_id(i) for i in range(len(grid))]
        first = functools.reduce(lambda p, q: p & q, [i == 0 for i in ids])
        last = functools.reduce(lambda p, q: p & q, [i == n - 1 for i, n in zip(ids, grid)])

        @pl.when(first)
        def _():
            _start_all(ride.plan(rin, rout, send, recv, 0))

        body(*h_in, *h_out, *h_scr)

        @pl.when(last)
        def _():
            if ride.finish is None:
                _wait_all(ride.plan(rin, rout, send, recv, 0))
            else:
                ride.finish(rin, rout, send, recv)

    res = pl.pallas_call(
        hosted, name=name, grid=grid, in_specs=list(in_specs) + [_ANY] * r_in,
        out_specs=list(out_specs) + [_ANY] * r_out, out_shape=list(out_shape) + list(ride.out_shapes),
        scratch_shapes=list(scratch) + [pltpu.SemaphoreType.DMA((ride.nsem,)), pltpu.SemaphoreType.DMA((ride.nsem,))],
        input_output_aliases={**own_aliases, **{n_in + i: n_out + j for i, j in ride.aliases.items()}},
        compiler_params=params,
    )(*args, *ride.ins)
    return list(res[:n_out]), list(res[n_out:])


def _matmul(a, b, *, ta=False, tb=False, out_dtype=BF16, add=None, post=None, name, ride=None, tn_max=MM_TILE_MN):
    m = a.shape[1] if ta else a.shape[0]
    k = a.shape[0] if ta else a.shape[1]
    n = b.shape[0] if tb else b.shape[1]
    assert k == (b.shape[1] if tb else b.shape[0])
    tm, tn, tk = _tile(m, MM_TILE_MN, LANES), _tile(n, tn_max, LANES), _tile(k, MM_TILE_K, LANES)
    nk = k // tk
    dims = (((0 if ta else 1,), (1 if tb else 0,)), ((), ()))
    single = post is None
    if add is not None:
        post = (lambda r, t: (r + t,), [add], [out_dtype])
    elif post is None:
        post = (lambda r: (r,), [], [out_dtype])
    post_fn, extras, out_dtypes = post
    n_ex, n_o = len(extras), len(out_dtypes)

    def body(*refs):
        a_ref, b_ref = refs[:2]
        ex_refs, o_refs = refs[2:2 + n_ex], refs[2 + n_ex:2 + n_ex + n_o]

        def finish(r):
            for o_ref, v in zip(o_refs, post_fn(r, *[e[...].astype(F32) for e in ex_refs])):
                o_ref[...] = v.astype(o_ref.dtype)

        part = lax.dot_general(a_ref[...].astype(BF16), b_ref[...].astype(BF16), dims, preferred_element_type=F32)
        if nk == 1:
            finish(part)
            return
        acc = refs[-1]
        kk = pl.program_id(2)

        @pl.when(kk == 0)
        def _():
            acc[...] = part

        @pl.when((kk > 0) & (kk < nk - 1))
        def _():
            acc[...] += part

        @pl.when(kk == nk - 1)
        def _():
            finish(acc[...] + part)

    a_spec = (pl.BlockSpec((tk, tm), lambda i, j, kk: (kk, i)) if ta
              else pl.BlockSpec((tm, tk), lambda i, j, kk: (i, kk)))
    b_spec = (pl.BlockSpec((tn, tk), lambda i, j, kk: (j, kk)) if tb
              else pl.BlockSpec((tk, tn), lambda i, j, kk: (kk, j)))
    o_spec = pl.BlockSpec((tm, tn), lambda i, j, kk: (i, j))
    outs, rides = _call(
        body, name=name, grid=(m // tm, n // tn, nk),
        in_specs=[a_spec, b_spec] + [o_spec] * n_ex, out_specs=[o_spec] * n_o,
        out_shape=[jax.ShapeDtypeStruct((m, n), dt) for dt in out_dtypes], args=(a, b, *extras),
        scratch=[pltpu.VMEM((tm, tn), F32)] if nk > 1 else [], sem=("parallel", "parallel", "arbitrary"), ride=ride)
    return (outs[0] if single else outs), rides


def _rows_call(fn, *, rows, tr, row_ins, full_ins, row_outs, acc_outs, name, ride=None):
    nr, nf, no, na = len(row_ins), len(full_ins), len(row_outs), len(acc_outs)

    def body(*refs):
        vals = [r[...] for r in refs[:nr + nf]]
        outs, accs = fn(*vals)
        for r, v in zip(refs[nr + nf:nr + nf + no], outs):
            r[...] = v.astype(r.dtype)
        if na:
            @pl.when(pl.program_id(0) == 0)
            def _():
                for r in refs[nr + nf + no:]:
                    r[...] = jnp.zeros_like(r)
            for r, v in zip(refs[nr + nf + no:], accs):
                r[...] += v

    in_specs = [pl.BlockSpec((tr, w), functools.partial(lambda cb, i: (i, cb), cb)) for _, w, cb in row_ins]
    in_specs += [pl.BlockSpec(f.shape, lambda i: (0, 0)) for f in full_ins]
    out_specs = [pl.BlockSpec((tr, o[0]), lambda i: (i, 0)) for o in row_outs]
    out_specs += [pl.BlockSpec(s, lambda i: (0, 0)) for s in acc_outs]
    out_shape = [jax.ShapeDtypeStruct((rows, o[-1] if len(o) == 3 else o[0]), o[1]) for o in row_outs]
    out_shape += [jax.ShapeDtypeStruct(s, F32) for s in acc_outs]
    return _call(body, name=name, grid=(rows // tr,), in_specs=in_specs, out_specs=out_specs, out_shape=out_shape,
                 args=tuple(a for a, _, _ in row_ins) + tuple(full_ins), sem=("arbitrary",), ride=ride)


def _cols_call(fn, *, rows, cols, cw, col_ins, par_ins, col_outs, par_outs, name, ride=None, into=None):
    nc, npar = len(col_ins), len(par_ins)

    def body(*refs):
        vals = [r[...] for r in refs[:nc + npar]]
        outs, pouts = fn(*vals)
        for r, v in zip(refs[nc + npar:], tuple(outs) + tuple(pouts)):
            r[...] = v.astype(r.dtype)

    in_specs = [pl.BlockSpec((rows, cw), functools.partial(lambda off, j: (0, off + j), off)) for _, off in col_ins]
    in_specs += [pl.BlockSpec((p.shape[0], cw), functools.partial(lambda off, j: (0, off + j), off))
                 for p, off in par_ins]
    out_specs = [pl.BlockSpec((rows, cw), lambda j: (0, j)) for _ in col_outs]
    out_specs += [pl.BlockSpec((k, cw), lambda j: (0, j)) for k in par_outs]
    out_shape = [jax.ShapeDtypeStruct((rows, cols), dt) for dt in col_outs]
    out_shape += [jax.ShapeDtypeStruct((k, cols), F32) for k in par_outs]
    if into is not None:
        out_specs[0] = pl.BlockSpec((rows, cw), lambda j: (0, into[1] + j))
        out_shape[0] = jax.ShapeDtypeStruct(into[0].shape, into[0].dtype)
    return _call(body, name=name, grid=(cols // cw,), in_specs=in_specs, out_specs=out_specs, out_shape=out_shape,
                 args=tuple(a for a, _ in col_ins) + tuple(p for p, _ in par_ins), sem=("arbitrary",), ride=ride,
                 base=None if into is None else (into[0], 0))


def _sigmoid(v):
    return 1.0 / (1.0 + jnp.exp(-v))


def _softplus(v):
    return jnp.maximum(v, 0.0) + jnp.log(1.0 + jnp.exp(-jnp.abs(v)))


def _rms(v, g):
    return v * lax.rsqrt(jnp.mean(v * v, axis=-1, keepdims=True) + EPS) * g


def _shift_down(v, s, row):
    return jnp.where(row >= s, pltpu.roll(v, s, 0), 0.0)


def _shift_up(v, s, row):
    n = v.shape[0]
    return jnp.where(row < n - s, pltpu.roll(v, n - s, 0), 0.0)


def _causal_conv(u, w, row):
    k_taps = w.shape[0]
    acc = u * w[k_taps - 1:k_taps, :]
    for k in range(k_taps - 1):
        acc = acc + _shift_down(u, k_taps - 1 - k, row) * w[k:k + 1, :]
    return acc


def _causal_conv_bwd(u, dy, w, row):
    k_taps = w.shape[0]
    tap = lax.broadcasted_iota(jnp.int32, w.shape, 0)
    du = dy * w[k_taps - 1:k_taps, :]
    dw = jnp.where(tap == k_taps - 1, jnp.sum(dy * u, axis=0, keepdims=True), 0.0)
    for k in range(k_taps - 1):
        s = k_taps - 1 - k
        du = du + _shift_up(dy, s, row) * w[k:k + 1, :]
        dw = dw + jnp.where(tap == k, jnp.sum(dy * _shift_down(u, s, row), axis=0, keepdims=True), 0.0)
    return du, dw


def _conv_silu_fwd(u, w, b):
    u = u.astype(F32)
    row = lax.broadcasted_iota(jnp.int32, u.shape, 0)
    pre = _causal_conv(u, w, row) + b
    return (pre * _sigmoid(pre),), ()


def _conv_silu_bwd(u, dy, w, b):
    u = u.astype(F32)
    dy = dy.astype(F32)
    row = lax.broadcasted_iota(jnp.int32, u.shape, 0)
    pre = _causal_conv(u, w, row) + b
    s = _sigmoid(pre)
    dpre = dy * (s * (1.0 + pre * (1.0 - s)))
    du, dw = _causal_conv_bwd(u, dpre, w, row)
    return (du,), (dw, jnp.sum(dpre, axis=0, keepdims=True))


def _shortconv_fwd(gb, gc, u, w):
    gb, gc, u = gb.astype(F32), gc.astype(F32), u.astype(F32)
    row = lax.broadcasted_iota(jnp.int32, u.shape, 0)
    return (gb * _causal_conv(gc * u, w, row),), ()


def _shortconv_bwd(gb, gc, u, dy, w):
    gb, gc, u, dy = gb.astype(F32), gc.astype(F32), u.astype(F32), dy.astype(F32)
    row = lax.broadcasted_iota(jnp.int32, u.shape, 0)
    v = gc * u
    dgb = dy * _causal_conv(v, w, row)
    dv, dw = _causal_conv_bwd(v, dy * gb, w, row)
    return (dgb, dv * u, dv * gc), (dw,)


def _split3(v):
    hi = v.astype(BF16)
    r1 = v - hi.astype(F32)
    mid = r1.astype(BF16)
    lo = (r1 - mid.astype(F32)).astype(BF16)
    return hi, mid, lo


def _exact_dot(v, m01, dims, v_is_lhs):
    def one(p):
        return (lax.dot_general(p, m01, dims, preferred_element_type=F32) if v_is_lhs
                else lax.dot_general(m01, p, dims, preferred_element_type=F32))
    hi, mid, lo = _split3(v)
    return (one(lo) + one(mid)) + one(hi)


_NN = (((1,), (0,)), ((), ()))
_NT = (((1,), (1,)), ((), ()))
_TN = (((0,), (0,)), ((), ()))


@functools.partial(jax.custom_vjp, nondiff_argnums=(2,))
def _bdot(a, b, dims):
    return lax.dot_general(a.astype(BF16), b.astype(BF16), dims, preferred_element_type=F32)


def _bdot_fwd(a, b, dims):
    return _bdot(a, b, dims), (a, b)


def _bdot_bwd(dims, res, g):
    a, b = res
    ab, bb, gb = a.astype(BF16), b.astype(BF16), g.astype(BF16)
    dot = lambda p, q, d: lax.dot_general(p, q, d, preferred_element_type=F32)
    if dims == _NN:
        da, db = dot(gb, bb, _NT), dot(ab, gb, _TN)
    elif dims == _NT:
        da, db = dot(gb, bb, _NN), dot(gb, ab, _TN)
    else:
        da, db = dot(bb, gb, _NT), dot(ab, gb, _NN)
    return da.astype(a.dtype), db.astype(b.dtype)


_bdot.defvjp(_bdot_fwd, _bdot_bwd)


@jax.custom_vjp
def _cumsum_rows(tril, v):
    return _exact_dot(v, tril, _NN, False)


def _cumsum_rows_fwd(tril, v):
    return _cumsum_rows(tril, v), tril


def _cumsum_rows_bwd(tril, ct):
    return None, _exact_dot(ct, tril, _TN, False)


_cumsum_rows.defvjp(_cumsum_rows_fwd, _cumsum_rows_bwd)


@jax.custom_vjp
def _cumsum_lanes(tril, v):
    return _exact_dot(v, tril, _NT, True)


def _cumsum_lanes_fwd(tril, v):
    return _cumsum_lanes(tril, v), tril


def _cumsum_lanes_bwd(tril, ct):
    return None, _exact_dot(ct, tril, _NN, True)


_cumsum_lanes.defvjp(_cumsum_lanes_fwd, _cumsum_lanes_bwd)


@jax.custom_vjp
def _expand(e01, v):
    return _exact_dot(v, e01, _NN, True)


def _expand_fwd(e01, v):
    return _expand(e01, v), e01


def _expand_bwd(e01, ct):
    return None, _exact_dot(ct, e01, _NT, True)


_expand.defvjp(_expand_fwd, _expand_bwd)


def _causal_mask(n):
    li = lax.broadcasted_iota(jnp.int32, (n, n), 0)
    si = lax.broadcasted_iota(jnp.int32, (n, n), 1)
    return si <= li


def _dt_prep(dtc, dtr, bias_r, bias_c, alog_r, alog_c):
    dt_c = _softplus(dtc + bias_r)
    dt_r = _softplus(dtr + bias_c)
    tril = jnp.where(_causal_mask(dtc.shape[0]), 1.0, 0.0).astype(BF16)
    cs_c = _cumsum_rows(tril, dt_c * (-jnp.exp(alog_r)))
    cs_r = _cumsum_lanes(tril, dt_r * (-jnp.exp(alog_c)))
    return dt_c, cs_c, cs_r


def _ssd_chunk(r_heads, xs, bg, cg, dt_c, cs_c, cs_rg, e01, dskip_e, hp):
    l_len, rp = xs.shape
    p = rp // r_heads
    causal = _causal_mask(l_len)
    lane_head = lax.broadcasted_iota(jnp.int32, (1, rp), 1) // p
    dt_e = _expand(e01, dt_c)
    cs_e = _expand(e01, cs_c)
    cl_e = cs_e[l_len - 1:l_len, :]
    x = xs * dt_e
    cb = _bdot(cg, bg, _NT)
    ms, xm = [], []
    for r in range(r_heads):
        seg = cs_e[:, r * p:r * p + 1] - cs_rg[r:r + 1, :]
        decay = jnp.exp(jnp.where(causal, seg, -1e30))
        ms.append((cb * decay).astype(BF16))
        xm.append(jnp.where(lane_head == r, x, 0.0).astype(BF16))
    y_diag = _bdot(jnp.concatenate(ms, axis=1), jnp.concatenate(xm, axis=0), _NN)
    y_off = _bdot(cg, hp, _NN) * jnp.exp(cs_e)
    states = _bdot(bg, x * jnp.exp(cl_e - cs_e), _TN)
    h_next = hp * jnp.exp(cl_e) + states
    y = y_diag + y_off + dskip_e * xs
    return y, h_next


def _ssd_dt(dtc, dtr, small, cots=None):
    t_len, heads = dtc.shape[0], dtr.shape[0]
    nc = t_len // CHUNK
    col = pl.BlockSpec((CHUNK, LANES), lambda c: (c, 0))
    row = pl.BlockSpec((heads, CHUNK), lambda c: (0, c))
    full = [pl.BlockSpec(s.shape, lambda c: (0, 0)) for s in small]
    shapes = [jax.ShapeDtypeStruct((t_len, LANES), F32), jax.ShapeDtypeStruct((t_len, LANES), F32),
              jax.ShapeDtypeStruct((heads, t_len), F32)]
    if cots is None:
        def body(dtc_ref, dtr_ref, br, bc, ar, ac, dt_ref, csc_ref, csr_ref):
            dt_ref[...], csc_ref[...], csr_ref[...] = _dt_prep(dtc_ref[...], dtr_ref[...], br[...], bc[...],
                                                                ar[...], ac[...])
        return _call(body, name="ssd_dt", grid=(nc,), in_specs=[col, row] + full, out_specs=[col, col, row],
                     out_shape=shapes, args=(dtc, dtr, *small), sem=("parallel",))[0]

    g_dt, g_csc, g_csr, ddk, e01 = cots

    def body(dtc_ref, dtr_ref, br, bc, ar, ac, g_dt_ref, g_csc_ref, g_csr_ref, ddk_ref, e_ref,
             ddtc_ref, ddtr_ref, *dsmall):
        _, vjp = jax.vjp(_dt_prep, dtc_ref[...], dtr_ref[...], br[...], bc[...], ar[...], ac[...])
        grads = vjp((g_dt_ref[...], g_csc_ref[...], g_csr_ref[...]))
        ddtc_ref[...], ddtr_ref[...] = grads[0], grads[1]
        ddk8 = jnp.broadcast_to(ddk_ref[...], (8, ddk_ref.shape[1]))
        dskip = _exact_dot(ddk8, e_ref[...], _NT, True)[0:1, :]

        @pl.when(pl.program_id(0) == 0)
        def _():
            for r in dsmall:
                r[...] = jnp.zeros_like(r)

        for r, gr in zip(dsmall, tuple(grads[2:]) + (dskip,)):
            r[...] += gr

    acc = list(small) + [small[0]]
    return _call(body, name="d_ssd_dt", grid=(nc,),
                 in_specs=[col, row] + full + [col, col, row, pl.BlockSpec((None, 1, e01.shape[1]), lambda c: (c, 0, 0)),
                                               pl.BlockSpec(e01.shape, lambda c: (0, 0))],
                 out_specs=[col, row] + [pl.BlockSpec(s.shape, lambda c: (0, 0)) for s in acc],
                 out_shape=[shapes[0], shapes[2]] + [jax.ShapeDtypeStruct(s.shape, F32) for s in acc],
                 args=(dtc, dtr, *small, g_dt, g_csc, g_csr, ddk, e01), sem=("arbitrary",))[0]


def _ssd_specs(t_len, d_ssm, r_heads, reverse):
    rp = r_heads * HEADDIM
    nc = t_len // CHUNK
    per = next(p for p in (SSD_CHUNKS_PER_STEP, 2, 1) if nc % p == 0)
    ns, rows, gs = nc // per, per * CHUNK, SSD_GROUPS_PER_STEP
    cidx = (lambda c: ns - 1 - c) if reverse else (lambda c: c)
    b_off = d_ssm // (N_STATE * gs)
    specs = dict(
        xs=pl.BlockSpec((rows, gs * rp), lambda c, g: (cidx(c), g)),
        b=pl.BlockSpec((rows, gs * N_STATE), lambda c, g: (cidx(c), b_off + g)),
        c=pl.BlockSpec((rows, gs * N_STATE), lambda c, g: (cidx(c), b_off + N_GROUPS // gs + g)),
        grad_bc=pl.BlockSpec((rows, gs * N_STATE), lambda c, g: (cidx(c), g)),
        col=pl.BlockSpec((rows, LANES), lambda c, g: (cidx(c), 0)),
        csr=pl.BlockSpec((gs, r_heads, rows), lambda c, g: (g, 0, cidx(c))),
        e01=pl.BlockSpec((LANES, gs * rp), lambda c, g: (0, g)),
        dskip=pl.BlockSpec((1, gs * rp), lambda c, g: (0, g)),
        hprev=pl.BlockSpec((per, gs, N_STATE, rp), lambda c, g: (cidx(c), g, 0, 0)),
        ddk=pl.BlockSpec((per, 1, gs * rp), lambda c, g: (cidx(c), 0, g)),
    )
    return specs, nc, ns, per, rp


def _ssd_fwd(xbc, dt_c, cs_c, cs_r3, e01, dskip_e, *, d_ssm, r_heads, ride=None):
    t_len = xbc.shape[0]
    sp, nc, ns, per, rp = _ssd_specs(t_len, d_ssm, r_heads, False)

    def body(xs_ref, b_ref, c_ref, dt_ref, csc_ref, csr_ref, e_ref, dk_ref, y_ref, hprev_ref, h_ref):
        c, gp = pl.program_id(0), pl.program_id(1)
        groups = [gp * SSD_GROUPS_PER_STEP + gi for gi in range(SSD_GROUPS_PER_STEP)]

        @pl.when(c == 0)
        def _():
            for g in groups:
                h_ref[g] = jnp.zeros((N_STATE, rp), F32)

        hp = [h_ref[g] for g in groups]
        for s in range(per):
            r = pl.ds(s * CHUNK, CHUNK)
            for gi in range(SSD_GROUPS_PER_STEP):
                cols, bc = pl.ds(gi * rp, rp), pl.ds(gi * N_STATE, N_STATE)
                hprev_ref[s, gi] = hp[gi]
                y, hp[gi] = _ssd_chunk(r_heads, xs_ref[r, cols].astype(F32), b_ref[r, bc].astype(F32),
                                       c_ref[r, bc].astype(F32), dt_ref[r, :], csc_ref[r, :], csr_ref[gi, :, r],
                                       e_ref[:, cols], dk_ref[:, cols], hp[gi])
                y_ref[r, cols] = y
        for gi, g in enumerate(groups):
            h_ref[g] = hp[gi]

    return _call(
        body, name="ssd_fwd", grid=(ns, N_GROUPS // SSD_GROUPS_PER_STEP),
        in_specs=[sp["xs"], sp["b"], sp["c"], sp["col"], sp["col"], sp["csr"], sp["e01"], sp["dskip"]],
        out_specs=[sp["xs"], sp["hprev"]],
        out_shape=[jax.ShapeDtypeStruct((t_len, d_ssm), F32),
                   jax.ShapeDtypeStruct((nc, N_GROUPS, N_STATE, rp), F32)],
        args=(xbc, xbc, xbc, dt_c, cs_c, cs_r3, e01, dskip_e), scratch=[pltpu.VMEM((N_GROUPS, N_STATE, rp), F32)],
        sem=("arbitrary", "arbitrary"), ride=ride)


def _ssd_bwd(xbc, dt_c, cs_c, cs_r3, e01, dskip_e, hprev, dy, *, d_ssm, r_heads, ride=None):
    t_len = xbc.shape[0]
    sp, nc, ns, per, rp = _ssd_specs(t_len, d_ssm, r_heads, True)

    def body(xs_ref, b_ref, c_ref, dt_ref, csc_ref, csr_ref, e_ref, dk_ref, hprev_ref, dy_ref,
             dxs_ref, db_ref, dc_ref, ddt_ref, dcsc_ref, dcsr_ref, ddk_ref, dh_ref):
        c, gp = pl.program_id(0), pl.program_id(1)
        groups = [gp * SSD_GROUPS_PER_STEP + gi for gi in range(SSD_GROUPS_PER_STEP)]

        @pl.when(gp == 0)
        def _():
            ddt_ref[...] = jnp.zeros_like(ddt_ref)
            dcsc_ref[...] = jnp.zeros_like(dcsc_ref)

        @pl.when(c == 0)
        def _():
            for g in groups:
                dh_ref[g] = jnp.zeros((N_STATE, rp), F32)

        dh = [dh_ref[g] for g in groups]
        for s in reversed(range(per)):
            r = pl.ds(s * CHUNK, CHUNK)
            ddt_sum, dcsc_sum = ddt_ref[r, :], dcsc_ref[r, :]
            for gi in range(SSD_GROUPS_PER_STEP):
                cols, bc = pl.ds(gi * rp, rp), pl.ds(gi * N_STATE, N_STATE)
                e01 = e_ref[:, cols]
                fn = lambda xs, bg, cg, dt, csc, csr, dk, hp: _ssd_chunk(r_heads, xs, bg, cg, dt, csc, csr, e01, dk, hp)
                _, vjp = jax.vjp(fn, xs_ref[r, cols].astype(F32), b_ref[r, bc].astype(F32), c_ref[r, bc].astype(F32),
                                 dt_ref[r, :], csc_ref[r, :], csr_ref[gi, :, r], dk_ref[:, cols], hprev_ref[s, gi])
                dxs, dbg, dcg, ddt, dcsc, dcsr, ddk, dh[gi] = vjp((dy_ref[r, cols], dh[gi]))
                dxs_ref[r, cols] = dxs.astype(dxs_ref.dtype)
                db_ref[r, bc] = dbg.astype(db_ref.dtype)
                dc_ref[r, bc] = dcg.astype(dc_ref.dtype)
                ddt_sum, dcsc_sum = ddt_sum + ddt, dcsc_sum + dcsc
                dcsr_ref[gi, :, r] = dcsr
                ddk_ref[s, :, cols] = ddk
            ddt_ref[r, :], dcsc_ref[r, :] = ddt_sum, dcsc_sum
        for gi, g in enumerate(groups):
            dh_ref[g] = dh[gi]

    n_bc = N_GROUPS * N_STATE
    return _call(
        body, name="ssd_bwd", grid=(ns, N_GROUPS // SSD_GROUPS_PER_STEP),
        in_specs=[sp["xs"], sp["b"], sp["c"], sp["col"], sp["col"], sp["csr"], sp["e01"], sp["dskip"], sp["hprev"],
                  sp["xs"]],
        out_specs=[sp["xs"], sp["grad_bc"], sp["grad_bc"], sp["col"], sp["col"], sp["csr"], sp["ddk"]],
        out_shape=[jax.ShapeDtypeStruct((t_len, d_ssm), BF16), jax.ShapeDtypeStruct((t_len, n_bc), BF16),
                   jax.ShapeDtypeStruct((t_len, n_bc), BF16), jax.ShapeDtypeStruct(dt_c.shape, F32),
                   jax.ShapeDtypeStruct(cs_c.shape, F32), jax.ShapeDtypeStruct(cs_r3.shape, F32),
                   jax.ShapeDtypeStruct((nc, 1, d_ssm), F32)],
        args=(xbc, xbc, xbc, dt_c, cs_c, cs_r3, e01, dskip_e, hprev, dy),
        scratch=[pltpu.VMEM((N_GROUPS, N_STATE, rp), F32)], sem=("arbitrary", "arbitrary"), ride=ride)


def _chip_sum(src, sib, *, name):
    rows, cols = src.shape[1:]
    tr = _tile(rows, 256, BF16_ROWS)
    core = lax.axis_index("c").astype(jnp.int32).reshape(1)

    def body(c_ref, a_ref, b_ref, o_ref):
        o_ref[...] = (a_ref[...].astype(F32) + b_ref[...].astype(F32)).astype(o_ref.dtype)

    grid_spec = pltpu.PrefetchScalarGridSpec(
        num_scalar_prefetch=1, grid=(N_CHIPS, rows // tr),
        in_specs=[pl.BlockSpec((None, tr, cols), lambda q, i, c_ref: (2 * q + c_ref[0], i, 0)),
                  pl.BlockSpec((None, tr, cols), lambda q, i, c_ref: (q, i, 0))],
        out_specs=pl.BlockSpec((None, tr, cols), lambda q, i, c_ref: (q, i, 0)))
    return pl.pallas_call(
        body, name=name, grid_spec=grid_spec, out_shape=jax.ShapeDtypeStruct(sib.shape, sib.dtype),
        compiler_params=pltpu.CompilerParams(dimension_semantics=("parallel", "parallel"), vmem_limit_bytes=VMEM_LIMIT),
    )(core, src, sib)


def _adamw(w, g, m, v):
    m = ADAM_B1 * m + (1.0 - ADAM_B1) * g
    v = ADAM_B2 * v + (1.0 - ADAM_B2) * (g * g)
    m_hat = m / (1.0 - ADAM_B1 ** ADAM_STEP)
    v_hat = v / (1.0 - ADAM_B2 ** ADAM_STEP)
    delta = -ADAM_LR * (m_hat / (jnp.sqrt(v_hat) + ADAM_EPS) + ADAM_WD * w)
    return delta, m, v


def _reduce_adamw(parts, w, m, v, *, name):
    n_parts = parts.shape[0]
    rows, cols = w.shape
    tr = _tile(rows, 128, BF16_ROWS)

    def body(p_ref, w_ref, m_ref, v_ref, g_ref, d_ref, mo_ref, vo_ref):
        g = p_ref[0].astype(F32)
        for k in range(1, n_parts):
            g = g + p_ref[k].astype(F32)
        delta, mn, vn = _adamw(w_ref[...], g, m_ref[...], v_ref[...])
        g_ref[...] = g
        d_ref[...] = delta
        mo_ref[...] = mn
        vo_ref[...] = vn

    spec = pl.BlockSpec((tr, cols), lambda i: (i, 0))
    outs, _ = _call(
        body, name=name, grid=(rows // tr,),
        in_specs=[pl.BlockSpec((n_parts, tr, cols), lambda i: (0, i, 0)), spec, spec, spec],
        out_specs=[spec] * 4, out_shape=[jax.ShapeDtypeStruct((rows, cols), F32)] * 4,
        args=(parts, w, m, v), sem=("parallel",))
    return outs


def _move_rows(src, src_row, name, extra=None, extra_row=None):
    rb, n_out, cols = ROW_BLOCK, len(src_row), src.shape[1]
    assert n_out % rb == 0 and src.shape[0] % rb == 0 and src.shape[0] // rb >= 3
    n_blocks, max_b0, seg_cap = n_out // rb, src.shape[0] // rb - 3, 4

    def segments(rows_of, lo):
        segs, r = [], 0
        while r < rb:
            if rows_of[r] < 0:
                r += 1
                continue
            e = r
            while e + 1 < rb and rows_of[e + 1] == rows_of[e] + 1:
                e += 1
            segs.append((r, e + 1, rows_of[r] - r - lo))
            r = e + 1
        assert len(segs) <= seg_cap
        return segs + [(0, 0, 0)] * (seg_cap - len(segs))

    table = []
    for j in range(n_blocks):
        rows_j = list(src_row[j * rb:(j + 1) * rb])
        valid = [v for v in rows_j if v >= 0]
        b0 = min(max((min(valid) // rb) if valid else 0, 0), max_b0)
        assert not valid or max(valid) < (b0 + 3) * rb
        row = [b0] + [v for seg in segments(rows_j, b0 * rb) for v in seg]
        extra_j = [] if extra is None else list(extra_row[j * rb:(j + 1) * rb])
        if extra is not None:
            row += [v for seg in segments(extra_j, 0) for v in seg]
        need_third = bool(valid) and max(valid) >= (b0 + 2) * rb
        third = b0 + 2 if need_third or not table else table[-1][-1]
        row += [int(need_third), int(any(v >= 0 for v in extra_j)), third]
        table.append(row)
    flag_third, flag_extra, col_third = len(table[0]) - 3, len(table[0]) - 2, len(table[0]) - 1
    table = jnp.asarray(table, jnp.int32)

    def select(tbl_ref, j, first, width, col0=0):
        r = lax.broadcasted_iota(jnp.int32, (rb, width), 0)
        c = lax.broadcasted_iota(jnp.int32, (rb, width), 1) + col0
        hit = jnp.zeros((rb, width), jnp.bool_)
        for s in range(seg_cap):
            lo, hi, off = (tbl_ref[j, first + 3 * s + i] for i in range(3))
            hit = hit | ((r >= lo) & (r < hi) & (c == r + off))
        return jnp.where(hit, 1.0, 0.0).astype(BF16)

    def body(tbl_ref, *refs):
        o_ref = refs[-1]
        j = pl.program_id(0)
        sel = select(tbl_ref, j, 1, 2 * rb)
        pick = lambda m, b: lax.dot_general(m, refs[b][...], _NN, preferred_element_type=F32)
        o_ref[...] = (pick(sel[:, :rb], 0) + pick(sel[:, rb:], 1)).astype(o_ref.dtype)

        @pl.when(tbl_ref[j, flag_third] == 1)
        def _():
            o_ref[...] = (o_ref[...].astype(F32) + pick(select(tbl_ref, j, 1, rb, 2 * rb), 2)).astype(o_ref.dtype)

        if extra is not None:
            @pl.when(tbl_ref[j, flag_extra] == 1)
            def _():
                more = lax.dot_general(select(tbl_ref, j, 1 + 3 * seg_cap, extra.shape[0]), refs[3][...], _NN,
                                       preferred_element_type=F32)
                o_ref[...] = (o_ref[...].astype(F32) + more).astype(o_ref.dtype)

    in_specs = [pl.BlockSpec((rb, cols), functools.partial(lambda b, j, tbl: (tbl[j, 0] + b, 0), b)) for b in range(2)]
    in_specs.append(pl.BlockSpec((rb, cols), lambda j, tbl: (tbl[j, col_third], 0)))
    args = [src, src, src]
    if extra is not None:
        in_specs.append(pl.BlockSpec(extra.shape, lambda j, tbl: (0, 0)))
        args.append(extra)
    grid_spec = pltpu.PrefetchScalarGridSpec(num_scalar_prefetch=1, grid=(n_blocks,), in_specs=in_specs,
                                             out_specs=pl.BlockSpec((rb, cols), lambda j, tbl: (j, 0)))
    return pl.pallas_call(
        body, name=name, grid_spec=grid_spec, out_shape=jax.ShapeDtypeStruct((n_out, cols), src.dtype),
        compiler_params=pltpu.CompilerParams(dimension_semantics=("parallel",), vmem_limit_bytes=VMEM_LIMIT),
    )(table, *args)


def _cols_of(g):
    return jnp.transpose(g, (1, 0, 2)).reshape(g.shape[1], -1)


def _pad_to(a, rows, cols):
    return jnp.pad(a, ((0, rows - a.shape[0]), (0, cols - a.shape[1])))


def kernel(x, norm_mix_g, w_in, ssm_conv_w, ssm_conv_b, ssm_dt_bias, ssm_A_log, ssm_D, ssm_norm_g, sc_conv_w, w_out, norm_ffn_g, w_gate, w_up, w_down, norm_final_g, loss_target, m_norm_mix_g, m_w_in, m_ssm_conv_w, m_ssm_conv_b, m_ssm_dt_bias, m_ssm_A_log, m_ssm_D, m_ssm_norm_g, m_sc_conv_w, m_w_out, m_norm_ffn_g, m_w_gate, m_w_up, m_w_down, m_norm_final_g, v_norm_mix_g, v_w_in, v_ssm_conv_w, v_ssm_conv_b, v_ssm_dt_bias, v_ssm_A_log, v_ssm_D, v_ssm_norm_g, v_sc_conv_w, v_w_out, v_norm_ffn_g, v_w_gate, v_w_up, v_w_down, v_norm_final_g):
    t_len, d = x.shape[1], x.shape[2]
    heads = d // HEADDIM
    r_heads = heads // N_GROUPS
    d_xbc = d + 2 * N_GROUPS * N_STATE
    ff_s = w_down.shape[1]
    ff = ff_s * N_DEV
    off_xbc, off_dt = d, d + d_xbc
    off_cb = off_dt + heads
    d_in = off_cb + 3 * d
    in_s = d_in // N_DEV
    in_p = -(-in_s // (2 * BF16_ROWS)) * (2 * BF16_ROWS)
    w_main = 4 * d + d_xbc
    me = 4 * lax.axis_index("x") + 2 * lax.axis_index("y") + lax.axis_index("c")

    x2 = x[0]
    target = loss_target[0]

    tpose = lambda a: jnp.transpose(a[0])
    win_s = _pad_to(tpose(w_in).astype(BF16), in_p, d)
    wg_s, wu_s = tpose(w_gate).astype(BF16), tpose(w_up).astype(BF16)
    wo_s, wd_s = w_out[0].astype(BF16), w_down[0].astype(BF16)
    small_w = jnp.concatenate([_pad_to(ssm_conv_w[0], K_SSM, d_xbc // N_DEV),
                               _pad_to(sc_conv_w[0], K_SC + 1, d_xbc // N_DEV)], axis=0)

    g1, g2, g3 = norm_mix_g, norm_ffn_g, norm_final_g.reshape(1, d)
    gs = ssm_norm_g
    small = [_pad_to(ssm_dt_bias, 1, LANES), ssm_dt_bias.reshape(heads, 1), _pad_to(ssm_A_log, 1, LANES),
             ssm_A_log.reshape(heads, 1)]
    e01 = (lax.broadcasted_iota(jnp.int32, (LANES, d), 1) // HEADDIM
           == lax.broadcasted_iota(jnp.int32, (LANES, d), 0)).astype(BF16)
    dskip_e = jnp.repeat(ssm_D, HEADDIM, axis=1)
    tr = _tile(t_len, 256, 8)
    tr_wide = _tile(t_len, 512, 8)
    tr_ff = _tile(t_len, 128, 8)
    cw = LANES
    slab = lambda col: col // cw

    (n1,), (gin, gsm) = _rows_call(lambda v, g: ((_rms(v, g),), ()), rows=t_len, tr=tr_wide, row_ins=[(x2, d, 0)],
                                   full_ins=[g1], row_outs=[(d, BF16)], acc_outs=[], name="norm_mix",
                                   ride=_gather_relayed([win_s], [small_w]))
    in_pieces = []
    for k in range(N_DEV):
        for a, b, dst, shift in ((0, off_dt, 0, 0), (off_dt, off_cb, 1, -off_dt), (off_cb, d_in, 0, -heads)):
            s, e = max(k * in_s, a), min((k + 1) * in_s, b)
            if s < e:
                in_pieces.append((k, s - k * in_s, e - s, dst, s + shift))
    ref_row = lambda t: t if t < off_dt else t + heads
    wtm = _move_rows(gin.reshape(N_DEV * in_p, d),
                     [(ref_row(t) // in_s) * in_p + ref_row(t) % in_s for t in range(w_main)], "place_w_in")
    wtdt = jnp.zeros((LANES, d), BF16)
    for k, r0, n, dst, d0 in in_pieces:
        if dst == 1:
            wtdt = lax.dynamic_update_slice(wtdt, gin[k, r0:r0 + n], (d0, 0))
    cw_ssm = _cols_of(gsm[:, :K_SSM, :])
    cw_sc = _cols_of(gsm[:, K_SSM:K_SSM + K_SC, :d // N_DEV])

    proj, (go_1, gg_1) = _matmul(n1, wtm, tb=True, out_dtype=BF16, name="proj_main",
                                 ride=_gather_chips([wo_s, wg_s]))
    dt_raw, _ = _matmul(n1, wtdt, tb=True, out_dtype=F32, name="proj_dt")
    dt_raw_t = jnp.transpose(dt_raw[:, :heads])
    (xbc,), (go, gg) = _cols_call(_conv_silu_fwd, rows=t_len, cols=d_xbc, cw=cw, col_ins=[(proj, slab(off_xbc))],
                                  par_ins=[(cw_ssm, 0), (ssm_conv_b, 0)], col_outs=[BF16], par_outs=[],
                                  name="ssm_conv", ride=_gather_sibling([go_1, gg_1]))
    dt_c, cs_c, cs_r = _ssd_dt(dt_raw, dt_raw_t, small)
    cs_r3 = cs_r.reshape(N_GROUPS, r_heads, t_len)
    up_cut = int(ff_s * W_UP_GATHER_SPLIT) // BF16_ROWS * BF16_ROWS
    down_cut = int(ff_s * W_DOWN_GATHER_SPLIT) // BF16_ROWS * BF16_ROWS
    half_cut = ff_s // 2 // BF16_ROWS * BF16_ROWS
    (y_ssd, hprev), (gu_1,) = _ssd_fwd(xbc, dt_c, cs_c, cs_r3, e01, dskip_e, d_ssm=d, r_heads=r_heads,
                                       ride=_gather_chips([wu_s], rows=(0, up_cut)))

    def gate_norm(y, z, g):
        z = z.astype(F32)
        return _rms(y * (z * _sigmoid(z)), g)

    (y_mix,), _ = _rows_call(lambda y, z, g: ((gate_norm(y, z, g),), ()), rows=t_len, tr=tr_wide,
                             row_ins=[(y_ssd, d, 0), (proj, d, 0)], full_ins=[gs], row_outs=[(d, BF16, 2 * d)],
                             acc_outs=[], name="ssm_gate_norm")
    wgt, wo = gg.reshape(ff, d), go.reshape(2 * d, d)
    sc0 = slab(d + d_xbc)
    (y_mix,), _ = _cols_call(_shortconv_fwd, rows=t_len, cols=d, cw=cw,
                             col_ins=[(proj, sc0), (proj, sc0 + slab(d)), (proj, sc0 + 2 * slab(d))],
                             par_ins=[(cw_sc, 0)], col_outs=[BF16], par_outs=[], name="shortconv",
                             into=(y_mix, slab(d)))
    h1, (gu_1, gd_1) = _matmul(y_mix, wo, out_dtype=F32, add=x2, name="out_proj", ride=_merge(
        _gather_chips([wu_s], rows=(up_cut, ff_s - up_cut), into=[gu_1]), _gather_chips([wd_s], rows=(0, down_cut))))
    (n2,), _ = _rows_call(lambda v, g: ((_rms(v, g),), ()), rows=t_len, tr=tr_wide, row_ins=[(h1, d, 0)],
                          full_ins=[g2], row_outs=[(d, BF16)], acc_outs=[], name="norm_ffn")
    g_ff, (gd_1, gu) = _matmul(n2, wgt, tb=True, out_dtype=BF16, name="ffn_gate", ride=_merge(
        _gather_chips([wd_s], rows=(down_cut, ff_s - down_cut), into=[gd_1]), _gather_sibling([gu_1])))
    wut = gu.reshape(ff, d)
    (u_ff, a_ff), (gd,) = _matmul(n2, wut, tb=True, name="ffn_up", ride=_gather_sibling([gd_1]),
                                  post=(lambda uv, gv: (uv, gv * _sigmoid(gv) * uv), [g_ff], [BF16, BF16]),
                                  tn_max=MM_TILE_N_POST)
    wd = gd.reshape(ff, d)
    h2, _ = _matmul(a_ff, wd, out_dtype=F32, add=h1, name="ffn_down")

    def head(hv, tv, g):
        def f(hh, gg_):
            e = _rms(hh, gg_) - tv
            return (0.5 / d) * jnp.sum(e * e)
        val, (dh, dg) = jax.value_and_grad(f, argnums=(0, 1))(hv, g)
        return (dh, dh), (jnp.full((1, LANES), val, F32), dg)

    (dh2, dh2_b, loss_acc, dg3), _ = _rows_call(head, rows=t_len, tr=tr, row_ins=[(h2, d, 0), (target, d, 0)],
                                                full_ins=[g3], row_outs=[(d, F32), (d, BF16)],
                                                acc_outs=[(1, LANES), (1, d)], name="loss_head")

    def act_bwd(dav, gv, uv):
        s = _sigmoid(gv)
        return dav * uv * (s * (1.0 + gv * (1.0 - s))), dav * gv * s

    (dg_ff, du_ff), _ = _matmul(dh2_b, wd, tb=True, name="d_ffn_gate_up",
                                post=(act_bwd, [g_ff, u_ff], [BF16, BF16]), tn_max=MM_TILE_N_POST)
    dwd, _ = _matmul(a_ff, dh2_b, ta=True, out_dtype=BF16, name="d_w_down")
    dwd8 = dwd.reshape(N_DEV, ff_s, d)
    dn2, (sib_d,) = _matmul(dg_ff, wgt, out_dtype=F32, name="d_norm_ffn_out_gate", ride=_scatter_sibling([dwd8]))
    chip_d = _chip_sum(dwd8, sib_d, name="chip_sum_w_down")
    dn2, (parts_d,) = _matmul(du_ff, wut, out_dtype=F32, add=dn2, name="d_norm_ffn_out_up",
                              ride=_scatter_chips([chip_d], rows=(0, half_cut)))
    dwg, (parts_d,) = _matmul(dg_ff, n2, ta=True, out_dtype=BF16, name="d_w_gate",
                              ride=_scatter_chips([chip_d], rows=(half_cut, ff_s - half_cut), into=[parts_d]))
    dwu, _ = _matmul(du_ff, n2, ta=True, out_dtype=BF16, name="d_w_up")
    dwg8, dwu8 = dwg.reshape(N_DEV, ff_s, d), dwu.reshape(N_DEV, ff_s, d)

    def norm_bwd(v, dn, dres, g):
        _, vjp = jax.vjp(_rms, v, g)
        dv, dg = vjp(dn)
        return (dv + dres,), (dg,)

    def norm_bwd_2(v, dn, dres, g):
        (dv,), acc = norm_bwd(v, dn, dres, g)
        return (dv, dv), acc

    (dh1, dh1_b, dg2), (sib_g, sib_u) = _rows_call(norm_bwd_2, rows=t_len, tr=tr,
                                                   row_ins=[(h1, d, 0), (dn2, d, 0), (dh2, d, 0)], full_ins=[g2],
                                                   row_outs=[(d, F32), (d, BF16)], acc_outs=[(1, d)], name="d_norm_ffn",
                                                   ride=_scatter_sibling([dwg8, dwu8]))
    chip_g = _chip_sum(dwg8, sib_g, name="chip_sum_w_gate")
    chip_u = _chip_sum(dwu8, sib_u, name="chip_sum_w_up")

    dy_mix, _ = _matmul(dh1_b, wo, tb=True, out_dtype=BF16, name="d_y_mix")
    dwo, _ = _matmul(y_mix, dh1_b, ta=True, out_dtype=BF16, name="d_w_out")
    dwo8 = dwo.reshape(N_DEV, 2 * d // N_DEV, d)
    def gate_norm_bwd(y, z, dyo, g):
        _, vjp = jax.vjp(gate_norm, y, z.astype(F32), g)
        dy, dz, dg = vjp(dyo.astype(F32))
        return (dy, dz), (dg,)

    (dy_ssd, dproj, dgs), _ = _rows_call(gate_norm_bwd, rows=t_len, tr=tr,
                                         row_ins=[(y_ssd, d, 0), (proj, d, 0), (dy_mix, d, 0)], full_ins=[gs],
                                         row_outs=[(d, F32), (d, BF16, w_main)], acc_outs=[(1, d)],
                                         name="d_ssm_gate_norm")
    (dproj, dgc, du, dcw_sc), (sib_o,) = _cols_call(
        _shortconv_bwd, rows=t_len, cols=d, cw=cw,
        col_ins=[(proj, sc0), (proj, sc0 + slab(d)), (proj, sc0 + 2 * slab(d)), (dy_mix, slab(d))],
        par_ins=[(cw_sc, 0)], col_outs=[BF16] * 3, par_outs=[K_SC], name="d_shortconv",
        ride=_scatter_sibling([dwo8]), into=(dproj, sc0))
    chip_o = _chip_sum(dwo8, sib_o, name="chip_sum_w_out")
    (dxs, dbm, dcm, g_dt, g_csc, g_csr3, ddk), (parts_g,) = _ssd_bwd(
        xbc, dt_c, cs_c, cs_r3, e01, dskip_e, hprev, dy_ssd, d_ssm=d, r_heads=r_heads,
        ride=_scatter_chips([chip_g]))
    ddt_c, ddt_r, dbias_r, dbias_c, dalog_r, dalog_c, ddskip = _ssd_dt(
        dt_raw, dt_raw_t, small, cots=(g_dt, g_csc, g_csr3.reshape(heads, t_len), ddk, e01))
    dcw_parts, dcb_parts, col0 = [], [], 0
    for tag, dpart in (("x", dxs), ("b", dbm), ("c", dcm)):
        (dproj, dcw_p, dcb_p), _ = _cols_call(
            _conv_silu_bwd, rows=t_len, cols=dpart.shape[1], cw=cw,
            col_ins=[(proj, slab(off_xbc + col0)), (dpart, 0)], par_ins=[(cw_ssm, slab(col0)), (ssm_conv_b, slab(col0))],
            col_outs=[BF16], par_outs=[K_SSM, 1], name="d_ssm_conv_" + tag, into=(dproj, slab(off_xbc + col0)))
        dcw_parts.append(dcw_p)
        dcb_parts.append(dcb_p)
        col0 += dpart.shape[1]
    dcw_ssm, dcb_ssm = jnp.concatenate(dcw_parts, axis=1), jnp.concatenate(dcb_parts, axis=1)
    for i, part in ((1, dgc), (2, du)):
        dproj = lax.dynamic_update_slice(dproj, part, (0, d + d_xbc + i * d))
    ddt = ddt_c + _pad_to(jnp.transpose(ddt_r), t_len, LANES)
    dwm, (parts_u, parts_o) = _matmul(dproj, n1, ta=True, out_dtype=BF16, name="d_w_in_main",
                                      ride=_scatter_chips([chip_u, chip_o]))
    dwdt, _ = _matmul(ddt, n1, ta=True, out_dtype=BF16, name="d_w_in_dt")
    own_ref = [k * in_s + i if i < in_s else -1 for k in range(N_DEV) for i in range(in_p)]
    dwin8 = _move_rows(
        dwm, [-1 if g < 0 or off_dt <= g < off_cb else (g if g < off_dt else g - heads) for g in own_ref],
        "place_d_w_in", extra=dwdt, extra_row=[g - off_dt if off_dt <= g < off_cb else -1 for g in own_ref],
    ).reshape(N_DEV, in_p, d)
    dn1, (sib_in,) = _matmul(ddt, wtdt, out_dtype=F32, name="d_norm_mix_out_dt", ride=_scatter_sibling([dwin8]))
    chip_in = _chip_sum(dwin8, sib_in, name="chip_sum_w_in")
    cut = int(in_p * W_IN_SCATTER_SPLIT) // BF16_ROWS * BF16_ROWS
    dn1, (parts_in,) = _matmul(dproj, wtm, out_dtype=F32, add=dn1, name="d_norm_mix_out",
                               ride=_scatter_chips([chip_in], rows=(0, cut)))
    (dx, dg1), _ = _rows_call(norm_bwd, rows=t_len, tr=tr, row_ins=[(x2, d, 0), (dn1, d, 0), (dh1, d, 0)],
                              full_ins=[g1], row_outs=[(d, F32)], acc_outs=[(1, d)], name="d_norm_mix")

    wide = d_xbc
    rows_small = [dg1, dcb_ssm, dbias_r + _pad_to(dbias_c.reshape(1, heads), 1, LANES),
                  dalog_r + _pad_to(dalog_c.reshape(1, heads), 1, LANES), ddskip, dgs, dg2, dg3]
    packed = jnp.concatenate([_pad_to(r, 1, wide) for r in rows_small]
                             + [dcw_ssm, _pad_to(dcw_sc, K_SC, wide), _pad_to(loss_acc, 1, wide)], axis=0)
    p_small, parts_in = _comm(_merge(_gather_all([packed]), _scatter_chips([chip_in], rows=(cut, in_p - cut),
                                                                           into=[parts_in])), "gather_small_grads")

    conv_lo = me * (d_xbc // N_DEV)
    sc_lo = me * (d // N_DEV)

    def pack_state(vals):
        (nm, cb, dtb, al, dk, sg, nf, nfin, cws, scs) = vals
        rows = [_pad_to(a.reshape(1, -1), 1, wide) for a in (nm, cb, dtb, al, dk, sg, nf, nfin)]
        cws_full = lax.dynamic_update_slice(jnp.zeros((K_SSM, wide), F32), cws[0], (0, conv_lo))
        scs_full = lax.dynamic_update_slice(jnp.zeros((K_SC, wide), F32), scs[0], (0, sc_lo))
        return jnp.concatenate(rows + [cws_full, scs_full, jnp.zeros((1, wide), F32)], axis=0)

    w_small = pack_state((norm_mix_g, ssm_conv_b, ssm_dt_bias, ssm_A_log, ssm_D, ssm_norm_g, norm_ffn_g, norm_final_g,
                          ssm_conv_w, sc_conv_w))
    m_small = pack_state((m_norm_mix_g, m_ssm_conv_b, m_ssm_dt_bias, m_ssm_A_log, m_ssm_D, m_ssm_norm_g, m_norm_ffn_g,
                          m_norm_final_g, m_ssm_conv_w, m_sc_conv_w))
    v_small = pack_state((v_norm_mix_g, v_ssm_conv_b, v_ssm_dt_bias, v_ssm_A_log, v_ssm_D, v_ssm_norm_g, v_norm_ffn_g,
                          v_norm_final_g, v_ssm_conv_w, v_sc_conv_w))

    tin = lambda a: _pad_to(tpose(a), in_p, d)
    tin_back = lambda a: jnp.transpose(a[:in_s])[None]
    t_back = lambda a: jnp.transpose(a)[None]
    upd = {
        "w_in": [tin_back(o) for o in _reduce_adamw(parts_in, tin(w_in), tin(m_w_in), tin(v_w_in), name="adamw_w_in")],
        "w_out": [o[None] for o in _reduce_adamw(parts_o, w_out[0], m_w_out[0], v_w_out[0], name="adamw_w_out")],
        "w_gate": [t_back(o) for o in _reduce_adamw(parts_g, tpose(w_gate), tpose(m_w_gate), tpose(v_w_gate),
                                                    name="adamw_w_gate")],
        "w_up": [t_back(o) for o in _reduce_adamw(parts_u, tpose(w_up), tpose(m_w_up), tpose(v_w_up),
                                                  name="adamw_w_up")],
        "w_down": [o[None] for o in _reduce_adamw(parts_d, w_down[0], m_w_down[0], v_w_down[0], name="adamw_w_down")],
    }
    small_upd = _reduce_adamw(p_small, w_small, m_small, v_small, name="adamw_small")
    loss = small_upd[0][8 + K_SSM + K_SC, 0]

    def unpack(packed_out):
        vec = lambda i, n, shape: packed_out[i, :n].reshape(shape)
        return {
            "norm_mix_g": vec(0, d, (1, d)), "ssm_conv_b": vec(1, d_xbc, (1, d_xbc)),
            "ssm_dt_bias": vec(2, heads, (1, heads)), "ssm_A_log": vec(3, heads, (1, heads)),
            "ssm_D": vec(4, heads, (1, heads)), "ssm_norm_g": vec(5, d, (1, d)), "norm_ffn_g": vec(6, d, (1, d)),
            "norm_final_g": vec(7, d, (d,)),
            "ssm_conv_w": lax.dynamic_slice(packed_out[8:8 + K_SSM], (0, conv_lo), (K_SSM, d_xbc // N_DEV))[None],
            "sc_conv_w": lax.dynamic_slice(packed_out[8 + K_SSM:8 + K_SSM + K_SC], (0, sc_lo), (K_SC, d // N_DEV))[None],
        }

    names = ["norm_mix_g", "w_in", "ssm_conv_w", "ssm_conv_b", "ssm_dt_bias", "ssm_A_log", "ssm_D", "ssm_norm_g",
             "sc_conv_w", "w_out", "norm_ffn_g", "w_gate", "w_up", "w_down", "norm_final_g"]
    outs = []
    for kind in range(4):
        small_k = unpack(small_upd[kind])
        for nm in names:
            outs.append(upd[nm][kind] if nm in upd else small_k[nm])
    return (loss, dx[None], *outs)
```
